```python
import math
import jax, jax.numpy as jnp
from jax import lax
import numpy as np

D_MODEL = 1024
BATCH = 8
SEQ = 16384
DEPTH = 4

HEAD_DIM = 64
N_MIX_HEADS = 8
N_MEM_HEADS = 4
MIX_WIDTH = N_MIX_HEADS * HEAD_DIM
MEM_WIDTH = N_MEM_HEADS * HEAD_DIM
MERGED_WIDTH = MIX_WIDTH + MEM_WIDTH
N_MEM = 256
D_FF = 4 * D_MODEL
BLOCK_Q = 128
GROUP_Q = 1024
N_A = DEPTH // 2
N_B = DEPTH - N_A
W_IN_A = 3 * MIX_WIDTH + MEM_WIDTH
W_IN_B = MIX_WIDTH + MEM_WIDTH
W_KV_SHARED = 2 * MIX_WIDTH + N_MIX_HEADS
EPS = 1e-6
NEG_INF = -1e30
FORGET_BIAS_INIT = 2.0

kernel_name = "yoco_stickbreak_fox_hybrid"


def rms_norm(x, g):
    xf = x.astype(jnp.float32)
    y = xf * lax.rsqrt(jnp.mean(xf * xf, axis=-1, keepdims=True) + EPS) * g.astype(jnp.float32)
    return y.astype(x.dtype)


def split_heads(t, n_heads):
    b, s, _ = t.shape
    return t.reshape(b, s, n_heads, HEAD_DIM).transpose(0, 2, 1, 3)


def merge_heads(t):
    b, h, s, d = t.shape
    return t.transpose(0, 2, 1, 3).reshape(b, s, h * d)


def to_blocks(t):
    b, h, s = t.shape[:3]
    t = t.reshape((b, h, s // BLOCK_Q, BLOCK_Q) + t.shape[3:])
    return jnp.moveaxis(t, 2, 0)


def from_blocks(t):
    nb, b, h, blk, d = t.shape
    return jnp.moveaxis(t, 0, 2).reshape(b, h, nb * blk, d)


def causal_sweep(block_fn, q_side, kv_side):
    s_len = q_side[0].shape[2]
    outs = []
    for g0 in range(0, s_len, GROUP_Q):
        g1 = min(g0 + GROUP_Q, s_len)
        kv_g = tuple(t[:, :, :g1] for t in kv_side)
        q_blocks = tuple(to_blocks(t[:, :, g0:g1]) for t in q_side)
        starts = g0 + jnp.arange((g1 - g0) // BLOCK_Q) * BLOCK_Q

        def body(args, kv_g=kv_g):
            return block_fn(*args, *kv_g)

        outs.append(from_blocks(lax.map(body, q_blocks + (starts,))))
    return jnp.concatenate(outs, axis=2)


def _stick_breaking_block(q_blk, t0, k, v):
    scale = 1.0 / math.sqrt(HEAD_DIM)
    z = jnp.einsum('bhqd,bhkd->bhqk', q_blk, k).astype(jnp.float32) * scale
    key_pos = jnp.arange(k.shape[2])
    t_pos = t0 + jnp.arange(BLOCK_Q)
    mask = key_pos[None, :] < t_pos[:, None]
    log_one_minus = jnp.where(mask, jax.nn.log_sigmoid(-z), 0.0)
    between = lax.cumsum(log_one_minus, axis=3, reverse=True) - log_one_minus
    w = jnp.where(mask, jnp.exp(jax.nn.log_sigmoid(z) + between), 0.0)
    return jnp.einsum('bhqk,bhkd->bhqd', w.astype(v.dtype), v)


def stick_breaking_attention(q, k, v):
    return causal_sweep(_stick_breaking_block, (q,), (k, v))


def _forgetting_block(q_blk, c_blk, t0, k, v, c_k):
    scale = 1.0 / math.sqrt(HEAD_DIM)
    z = jnp.einsum('bhqd,bhkd->bhqk', q_blk, k).astype(jnp.float32) * scale
    z = z + c_blk[..., :, None] - c_k[:, :, None, :]
    key_pos = jnp.arange(k.shape[2])
    t_pos = t0 + jnp.arange(BLOCK_Q)
    mask = key_pos[None, :] <= t_pos[:, None]
    p = jax.nn.softmax(jnp.where(mask, z, NEG_INF), axis=-1)
    return jnp.einsum('bhqk,bhkd->bhqd', p.astype(v.dtype), v)


def forgetting_attention(q, k, v, log_f_cum):
    return causal_sweep(_forgetting_block, (q, log_f_cum), (k, v, log_f_cum))


def memory_attention(q_mem, mem_k, mem_v):
    scale = 1.0 / math.sqrt(HEAD_DIM)
    s = jnp.einsum('bshd,bmhd->bhsm', q_mem, mem_k).astype(jnp.float32) * scale
    p = jax.nn.softmax(s, axis=-1)
    o = jnp.einsum('bhsm,bmhd->bshd', p.astype(mem_v.dtype), mem_v)
    b, sl = q_mem.shape[:2]
    return o.reshape(b, sl, MEM_WIDTH)


def squared_relu_mlp(x, w1, w2):
    h = jnp.square(jax.nn.relu(x @ w1))
    return h @ w2


def _fwd_setup_inputs(seed: int = 0) -> dict:
    key = jax.random.key(seed)
    ks = jax.random.split(key, 16)
    f32 = jnp.float32
    nrm = lambda k, shape, s: (jax.random.normal(k, shape, f32) * s).astype(f32)
    x = jax.random.normal(ks[0], (BATCH, SEQ, D_MODEL), f32)
    mem = jax.random.normal(ks[1], (BATCH, N_MEM, D_MODEL), f32)
    norm1_g = 1.0 + nrm(ks[2], (DEPTH, D_MODEL), 0.02)
    w_in_a = nrm(ks[3], (N_A, D_MODEL, W_IN_A), D_MODEL ** -0.5)
    w_in_b = nrm(ks[4], (N_B, D_MODEL, W_IN_B), D_MODEL ** -0.5)
    w_mem_kv = nrm(ks[5], (DEPTH, D_MODEL, 2 * MEM_WIDTH), D_MODEL ** -0.5)
    mem_norm_g = 1.0 + nrm(ks[6], (DEPTH, D_MODEL), 0.02)
    w_o = nrm(ks[7], (DEPTH, MERGED_WIDTH, D_MODEL), MERGED_WIDTH ** -0.5)
    norm2_g = 1.0 + nrm(ks[8], (DEPTH, D_MODEL), 0.02)
    w_mlp1 = nrm(ks[9], (DEPTH, D_MODEL, D_FF), D_MODEL ** -0.5)
    w_mlp2 = nrm(ks[10], (DEPTH, D_FF, D_MODEL), D_FF ** -0.5)
    kv_norm_g = 1.0 + nrm(ks[11], (D_MODEL,), 0.02)
    w_kv_shared = nrm(ks[12], (D_MODEL, W_KV_SHARED), D_MODEL ** -0.5)
    b_f = FORGET_BIAS_INIT + nrm(ks[13], (N_MIX_HEADS,), 0.1)
    final_norm_g = 1.0 + nrm(ks[14], (D_MODEL,), 0.02)
    return {"x": x, "mem": mem, "norm1_g": norm1_g, "w_in_a": w_in_a, "w_in_b": w_in_b,
            "w_mem_kv": w_mem_kv, "mem_norm_g": mem_norm_g, "w_o": w_o, "norm2_g": norm2_g,
            "w_mlp1": w_mlp1, "w_mlp2": w_mlp2, "kv_norm_g": kv_norm_g,
            "w_kv_shared": w_kv_shared, "b_f": b_f, "final_norm_g": final_norm_g}


def _fwd_reference(x, mem, norm1_g, w_in_a, w_in_b, w_mem_kv, mem_norm_g, w_o, norm2_g,
              w_mlp1, w_mlp2, kv_norm_g, w_kv_shared, b_f, final_norm_g):
    b, s_len, _ = x.shape
    m_len = mem.shape[1]
    h = x
    k_sh = v_sh = log_f_cum = None
    for l in range(DEPTH):
        if l == N_A:
            hs = rms_norm(h, kv_norm_g)
            kvf = hs @ w_kv_shared
            k_sh = split_heads(kvf[..., :MIX_WIDTH], N_MIX_HEADS)
            v_sh = split_heads(kvf[..., MIX_WIDTH:2 * MIX_WIDTH], N_MIX_HEADS)
            f_logit = kvf[..., 2 * MIX_WIDTH:].astype(jnp.float32) + b_f.astype(jnp.float32)
            log_f_cum = jnp.moveaxis(lax.cumsum(jax.nn.log_sigmoid(f_logit), axis=1), 1, 2)

        hn = rms_norm(h, norm1_g[l])
        mkv = rms_norm(mem, mem_norm_g[l]) @ w_mem_kv[l]
        mem_k = mkv[..., :MEM_WIDTH].reshape(b, m_len, N_MEM_HEADS, HEAD_DIM)
        mem_v = mkv[..., MEM_WIDTH:].reshape(b, m_len, N_MEM_HEADS, HEAD_DIM)

        if l < N_A:
            proj = hn @ w_in_a[l]
            q = split_heads(proj[..., :MIX_WIDTH], N_MIX_HEADS)
            k = split_heads(proj[..., MIX_WIDTH:2 * MIX_WIDTH], N_MIX_HEADS)
            v = split_heads(proj[..., 2 * MIX_WIDTH:3 * MIX_WIDTH], N_MIX_HEADS)
            q_mem = proj[..., 3 * MIX_WIDTH:]
            mix = stick_breaking_attention(q, k, v)
        else:
            proj = hn @ w_in_b[l - N_A]
            q = split_heads(proj[..., :MIX_WIDTH], N_MIX_HEADS)
            q_mem = proj[..., MIX_WIDTH:]
            mix = forgetting_attention(q, k_sh, v_sh, log_f_cum)

        mem_out = memory_attention(q_mem.reshape(b, s_len, N_MEM_HEADS, HEAD_DIM), mem_k, mem_v)
        merged = jnp.concatenate([merge_heads(mix), mem_out], axis=-1)
        h = h + merged @ w_o[l]
        h = h + squared_relu_mlp(rms_norm(h, norm2_g[l]), w_mlp1[l], w_mlp2[l])
    return rms_norm(h, final_norm_g)


import jax as _jax
import jax.numpy as _jnp

TWIN_FORMAT = 'train_step'
FWD_PARAMS = ['x', 'mem', 'norm1_g', 'w_in_a', 'w_in_b', 'w_mem_kv', 'mem_norm_g', 'w_o', 'norm2_g', 'w_mlp1', 'w_mlp2', 'kv_norm_g', 'w_kv_shared', 'b_f', 'final_norm_g']
TWIN_WEIGHTS = ['norm1_g', 'w_in_a', 'w_in_b', 'w_mem_kv', 'mem_norm_g', 'w_o', 'norm2_g', 'w_mlp1', 'w_mlp2', 'kv_norm_g', 'w_kv_shared', 'b_f', 'final_norm_g']
TWIN_DIFF_INPUT = 'x'
TWIN_INPUTS = ['x', 'mem', 'norm1_g', 'w_in_a', 'w_in_b', 'w_mem_kv', 'mem_norm_g', 'w_o', 'norm2_g', 'w_mlp1', 'w_mlp2', 'kv_norm_g', 'w_kv_shared', 'b_f', 'final_norm_g', 'loss_target', 'm_norm1_g', 'm_w_in_a', 'm_w_in_b', 'm_w_mem_kv', 'm_mem_norm_g', 'm_w_o', 'm_norm2_g', 'm_w_mlp1', 'm_w_mlp2', 'm_kv_norm_g', 'm_w_kv_shared', 'm_b_f', 'm_final_norm_g', 'v_norm1_g', 'v_w_in_a', 'v_w_in_b', 'v_w_mem_kv', 'v_mem_norm_g', 'v_w_o', 'v_norm2_g', 'v_w_mlp1', 'v_w_mlp2', 'v_kv_norm_g', 'v_w_kv_shared', 'v_b_f', 'v_final_norm_g']
TWIN_OUTPUTS = ['loss', 'grad_x', 'grad_norm1_g', 'grad_w_in_a', 'grad_w_in_b', 'grad_w_mem_kv', 'grad_mem_norm_g', 'grad_w_o', 'grad_norm2_g', 'grad_w_mlp1', 'grad_w_mlp2', 'grad_kv_norm_g', 'grad_w_kv_shared', 'grad_b_f', 'grad_final_norm_g', 'delta_norm1_g', 'delta_w_in_a', 'delta_w_in_b', 'delta_w_mem_kv', 'delta_mem_norm_g', 'delta_w_o', 'delta_norm2_g', 'delta_w_mlp1', 'delta_w_mlp2', 'delta_kv_norm_g', 'delta_w_kv_shared', 'delta_b_f', 'delta_final_norm_g', 'new_m_norm1_g', 'new_m_w_in_a', 'new_m_w_in_b', 'new_m_w_mem_kv', 'new_m_mem_norm_g', 'new_m_w_o', 'new_m_norm2_g', 'new_m_w_mlp1', 'new_m_w_mlp2', 'new_m_kv_norm_g', 'new_m_w_kv_shared', 'new_m_b_f', 'new_m_final_norm_g', 'new_v_norm1_g', 'new_v_w_in_a', 'new_v_w_in_b', 'new_v_w_mem_kv', 'new_v_mem_norm_g', 'new_v_w_o', 'new_v_norm2_g', 'new_v_w_mlp1', 'new_v_w_mlp2', 'new_v_kv_norm_g', 'new_v_w_kv_shared', 'new_v_b_f', 'new_v_final_norm_g']
TWIN_LEAF_KINDS = {'loss': 'loss', 'grad_x': 'grad_x', 'grad_norm1_g': 'grad_w', 'grad_w_in_a': 'grad_w', 'grad_w_in_b': 'grad_w', 'grad_w_mem_kv': 'grad_w', 'grad_mem_norm_g': 'grad_w', 'grad_w_o': 'grad_w', 'grad_norm2_g': 'grad_w', 'grad_w_mlp1': 'grad_w', 'grad_w_mlp2': 'grad_w', 'grad_kv_norm_g': 'grad_w', 'grad_w_kv_shared': 'grad_w', 'grad_b_f': 'grad_w', 'grad_final_norm_g': 'grad_w', 'delta_norm1_g': 'delta_w', 'delta_w_in_a': 'delta_w', 'delta_w_in_b': 'delta_w', 'delta_w_mem_kv': 'delta_w', 'delta_mem_norm_g': 'delta_w', 'delta_w_o': 'delta_w', 'delta_norm2_g': 'delta_w', 'delta_w_mlp1': 'delta_w', 'delta_w_mlp2': 'delta_w', 'delta_kv_norm_g': 'delta_w', 'delta_w_kv_shared': 'delta_w', 'delta_b_f': 'delta_w', 'delta_final_norm_g': 'delta_w', 'new_m_norm1_g': 'new_m', 'new_m_w_in_a': 'new_m', 'new_m_w_in_b': 'new_m', 'new_m_w_mem_kv': 'new_m', 'new_m_mem_norm_g': 'new_m', 'new_m_w_o': 'new_m', 'new_m_norm2_g': 'new_m', 'new_m_w_mlp1': 'new_m', 'new_m_w_mlp2': 'new_m', 'new_m_kv_norm_g': 'new_m', 'new_m_w_kv_shared': 'new_m', 'new_m_b_f': 'new_m', 'new_m_final_norm_g': 'new_m', 'new_v_norm1_g': 'new_v', 'new_v_w_in_a': 'new_v', 'new_v_w_in_b': 'new_v', 'new_v_w_mem_kv': 'new_v', 'new_v_mem_norm_g': 'new_v', 'new_v_w_o': 'new_v', 'new_v_norm2_g': 'new_v', 'new_v_w_mlp1': 'new_v', 'new_v_w_mlp2': 'new_v', 'new_v_kv_norm_g': 'new_v', 'new_v_w_kv_shared': 'new_v', 'new_v_b_f': 'new_v', 'new_v_final_norm_g': 'new_v'}


def _forward(args):
    return _fwd_reference(*[args[k] for k in FWD_PARAMS])


def _output_shape():
    def fwd():
        inp = _fwd_setup_inputs(0)
        return _fwd_reference(*[inp[k] for k in FWD_PARAMS])
    out = _jax.eval_shape(fwd)
    return out.shape, out.dtype

N_MICROBATCH = 1
ADAM_LR = 0.001
ADAM_B1 = 0.9
ADAM_B2 = 0.999
ADAM_EPS = 1e-08
ADAM_WD = 0.01
ADAM_STEP = 10
PER_EXAMPLE_BATCH_AXIS = {'x': 0, 'mem': 0, 'loss_target': 0}
SHARED_INPUTS = []
_WEIGHT_DTYPES = {'norm1_g': _jnp.float32, 'w_in_a': _jnp.float32, 'w_in_b': _jnp.float32, 'w_mem_kv': _jnp.float32, 'mem_norm_g': _jnp.float32, 'w_o': _jnp.float32, 'norm2_g': _jnp.float32, 'w_mlp1': _jnp.float32, 'w_mlp2': _jnp.float32, 'kv_norm_g': _jnp.float32, 'w_kv_shared': _jnp.float32, 'b_f': _jnp.float32, 'final_norm_g': _jnp.float32}
MOMENT_SCALE = {'norm1_g': 1.693294e-01, 'w_in_a': 1.723409e-01, 'w_in_b': 4.059413e-02, 'w_mem_kv': 3.289336e-02, 'mem_norm_g': 2.411420e-02, 'w_o': 1.667895e-01, 'norm2_g': 2.962764e-01, 'w_mlp1': 1.372638e-01, 'w_mlp2': 3.525003e-01, 'kv_norm_g': 2.004352e-01, 'w_kv_shared': 2.090374e-01, 'b_f': 1.010793e+00, 'final_norm_g': 1.320284e+02}


def _to_microbatches(a, axis):
    t = _jnp.moveaxis(a, axis, 0)
    t = t.reshape((N_MICROBATCH, t.shape[0] // N_MICROBATCH) + t.shape[1:])
    return _jnp.moveaxis(t, 1, axis + 1)


def setup_inputs(seed: int = 0) -> dict:
    inp = _fwd_setup_inputs(seed)
    key = _jax.random.fold_in(_jax.random.key(seed), 7919)
    shape, _ = _output_shape()
    out = dict(inp)
    out["loss_target"] = _jax.random.normal(_jax.random.fold_in(key, 0), shape, _jnp.float32)
    for i, name in enumerate(TWIN_WEIGHTS):
        w = inp[name].astype(_jnp.float32)
        if MOMENT_SCALE is None:
            s = _jnp.sqrt(_jnp.mean(_jnp.square(w)) + 1e-30)
        else:
            s = MOMENT_SCALE[name]
        km, kv = _jax.random.split(_jax.random.fold_in(key, i + 1))
        out[name] = w
        out["m_" + name] = s * _jax.random.normal(km, w.shape, _jnp.float32)
        out["v_" + name] = (s * s) * _jax.random.uniform(kv, w.shape, _jnp.float32, 0.5, 1.5)
    if N_MICROBATCH > 1:
        for name, axis in PER_EXAMPLE_BATCH_AXIS.items():
            out[name] = _to_microbatches(out[name], axis)
    return {'x': out['x'], 'mem': out['mem'], 'norm1_g': out['norm1_g'], 'w_in_a': out['w_in_a'], 'w_in_b': out['w_in_b'], 'w_mem_kv': out['w_mem_kv'], 'mem_norm_g': out['mem_norm_g'], 'w_o': out['w_o'], 'norm2_g': out['norm2_g'], 'w_mlp1': out['w_mlp1'], 'w_mlp2': out['w_mlp2'], 'kv_norm_g': out['kv_norm_g'], 'w_kv_shared': out['w_kv_shared'], 'b_f': out['b_f'], 'final_norm_g': out['final_norm_g'], 'loss_target': out['loss_target'], 'm_norm1_g': out['m_norm1_g'], 'm_w_in_a': out['m_w_in_a'], 'm_w_in_b': out['m_w_in_b'], 'm_w_mem_kv': out['m_w_mem_kv'], 'm_mem_norm_g': out['m_mem_norm_g'], 'm_w_o': out['m_w_o'], 'm_norm2_g': out['m_norm2_g'], 'm_w_mlp1': out['m_w_mlp1'], 'm_w_mlp2': out['m_w_mlp2'], 'm_kv_norm_g': out['m_kv_norm_g'], 'm_w_kv_shared': out['m_w_kv_shared'], 'm_b_f': out['m_b_f'], 'm_final_norm_g': out['m_final_norm_g'], 'v_norm1_g': out['v_norm1_g'], 'v_w_in_a': out['v_w_in_a'], 'v_w_in_b': out['v_w_in_b'], 'v_w_mem_kv': out['v_w_mem_kv'], 'v_mem_norm_g': out['v_mem_norm_g'], 'v_w_o': out['v_w_o'], 'v_norm2_g': out['v_norm2_g'], 'v_w_mlp1': out['v_w_mlp1'], 'v_w_mlp2': out['v_w_mlp2'], 'v_kv_norm_g': out['v_kv_norm_g'], 'v_w_kv_shared': out['v_w_kv_shared'], 'v_b_f': out['v_b_f'], 'v_final_norm_g': out['v_final_norm_g']}


def _loss(weights, diff, rest, loss_target):
    with _jax.named_scope("forward"):
        args = {**rest, TWIN_DIFF_INPUT: diff, **{k: w.astype(_WEIGHT_DTYPES[k]) for k, w in weights.items()}}
        y = _forward(args)
    with _jax.named_scope("loss_head"):
        err = _jnp.square(y.astype(_jnp.float32) - loss_target)
        return 0.5 * _jnp.sum(_jnp.mean(err, axis=-1)) if err.ndim else 0.5 * err


def _adamw(w, g, m, v):
    m = ADAM_B1 * m + (1.0 - ADAM_B1) * g
    v = ADAM_B2 * v + (1.0 - ADAM_B2) * _jnp.square(g)
    m_hat = m / (1.0 - ADAM_B1 ** ADAM_STEP)
    v_hat = v / (1.0 - ADAM_B2 ** ADAM_STEP)
    delta = -ADAM_LR * (m_hat / (_jnp.sqrt(v_hat) + ADAM_EPS) + ADAM_WD * w)
    return delta, m, v


def reference(x, mem, norm1_g, w_in_a, w_in_b, w_mem_kv, mem_norm_g, w_o, norm2_g, w_mlp1, w_mlp2, kv_norm_g, w_kv_shared, b_f, final_norm_g, loss_target, m_norm1_g, m_w_in_a, m_w_in_b, m_w_mem_kv, m_mem_norm_g, m_w_o, m_norm2_g, m_w_mlp1, m_w_mlp2, m_kv_norm_g, m_w_kv_shared, m_b_f, m_final_norm_g, v_norm1_g, v_w_in_a, v_w_in_b, v_w_mem_kv, v_mem_norm_g, v_w_o, v_norm2_g, v_w_mlp1, v_w_mlp2, v_kv_norm_g, v_w_kv_shared, v_b_f, v_final_norm_g):
    given = dict(x=x, mem=mem, norm1_g=norm1_g, w_in_a=w_in_a, w_in_b=w_in_b, w_mem_kv=w_mem_kv, mem_norm_g=mem_norm_g, w_o=w_o, norm2_g=norm2_g, w_mlp1=w_mlp1, w_mlp2=w_mlp2, kv_norm_g=kv_norm_g, w_kv_shared=w_kv_shared, b_f=b_f, final_norm_g=final_norm_g, loss_target=loss_target, m_norm1_g=m_norm1_g, m_w_in_a=m_w_in_a, m_w_in_b=m_w_in_b, m_w_mem_kv=m_w_mem_kv, m_mem_norm_g=m_mem_norm_g, m_w_o=m_w_o, m_norm2_g=m_norm2_g, m_w_mlp1=m_w_mlp1, m_w_mlp2=m_w_mlp2, m_kv_norm_g=m_kv_norm_g, m_w_kv_shared=m_w_kv_shared, m_b_f=m_b_f, m_final_norm_g=m_final_norm_g, v_norm1_g=v_norm1_g, v_w_in_a=v_w_in_a, v_w_in_b=v_w_in_b, v_w_mem_kv=v_w_mem_kv, v_mem_norm_g=v_mem_norm_g, v_w_o=v_w_o, v_norm2_g=v_norm2_g, v_w_mlp1=v_w_mlp1, v_w_mlp2=v_w_mlp2, v_kv_norm_g=v_kv_norm_g, v_w_kv_shared=v_w_kv_shared, v_b_f=v_b_f, v_final_norm_g=v_final_norm_g)
    weights = {n: given[n] for n in TWIN_WEIGHTS}
    shared = {n: given[n] for n in SHARED_INPUTS}
    per_example = {n: given[n] for n in ['x', 'mem']}
    grad_fn = _jax.value_and_grad(_loss, argnums=(0, 1))

    def one_microbatch(ex, loss_target):
        ex = dict(ex)
        diff = ex.pop(TWIN_DIFF_INPUT)
        return grad_fn(weights, diff, {**shared, **ex}, loss_target)

    if N_MICROBATCH == 1:
        loss, (grad_w, grad_x) = one_microbatch(per_example, given["loss_target"])
    else:
        def body(carry, xs):
            loss_sum, grad_sum = carry
            l_k, (gw_k, gx_k) = one_microbatch(xs[0], xs[1])
            with _jax.named_scope("update"):
                return (loss_sum + l_k, _jax.tree.map(_jnp.add, grad_sum, gw_k)), gx_k

        init = (_jnp.zeros((), _jnp.float32), _jax.tree.map(_jnp.zeros_like, weights))
        (loss, grad_w), grad_x = _jax.lax.scan(body, init, (per_example, given["loss_target"]))
    with _jax.named_scope("update"):
        delta_w, new_m, new_v = {}, {}, {}
        for n in TWIN_WEIGHTS:
            delta_w[n], new_m[n], new_v[n] = _adamw(weights[n], grad_w[n], given["m_" + n], given["v_" + n])
    return (loss, grad_x, *[grad_w[n] for n in TWIN_WEIGHTS], *[delta_w[n] for n in TWIN_WEIGHTS],
            *[new_m[n] for n in TWIN_WEIGHTS], *[new_v[n] for n in TWIN_WEIGHTS])
```

```python
import functools
import math

import jax
import jax.numpy as jnp
from jax import lax
from jax.experimental import pallas as pl
from jax.experimental.pallas import tpu as pltpu

F32 = jnp.float32
BF16 = jnp.bfloat16

D_MODEL = 1024
HEAD_DIM = 64
MIX_WIDTH = 512
MEM_WIDTH = 256
DEPTH = 4
N_A = 2
D_FF = 4096
EPS = 1e-6
NEG_INF = -1e30
SCALE = 1.0 / math.sqrt(HEAD_DIM)

ADAM_LR = 0.001
ADAM_B1 = 0.9
ADAM_B2 = 0.999
ADAM_EPS = 1e-08
ADAM_WD = 0.01
ADAM_STEP = 10

LANES = 128
BQ = 128
BK = 128
UNDERFLOW_BOUND = -110.0
VMEM_LIMIT = 56 * 1024 * 1024

MESH = pl.DeviceIdType.MESH
N_CHIPS = 4

BIG = (
    ("w_in_a", (2, 1024, 448), 2),
    ("w_in_b", (2, 256, 768), 1),
    ("w_mem_kv", (4, 256, 512), 1),
    ("w_o", (4, 768, 256), 2),
    ("w_mlp1", (4, 1024, 1024), 2),
    ("w_mlp2", (4, 1024, 1024), 1),
    ("w_kv_shared", (1024, 258), 1),
)
SMALL = (
    ("norm1_g", (4, 1024)),
    ("mem_norm_g", (4, 1024)),
    ("norm2_g", (4, 1024)),
    ("kv_norm_g", (1, 1024)),
    ("final_norm_g", (1, 1024)),
    ("b_f", (1, 1024)),
)
WEIGHT_ORDER = ("norm1_g", "w_in_a", "w_in_b", "w_mem_kv", "mem_norm_g", "w_o", "norm2_g", "w_mlp1",
                "w_mlp2", "kv_norm_g", "w_kv_shared", "b_f", "final_norm_g")


def _rows(shape):
    return math.prod(shape) // D_MODEL


BIG_ROWS = sum(_rows(s) for _, s, _ in BIG)
SMALL_ROWS = sum(_rows(s) for _, s in SMALL)
PACK_TILE = 240
PACK_ROWS = -(-(BIG_ROWS + SMALL_ROWS) // PACK_TILE) * PACK_TILE


def _params(sem=None):
    return pltpu.CompilerParams(dimension_semantics=sem, vmem_limit_bytes=VMEM_LIMIT)


def _pick(n, cands):
    for c in cands:
        if n % c == 0:
            return c
    raise ValueError(f"no tile for {n}")


def _pack_local(big, small, dtype):
    parts = [big[n].reshape(-1, D_MODEL).astype(dtype) for n, _, _ in BIG]
    for n, shp in SMALL:
        a = small[n].astype(dtype)
        if n == "b_f":
            a = jnp.pad(a.reshape(1, -1), ((0, 0), (0, D_MODEL - a.size)))
        parts.append(a.reshape(shp))
    pad = PACK_ROWS - BIG_ROWS - SMALL_ROWS
    parts.append(jnp.zeros((pad, D_MODEL), dtype))
    return jnp.concatenate(parts, axis=0)


def _unpack_local(p):
    out = {}
    off = 0
    for n, shp, _ in BIG:
        r = _rows(shp)
        out[n] = p[off:off + r].reshape(shp)
        off += r
    for n, shp in SMALL:
        r = _rows(shp)
        a = p[off:off + r]
        if n == "b_f":
            out[n] = a[0, :8]
        elif shp[0] == 1:
            out[n] = a.reshape(D_MODEL)
        else:
            out[n] = a
        off += r
    return out


def _unpack_gathered(g):
    out = {}
    off = 0
    for n, shp, ax in BIG:
        r = _rows(shp)
        sec = g[:, off:off + r].reshape((N_CHIPS,) + shp)
        out[n] = jnp.concatenate([sec[j] for j in range(N_CHIPS)], axis=ax)
        off += r
    return out


def _pack_grads(gbig, gsmall):
    small_parts = []
    for n, shp in SMALL:
        a = gsmall[n]
        if n == "b_f":
            a = jnp.pad(a.reshape(1, -1), ((0, 0), (0, D_MODEL - a.size)))
        small_parts.append(a.reshape(shp))
    pad = jnp.zeros((PACK_ROWS - BIG_ROWS - SMALL_ROWS, D_MODEL), F32)
    chunks = []
    for j in range(N_CHIPS):
        parts = []
        for n, shp, ax in BIG:
            w = shp[ax]
            parts.append(lax.slice_in_dim(gbig[n], j * w, (j + 1) * w, axis=ax).reshape(-1, D_MODEL))
        chunks.append(jnp.concatenate(parts + small_parts + [pad], axis=0))
    return jnp.stack(chunks, axis=0)


ANY = pl.BlockSpec(memory_space=pl.ANY)


def _other_chips(x, y):
    return [(1 - x, y), (x, 1 - y), (1 - x, 1 - y)]


def _allgather_chips(w):
    def body(w_ref, o_ref, send_sems, recv_sems, local_sem):
        x, y, c = lax.axis_index("x"), lax.axis_index("y"), lax.axis_index("c")
        me = 2 * x + y
        local = pltpu.make_async_copy(w_ref, o_ref.at[me], local_sem)
        local.start()
        copies = [
            pltpu.make_async_remote_copy(src_ref=w_ref, dst_ref=o_ref.at[me], send_sem=send_sems.at[j],
                                         recv_sem=recv_sems.at[j], device_id=(px, py, c), device_id_type=MESH)
            for j, (px, py) in enumerate(_other_chips(x, y))
        ]
        for cp in copies:
            cp.start()
        for cp in copies:
            cp.wait()
        local.wait()

    return pl.pallas_call(
        body, name="allgather_weights",
        out_shape=jax.ShapeDtypeStruct((N_CHIPS,) + w.shape, w.dtype),
        in_specs=[ANY], out_specs=ANY,
        scratch_shapes=[pltpu.SemaphoreType.DMA((3,)), pltpu.SemaphoreType.DMA((3,)), pltpu.SemaphoreType.DMA],
    )(w)


def _scatter_chips(g4):
    def body(g_ref, o_ref, send_sems, recv_sems, local_sem):
        x, y, c = lax.axis_index("x"), lax.axis_index("y"), lax.axis_index("c")
        me = 2 * x + y
        local = pltpu.make_async_copy(g_ref.at[me], o_ref.at[me], local_sem)
        local.start()
        copies = [
            pltpu.make_async_remote_copy(src_ref=g_ref.at[2 * px + py], dst_ref=o_ref.at[me],
                                         send_sem=send_sems.at[j], recv_sem=recv_sems.at[j],
                                         device_id=(px, py, c), device_id_type=MESH)
            for j, (px, py) in enumerate(_other_chips(x, y))
        ]
        for cp in copies:
            cp.start()
        for cp in copies:
            cp.wait()
        local.wait()

    return pl.pallas_call(
        body, name="scatter_grads",
        out_shape=jax.ShapeDtypeStruct(g4.shape, g4.dtype),
        in_specs=[ANY], out_specs=ANY,
        scratch_shapes=[pltpu.SemaphoreType.DMA((3,)), pltpu.SemaphoreType.DMA((3,)), pltpu.SemaphoreType.DMA],
    )(g4)


def _swap_cores(p):
    def body(p_ref, o_ref, send_sem, recv_sem):
        x, y, c = lax.axis_index("x"), lax.axis_index("y"), lax.axis_index("c")
        cp = pltpu.make_async_remote_copy(src_ref=p_ref, dst_ref=o_ref, send_sem=send_sem, recv_sem=recv_sem,
                                          device_id=(x, y, 1 - c), device_id_type=MESH)
        cp.start()
        cp.wait()

    return pl.pallas_call(
        body, name="swap_cores",
        out_shape=jax.ShapeDtypeStruct(p.shape, p.dtype),
        in_specs=[ANY], out_specs=ANY,
        scratch_shapes=[pltpu.SemaphoreType.DMA, pltpu.SemaphoreType.DMA],
    )(p)


def _sum4(r4):
    _, R, C = r4.shape

    def body(r_ref, o_ref):
        o_ref[...] = ((r_ref[0] + r_ref[1]) + r_ref[2]) + r_ref[3]

    return pl.pallas_call(
        body, name="sum_chips", grid=(R // PACK_TILE,),
        in_specs=[pl.BlockSpec((N_CHIPS, PACK_TILE, C), lambda i: (0, i, 0))],
        out_specs=pl.BlockSpec((PACK_TILE, C), lambda i: (i, 0)),
        out_shape=jax.ShapeDtypeStruct((R, C), F32),
        compiler_params=_params(("parallel",)),
    )(r4)


def _adamw(pa, pb, w, m, v):
    R, C = w.shape
    c1 = 1.0 - ADAM_B1
    c2 = 1.0 - ADAM_B2
    bc1 = 1.0 - ADAM_B1 ** ADAM_STEP
    bc2 = 1.0 - ADAM_B2 ** ADAM_STEP

    def body(pa_ref, pb_ref, w_ref, m_ref, v_ref, g_ref, d_ref, mo_ref, vo_ref):
        g = pa_ref[...] + pb_ref[...]
        mn = ADAM_B1 * m_ref[...] + c1 * g
        vn = ADAM_B2 * v_ref[...] + c2 * (g * g)
        m_hat = mn / bc1
        v_hat = vn / bc2
        g_ref[...] = g
        d_ref[...] = -ADAM_LR * (m_hat / (jnp.sqrt(v_hat) + ADAM_EPS) + ADAM_WD * w_ref[...])
        mo_ref[...] = mn
        vo_ref[...] = vn

    spec = pl.BlockSpec((PACK_TILE, C), lambda i: (i, 0))
    shp = jax.ShapeDtypeStruct((R, C), F32)
    return pl.pallas_call(
        body, name="adamw", grid=(R // PACK_TILE,),
        in_specs=[spec] * 5, out_specs=[spec] * 4, out_shape=[shp] * 4,
        compiler_params=_params(("parallel",)),
    )(pa, pb, w, m, v)


def _rms_fwd(x, g, name):
    R, Dm = x.shape
    tr = _pick(R, (512, 256, 128))

    def body(x_ref, g_ref, o_ref):
        xf = x_ref[...]
        r = lax.rsqrt(jnp.mean(xf * xf, axis=-1, keepdims=True) + EPS)
        o_ref[...] = (xf * r * g_ref[...]).astype(o_ref.dtype)

    return pl.pallas_call(
        body, name=name, grid=(R // tr,),
        in_specs=[pl.BlockSpec((tr, Dm), lambda i: (i, 0)), pl.BlockSpec((1, Dm), lambda i: (0, 0))],
        out_specs=pl.BlockSpec((tr, Dm), lambda i: (i, 0)),
        out_shape=jax.ShapeDtypeStruct((R, Dm), BF16),
        compiler_params=_params(("parallel",)),
    )(x, g)


def _rms_bwd(x, g, dy, dres, name):
    R, Dm = x.shape
    tr = _pick(R, (256, 128))
    has_res = dres is not None

    def body(*refs):
        if has_res:
            x_ref, g_ref, dy_ref, dres_ref, dx_ref, dg_ref = refs
        else:
            x_ref, g_ref, dy_ref, dx_ref, dg_ref = refs
        xf = x_ref[...]
        dy_ = dy_ref[...].astype(F32)
        r = lax.rsqrt(jnp.mean(xf * xf, axis=-1, keepdims=True) + EPS)
        gdy = dy_ * g_ref[...]
        mdot = jnp.mean(xf * gdy, axis=-1, keepdims=True)
        dx = r * gdy - xf * ((r * r * r) * mdot)
        if has_res:
            dx = dres_ref[...] + dx
        dx_ref[...] = dx

        @pl.when(pl.program_id(0) == 0)
        def _():
            dg_ref[...] = jnp.zeros_like(dg_ref)

        dg_ref[...] += jnp.sum(dy_ * (xf * r), axis=0, keepdims=True)

    row = pl.BlockSpec((tr, Dm), lambda i: (i, 0))
    vec = pl.BlockSpec((1, Dm), lambda i: (0, 0))
    ins = [x, g, dy] + ([dres] if has_res else [])
    return pl.pallas_call(
        body, name=name, grid=(R // tr,),
        in_specs=[row, vec, row] + ([row] if has_res else []),
        out_specs=[row, vec],
        out_shape=[jax.ShapeDtypeStruct((R, Dm), F32), jax.ShapeDtypeStruct((1, Dm), F32)],
        compiler_params=_params(("arbitrary",)),
    )(*ins)


def _final_loss(x, g, tgt):
    R, Dm = x.shape
    tr = _pick(R, (256, 128))

    def body(x_ref, g_ref, t_ref, l_ref, dx_ref, dg_ref):
        xf = x_ref[...]
        gv = g_ref[...]
        r = lax.rsqrt(jnp.mean(xf * xf, axis=-1, keepdims=True) + EPS)
        xr = xf * r
        err = xr * gv - t_ref[...]
        dy_ = err * (1.0 / Dm)
        gdy = dy_ * gv
        mdot = jnp.mean(xf * gdy, axis=-1, keepdims=True)
        dx_ref[...] = r * gdy - xf * ((r * r * r) * mdot)

        @pl.when(pl.program_id(0) == 0)
        def _():
            dg_ref[...] = jnp.zeros_like(dg_ref)
            l_ref[...] = jnp.zeros_like(l_ref)

        dg_ref[...] += jnp.sum(dy_ * xr, axis=0, keepdims=True)
        sq = jnp.sum(err * err, axis=1, keepdims=True)
        l_ref[...] += jnp.sum(sq, axis=0, keepdims=True) * (0.5 / Dm)

    row = pl.BlockSpec((tr, Dm), lambda i: (i, 0))
    vec = pl.BlockSpec((1, Dm), lambda i: (0, 0))
    return pl.pallas_call(
        body, name="final_norm_loss", grid=(R // tr,),
        in_specs=[row, vec, row],
        out_specs=[pl.BlockSpec((1, 1), lambda i: (0, 0)), row, vec],
        out_shape=[jax.ShapeDtypeStruct((1, 1), F32), jax.ShapeDtypeStruct((R, Dm), F32),
                   jax.ShapeDtypeStruct((1, Dm), F32)],
        compiler_params=_params(("arbitrary",)),
    )(x, g, tgt)


_DIMS = {"nn": (((1,), (0,)), ((), ())), "nt": (((1,), (1,)), ((), ())), "tn": (((0,), (0,)), ((), ()))}


def _mm(a, b, *, mode="nn", out_dtype=BF16, epi=None, extra=None, name):
    if mode == "nn":
        (M, K), N = a.shape, b.shape[1]
    elif mode == "nt":
        (M, K), N = a.shape, b.shape[0]
    else:
        (K, M), N = a.shape, b.shape[1]
    tm = _pick(M, (1024, 768, 512, 256, 128))
    tn = _pick(N, (1024, 896, 768, 640, 512, 384, 256, 128))
    tk = _pick(K, (512, 384, 256, 128))
    nk = K // tk

    def body(*refs):
        if extra is not None:
            a_ref, b_ref, e_ref = refs[:3]
            outs = refs[3:-1]
        else:
            a_ref, b_ref = refs[:2]
            outs = refs[2:-1]
        acc_ref = refs[-1]
        k = pl.program_id(2)

        @pl.when(k == 0)
        def _():
            acc_ref[...] = jnp.zeros_like(acc_ref)

        acc_ref[...] += lax.dot_general(a_ref[...].astype(BF16), b_ref[...].astype(BF16), _DIMS[mode],
                                        preferred_element_type=F32)

        @pl.when(k == nk - 1)
        def _():
            acc = acc_ref[...]
            if epi is None:
                outs[0][...] = acc.astype(outs[0].dtype)
            elif epi == "add":
                outs[0][...] = (e_ref[...] + acc).astype(outs[0].dtype)
            elif epi == "relu2":
                outs[0][...] = acc.astype(BF16)
                rl = jnp.maximum(acc, 0.0)
                outs[1][...] = (rl * rl).astype(BF16)
            elif epi == "drelu2":
                u = e_ref[...].astype(F32)
                outs[0][...] = (acc * (2.0 * jnp.maximum(u, 0.0))).astype(outs[0].dtype)

    if mode == "tn":
        a_spec = pl.BlockSpec((tk, tm), lambda i, j, k: (k, i))
    else:
        a_spec = pl.BlockSpec((tm, tk), lambda i, j, k: (i, k))
    if mode == "nt":
        b_spec = pl.BlockSpec((tn, tk), lambda i, j, k: (j, k))
    else:
        b_spec = pl.BlockSpec((tk, tn), lambda i, j, k: (k, j))
    o_spec = pl.BlockSpec((tm, tn), lambda i, j, k: (i, j))
    ins, in_specs = [a, b], [a_spec, b_spec]
    if extra is not None:
        ins.append(extra)
        in_specs.append(o_spec)
    if epi == "relu2":
        out_shape = [jax.ShapeDtypeStruct((M, N), BF16)] * 2
        out_specs = [o_spec] * 2
    else:
        out_shape = [jax.ShapeDtypeStruct((M, N), out_dtype)]
        out_specs = [o_spec]
    res = pl.pallas_call(
        body, name=name, grid=(M // tm, N // tn, nk),
        in_specs=in_specs, out_specs=out_specs, out_shape=out_shape,
        scratch_shapes=[pltpu.VMEM((tm, tn), F32)],
        compiler_params=_params(("parallel", "parallel", "arbitrary")),
    )(*ins)
    return res if epi == "relu2" else res[0]


def _dot(a, b):
    return lax.dot_general(a, b, _DIMS["nn"], preferred_element_type=F32)


def _dot_nt(a, b):
    return lax.dot_general(a, b, _DIMS["nt"], preferred_element_type=F32)


def _dot_tn(a, b):
    return lax.dot_general(a, b, _DIMS["tn"], preferred_element_type=F32)


def _split_dot(x, t):
    hi = x.astype(BF16)
    lo = (x - hi.astype(F32)).astype(BF16)
    return _dot(hi, t) + _dot(lo, t)


def _head_pair(ref, scale=None):
    xf = ref[...].astype(F32)
    if scale is not None:
        xf = xf * scale
    is_a = lax.broadcasted_iota(jnp.int32, xf.shape, 1) < HEAD_DIM
    return jnp.where(is_a, xf, 0.0).astype(BF16), jnp.where(is_a, 0.0, xf).astype(BF16)


def _pair_rowsum(x):
    is_a = lax.broadcasted_iota(jnp.int32, x.shape, 1) < HEAD_DIM
    return (jnp.sum(jnp.where(is_a, x, 0.0), axis=1, keepdims=True),
            jnp.sum(jnp.where(is_a, 0.0, x), axis=1, keepdims=True))


def _select_pair(xa, xb):
    is_a = lax.broadcasted_iota(jnp.int32, xa.shape, 1) < HEAD_DIM
    return jnp.where(is_a, xa, xb)


def _two_cols(xa, xb):
    rows = xa.shape[0]
    first = lax.broadcasted_iota(jnp.int32, (rows, 2), 1) == 0
    return jnp.where(first, xa, xb)


def _softplus_parts(z):
    e = jnp.exp(-jnp.abs(z))
    return jnp.maximum(z, 0.0) + jnp.log(1.0 + e), e


def _tile_iotas():
    row = lax.broadcasted_iota(jnp.int32, (BQ, BK), 0)
    col = lax.broadcasted_iota(jnp.int32, (BQ, BK), 1)
    return row, col


def _sb_fwd(proj, name):
    S = proj.shape[0]
    nqb = S // BQ

    def body(q_ref, k_ref, v_ref, o_ref, acc_ref):
        i = pl.program_id(1)
        q2 = _head_pair(q_ref, SCALE)
        row, col = _tile_iotas()
        dmask = col < row
        tri = (row > col).astype(BF16)
        acc_ref[...] = jnp.zeros_like(acc_ref)

        def tile(kb, carries, masked):
            r0 = pl.multiple_of(kb * BK, BK)
            kblk = k_ref[pl.ds(r0, BK), :]
            vblk = v_ref[pl.ds(r0, BK), :]
            new = []
            for h in range(2):
                z = _dot_nt(q2[h], kblk)
                sp, _ = _softplus_parts(z)
                lm = -sp
                if masked:
                    lm = jnp.where(dmask, lm, 0.0)
                btw = _split_dot(lm, tri)
                w = jnp.exp((z - sp) + btw + carries[h])
                if masked:
                    w = jnp.where(dmask, w, 0.0)
                acc_ref[h] += _dot(w.astype(BF16), vblk)
                new.append(carries[h] + btw[:, 0:1] + lm[:, 0:1])
            return tuple(new)

        def alive(cs):
            return (jnp.max(jnp.maximum(cs[0], cs[1])) > UNDERFLOW_BOUND).astype(jnp.int32)

        zero = jnp.zeros((BQ, 1), F32)
        cs = tile(i, (zero, zero), True)

        def cond(st):
            return jnp.logical_and(st[0] >= 0, st[1] > 0)

        def step(st):
            kb, _, ca, cb = st
            ca, cb = tile(kb, (ca, cb), False)
            return kb - 1, alive((ca, cb)), ca, cb

        lax.while_loop(cond, step, (i - 1, alive(cs), cs[0], cs[1]))
        o_ref[...] = _select_pair(acc_ref[0], acc_ref[1])

    return pl.pallas_call(
        body, name=name, grid=(4, nqb),
        in_specs=[pl.BlockSpec((BQ, LANES), lambda p, i: (i, p)),
                  pl.BlockSpec((S, LANES), lambda p, i: (0, 4 + p)),
                  pl.BlockSpec((S, LANES), lambda p, i: (0, 8 + p))],
        out_specs=pl.BlockSpec((BQ, LANES), lambda p, i: (i, p)),
        out_shape=jax.ShapeDtypeStruct((S, MIX_WIDTH), F32),
        scratch_shapes=[pltpu.VMEM((2, BQ, LANES), F32)],
        compiler_params=_params(("parallel", "arbitrary")),
    )(proj, proj, proj)


def _sb_bwd(proj, merged, dmerged, name):
    S = proj.shape[0]
    nqb = S // BQ

    def body(q_ref, k_ref, v_ref, o_ref, do_ref, dq_ref, dk_hbm, dv_hbm, dq_acc, dk_acc, dv_acc, sem):
        p = pl.program_id(0)
        i = pl.program_id(1)

        @pl.when(i == 0)
        def _():
            dk_acc[...] = jnp.zeros_like(dk_acc)
            dv_acc[...] = jnp.zeros_like(dv_acc)

        q2 = _head_pair(q_ref, SCALE)
        do2 = _head_pair(do_ref)
        tot = _pair_rowsum(do_ref[...].astype(F32) * o_ref[...])
        row, col = _tile_iotas()
        dmask = col < row
        tri_gt = (row > col).astype(BF16)
        tri_ge = (row >= col).astype(BF16)
        dq_acc[...] = jnp.zeros_like(dq_acc)

        def tile(kb, st, masked):
            r0 = pl.multiple_of(kb * BK, BK)
            kblk = k_ref[pl.ds(r0, BK), :]
            vblk = v_ref[pl.ds(r0, BK), :]
            dk_t = jnp.zeros((BK, LANES), F32)
            dv_t = jnp.zeros((BK, LANES), F32)
            new = []
            for h in range(2):
                c, r = st[h]
                z = _dot_nt(q2[h], kblk)
                sp, e = _softplus_parts(z)
                lm = -sp
                if masked:
                    lm = jnp.where(dmask, lm, 0.0)
                btw = _split_dot(lm, tri_gt)
                w = jnp.exp((z - sp) + btw + c)
                if masked:
                    w = jnp.where(dmask, w, 0.0)
                wb = w.astype(BF16)
                a = wb.astype(F32) * _dot_nt(do2[h], vblk)
                suffix = _split_dot(a, tri_ge) + r
                rcp = 1.0 / (1.0 + e)
                pos = z >= 0.0
                sig = jnp.where(pos, rcp, e * rcp)
                sig_neg = jnp.where(pos, e * rcp, rcp)
                dz = a * sig_neg - (tot[h] - suffix) * sig
                if masked:
                    dz = jnp.where(dmask, dz, 0.0)
                dzb = dz.astype(BF16)
                dq_acc[h] += _dot(dzb, kblk)
                dk_t = dk_t + _dot_tn(dzb, q2[h])
                dv_t = dv_t + _dot_tn(wb, do2[h])
                new.append((c + btw[:, 0:1] + lm[:, 0:1], suffix[:, 0:1]))
            dk_acc[pl.ds(r0, BK), :] += dk_t
            dv_acc[pl.ds(r0, BK), :] += dv_t
            return tuple(new)

        def alive(st):
            return (jnp.max(jnp.maximum(st[0][0], st[1][0])) > UNDERFLOW_BOUND).astype(jnp.int32)

        zero = jnp.zeros((BQ, 1), F32)
        st0 = tile(i, ((zero, zero), (zero, zero)), True)

        def cond(s):
            return jnp.logical_and(s[0] >= 0, s[1] > 0)

        def step(s):
            kb, _, st = s
            st = tile(kb, st, False)
            return kb - 1, alive(st), st

        lax.while_loop(cond, step, (i - 1, alive(st0), st0))
        dq_ref[...] = (_select_pair(dq_acc[0], dq_acc[1]) * SCALE).astype(dq_ref.dtype)

        @pl.when(i == nqb - 1)
        def _():
            ck = pltpu.make_async_copy(dk_acc, dk_hbm.at[p], sem.at[0])
            cv = pltpu.make_async_copy(dv_acc, dv_hbm.at[p], sem.at[1])
            ck.start()
            cv.start()
            ck.wait()
            cv.wait()

    blk = lambda off: pl.BlockSpec((BQ, LANES), lambda p, i: (i, off + p))
    slab = lambda off: pl.BlockSpec((S, LANES), lambda p, i: (0, off + p))
    return pl.pallas_call(
        body, name=name, grid=(4, nqb),
        in_specs=[blk(0), slab(4), slab(8), blk(0), blk(0)],
        out_specs=[blk(0), ANY, ANY],
        out_shape=[jax.ShapeDtypeStruct((S, MIX_WIDTH), BF16),
                   jax.ShapeDtypeStruct((4, S, LANES), F32), jax.ShapeDtypeStruct((4, S, LANES), F32)],
        scratch_shapes=[pltpu.VMEM((2, BQ, LANES), F32), pltpu.VMEM((S, LANES), F32),
                        pltpu.VMEM((S, LANES), F32), pltpu.SemaphoreType.DMA((2,))],
        compiler_params=_params(("arbitrary", "arbitrary")),
    )(proj, proj, proj, merged, dmerged)


def _key_absmax(k_ref, kmax_ref, nkb):
    def step(kb, m):
        r0 = pl.multiple_of(kb * BK, BK)
        blk = jnp.abs(k_ref[pl.ds(r0, BK), :].astype(F32))
        return jnp.maximum(m, jnp.max(blk, axis=0, keepdims=True))

    kmax_ref[...] = lax.fori_loop(0, nkb, step, jnp.zeros((1, LANES), F32))


def _fox_fwd(proj, kv, c_col, c_row, name):
    S = proj.shape[0]
    nqb = S // BQ

    def body(q_ref, k_ref, v_ref, cc_ref, cr_ref, o_ref, lse_ref, acc_ref, kmax_ref):
        i = pl.program_id(1)

        @pl.when(i == 0)
        def _():
            _key_absmax(k_ref, kmax_ref, nqb)

        q2 = _head_pair(q_ref, SCALE)
        qabs = jnp.abs(q_ref[...].astype(F32) * SCALE) * kmax_ref[...]
        bound = _pair_rowsum(qabs)
        cc = cc_ref[0]
        ct = (cc[:, 0:1], cc[:, 1:2])
        row, col = _tile_iotas()
        dmask = col <= row
        acc_ref[...] = jnp.zeros_like(acc_ref)

        def tile(kb, st, masked):
            r0 = pl.multiple_of(kb * BK, BK)
            kblk = k_ref[pl.ds(r0, BK), :]
            vblk = v_ref[pl.ds(r0, BK), :]
            cs = cr_ref[0, kb]
            new = []
            for h in range(2):
                m, l = st[h]
                z = _dot_nt(q2[h], kblk) + ct[h] - cs[h:h + 1, :]
                if masked:
                    z = jnp.where(dmask, z, NEG_INF)
                m_new = jnp.maximum(m, jnp.max(z, axis=1, keepdims=True))
                alpha = jnp.exp(m - m_new)
                pr = jnp.exp(z - m_new)
                acc_ref[h] = alpha * acc_ref[h] + _split_dot(pr, vblk)
                new.append((m_new, alpha * l + jnp.sum(pr, axis=1, keepdims=True)))
            return tuple(new)

        def alive(kb, st):
            cs = cr_ref[0, jnp.maximum(kb, 0)]
            reach = jnp.maximum(bound[0] + ct[0] - cs[0:1, BK - 1:BK] - st[0][0],
                                bound[1] + ct[1] - cs[1:2, BK - 1:BK] - st[1][0])
            return (jnp.max(reach) > UNDERFLOW_BOUND).astype(jnp.int32)

        neg = jnp.full((BQ, 1), NEG_INF, F32)
        zero = jnp.zeros((BQ, 1), F32)
        st0 = tile(i, ((neg, zero), (neg, zero)), True)

        def cond(s):
            return jnp.logical_and(s[0] >= 0, s[1] > 0)

        def step(s):
            kb, _, st = s
            st = tile(kb, st, False)
            return kb - 1, alive(kb - 1, st), st

        _, _, st = lax.while_loop(cond, step, (i - 1, alive(i - 1, st0), st0))
        o_ref[...] = _select_pair(acc_ref[0] / st[0][1], acc_ref[1] / st[1][1])
        lse_ref[0] = _two_cols(st[0][0] + jnp.log(st[0][1]), st[1][0] + jnp.log(st[1][1]))

    return pl.pallas_call(
        body, name=name, grid=(4, nqb),
        in_specs=[pl.BlockSpec((BQ, LANES), lambda p, i: (i, p)),
                  pl.BlockSpec((S, LANES), lambda p, i: (0, p)),
                  pl.BlockSpec((S, LANES), lambda p, i: (0, 4 + p)),
                  pl.BlockSpec((1, BQ, 2), lambda p, i: (p, i, 0)),
                  pl.BlockSpec((1, nqb, 8, LANES), lambda p, i: (p, 0, 0, 0))],
        out_specs=[pl.BlockSpec((BQ, LANES), lambda p, i: (i, p)),
                   pl.BlockSpec((1, BQ, 2), lambda p, i: (p, i, 0))],
        out_shape=[jax.ShapeDtypeStruct((S, MIX_WIDTH), F32), jax.ShapeDtypeStruct((4, S, 2), F32)],
        scratch_shapes=[pltpu.VMEM((2, BQ, LANES), F32), pltpu.VMEM((1, LANES), F32)],
        compiler_params=_params(("arbitrary", "arbitrary")),
    )(proj, kv, kv, c_col, c_row)


def _fox_bwd(proj, kv, c_col, c_row, lse, merged, dmerged, dk_prev, dv_prev, dc_prev, name):
    S = proj.shape[0]
    nqb = S // BQ

    def body(q_ref, k_ref, v_ref, cc_ref, cr_ref, lse_ref, o_ref, do_ref, dkp_hbm, dvp_hbm, dcp_ref,
             dq_ref, dk_hbm, dv_hbm, dc_ref, dq_acc, dk_acc, dv_acc, kmax_ref, sem):
        p = pl.program_id(0)
        i = pl.program_id(1)

        @pl.when(i == 0)
        def _():
            ck = pltpu.make_async_copy(dkp_hbm.at[p], dk_acc, sem.at[0])
            cv = pltpu.make_async_copy(dvp_hbm.at[p], dv_acc, sem.at[1])
            ck.start()
            cv.start()
            dc_ref[...] = dcp_ref[...]
            _key_absmax(k_ref, kmax_ref, nqb)
            ck.wait()
            cv.wait()

        q2 = _head_pair(q_ref, SCALE)
        do2 = _head_pair(do_ref)
        tot = _pair_rowsum(do_ref[...].astype(F32) * o_ref[...])
        qabs = jnp.abs(q_ref[...].astype(F32) * SCALE) * kmax_ref[...]
        bound = _pair_rowsum(qabs)
        cc = cc_ref[0]
        ct = (cc[:, 0:1], cc[:, 1:2])
        ls = lse_ref[0]
        lse2 = (ls[:, 0:1], ls[:, 1:2])
        row, col = _tile_iotas()
        dmask = col <= row
        sub = lax.broadcasted_iota(jnp.int32, (8, LANES), 0)
        dq_acc[...] = jnp.zeros_like(dq_acc)

        def tile(kb, masked):
            r0 = pl.multiple_of(kb * BK, BK)
            kblk = k_ref[pl.ds(r0, BK), :]
            vblk = v_ref[pl.ds(r0, BK), :]
            cs = cr_ref[0, kb]
            dk_t = jnp.zeros((BK, LANES), F32)
            dv_t = jnp.zeros((BK, LANES), F32)
            dc_t = dc_ref[0, kb]
            for h in range(2):
                z = _dot_nt(q2[h], kblk) + ct[h] - cs[h:h + 1, :]
                if masked:
                    z = jnp.where(dmask, z, NEG_INF)
                pr = jnp.exp(z - lse2[h])
                ds = pr * (_dot_nt(do2[h], vblk) - tot[h])
                dsb = ds.astype(BF16)
                dq_acc[h] += _dot(dsb, kblk)
                dk_t = dk_t + _dot_tn(dsb, q2[h])
                dv_t = dv_t + _dot_tn(pr.astype(BF16), do2[h])
                dc_t = jnp.where(sub == h, dc_t - jnp.sum(ds, axis=0, keepdims=True), dc_t)
            dk_acc[pl.ds(r0, BK), :] += dk_t
            dv_acc[pl.ds(r0, BK), :] += dv_t
            dc_ref[0, kb] = dc_t

        def alive(kb):
            cs = cr_ref[0, jnp.maximum(kb, 0)]
            reach = jnp.maximum(bound[0] + ct[0] - cs[0:1, BK - 1:BK] - lse2[0],
                                bound[1] + ct[1] - cs[1:2, BK - 1:BK] - lse2[1])
            return (jnp.max(reach) > UNDERFLOW_BOUND).astype(jnp.int32)

        tile(i, True)

        def cond(s):
            return jnp.logical_and(s[0] >= 0, s[1] > 0)

        def step(s):
            kb, _ = s
            tile(kb, False)
            return kb - 1, alive(kb - 1)

        lax.while_loop(cond, step, (i - 1, alive(i - 1)))
        dq_ref[...] = (_select_pair(dq_acc[0], dq_acc[1]) * SCALE).astype(dq_ref.dtype)

        @pl.when(i == nqb - 1)
        def _():
            ck = pltpu.make_async_copy(dk_acc, dk_hbm.at[p], sem.at[0])
            cv = pltpu.make_async_copy(dv_acc, dv_hbm.at[p], sem.at[1])
            ck.start()
            cv.start()
            ck.wait()
            cv.wait()

    blk = lambda off: pl.BlockSpec((BQ, LANES), lambda p, i: (i, off + p))
    slab = lambda off: pl.BlockSpec((S, LANES), lambda p, i: (0, off + p))
    cols = pl.BlockSpec((1, BQ, 2), lambda p, i: (p, i, 0))
    rows = pl.BlockSpec((1, nqb, 8, LANES), lambda p, i: (p, 0, 0, 0))
    return pl.pallas_call(
        body, name=name, grid=(4, nqb),
        in_specs=[blk(0), slab(0), slab(4), cols, rows, cols, blk(0), blk(0), ANY, ANY, rows],
        out_specs=[blk(0), ANY, ANY, rows],
        out_shape=[jax.ShapeDtypeStruct((S, MIX_WIDTH), BF16),
                   jax.ShapeDtypeStruct((4, S, LANES), F32), jax.ShapeDtypeStruct((4, S, LANES), F32),
                   jax.ShapeDtypeStruct((4, nqb, 8, LANES), F32)],
        scratch_shapes=[pltpu.VMEM((2, BQ, LANES), F32), pltpu.VMEM((S, LANES), F32),
                        pltpu.VMEM((S, LANES), F32), pltpu.VMEM((1, LANES), F32),
                        pltpu.SemaphoreType.DMA((2,))],
        compiler_params=_params(("arbitrary", "arbitrary")),
    )(proj, kv, kv, c_col, c_row, lse, merged, dmerged, dk_prev, dv_prev, dc_prev)


def _lane_scan(x, reverse):
    lane = lax.broadcasted_iota(jnp.int32, x.shape, 1)
    d = 1
    while d < LANES:
        if reverse:
            x = x + jnp.where(lane < LANES - d, pltpu.roll(x, LANES - d, 1), 0.0)
        else:
            x = x + jnp.where(lane >= d, pltpu.roll(x, d, 1), 0.0)
        d *= 2
    return x


def _gate_fwd(fl3, b8):
    nb = fl3.shape[0]

    def body(fl_ref, b_ref, c_ref):
        def step(kb, carry):
            x = fl_ref[kb] + b_ref[...]
            sp, _ = _softplus_parts(-x)
            c = _lane_scan(-sp, False) + carry
            c_ref[kb] = c
            return c[:, LANES - 1:LANES]

        lax.fori_loop(0, nb, step, jnp.zeros((8, 1), F32))

    return pl.pallas_call(body, name="forget_gate_cumsum",
                          out_shape=jax.ShapeDtypeStruct(fl3.shape, F32),
                          compiler_params=_params())(fl3, b8)


def _gate_bwd(dc3, fl3, b8):
    nb = fl3.shape[0]

    def body(dc_ref, fl_ref, b_ref, dfl_ref, db_ref):
        def step(t, st):
            carry, dbs = st
            kb = nb - 1 - t
            g = _lane_scan(dc_ref[kb], True) + carry
            x = fl_ref[kb] + b_ref[...]
            e = jnp.exp(-jnp.abs(x))
            rcp = 1.0 / (1.0 + e)
            dfl = g * jnp.where(x >= 0.0, e * rcp, rcp)
            dfl_ref[kb] = dfl
            return g[:, 0:1], dbs + dfl

        _, dbs = lax.fori_loop(0, nb, step, (jnp.zeros((8, 1), F32), jnp.zeros((8, LANES), F32)))
        db_ref[...] = jnp.broadcast_to(jnp.sum(dbs, axis=1, keepdims=True), (8, LANES))

    return pl.pallas_call(body, name="forget_gate_bwd",
                          out_shape=[jax.ShapeDtypeStruct(fl3.shape, F32), jax.ShapeDtypeStruct((8, LANES), F32)],
                          compiler_params=_params())(dc3, fl3, b8)


MEM_TQ = 256


def _mem_fwd(proj, qcol, mkv, name):
    S = proj.shape[0]
    M = mkv.shape[0]

    def body(q_ref, mk_ref, mv_ref, o_ref, lse_ref):
        q2 = _head_pair(q_ref, SCALE)
        outs, lses = [], []
        for h in range(2):
            s = _dot_nt(q2[h], mk_ref[...])
            m = jnp.max(s, axis=1, keepdims=True)
            pr = jnp.exp(s - m)
            l = jnp.sum(pr, axis=1, keepdims=True)
            outs.append(_dot(pr.astype(BF16), mv_ref[...]) / l)
            lses.append(m + jnp.log(l))
        o_ref[...] = _select_pair(outs[0], outs[1])
        lse_ref[0] = _two_cols(lses[0], lses[1])

    return pl.pallas_call(
        body, name=name, grid=(2, S // MEM_TQ),
        in_specs=[pl.BlockSpec((MEM_TQ, LANES), lambda p, i: (i, qcol + p)),
                  pl.BlockSpec((M, LANES), lambda p, i: (0, p)),
                  pl.BlockSpec((M, LANES), lambda p, i: (0, 2 + p))],
        out_specs=[pl.BlockSpec((MEM_TQ, LANES), lambda p, i: (i, p)),
                   pl.BlockSpec((1, MEM_TQ, 2), lambda p, i: (p, i, 0))],
        out_shape=[jax.ShapeDtypeStruct((S, MEM_WIDTH), F32), jax.ShapeDtypeStruct((2, S, 2), F32)],
        compiler_params=_params(("parallel", "parallel")),
    )(proj, mkv, mkv)


def _mem_bwd(proj, qcol, mkv, lse, merged, dmerged, name):
    S = proj.shape[0]
    M = mkv.shape[0]

    def body(q_ref, mk_ref, mv_ref, lse_ref, o_ref, do_ref, dq_ref, dmk_ref, dmv_ref):
        @pl.when(pl.program_id(1) == 0)
        def _():
            dmk_ref[...] = jnp.zeros_like(dmk_ref)
            dmv_ref[...] = jnp.zeros_like(dmv_ref)

        q2 = _head_pair(q_ref, SCALE)
        do2 = _head_pair(do_ref)
        tot = _pair_rowsum(do_ref[...].astype(F32) * o_ref[...])
        ls = lse_ref[0]
        dqs = []
        for h in range(2):
            s = _dot_nt(q2[h], mk_ref[...])
            pr = jnp.exp(s - ls[:, h:h + 1])
            ds = pr * (_dot_nt(do2[h], mv_ref[...]) - tot[h])
            dsb = ds.astype(BF16)
            dqs.append(_dot(dsb, mk_ref[...]))
            dmk_ref[...] += _dot_tn(dsb, q2[h])
            dmv_ref[...] += _dot_tn(pr.astype(BF16), do2[h])
        dq_ref[...] = (_select_pair(dqs[0], dqs[1]) * SCALE).astype(dq_ref.dtype)

    blk = lambda off: pl.BlockSpec((MEM_TQ, LANES), lambda p, i: (i, off + p))
    acc = pl.BlockSpec((M, LANES), lambda p, i: (0, p))
    return pl.pallas_call(
        body, name=name, grid=(2, S // MEM_TQ),
        in_specs=[blk(qcol), pl.BlockSpec((M, LANES), lambda p, i: (0, p)),
                  pl.BlockSpec((M, LANES), lambda p, i: (0, 2 + p)),
                  pl.BlockSpec((1, MEM_TQ, 2), lambda p, i: (p, i, 0)), blk(4), blk(4)],
        out_specs=[blk(0), acc, acc],
        out_shape=[jax.ShapeDtypeStruct((S, MEM_WIDTH), BF16), jax.ShapeDtypeStruct((M, MEM_WIDTH), F32),
                   jax.ShapeDtypeStruct((M, MEM_WIDTH), F32)],
        compiler_params=_params(("parallel", "arbitrary")),
    )(proj, mkv, mkv, lse, merged, dmerged)


def _c_layouts(c3):
    nb = c3.shape[0]
    pairs = c3.reshape(nb, 4, 2, LANES).transpose(1, 0, 2, 3)
    c_row = jnp.pad(pairs, ((0, 0), (0, 0), (0, 6), (0, 0)))
    c_col = pairs.transpose(0, 1, 3, 2).reshape(4, nb * LANES, 2)
    return c_col, c_row


def _local_step(x, mem, wb, sm, loss_target):
    S = x.shape[0]
    nb = S // BK
    vec = lambda a: a.reshape(1, D_MODEL)
    w_kvf = jnp.pad(wb["w_kv_shared"], ((0, 0), (0, 1152 - 1032)))
    b8 = jnp.broadcast_to(sm["b_f"].reshape(8, 1), (8, LANES))

    saved = []
    shared = None
    h = x
    for l in range(DEPTH):
        if l == N_A:
            hs = _rms_fwd(h, vec(sm["kv_norm_g"]), "kv_norm")
            kvf = _mm(hs, w_kvf, out_dtype=F32, name="kv_shared_proj")
            kv = kvf[:, :2 * MIX_WIDTH].astype(BF16)
            fl3 = kvf[:, 2 * MIX_WIDTH:2 * MIX_WIDTH + 8].T.reshape(8, nb, LANES).transpose(1, 0, 2)
            c3 = _gate_fwd(fl3, b8)
            c_col, c_row = _c_layouts(c3)
            shared = dict(h=h, hs=hs, kv=kv, fl3=fl3, c_col=c_col, c_row=c_row)
        hn = _rms_fwd(h, vec(sm["norm1_g"][l]), f"norm1_{l}")
        mn = _rms_fwd(mem, vec(sm["mem_norm_g"][l]), f"mem_norm_{l}")
        mkv = _mm(mn, wb["w_mem_kv"][l], name=f"mem_kv_proj_{l}")
        if l < N_A:
            w_in = wb["w_in_a"][l]
            proj = _mm(hn, w_in, name=f"in_proj_{l}")
            mix = _sb_fwd(proj, f"stickbreak_fwd_{l}")
            lse, qcol = None, 12
        else:
            w_in = wb["w_in_b"][l - N_A]
            proj = _mm(hn, w_in, name=f"in_proj_{l}")
            mix, lse = _fox_fwd(proj, shared["kv"], shared["c_col"], shared["c_row"], f"fox_fwd_{l}")
            qcol = 4
        mo, mlse = _mem_fwd(proj, qcol, mkv, f"mem_attn_fwd_{l}")
        merged = jnp.concatenate([mix, mo], axis=1)
        h_mid = _mm(merged, wb["w_o"][l], out_dtype=F32, epi="add", extra=h, name=f"out_proj_{l}")
        hn2 = _rms_fwd(h_mid, vec(sm["norm2_g"][l]), f"norm2_{l}")
        u, act = _mm(hn2, wb["w_mlp1"][l], epi="relu2", name=f"mlp1_{l}")
        h_out = _mm(act, wb["w_mlp2"][l], out_dtype=F32, epi="add", extra=h_mid, name=f"mlp2_{l}")
        saved.append(dict(h=h, hn=hn, mn=mn, mkv=mkv, proj=proj, lse=lse, mlse=mlse, qcol=qcol, merged=merged,
                          h_mid=h_mid, hn2=hn2, u=u, act=act, w_in=w_in))
        h = h_out

    loss, dh, dg_final = _final_loss(h, vec(sm["final_norm_g"]), loss_target)

    gb = {n: [None] * (DEPTH if n not in ("w_in_a", "w_in_b") else 2) for n in
          ("w_in_a", "w_in_b", "w_mem_kv", "w_o", "w_mlp1", "w_mlp2")}
    gs = {n: [None] * DEPTH for n in ("norm1_g", "mem_norm_g", "norm2_g")}
    dk_sh = jnp.zeros((4, S, LANES), F32)
    dv_sh = jnp.zeros((4, S, LANES), F32)
    dc_sh = jnp.zeros((4, nb, 8, LANES), F32)
    for l in reversed(range(DEPTH)):
        sv = saved[l]
        du = _mm(dh, wb["w_mlp2"][l], mode="nt", epi="drelu2", extra=sv["u"], name=f"mlp2_dx_{l}")
        gb["w_mlp2"][l] = _mm(sv["act"], dh, mode="tn", out_dtype=F32, name=f"mlp2_dw_{l}")
        gb["w_mlp1"][l] = _mm(sv["hn2"], du, mode="tn", out_dtype=F32, name=f"mlp1_dw_{l}")
        dhn2 = _mm(du, wb["w_mlp1"][l], mode="nt", out_dtype=F32, name=f"mlp1_dx_{l}")
        dh, gs["norm2_g"][l] = _rms_bwd(sv["h_mid"], vec(sm["norm2_g"][l]), dhn2, dh, f"norm2_bwd_{l}")
        dmerged = _mm(dh, wb["w_o"][l], mode="nt", name=f"out_proj_dx_{l}")
        gb["w_o"][l] = _mm(sv["merged"], dh, mode="tn", out_dtype=F32, name=f"out_proj_dw_{l}")
        if l < N_A:
            dq, dk, dv = _sb_bwd(sv["proj"], sv["merged"], dmerged, f"stickbreak_bwd_{l}")
        else:
            dq, dk_sh, dv_sh, dc_sh = _fox_bwd(sv["proj"], shared["kv"], shared["c_col"], shared["c_row"],
                                               sv["lse"], sv["merged"], dmerged, dk_sh, dv_sh, dc_sh,
                                               f"fox_bwd_{l}")
        dqm, dmk, dmv = _mem_bwd(sv["proj"], sv["qcol"], sv["mkv"], sv["mlse"], sv["merged"], dmerged,
                                 f"mem_attn_bwd_{l}")
        if l < N_A:
            flat = lambda t: t.transpose(1, 0, 2).reshape(S, MIX_WIDTH).astype(BF16)
            dproj = jnp.concatenate([dq, flat(dk), flat(dv), dqm], axis=1)
        else:
            dproj = jnp.concatenate([dq, dqm], axis=1)
        name_in = "w_in_a" if l < N_A else "w_in_b"
        gb[name_in][l if l < N_A else l - N_A] = _mm(sv["hn"], dproj, mode="tn", out_dtype=F32,
                                                      name=f"in_proj_dw_{l}")
        dhn = _mm(dproj, sv["w_in"], mode="nt", out_dtype=F32, name=f"in_proj_dx_{l}")
        dh, gs["norm1_g"][l] = _rms_bwd(sv["h"], vec(sm["norm1_g"][l]), dhn, dh, f"norm1_bwd_{l}")
        dmkv = jnp.concatenate([dmk, dmv], axis=1)
        gb["w_mem_kv"][l] = _mm(sv["mn"], dmkv, mode="tn", out_dtype=F32, name=f"mem_kv_dw_{l}")
        dmn = _mm(dmkv, wb["w_mem_kv"][l], mode="nt", out_dtype=F32, name=f"mem_kv_dx_{l}")
        _, gs["mem_norm_g"][l] = _rms_bwd(mem, vec(sm["mem_norm_g"][l]), dmn, None, f"mem_norm_bwd_{l}")
        if l == N_A:
            dfl3, db8 = _gate_bwd(dc_sh.reshape(4, nb, 8, LANES)[:, :, :2].transpose(1, 0, 2, 3).reshape(nb, 8, LANES),
                                  shared["fl3"], b8)
            dfl = dfl3.transpose(1, 0, 2).reshape(8, S).T
            flat = lambda t: t.transpose(1, 0, 2).reshape(S, MIX_WIDTH).astype(BF16)
            dkvf = jnp.concatenate([flat(dk_sh), flat(dv_sh),
                                    jnp.pad(dfl, ((0, 0), (0, LANES - 8))).astype(BF16)], axis=1)
            g_kvf = _mm(shared["hs"], dkvf, mode="tn", out_dtype=F32, name="kv_shared_dw")
            dhs = _mm(dkvf, w_kvf, mode="nt", out_dtype=F32, name="kv_shared_dx")
            dh, g_kvn = _rms_bwd(shared["h"], vec(sm["kv_norm_g"]), dhs, dh, "kv_norm_bwd")
            g_bf = db8[:, 0]

    gbig = {n: jnp.stack(v, axis=0) for n, v in gb.items()}
    gbig["w_kv_shared"] = g_kvf[:, :1032]
    gsmall = {n: jnp.concatenate(v, axis=0) for n, v in gs.items()}
    gsmall["kv_norm_g"] = g_kvn
    gsmall["final_norm_g"] = dg_final
    gsmall["b_f"] = g_bf
    return loss, dh, gbig, gsmall


def kernel(x, mem, norm1_g, w_in_a, w_in_b, w_mem_kv, mem_norm_g, w_o, norm2_g, w_mlp1, w_mlp2, kv_norm_g, w_kv_shared, b_f, final_norm_g, loss_target, m_norm1_g, m_w_in_a, m_w_in_b, m_w_mem_kv, m_mem_norm_g, m_w_o, m_norm2_g, m_w_mlp1, m_w_mlp2, m_kv_norm_g, m_w_kv_shared, m_b_f, m_final_norm_g, v_norm1_g, v_w_in_a, v_w_in_b, v_w_mem_kv, v_mem_norm_g, v_w_o, v_norm2_g, v_w_mlp1, v_w_mlp2, v_kv_norm_g, v_w_kv_shared, v_b_f, v_final_norm_g):
    big_w = dict(w_in_a=w_in_a, w_in_b=w_in_b, w_mem_kv=w_mem_kv, w_o=w_o, w_mlp1=w_mlp1, w_mlp2=w_mlp2,
                 w_kv_shared=w_kv_shared)
    small_w = dict(norm1_g=norm1_g, mem_norm_g=mem_norm_g, norm2_g=norm2_g, kv_norm_g=kv_norm_g,
                   final_norm_g=final_norm_g, b_f=b_f)
    big_m = dict(w_in_a=m_w_in_a, w_in_b=m_w_in_b, w_mem_kv=m_w_mem_kv, w_o=m_w_o, w_mlp1=m_w_mlp1,
                 w_mlp2=m_w_mlp2, w_kv_shared=m_w_kv_shared)
    small_m = dict(norm1_g=m_norm1_g, mem_norm_g=m_mem_norm_g, norm2_g=m_norm2_g, kv_norm_g=m_kv_norm_g,
                   final_norm_g=m_final_norm_g, b_f=m_b_f)
    big_v = dict(w_in_a=v_w_in_a, w_in_b=v_w_in_b, w_mem_kv=v_w_mem_kv, w_o=v_w_o, w_mlp1=v_w_mlp1,
                 w_mlp2=v_w_mlp2, w_kv_shared=v_w_kv_shared)
    small_v = dict(norm1_g=v_norm1_g, mem_norm_g=v_mem_norm_g, norm2_g=v_norm2_g, kv_norm_g=v_kv_norm_g,
                   final_norm_g=v_final_norm_g, b_f=v_b_f)

    gathered = _allgather_chips(_pack_local(big_w, small_w, BF16))
    wb = _unpack_gathered(gathered)

    loss, dx, gbig, gsmall = _local_step(x[0], mem[0], wb, small_w, loss_target[0])

    received = _scatter_chips(_pack_grads(gbig, gsmall))
    part = _sum4(received)
    other = _swap_cores(part)
    g, delta, new_m, new_v = _adamw(part, other, _pack_local(big_w, small_w, F32),
                                    _pack_local(big_m, small_m, F32), _pack_local(big_v, small_v, F32))

    outs = [lax.psum(loss[0, 0], ("x", "y", "c")), dx[None]]
    for packed in (g, delta, new_m, new_v):
        d = _unpack_local(packed)
        outs.extend(d[n] for n in WEIGHT_ORDER)
    return tuple(outs)
```

```python
import functools
import math

import jax
import jax.numpy as jnp
from jax import lax
from jax.experimental import pallas as pl
from jax.experimental.pallas import tpu as pltpu

F32 = jnp.float32
BF16 = jnp.bfloat16

D_MODEL = 1024
HEAD_DIM = 64
MIX_WIDTH = 512
MEM_WIDTH = 256
DEPTH = 4
N_A = 2
D_FF = 4096
EPS = 1e-6
NEG_INF = -1e30
SCALE = 1.0 / math.sqrt(HEAD_DIM)

ADAM_LR = 0.001
ADAM_B1 = 0.9
ADAM_B2 = 0.999
ADAM_EPS = 1e-08
ADAM_WD = 0.01
ADAM_STEP = 10

LANES = 128
BQ = 128
BK = 128
UNDERFLOW_BOUND = -110.0
VMEM_LIMIT = 56 * 1024 * 1024

MESH = pl.DeviceIdType.MESH
N_CHIPS = 4

BIG = (
    ("w_in_a", (2, 1024, 448), 2),
    ("w_in_b", (2, 256, 768), 1),
    ("w_mem_kv", (4, 256, 512), 1),
    ("w_o", (4, 768, 256), 2),
    ("w_mlp1", (4, 1024, 1024), 2),
    ("w_mlp2", (4, 1024, 1024), 1),
    ("w_kv_shared", (1024, 258), 1),
)
SMALL = (
    ("norm1_g", (4, 1024)),
    ("mem_norm_g", (4, 1024)),
    ("norm2_g", (4, 1024)),
    ("kv_norm_g", (1, 1024)),
    ("final_norm_g", (1, 1024)),
    ("b_f", (1, 1024)),
)
WEIGHT_ORDER = ("norm1_g", "w_in_a", "w_in_b", "w_mem_kv", "mem_norm_g", "w_o", "norm2_g", "w_mlp1",
                "w_mlp2", "kv_norm_g", "w_kv_shared", "b_f", "final_norm_g")


def _rows(shape):
    return math.prod(shape) // D_MODEL


ROW_ALIGN = 16


def _padded_rows(shape):
    return -(-_rows(shape) // ROW_ALIGN) * ROW_ALIGN


BIG_ROWS = sum(_padded_rows(s) for _, s, _ in BIG)
SMALL_ROWS = ROW_ALIGN
assert sum(_rows(s) for _, s in SMALL) <= SMALL_ROWS
PACK_TILE = 240
PACK_ROWS = BIG_ROWS + SMALL_ROWS
assert PACK_ROWS % PACK_TILE == 0


def _params(sem=None):
    return pltpu.CompilerParams(dimension_semantics=sem, vmem_limit_bytes=VMEM_LIMIT)


def _pick(n, cands):
    for c in cands:
        if n % c == 0:
            return c
    raise ValueError(f"no tile for {n}")


def _section(a, shape):
    a = a.reshape(-1, D_MODEL)
    return jnp.pad(a, ((0, _padded_rows(shape) - a.shape[0]), (0, 0)))


def _small_block(small, dtype):
    blk = jnp.zeros((SMALL_ROWS, D_MODEL), dtype)
    off = 0
    for n, shp in SMALL:
        a = small[n].astype(dtype)
        if n == "b_f":
            blk = blk.at[off, :a.size].set(a.reshape(-1))
        else:
            blk = blk.at[off:off + shp[0]].set(a.reshape(shp))
        off += shp[0]
    return blk


def _pack_local(big, small, dtype):
    parts = [_section(big[n].astype(dtype), s) for n, s, _ in BIG]
    return jnp.concatenate(parts + [_small_block(small, dtype)], axis=0)


def _unpack_local(p):
    out = {}
    off = 0
    for n, shp, _ in BIG:
        out[n] = p[off:off + _rows(shp)].reshape(shp)
        off += _padded_rows(shp)
    for n, shp in SMALL:
        a = p[off:off + shp[0]]
        if n == "b_f":
            out[n] = a[0, :8]
        elif shp[0] == 1:
            out[n] = a.reshape(D_MODEL)
        else:
            out[n] = a
        off += shp[0]
    return out


def _unpack_gathered(g):
    out = {}
    off = 0
    for n, shp, ax in BIG:
        sec = g[:, off:off + _rows(shp)].reshape((N_CHIPS,) + shp)
        out[n] = jnp.concatenate([sec[j] for j in range(N_CHIPS)], axis=ax)
        off += _padded_rows(shp)
    return out


def _pack_grads(gbig, gsmall):
    small = _small_block(gsmall, BF16)
    chunks = []
    for j in range(N_CHIPS):
        parts = []
        for n, shp, ax in BIG:
            w = shp[ax]
            parts.append(_section(lax.slice_in_dim(gbig[n], j * w, (j + 1) * w, axis=ax).astype(BF16), shp))
        chunks.append(jnp.concatenate(parts + [small], axis=0))
    return jnp.stack(chunks, axis=0)


ANY = pl.BlockSpec(memory_space=pl.ANY)


def _other_chips(x, y):
    return [(1 - x, y), (x, 1 - y), (1 - x, 1 - y)]


def _allgather_chips(w):
    half = w.shape[0] // 2

    def body(w_ref, o_ref, send_sems, recv_sems, pass_send, pass_recv, local_sem):
        x, y, c = lax.axis_index("x"), lax.axis_index("y"), lax.axis_index("c")
        me = 2 * x + y
        sibling = (x, y, 1 - c)
        mine = pl.ds(pl.multiple_of(c * half, ROW_ALIGN), half)
        other = pl.ds(pl.multiple_of((1 - c) * half, ROW_ALIGN), half)
        chips = _other_chips(x, y)

        def over_ici(j, chip, rows_of):
            return pltpu.make_async_remote_copy(src_ref=w_ref.at[mine], dst_ref=o_ref.at[rows_of, mine],
                                                send_sem=send_sems.at[j], recv_sem=recv_sems.at[j],
                                                device_id=(chip[0], chip[1], c), device_id_type=MESH)

        def over_d2d(j, rows_of, rows):
            return pltpu.make_async_remote_copy(src_ref=o_ref.at[rows_of, rows], dst_ref=o_ref.at[rows_of, rows],
                                                send_sem=pass_send.at[j], recv_sem=pass_recv.at[j],
                                                device_id=sibling, device_id_type=MESH)

        local = pltpu.make_async_copy(w_ref, o_ref.at[me], local_sem)
        local.start()
        first = [over_ici(j, chip, me) for j, chip in enumerate(chips)]
        for cp in first:
            cp.start()
        passed = []
        for j, chip in enumerate(chips):
            over_ici(j, chip, 2 * chip[0] + chip[1]).wait_recv()
            passed.append(over_d2d(j, 2 * chip[0] + chip[1], mine))
            passed[j].start()
        for j, chip in enumerate(chips):
            over_d2d(j, 2 * chip[0] + chip[1], other).wait_recv()
        for cp in first + passed:
            cp.wait_send()
        local.wait()

    return pl.pallas_call(
        body, name="allgather_weights",
        out_shape=jax.ShapeDtypeStruct((N_CHIPS,) + w.shape, w.dtype),
        in_specs=[ANY], out_specs=ANY,
        scratch_shapes=[pltpu.SemaphoreType.DMA((3,)), pltpu.SemaphoreType.DMA((3,)),
                        pltpu.SemaphoreType.DMA((3,)), pltpu.SemaphoreType.DMA((3,)), pltpu.SemaphoreType.DMA],
    )(w)


def _scatter_chips(g4):
    def body(g_ref, o_ref, send_sems, recv_sems, local_sem):
        x, y, c = lax.axis_index("x"), lax.axis_index("y"), lax.axis_index("c")
        me = 2 * x + y
        local = pltpu.make_async_copy(g_ref.at[me], o_ref.at[me], local_sem)
        local.start()
        copies = [
            pltpu.make_async_remote_copy(src_ref=g_ref.at[2 * px + py], dst_ref=o_ref.at[me],
                                         send_sem=send_sems.at[j], recv_sem=recv_sems.at[j],
                                         device_id=(px, py, c), device_id_type=MESH)
            for j, (px, py) in enumerate(_other_chips(x, y))
        ]
        for cp in copies:
            cp.start()
        for cp in copies:
            cp.wait()
        local.wait()

    return pl.pallas_call(
        body, name="scatter_grads",
        out_shape=jax.ShapeDtypeStruct(g4.shape, g4.dtype),
        in_specs=[ANY], out_specs=ANY,
        scratch_shapes=[pltpu.SemaphoreType.DMA((3,)), pltpu.SemaphoreType.DMA((3,)), pltpu.SemaphoreType.DMA],
    )(g4)


def _swap_cores(p):
    def body(p_ref, o_ref, send_sem, recv_sem):
        x, y, c = lax.axis_index("x"), lax.axis_index("y"), lax.axis_index("c")
        cp = pltpu.make_async_remote_copy(src_ref=p_ref, dst_ref=o_ref, send_sem=send_sem, recv_sem=recv_sem,
                                          device_id=(x, y, 1 - c), device_id_type=MESH)
        cp.start()
        cp.wait()

    return pl.pallas_call(
        body, name="swap_cores",
        out_shape=jax.ShapeDtypeStruct(p.shape, p.dtype),
        in_specs=[ANY], out_specs=ANY,
        scratch_shapes=[pltpu.SemaphoreType.DMA, pltpu.SemaphoreType.DMA],
    )(p)


def _sum4(r4):
    _, R, C = r4.shape

    def body(r_ref, o_ref):
        o_ref[...] = ((r_ref[0].astype(F32) + r_ref[1].astype(F32)) + r_ref[2].astype(F32)) + r_ref[3].astype(F32)

    return pl.pallas_call(
        body, name="sum_chips", grid=(R // PACK_TILE,),
        in_specs=[pl.BlockSpec((N_CHIPS, PACK_TILE, C), lambda i: (0, i, 0))],
        out_specs=pl.BlockSpec((PACK_TILE, C), lambda i: (i, 0)),
        out_shape=jax.ShapeDtypeStruct((R, C), F32),
        compiler_params=_params(("parallel",)),
    )(r4)


def _adamw(pa, pb, w, m, v):
    R, C = w.shape
    c1 = 1.0 - ADAM_B1
    c2 = 1.0 - ADAM_B2
    bc1 = 1.0 - ADAM_B1 ** ADAM_STEP
    bc2 = 1.0 - ADAM_B2 ** ADAM_STEP

    def body(pa_ref, pb_ref, w_ref, m_ref, v_ref, g_ref, d_ref, mo_ref, vo_ref):
        g = pa_ref[...] + pb_ref[...]
        mn = ADAM_B1 * m_ref[...] + c1 * g
        vn = ADAM_B2 * v_ref[...] + c2 * (g * g)
        m_hat = mn / bc1
        v_hat = vn / bc2
        g_ref[...] = g
        d_ref[...] = -ADAM_LR * (m_hat / (jnp.sqrt(v_hat) + ADAM_EPS) + ADAM_WD * w_ref[...])
        mo_ref[...] = mn
        vo_ref[...] = vn

    spec = pl.BlockSpec((PACK_TILE, C), lambda i: (i, 0))
    shp = jax.ShapeDtypeStruct((R, C), F32)
    return pl.pallas_call(
        body, name="adamw", grid=(R // PACK_TILE,),
        in_specs=[spec] * 5, out_specs=[spec] * 4, out_shape=[shp] * 4,
        compiler_params=_params(("parallel",)),
    )(pa, pb, w, m, v)


def _rms_fwd(x, g, name):
    R, Dm = x.shape
    tr = _pick(R, (512, 256, 128))

    def body(x_ref, g_ref, o_ref):
        xf = x_ref[...]
        r = lax.rsqrt(jnp.mean(xf * xf, axis=-1, keepdims=True) + EPS)
        o_ref[...] = (xf * r * g_ref[...]).astype(o_ref.dtype)

    return pl.pallas_call(
        body, name=name, grid=(R // tr,),
        in_specs=[pl.BlockSpec((tr, Dm), lambda i: (i, 0)), pl.BlockSpec((1, Dm), lambda i: (0, 0))],
        out_specs=pl.BlockSpec((tr, Dm), lambda i: (i, 0)),
        out_shape=jax.ShapeDtypeStruct((R, Dm), BF16),
        compiler_params=_params(("parallel",)),
    )(x, g)


def _rms_bwd(x, g, dy, dres, name):
    R, Dm = x.shape
    tr = _pick(R, (256, 128))
    has_res = dres is not None

    def body(*refs):
        if has_res:
            x_ref, g_ref, dy_ref, dres_ref, dx_ref, dg_ref = refs
        else:
            x_ref, g_ref, dy_ref, dx_ref, dg_ref = refs
        xf = x_ref[...]
        dy_ = dy_ref[...].astype(F32)
        r = lax.rsqrt(jnp.mean(xf * xf, axis=-1, keepdims=True) + EPS)
        gdy = dy_ * g_ref[...]
        mdot = jnp.mean(xf * gdy, axis=-1, keepdims=True)
        dx = r * gdy - xf * ((r * r * r) * mdot)
        if has_res:
            dx = dres_ref[...] + dx
        dx_ref[...] = dx

        @pl.when(pl.program_id(0) == 0)
        def _():
            dg_ref[...] = jnp.zeros_like(dg_ref)

        dg_ref[...] += jnp.sum(dy_ * (xf * r), axis=0, keepdims=True)

    row = pl.BlockSpec((tr, Dm), lambda i: (i, 0))
    vec = pl.BlockSpec((1, Dm), lambda i: (0, 0))
    ins = [x, g, dy] + ([dres] if has_res else [])
    return pl.pallas_call(
        body, name=name, grid=(R // tr,),
        in_specs=[row, vec, row] + ([row] if has_res else []),
        out_specs=[row, vec],
        out_shape=[jax.ShapeDtypeStruct((R, Dm), F32), jax.ShapeDtypeStruct((1, Dm), F32)],
        compiler_params=_params(("arbitrary",)),
    )(*ins)


def _final_loss(x, g, tgt):
    R, Dm = x.shape
    tr = _pick(R, (256, 128))

    def body(x_ref, g_ref, t_ref, l_ref, dx_ref, dg_ref):
        xf = x_ref[...]
        gv = g_ref[...]
        r = lax.rsqrt(jnp.mean(xf * xf, axis=-1, keepdims=True) + EPS)
        xr = xf * r
        err = xr * gv - t_ref[...]
        dy_ = err * (1.0 / Dm)
        gdy = dy_ * gv
        mdot = jnp.mean(xf * gdy, axis=-1, keepdims=True)
        dx_ref[...] = r * gdy - xf * ((r * r * r) * mdot)

        @pl.when(pl.program_id(0) == 0)
        def _():
            dg_ref[...] = jnp.zeros_like(dg_ref)
            l_ref[...] = jnp.zeros_like(l_ref)

        dg_ref[...] += jnp.sum(dy_ * xr, axis=0, keepdims=True)
        sq = jnp.sum(err * err, axis=1, keepdims=True)
        l_ref[...] += jnp.sum(sq, axis=0, keepdims=True) * (0.5 / Dm)

    row = pl.BlockSpec((tr, Dm), lambda i: (i, 0))
    vec = pl.BlockSpec((1, Dm), lambda i: (0, 0))
    return pl.pallas_call(
        body, name="final_norm_loss", grid=(R // tr,),
        in_specs=[row, vec, row],
        out_specs=[pl.BlockSpec((1, 1), lambda i: (0, 0)), row, vec],
        out_shape=[jax.ShapeDtypeStruct((1, 1), F32), jax.ShapeDtypeStruct((R, Dm), F32),
                   jax.ShapeDtypeStruct((1, Dm), F32)],
        compiler_params=_params(("arbitrary",)),
    )(x, g, tgt)


_DIMS = {"nn": (((1,), (0,)), ((), ())), "nt": (((1,), (1,)), ((), ())), "tn": (((0,), (0,)), ((), ()))}


def _mm(a, b, *, mode="nn", out_dtype=BF16, epi=None, extra=None, name):
    if mode == "nn":
        (M, K), N = a.shape, b.shape[1]
    elif mode == "nt":
        (M, K), N = a.shape, b.shape[0]
    else:
        (K, M), N = a.shape, b.shape[1]
    tm = _pick(M, (1024, 768, 512, 256, 128))
    tn = _pick(N, (1024, 896, 768, 640, 512, 384, 256, 128))
    tk = _pick(K, (512, 384, 256, 128))
    nk = K // tk

    def body(*refs):
        if extra is not None:
            a_ref, b_ref, e_ref = refs[:3]
            outs = refs[3:-1]
        else:
            a_ref, b_ref = refs[:2]
            outs = refs[2:-1]
        acc_ref = refs[-1]
        k = pl.program_id(2)

        @pl.when(k == 0)
        def _():
            acc_ref[...] = jnp.zeros_like(acc_ref)

        acc_ref[...] += lax.dot_general(a_ref[...].astype(BF16), b_ref[...].astype(BF16), _DIMS[mode],
                                        preferred_element_type=F32)

        @pl.when(k == nk - 1)
        def _():
            acc = acc_ref[...]
            if epi is None:
                outs[0][...] = acc.astype(outs[0].dtype)
            elif epi == "add":
                outs[0][...] = (e_ref[...] + acc).astype(outs[0].dtype)
            elif epi == "relu2":
                outs[0][...] = acc.astype(BF16)
                rl = jnp.maximum(acc, 0.0)
                outs[1][...] = (rl * rl).astype(BF16)
            elif epi == "drelu2":
                u = e_ref[...].astype(F32)
                outs[0][...] = (acc * (2.0 * jnp.maximum(u, 0.0))).astype(outs[0].dtype)

    if mode == "tn":
        a_spec = pl.BlockSpec((tk, tm), lambda i, j, k: (k, i))
    else:
        a_spec = pl.BlockSpec((tm, tk), lambda i, j, k: (i, k))
    if mode == "nt":
        b_spec = pl.BlockSpec((tn, tk), lambda i, j, k: (j, k))
    else:
        b_spec = pl.BlockSpec((tk, tn), lambda i, j, k: (k, j))
    o_spec = pl.BlockSpec((tm, tn), lambda i, j, k: (i, j))
    ins, in_specs = [a, b], [a_spec, b_spec]
    if extra is not None:
        ins.append(extra)
        in_specs.append(o_spec)
    if epi == "relu2":
        out_shape = [jax.ShapeDtypeStruct((M, N), BF16)] * 2
        out_specs = [o_spec] * 2
    else:
        out_shape = [jax.ShapeDtypeStruct((M, N), out_dtype)]
        out_specs = [o_spec]
    res = pl.pallas_call(
        body, name=name, grid=(M // tm, N // tn, nk),
        in_specs=in_specs, out_specs=out_specs, out_shape=out_shape,
        scratch_shapes=[pltpu.VMEM((tm, tn), F32)],
        compiler_params=_params(("parallel", "parallel", "arbitrary")),
    )(*ins)
    return res if epi == "relu2" else res[0]


def _dot(a, b):
    return lax.dot_general(a, b, _DIMS["nn"], preferred_element_type=F32)


def _dot_nt(a, b):
    return lax.dot_general(a, b, _DIMS["nt"], preferred_element_type=F32)


def _dot_tn(a, b):
    return lax.dot_general(a, b, _DIMS["tn"], preferred_element_type=F32)


def _split_dot(x, t):
    hi = x.astype(BF16)
    lo = (x - hi.astype(F32)).astype(BF16)
    return _dot(hi, t) + _dot(lo, t)


def _head_pair(ref, scale=None):
    xf = ref[...].astype(F32)
    if scale is not None:
        xf = xf * scale
    is_a = lax.broadcasted_iota(jnp.int32, xf.shape, 1) < HEAD_DIM
    return jnp.where(is_a, xf, 0.0).astype(BF16), jnp.where(is_a, 0.0, xf).astype(BF16)


def _stack(a, b):
    return jnp.concatenate([a, b], axis=0)


def _head_rows(ref, scale=None):
    return _stack(*_head_pair(ref, scale))


def _unstack_heads(x):
    rows = x.shape[0] // 2
    return _select_pair(x[:rows], x[rows:])


def _pair_rowsum(x):
    is_a = lax.broadcasted_iota(jnp.int32, x.shape, 1) < HEAD_DIM
    return (jnp.sum(jnp.where(is_a, x, 0.0), axis=1, keepdims=True),
            jnp.sum(jnp.where(is_a, 0.0, x), axis=1, keepdims=True))


def _select_pair(xa, xb):
    is_a = lax.broadcasted_iota(jnp.int32, xa.shape, 1) < HEAD_DIM
    return jnp.where(is_a, xa, xb)


def _two_cols(xa, xb):
    rows = xa.shape[0]
    first = lax.broadcasted_iota(jnp.int32, (rows, 2), 1) == 0
    return jnp.where(first, xa, xb)


def _softplus_parts(z):
    e = jnp.exp(-jnp.abs(z))
    return jnp.maximum(z, 0.0) + jnp.log(1.0 + e), e


def _tile_iotas():
    row = lax.broadcasted_iota(jnp.int32, (BQ, BK), 0)
    col = lax.broadcasted_iota(jnp.int32, (BQ, BK), 1)
    return row, col


def _stacked_iotas(nk):
    row = lax.broadcasted_iota(jnp.int32, (2 * BQ, nk), 0) & (BQ - 1)
    col = lax.broadcasted_iota(jnp.int32, (2 * BQ, nk), 1)
    return row, col


def _sb_fwd(proj, name):
    S = proj.shape[0]
    nqb = S // BQ

    def body(q_ref, k_ref, v_ref, o_ref, acc_ref):
        i = pl.program_id(1)
        q2 = _head_rows(q_ref, SCALE)
        row, col = _tile_iotas()
        tri = (row > col).astype(BF16)
        srow, scol = _stacked_iotas(BK)
        dmask = scol < srow
        acc_ref[...] = jnp.zeros_like(acc_ref)

        def tile(kb, c, masked):
            r0 = pl.multiple_of(kb * BK, BK)
            kblk = k_ref[pl.ds(r0, BK), :]
            vblk = v_ref[pl.ds(r0, BK), :]
            z = _dot_nt(q2, kblk)
            sp, _ = _softplus_parts(z)
            lm = -sp
            if masked:
                lm = jnp.where(dmask, lm, 0.0)
            btw = _split_dot(lm, tri)
            w = jnp.exp((z - sp) + btw + c)
            if masked:
                w = jnp.where(dmask, w, 0.0)
            acc_ref[...] += _dot(w.astype(BF16), vblk)
            return c + btw[:, 0:1] + lm[:, 0:1]

        def alive(c):
            return (jnp.max(c) > UNDERFLOW_BOUND).astype(jnp.int32)

        c0 = tile(i, jnp.zeros((2 * BQ, 1), F32), True)

        def cond(st):
            return jnp.logical_and(st[0] >= 0, st[1] > 0)

        def step(st):
            kb, _, c = st
            c = tile(kb, c, False)
            return kb - 1, alive(c), c

        lax.while_loop(cond, step, (i - 1, alive(c0), c0))
        o_ref[...] = _unstack_heads(acc_ref[...])

    return pl.pallas_call(
        body, name=name, grid=(4, nqb),
        in_specs=[pl.BlockSpec((BQ, LANES), lambda p, i: (i, p)),
                  pl.BlockSpec((S, LANES), lambda p, i: (0, 4 + p)),
                  pl.BlockSpec((S, LANES), lambda p, i: (0, 8 + p))],
        out_specs=pl.BlockSpec((BQ, LANES), lambda p, i: (i, p)),
        out_shape=jax.ShapeDtypeStruct((S, MIX_WIDTH), F32),
        scratch_shapes=[pltpu.VMEM((2 * BQ, LANES), F32)],
        compiler_params=_params(("parallel", "arbitrary")),
    )(proj, proj, proj)


def _sb_bwd(proj, merged, dmerged, name):
    S = proj.shape[0]
    nqb = S // BQ

    def body(q_ref, k_ref, v_ref, o_ref, do_ref, dq_ref, dk_hbm, dv_hbm, dq_acc, dk_acc, dv_acc, sem):
        p = pl.program_id(0)
        i = pl.program_id(1)

        @pl.when(i == 0)
        def _():
            dk_acc[...] = jnp.zeros_like(dk_acc)
            dv_acc[...] = jnp.zeros_like(dv_acc)

        q2 = _head_rows(q_ref, SCALE)
        do2 = _head_rows(do_ref)
        tot = _stack(*_pair_rowsum(do_ref[...].astype(F32) * o_ref[...]))
        row, col = _tile_iotas()
        tri_gt = (row > col).astype(BF16)
        tri_ge = (row >= col).astype(BF16)
        srow, scol = _stacked_iotas(BK)
        dmask = scol < srow
        dq_acc[...] = jnp.zeros_like(dq_acc)

        def tile(kb, st, masked):
            c, r = st
            r0 = pl.multiple_of(kb * BK, BK)
            kblk = k_ref[pl.ds(r0, BK), :]
            vblk = v_ref[pl.ds(r0, BK), :]
            z = _dot_nt(q2, kblk)
            sp, e = _softplus_parts(z)
            lm = -sp
            if masked:
                lm = jnp.where(dmask, lm, 0.0)
            btw = _split_dot(lm, tri_gt)
            w = jnp.exp((z - sp) + btw + c)
            if masked:
                w = jnp.where(dmask, w, 0.0)
            wb = w.astype(BF16)
            a = wb.astype(F32) * _dot_nt(do2, vblk)
            suffix = _split_dot(a, tri_ge) + r
            rcp = 1.0 / (1.0 + e)
            pos = z >= 0.0
            sig = jnp.where(pos, rcp, e * rcp)
            sig_neg = jnp.where(pos, e * rcp, rcp)
            dz = a * sig_neg - (tot - suffix) * sig
            if masked:
                dz = jnp.where(dmask, dz, 0.0)
            dzb = dz.astype(BF16)
            dq_acc[...] += _dot(dzb, kblk)
            dk_acc[pl.ds(r0, BK), :] += _dot_tn(dzb, q2)
            dv_acc[pl.ds(r0, BK), :] += _dot_tn(wb, do2)
            return c + btw[:, 0:1] + lm[:, 0:1], suffix[:, 0:1]

        def alive(st):
            return (jnp.max(st[0]) > UNDERFLOW_BOUND).astype(jnp.int32)

        zero = jnp.zeros((2 * BQ, 1), F32)
        st0 = tile(i, (zero, zero), True)

        def cond(s):
            return jnp.logical_and(s[0] >= 0, s[1] > 0)

        def step(s):
            kb, _, st = s
            st = tile(kb, st, False)
            return kb - 1, alive(st), st

        lax.while_loop(cond, step, (i - 1, alive(st0), st0))
        dq_ref[...] = (_unstack_heads(dq_acc[...]) * SCALE).astype(dq_ref.dtype)

        @pl.when(i == nqb - 1)
        def _():
            ck = pltpu.make_async_copy(dk_acc, dk_hbm.at[p], sem.at[0])
            cv = pltpu.make_async_copy(dv_acc, dv_hbm.at[p], sem.at[1])
            ck.start()
            cv.start()
            ck.wait()
            cv.wait()

    blk = lambda off: pl.BlockSpec((BQ, LANES), lambda p, i: (i, off + p))
    slab = lambda off: pl.BlockSpec((S, LANES), lambda p, i: (0, off + p))
    return pl.pallas_call(
        body, name=name, grid=(4, nqb),
        in_specs=[blk(0), slab(4), slab(8), blk(0), blk(0)],
        out_specs=[blk(0), ANY, ANY],
        out_shape=[jax.ShapeDtypeStruct((S, MIX_WIDTH), BF16),
                   jax.ShapeDtypeStruct((4, S, LANES), F32), jax.ShapeDtypeStruct((4, S, LANES), F32)],
        scratch_shapes=[pltpu.VMEM((2 * BQ, LANES), F32), pltpu.VMEM((S, LANES), F32),
                        pltpu.VMEM((S, LANES), F32), pltpu.SemaphoreType.DMA((2,))],
        compiler_params=_params(("arbitrary", "arbitrary")),
    )(proj, proj, proj, merged, dmerged)


def _key_absmax(k_ref, kmax_ref, nkb):
    def step(kb, m):
        r0 = pl.multiple_of(kb * BK, BK)
        blk = jnp.abs(k_ref[pl.ds(r0, BK), :].astype(F32))
        return jnp.maximum(m, jnp.max(blk, axis=0, keepdims=True))

    kmax_ref[...] = lax.fori_loop(0, nkb, step, jnp.zeros((1, LANES), F32))


FK = 2 * BK


def _key_gates(cr_ref, kb):
    lo = cr_ref[0, 2 * kb]
    hi = cr_ref[0, 2 * kb + 1]
    per_head = [jnp.broadcast_to(jnp.concatenate([lo[h:h + 1], hi[h:h + 1]], axis=1), (BQ, FK)) for h in range(2)]
    return _stack(*per_head)


def _last_gate(cr_ref, kb):
    hi = cr_ref[0, 2 * jnp.maximum(kb, 0) + 1]
    return _stack(*[jnp.broadcast_to(hi[h:h + 1, BK - 1:BK], (BQ, 1)) for h in range(2)])


def _fox_fwd(proj, kv, c_col, c_row, name):
    S = proj.shape[0]
    nqb = S // BQ

    def body(q_ref, k_ref, v_ref, cc_ref, cr_ref, o_ref, lse_ref, acc_ref, kmax_ref):
        i = pl.program_id(1)

        @pl.when(i == 0)
        def _():
            _key_absmax(k_ref, kmax_ref, nqb)

        q2 = _head_rows(q_ref, SCALE)
        qabs = jnp.abs(q_ref[...].astype(F32) * SCALE) * kmax_ref[...]
        bound = _stack(*_pair_rowsum(qabs))
        cc = cc_ref[0]
        ct = _stack(cc[:, 0:1], cc[:, 1:2])
        srow, scol = _stacked_iotas(FK)
        dmask = scol <= srow + (i % 2) * BQ
        acc_ref[...] = jnp.zeros_like(acc_ref)

        def tile(kb, st, masked):
            m, l = st
            r0 = pl.multiple_of(kb * FK, FK)
            kblk = k_ref[pl.ds(r0, FK), :]
            vblk = v_ref[pl.ds(r0, FK), :]
            z = _dot_nt(q2, kblk) + ct - _key_gates(cr_ref, kb)
            if masked:
                z = jnp.where(dmask, z, NEG_INF)
            m_new = jnp.maximum(m, jnp.max(z, axis=1, keepdims=True))
            alpha = jnp.exp(m - m_new)
            pr = jnp.exp(z - m_new)
            acc_ref[...] = alpha * acc_ref[...] + _split_dot(pr, vblk)
            return m_new, alpha * l + jnp.sum(pr, axis=1, keepdims=True)

        def alive(kb, st):
            reach = bound + ct - _last_gate(cr_ref, kb) - st[0]
            return (jnp.max(reach) > UNDERFLOW_BOUND).astype(jnp.int32)

        neg = jnp.full((2 * BQ, 1), NEG_INF, F32)
        zero = jnp.zeros((2 * BQ, 1), F32)
        st0 = tile(i // 2, (neg, zero), True)

        def cond(s):
            return jnp.logical_and(s[0] >= 0, s[1] > 0)

        def step(s):
            kb, _, st = s
            st = tile(kb, st, False)
            return kb - 1, alive(kb - 1, st), st

        _, _, (m, l) = lax.while_loop(cond, step, (i // 2 - 1, alive(i // 2 - 1, st0), st0))
        o_ref[...] = _unstack_heads(acc_ref[...] / l)
        lse = m + jnp.log(l)
        lse_ref[0] = _two_cols(lse[:BQ], lse[BQ:])

    return pl.pallas_call(
        body, name=name, grid=(4, nqb),
        in_specs=[pl.BlockSpec((BQ, LANES), lambda p, i: (i, p)),
                  pl.BlockSpec((S, LANES), lambda p, i: (0, p)),
                  pl.BlockSpec((S, LANES), lambda p, i: (0, 4 + p)),
                  pl.BlockSpec((1, BQ, 2), lambda p, i: (p, i, 0)),
                  pl.BlockSpec((1, nqb, 8, LANES), lambda p, i: (p, 0, 0, 0))],
        out_specs=[pl.BlockSpec((BQ, LANES), lambda p, i: (i, p)),
                   pl.BlockSpec((1, BQ, 2), lambda p, i: (p, i, 0))],
        out_shape=[jax.ShapeDtypeStruct((S, MIX_WIDTH), F32), jax.ShapeDtypeStruct((4, S, 2), F32)],
        scratch_shapes=[pltpu.VMEM((2 * BQ, LANES), F32), pltpu.VMEM((1, LANES), F32)],
        compiler_params=_params(("arbitrary", "arbitrary")),
    )(proj, kv, kv, c_col, c_row)


def _fox_bwd(proj, kv, c_col, c_row, lse, merged, dmerged, dk_prev, dv_prev, dc_prev, name):
    S = proj.shape[0]
    nqb = S // BQ

    def body(q_ref, k_ref, v_ref, cc_ref, cr_ref, lse_ref, o_ref, do_ref, dkp_hbm, dvp_hbm, dcp_ref,
             dq_ref, dk_hbm, dv_hbm, dc_ref, dq_acc, dk_acc, dv_acc, kmax_ref, sem):
        p = pl.program_id(0)
        i = pl.program_id(1)

        @pl.when(i == 0)
        def _():
            ck = pltpu.make_async_copy(dkp_hbm.at[p], dk_acc, sem.at[0])
            cv = pltpu.make_async_copy(dvp_hbm.at[p], dv_acc, sem.at[1])
            ck.start()
            cv.start()
            dc_ref[...] = dcp_ref[...]
            _key_absmax(k_ref, kmax_ref, nqb)
            ck.wait()
            cv.wait()

        q2 = _head_rows(q_ref, SCALE)
        do2 = _head_rows(do_ref)
        tot = _stack(*_pair_rowsum(do_ref[...].astype(F32) * o_ref[...]))
        qabs = jnp.abs(q_ref[...].astype(F32) * SCALE) * kmax_ref[...]
        bound = _stack(*_pair_rowsum(qabs))
        cc = cc_ref[0]
        ct = _stack(cc[:, 0:1], cc[:, 1:2])
        ls = lse_ref[0]
        lse = _stack(ls[:, 0:1], ls[:, 1:2])
        srow, scol = _stacked_iotas(FK)
        dmask = scol <= srow + (i % 2) * BQ
        sub = lax.broadcasted_iota(jnp.int32, (8, LANES), 0)
        dq_acc[...] = jnp.zeros_like(dq_acc)

        def tile(kb, masked):
            r0 = pl.multiple_of(kb * FK, FK)
            kblk = k_ref[pl.ds(r0, FK), :]
            vblk = v_ref[pl.ds(r0, FK), :]
            z = _dot_nt(q2, kblk) + ct - _key_gates(cr_ref, kb)
            if masked:
                z = jnp.where(dmask, z, NEG_INF)
            pr = jnp.exp(z - lse)
            ds = pr * (_dot_nt(do2, vblk) - tot)
            dsb = ds.astype(BF16)
            dq_acc[...] += _dot(dsb, kblk)
            dk_acc[pl.ds(r0, FK), :] += _dot_tn(dsb, q2)
            dv_acc[pl.ds(r0, FK), :] += _dot_tn(pr.astype(BF16), do2)
            dca = jnp.sum(ds[:BQ], axis=0, keepdims=True)
            dcb = jnp.sum(ds[BQ:], axis=0, keepdims=True)
            for j in range(2):
                cols = slice(j * BK, (j + 1) * BK)
                old = dc_ref[0, 2 * kb + j]
                dc_ref[0, 2 * kb + j] = jnp.where(sub == 0, old - dca[:, cols],
                                                  jnp.where(sub == 1, old - dcb[:, cols], old))

        def alive(kb):
            reach = bound + ct - _last_gate(cr_ref, kb) - lse
            return (jnp.max(reach) > UNDERFLOW_BOUND).astype(jnp.int32)

        tile(i // 2, True)

        def cond(s):
            return jnp.logical_and(s[0] >= 0, s[1] > 0)

        def step(s):
            kb, _ = s
            tile(kb, False)
            return kb - 1, alive(kb - 1)

        lax.while_loop(cond, step, (i // 2 - 1, alive(i // 2 - 1)))
        dq_ref[...] = (_unstack_heads(dq_acc[...]) * SCALE).astype(dq_ref.dtype)

        @pl.when(i == nqb - 1)
        def _():
            ck = pltpu.make_async_copy(dk_acc, dk_hbm.at[p], sem.at[0])
            cv = pltpu.make_async_copy(dv_acc, dv_hbm.at[p], sem.at[1])
            ck.start()
            cv.start()
            ck.wait()
            cv.wait()

    blk = lambda off: pl.BlockSpec((BQ, LANES), lambda p, i: (i, off + p))
    slab = lambda off: pl.BlockSpec((S, LANES), lambda p, i: (0, off + p))
    cols = pl.BlockSpec((1, BQ, 2), lambda p, i: (p, i, 0))
    rows = pl.BlockSpec((1, nqb, 8, LANES), lambda p, i: (p, 0, 0, 0))
    return pl.pallas_call(
        body, name=name, grid=(4, nqb),
        in_specs=[blk(0), slab(0), slab(4), cols, rows, cols, blk(0), blk(0), ANY, ANY, rows],
        out_specs=[blk(0), ANY, ANY, rows],
        out_shape=[jax.ShapeDtypeStruct((S, MIX_WIDTH), BF16),
                   jax.ShapeDtypeStruct((4, S, LANES), F32), jax.ShapeDtypeStruct((4, S, LANES), F32),
                   jax.ShapeDtypeStruct((4, nqb, 8, LANES), F32)],
        scratch_shapes=[pltpu.VMEM((2 * BQ, LANES), F32), pltpu.VMEM((S, LANES), F32),
                        pltpu.VMEM((S, LANES), F32), pltpu.VMEM((1, LANES), F32),
                        pltpu.SemaphoreType.DMA((2,))],
        compiler_params=_params(("arbitrary", "arbitrary")),
    )(proj, kv, kv, c_col, c_row, lse, merged, dmerged, dk_prev, dv_prev, dc_prev)


def _lane_scan(x, reverse):
    lane = lax.broadcasted_iota(jnp.int32, x.shape, 1)
    d = 1
    while d < LANES:
        if reverse:
            x = x + jnp.where(lane < LANES - d, pltpu.roll(x, LANES - d, 1), 0.0)
        else:
            x = x + jnp.where(lane >= d, pltpu.roll(x, d, 1), 0.0)
        d *= 2
    return x


def _gate_fwd(fl3, b8):
    nb = fl3.shape[0]

    def body(fl_ref, b_ref, c_ref):
        def step(kb, carry):
            x = fl_ref[kb] + b_ref[...]
            sp, _ = _softplus_parts(-x)
            c = _lane_scan(-sp, False) + carry
            c_ref[kb] = c
            return c[:, LANES - 1:LANES]

        lax.fori_loop(0, nb, step, jnp.zeros((8, 1), F32))

    return pl.pallas_call(body, name="forget_gate_cumsum",
                          out_shape=jax.ShapeDtypeStruct(fl3.shape, F32),
                          compiler_params=_params())(fl3, b8)


def _gate_bwd(dc3, fl3, b8):
    nb = fl3.shape[0]

    def body(dc_ref, fl_ref, b_ref, dfl_ref, db_ref):
        def step(t, st):
            carry, dbs = st
            kb = nb - 1 - t
            g = _lane_scan(dc_ref[kb], True) + carry
            x = fl_ref[kb] + b_ref[...]
            e = jnp.exp(-jnp.abs(x))
            rcp = 1.0 / (1.0 + e)
            dfl = g * jnp.where(x >= 0.0, e * rcp, rcp)
            dfl_ref[kb] = dfl
            return g[:, 0:1], dbs + dfl

        _, dbs = lax.fori_loop(0, nb, step, (jnp.zeros((8, 1), F32), jnp.zeros((8, LANES), F32)))
        db_ref[...] = jnp.broadcast_to(jnp.sum(dbs, axis=1, keepdims=True), (8, LANES))

    return pl.pallas_call(body, name="forget_gate_bwd",
                          out_shape=[jax.ShapeDtypeStruct(fl3.shape, F32), jax.ShapeDtypeStruct((8, LANES), F32)],
                          compiler_params=_params())(dc3, fl3, b8)


MEM_TQ = 256


def _mem_fwd(proj, qcol, mkv, name):
    S = proj.shape[0]
    M = mkv.shape[0]

    def body(q_ref, mk_ref, mv_ref, o_ref, lse_ref):
        q2 = _head_pair(q_ref, SCALE)
        outs, lses = [], []
        for h in range(2):
            s = _dot_nt(q2[h], mk_ref[...])
            m = jnp.max(s, axis=1, keepdims=True)
            pr = jnp.exp(s - m)
            l = jnp.sum(pr, axis=1, keepdims=True)
            outs.append(_dot(pr.astype(BF16), mv_ref[...]) / l)
            lses.append(m + jnp.log(l))
        o_ref[...] = _select_pair(outs[0], outs[1])
        lse_ref[0] = _two_cols(lses[0], lses[1])

    return pl.pallas_call(
        body, name=name, grid=(2, S // MEM_TQ),
        in_specs=[pl.BlockSpec((MEM_TQ, LANES), lambda p, i: (i, qcol + p)),
                  pl.BlockSpec((M, LANES), lambda p, i: (0, p)),
                  pl.BlockSpec((M, LANES), lambda p, i: (0, 2 + p))],
        out_specs=[pl.BlockSpec((MEM_TQ, LANES), lambda p, i: (i, p)),
                   pl.BlockSpec((1, MEM_TQ, 2), lambda p, i: (p, i, 0))],
        out_shape=[jax.ShapeDtypeStruct((S, MEM_WIDTH), F32), jax.ShapeDtypeStruct((2, S, 2), F32)],
        compiler_params=_params(("parallel", "parallel")),
    )(proj, mkv, mkv)


def _mem_bwd(proj, qcol, mkv, lse, merged, dmerged, name):
    S = proj.shape[0]
    M = mkv.shape[0]

    def body(q_ref, mk_ref, mv_ref, lse_ref, o_ref, do_ref, dq_ref, dmk_ref, dmv_ref):
        @pl.when(pl.program_id(1) == 0)
        def _():
            dmk_ref[...] = jnp.zeros_like(dmk_ref)
            dmv_ref[...] = jnp.zeros_like(dmv_ref)

        q2 = _head_pair(q_ref, SCALE)
        do2 = _head_pair(do_ref)
        tot = _pair_rowsum(do_ref[...].astype(F32) * o_ref[...])
        ls = lse_ref[0]
        dqs = []
        for h in range(2):
            s = _dot_nt(q2[h], mk_ref[...])
            pr = jnp.exp(s - ls[:, h:h + 1])
            ds = pr * (_dot_nt(do2[h], mv_ref[...]) - tot[h])
            dsb = ds.astype(BF16)
            dqs.append(_dot(dsb, mk_ref[...]))
            dmk_ref[...] += _dot_tn(dsb, q2[h])
            dmv_ref[...] += _dot_tn(pr.astype(BF16), do2[h])
        dq_ref[...] = (_select_pair(dqs[0], dqs[1]) * SCALE).astype(dq_ref.dtype)

    blk = lambda off: pl.BlockSpec((MEM_TQ, LANES), lambda p, i: (i, off + p))
    acc = pl.BlockSpec((M, LANES), lambda p, i: (0, p))
    return pl.pallas_call(
        body, name=name, grid=(2, S // MEM_TQ),
        in_specs=[blk(qcol), pl.BlockSpec((M, LANES), lambda p, i: (0, p)),
                  pl.BlockSpec((M, LANES), lambda p, i: (0, 2 + p)),
                  pl.BlockSpec((1, MEM_TQ, 2), lambda p, i: (p, i, 0)), blk(4), blk(4)],
        out_specs=[blk(0), acc, acc],
        out_shape=[jax.ShapeDtypeStruct((S, MEM_WIDTH), BF16), jax.ShapeDtypeStruct((M, MEM_WIDTH), F32),
                   jax.ShapeDtypeStruct((M, MEM_WIDTH), F32)],
        compiler_params=_params(("parallel", "arbitrary")),
    )(proj, mkv, mkv, lse, merged, dmerged)


def _c_layouts(c3):
    nb = c3.shape[0]
    pairs = c3.reshape(nb, 4, 2, LANES).transpose(1, 0, 2, 3)
    c_row = jnp.pad(pairs, ((0, 0), (0, 0), (0, 6), (0, 0)))
    c_col = pairs.transpose(0, 1, 3, 2).reshape(4, nb * LANES, 2)
    return c_col, c_row


def _local_step(x, mem, wb, sm, loss_target):
    S = x.shape[0]
    nb = S // BK
    vec = lambda a: a.reshape(1, D_MODEL)
    w_kvf = jnp.pad(wb["w_kv_shared"], ((0, 0), (0, 1152 - 1032)))
    b8 = jnp.broadcast_to(sm["b_f"].reshape(8, 1), (8, LANES))

    saved = []
    shared = None
    h = x
    for l in range(DEPTH):
        if l == N_A:
            hs = _rms_fwd(h, vec(sm["kv_norm_g"]), "kv_norm")
            kvf = _mm(hs, w_kvf, out_dtype=F32, name="kv_shared_proj")
            kv = kvf[:, :2 * MIX_WIDTH].astype(BF16)
            fl3 = kvf[:, 2 * MIX_WIDTH:2 * MIX_WIDTH + 8].T.reshape(8, nb, LANES).transpose(1, 0, 2)
            c3 = _gate_fwd(fl3, b8)
            c_col, c_row = _c_layouts(c3)
            shared = dict(h=h, hs=hs, kv=kv, fl3=fl3, c_col=c_col, c_row=c_row)
        hn = _rms_fwd(h, vec(sm["norm1_g"][l]), f"norm1_{l}")
        mn = _rms_fwd(mem, vec(sm["mem_norm_g"][l]), f"mem_norm_{l}")
        mkv = _mm(mn, wb["w_mem_kv"][l], name=f"mem_kv_proj_{l}")
        if l < N_A:
            w_in = wb["w_in_a"][l]
            proj = _mm(hn, w_in, name=f"in_proj_{l}")
            mix = _sb_fwd(proj, f"stickbreak_fwd_{l}")
            lse, qcol = None, 12
        else:
            w_in = wb["w_in_b"][l - N_A]
            proj = _mm(hn, w_in, name=f"in_proj_{l}")
            mix, lse = _fox_fwd(proj, shared["kv"], shared["c_col"], shared["c_row"], f"fox_fwd_{l}")
            qcol = 4
        mo, mlse = _mem_fwd(proj, qcol, mkv, f"mem_attn_fwd_{l}")
        merged = jnp.concatenate([mix, mo], axis=1)
        h_mid = _mm(merged, wb["w_o"][l], out_dtype=F32, epi="add", extra=h, name=f"out_proj_{l}")
        hn2 = _rms_fwd(h_mid, vec(sm["norm2_g"][l]), f"norm2_{l}")
        u, act = _mm(hn2, wb["w_mlp1"][l], epi="relu2", name=f"mlp1_{l}")
        h_out = _mm(act, wb["w_mlp2"][l], out_dtype=F32, epi="add", extra=h_mid, name=f"mlp2_{l}")
        saved.append(dict(h=h, hn=hn, mn=mn, mkv=mkv, proj=proj, lse=lse, mlse=mlse, qcol=qcol, merged=merged,
                          h_mid=h_mid, hn2=hn2, u=u, act=act, w_in=w_in))
        h = h_out

    loss, dh, dg_final = _final_loss(h, vec(sm["final_norm_g"]), loss_target)

    gb = {n: [None] * (DEPTH if n not in ("w_in_a", "w_in_b") else 2) for n in
          ("w_in_a", "w_in_b", "w_mem_kv", "w_o", "w_mlp1", "w_mlp2")}
    gs = {n: [None] * DEPTH for n in ("norm1_g", "mem_norm_g", "norm2_g")}
    dk_sh = jnp.zeros((4, S, LANES), F32)
    dv_sh = jnp.zeros((4, S, LANES), F32)
    dc_sh = jnp.zeros((4, nb, 8, LANES), F32)
    for l in reversed(range(DEPTH)):
        sv = saved[l]
        du = _mm(dh, wb["w_mlp2"][l], mode="nt", epi="drelu2", extra=sv["u"], name=f"mlp2_dx_{l}")
        gb["w_mlp2"][l] = _mm(sv["act"], dh, mode="tn", out_dtype=F32, name=f"mlp2_dw_{l}")
        gb["w_mlp1"][l] = _mm(sv["hn2"], du, mode="tn", out_dtype=F32, name=f"mlp1_dw_{l}")
        dhn2 = _mm(du, wb["w_mlp1"][l], mode="nt", out_dtype=F32, name=f"mlp1_dx_{l}")
        dh, gs["norm2_g"][l] = _rms_bwd(sv["h_mid"], vec(sm["norm2_g"][l]), dhn2, dh, f"norm2_bwd_{l}")
        dmerged = _mm(dh, wb["w_o"][l], mode="nt", name=f"out_proj_dx_{l}")
        gb["w_o"][l] = _mm(sv["merged"], dh, mode="tn", out_dtype=F32, name=f"out_proj_dw_{l}")
        if l < N_A:
            dq, dk, dv = _sb_bwd(sv["proj"], sv["merged"], dmerged, f"stickbreak_bwd_{l}")
        else:
            dq, dk_sh, dv_sh, dc_sh = _fox_bwd(sv["proj"], shared["kv"], shared["c_col"], shared["c_row"],
                                               sv["lse"], sv["merged"], dmerged, dk_sh, dv_sh, dc_sh,
                                               f"fox_bwd_{l}")
        dqm, dmk, dmv = _mem_bwd(sv["proj"], sv["qcol"], sv["mkv"], sv["mlse"], sv["merged"], dmerged,
                                 f"mem_attn_bwd_{l}")
        if l < N_A:
            flat = lambda t: t.transpose(1, 0, 2).reshape(S, MIX_WIDTH).astype(BF16)
            dproj = jnp.concatenate([dq, flat(dk), flat(dv), dqm], axis=1)
        else:
            dproj = jnp.concatenate([dq, dqm], axis=1)
        name_in = "w_in_a" if l < N_A else "w_in_b"
        gb[name_in][l if l < N_A else l - N_A] = _mm(sv["hn"], dproj, mode="tn", out_dtype=F32,
                                                      name=f"in_proj_dw_{l}")
        dhn = _mm(dproj, sv["w_in"], mode="nt", out_dtype=F32, name=f"in_proj_dx_{l}")
        dh, gs["norm1_g"][l] = _rms_bwd(sv["h"], vec(sm["norm1_g"][l]), dhn, dh, f"norm1_bwd_{l}")
        dmkv = jnp.concatenate([dmk, dmv], axis=1)
        gb["w_mem_kv"][l] = _mm(sv["mn"], dmkv, mode="tn", out_dtype=F32, name=f"mem_kv_dw_{l}")
        dmn = _mm(dmkv, wb["w_mem_kv"][l], mode="nt", out_dtype=F32, name=f"mem_kv_dx_{l}")
        _, gs["mem_norm_g"][l] = _rms_bwd(mem, vec(sm["mem_norm_g"][l]), dmn, None, f"mem_norm_bwd_{l}")
        if l == N_A:
            dfl3, db8 = _gate_bwd(dc_sh.reshape(4, nb, 8, LANES)[:, :, :2].transpose(1, 0, 2, 3).reshape(nb, 8, LANES),
                                  shared["fl3"], b8)
            dfl = dfl3.transpose(1, 0, 2).reshape(8, S).T
            flat = lambda t: t.transpose(1, 0, 2).reshape(S, MIX_WIDTH).astype(BF16)
            dkvf = jnp.concatenate([flat(dk_sh), flat(dv_sh),
                                    jnp.pad(dfl, ((0, 0), (0, LANES - 8))).astype(BF16)], axis=1)
            g_kvf = _mm(shared["hs"], dkvf, mode="tn", out_dtype=F32, name="kv_shared_dw")
            dhs = _mm(dkvf, w_kvf, mode="nt", out_dtype=F32, name="kv_shared_dx")
            dh, g_kvn = _rms_bwd(shared["h"], vec(sm["kv_norm_g"]), dhs, dh, "kv_norm_bwd")
            g_bf = db8[:, 0]

    gbig = {n: jnp.stack(v, axis=0) for n, v in gb.items()}
    gbig["w_kv_shared"] = g_kvf[:, :1032]
    gsmall = {n: jnp.concatenate(v, axis=0) for n, v in gs.items()}
    gsmall["kv_norm_g"] = g_kvn
    gsmall["final_norm_g"] = dg_final
    gsmall["b_f"] = g_bf
    return loss, dh, gbig, gsmall


def kernel(x, mem, norm1_g, w_in_a, w_in_b, w_mem_kv, mem_norm_g, w_o, norm2_g, w_mlp1, w_mlp2, kv_norm_g, w_kv_shared, b_f, final_norm_g, loss_target, m_norm1_g, m_w_in_a, m_w_in_b, m_w_mem_kv, m_mem_norm_g, m_w_o, m_norm2_g, m_w_mlp1, m_w_mlp2, m_kv_norm_g, m_w_kv_shared, m_b_f, m_final_norm_g, v_norm1_g, v_w_in_a, v_w_in_b, v_w_mem_kv, v_mem_norm_g, v_w_o, v_norm2_g, v_w_mlp1, v_w_mlp2, v_kv_norm_g, v_w_kv_shared, v_b_f, v_final_norm_g):
    big_w = dict(w_in_a=w_in_a, w_in_b=w_in_b, w_mem_kv=w_mem_kv, w_o=w_o, w_mlp1=w_mlp1, w_mlp2=w_mlp2,
                 w_kv_shared=w_kv_shared)
    small_w = dict(norm1_g=norm1_g, mem_norm_g=mem_norm_g, norm2_g=norm2_g, kv_norm_g=kv_norm_g,
                   final_norm_g=final_norm_g, b_f=b_f)
    big_m = dict(w_in_a=m_w_in_a, w_in_b=m_w_in_b, w_mem_kv=m_w_mem_kv, w_o=m_w_o, w_mlp1=m_w_mlp1,
                 w_mlp2=m_w_mlp2, w_kv_shared=m_w_kv_shared)
    small_m = dict(norm1_g=m_norm1_g, mem_norm_g=m_mem_norm_g, norm2_g=m_norm2_g, kv_norm_g=m_kv_norm_g,
                   final_norm_g=m_final_norm_g, b_f=m_b_f)
    big_v = dict(w_in_a=v_w_in_a, w_in_b=v_w_in_b, w_mem_kv=v_w_mem_kv, w_o=v_w_o, w_mlp1=v_w_mlp1,
                 w_mlp2=v_w_mlp2, w_kv_shared=v_w_kv_shared)
    small_v = dict(norm1_g=v_norm1_g, mem_norm_g=v_mem_norm_g, norm2_g=v_norm2_g, kv_norm_g=v_kv_norm_g,
                   final_norm_g=v_final_norm_g, b_f=v_b_f)

    gathered = _allgather_chips(_pack_local(big_w, small_w, BF16))
    wb = _unpack_gathered(gathered)

    loss, dx, gbig, gsmall = _local_step(x[0], mem[0], wb, small_w, loss_target[0])

    received = _scatter_chips(_pack_grads(gbig, gsmall))
    part = _sum4(received)
    other = _swap_cores(part)
    g, delta, new_m, new_v = _adamw(part, other, _pack_local(big_w, small_w, F32),
                                    _pack_local(big_m, small_m, F32), _pack_local(big_v, small_v, F32))

    outs = [lax.psum(loss[0, 0], ("x", "y", "c")), dx[None]]
    for packed in (g, delta, new_m, new_v):
        d = _unpack_local(packed)
        outs.extend(d[n] for n in WEIGHT_ORDER)
    return tuple(outs)
```

```python
import functools
import math

import jax
import jax.numpy as jnp
from jax import lax
from jax.experimental import pallas as pl
from jax.experimental.pallas import tpu as pltpu

F32 = jnp.float32
BF16 = jnp.bfloat16

D_MODEL = 1024
HEAD_DIM = 64
MIX_WIDTH = 512
MEM_WIDTH = 256
DEPTH = 4
N_A = 2
D_FF = 4096
EPS = 1e-6
NEG_INF = -1e30
SCALE = 1.0 / math.sqrt(HEAD_DIM)

ADAM_LR = 0.001
ADAM_B1 = 0.9
ADAM_B2 = 0.999
ADAM_EPS = 1e-08
ADAM_WD = 0.01
ADAM_STEP = 10

LANES = 128
BQ = 256
BK = 128
DIAG_TILES = BQ // BK
UNDERFLOW_BOUND = -110.0
VMEM_LIMIT = 56 * 1024 * 1024

MESH = pl.DeviceIdType.MESH
N_CHIPS = 4

BIG = (
    ("w_in_a", (2, 1024, 448), 2),
    ("w_in_b", (2, 256, 768), 1),
    ("w_mem_kv", (4, 256, 512), 1),
    ("w_o", (4, 768, 256), 2),
    ("w_mlp1", (4, 1024, 1024), 2),
    ("w_mlp2", (4, 1024, 1024), 1),
    ("w_kv_shared", (1024, 258), 1),
)
SMALL = (
    ("norm1_g", (4, 1024)),
    ("mem_norm_g", (4, 1024)),
    ("norm2_g", (4, 1024)),
    ("kv_norm_g", (1, 1024)),
    ("final_norm_g", (1, 1024)),
    ("b_f", (1, 1024)),
)
WEIGHT_ORDER = ("norm1_g", "w_in_a", "w_in_b", "w_mem_kv", "mem_norm_g", "w_o", "norm2_g", "w_mlp1",
                "w_mlp2", "kv_norm_g", "w_kv_shared", "b_f", "final_norm_g")


def _rows(shape):
    return math.prod(shape) // D_MODEL


ROW_ALIGN = 16


def _padded_rows(shape):
    return -(-_rows(shape) // ROW_ALIGN) * ROW_ALIGN


BIG_ROWS = sum(_padded_rows(s) for _, s, _ in BIG)
SMALL_ROWS = ROW_ALIGN
assert sum(_rows(s) for _, s in SMALL) <= SMALL_ROWS
PACK_TILE = 240
PACK_ROWS = BIG_ROWS + SMALL_ROWS
assert PACK_ROWS % PACK_TILE == 0


def _params(sem=None):
    return pltpu.CompilerParams(dimension_semantics=sem, vmem_limit_bytes=VMEM_LIMIT)


def _pick(n, cands):
    for c in cands:
        if n % c == 0:
            return c
    raise ValueError(f"no tile for {n}")


def _section(a, shape):
    a = a.reshape(-1, D_MODEL)
    return jnp.pad(a, ((0, _padded_rows(shape) - a.shape[0]), (0, 0)))


def _small_block(small, dtype):
    blk = jnp.zeros((SMALL_ROWS, D_MODEL), dtype)
    off = 0
    for n, shp in SMALL:
        a = small[n].astype(dtype)
        if n == "b_f":
            blk = blk.at[off, :a.size].set(a.reshape(-1))
        else:
            blk = blk.at[off:off + shp[0]].set(a.reshape(shp))
        off += shp[0]
    return blk


def _pack_local(big, small, dtype):
    parts = [_section(big[n].astype(dtype), s) for n, s, _ in BIG]
    return jnp.concatenate(parts + [_small_block(small, dtype)], axis=0)


def _unpack_local(p):
    out = {}
    off = 0
    for n, shp, _ in BIG:
        out[n] = p[off:off + _rows(shp)].reshape(shp)
        off += _padded_rows(shp)
    for n, shp in SMALL:
        a = p[off:off + shp[0]]
        if n == "b_f":
            out[n] = a[0, :8]
        elif shp[0] == 1:
            out[n] = a.reshape(D_MODEL)
        else:
            out[n] = a
        off += shp[0]
    return out


def _unpack_gathered(g):
    out = {}
    off = 0
    for n, shp, ax in BIG:
        sec = g[:, off:off + _rows(shp)].reshape((N_CHIPS,) + shp)
        out[n] = jnp.concatenate([sec[j] for j in range(N_CHIPS)], axis=ax)
        off += _padded_rows(shp)
    return out


def _pack_grads(gbig, gsmall):
    small = _small_block(gsmall, BF16)
    chunks = []
    for j in range(N_CHIPS):
        parts = []
        for n, shp, ax in BIG:
            w = shp[ax]
            parts.append(_section(lax.slice_in_dim(gbig[n], j * w, (j + 1) * w, axis=ax).astype(BF16), shp))
        chunks.append(jnp.concatenate(parts + [small], axis=0))
    return jnp.stack(chunks, axis=0)


ANY = pl.BlockSpec(memory_space=pl.ANY)


def _other_chips(x, y):
    return [(1 - x, y), (x, 1 - y), (1 - x, 1 - y)]


def _allgather_chips(w):
    half = w.shape[0] // 2

    def body(w_ref, o_ref, send_sems, recv_sems, pass_send, pass_recv, local_sem):
        x, y, c = lax.axis_index("x"), lax.axis_index("y"), lax.axis_index("c")
        me = 2 * x + y
        sibling = (x, y, 1 - c)
        mine = pl.ds(pl.multiple_of(c * half, ROW_ALIGN), half)
        other = pl.ds(pl.multiple_of((1 - c) * half, ROW_ALIGN), half)
        chips = _other_chips(x, y)

        def over_ici(j, chip, rows_of):
            return pltpu.make_async_remote_copy(src_ref=w_ref.at[mine], dst_ref=o_ref.at[rows_of, mine],
                                                send_sem=send_sems.at[j], recv_sem=recv_sems.at[j],
                                                device_id=(chip[0], chip[1], c), device_id_type=MESH)

        def over_d2d(j, rows_of, rows):
            return pltpu.make_async_remote_copy(src_ref=o_ref.at[rows_of, rows], dst_ref=o_ref.at[rows_of, rows],
                                                send_sem=pass_send.at[j], recv_sem=pass_recv.at[j],
                                                device_id=sibling, device_id_type=MESH)

        local = pltpu.make_async_copy(w_ref, o_ref.at[me], local_sem)
        local.start()
        first = [over_ici(j, chip, me) for j, chip in enumerate(chips)]
        for cp in first:
            cp.start()
        passed = []
        for j, chip in enumerate(chips):
            over_ici(j, chip, 2 * chip[0] + chip[1]).wait_recv()
            passed.append(over_d2d(j, 2 * chip[0] + chip[1], mine))
            passed[j].start()
        for j, chip in enumerate(chips):
            over_d2d(j, 2 * chip[0] + chip[1], other).wait_recv()
        for cp in first + passed:
            cp.wait_send()
        local.wait()

    return pl.pallas_call(
        body, name="allgather_weights",
        out_shape=jax.ShapeDtypeStruct((N_CHIPS,) + w.shape, w.dtype),
        in_specs=[ANY], out_specs=ANY,
        scratch_shapes=[pltpu.SemaphoreType.DMA((3,)), pltpu.SemaphoreType.DMA((3,)),
                        pltpu.SemaphoreType.DMA((3,)), pltpu.SemaphoreType.DMA((3,)), pltpu.SemaphoreType.DMA],
    )(w)


def _scatter_chips(g4):
    def body(g_ref, o_ref, send_sems, recv_sems, local_sem):
        x, y, c = lax.axis_index("x"), lax.axis_index("y"), lax.axis_index("c")
        me = 2 * x + y
        local = pltpu.make_async_copy(g_ref.at[me], o_ref.at[me], local_sem)
        local.start()
        copies = [
            pltpu.make_async_remote_copy(src_ref=g_ref.at[2 * px + py], dst_ref=o_ref.at[me],
                                         send_sem=send_sems.at[j], recv_sem=recv_sems.at[j],
                                         device_id=(px, py, c), device_id_type=MESH)
            for j, (px, py) in enumerate(_other_chips(x, y))
        ]
        for cp in copies:
            cp.start()
        for cp in copies:
            cp.wait()
        local.wait()

    return pl.pallas_call(
        body, name="scatter_grads",
        out_shape=jax.ShapeDtypeStruct(g4.shape, g4.dtype),
        in_specs=[ANY], out_specs=ANY,
        scratch_shapes=[pltpu.SemaphoreType.DMA((3,)), pltpu.SemaphoreType.DMA((3,)), pltpu.SemaphoreType.DMA],
    )(g4)


def _swap_cores(p):
    def body(p_ref, o_ref, send_sem, recv_sem):
        x, y, c = lax.axis_index("x"), lax.axis_index("y"), lax.axis_index("c")
        cp = pltpu.make_async_remote_copy(src_ref=p_ref, dst_ref=o_ref, send_sem=send_sem, recv_sem=recv_sem,
                                          device_id=(x, y, 1 - c), device_id_type=MESH)
        cp.start()
        cp.wait()

    return pl.pallas_call(
        body, name="swap_cores",
        out_shape=jax.ShapeDtypeStruct(p.shape, p.dtype),
        in_specs=[ANY], out_specs=ANY,
        scratch_shapes=[pltpu.SemaphoreType.DMA, pltpu.SemaphoreType.DMA],
    )(p)


def _sum4(r4):
    _, R, C = r4.shape

    def body(r_ref, o_ref):
        o_ref[...] = ((r_ref[0].astype(F32) + r_ref[1].astype(F32)) + r_ref[2].astype(F32)) + r_ref[3].astype(F32)

    return pl.pallas_call(
        body, name="sum_chips", grid=(R // PACK_TILE,),
        in_specs=[pl.BlockSpec((N_CHIPS, PACK_TILE, C), lambda i: (0, i, 0))],
        out_specs=pl.BlockSpec((PACK_TILE, C), lambda i: (i, 0)),
        out_shape=jax.ShapeDtypeStruct((R, C), F32),
        compiler_params=_params(("parallel",)),
    )(r4)


def _adamw(pa, pb, w, m, v):
    R, C = w.shape
    c1 = 1.0 - ADAM_B1
    c2 = 1.0 - ADAM_B2
    bc1 = 1.0 - ADAM_B1 ** ADAM_STEP
    bc2 = 1.0 - ADAM_B2 ** ADAM_STEP

    def body(pa_ref, pb_ref, w_ref, m_ref, v_ref, g_ref, d_ref, mo_ref, vo_ref):
        g = pa_ref[...] + pb_ref[...]
        mn = ADAM_B1 * m_ref[...] + c1 * g
        vn = ADAM_B2 * v_ref[...] + c2 * (g * g)
        m_hat = mn / bc1
        v_hat = vn / bc2
        g_ref[...] = g
        d_ref[...] = -ADAM_LR * (m_hat / (jnp.sqrt(v_hat) + ADAM_EPS) + ADAM_WD * w_ref[...])
        mo_ref[...] = mn
        vo_ref[...] = vn

    spec = pl.BlockSpec((PACK_TILE, C), lambda i: (i, 0))
    shp = jax.ShapeDtypeStruct((R, C), F32)
    return pl.pallas_call(
        body, name="adamw", grid=(R // PACK_TILE,),
        in_specs=[spec] * 5, out_specs=[spec] * 4, out_shape=[shp] * 4,
        compiler_params=_params(("parallel",)),
    )(pa, pb, w, m, v)


def _rms_fwd(x, g, name):
    R, Dm = x.shape
    tr = _pick(R, (512, 256, 128))

    def body(x_ref, g_ref, o_ref):
        xf = x_ref[...]
        r = lax.rsqrt(jnp.mean(xf * xf, axis=-1, keepdims=True) + EPS)
        o_ref[...] = (xf * r * g_ref[...]).astype(o_ref.dtype)

    return pl.pallas_call(
        body, name=name, grid=(R // tr,),
        in_specs=[pl.BlockSpec((tr, Dm), lambda i: (i, 0)), pl.BlockSpec((1, Dm), lambda i: (0, 0))],
        out_specs=pl.BlockSpec((tr, Dm), lambda i: (i, 0)),
        out_shape=jax.ShapeDtypeStruct((R, Dm), BF16),
        compiler_params=_params(("parallel",)),
    )(x, g)


def _rms_bwd(x, g, dy, dres, name):
    R, Dm = x.shape
    tr = _pick(R, (256, 128))
    has_res = dres is not None

    def body(*refs):
        if has_res:
            x_ref, g_ref, dy_ref, dres_ref, dx_ref, dg_ref = refs
        else:
            x_ref, g_ref, dy_ref, dx_ref, dg_ref = refs
        xf = x_ref[...]
        dy_ = dy_ref[...].astype(F32)
        r = lax.rsqrt(jnp.mean(xf * xf, axis=-1, keepdims=True) + EPS)
        gdy = dy_ * g_ref[...]
        mdot = jnp.mean(xf * gdy, axis=-1, keepdims=True)
        dx = r * gdy - xf * ((r * r * r) * mdot)
        if has_res:
            dx = dres_ref[...] + dx
        dx_ref[...] = dx

        @pl.when(pl.program_id(0) == 0)
        def _():
            dg_ref[...] = jnp.zeros_like(dg_ref)

        dg_ref[...] += jnp.sum(dy_ * (xf * r), axis=0, keepdims=True)

    row = pl.BlockSpec((tr, Dm), lambda i: (i, 0))
    vec = pl.BlockSpec((1, Dm), lambda i: (0, 0))
    ins = [x, g, dy] + ([dres] if has_res else [])
    return pl.pallas_call(
        body, name=name, grid=(R // tr,),
        in_specs=[row, vec, row] + ([row] if has_res else []),
        out_specs=[row, vec],
        out_shape=[jax.ShapeDtypeStruct((R, Dm), F32), jax.ShapeDtypeStruct((1, Dm), F32)],
        compiler_params=_params(("arbitrary",)),
    )(*ins)


def _final_loss(x, g, tgt):
    R, Dm = x.shape
    tr = _pick(R, (256, 128))

    def body(x_ref, g_ref, t_ref, l_ref, dx_ref, dg_ref):
        xf = x_ref[...]
        gv = g_ref[...]
        r = lax.rsqrt(jnp.mean(xf * xf, axis=-1, keepdims=True) + EPS)
        xr = xf * r
        err = xr * gv - t_ref[...]
        dy_ = err * (1.0 / Dm)
        gdy = dy_ * gv
        mdot = jnp.mean(xf * gdy, axis=-1, keepdims=True)
        dx_ref[...] = r * gdy - xf * ((r * r * r) * mdot)

        @pl.when(pl.program_id(0) == 0)
        def _():
            dg_ref[...] = jnp.zeros_like(dg_ref)
            l_ref[...] = jnp.zeros_like(l_ref)

        dg_ref[...] += jnp.sum(dy_ * xr, axis=0, keepdims=True)
        sq = jnp.sum(err * err, axis=1, keepdims=True)
        l_ref[...] += jnp.sum(sq, axis=0, keepdims=True) * (0.5 / Dm)

    row = pl.BlockSpec((tr, Dm), lambda i: (i, 0))
    vec = pl.BlockSpec((1, Dm), lambda i: (0, 0))
    return pl.pallas_call(
        body, name="final_norm_loss", grid=(R // tr,),
        in_specs=[row, vec, row],
        out_specs=[pl.BlockSpec((1, 1), lambda i: (0, 0)), row, vec],
        out_shape=[jax.ShapeDtypeStruct((1, 1), F32), jax.ShapeDtypeStruct((R, Dm), F32),
                   jax.ShapeDtypeStruct((1, Dm), F32)],
        compiler_params=_params(("arbitrary",)),
    )(x, g, tgt)


MAX_TK = 2048

_DIMS = {"nn": (((1,), (0,)), ((), ())), "nt": (((1,), (1,)), ((), ())), "tn": (((0,), (0,)), ((), ()))}


def _mm(a, b, *, mode="nn", out_dtype=BF16, epi=None, extra=None, name):
    if mode == "nn":
        (M, K), N = a.shape, b.shape[1]
    elif mode == "nt":
        (M, K), N = a.shape, b.shape[0]
    else:
        (K, M), N = a.shape, b.shape[1]
    tm = _pick(M, (1024, 768, 512, 256, 128))
    tn = _pick(N, (1024, 896, 768, 640, 512, 384, 256, 128))
    tk = K if K <= MAX_TK else _pick(K, (MAX_TK, 1024, 512, 256, 128))
    nk = K // tk

    def body(*refs):
        n_in = 3 if extra is not None else 2
        a_ref, b_ref = refs[:2]
        e_ref = refs[2] if extra is not None else None
        outs = refs[n_in:n_in + (2 if epi == "relu2" else 1)]
        k = pl.program_id(2)
        part = lax.dot_general(a_ref[...].astype(BF16), b_ref[...].astype(BF16), _DIMS[mode],
                               preferred_element_type=F32)

        def finish(acc):
            if epi is None:
                outs[0][...] = acc.astype(outs[0].dtype)
            elif epi == "add":
                outs[0][...] = (e_ref[...] + acc).astype(outs[0].dtype)
            elif epi == "relu2":
                outs[0][...] = acc.astype(BF16)
                rl = jnp.maximum(acc, 0.0)
                outs[1][...] = (rl * rl).astype(BF16)
            elif epi == "drelu2":
                u = e_ref[...].astype(F32)
                outs[0][...] = (acc * (2.0 * jnp.maximum(u, 0.0))).astype(outs[0].dtype)

        if nk == 1:
            finish(part)
        else:
            acc_ref = refs[-1]

            @pl.when(k == 0)
            def _():
                acc_ref[...] = part

            @pl.when(jnp.logical_and(k > 0, k < nk - 1))
            def _():
                acc_ref[...] += part

            @pl.when(k == nk - 1)
            def _():
                finish(acc_ref[...] + part)

    if mode == "tn":
        a_spec = pl.BlockSpec((tk, tm), lambda i, j, k: (k, i))
    else:
        a_spec = pl.BlockSpec((tm, tk), lambda i, j, k: (i, k))
    if mode == "nt":
        b_spec = pl.BlockSpec((tn, tk), lambda i, j, k: (j, k))
    else:
        b_spec = pl.BlockSpec((tk, tn), lambda i, j, k: (k, j))
    o_spec = pl.BlockSpec((tm, tn), lambda i, j, k: (i, j))
    ins, in_specs = [a, b], [a_spec, b_spec]
    if extra is not None:
        ins.append(extra)
        in_specs.append(o_spec)
    if epi == "relu2":
        out_shape = [jax.ShapeDtypeStruct((M, N), BF16)] * 2
        out_specs = [o_spec] * 2
    else:
        out_shape = [jax.ShapeDtypeStruct((M, N), out_dtype)]
        out_specs = [o_spec]
    res = pl.pallas_call(
        body, name=name, grid=(M // tm, N // tn, nk),
        in_specs=in_specs, out_specs=out_specs, out_shape=out_shape,
        scratch_shapes=[pltpu.VMEM((tm, tn), F32)] if nk > 1 else [],
        compiler_params=_params(("parallel", "parallel", "arbitrary")),
    )(*ins)
    return res if epi == "relu2" else res[0]


def _dot(a, b):
    return lax.dot_general(a, b, _DIMS["nn"], preferred_element_type=F32)


def _dot_nt(a, b):
    return lax.dot_general(a, b, _DIMS["nt"], preferred_element_type=F32)


def _dot_tn(a, b):
    return lax.dot_general(a, b, _DIMS["tn"], preferred_element_type=F32)


def _split_dot(x, t):
    hi = x.astype(BF16)
    lo = (x - hi.astype(F32)).astype(BF16)
    return _dot(jnp.concatenate([hi, lo], axis=1), jnp.concatenate([t, t], axis=0))


def _head_pair(ref, scale=None):
    xf = ref[...].astype(F32)
    if scale is not None:
        xf = xf * scale
    is_a = lax.broadcasted_iota(jnp.int32, xf.shape, 1) < HEAD_DIM
    return jnp.where(is_a, xf, 0.0).astype(BF16), jnp.where(is_a, 0.0, xf).astype(BF16)


def _stack(a, b):
    return jnp.concatenate([a, b], axis=0)


def _head_rows(ref, scale=None):
    return _stack(*_head_pair(ref, scale))


def _unstack_heads(x):
    rows = x.shape[0] // 2
    return _select_pair(x[:rows], x[rows:])


def _pair_rowsum(x):
    is_a = lax.broadcasted_iota(jnp.int32, x.shape, 1) < HEAD_DIM
    return (jnp.sum(jnp.where(is_a, x, 0.0), axis=1, keepdims=True),
            jnp.sum(jnp.where(is_a, 0.0, x), axis=1, keepdims=True))


def _select_pair(xa, xb):
    is_a = lax.broadcasted_iota(jnp.int32, xa.shape, 1) < HEAD_DIM
    return jnp.where(is_a, xa, xb)


def _two_cols(xa, xb):
    rows = xa.shape[0]
    first = lax.broadcasted_iota(jnp.int32, (rows, 2), 1) == 0
    return jnp.where(first, xa, xb)


def _softplus_parts(z):
    e = jnp.exp(-jnp.abs(z))
    return jnp.maximum(z, 0.0) + jnp.log(1.0 + e), e


def _tile_iotas():
    row = lax.broadcasted_iota(jnp.int32, (BK, BK), 0)
    col = lax.broadcasted_iota(jnp.int32, (BK, BK), 1)
    return row, col


def _stacked_iotas(nk):
    row = lax.broadcasted_iota(jnp.int32, (2 * BQ, nk), 0) & (BQ - 1)
    col = lax.broadcasted_iota(jnp.int32, (2 * BQ, nk), 1)
    return row, col


def _sb_fwd(proj, name):
    S = proj.shape[0]
    nqb = S // BQ

    def body(q_ref, k_ref, v_ref, o_ref, acc_ref):
        i = pl.program_id(1)
        q2 = _head_rows(q_ref, SCALE)
        row, col = _tile_iotas()
        tri = (row > col).astype(BF16)
        srow, scol = _stacked_iotas(BK)
        acc_ref[...] = jnp.zeros_like(acc_ref)

        def tile(kb, c, dmask):
            r0 = pl.multiple_of(kb * BK, BK)
            kblk = k_ref[pl.ds(r0, BK), :]
            vblk = v_ref[pl.ds(r0, BK), :]
            z = _dot_nt(q2, kblk)
            sp, _ = _softplus_parts(z)
            lm = -sp
            if dmask is not None:
                lm = jnp.where(dmask, lm, 0.0)
            btw = _split_dot(lm, tri)
            w = jnp.exp((z - sp) + btw + c)
            if dmask is not None:
                w = jnp.where(dmask, w, 0.0)
            acc_ref[...] += _dot(w.astype(BF16), vblk)
            return c + btw[:, 0:1] + lm[:, 0:1]

        def alive(c):
            return (jnp.max(c) > UNDERFLOW_BOUND).astype(jnp.int32)

        c0 = jnp.zeros((2 * BQ, 1), F32)
        for d in reversed(range(DIAG_TILES)):
            c0 = tile(i * DIAG_TILES + d, c0, scol < srow - d * BK)

        def cond(st):
            return jnp.logical_and(st[0] >= 0, st[1] > 0)

        def step(st):
            kb, _, c = st
            c = tile(kb, c, None)
            return kb - 1, alive(c), c

        lax.while_loop(cond, step, (i * DIAG_TILES - 1, alive(c0), c0))
        o_ref[...] = _unstack_heads(acc_ref[...])

    return pl.pallas_call(
        body, name=name, grid=(4, nqb),
        in_specs=[pl.BlockSpec((BQ, LANES), lambda p, i: (i, p)),
                  pl.BlockSpec((S, LANES), lambda p, i: (0, 4 + p)),
                  pl.BlockSpec((S, LANES), lambda p, i: (0, 8 + p))],
        out_specs=pl.BlockSpec((BQ, LANES), lambda p, i: (i, p)),
        out_shape=jax.ShapeDtypeStruct((S, MIX_WIDTH), F32),
        scratch_shapes=[pltpu.VMEM((2 * BQ, LANES), F32)],
        compiler_params=_params(("parallel", "arbitrary")),
    )(proj, proj, proj)


def _sb_bwd(proj, merged, dmerged, name):
    S = proj.shape[0]
    nqb = S // BQ

    def body(q_ref, k_ref, v_ref, o_ref, do_ref, dq_ref, dk_hbm, dv_hbm, dq_acc, dk_acc, dv_acc, sem):
        p = pl.program_id(0)
        i = pl.program_id(1)

        @pl.when(i == 0)
        def _():
            dk_acc[...] = jnp.zeros_like(dk_acc)
            dv_acc[...] = jnp.zeros_like(dv_acc)

        q2 = _head_rows(q_ref, SCALE)
        do2 = _head_rows(do_ref)
        tot = _stack(*_pair_rowsum(do_ref[...].astype(F32) * o_ref[...]))
        row, col = _tile_iotas()
        tri_gt = (row > col).astype(BF16)
        tri_ge = (row >= col).astype(BF16)
        srow, scol = _stacked_iotas(BK)
        dq_acc[...] = jnp.zeros_like(dq_acc)

        def tile(kb, st, dmask):
            masked = dmask is not None
            c, r = st
            r0 = pl.multiple_of(kb * BK, BK)
            kblk = k_ref[pl.ds(r0, BK), :]
            vblk = v_ref[pl.ds(r0, BK), :]
            z = _dot_nt(q2, kblk)
            sp, e = _softplus_parts(z)
            lm = -sp
            if masked:
                lm = jnp.where(dmask, lm, 0.0)
            btw = _split_dot(lm, tri_gt)
            w = jnp.exp((z - sp) + btw + c)
            if masked:
                w = jnp.where(dmask, w, 0.0)
            wb = w.astype(BF16)
            a = wb.astype(F32) * _dot_nt(do2, vblk)
            suffix = _split_dot(a, tri_ge) + r
            rcp = 1.0 / (1.0 + e)
            pos = z >= 0.0
            sig = jnp.where(pos, rcp, e * rcp)
            sig_neg = jnp.where(pos, e * rcp, rcp)
            dz = a * sig_neg - (tot - suffix) * sig
            if masked:
                dz = jnp.where(dmask, dz, 0.0)
            dzb = dz.astype(BF16)
            dq_acc[...] += _dot(dzb, kblk)
            dk_acc[pl.ds(r0, BK), :] += _dot_tn(dzb, q2)
            dv_acc[pl.ds(r0, BK), :] += _dot_tn(wb, do2)
            return c + btw[:, 0:1] + lm[:, 0:1], suffix[:, 0:1]

        def alive(st):
            return (jnp.max(st[0]) > UNDERFLOW_BOUND).astype(jnp.int32)

        zero = jnp.zeros((2 * BQ, 1), F32)
        st0 = (zero, zero)
        for d in reversed(range(DIAG_TILES)):
            st0 = tile(i * DIAG_TILES + d, st0, scol < srow - d * BK)

        def cond(s):
            return jnp.logical_and(s[0] >= 0, s[1] > 0)

        def step(s):
            kb, _, st = s
            st = tile(kb, st, None)
            return kb - 1, alive(st), st

        lax.while_loop(cond, step, (i * DIAG_TILES - 1, alive(st0), st0))
        dq_ref[...] = (_unstack_heads(dq_acc[...]) * SCALE).astype(dq_ref.dtype)

        @pl.when(i == nqb - 1)
        def _():
            ck = pltpu.make_async_copy(dk_acc, dk_hbm.at[p], sem.at[0])
            cv = pltpu.make_async_copy(dv_acc, dv_hbm.at[p], sem.at[1])
            ck.start()
            cv.start()
            ck.wait()
            cv.wait()

    blk = lambda off: pl.BlockSpec((BQ, LANES), lambda p, i: (i, off + p))
    slab = lambda off: pl.BlockSpec((S, LANES), lambda p, i: (0, off + p))
    return pl.pallas_call(
        body, name=name, grid=(4, nqb),
        in_specs=[blk(0), slab(4), slab(8), blk(0), blk(0)],
        out_specs=[blk(0), ANY, ANY],
        out_shape=[jax.ShapeDtypeStruct((S, MIX_WIDTH), BF16),
                   jax.ShapeDtypeStruct((4, S, LANES), F32), jax.ShapeDtypeStruct((4, S, LANES), F32)],
        scratch_shapes=[pltpu.VMEM((2 * BQ, LANES), F32), pltpu.VMEM((S, LANES), F32),
                        pltpu.VMEM((S, LANES), F32), pltpu.SemaphoreType.DMA((2,))],
        compiler_params=_params(("arbitrary", "arbitrary")),
    )(proj, proj, proj, merged, dmerged)


def _key_absmax(k_ref, kmax_ref, nkb):
    def step(kb, m):
        r0 = pl.multiple_of(kb * BK, BK)
        blk = jnp.abs(k_ref[pl.ds(r0, BK), :].astype(F32))
        return jnp.maximum(m, jnp.max(blk, axis=0, keepdims=True))

    kmax_ref[...] = lax.fori_loop(0, nkb, step, jnp.zeros((1, LANES), F32))


FK = 2 * BK
assert FK % BQ == 0
Q_PER_FK = FK // BQ


def _key_gates(cr_ref, kb):
    lo = cr_ref[0, 2 * kb]
    hi = cr_ref[0, 2 * kb + 1]
    per_head = [jnp.broadcast_to(jnp.concatenate([lo[h:h + 1], hi[h:h + 1]], axis=1), (BQ, FK)) for h in range(2)]
    return _stack(*per_head)


def _last_gate(cr_ref, kb):
    hi = cr_ref[0, 2 * jnp.maximum(kb, 0) + 1]
    return _stack(*[jnp.broadcast_to(hi[h:h + 1, BK - 1:BK], (BQ, 1)) for h in range(2)])


def _fox_fwd(proj, kv, c_col, c_row, name):
    S = proj.shape[0]
    nqb = S // BQ

    def body(q_ref, k_ref, v_ref, cc_ref, cr_ref, o_ref, lse_ref, acc_ref, kmax_ref):
        i = pl.program_id(1)

        @pl.when(i == 0)
        def _():
            _key_absmax(k_ref, kmax_ref, S // BK)

        q2 = _head_rows(q_ref, SCALE)
        qabs = jnp.abs(q_ref[...].astype(F32) * SCALE) * kmax_ref[...]
        bound = _stack(*_pair_rowsum(qabs))
        cc = cc_ref[0]
        ct = _stack(cc[:, 0:1], cc[:, 1:2])
        srow, scol = _stacked_iotas(FK)
        dmask = scol <= srow + (i % Q_PER_FK) * BQ
        acc_ref[...] = jnp.zeros_like(acc_ref)

        def tile(kb, st, masked):
            m, l = st
            r0 = pl.multiple_of(kb * FK, FK)
            kblk = k_ref[pl.ds(r0, FK), :]
            vblk = v_ref[pl.ds(r0, FK), :]
            z = _dot_nt(q2, kblk) + ct - _key_gates(cr_ref, kb)
            if masked:
                z = jnp.where(dmask, z, NEG_INF)
            m_new = jnp.maximum(m, jnp.max(z, axis=1, keepdims=True))
            alpha = jnp.exp(m - m_new)
            pr = jnp.exp(z - m_new)
            acc_ref[...] = alpha * acc_ref[...] + _split_dot(pr, vblk)
            return m_new, alpha * l + jnp.sum(pr, axis=1, keepdims=True)

        def alive(kb, st):
            reach = bound + ct - _last_gate(cr_ref, kb) - st[0]
            return (jnp.max(reach) > UNDERFLOW_BOUND).astype(jnp.int32)

        neg = jnp.full((2 * BQ, 1), NEG_INF, F32)
        zero = jnp.zeros((2 * BQ, 1), F32)
        st0 = tile(i // Q_PER_FK, (neg, zero), True)

        def cond(s):
            return jnp.logical_and(s[0] >= 0, s[1] > 0)

        def step(s):
            kb, _, st = s
            st = tile(kb, st, False)
            return kb - 1, alive(kb - 1, st), st

        _, _, (m, l) = lax.while_loop(cond, step, (i // Q_PER_FK - 1, alive(i // Q_PER_FK - 1, st0), st0))
        o_ref[...] = _unstack_heads(acc_ref[...] / l)
        lse = m + jnp.log(l)
        lse_ref[0] = _two_cols(lse[:BQ], lse[BQ:])

    return pl.pallas_call(
        body, name=name, grid=(4, nqb),
        in_specs=[pl.BlockSpec((BQ, LANES), lambda p, i: (i, p)),
                  pl.BlockSpec((S, LANES), lambda p, i: (0, p)),
                  pl.BlockSpec((S, LANES), lambda p, i: (0, 4 + p)),
                  pl.BlockSpec((1, BQ, 2), lambda p, i: (p, i, 0)),
                  pl.BlockSpec((1, S // BK, 8, LANES), lambda p, i: (p, 0, 0, 0))],
        out_specs=[pl.BlockSpec((BQ, LANES), lambda p, i: (i, p)),
                   pl.BlockSpec((1, BQ, 2), lambda p, i: (p, i, 0))],
        out_shape=[jax.ShapeDtypeStruct((S, MIX_WIDTH), F32), jax.ShapeDtypeStruct((4, S, 2), F32)],
        scratch_shapes=[pltpu.VMEM((2 * BQ, LANES), F32), pltpu.VMEM((1, LANES), F32)],
        compiler_params=_params(("arbitrary", "arbitrary")),
    )(proj, kv, kv, c_col, c_row)


def _fox_bwd(proj, kv, c_col, c_row, lse, merged, dmerged, dk_prev, dv_prev, dc_prev, name):
    S = proj.shape[0]
    nqb = S // BQ

    def body(q_ref, k_ref, v_ref, cc_ref, cr_ref, lse_ref, o_ref, do_ref, dkp_hbm, dvp_hbm, dcp_ref,
             dq_ref, dk_hbm, dv_hbm, dc_ref, dq_acc, dk_acc, dv_acc, kmax_ref, sem):
        p = pl.program_id(0)
        i = pl.program_id(1)

        @pl.when(i == 0)
        def _():
            ck = pltpu.make_async_copy(dkp_hbm.at[p], dk_acc, sem.at[0])
            cv = pltpu.make_async_copy(dvp_hbm.at[p], dv_acc, sem.at[1])
            ck.start()
            cv.start()
            dc_ref[...] = dcp_ref[...]
            _key_absmax(k_ref, kmax_ref, S // BK)
            ck.wait()
            cv.wait()

        q2 = _head_rows(q_ref, SCALE)
        do2 = _head_rows(do_ref)
        tot = _stack(*_pair_rowsum(do_ref[...].astype(F32) * o_ref[...]))
        qabs = jnp.abs(q_ref[...].astype(F32) * SCALE) * kmax_ref[...]
        bound = _stack(*_pair_rowsum(qabs))
        cc = cc_ref[0]
        ct = _stack(cc[:, 0:1], cc[:, 1:2])
        ls = lse_ref[0]
        lse = _stack(ls[:, 0:1], ls[:, 1:2])
        srow, scol = _stacked_iotas(FK)
        dmask = scol <= srow + (i % Q_PER_FK) * BQ
        sub = lax.broadcasted_iota(jnp.int32, (8, LANES), 0)
        dq_acc[...] = jnp.zeros_like(dq_acc)

        def tile(kb, masked):
            r0 = pl.multiple_of(kb * FK, FK)
            kblk = k_ref[pl.ds(r0, FK), :]
            vblk = v_ref[pl.ds(r0, FK), :]
            z = _dot_nt(q2, kblk) + ct - _key_gates(cr_ref, kb)
            if masked:
                z = jnp.where(dmask, z, NEG_INF)
            pr = jnp.exp(z - lse)
            ds = pr * (_dot_nt(do2, vblk) - tot)
            dsb = ds.astype(BF16)
            dq_acc[...] += _dot(dsb, kblk)
            dk_acc[pl.ds(r0, FK), :] += _dot_tn(dsb, q2)
            dv_acc[pl.ds(r0, FK), :] += _dot_tn(pr.astype(BF16), do2)
            dca = jnp.sum(ds[:BQ], axis=0, keepdims=True)
            dcb = jnp.sum(ds[BQ:], axis=0, keepdims=True)
            for j in range(2):
                cols = slice(j * BK, (j + 1) * BK)
                old = dc_ref[0, 2 * kb + j]
                dc_ref[0, 2 * kb + j] = jnp.where(sub == 0, old - dca[:, cols],
                                                  jnp.where(sub == 1, old - dcb[:, cols], old))

        def alive(kb):
            reach = bound + ct - _last_gate(cr_ref, kb) - lse
            return (jnp.max(reach) > UNDERFLOW_BOUND).astype(jnp.int32)

        tile(i // Q_PER_FK, True)

        def cond(s):
            return jnp.logical_and(s[0] >= 0, s[1] > 0)

        def step(s):
            kb, _ = s
            tile(kb, False)
            return kb - 1, alive(kb - 1)

        lax.while_loop(cond, step, (i // Q_PER_FK - 1, alive(i // Q_PER_FK - 1)))
        dq_ref[...] = (_unstack_heads(dq_acc[...]) * SCALE).astype(dq_ref.dtype)

        @pl.when(i == nqb - 1)
        def _():
            ck = pltpu.make_async_copy(dk_acc, dk_hbm.at[p], sem.at[0])
            cv = pltpu.make_async_copy(dv_acc, dv_hbm.at[p], sem.at[1])
            ck.start()
            cv.start()
            ck.wait()
            cv.wait()

    blk = lambda off: pl.BlockSpec((BQ, LANES), lambda p, i: (i, off + p))
    slab = lambda off: pl.BlockSpec((S, LANES), lambda p, i: (0, off + p))
    cols = pl.BlockSpec((1, BQ, 2), lambda p, i: (p, i, 0))
    rows = pl.BlockSpec((1, S // BK, 8, LANES), lambda p, i: (p, 0, 0, 0))
    return pl.pallas_call(
        body, name=name, grid=(4, nqb),
        in_specs=[blk(0), slab(0), slab(4), cols, rows, cols, blk(0), blk(0), ANY, ANY, rows],
        out_specs=[blk(0), ANY, ANY, rows],
        out_shape=[jax.ShapeDtypeStruct((S, MIX_WIDTH), BF16),
                   jax.ShapeDtypeStruct((4, S, LANES), F32), jax.ShapeDtypeStruct((4, S, LANES), F32),
                   jax.ShapeDtypeStruct((4, S // BK, 8, LANES), F32)],
        scratch_shapes=[pltpu.VMEM((2 * BQ, LANES), F32), pltpu.VMEM((S, LANES), F32),
                        pltpu.VMEM((S, LANES), F32), pltpu.VMEM((1, LANES), F32),
                        pltpu.SemaphoreType.DMA((2,))],
        compiler_params=_params(("arbitrary", "arbitrary")),
    )(proj, kv, kv, c_col, c_row, lse, merged, dmerged, dk_prev, dv_prev, dc_prev)


def _lane_scan(x, reverse):
    lane = lax.broadcasted_iota(jnp.int32, x.shape, 1)
    d = 1
    while d < LANES:
        if reverse:
            x = x + jnp.where(lane < LANES - d, pltpu.roll(x, LANES - d, 1), 0.0)
        else:
            x = x + jnp.where(lane >= d, pltpu.roll(x, d, 1), 0.0)
        d *= 2
    return x


def _gate_fwd(fl3, b8):
    nb = fl3.shape[0]

    def body(fl_ref, b_ref, c_ref):
        def step(kb, carry):
            x = fl_ref[kb] + b_ref[...]
            sp, _ = _softplus_parts(-x)
            c = _lane_scan(-sp, False) + carry
            c_ref[kb] = c
            return c[:, LANES - 1:LANES]

        lax.fori_loop(0, nb, step, jnp.zeros((8, 1), F32))

    return pl.pallas_call(body, name="forget_gate_cumsum",
                          out_shape=jax.ShapeDtypeStruct(fl3.shape, F32),
                          compiler_params=_params())(fl3, b8)


def _gate_bwd(dc3, fl3, b8):
    nb = fl3.shape[0]

    def body(dc_ref, fl_ref, b_ref, dfl_ref, db_ref):
        def step(t, st):
            carry, dbs = st
            kb = nb - 1 - t
            g = _lane_scan(dc_ref[kb], True) + carry
            x = fl_ref[kb] + b_ref[...]
            e = jnp.exp(-jnp.abs(x))
            rcp = 1.0 / (1.0 + e)
            dfl = g * jnp.where(x >= 0.0, e * rcp, rcp)
            dfl_ref[kb] = dfl
            return g[:, 0:1], dbs + dfl

        _, dbs = lax.fori_loop(0, nb, step, (jnp.zeros((8, 1), F32), jnp.zeros((8, LANES), F32)))
        db_ref[...] = jnp.broadcast_to(jnp.sum(dbs, axis=1, keepdims=True), (8, LANES))

    return pl.pallas_call(body, name="forget_gate_bwd",
                          out_shape=[jax.ShapeDtypeStruct(fl3.shape, F32), jax.ShapeDtypeStruct((8, LANES), F32)],
                          compiler_params=_params())(dc3, fl3, b8)


MEM_TQ = 256


def _mem_fwd(proj, qcol, mkv, name):
    S = proj.shape[0]
    M = mkv.shape[0]

    def body(q_ref, mk_ref, mv_ref, o_ref, lse_ref):
        q2 = _head_pair(q_ref, SCALE)
        outs, lses = [], []
        for h in range(2):
            s = _dot_nt(q2[h], mk_ref[...])
            m = jnp.max(s, axis=1, keepdims=True)
            pr = jnp.exp(s - m)
            l = jnp.sum(pr, axis=1, keepdims=True)
            outs.append(_dot(pr.astype(BF16), mv_ref[...]) / l)
            lses.append(m + jnp.log(l))
        o_ref[...] = _select_pair(outs[0], outs[1])
        lse_ref[0] = _two_cols(lses[0], lses[1])

    return pl.pallas_call(
        body, name=name, grid=(2, S // MEM_TQ),
        in_specs=[pl.BlockSpec((MEM_TQ, LANES), lambda p, i: (i, qcol + p)),
                  pl.BlockSpec((M, LANES), lambda p, i: (0, p)),
                  pl.BlockSpec((M, LANES), lambda p, i: (0, 2 + p))],
        out_specs=[pl.BlockSpec((MEM_TQ, LANES), lambda p, i: (i, p)),
                   pl.BlockSpec((1, MEM_TQ, 2), lambda p, i: (p, i, 0))],
        out_shape=[jax.ShapeDtypeStruct((S, MEM_WIDTH), F32), jax.ShapeDtypeStruct((2, S, 2), F32)],
        compiler_params=_params(("parallel", "parallel")),
    )(proj, mkv, mkv)


def _mem_bwd(proj, qcol, mkv, lse, merged, dmerged, name):
    S = proj.shape[0]
    M = mkv.shape[0]

    def body(q_ref, mk_ref, mv_ref, lse_ref, o_ref, do_ref, dq_ref, dmk_ref, dmv_ref):
        @pl.when(pl.program_id(1) == 0)
        def _():
            dmk_ref[...] = jnp.zeros_like(dmk_ref)
            dmv_ref[...] = jnp.zeros_like(dmv_ref)

        q2 = _head_pair(q_ref, SCALE)
        do2 = _head_pair(do_ref)
        tot = _pair_rowsum(do_ref[...].astype(F32) * o_ref[...])
        ls = lse_ref[0]
        dqs = []
        for h in range(2):
            s = _dot_nt(q2[h], mk_ref[...])
            pr = jnp.exp(s - ls[:, h:h + 1])
            ds = pr * (_dot_nt(do2[h], mv_ref[...]) - tot[h])
            dsb = ds.astype(BF16)
            dqs.append(_dot(dsb, mk_ref[...]))
            dmk_ref[...] += _dot_tn(dsb, q2[h])
            dmv_ref[...] += _dot_tn(pr.astype(BF16), do2[h])
        dq_ref[...] = (_select_pair(dqs[0], dqs[1]) * SCALE).astype(dq_ref.dtype)

    blk = lambda off: pl.BlockSpec((MEM_TQ, LANES), lambda p, i: (i, off + p))
    acc = pl.BlockSpec((M, LANES), lambda p, i: (0, p))
    return pl.pallas_call(
        body, name=name, grid=(2, S // MEM_TQ),
        in_specs=[blk(qcol), pl.BlockSpec((M, LANES), lambda p, i: (0, p)),
                  pl.BlockSpec((M, LANES), lambda p, i: (0, 2 + p)),
                  pl.BlockSpec((1, MEM_TQ, 2), lambda p, i: (p, i, 0)), blk(4), blk(4)],
        out_specs=[blk(0), acc, acc],
        out_shape=[jax.ShapeDtypeStruct((S, MEM_WIDTH), BF16), jax.ShapeDtypeStruct((M, MEM_WIDTH), F32),
                   jax.ShapeDtypeStruct((M, MEM_WIDTH), F32)],
        compiler_params=_params(("parallel", "arbitrary")),
    )(proj, mkv, mkv, lse, merged, dmerged)


def _c_layouts(c3):
    nb = c3.shape[0]
    pairs = c3.reshape(nb, 4, 2, LANES).transpose(1, 0, 2, 3)
    c_row = jnp.pad(pairs, ((0, 0), (0, 0), (0, 6), (0, 0)))
    c_col = pairs.transpose(0, 1, 3, 2).reshape(4, nb * LANES, 2)
    return c_col, c_row


def _local_step(x, mem, wb, sm, loss_target):
    S = x.shape[0]
    nb = S // BK
    vec = lambda a: a.reshape(1, D_MODEL)
    w_kvf = jnp.pad(wb["w_kv_shared"], ((0, 0), (0, 1152 - 1032)))
    b8 = jnp.broadcast_to(sm["b_f"].reshape(8, 1), (8, LANES))

    saved = []
    shared = None
    h = x
    for l in range(DEPTH):
        if l == N_A:
            hs = _rms_fwd(h, vec(sm["kv_norm_g"]), "kv_norm")
            kvf = _mm(hs, w_kvf, out_dtype=F32, name="kv_shared_proj")
            kv = kvf[:, :2 * MIX_WIDTH].astype(BF16)
            fl3 = kvf[:, 2 * MIX_WIDTH:2 * MIX_WIDTH + 8].T.reshape(8, nb, LANES).transpose(1, 0, 2)
            c3 = _gate_fwd(fl3, b8)
            c_col, c_row = _c_layouts(c3)
            shared = dict(h=h, hs=hs, kv=kv, fl3=fl3, c_col=c_col, c_row=c_row)
        hn = _rms_fwd(h, vec(sm["norm1_g"][l]), f"norm1_{l}")
        mn = _rms_fwd(mem, vec(sm["mem_norm_g"][l]), f"mem_norm_{l}")
        mkv = _mm(mn, wb["w_mem_kv"][l], name=f"mem_kv_proj_{l}")
        if l < N_A:
            w_in = wb["w_in_a"][l]
            proj = _mm(hn, w_in, name=f"in_proj_{l}")
            mix = _sb_fwd(proj, f"stickbreak_fwd_{l}")
            lse, qcol = None, 12
        else:
            w_in = wb["w_in_b"][l - N_A]
            proj = _mm(hn, w_in, name=f"in_proj_{l}")
            mix, lse = _fox_fwd(proj, shared["kv"], shared["c_col"], shared["c_row"], f"fox_fwd_{l}")
            qcol = 4
        mo, mlse = _mem_fwd(proj, qcol, mkv, f"mem_attn_fwd_{l}")
        merged = jnp.concatenate([mix, mo], axis=1)
        h_mid = _mm(merged, wb["w_o"][l], out_dtype=F32, epi="add", extra=h, name=f"out_proj_{l}")
        hn2 = _rms_fwd(h_mid, vec(sm["norm2_g"][l]), f"norm2_{l}")
        u, act = _mm(hn2, wb["w_mlp1"][l], epi="relu2", name=f"mlp1_{l}")
        h_out = _mm(act, wb["w_mlp2"][l], out_dtype=F32, epi="add", extra=h_mid, name=f"mlp2_{l}")
        saved.append(dict(h=h, hn=hn, mn=mn, mkv=mkv, proj=proj, lse=lse, mlse=mlse, qcol=qcol, merged=merged,
                          h_mid=h_mid, hn2=hn2, u=u, act=act, w_in=w_in))
        h = h_out

    loss, dh, dg_final = _final_loss(h, vec(sm["final_norm_g"]), loss_target)

    gb = {n: [None] * (DEPTH if n not in ("w_in_a", "w_in_b") else 2) for n in
          ("w_in_a", "w_in_b", "w_mem_kv", "w_o", "w_mlp1", "w_mlp2")}
    gs = {n: [None] * DEPTH for n in ("norm1_g", "mem_norm_g", "norm2_g")}
    dk_sh = jnp.zeros((4, S, LANES), F32)
    dv_sh = jnp.zeros((4, S, LANES), F32)
    dc_sh = jnp.zeros((4, nb, 8, LANES), F32)
    for l in reversed(range(DEPTH)):
        sv = saved[l]
        du = _mm(dh, wb["w_mlp2"][l], mode="nt", epi="drelu2", extra=sv["u"], name=f"mlp2_dx_{l}")
        gb["w_mlp2"][l] = _mm(sv["act"], dh, mode="tn", out_dtype=F32, name=f"mlp2_dw_{l}")
        gb["w_mlp1"][l] = _mm(sv["hn2"], du, mode="tn", out_dtype=F32, name=f"mlp1_dw_{l}")
        dhn2 = _mm(du, wb["w_mlp1"][l], mode="nt", out_dtype=F32, name=f"mlp1_dx_{l}")
        dh, gs["norm2_g"][l] = _rms_bwd(sv["h_mid"], vec(sm["norm2_g"][l]), dhn2, dh, f"norm2_bwd_{l}")
        dmerged = _mm(dh, wb["w_o"][l], mode="nt", name=f"out_proj_dx_{l}")
        gb["w_o"][l] = _mm(sv["merged"], dh, mode="tn", out_dtype=F32, name=f"out_proj_dw_{l}")
        if l < N_A:
            dq, dk, dv = _sb_bwd(sv["proj"], sv["merged"], dmerged, f"stickbreak_bwd_{l}")
        else:
            dq, dk_sh, dv_sh, dc_sh = _fox_bwd(sv["proj"], shared["kv"], shared["c_col"], shared["c_row"],
                                               sv["lse"], sv["merged"], dmerged, dk_sh, dv_sh, dc_sh,
                                               f"fox_bwd_{l}")
        dqm, dmk, dmv = _mem_bwd(sv["proj"], sv["qcol"], sv["mkv"], sv["mlse"], sv["merged"], dmerged,
                                 f"mem_attn_bwd_{l}")
        if l < N_A:
            flat = lambda t: t.transpose(1, 0, 2).reshape(S, MIX_WIDTH).astype(BF16)
            dproj = jnp.concatenate([dq, flat(dk), flat(dv), dqm], axis=1)
        else:
            dproj = jnp.concatenate([dq, dqm], axis=1)
        name_in = "w_in_a" if l < N_A else "w_in_b"
        gb[name_in][l if l < N_A else l - N_A] = _mm(sv["hn"], dproj, mode="tn", out_dtype=F32,
                                                      name=f"in_proj_dw_{l}")
        dhn = _mm(dproj, sv["w_in"], mode="nt", out_dtype=F32, name=f"in_proj_dx_{l}")
        dh, gs["norm1_g"][l] = _rms_bwd(sv["h"], vec(sm["norm1_g"][l]), dhn, dh, f"norm1_bwd_{l}")
        dmkv = jnp.concatenate([dmk, dmv], axis=1)
        gb["w_mem_kv"][l] = _mm(sv["mn"], dmkv, mode="tn", out_dtype=F32, name=f"mem_kv_dw_{l}")
        dmn = _mm(dmkv, wb["w_mem_kv"][l], mode="nt", out_dtype=F32, name=f"mem_kv_dx_{l}")
        _, gs["mem_norm_g"][l] = _rms_bwd(mem, vec(sm["mem_norm_g"][l]), dmn, None, f"mem_norm_bwd_{l}")
        if l == N_A:
            dfl3, db8 = _gate_bwd(dc_sh.reshape(4, nb, 8, LANES)[:, :, :2].transpose(1, 0, 2, 3).reshape(nb, 8, LANES),
                                  shared["fl3"], b8)
            dfl = dfl3.transpose(1, 0, 2).reshape(8, S).T
            flat = lambda t: t.transpose(1, 0, 2).reshape(S, MIX_WIDTH).astype(BF16)
            dkvf = jnp.concatenate([flat(dk_sh), flat(dv_sh),
                                    jnp.pad(dfl, ((0, 0), (0, LANES - 8))).astype(BF16)], axis=1)
            g_kvf = _mm(shared["hs"], dkvf, mode="tn", out_dtype=F32, name="kv_shared_dw")
            dhs = _mm(dkvf, w_kvf, mode="nt", out_dtype=F32, name="kv_shared_dx")
            dh, g_kvn = _rms_bwd(shared["h"], vec(sm["kv_norm_g"]), dhs, dh, "kv_norm_bwd")
            g_bf = db8[:, 0]

    gbig = {n: jnp.stack(v, axis=0) for n, v in gb.items()}
    gbig["w_kv_shared"] = g_kvf[:, :1032]
    gsmall = {n: jnp.concatenate(v, axis=0) for n, v in gs.items()}
    gsmall["kv_norm_g"] = g_kvn
    gsmall["final_norm_g"] = dg_final
    gsmall["b_f"] = g_bf
    return loss, dh, gbig, gsmall


def kernel(x, mem, norm1_g, w_in_a, w_in_b, w_mem_kv, mem_norm_g, w_o, norm2_g, w_mlp1, w_mlp2, kv_norm_g, w_kv_shared, b_f, final_norm_g, loss_target, m_norm1_g, m_w_in_a, m_w_in_b, m_w_mem_kv, m_mem_norm_g, m_w_o, m_norm2_g, m_w_mlp1, m_w_mlp2, m_kv_norm_g, m_w_kv_shared, m_b_f, m_final_norm_g, v_norm1_g, v_w_in_a, v_w_in_b, v_w_mem_kv, v_mem_norm_g, v_w_o, v_norm2_g, v_w_mlp1, v_w_mlp2, v_kv_norm_g, v_w_kv_shared, v_b_f, v_final_norm_g):
    big_w = dict(w_in_a=w_in_a, w_in_b=w_in_b, w_mem_kv=w_mem_kv, w_o=w_o, w_mlp1=w_mlp1, w_mlp2=w_mlp2,
                 w_kv_shared=w_kv_shared)
    small_w = dict(norm1_g=norm1_g, mem_norm_g=mem_norm_g, norm2_g=norm2_g, kv_norm_g=kv_norm_g,
                   final_norm_g=final_norm_g, b_f=b_f)
    big_m = dict(w_in_a=m_w_in_a, w_in_b=m_w_in_b, w_mem_kv=m_w_mem_kv, w_o=m_w_o, w_mlp1=m_w_mlp1,
                 w_mlp2=m_w_mlp2, w_kv_shared=m_w_kv_shared)
    small_m = dict(norm1_g=m_norm1_g, mem_norm_g=m_mem_norm_g, norm2_g=m_norm2_g, kv_norm_g=m_kv_norm_g,
                   final_norm_g=m_final_norm_g, b_f=m_b_f)
    big_v = dict(w_in_a=v_w_in_a, w_in_b=v_w_in_b, w_mem_kv=v_w_mem_kv, w_o=v_w_o, w_mlp1=v_w_mlp1,
                 w_mlp2=v_w_mlp2, w_kv_shared=v_w_kv_shared)
    small_v = dict(norm1_g=v_norm1_g, mem_norm_g=v_mem_norm_g, norm2_g=v_norm2_g, kv_norm_g=v_kv_norm_g,
                   final_norm_g=v_final_norm_g, b_f=v_b_f)

    gathered = _allgather_chips(_pack_local(big_w, small_w, BF16))
    wb = _unpack_gathered(gathered)

    loss, dx, gbig, gsmall = _local_step(x[0], mem[0], wb, small_w, loss_target[0])

    received = _scatter_chips(_pack_grads(gbig, gsmall))
    part = _sum4(received)
    other = _swap_cores(part)
    g, delta, new_m, new_v = _adamw(part, other, _pack_local(big_w, small_w, F32),
                                    _pack_local(big_m, small_m, F32), _pack_local(big_v, small_v, F32))

    outs = [lax.psum(loss[0, 0], ("x", "y", "c")), dx[None]]
    for packed in (g, delta, new_m, new_v):
        d = _unpack_local(packed)
        outs.extend(d[n] for n in WEIGHT_ORDER)
    return tuple(outs)
```

```python
import functools
import math

import jax
import jax.numpy as jnp
from jax import lax
from jax.experimental import pallas as pl
from jax.experimental.pallas import tpu as pltpu

F32 = jnp.float32
BF16 = jnp.bfloat16

D_MODEL = 1024
HEAD_DIM = 64
MIX_WIDTH = 512
MEM_WIDTH = 256
DEPTH = 4
N_A = 2
D_FF = 4096
EPS = 1e-6
NEG_INF = -1e30
SCALE = 1.0 / math.sqrt(HEAD_DIM)

ADAM_LR = 0.001
ADAM_B1 = 0.9
ADAM_B2 = 0.999
ADAM_EPS = 1e-08
ADAM_WD = 0.01
ADAM_STEP = 10

LANES = 128
BQ = 256
BK = 128
DIAG_TILES = BQ // BK
UNDERFLOW_BOUND = -110.0
VMEM_LIMIT = 56 * 1024 * 1024

MESH = pl.DeviceIdType.MESH
N_CHIPS = 4

BIG = (
    ("w_in_a", (2, 1024, 448), 2),
    ("w_in_b", (2, 256, 768), 1),
    ("w_mem_kv", (4, 256, 512), 1),
    ("w_o", (4, 768, 256), 2),
    ("w_mlp1", (4, 1024, 1024), 2),
    ("w_mlp2", (4, 1024, 1024), 1),
    ("w_kv_shared", (1024, 258), 1),
)
SMALL = (
    ("norm1_g", (4, 1024)),
    ("mem_norm_g", (4, 1024)),
    ("norm2_g", (4, 1024)),
    ("kv_norm_g", (1, 1024)),
    ("final_norm_g", (1, 1024)),
    ("b_f", (1, 1024)),
)
WEIGHT_ORDER = ("norm1_g", "w_in_a", "w_in_b", "w_mem_kv", "mem_norm_g", "w_o", "norm2_g", "w_mlp1",
                "w_mlp2", "kv_norm_g", "w_kv_shared", "b_f", "final_norm_g")


def _rows(shape):
    return math.prod(shape) // D_MODEL


ROW_ALIGN = 16


def _padded_rows(shape):
    return -(-_rows(shape) // ROW_ALIGN) * ROW_ALIGN


BIG_ROWS = sum(_padded_rows(s) for _, s, _ in BIG)
SMALL_ROWS = ROW_ALIGN
assert sum(_rows(s) for _, s in SMALL) <= SMALL_ROWS
PACK_TILE = 240
PACK_ROWS = BIG_ROWS + SMALL_ROWS
assert PACK_ROWS % PACK_TILE == 0


def _params(sem=None):
    return pltpu.CompilerParams(dimension_semantics=sem, vmem_limit_bytes=VMEM_LIMIT)


def _pick(n, cands):
    for c in cands:
        if n % c == 0:
            return c
    raise ValueError(f"no tile for {n}")


def _section(a, shape):
    a = a.reshape(-1, D_MODEL)
    return jnp.pad(a, ((0, _padded_rows(shape) - a.shape[0]), (0, 0)))


def _small_block(small, dtype):
    blk = jnp.zeros((SMALL_ROWS, D_MODEL), dtype)
    off = 0
    for n, shp in SMALL:
        a = small[n].astype(dtype)
        if n == "b_f":
            blk = blk.at[off, :a.size].set(a.reshape(-1))
        else:
            blk = blk.at[off:off + shp[0]].set(a.reshape(shp))
        off += shp[0]
    return blk


def _pack_local(big, small, dtype):
    parts = [_section(big[n].astype(dtype), s) for n, s, _ in BIG]
    return jnp.concatenate(parts + [_small_block(small, dtype)], axis=0)


def _unpack_local(p):
    out = {}
    off = 0
    for n, shp, _ in BIG:
        out[n] = p[off:off + _rows(shp)].reshape(shp)
        off += _padded_rows(shp)
    for n, shp in SMALL:
        a = p[off:off + shp[0]]
        if n == "b_f":
            out[n] = a[0, :8]
        elif shp[0] == 1:
            out[n] = a.reshape(D_MODEL)
        else:
            out[n] = a
        off += shp[0]
    return out


def _unpack_gathered(g):
    out = {}
    off = 0
    for n, shp, ax in BIG:
        sec = g[:, off:off + _rows(shp)].reshape((N_CHIPS,) + shp)
        out[n] = jnp.concatenate([sec[j] for j in range(N_CHIPS)], axis=ax)
        off += _padded_rows(shp)
    return out


def _pack_grads(gbig, gsmall):
    small = _small_block(gsmall, BF16)
    chunks = []
    for j in range(N_CHIPS):
        parts = []
        for n, shp, ax in BIG:
            w = shp[ax]
            parts.append(_section(lax.slice_in_dim(gbig[n], j * w, (j + 1) * w, axis=ax).astype(BF16), shp))
        chunks.append(jnp.concatenate(parts + [small], axis=0))
    return jnp.stack(chunks, axis=0)


ANY = pl.BlockSpec(memory_space=pl.ANY)


def _other_chips(x, y):
    return [(1 - x, y), (x, 1 - y), (1 - x, 1 - y)]


def _allgather_chips(w):
    half = w.shape[0] // 2

    def body(w_ref, o_ref, send_sems, recv_sems, pass_send, pass_recv, local_sem):
        x, y, c = lax.axis_index("x"), lax.axis_index("y"), lax.axis_index("c")
        me = 2 * x + y
        sibling = (x, y, 1 - c)
        mine = pl.ds(pl.multiple_of(c * half, ROW_ALIGN), half)
        other = pl.ds(pl.multiple_of((1 - c) * half, ROW_ALIGN), half)
        chips = _other_chips(x, y)

        def over_ici(j, chip, rows_of):
            return pltpu.make_async_remote_copy(src_ref=w_ref.at[mine], dst_ref=o_ref.at[rows_of, mine],
                                                send_sem=send_sems.at[j], recv_sem=recv_sems.at[j],
                                                device_id=(chip[0], chip[1], c), device_id_type=MESH)

        def over_d2d(j, rows_of, rows):
            return pltpu.make_async_remote_copy(src_ref=o_ref.at[rows_of, rows], dst_ref=o_ref.at[rows_of, rows],
                                                send_sem=pass_send.at[j], recv_sem=pass_recv.at[j],
                                                device_id=sibling, device_id_type=MESH)

        local = pltpu.make_async_copy(w_ref, o_ref.at[me], local_sem)
        local.start()
        first = [over_ici(j, chip, me) for j, chip in enumerate(chips)]
        for cp in first:
            cp.start()
        passed = []
        for j, chip in enumerate(chips):
            over_ici(j, chip, 2 * chip[0] + chip[1]).wait_recv()
            passed.append(over_d2d(j, 2 * chip[0] + chip[1], mine))
            passed[j].start()
        for j, chip in enumerate(chips):
            over_d2d(j, 2 * chip[0] + chip[1], other).wait_recv()
        for cp in first + passed:
            cp.wait_send()
        local.wait()

    return pl.pallas_call(
        body, name="allgather_weights",
        out_shape=jax.ShapeDtypeStruct((N_CHIPS,) + w.shape, w.dtype),
        in_specs=[ANY], out_specs=ANY,
        scratch_shapes=[pltpu.SemaphoreType.DMA((3,)), pltpu.SemaphoreType.DMA((3,)),
                        pltpu.SemaphoreType.DMA((3,)), pltpu.SemaphoreType.DMA((3,)), pltpu.SemaphoreType.DMA],
    )(w)


def _scatter_chips(g4):
    def body(g_ref, o_ref, send_sems, recv_sems, local_sem):
        x, y, c = lax.axis_index("x"), lax.axis_index("y"), lax.axis_index("c")
        me = 2 * x + y
        local = pltpu.make_async_copy(g_ref.at[me], o_ref.at[me], local_sem)
        local.start()
        copies = [
            pltpu.make_async_remote_copy(src_ref=g_ref.at[2 * px + py], dst_ref=o_ref.at[me],
                                         send_sem=send_sems.at[j], recv_sem=recv_sems.at[j],
                                         device_id=(px, py, c), device_id_type=MESH)
            for j, (px, py) in enumerate(_other_chips(x, y))
        ]
        for cp in copies:
            cp.start()
        for cp in copies:
            cp.wait()
        local.wait()

    return pl.pallas_call(
        body, name="scatter_grads",
        out_shape=jax.ShapeDtypeStruct(g4.shape, g4.dtype),
        in_specs=[ANY], out_specs=ANY,
        scratch_shapes=[pltpu.SemaphoreType.DMA((3,)), pltpu.SemaphoreType.DMA((3,)), pltpu.SemaphoreType.DMA],
    )(g4)


def _swap_cores(p):
    def body(p_ref, o_ref, send_sem, recv_sem):
        x, y, c = lax.axis_index("x"), lax.axis_index("y"), lax.axis_index("c")
        cp = pltpu.make_async_remote_copy(src_ref=p_ref, dst_ref=o_ref, send_sem=send_sem, recv_sem=recv_sem,
                                          device_id=(x, y, 1 - c), device_id_type=MESH)
        cp.start()
        cp.wait()

    return pl.pallas_call(
        body, name="swap_cores",
        out_shape=jax.ShapeDtypeStruct(p.shape, p.dtype),
        in_specs=[ANY], out_specs=ANY,
        scratch_shapes=[pltpu.SemaphoreType.DMA, pltpu.SemaphoreType.DMA],
    )(p)


def _sum4(r4):
    _, R, C = r4.shape

    def body(r_ref, o_ref):
        o_ref[...] = ((r_ref[0].astype(F32) + r_ref[1].astype(F32)) + r_ref[2].astype(F32)) + r_ref[3].astype(F32)

    return pl.pallas_call(
        body, name="sum_chips", grid=(R // PACK_TILE,),
        in_specs=[pl.BlockSpec((N_CHIPS, PACK_TILE, C), lambda i: (0, i, 0))],
        out_specs=pl.BlockSpec((PACK_TILE, C), lambda i: (i, 0)),
        out_shape=jax.ShapeDtypeStruct((R, C), F32),
        compiler_params=_params(("parallel",)),
    )(r4)


def _adamw(pa, pb, w, m, v):
    R, C = w.shape
    c1 = 1.0 - ADAM_B1
    c2 = 1.0 - ADAM_B2
    bc1 = 1.0 - ADAM_B1 ** ADAM_STEP
    bc2 = 1.0 - ADAM_B2 ** ADAM_STEP

    def body(pa_ref, pb_ref, w_ref, m_ref, v_ref, g_ref, d_ref, mo_ref, vo_ref):
        g = pa_ref[...] + pb_ref[...]
        mn = ADAM_B1 * m_ref[...] + c1 * g
        vn = ADAM_B2 * v_ref[...] + c2 * (g * g)
        m_hat = mn / bc1
        v_hat = vn / bc2
        g_ref[...] = g
        d_ref[...] = -ADAM_LR * (m_hat / (jnp.sqrt(v_hat) + ADAM_EPS) + ADAM_WD * w_ref[...])
        mo_ref[...] = mn
        vo_ref[...] = vn

    spec = pl.BlockSpec((PACK_TILE, C), lambda i: (i, 0))
    shp = jax.ShapeDtypeStruct((R, C), F32)
    return pl.pallas_call(
        body, name="adamw", grid=(R // PACK_TILE,),
        in_specs=[spec] * 5, out_specs=[spec] * 4, out_shape=[shp] * 4,
        compiler_params=_params(("parallel",)),
    )(pa, pb, w, m, v)


def _rms_fwd(x, g, name):
    R, Dm = x.shape
    tr = _pick(R, (512, 256, 128))

    def body(x_ref, g_ref, o_ref):
        xf = x_ref[...]
        r = lax.rsqrt(jnp.mean(xf * xf, axis=-1, keepdims=True) + EPS)
        o_ref[...] = (xf * r * g_ref[...]).astype(o_ref.dtype)

    return pl.pallas_call(
        body, name=name, grid=(R // tr,),
        in_specs=[pl.BlockSpec((tr, Dm), lambda i: (i, 0)), pl.BlockSpec((1, Dm), lambda i: (0, 0))],
        out_specs=pl.BlockSpec((tr, Dm), lambda i: (i, 0)),
        out_shape=jax.ShapeDtypeStruct((R, Dm), BF16),
        compiler_params=_params(("parallel",)),
    )(x, g)


def _rms_bwd(x, g, dy, dres, name):
    R, Dm = x.shape
    tr = _pick(R, (256, 128))
    has_res = dres is not None

    def body(*refs):
        if has_res:
            x_ref, g_ref, dy_ref, dres_ref, dx_ref, dg_ref = refs
        else:
            x_ref, g_ref, dy_ref, dx_ref, dg_ref = refs
        xf = x_ref[...]
        dy_ = dy_ref[...].astype(F32)
        r = lax.rsqrt(jnp.mean(xf * xf, axis=-1, keepdims=True) + EPS)
        gdy = dy_ * g_ref[...]
        mdot = jnp.mean(xf * gdy, axis=-1, keepdims=True)
        dx = r * gdy - xf * ((r * r * r) * mdot)
        if has_res:
            dx = dres_ref[...] + dx
        dx_ref[...] = dx

        @pl.when(pl.program_id(0) == 0)
        def _():
            dg_ref[...] = jnp.zeros_like(dg_ref)

        dg_ref[...] += jnp.sum(dy_ * (xf * r), axis=0, keepdims=True)

    row = pl.BlockSpec((tr, Dm), lambda i: (i, 0))
    vec = pl.BlockSpec((1, Dm), lambda i: (0, 0))
    ins = [x, g, dy] + ([dres] if has_res else [])
    return pl.pallas_call(
        body, name=name, grid=(R // tr,),
        in_specs=[row, vec, row] + ([row] if has_res else []),
        out_specs=[row, vec],
        out_shape=[jax.ShapeDtypeStruct((R, Dm), F32), jax.ShapeDtypeStruct((1, Dm), F32)],
        compiler_params=_params(("arbitrary",)),
    )(*ins)


def _final_loss(x, g, tgt):
    R, Dm = x.shape
    tr = _pick(R, (256, 128))

    def body(x_ref, g_ref, t_ref, l_ref, dx_ref, dg_ref):
        xf = x_ref[...]
        gv = g_ref[...]
        r = lax.rsqrt(jnp.mean(xf * xf, axis=-1, keepdims=True) + EPS)
        xr = xf * r
        err = xr * gv - t_ref[...]
        dy_ = err * (1.0 / Dm)
        gdy = dy_ * gv
        mdot = jnp.mean(xf * gdy, axis=-1, keepdims=True)
        dx_ref[...] = r * gdy - xf * ((r * r * r) * mdot)

        @pl.when(pl.program_id(0) == 0)
        def _():
            dg_ref[...] = jnp.zeros_like(dg_ref)
            l_ref[...] = jnp.zeros_like(l_ref)

        dg_ref[...] += jnp.sum(dy_ * xr, axis=0, keepdims=True)
        sq = jnp.sum(err * err, axis=1, keepdims=True)
        l_ref[...] += jnp.sum(sq, axis=0, keepdims=True) * (0.5 / Dm)

    row = pl.BlockSpec((tr, Dm), lambda i: (i, 0))
    vec = pl.BlockSpec((1, Dm), lambda i: (0, 0))
    return pl.pallas_call(
        body, name="final_norm_loss", grid=(R // tr,),
        in_specs=[row, vec, row],
        out_specs=[pl.BlockSpec((1, 1), lambda i: (0, 0)), row, vec],
        out_shape=[jax.ShapeDtypeStruct((1, 1), F32), jax.ShapeDtypeStruct((R, Dm), F32),
                   jax.ShapeDtypeStruct((1, Dm), F32)],
        compiler_params=_params(("arbitrary",)),
    )(x, g, tgt)


MAX_TK = 2048

_DIMS = {"nn": (((1,), (0,)), ((), ())), "nt": (((1,), (1,)), ((), ())), "tn": (((0,), (0,)), ((), ()))}


def _mm(a, b, *, mode="nn", out_dtype=BF16, epi=None, extra=None, name):
    if mode == "nn":
        (M, K), N = a.shape, b.shape[1]
    elif mode == "nt":
        (M, K), N = a.shape, b.shape[0]
    else:
        (K, M), N = a.shape, b.shape[1]
    tm = _pick(M, (1024, 768, 512, 256, 128))
    tn = _pick(N, (1024, 896, 768, 640, 512, 384, 256, 128))
    tk = K if K <= MAX_TK else _pick(K, (MAX_TK, 1024, 512, 256, 128))
    nk = K // tk

    def body(*refs):
        n_in = 3 if extra is not None else 2
        a_ref, b_ref = refs[:2]
        e_ref = refs[2] if extra is not None else None
        outs = refs[n_in:n_in + (2 if epi == "relu2" else 1)]
        k = pl.program_id(2)
        part = lax.dot_general(a_ref[...].astype(BF16), b_ref[...].astype(BF16), _DIMS[mode],
                               preferred_element_type=F32)

        def finish(acc):
            if epi is None:
                outs[0][...] = acc.astype(outs[0].dtype)
            elif epi == "add":
                outs[0][...] = (e_ref[...] + acc).astype(outs[0].dtype)
            elif epi == "relu2":
                outs[0][...] = acc.astype(BF16)
                rl = jnp.maximum(acc, 0.0)
                outs[1][...] = (rl * rl).astype(BF16)
            elif epi == "drelu2":
                u = e_ref[...].astype(F32)
                outs[0][...] = (acc * (2.0 * jnp.maximum(u, 0.0))).astype(outs[0].dtype)

        if nk == 1:
            finish(part)
        else:
            acc_ref = refs[-1]

            @pl.when(k == 0)
            def _():
                acc_ref[...] = part

            @pl.when(jnp.logical_and(k > 0, k < nk - 1))
            def _():
                acc_ref[...] += part

            @pl.when(k == nk - 1)
            def _():
                finish(acc_ref[...] + part)

    if mode == "tn":
        a_spec = pl.BlockSpec((tk, tm), lambda i, j, k: (k, i))
    else:
        a_spec = pl.BlockSpec((tm, tk), lambda i, j, k: (i, k))
    if mode == "nt":
        b_spec = pl.BlockSpec((tn, tk), lambda i, j, k: (j, k))
    else:
        b_spec = pl.BlockSpec((tk, tn), lambda i, j, k: (k, j))
    o_spec = pl.BlockSpec((tm, tn), lambda i, j, k: (i, j))
    ins, in_specs = [a, b], [a_spec, b_spec]
    if extra is not None:
        ins.append(extra)
        in_specs.append(o_spec)
    if epi == "relu2":
        out_shape = [jax.ShapeDtypeStruct((M, N), BF16)] * 2
        out_specs = [o_spec] * 2
    else:
        out_shape = [jax.ShapeDtypeStruct((M, N), out_dtype)]
        out_specs = [o_spec]
    res = pl.pallas_call(
        body, name=name, grid=(M // tm, N // tn, nk),
        in_specs=in_specs, out_specs=out_specs, out_shape=out_shape,
        scratch_shapes=[pltpu.VMEM((tm, tn), F32)] if nk > 1 else [],
        compiler_params=_params(("parallel", "parallel", "arbitrary")),
    )(*ins)
    return res if epi == "relu2" else res[0]


def _dot(a, b):
    return lax.dot_general(a, b, _DIMS["nn"], preferred_element_type=F32)


def _dot_nt(a, b):
    return lax.dot_general(a, b, _DIMS["nt"], preferred_element_type=F32)


def _dot_tn(a, b):
    return lax.dot_general(a, b, _DIMS["tn"], preferred_element_type=F32)


def _split_dot(x, t):
    hi = x.astype(BF16)
    lo = (x - hi.astype(F32)).astype(BF16)
    return _dot(jnp.concatenate([hi, lo], axis=1), jnp.concatenate([t, t], axis=0))


def _head_pair(ref, scale=None):
    xf = ref[...].astype(F32)
    if scale is not None:
        xf = xf * scale
    is_a = lax.broadcasted_iota(jnp.int32, xf.shape, 1) < HEAD_DIM
    return jnp.where(is_a, xf, 0.0).astype(BF16), jnp.where(is_a, 0.0, xf).astype(BF16)


def _stack(a, b):
    return jnp.concatenate([a, b], axis=0)


def _head_rows(ref, scale=None):
    return _stack(*_head_pair(ref, scale))


def _unstack_heads(x):
    rows = x.shape[0] // 2
    return _select_pair(x[:rows], x[rows:])


def _pair_rowsum(x):
    is_a = lax.broadcasted_iota(jnp.int32, x.shape, 1) < HEAD_DIM
    return (jnp.sum(jnp.where(is_a, x, 0.0), axis=1, keepdims=True),
            jnp.sum(jnp.where(is_a, 0.0, x), axis=1, keepdims=True))


def _select_pair(xa, xb):
    is_a = lax.broadcasted_iota(jnp.int32, xa.shape, 1) < HEAD_DIM
    return jnp.where(is_a, xa, xb)


def _two_cols(xa, xb):
    rows = xa.shape[0]
    first = lax.broadcasted_iota(jnp.int32, (rows, 2), 1) == 0
    return jnp.where(first, xa, xb)


def _softplus_parts(z):
    e = jnp.exp(-jnp.abs(z))
    return jnp.maximum(z, 0.0) + jnp.log(1.0 + e), e


def _tile_iotas():
    row = lax.broadcasted_iota(jnp.int32, (BK, BK), 0)
    col = lax.broadcasted_iota(jnp.int32, (BK, BK), 1)
    return row, col


def _stacked_iotas(bq, nk):
    row = lax.broadcasted_iota(jnp.int32, (2 * bq, nk), 0) & (bq - 1)
    col = lax.broadcasted_iota(jnp.int32, (2 * bq, nk), 1)
    return row, col


def _sb_fwd(proj, name):
    S = proj.shape[0]
    nqb = S // BQ

    def body(q_ref, k_ref, v_ref, o_ref, acc_ref):
        i = pl.program_id(1)
        q2 = _head_rows(q_ref, SCALE)
        row, col = _tile_iotas()
        tri = (row > col).astype(BF16)
        srow, scol = _stacked_iotas(BQ, BK)
        acc_ref[...] = jnp.zeros_like(acc_ref)

        def tile(kb, c, dmask):
            r0 = pl.multiple_of(kb * BK, BK)
            kblk = k_ref[pl.ds(r0, BK), :]
            vblk = v_ref[pl.ds(r0, BK), :]
            z = _dot_nt(q2, kblk)
            sp, _ = _softplus_parts(z)
            lm = -sp
            if dmask is not None:
                lm = jnp.where(dmask, lm, 0.0)
            btw = _split_dot(lm, tri)
            w = jnp.exp((z - sp) + btw + c)
            if dmask is not None:
                w = jnp.where(dmask, w, 0.0)
            acc_ref[...] += _dot(w.astype(BF16), vblk)
            return c + btw[:, 0:1] + lm[:, 0:1]

        def alive(c):
            return (jnp.max(c) > UNDERFLOW_BOUND).astype(jnp.int32)

        c0 = jnp.zeros((2 * BQ, 1), F32)
        for d in reversed(range(DIAG_TILES)):
            c0 = tile(i * DIAG_TILES + d, c0, scol < srow - d * BK)

        def cond(st):
            return jnp.logical_and(st[0] >= 0, st[1] > 0)

        def step(st):
            kb, _, c = st
            c = tile(kb, c, None)
            return kb - 1, alive(c), c

        lax.while_loop(cond, step, (i * DIAG_TILES - 1, alive(c0), c0))
        o_ref[...] = _unstack_heads(acc_ref[...])

    return pl.pallas_call(
        body, name=name, grid=(4, nqb),
        in_specs=[pl.BlockSpec((BQ, LANES), lambda p, i: (i, p)),
                  pl.BlockSpec((S, LANES), lambda p, i: (0, 4 + p)),
                  pl.BlockSpec((S, LANES), lambda p, i: (0, 8 + p))],
        out_specs=pl.BlockSpec((BQ, LANES), lambda p, i: (i, p)),
        out_shape=jax.ShapeDtypeStruct((S, MIX_WIDTH), F32),
        scratch_shapes=[pltpu.VMEM((2 * BQ, LANES), F32)],
        compiler_params=_params(("parallel", "arbitrary")),
    )(proj, proj, proj)


def _sb_bwd(proj, merged, dmerged, name):
    S = proj.shape[0]
    nqb = S // BQ

    def body(q_ref, k_ref, v_ref, o_ref, do_ref, dq_ref, dk_hbm, dv_hbm, dq_acc, dk_acc, dv_acc, sem):
        p = pl.program_id(0)
        i = pl.program_id(1)

        @pl.when(i == 0)
        def _():
            dk_acc[...] = jnp.zeros_like(dk_acc)
            dv_acc[...] = jnp.zeros_like(dv_acc)

        q2 = _head_rows(q_ref, SCALE)
        do2 = _head_rows(do_ref)
        tot = _stack(*_pair_rowsum(do_ref[...].astype(F32) * o_ref[...]))
        row, col = _tile_iotas()
        tri_gt = (row > col).astype(BF16)
        tri_ge = (row >= col).astype(BF16)
        srow, scol = _stacked_iotas(BQ, BK)
        dq_acc[...] = jnp.zeros_like(dq_acc)

        def tile(kb, st, dmask):
            masked = dmask is not None
            c, r = st
            r0 = pl.multiple_of(kb * BK, BK)
            kblk = k_ref[pl.ds(r0, BK), :]
            vblk = v_ref[pl.ds(r0, BK), :]
            z = _dot_nt(q2, kblk)
            sp, e = _softplus_parts(z)
            lm = -sp
            if masked:
                lm = jnp.where(dmask, lm, 0.0)
            btw = _split_dot(lm, tri_gt)
            w = jnp.exp((z - sp) + btw + c)
            if masked:
                w = jnp.where(dmask, w, 0.0)
            wb = w.astype(BF16)
            a = wb.astype(F32) * _dot_nt(do2, vblk)
            suffix = _split_dot(a, tri_ge) + r
            rcp = 1.0 / (1.0 + e)
            pos = z >= 0.0
            sig = jnp.where(pos, rcp, e * rcp)
            sig_neg = jnp.where(pos, e * rcp, rcp)
            dz = a * sig_neg - (tot - suffix) * sig
            if masked:
                dz = jnp.where(dmask, dz, 0.0)
            dzb = dz.astype(BF16)
            dq_acc[...] += _dot(dzb, kblk)
            dk_acc[pl.ds(r0, BK), :] += _dot_tn(dzb, q2)
            dv_acc[pl.ds(r0, BK), :] += _dot_tn(wb, do2)
            return c + btw[:, 0:1] + lm[:, 0:1], suffix[:, 0:1]

        def alive(st):
            return (jnp.max(st[0]) > UNDERFLOW_BOUND).astype(jnp.int32)

        zero = jnp.zeros((2 * BQ, 1), F32)
        st0 = (zero, zero)
        for d in reversed(range(DIAG_TILES)):
            st0 = tile(i * DIAG_TILES + d, st0, scol < srow - d * BK)

        def cond(s):
            return jnp.logical_and(s[0] >= 0, s[1] > 0)

        def step(s):
            kb, _, st = s
            st = tile(kb, st, None)
            return kb - 1, alive(st), st

        lax.while_loop(cond, step, (i * DIAG_TILES - 1, alive(st0), st0))
        dq_ref[...] = (_unstack_heads(dq_acc[...]) * SCALE).astype(dq_ref.dtype)

        @pl.when(i == nqb - 1)
        def _():
            ck = pltpu.make_async_copy(dk_acc, dk_hbm.at[p], sem.at[0])
            cv = pltpu.make_async_copy(dv_acc, dv_hbm.at[p], sem.at[1])
            ck.start()
            cv.start()
            ck.wait()
            cv.wait()

    blk = lambda off: pl.BlockSpec((BQ, LANES), lambda p, i: (i, off + p))
    slab = lambda off: pl.BlockSpec((S, LANES), lambda p, i: (0, off + p))
    return pl.pallas_call(
        body, name=name, grid=(4, nqb),
        in_specs=[blk(0), slab(4), slab(8), blk(0), blk(0)],
        out_specs=[blk(0), ANY, ANY],
        out_shape=[jax.ShapeDtypeStruct((S, MIX_WIDTH), BF16),
                   jax.ShapeDtypeStruct((4, S, LANES), F32), jax.ShapeDtypeStruct((4, S, LANES), F32)],
        scratch_shapes=[pltpu.VMEM((2 * BQ, LANES), F32), pltpu.VMEM((S, LANES), F32),
                        pltpu.VMEM((S, LANES), F32), pltpu.SemaphoreType.DMA((2,))],
        compiler_params=_params(("arbitrary", "arbitrary")),
    )(proj, proj, proj, merged, dmerged)


def _key_norm_max(k_ref, knorm_ref, nkb):
    def step(kb, m):
        r0 = pl.multiple_of(kb * BK, BK)
        blk = k_ref[pl.ds(r0, BK), :].astype(F32)
        sa, sb = _pair_rowsum(blk * blk)
        return (jnp.maximum(m[0], jnp.max(sa, axis=0, keepdims=True)),
                jnp.maximum(m[1], jnp.max(sb, axis=0, keepdims=True)))

    zero = jnp.zeros((1, 1), F32)
    ma, mb = lax.fori_loop(0, nkb, step, (zero, zero))
    knorm_ref[...] = _select_pair(jnp.broadcast_to(ma, (1, LANES)), jnp.broadcast_to(mb, (1, LANES)))


FQ = 512
FK = FQ
GATE_BLOCKS = FK // BK


def _key_gates(cr_ref, kb):
    blocks = [cr_ref[0, GATE_BLOCKS * kb + j] for j in range(GATE_BLOCKS)]
    per_head = [jnp.broadcast_to(jnp.concatenate([b[h:h + 1] for b in blocks], axis=1), (FQ, FK)) for h in range(2)]
    return _stack(*per_head)


def _last_gate(cr_ref, kb):
    last = cr_ref[0, GATE_BLOCKS * jnp.maximum(kb, 0) + GATE_BLOCKS - 1]
    return _stack(*[jnp.broadcast_to(last[h:h + 1, BK - 1:BK], (FQ, 1)) for h in range(2)])


def _logit_bound(q_ref, knorm_ref):
    qf = q_ref[...].astype(F32) * SCALE
    qa, qb = _pair_rowsum(qf * qf)
    kn = knorm_ref[...]
    return _stack(jnp.sqrt(qa * kn[:, 0:1]), jnp.sqrt(qb * kn[:, HEAD_DIM:HEAD_DIM + 1]))


def _causal_bias(bias_ref):
    srow, scol = _stacked_iotas(FQ, FK)
    bias_ref[...] = jnp.where(scol <= srow, 0.0, NEG_INF)


def _fox_fwd(proj, kv, c_col, c_row, name):
    S = proj.shape[0]
    nqb = S // FQ

    def body(q_ref, k_ref, v_ref, cc_ref, cr_ref, o_ref, lse_ref, acc_ref, knorm_ref, bias_ref):
        i = pl.program_id(1)

        @pl.when(i == 0)
        def _():
            _key_norm_max(k_ref, knorm_ref, S // BK)
            _causal_bias(bias_ref)

        q2 = _head_rows(q_ref, SCALE)
        bound = _logit_bound(q_ref, knorm_ref)
        cc = cc_ref[0]
        ct = _stack(cc[:, 0:1], cc[:, 1:2])
        acc_ref[...] = jnp.zeros_like(acc_ref)

        def tile(kb, st, masked):
            m, l = st
            r0 = pl.multiple_of(kb * FK, FK)
            kblk = k_ref[pl.ds(r0, FK), :]
            vblk = v_ref[pl.ds(r0, FK), :]
            z = _dot_nt(q2, kblk) + ct - _key_gates(cr_ref, kb)
            if masked:
                z = z + bias_ref[...]
            m_new = jnp.maximum(m, jnp.max(z, axis=1, keepdims=True))
            alpha = jnp.exp(m - m_new)
            pr = jnp.exp(z - m_new)
            acc_ref[...] = alpha * acc_ref[...] + _split_dot(pr, vblk)
            return m_new, alpha * l + jnp.sum(pr, axis=1, keepdims=True)

        def alive(kb, st):
            reach = bound + ct - _last_gate(cr_ref, kb) - st[0]
            return (jnp.max(reach) > UNDERFLOW_BOUND).astype(jnp.int32)

        neg = jnp.full((2 * FQ, 1), NEG_INF, F32)
        zero = jnp.zeros((2 * FQ, 1), F32)
        st0 = tile(i, (neg, zero), True)

        def cond(s):
            return jnp.logical_and(s[0] >= 0, s[1] > 0)

        def step(s):
            kb, _, st = s
            st = tile(kb, st, False)
            return kb - 1, alive(kb - 1, st), st

        _, _, (m, l) = lax.while_loop(cond, step, (i - 1, alive(i - 1, st0), st0))
        o_ref[...] = _unstack_heads(acc_ref[...] / l)
        lse = m + jnp.log(l)
        lse_ref[0] = _two_cols(lse[:FQ], lse[FQ:])

    return pl.pallas_call(
        body, name=name, grid=(4, nqb),
        in_specs=[pl.BlockSpec((FQ, LANES), lambda p, i: (i, p)),
                  pl.BlockSpec((S, LANES), lambda p, i: (0, p)),
                  pl.BlockSpec((S, LANES), lambda p, i: (0, 4 + p)),
                  pl.BlockSpec((1, FQ, 2), lambda p, i: (p, i, 0)),
                  pl.BlockSpec((1, S // BK, 8, LANES), lambda p, i: (p, 0, 0, 0))],
        out_specs=[pl.BlockSpec((FQ, LANES), lambda p, i: (i, p)),
                   pl.BlockSpec((1, FQ, 2), lambda p, i: (p, i, 0))],
        out_shape=[jax.ShapeDtypeStruct((S, MIX_WIDTH), F32), jax.ShapeDtypeStruct((4, S, 2), F32)],
        scratch_shapes=[pltpu.VMEM((2 * FQ, LANES), F32), pltpu.VMEM((1, LANES), F32),
                        pltpu.VMEM((2 * FQ, FK), F32)],
        compiler_params=_params(("arbitrary", "arbitrary")),
    )(proj, kv, kv, c_col, c_row)


def _fox_bwd(proj, kv, c_col, c_row, lse, merged, dmerged, dk_prev, dv_prev, dc_prev, name):
    S = proj.shape[0]
    nqb = S // FQ

    def body(q_ref, k_ref, v_ref, cc_ref, cr_ref, lse_ref, o_ref, do_ref, dkp_hbm, dvp_hbm, dcp_ref,
             dq_ref, dk_hbm, dv_hbm, dc_ref, dq_acc, dk_acc, dv_acc, knorm_ref, bias_ref, sem):
        p = pl.program_id(0)
        i = pl.program_id(1)

        @pl.when(i == 0)
        def _():
            ck = pltpu.make_async_copy(dkp_hbm.at[p], dk_acc, sem.at[0])
            cv = pltpu.make_async_copy(dvp_hbm.at[p], dv_acc, sem.at[1])
            ck.start()
            cv.start()
            dc_ref[...] = dcp_ref[...]
            _key_norm_max(k_ref, knorm_ref, S // BK)
            _causal_bias(bias_ref)
            ck.wait()
            cv.wait()

        q2 = _head_rows(q_ref, SCALE)
        do2 = _head_rows(do_ref)
        tot = _stack(*_pair_rowsum(do_ref[...].astype(F32) * o_ref[...]))
        bound = _logit_bound(q_ref, knorm_ref)
        cc = cc_ref[0]
        ct = _stack(cc[:, 0:1], cc[:, 1:2])
        ls = lse_ref[0]
        lse = _stack(ls[:, 0:1], ls[:, 1:2])
        sub = lax.broadcasted_iota(jnp.int32, (8, LANES), 0)
        dq_acc[...] = jnp.zeros_like(dq_acc)

        def tile(kb, masked):
            r0 = pl.multiple_of(kb * FK, FK)
            kblk = k_ref[pl.ds(r0, FK), :]
            vblk = v_ref[pl.ds(r0, FK), :]
            z = _dot_nt(q2, kblk) + ct - _key_gates(cr_ref, kb)
            if masked:
                z = z + bias_ref[...]
            pr = jnp.exp(z - lse)
            ds = pr * (_dot_nt(do2, vblk) - tot)
            dsb = ds.astype(BF16)
            dq_acc[...] += _dot(dsb, kblk)
            dk_acc[pl.ds(r0, FK), :] += _dot_tn(dsb, q2)
            dv_acc[pl.ds(r0, FK), :] += _dot_tn(pr.astype(BF16), do2)
            dca = jnp.sum(ds[:FQ], axis=0, keepdims=True)
            dcb = jnp.sum(ds[FQ:], axis=0, keepdims=True)
            for j in range(GATE_BLOCKS):
                cols = slice(j * BK, (j + 1) * BK)
                old = dc_ref[0, GATE_BLOCKS * kb + j]
                dc_ref[0, GATE_BLOCKS * kb + j] = jnp.where(sub == 0, old - dca[:, cols],
                                                            jnp.where(sub == 1, old - dcb[:, cols], old))

        def alive(kb):
            reach = bound + ct - _last_gate(cr_ref, kb) - lse
            return (jnp.max(reach) > UNDERFLOW_BOUND).astype(jnp.int32)

        tile(i, True)

        def cond(s):
            return jnp.logical_and(s[0] >= 0, s[1] > 0)

        def step(s):
            kb, _ = s
            tile(kb, False)
            return kb - 1, alive(kb - 1)

        lax.while_loop(cond, step, (i - 1, alive(i - 1)))
        dq_ref[...] = (_unstack_heads(dq_acc[...]) * SCALE).astype(dq_ref.dtype)

        @pl.when(i == nqb - 1)
        def _():
            ck = pltpu.make_async_copy(dk_acc, dk_hbm.at[p], sem.at[0])
            cv = pltpu.make_async_copy(dv_acc, dv_hbm.at[p], sem.at[1])
            ck.start()
            cv.start()
            ck.wait()
            cv.wait()

    blk = lambda off: pl.BlockSpec((FQ, LANES), lambda p, i: (i, off + p))
    slab = lambda off: pl.BlockSpec((S, LANES), lambda p, i: (0, off + p))
    cols = pl.BlockSpec((1, FQ, 2), lambda p, i: (p, i, 0))
    rows = pl.BlockSpec((1, S // BK, 8, LANES), lambda p, i: (p, 0, 0, 0))
    return pl.pallas_call(
        body, name=name, grid=(4, nqb),
        in_specs=[blk(0), slab(0), slab(4), cols, rows, cols, blk(0), blk(0), ANY, ANY, rows],
        out_specs=[blk(0), ANY, ANY, rows],
        out_shape=[jax.ShapeDtypeStruct((S, MIX_WIDTH), BF16),
                   jax.ShapeDtypeStruct((4, S, LANES), F32), jax.ShapeDtypeStruct((4, S, LANES), F32),
                   jax.ShapeDtypeStruct((4, S // BK, 8, LANES), F32)],
        scratch_shapes=[pltpu.VMEM((2 * FQ, LANES), F32), pltpu.VMEM((S, LANES), F32),
                        pltpu.VMEM((S, LANES), F32), pltpu.VMEM((1, LANES), F32),
                        pltpu.VMEM((2 * FQ, FK), F32), pltpu.SemaphoreType.DMA((2,))],
        compiler_params=_params(("arbitrary", "arbitrary")),
    )(proj, kv, kv, c_col, c_row, lse, merged, dmerged, dk_prev, dv_prev, dc_prev)


def _lane_scan(x, reverse):
    lane = lax.broadcasted_iota(jnp.int32, x.shape, 1)
    d = 1
    while d < LANES:
        if reverse:
            x = x + jnp.where(lane < LANES - d, pltpu.roll(x, LANES - d, 1), 0.0)
        else:
            x = x + jnp.where(lane >= d, pltpu.roll(x, d, 1), 0.0)
        d *= 2
    return x


def _gate_fwd(fl3, b8):
    nb = fl3.shape[0]

    def body(fl_ref, b_ref, c_ref):
        def step(kb, carry):
            x = fl_ref[kb] + b_ref[...]
            sp, _ = _softplus_parts(-x)
            c = _lane_scan(-sp, False) + carry
            c_ref[kb] = c
            return c[:, LANES - 1:LANES]

        lax.fori_loop(0, nb, step, jnp.zeros((8, 1), F32))

    return pl.pallas_call(body, name="forget_gate_cumsum",
                          out_shape=jax.ShapeDtypeStruct(fl3.shape, F32),
                          compiler_params=_params())(fl3, b8)


def _gate_bwd(dc3, fl3, b8):
    nb = fl3.shape[0]

    def body(dc_ref, fl_ref, b_ref, dfl_ref, db_ref):
        def step(t, st):
            carry, dbs = st
            kb = nb - 1 - t
            g = _lane_scan(dc_ref[kb], True) + carry
            x = fl_ref[kb] + b_ref[...]
            e = jnp.exp(-jnp.abs(x))
            rcp = 1.0 / (1.0 + e)
            dfl = g * jnp.where(x >= 0.0, e * rcp, rcp)
            dfl_ref[kb] = dfl
            return g[:, 0:1], dbs + dfl

        _, dbs = lax.fori_loop(0, nb, step, (jnp.zeros((8, 1), F32), jnp.zeros((8, LANES), F32)))
        db_ref[...] = jnp.broadcast_to(jnp.sum(dbs, axis=1, keepdims=True), (8, LANES))

    return pl.pallas_call(body, name="forget_gate_bwd",
                          out_shape=[jax.ShapeDtypeStruct(fl3.shape, F32), jax.ShapeDtypeStruct((8, LANES), F32)],
                          compiler_params=_params())(dc3, fl3, b8)


MEM_TQ = 512


def _mem_fwd(proj, qcol, mkv, name):
    S = proj.shape[0]
    M = mkv.shape[0]

    def body(q_ref, mk_ref, mv_ref, o_ref, lse_ref):
        q2 = _head_rows(q_ref, SCALE)
        s = _dot_nt(q2, mk_ref[...])
        m = jnp.max(s, axis=1, keepdims=True)
        pr = jnp.exp(s - m)
        l = jnp.sum(pr, axis=1, keepdims=True)
        o_ref[...] = _unstack_heads(_dot(pr.astype(BF16), mv_ref[...]) / l)
        lse = m + jnp.log(l)
        lse_ref[0] = _two_cols(lse[:MEM_TQ], lse[MEM_TQ:])

    return pl.pallas_call(
        body, name=name, grid=(2, S // MEM_TQ),
        in_specs=[pl.BlockSpec((MEM_TQ, LANES), lambda p, i: (i, qcol + p)),
                  pl.BlockSpec((M, LANES), lambda p, i: (0, p)),
                  pl.BlockSpec((M, LANES), lambda p, i: (0, 2 + p))],
        out_specs=[pl.BlockSpec((MEM_TQ, LANES), lambda p, i: (i, p)),
                   pl.BlockSpec((1, MEM_TQ, 2), lambda p, i: (p, i, 0))],
        out_shape=[jax.ShapeDtypeStruct((S, MEM_WIDTH), F32), jax.ShapeDtypeStruct((2, S, 2), F32)],
        compiler_params=_params(("parallel", "parallel")),
    )(proj, mkv, mkv)


def _mem_bwd(proj, qcol, mkv, lse, merged, dmerged, name):
    S = proj.shape[0]
    M = mkv.shape[0]

    def body(q_ref, mk_ref, mv_ref, lse_ref, o_ref, do_ref, dq_ref, dmk_ref, dmv_ref):
        @pl.when(pl.program_id(1) == 0)
        def _():
            dmk_ref[...] = jnp.zeros_like(dmk_ref)
            dmv_ref[...] = jnp.zeros_like(dmv_ref)

        q2 = _head_rows(q_ref, SCALE)
        do2 = _head_rows(do_ref)
        tot = _stack(*_pair_rowsum(do_ref[...].astype(F32) * o_ref[...]))
        ls = lse_ref[0]
        pr = jnp.exp(_dot_nt(q2, mk_ref[...]) - _stack(ls[:, 0:1], ls[:, 1:2]))
        ds = pr * (_dot_nt(do2, mv_ref[...]) - tot)
        dsb = ds.astype(BF16)
        dmk_ref[...] += _dot_tn(dsb, q2)
        dmv_ref[...] += _dot_tn(pr.astype(BF16), do2)
        dq_ref[...] = (_unstack_heads(_dot(dsb, mk_ref[...])) * SCALE).astype(dq_ref.dtype)

    blk = lambda off: pl.BlockSpec((MEM_TQ, LANES), lambda p, i: (i, off + p))
    acc = pl.BlockSpec((M, LANES), lambda p, i: (0, p))
    return pl.pallas_call(
        body, name=name, grid=(2, S // MEM_TQ),
        in_specs=[blk(qcol), pl.BlockSpec((M, LANES), lambda p, i: (0, p)),
                  pl.BlockSpec((M, LANES), lambda p, i: (0, 2 + p)),
                  pl.BlockSpec((1, MEM_TQ, 2), lambda p, i: (p, i, 0)), blk(4), blk(4)],
        out_specs=[blk(0), acc, acc],
        out_shape=[jax.ShapeDtypeStruct((S, MEM_WIDTH), BF16), jax.ShapeDtypeStruct((M, MEM_WIDTH), F32),
                   jax.ShapeDtypeStruct((M, MEM_WIDTH), F32)],
        compiler_params=_params(("parallel", "arbitrary")),
    )(proj, mkv, mkv, lse, merged, dmerged)


def _c_layouts(c3):
    nb = c3.shape[0]
    pairs = c3.reshape(nb, 4, 2, LANES).transpose(1, 0, 2, 3)
    c_row = jnp.pad(pairs, ((0, 0), (0, 0), (0, 6), (0, 0)))
    c_col = pairs.transpose(0, 1, 3, 2).reshape(4, nb * LANES, 2)
    return c_col, c_row


def _local_step(x, mem, wb, sm, loss_target):
    S = x.shape[0]
    nb = S // BK
    vec = lambda a: a.reshape(1, D_MODEL)
    w_kvf = jnp.pad(wb["w_kv_shared"], ((0, 0), (0, 1152 - 1032)))
    b8 = jnp.broadcast_to(sm["b_f"].reshape(8, 1), (8, LANES))

    saved = []
    shared = None
    h = x
    for l in range(DEPTH):
        if l == N_A:
            hs = _rms_fwd(h, vec(sm["kv_norm_g"]), "kv_norm")
            kvf = _mm(hs, w_kvf, out_dtype=F32, name="kv_shared_proj")
            kv = kvf[:, :2 * MIX_WIDTH].astype(BF16)
            fl3 = kvf[:, 2 * MIX_WIDTH:2 * MIX_WIDTH + 8].T.reshape(8, nb, LANES).transpose(1, 0, 2)
            c3 = _gate_fwd(fl3, b8)
            c_col, c_row = _c_layouts(c3)
            shared = dict(h=h, hs=hs, kv=kv, fl3=fl3, c_col=c_col, c_row=c_row)
        hn = _rms_fwd(h, vec(sm["norm1_g"][l]), f"norm1_{l}")
        mn = _rms_fwd(mem, vec(sm["mem_norm_g"][l]), f"mem_norm_{l}")
        mkv = _mm(mn, wb["w_mem_kv"][l], name=f"mem_kv_proj_{l}")
        if l < N_A:
            w_in = wb["w_in_a"][l]
            proj = _mm(hn, w_in, name=f"in_proj_{l}")
            mix = _sb_fwd(proj, f"stickbreak_fwd_{l}")
            lse, qcol = None, 12
        else:
            w_in = wb["w_in_b"][l - N_A]
            proj = _mm(hn, w_in, name=f"in_proj_{l}")
            mix, lse = _fox_fwd(proj, shared["kv"], shared["c_col"], shared["c_row"], f"fox_fwd_{l}")
            qcol = 4
        mo, mlse = _mem_fwd(proj, qcol, mkv, f"mem_attn_fwd_{l}")
        merged = jnp.concatenate([mix, mo], axis=1)
        h_mid = _mm(merged, wb["w_o"][l], out_dtype=F32, epi="add", extra=h, name=f"out_proj_{l}")
        hn2 = _rms_fwd(h_mid, vec(sm["norm2_g"][l]), f"norm2_{l}")
        u, act = _mm(hn2, wb["w_mlp1"][l], epi="relu2", name=f"mlp1_{l}")
        h_out = _mm(act, wb["w_mlp2"][l], out_dtype=F32, epi="add", extra=h_mid, name=f"mlp2_{l}")
        saved.append(dict(h=h, hn=hn, mn=mn, mkv=mkv, proj=proj, lse=lse, mlse=mlse, qcol=qcol, merged=merged,
                          h_mid=h_mid, hn2=hn2, u=u, act=act, w_in=w_in))
        h = h_out

    loss, dh, dg_final = _final_loss(h, vec(sm["final_norm_g"]), loss_target)

    gb = {n: [None] * (DEPTH if n not in ("w_in_a", "w_in_b") else 2) for n in
          ("w_in_a", "w_in_b", "w_mem_kv", "w_o", "w_mlp1", "w_mlp2")}
    gs = {n: [None] * DEPTH for n in ("norm1_g", "mem_norm_g", "norm2_g")}
    dk_sh = jnp.zeros((4, S, LANES), F32)
    dv_sh = jnp.zeros((4, S, LANES), F32)
    dc_sh = jnp.zeros((4, nb, 8, LANES), F32)
    for l in reversed(range(DEPTH)):
        sv = saved[l]
        du = _mm(dh, wb["w_mlp2"][l], mode="nt", epi="drelu2", extra=sv["u"], name=f"mlp2_dx_{l}")
        gb["w_mlp2"][l] = _mm(sv["act"], dh, mode="tn", out_dtype=F32, name=f"mlp2_dw_{l}")
        gb["w_mlp1"][l] = _mm(sv["hn2"], du, mode="tn", out_dtype=F32, name=f"mlp1_dw_{l}")
        dhn2 = _mm(du, wb["w_mlp1"][l], mode="nt", out_dtype=F32, name=f"mlp1_dx_{l}")
        dh, gs["norm2_g"][l] = _rms_bwd(sv["h_mid"], vec(sm["norm2_g"][l]), dhn2, dh, f"norm2_bwd_{l}")
        dmerged = _mm(dh, wb["w_o"][l], mode="nt", name=f"out_proj_dx_{l}")
        gb["w_o"][l] = _mm(sv["merged"], dh, mode="tn", out_dtype=F32, name=f"out_proj_dw_{l}")
        if l < N_A:
            dq, dk, dv = _sb_bwd(sv["proj"], sv["merged"], dmerged, f"stickbreak_bwd_{l}")
        else:
            dq, dk_sh, dv_sh, dc_sh = _fox_bwd(sv["proj"], shared["kv"], shared["c_col"], shared["c_row"],
                                               sv["lse"], sv["merged"], dmerged, dk_sh, dv_sh, dc_sh,
                                               f"fox_bwd_{l}")
        dqm, dmk, dmv = _mem_bwd(sv["proj"], sv["qcol"], sv["mkv"], sv["mlse"], sv["merged"], dmerged,
                                 f"mem_attn_bwd_{l}")
        if l < N_A:
            flat = lambda t: t.transpose(1, 0, 2).reshape(S, MIX_WIDTH).astype(BF16)
            dproj = jnp.concatenate([dq, flat(dk), flat(dv), dqm], axis=1)
        else:
            dproj = jnp.concatenate([dq, dqm], axis=1)
        name_in = "w_in_a" if l < N_A else "w_in_b"
        gb[name_in][l if l < N_A else l - N_A] = _mm(sv["hn"], dproj, mode="tn", out_dtype=F32,
                                                      name=f"in_proj_dw_{l}")
        dhn = _mm(dproj, sv["w_in"], mode="nt", out_dtype=F32, name=f"in_proj_dx_{l}")
        dh, gs["norm1_g"][l] = _rms_bwd(sv["h"], vec(sm["norm1_g"][l]), dhn, dh, f"norm1_bwd_{l}")
        dmkv = jnp.concatenate([dmk, dmv], axis=1)
        gb["w_mem_kv"][l] = _mm(sv["mn"], dmkv, mode="tn", out_dtype=F32, name=f"mem_kv_dw_{l}")
        dmn = _mm(dmkv, wb["w_mem_kv"][l], mode="nt", out_dtype=F32, name=f"mem_kv_dx_{l}")
        _, gs["mem_norm_g"][l] = _rms_bwd(mem, vec(sm["mem_norm_g"][l]), dmn, None, f"mem_norm_bwd_{l}")
        if l == N_A:
            dfl3, db8 = _gate_bwd(dc_sh.reshape(4, nb, 8, LANES)[:, :, :2].transpose(1, 0, 2, 3).reshape(nb, 8, LANES),
                                  shared["fl3"], b8)
            dfl = dfl3.transpose(1, 0, 2).reshape(8, S).T
            flat = lambda t: t.transpose(1, 0, 2).reshape(S, MIX_WIDTH).astype(BF16)
            dkvf = jnp.concatenate([flat(dk_sh), flat(dv_sh),
                                    jnp.pad(dfl, ((0, 0), (0, LANES - 8))).astype(BF16)], axis=1)
            g_kvf = _mm(shared["hs"], dkvf, mode="tn", out_dtype=F32, name="kv_shared_dw")
            dhs = _mm(dkvf, w_kvf, mode="nt", out_dtype=F32, name="kv_shared_dx")
            dh, g_kvn = _rms_bwd(shared["h"], vec(sm["kv_norm_g"]), dhs, dh, "kv_norm_bwd")
            g_bf = db8[:, 0]

    gbig = {n: jnp.stack(v, axis=0) for n, v in gb.items()}
    gbig["w_kv_shared"] = g_kvf[:, :1032]
    gsmall = {n: jnp.concatenate(v, axis=0) for n, v in gs.items()}
    gsmall["kv_norm_g"] = g_kvn
    gsmall["final_norm_g"] = dg_final
    gsmall["b_f"] = g_bf
    return loss, dh, gbig, gsmall


def kernel(x, mem, norm1_g, w_in_a, w_in_b, w_mem_kv, mem_norm_g, w_o, norm2_g, w_mlp1, w_mlp2, kv_norm_g, w_kv_shared, b_f, final_norm_g, loss_target, m_norm1_g, m_w_in_a, m_w_in_b, m_w_mem_kv, m_mem_norm_g, m_w_o, m_norm2_g, m_w_mlp1, m_w_mlp2, m_kv_norm_g, m_w_kv_shared, m_b_f, m_final_norm_g, v_norm1_g, v_w_in_a, v_w_in_b, v_w_mem_kv, v_mem_norm_g, v_w_o, v_norm2_g, v_w_mlp1, v_w_mlp2, v_kv_norm_g, v_w_kv_shared, v_b_f, v_final_norm_g):
    big_w = dict(w_in_a=w_in_a, w_in_b=w_in_b, w_mem_kv=w_mem_kv, w_o=w_o, w_mlp1=w_mlp1, w_mlp2=w_mlp2,
                 w_kv_shared=w_kv_shared)
    small_w = dict(norm1_g=norm1_g, mem_norm_g=mem_norm_g, norm2_g=norm2_g, kv_norm_g=kv_norm_g,
                   final_norm_g=final_norm_g, b_f=b_f)
    big_m = dict(w_in_a=m_w_in_a, w_in_b=m_w_in_b, w_mem_kv=m_w_mem_kv, w_o=m_w_o, w_mlp1=m_w_mlp1,
                 w_mlp2=m_w_mlp2, w_kv_shared=m_w_kv_shared)
    small_m = dict(norm1_g=m_norm1_g, mem_norm_g=m_mem_norm_g, norm2_g=m_norm2_g, kv_norm_g=m_kv_norm_g,
                   final_norm_g=m_final_norm_g, b_f=m_b_f)
    big_v = dict(w_in_a=v_w_in_a, w_in_b=v_w_in_b, w_mem_kv=v_w_mem_kv, w_o=v_w_o, w_mlp1=v_w_mlp1,
                 w_mlp2=v_w_mlp2, w_kv_shared=v_w_kv_shared)
    small_v = dict(norm1_g=v_norm1_g, mem_norm_g=v_mem_norm_g, norm2_g=v_norm2_g, kv_norm_g=v_kv_norm_g,
                   final_norm_g=v_final_norm_g, b_f=v_b_f)

    gathered = _allgather_chips(_pack_local(big_w, small_w, BF16))
    wb = _unpack_gathered(gathered)

    loss, dx, gbig, gsmall = _local_step(x[0], mem[0], wb, small_w, loss_target[0])

    received = _scatter_chips(_pack_grads(gbig, gsmall))
    part = _sum4(received)
    other = _swap_cores(part)
    g, delta, new_m, new_v = _adamw(part, other, _pack_local(big_w, small_w, F32),
                                    _pack_local(big_m, small_m, F32), _pack_local(big_v, small_v, F32))

    outs = [lax.psum(loss[0, 0], ("x", "y", "c")), dx[None]]
    for packed in (g, delta, new_m, new_v):
        d = _unpack_local(packed)
        outs.extend(d[n] for n in WEIGHT_ORDER)
    return tuple(outs)
```

```python
import functools
import math

import jax
import jax.numpy as jnp
from jax import lax
from jax.experimental import pallas as pl
from jax.experimental.pallas import tpu as pltpu

F32 = jnp.float32
BF16 = jnp.bfloat16

D_MODEL = 1024
HEAD_DIM = 64
MIX_WIDTH = 512
MEM_WIDTH = 256
DEPTH = 4
N_A = 2
D_FF = 4096
EPS = 1e-6
NEG_INF = -1e30
SCALE = 1.0 / math.sqrt(HEAD_DIM)

ADAM_LR = 0.001
ADAM_B1 = 0.9
ADAM_B2 = 0.999
ADAM_EPS = 1e-08
ADAM_WD = 0.01
ADAM_STEP = 10

LANES = 128
BQ = 256
BK = 128
DIAG_TILES = BQ // BK
UNDERFLOW_BOUND = -110.0
VMEM_LIMIT = 56 * 1024 * 1024

MESH = pl.DeviceIdType.MESH
N_CHIPS = 4

PART_LAYER0 = (
    ("w_in_a", 0, 1, (1024, 448), 1),
    ("w_mem_kv", 0, 1, (256, 512), 0),
    ("w_o", 0, 1, (768, 256), 1),
    ("w_mlp1", 0, 1, (1024, 1024), 1),
    ("w_mlp2", 0, 1, (1024, 1024), 0),
)
PART_REST = (
    ("w_in_a", 1, 2, (1024, 448), 1),
    ("w_in_b", 0, 2, (256, 768), 0),
    ("w_mem_kv", 1, 4, (256, 512), 0),
    ("w_o", 1, 4, (768, 256), 1),
    ("w_mlp1", 1, 4, (1024, 1024), 1),
    ("w_mlp2", 1, 4, (1024, 1024), 0),
    ("w_kv_shared", None, None, (1024, 258), 1),
)
BIG_NAMES = ("w_in_a", "w_in_b", "w_mem_kv", "w_o", "w_mlp1", "w_mlp2", "w_kv_shared")
SMALL = (
    ("norm1_g", (4, 1024)),
    ("mem_norm_g", (4, 1024)),
    ("norm2_g", (4, 1024)),
    ("kv_norm_g", (1, 1024)),
    ("final_norm_g", (1, 1024)),
    ("b_f", (1, 1024)),
)
WEIGHT_ORDER = ("norm1_g", "w_in_a", "w_in_b", "w_mem_kv", "mem_norm_g", "w_o", "norm2_g", "w_mlp1",
                "w_mlp2", "kv_norm_g", "w_kv_shared", "b_f", "final_norm_g")


ROW_ALIGN = 16
PACK_TILE = 256
SMALL_ROWS = ROW_ALIGN
assert sum(s[0] for _, s in SMALL) <= SMALL_ROWS


def _section_rows(entry):
    _, lo, hi, shape, _ = entry
    rows = (1 if lo is None else hi - lo) * math.prod(shape) // D_MODEL
    return rows, -(-rows // ROW_ALIGN) * ROW_ALIGN


def _part_rows(part, extra):
    rows = sum(_section_rows(e)[1] for e in part) + extra
    return -(-rows // PACK_TILE) * PACK_TILE


ROWS_LAYER0 = _part_rows(PART_LAYER0, SMALL_ROWS)
ROWS_REST = _part_rows(PART_REST, 0)


def _params(sem=None):
    return pltpu.CompilerParams(dimension_semantics=sem, vmem_limit_bytes=VMEM_LIMIT)


def _pick(n, cands):
    for c in cands:
        if n % c == 0:
            return c
    raise ValueError(f"no tile for {n}")


def _section(a, entry):
    a = a.reshape(-1, D_MODEL)
    return jnp.pad(a, ((0, _section_rows(entry)[1] - a.shape[0]), (0, 0)))


def _small_block(small, dtype):
    blk = jnp.zeros((SMALL_ROWS, D_MODEL), dtype)
    off = 0
    for n, shp in SMALL:
        a = small[n].astype(dtype)
        if n == "b_f":
            blk = blk.at[off, :a.size].set(a.reshape(-1))
        else:
            blk = blk.at[off:off + shp[0]].set(a.reshape(shp))
        off += shp[0]
    return blk


def _fill(parts, rows, dtype):
    used = sum(p.shape[0] for p in parts)
    return jnp.concatenate(parts + [jnp.zeros((rows - used, D_MODEL), dtype)], axis=0)


def _pack_local(part, rows, big, small, dtype):
    parts = [_section((big[e[0]] if e[1] is None else big[e[0]][e[1]:e[2]]).astype(dtype), e) for e in part]
    if small is not None:
        parts.append(_small_block(small, dtype))
    return _fill(parts, rows, dtype)


def _unpack_local(part, p, with_small, pieces, small):
    off = 0
    for e in part:
        n, lo, hi, shp, _ = e
        rows, reserved = _section_rows(e)
        pieces[n].append((lo, p[off:off + rows].reshape(shp if lo is None else (hi - lo,) + shp)))
        off += reserved
    if with_small:
        for n, shp in SMALL:
            a = p[off:off + shp[0]]
            small[n] = a[0, :8] if n == "b_f" else (a.reshape(D_MODEL) if shp[0] == 1 else a)
            off += shp[0]


def _join_layers(pieces):
    out = {}
    for n, ps in pieces.items():
        ps = sorted(ps, key=lambda t: -1 if t[0] is None else t[0])
        out[n] = ps[0][1] if len(ps) == 1 else jnp.concatenate([a for _, a in ps], axis=0)
    return out


def _unpack_gathered(part, g, weights):
    off = 0
    for e in part:
        n, lo, hi, shp, ax = e
        rows, reserved = _section_rows(e)
        if lo is None:
            sec = g[:, off:off + rows].reshape((N_CHIPS,) + shp)
            weights[n] = jnp.concatenate([sec[j] for j in range(N_CHIPS)], axis=ax)
        else:
            sec = g[:, off:off + rows].reshape((N_CHIPS, hi - lo) + shp)
            for l in range(lo, hi):
                weights[n][l] = jnp.concatenate([sec[j, l - lo] for j in range(N_CHIPS)], axis=ax)
        off += reserved


def _pack_grads(part, rows, gbig, gsmall):
    small = None if gsmall is None else _small_block(gsmall, BF16)
    chunks = []
    for j in range(N_CHIPS):
        parts = []
        for e in part:
            n, lo, hi, shp, ax = e
            w = shp[ax]
            layers = [gbig[n]] if lo is None else [gbig[n][l] for l in range(lo, hi)]
            cut = [lax.slice_in_dim(g, j * w, (j + 1) * w, axis=ax).astype(BF16).reshape(-1, D_MODEL) for g in layers]
            parts.append(_section(cut[0] if len(cut) == 1 else jnp.concatenate(cut, axis=0), e))
        if small is not None:
            parts.append(small)
        chunks.append(_fill(parts, rows, BF16))
    return jnp.stack(chunks, axis=0)


ANY = pl.BlockSpec(memory_space=pl.ANY)


def _other_chips(x, y):
    return [(1 - x, y), (x, 1 - y), (1 - x, 1 - y)]


class _AllGather:
    SCRATCH = [pltpu.SemaphoreType.DMA((3,)), pltpu.SemaphoreType.DMA((3,)), pltpu.SemaphoreType.DMA((3,)),
               pltpu.SemaphoreType.DMA((3,)), pltpu.SemaphoreType.DMA]

    def __init__(self, w_ref, o_ref, send_sems, recv_sems, pass_send, pass_recv, local_sem):
        self.w_ref, self.o_ref = w_ref, o_ref
        self.sems = (send_sems, recv_sems, pass_send, pass_recv, local_sem)
        x, y, c = lax.axis_index("x"), lax.axis_index("y"), lax.axis_index("c")
        half = w_ref.shape[0] // 2
        self.c, self.me, self.sibling = c, 2 * x + y, (x, y, 1 - c)
        self.mine = pl.ds(pl.multiple_of(c * half, ROW_ALIGN), half)
        self.other = pl.ds(pl.multiple_of((1 - c) * half, ROW_ALIGN), half)
        self.chips = _other_chips(x, y)

    def _over_ici(self, j, rows_of):
        chip = self.chips[j]
        return pltpu.make_async_remote_copy(
            src_ref=self.w_ref.at[self.mine], dst_ref=self.o_ref.at[rows_of, self.mine],
            send_sem=self.sems[0].at[j], recv_sem=self.sems[1].at[j],
            device_id=(chip[0], chip[1], self.c), device_id_type=MESH)

    def _over_d2d(self, j, rows):
        where = self.o_ref.at[2 * self.chips[j][0] + self.chips[j][1], rows]
        return pltpu.make_async_remote_copy(src_ref=where, dst_ref=where, send_sem=self.sems[2].at[j],
                                            recv_sem=self.sems[3].at[j], device_id=self.sibling,
                                            device_id_type=MESH)

    def _local(self):
        return pltpu.make_async_copy(self.w_ref, self.o_ref.at[self.me], self.sems[4])

    def start(self):
        self._local().start()
        for j in range(3):
            self._over_ici(j, self.me).start()

    def finish(self):
        for j in range(3):
            self._over_ici(j, 2 * self.chips[j][0] + self.chips[j][1]).wait_recv()
            self._over_d2d(j, self.mine).start()
        for j in range(3):
            self._over_d2d(j, self.other).wait_recv()
        for j in range(3):
            self._over_ici(j, self.me).wait_send()
            self._over_d2d(j, self.mine).wait_send()
        self._local().wait()


class _Scatter:
    SCRATCH = [pltpu.SemaphoreType.DMA((3,)), pltpu.SemaphoreType.DMA((3,)), pltpu.SemaphoreType.DMA]

    def __init__(self, g_ref, o_ref, send_sems, recv_sems, local_sem):
        self.g_ref, self.o_ref, self.sems = g_ref, o_ref, (send_sems, recv_sems, local_sem)
        x, y, c = lax.axis_index("x"), lax.axis_index("y"), lax.axis_index("c")
        self.c, self.me, self.chips = c, 2 * x + y, _other_chips(x, y)

    def _copy(self, j):
        chip = self.chips[j]
        return pltpu.make_async_remote_copy(
            src_ref=self.g_ref.at[2 * chip[0] + chip[1]], dst_ref=self.o_ref.at[self.me],
            send_sem=self.sems[0].at[j], recv_sem=self.sems[1].at[j],
            device_id=(chip[0], chip[1], self.c), device_id_type=MESH)

    def _local(self):
        return pltpu.make_async_copy(self.g_ref.at[self.me], self.o_ref.at[self.me], self.sems[2])

    def start(self):
        self._local().start()
        for j in range(3):
            self._copy(j).start()

    def finish(self):
        for j in range(3):
            self._copy(j).wait()
        self._local().wait()


def _allgather_chips(w):
    def body(w_ref, o_ref, *sems):
        ag = _AllGather(w_ref, o_ref, *sems)
        ag.start()
        ag.finish()

    return pl.pallas_call(
        body, name="allgather_weights",
        out_shape=jax.ShapeDtypeStruct((N_CHIPS,) + w.shape, w.dtype),
        in_specs=[ANY], out_specs=ANY, scratch_shapes=_AllGather.SCRATCH,
    )(w)


def _scatter_chips(g4):
    def body(g_ref, o_ref, *sems):
        sc = _Scatter(g_ref, o_ref, *sems)
        sc.start()
        sc.finish()

    return pl.pallas_call(
        body, name="scatter_grads",
        out_shape=jax.ShapeDtypeStruct(g4.shape, g4.dtype),
        in_specs=[ANY], out_specs=ANY, scratch_shapes=_Scatter.SCRATCH,
    )(g4)


def _swap_cores(p):
    def body(p_ref, o_ref, send_sem, recv_sem):
        x, y, c = lax.axis_index("x"), lax.axis_index("y"), lax.axis_index("c")
        cp = pltpu.make_async_remote_copy(src_ref=p_ref, dst_ref=o_ref, send_sem=send_sem, recv_sem=recv_sem,
                                          device_id=(x, y, 1 - c), device_id_type=MESH)
        cp.start()
        cp.wait()

    return pl.pallas_call(
        body, name="swap_cores",
        out_shape=jax.ShapeDtypeStruct(p.shape, p.dtype),
        in_specs=[ANY], out_specs=ANY,
        scratch_shapes=[pltpu.SemaphoreType.DMA, pltpu.SemaphoreType.DMA],
    )(p)


def _sum4(r4):
    _, R, C = r4.shape

    def body(r_ref, o_ref):
        o_ref[...] = ((r_ref[0].astype(F32) + r_ref[1].astype(F32)) + r_ref[2].astype(F32)) + r_ref[3].astype(F32)

    return pl.pallas_call(
        body, name="sum_chips", grid=(R // PACK_TILE,),
        in_specs=[pl.BlockSpec((N_CHIPS, PACK_TILE, C), lambda i: (0, i, 0))],
        out_specs=pl.BlockSpec((PACK_TILE, C), lambda i: (i, 0)),
        out_shape=jax.ShapeDtypeStruct((R, C), F32),
        compiler_params=_params(("parallel",)),
    )(r4)


def _adamw(pa, pb, w, m, v):
    R, C = w.shape
    c1 = 1.0 - ADAM_B1
    c2 = 1.0 - ADAM_B2
    bc1 = 1.0 - ADAM_B1 ** ADAM_STEP
    bc2 = 1.0 - ADAM_B2 ** ADAM_STEP

    def body(pa_ref, pb_ref, w_ref, m_ref, v_ref, g_ref, d_ref, mo_ref, vo_ref):
        g = pa_ref[...] + pb_ref[...]
        mn = ADAM_B1 * m_ref[...] + c1 * g
        vn = ADAM_B2 * v_ref[...] + c2 * (g * g)
        m_hat = mn / bc1
        v_hat = vn / bc2
        g_ref[...] = g
        d_ref[...] = -ADAM_LR * (m_hat / (jnp.sqrt(v_hat) + ADAM_EPS) + ADAM_WD * w_ref[...])
        mo_ref[...] = mn
        vo_ref[...] = vn

    spec = pl.BlockSpec((PACK_TILE, C), lambda i: (i, 0))
    shp = jax.ShapeDtypeStruct((R, C), F32)
    return pl.pallas_call(
        body, name="adamw", grid=(R // PACK_TILE,),
        in_specs=[spec] * 5, out_specs=[spec] * 4, out_shape=[shp] * 4,
        compiler_params=_params(("parallel",)),
    )(pa, pb, w, m, v)


def _rms_fwd(x, g, name):
    R, Dm = x.shape
    tr = _pick(R, (512, 256, 128))

    def body(x_ref, g_ref, o_ref):
        xf = x_ref[...]
        r = lax.rsqrt(jnp.mean(xf * xf, axis=-1, keepdims=True) + EPS)
        o_ref[...] = (xf * r * g_ref[...]).astype(o_ref.dtype)

    return pl.pallas_call(
        body, name=name, grid=(R // tr,),
        in_specs=[pl.BlockSpec((tr, Dm), lambda i: (i, 0)), pl.BlockSpec((1, Dm), lambda i: (0, 0))],
        out_specs=pl.BlockSpec((tr, Dm), lambda i: (i, 0)),
        out_shape=jax.ShapeDtypeStruct((R, Dm), BF16),
        compiler_params=_params(("parallel",)),
    )(x, g)


def _rms_bwd(x, g, dy, dres, name):
    R, Dm = x.shape
    tr = _pick(R, (256, 128))
    has_res = dres is not None

    def body(*refs):
        if has_res:
            x_ref, g_ref, dy_ref, dres_ref, dx_ref, dg_ref = refs
        else:
            x_ref, g_ref, dy_ref, dx_ref, dg_ref = refs
        xf = x_ref[...]
        dy_ = dy_ref[...].astype(F32)
        r = lax.rsqrt(jnp.mean(xf * xf, axis=-1, keepdims=True) + EPS)
        gdy = dy_ * g_ref[...]
        mdot = jnp.mean(xf * gdy, axis=-1, keepdims=True)
        dx = r * gdy - xf * ((r * r * r) * mdot)
        if has_res:
            dx = dres_ref[...] + dx
        dx_ref[...] = dx

        @pl.when(pl.program_id(0) == 0)
        def _():
            dg_ref[...] = jnp.zeros_like(dg_ref)

        dg_ref[...] += jnp.sum(dy_ * (xf * r), axis=0, keepdims=True)

    row = pl.BlockSpec((tr, Dm), lambda i: (i, 0))
    vec = pl.BlockSpec((1, Dm), lambda i: (0, 0))
    ins = [x, g, dy] + ([dres] if has_res else [])
    return pl.pallas_call(
        body, name=name, grid=(R // tr,),
        in_specs=[row, vec, row] + ([row] if has_res else []),
        out_specs=[row, vec],
        out_shape=[jax.ShapeDtypeStruct((R, Dm), F32), jax.ShapeDtypeStruct((1, Dm), F32)],
        compiler_params=_params(("arbitrary",)),
    )(*ins)


def _final_loss(x, g, tgt):
    R, Dm = x.shape
    tr = _pick(R, (256, 128))

    def body(x_ref, g_ref, t_ref, l_ref, dx_ref, dg_ref):
        xf = x_ref[...]
        gv = g_ref[...]
        r = lax.rsqrt(jnp.mean(xf * xf, axis=-1, keepdims=True) + EPS)
        xr = xf * r
        err = xr * gv - t_ref[...]
        dy_ = err * (1.0 / Dm)
        gdy = dy_ * gv
        mdot = jnp.mean(xf * gdy, axis=-1, keepdims=True)
        dx_ref[...] = r * gdy - xf * ((r * r * r) * mdot)

        @pl.when(pl.program_id(0) == 0)
        def _():
            dg_ref[...] = jnp.zeros_like(dg_ref)
            l_ref[...] = jnp.zeros_like(l_ref)

        dg_ref[...] += jnp.sum(dy_ * xr, axis=0, keepdims=True)
        sq = jnp.sum(err * err, axis=1, keepdims=True)
        l_ref[...] += jnp.sum(sq, axis=0, keepdims=True) * (0.5 / Dm)

    row = pl.BlockSpec((tr, Dm), lambda i: (i, 0))
    vec = pl.BlockSpec((1, Dm), lambda i: (0, 0))
    return pl.pallas_call(
        body, name="final_norm_loss", grid=(R // tr,),
        in_specs=[row, vec, row],
        out_specs=[pl.BlockSpec((1, 1), lambda i: (0, 0)), row, vec],
        out_shape=[jax.ShapeDtypeStruct((1, 1), F32), jax.ShapeDtypeStruct((R, Dm), F32),
                   jax.ShapeDtypeStruct((1, Dm), F32)],
        compiler_params=_params(("arbitrary",)),
    )(x, g, tgt)


MAX_TK = 2048

_DIMS = {"nn": (((1,), (0,)), ((), ())), "nt": (((1,), (1,)), ((), ())), "tn": (((0,), (0,)), ((), ()))}


def _mm(a, b, *, mode="nn", out_dtype=BF16, epi=None, extra=None, name):
    if mode == "nn":
        (M, K), N = a.shape, b.shape[1]
    elif mode == "nt":
        (M, K), N = a.shape, b.shape[0]
    else:
        (K, M), N = a.shape, b.shape[1]
    tm = _pick(M, (1024, 768, 512, 256, 128))
    tn = _pick(N, (1024, 896, 768, 640, 512, 384, 256, 128))
    tk = K if K <= MAX_TK else _pick(K, (MAX_TK, 1024, 512, 256, 128))
    nk = K // tk

    def body(*refs):
        n_in = 3 if extra is not None else 2
        a_ref, b_ref = refs[:2]
        e_ref = refs[2] if extra is not None else None
        outs = refs[n_in:n_in + (2 if epi == "relu2" else 1)]
        k = pl.program_id(2)
        part = lax.dot_general(a_ref[...].astype(BF16), b_ref[...].astype(BF16), _DIMS[mode],
                               preferred_element_type=F32)

        def finish(acc):
            if epi is None:
                outs[0][...] = acc.astype(outs[0].dtype)
            elif epi == "add":
                outs[0][...] = (e_ref[...] + acc).astype(outs[0].dtype)
            elif epi == "relu2":
                outs[0][...] = acc.astype(BF16)
                rl = jnp.maximum(acc, 0.0)
                outs[1][...] = (rl * rl).astype(BF16)
            elif epi == "drelu2":
                u = e_ref[...].astype(F32)
                outs[0][...] = (acc * (2.0 * jnp.maximum(u, 0.0))).astype(outs[0].dtype)

        if nk == 1:
            finish(part)
        else:
            acc_ref = refs[-1]

            @pl.when(k == 0)
            def _():
                acc_ref[...] = part

            @pl.when(jnp.logical_and(k > 0, k < nk - 1))
            def _():
                acc_ref[...] += part

            @pl.when(k == nk - 1)
            def _():
                finish(acc_ref[...] + part)

    if mode == "tn":
        a_spec = pl.BlockSpec((tk, tm), lambda i, j, k: (k, i))
    else:
        a_spec = pl.BlockSpec((tm, tk), lambda i, j, k: (i, k))
    if mode == "nt":
        b_spec = pl.BlockSpec((tn, tk), lambda i, j, k: (j, k))
    else:
        b_spec = pl.BlockSpec((tk, tn), lambda i, j, k: (k, j))
    o_spec = pl.BlockSpec((tm, tn), lambda i, j, k: (i, j))
    ins, in_specs = [a, b], [a_spec, b_spec]
    if extra is not None:
        ins.append(extra)
        in_specs.append(o_spec)
    if epi == "relu2":
        out_shape = [jax.ShapeDtypeStruct((M, N), BF16)] * 2
        out_specs = [o_spec] * 2
    else:
        out_shape = [jax.ShapeDtypeStruct((M, N), out_dtype)]
        out_specs = [o_spec]
    res = pl.pallas_call(
        body, name=name, grid=(M // tm, N // tn, nk),
        in_specs=in_specs, out_specs=out_specs, out_shape=out_shape,
        scratch_shapes=[pltpu.VMEM((tm, tn), F32)] if nk > 1 else [],
        compiler_params=_params(("parallel", "parallel", "arbitrary")),
    )(*ins)
    return res if epi == "relu2" else res[0]


def _dot(a, b):
    return lax.dot_general(a, b, _DIMS["nn"], preferred_element_type=F32)


def _dot_nt(a, b):
    return lax.dot_general(a, b, _DIMS["nt"], preferred_element_type=F32)


def _dot_tn(a, b):
    return lax.dot_general(a, b, _DIMS["tn"], preferred_element_type=F32)


def _split_dot(x, t):
    hi = x.astype(BF16)
    lo = (x - hi.astype(F32)).astype(BF16)
    return _dot(jnp.concatenate([hi, lo], axis=1), jnp.concatenate([t, t], axis=0))


def _head_pair(ref, scale=None):
    xf = ref[...].astype(F32)
    if scale is not None:
        xf = xf * scale
    is_a = lax.broadcasted_iota(jnp.int32, xf.shape, 1) < HEAD_DIM
    return jnp.where(is_a, xf, 0.0).astype(BF16), jnp.where(is_a, 0.0, xf).astype(BF16)


def _stack(a, b):
    return jnp.concatenate([a, b], axis=0)


def _head_rows(ref, scale=None):
    return _stack(*_head_pair(ref, scale))


def _unstack_heads(x):
    rows = x.shape[0] // 2
    return _select_pair(x[:rows], x[rows:])


def _pair_rowsum(x):
    is_a = lax.broadcasted_iota(jnp.int32, x.shape, 1) < HEAD_DIM
    return (jnp.sum(jnp.where(is_a, x, 0.0), axis=1, keepdims=True),
            jnp.sum(jnp.where(is_a, 0.0, x), axis=1, keepdims=True))


def _select_pair(xa, xb):
    is_a = lax.broadcasted_iota(jnp.int32, xa.shape, 1) < HEAD_DIM
    return jnp.where(is_a, xa, xb)


def _two_cols(xa, xb):
    rows = xa.shape[0]
    first = lax.broadcasted_iota(jnp.int32, (rows, 2), 1) == 0
    return jnp.where(first, xa, xb)


def _softplus_parts(z):
    e = jnp.exp(-jnp.abs(z))
    return jnp.maximum(z, 0.0) + jnp.log(1.0 + e), e


def _tile_iotas():
    row = lax.broadcasted_iota(jnp.int32, (BK, BK), 0)
    col = lax.broadcasted_iota(jnp.int32, (BK, BK), 1)
    return row, col


def _stacked_iotas(bq, nk):
    row = lax.broadcasted_iota(jnp.int32, (2 * bq, nk), 0) & (bq - 1)
    col = lax.broadcasted_iota(jnp.int32, (2 * bq, nk), 1)
    return row, col


def _side_exchange(exchange, operand, n_in, n_out):
    if exchange is None:
        return [], [], [], [], lambda refs: (lambda: None, lambda: None)
    out_shape = jax.ShapeDtypeStruct(((N_CHIPS,) + operand.shape) if exchange is _AllGather else operand.shape,
                                     operand.dtype)
    n_sem = len(exchange.SCRATCH)

    def pick(refs):
        def make():
            return exchange(refs[n_in], refs[n_in + 1 + n_out], *refs[len(refs) - n_sem:])

        return (lambda: make().start()), (lambda: make().finish())

    return [operand], [ANY], [out_shape], [ANY], pick


def _sb_fwd(proj, name, exchange=None, operand=None):
    S = proj.shape[0]
    nqb = S // BQ
    x_in, x_in_specs, x_out, x_out_specs, pick = _side_exchange(exchange, operand, 3, 1)

    def body(*refs):
        q_ref, k_ref, v_ref = refs[:3]
        o_ref = refs[3 + len(x_in)]
        acc_ref = refs[3 + len(x_in) + 1 + len(x_out)]
        start, finish = pick(refs)
        p = pl.program_id(0)
        i = pl.program_id(1)

        @pl.when(jnp.logical_and(p == 0, i == 0))
        def _():
            start()

        q2 = _head_rows(q_ref, SCALE)
        row, col = _tile_iotas()
        tri = (row > col).astype(BF16)
        srow, scol = _stacked_iotas(BQ, BK)
        acc_ref[...] = jnp.zeros_like(acc_ref)

        def tile(kb, c, dmask):
            r0 = pl.multiple_of(kb * BK, BK)
            kblk = k_ref[pl.ds(r0, BK), :]
            vblk = v_ref[pl.ds(r0, BK), :]
            z = _dot_nt(q2, kblk)
            sp, _ = _softplus_parts(z)
            lm = -sp
            if dmask is not None:
                lm = jnp.where(dmask, lm, 0.0)
            btw = _split_dot(lm, tri)
            w = jnp.exp((z - sp) + btw + c)
            if dmask is not None:
                w = jnp.where(dmask, w, 0.0)
            acc_ref[...] += _dot(w.astype(BF16), vblk)
            return c + btw[:, 0:1] + lm[:, 0:1]

        def alive(c):
            return (jnp.max(c) > UNDERFLOW_BOUND).astype(jnp.int32)

        c0 = jnp.zeros((2 * BQ, 1), F32)
        for d in reversed(range(DIAG_TILES)):
            c0 = tile(i * DIAG_TILES + d, c0, scol < srow - d * BK)

        def cond(st):
            return jnp.logical_and(st[0] >= 0, st[1] > 0)

        def step(st):
            kb, _, c = st
            c = tile(kb, c, None)
            return kb - 1, alive(c), c

        lax.while_loop(cond, step, (i * DIAG_TILES - 1, alive(c0), c0))
        o_ref[...] = _unstack_heads(acc_ref[...])

        @pl.when(jnp.logical_and(p == 3, i == nqb - 1))
        def _():
            finish()

    res = pl.pallas_call(
        body, name=name, grid=(4, nqb),
        in_specs=[pl.BlockSpec((BQ, LANES), lambda p, i: (i, p)),
                  pl.BlockSpec((S, LANES), lambda p, i: (0, 4 + p)),
                  pl.BlockSpec((S, LANES), lambda p, i: (0, 8 + p))] + x_in_specs,
        out_specs=[pl.BlockSpec((BQ, LANES), lambda p, i: (i, p))] + x_out_specs,
        out_shape=[jax.ShapeDtypeStruct((S, MIX_WIDTH), F32)] + x_out,
        scratch_shapes=[pltpu.VMEM((2 * BQ, LANES), F32)] + ([] if exchange is None else exchange.SCRATCH),
        compiler_params=_params(("arbitrary", "arbitrary")),
    )(proj, proj, proj, *x_in)
    return res if exchange is not None else res[0]


def _sb_bwd(proj, merged, dmerged, name, exchange=None, operand=None):
    S = proj.shape[0]
    nqb = S // BQ
    x_in, x_in_specs, x_out, x_out_specs, pick = _side_exchange(exchange, operand, 5, 3)

    def body(*refs):
        q_ref, k_ref, v_ref, o_ref, do_ref = refs[:5]
        dq_ref, dk_hbm, dv_hbm = refs[5 + len(x_in):8 + len(x_in)]
        dq_acc, dk_acc, dv_acc, sem = refs[8 + len(x_in) + len(x_out):12 + len(x_in) + len(x_out)]
        start, finish = pick(refs)
        p = pl.program_id(0)
        i = pl.program_id(1)

        @pl.when(jnp.logical_and(p == 0, i == 0))
        def _():
            start()

        @pl.when(i == 0)
        def _():
            dk_acc[...] = jnp.zeros_like(dk_acc)
            dv_acc[...] = jnp.zeros_like(dv_acc)

        q2 = _head_rows(q_ref, SCALE)
        do2 = _head_rows(do_ref)
        tot = _stack(*_pair_rowsum(do_ref[...].astype(F32) * o_ref[...]))
        row, col = _tile_iotas()
        tri_gt = (row > col).astype(BF16)
        tri_ge = (row >= col).astype(BF16)
        srow, scol = _stacked_iotas(BQ, BK)
        dq_acc[...] = jnp.zeros_like(dq_acc)

        def tile(kb, st, dmask):
            masked = dmask is not None
            c, r = st
            r0 = pl.multiple_of(kb * BK, BK)
            kblk = k_ref[pl.ds(r0, BK), :]
            vblk = v_ref[pl.ds(r0, BK), :]
            z = _dot_nt(q2, kblk)
            sp, e = _softplus_parts(z)
            lm = -sp
            if masked:
                lm = jnp.where(dmask, lm, 0.0)
            btw = _split_dot(lm, tri_gt)
            w = jnp.exp((z - sp) + btw + c)
            if masked:
                w = jnp.where(dmask, w, 0.0)
            wb = w.astype(BF16)
            a = wb.astype(F32) * _dot_nt(do2, vblk)
            suffix = _split_dot(a, tri_ge) + r
            rcp = 1.0 / (1.0 + e)
            pos = z >= 0.0
            sig = jnp.where(pos, rcp, e * rcp)
            sig_neg = jnp.where(pos, e * rcp, rcp)
            dz = a * sig_neg - (tot - suffix) * sig
            if masked:
                dz = jnp.where(dmask, dz, 0.0)
            dzb = dz.astype(BF16)
            dq_acc[...] += _dot(dzb, kblk)
            dk_acc[pl.ds(r0, BK), :] += _dot_tn(dzb, q2)
            dv_acc[pl.ds(r0, BK), :] += _dot_tn(wb, do2)
            return c + btw[:, 0:1] + lm[:, 0:1], suffix[:, 0:1]

        def alive(st):
            return (jnp.max(st[0]) > UNDERFLOW_BOUND).astype(jnp.int32)

        zero = jnp.zeros((2 * BQ, 1), F32)
        st0 = (zero, zero)
        for d in reversed(range(DIAG_TILES)):
            st0 = tile(i * DIAG_TILES + d, st0, scol < srow - d * BK)

        def cond(s):
            return jnp.logical_and(s[0] >= 0, s[1] > 0)

        def step(s):
            kb, _, st = s
            st = tile(kb, st, None)
            return kb - 1, alive(st), st

        lax.while_loop(cond, step, (i * DIAG_TILES - 1, alive(st0), st0))
        dq_ref[...] = (_unstack_heads(dq_acc[...]) * SCALE).astype(dq_ref.dtype)

        @pl.when(i == nqb - 1)
        def _():
            ck = pltpu.make_async_copy(dk_acc, dk_hbm.at[p], sem.at[0])
            cv = pltpu.make_async_copy(dv_acc, dv_hbm.at[p], sem.at[1])
            ck.start()
            cv.start()
            ck.wait()
            cv.wait()

        @pl.when(jnp.logical_and(p == 3, i == nqb - 1))
        def _():
            finish()

    blk = lambda off: pl.BlockSpec((BQ, LANES), lambda p, i: (i, off + p))
    slab = lambda off: pl.BlockSpec((S, LANES), lambda p, i: (0, off + p))
    return pl.pallas_call(
        body, name=name, grid=(4, nqb),
        in_specs=[blk(0), slab(4), slab(8), blk(0), blk(0)] + x_in_specs,
        out_specs=[blk(0), ANY, ANY] + x_out_specs,
        out_shape=[jax.ShapeDtypeStruct((S, MIX_WIDTH), BF16),
                   jax.ShapeDtypeStruct((4, S, LANES), F32), jax.ShapeDtypeStruct((4, S, LANES), F32)] + x_out,
        scratch_shapes=[pltpu.VMEM((2 * BQ, LANES), F32), pltpu.VMEM((S, LANES), F32),
                        pltpu.VMEM((S, LANES), F32), pltpu.SemaphoreType.DMA((2,))]
        + ([] if exchange is None else exchange.SCRATCH),
        compiler_params=_params(("arbitrary", "arbitrary")),
    )(proj, proj, proj, merged, dmerged, *x_in)


def _key_norm_max(k_ref, knorm_ref, nkb):
    def step(kb, m):
        r0 = pl.multiple_of(kb * BK, BK)
        blk = k_ref[pl.ds(r0, BK), :].astype(F32)
        sa, sb = _pair_rowsum(blk * blk)
        return (jnp.maximum(m[0], jnp.max(sa, axis=0, keepdims=True)),
                jnp.maximum(m[1], jnp.max(sb, axis=0, keepdims=True)))

    zero = jnp.zeros((1, 1), F32)
    ma, mb = lax.fori_loop(0, nkb, step, (zero, zero))
    knorm_ref[...] = _select_pair(jnp.broadcast_to(ma, (1, LANES)), jnp.broadcast_to(mb, (1, LANES)))


FQ = 512
FK = FQ
GATE_BLOCKS = FK // BK


def _key_gates(cr_ref, kb):
    blocks = [cr_ref[0, GATE_BLOCKS * kb + j] for j in range(GATE_BLOCKS)]
    per_head = [jnp.broadcast_to(jnp.concatenate([b[h:h + 1] for b in blocks], axis=1), (FQ, FK)) for h in range(2)]
    return _stack(*per_head)


def _last_gate(cr_ref, kb):
    last = cr_ref[0, GATE_BLOCKS * jnp.maximum(kb, 0) + GATE_BLOCKS - 1]
    return _stack(*[jnp.broadcast_to(last[h:h + 1, BK - 1:BK], (FQ, 1)) for h in range(2)])


def _logit_bound(q_ref, knorm_ref):
    qf = q_ref[...].astype(F32) * SCALE
    qa, qb = _pair_rowsum(qf * qf)
    kn = knorm_ref[...]
    return _stack(jnp.sqrt(qa * kn[:, 0:1]), jnp.sqrt(qb * kn[:, HEAD_DIM:HEAD_DIM + 1]))


def _causal_bias(bias_ref):
    srow, scol = _stacked_iotas(FQ, FK)
    bias_ref[...] = jnp.where(scol <= srow, 0.0, NEG_INF)


def _fox_fwd(proj, kv, c_col, c_row, name):
    S = proj.shape[0]
    nqb = S // FQ

    def body(q_ref, k_ref, v_ref, cc_ref, cr_ref, o_ref, lse_ref, acc_ref, knorm_ref, bias_ref):
        i = pl.program_id(1)

        @pl.when(i == 0)
        def _():
            _key_norm_max(k_ref, knorm_ref, S // BK)
            _causal_bias(bias_ref)

        q2 = _head_rows(q_ref, SCALE)
        bound = _logit_bound(q_ref, knorm_ref)
        cc = cc_ref[0]
        ct = _stack(cc[:, 0:1], cc[:, 1:2])
        acc_ref[...] = jnp.zeros_like(acc_ref)

        def tile(kb, st, masked):
            m, l = st
            r0 = pl.multiple_of(kb * FK, FK)
            kblk = k_ref[pl.ds(r0, FK), :]
            vblk = v_ref[pl.ds(r0, FK), :]
            z = _dot_nt(q2, kblk) + ct - _key_gates(cr_ref, kb)
            if masked:
                z = z + bias_ref[...]
            m_new = jnp.maximum(m, jnp.max(z, axis=1, keepdims=True))
            alpha = jnp.exp(m - m_new)
            pr = jnp.exp(z - m_new)
            acc_ref[...] = alpha * acc_ref[...] + _split_dot(pr, vblk)
            return m_new, alpha * l + jnp.sum(pr, axis=1, keepdims=True)

        def alive(kb, st):
            reach = bound + ct - _last_gate(cr_ref, kb) - st[0]
            return (jnp.max(reach) > UNDERFLOW_BOUND).astype(jnp.int32)

        neg = jnp.full((2 * FQ, 1), NEG_INF, F32)
        zero = jnp.zeros((2 * FQ, 1), F32)
        st0 = tile(i, (neg, zero), True)

        def cond(s):
            return jnp.logical_and(s[0] >= 0, s[1] > 0)

        def step(s):
            kb, _, st = s
            st = tile(kb, st, False)
            return kb - 1, alive(kb - 1, st), st

        _, _, (m, l) = lax.while_loop(cond, step, (i - 1, alive(i - 1, st0), st0))
        o_ref[...] = _unstack_heads(acc_ref[...] / l)
        lse = m + jnp.log(l)
        lse_ref[0] = _two_cols(lse[:FQ], lse[FQ:])

    return pl.pallas_call(
        body, name=name, grid=(4, nqb),
        in_specs=[pl.BlockSpec((FQ, LANES), lambda p, i: (i, p)),
                  pl.BlockSpec((S, LANES), lambda p, i: (0, p)),
                  pl.BlockSpec((S, LANES), lambda p, i: (0, 4 + p)),
                  pl.BlockSpec((1, FQ, 2), lambda p, i: (p, i, 0)),
                  pl.BlockSpec((1, S // BK, 8, LANES), lambda p, i: (p, 0, 0, 0))],
        out_specs=[pl.BlockSpec((FQ, LANES), lambda p, i: (i, p)),
                   pl.BlockSpec((1, FQ, 2), lambda p, i: (p, i, 0))],
        out_shape=[jax.ShapeDtypeStruct((S, MIX_WIDTH), F32), jax.ShapeDtypeStruct((4, S, 2), F32)],
        scratch_shapes=[pltpu.VMEM((2 * FQ, LANES), F32), pltpu.VMEM((1, LANES), F32),
                        pltpu.VMEM((2 * FQ, FK), F32)],
        compiler_params=_params(("arbitrary", "arbitrary")),
    )(proj, kv, kv, c_col, c_row)


def _fox_bwd(proj, kv, c_col, c_row, lse, merged, dmerged, dk_prev, dv_prev, dc_prev, name):
    S = proj.shape[0]
    nqb = S // FQ

    def body(q_ref, k_ref, v_ref, cc_ref, cr_ref, lse_ref, o_ref, do_ref, dkp_hbm, dvp_hbm, dcp_ref,
             dq_ref, dk_hbm, dv_hbm, dc_ref, dq_acc, dk_acc, dv_acc, knorm_ref, bias_ref, sem):
        p = pl.program_id(0)
        i = pl.program_id(1)

        @pl.when(i == 0)
        def _():
            ck = pltpu.make_async_copy(dkp_hbm.at[p], dk_acc, sem.at[0])
            cv = pltpu.make_async_copy(dvp_hbm.at[p], dv_acc, sem.at[1])
            ck.start()
            cv.start()
            dc_ref[...] = dcp_ref[...]
            _key_norm_max(k_ref, knorm_ref, S // BK)
            _causal_bias(bias_ref)
            ck.wait()
            cv.wait()

        q2 = _head_rows(q_ref, SCALE)
        do2 = _head_rows(do_ref)
        tot = _stack(*_pair_rowsum(do_ref[...].astype(F32) * o_ref[...]))
        bound = _logit_bound(q_ref, knorm_ref)
        cc = cc_ref[0]
        ct = _stack(cc[:, 0:1], cc[:, 1:2])
        ls = lse_ref[0]
        lse = _stack(ls[:, 0:1], ls[:, 1:2])
        sub = lax.broadcasted_iota(jnp.int32, (8, LANES), 0)
        dq_acc[...] = jnp.zeros_like(dq_acc)

        def tile(kb, masked):
            r0 = pl.multiple_of(kb * FK, FK)
            kblk = k_ref[pl.ds(r0, FK), :]
            vblk = v_ref[pl.ds(r0, FK), :]
            z = _dot_nt(q2, kblk) + ct - _key_gates(cr_ref, kb)
            if masked:
                z = z + bias_ref[...]
            pr = jnp.exp(z - lse)
            ds = pr * (_dot_nt(do2, vblk) - tot)
            dsb = ds.astype(BF16)
            dq_acc[...] += _dot(dsb, kblk)
            dk_acc[pl.ds(r0, FK), :] += _dot_tn(dsb, q2)
            dv_acc[pl.ds(r0, FK), :] += _dot_tn(pr.astype(BF16), do2)
            dca = jnp.sum(ds[:FQ], axis=0, keepdims=True)
            dcb = jnp.sum(ds[FQ:], axis=0, keepdims=True)
            for j in range(GATE_BLOCKS):
                cols = slice(j * BK, (j + 1) * BK)
                old = dc_ref[0, GATE_BLOCKS * kb + j]
                dc_ref[0, GATE_BLOCKS * kb + j] = jnp.where(sub == 0, old - dca[:, cols],
                                                            jnp.where(sub == 1, old - dcb[:, cols], old))

        def alive(kb):
            reach = bound + ct - _last_gate(cr_ref, kb) - lse
            return (jnp.max(reach) > UNDERFLOW_BOUND).astype(jnp.int32)

        tile(i, True)

        def cond(s):
            return jnp.logical_and(s[0] >= 0, s[1] > 0)

        def step(s):
            kb, _ = s
            tile(kb, False)
            return kb - 1, alive(kb - 1)

        lax.while_loop(cond, step, (i - 1, alive(i - 1)))
        dq_ref[...] = (_unstack_heads(dq_acc[...]) * SCALE).astype(dq_ref.dtype)

        @pl.when(i == nqb - 1)
        def _():
            ck = pltpu.make_async_copy(dk_acc, dk_hbm.at[p], sem.at[0])
            cv = pltpu.make_async_copy(dv_acc, dv_hbm.at[p], sem.at[1])
            ck.start()
            cv.start()
            ck.wait()
            cv.wait()

    blk = lambda off: pl.BlockSpec((FQ, LANES), lambda p, i: (i, off + p))
    slab = lambda off: pl.BlockSpec((S, LANES), lambda p, i: (0, off + p))
    cols = pl.BlockSpec((1, FQ, 2), lambda p, i: (p, i, 0))
    rows = pl.BlockSpec((1, S // BK, 8, LANES), lambda p, i: (p, 0, 0, 0))
    return pl.pallas_call(
        body, name=name, grid=(4, nqb),
        in_specs=[blk(0), slab(0), slab(4), cols, rows, cols, blk(0), blk(0), ANY, ANY, rows],
        out_specs=[blk(0), ANY, ANY, rows],
        out_shape=[jax.ShapeDtypeStruct((S, MIX_WIDTH), BF16),
                   jax.ShapeDtypeStruct((4, S, LANES), F32), jax.ShapeDtypeStruct((4, S, LANES), F32),
                   jax.ShapeDtypeStruct((4, S // BK, 8, LANES), F32)],
        scratch_shapes=[pltpu.VMEM((2 * FQ, LANES), F32), pltpu.VMEM((S, LANES), F32),
                        pltpu.VMEM((S, LANES), F32), pltpu.VMEM((1, LANES), F32),
                        pltpu.VMEM((2 * FQ, FK), F32), pltpu.SemaphoreType.DMA((2,))],
        compiler_params=_params(("arbitrary", "arbitrary")),
    )(proj, kv, kv, c_col, c_row, lse, merged, dmerged, dk_prev, dv_prev, dc_prev)


def _lane_scan(x, reverse):
    lane = lax.broadcasted_iota(jnp.int32, x.shape, 1)
    d = 1
    while d < LANES:
        if reverse:
            x = x + jnp.where(lane < LANES - d, pltpu.roll(x, LANES - d, 1), 0.0)
        else:
            x = x + jnp.where(lane >= d, pltpu.roll(x, d, 1), 0.0)
        d *= 2
    return x


def _gate_fwd(fl3, b8):
    nb = fl3.shape[0]

    def body(fl_ref, b_ref, c_ref):
        def step(kb, carry):
            x = fl_ref[kb] + b_ref[...]
            sp, _ = _softplus_parts(-x)
            c = _lane_scan(-sp, False) + carry
            c_ref[kb] = c
            return c[:, LANES - 1:LANES]

        lax.fori_loop(0, nb, step, jnp.zeros((8, 1), F32))

    return pl.pallas_call(body, name="forget_gate_cumsum",
                          out_shape=jax.ShapeDtypeStruct(fl3.shape, F32),
                          compiler_params=_params())(fl3, b8)


def _gate_bwd(dc3, fl3, b8):
    nb = fl3.shape[0]

    def body(dc_ref, fl_ref, b_ref, dfl_ref, db_ref):
        def step(t, st):
            carry, dbs = st
            kb = nb - 1 - t
            g = _lane_scan(dc_ref[kb], True) + carry
            x = fl_ref[kb] + b_ref[...]
            e = jnp.exp(-jnp.abs(x))
            rcp = 1.0 / (1.0 + e)
            dfl = g * jnp.where(x >= 0.0, e * rcp, rcp)
            dfl_ref[kb] = dfl
            return g[:, 0:1], dbs + dfl

        _, dbs = lax.fori_loop(0, nb, step, (jnp.zeros((8, 1), F32), jnp.zeros((8, LANES), F32)))
        db_ref[...] = jnp.broadcast_to(jnp.sum(dbs, axis=1, keepdims=True), (8, LANES))

    return pl.pallas_call(body, name="forget_gate_bwd",
                          out_shape=[jax.ShapeDtypeStruct(fl3.shape, F32), jax.ShapeDtypeStruct((8, LANES), F32)],
                          compiler_params=_params())(dc3, fl3, b8)


MEM_TQ = 512


def _mem_fwd(proj, qcol, mkv, name):
    S = proj.shape[0]
    M = mkv.shape[0]

    def body(q_ref, mk_ref, mv_ref, o_ref, lse_ref):
        q2 = _head_rows(q_ref, SCALE)
        s = _dot_nt(q2, mk_ref[...])
        m = jnp.max(s, axis=1, keepdims=True)
        pr = jnp.exp(s - m)
        l = jnp.sum(pr, axis=1, keepdims=True)
        o_ref[...] = _unstack_heads(_dot(pr.astype(BF16), mv_ref[...]) / l)
        lse = m + jnp.log(l)
        lse_ref[0] = _two_cols(lse[:MEM_TQ], lse[MEM_TQ:])

    return pl.pallas_call(
        body, name=name, grid=(2, S // MEM_TQ),
        in_specs=[pl.BlockSpec((MEM_TQ, LANES), lambda p, i: (i, qcol + p)),
                  pl.BlockSpec((M, LANES), lambda p, i: (0, p)),
                  pl.BlockSpec((M, LANES), lambda p, i: (0, 2 + p))],
        out_specs=[pl.BlockSpec((MEM_TQ, LANES), lambda p, i: (i, p)),
                   pl.BlockSpec((1, MEM_TQ, 2), lambda p, i: (p, i, 0))],
        out_shape=[jax.ShapeDtypeStruct((S, MEM_WIDTH), F32), jax.ShapeDtypeStruct((2, S, 2), F32)],
        compiler_params=_params(("parallel", "parallel")),
    )(proj, mkv, mkv)


def _mem_bwd(proj, qcol, mkv, lse, merged, dmerged, name):
    S = proj.shape[0]
    M = mkv.shape[0]

    def body(q_ref, mk_ref, mv_ref, lse_ref, o_ref, do_ref, dq_ref, dmk_ref, dmv_ref):
        @pl.when(pl.program_id(1) == 0)
        def _():
            dmk_ref[...] = jnp.zeros_like(dmk_ref)
            dmv_ref[...] = jnp.zeros_like(dmv_ref)

        q2 = _head_rows(q_ref, SCALE)
        do2 = _head_rows(do_ref)
        tot = _stack(*_pair_rowsum(do_ref[...].astype(F32) * o_ref[...]))
        ls = lse_ref[0]
        pr = jnp.exp(_dot_nt(q2, mk_ref[...]) - _stack(ls[:, 0:1], ls[:, 1:2]))
        ds = pr * (_dot_nt(do2, mv_ref[...]) - tot)
        dsb = ds.astype(BF16)
        dmk_ref[...] += _dot_tn(dsb, q2)
        dmv_ref[...] += _dot_tn(pr.astype(BF16), do2)
        dq_ref[...] = (_unstack_heads(_dot(dsb, mk_ref[...])) * SCALE).astype(dq_ref.dtype)

    blk = lambda off: pl.BlockSpec((MEM_TQ, LANES), lambda p, i: (i, off + p))
    acc = pl.BlockSpec((M, LANES), lambda p, i: (0, p))
    return pl.pallas_call(
        body, name=name, grid=(2, S // MEM_TQ),
        in_specs=[blk(qcol), pl.BlockSpec((M, LANES), lambda p, i: (0, p)),
                  pl.BlockSpec((M, LANES), lambda p, i: (0, 2 + p)),
                  pl.BlockSpec((1, MEM_TQ, 2), lambda p, i: (p, i, 0)), blk(4), blk(4)],
        out_specs=[blk(0), acc, acc],
        out_shape=[jax.ShapeDtypeStruct((S, MEM_WIDTH), BF16), jax.ShapeDtypeStruct((M, MEM_WIDTH), F32),
                   jax.ShapeDtypeStruct((M, MEM_WIDTH), F32)],
        compiler_params=_params(("parallel", "arbitrary")),
    )(proj, mkv, mkv, lse, merged, dmerged)


def _c_layouts(c3):
    nb = c3.shape[0]
    pairs = c3.reshape(nb, 4, 2, LANES).transpose(1, 0, 2, 3)
    c_row = jnp.pad(pairs, ((0, 0), (0, 0), (0, 6), (0, 0)))
    c_col = pairs.transpose(0, 1, 3, 2).reshape(4, nb * LANES, 2)
    return c_col, c_row


def _local_step(x, mem, wb, rest_shard, sm, loss_target):
    S = x.shape[0]
    nb = S // BK
    vec = lambda a: a.reshape(1, D_MODEL)
    b8 = jnp.broadcast_to(sm["b_f"].reshape(8, 1), (8, LANES))

    saved = []
    shared = None
    h = x
    for l in range(DEPTH):
        if l == N_A:
            w_kvf = jnp.pad(wb["w_kv_shared"], ((0, 0), (0, 1152 - 1032)))
            hs = _rms_fwd(h, vec(sm["kv_norm_g"]), "kv_norm")
            kvf = _mm(hs, w_kvf, out_dtype=F32, name="kv_shared_proj")
            kv = kvf[:, :2 * MIX_WIDTH].astype(BF16)
            fl3 = kvf[:, 2 * MIX_WIDTH:2 * MIX_WIDTH + 8].T.reshape(8, nb, LANES).transpose(1, 0, 2)
            c3 = _gate_fwd(fl3, b8)
            c_col, c_row = _c_layouts(c3)
            shared = dict(h=h, hs=hs, kv=kv, fl3=fl3, c_col=c_col, c_row=c_row)
        hn = _rms_fwd(h, vec(sm["norm1_g"][l]), f"norm1_{l}")
        mn = _rms_fwd(mem, vec(sm["mem_norm_g"][l]), f"mem_norm_{l}")
        mkv = _mm(mn, wb["w_mem_kv"][l], name=f"mem_kv_proj_{l}")
        if l < N_A:
            w_in = wb["w_in_a"][l]
            proj = _mm(hn, w_in, name=f"in_proj_{l}")
            if l == 0:
                mix, gathered = _sb_fwd(proj, f"stickbreak_fwd_{l}", _AllGather, rest_shard)
                _unpack_gathered(PART_REST, gathered, wb)
            else:
                mix = _sb_fwd(proj, f"stickbreak_fwd_{l}")
            lse, qcol = None, 12
        else:
            w_in = wb["w_in_b"][l - N_A]
            proj = _mm(hn, w_in, name=f"in_proj_{l}")
            mix, lse = _fox_fwd(proj, shared["kv"], shared["c_col"], shared["c_row"], f"fox_fwd_{l}")
            qcol = 4
        mo, mlse = _mem_fwd(proj, qcol, mkv, f"mem_attn_fwd_{l}")
        merged = jnp.concatenate([mix, mo], axis=1)
        h_mid = _mm(merged, wb["w_o"][l], out_dtype=F32, epi="add", extra=h, name=f"out_proj_{l}")
        hn2 = _rms_fwd(h_mid, vec(sm["norm2_g"][l]), f"norm2_{l}")
        u, act = _mm(hn2, wb["w_mlp1"][l], epi="relu2", name=f"mlp1_{l}")
        h_out = _mm(act, wb["w_mlp2"][l], out_dtype=F32, epi="add", extra=h_mid, name=f"mlp2_{l}")
        saved.append(dict(h=h, hn=hn, mn=mn, mkv=mkv, proj=proj, lse=lse, mlse=mlse, qcol=qcol, merged=merged,
                          h_mid=h_mid, hn2=hn2, u=u, act=act, w_in=w_in))
        h = h_out

    loss, dh, dg_final = _final_loss(h, vec(sm["final_norm_g"]), loss_target)

    gb = {n: [None] * (DEPTH if n not in ("w_in_a", "w_in_b") else 2) for n in
          ("w_in_a", "w_in_b", "w_mem_kv", "w_o", "w_mlp1", "w_mlp2")}
    gs = {n: [None] * DEPTH for n in ("norm1_g", "mem_norm_g", "norm2_g")}
    dk_sh = jnp.zeros((4, S, LANES), F32)
    dv_sh = jnp.zeros((4, S, LANES), F32)
    dc_sh = jnp.zeros((4, nb, 8, LANES), F32)
    for l in reversed(range(DEPTH)):
        sv = saved[l]
        du = _mm(dh, wb["w_mlp2"][l], mode="nt", epi="drelu2", extra=sv["u"], name=f"mlp2_dx_{l}")
        gb["w_mlp2"][l] = _mm(sv["act"], dh, mode="tn", out_dtype=F32, name=f"mlp2_dw_{l}")
        gb["w_mlp1"][l] = _mm(sv["hn2"], du, mode="tn", out_dtype=F32, name=f"mlp1_dw_{l}")
        dhn2 = _mm(du, wb["w_mlp1"][l], mode="nt", out_dtype=F32, name=f"mlp1_dx_{l}")
        dh, gs["norm2_g"][l] = _rms_bwd(sv["h_mid"], vec(sm["norm2_g"][l]), dhn2, dh, f"norm2_bwd_{l}")
        dmerged = _mm(dh, wb["w_o"][l], mode="nt", name=f"out_proj_dx_{l}")
        gb["w_o"][l] = _mm(sv["merged"], dh, mode="tn", out_dtype=F32, name=f"out_proj_dw_{l}")
        if l == 0:
            gb["w_kv_shared"] = g_kvf[:, :1032]
            rest_grads = _pack_grads(PART_REST, ROWS_REST, gb, None)
            dq, dk, dv, rest_received = _sb_bwd(sv["proj"], sv["merged"], dmerged, f"stickbreak_bwd_{l}",
                                                _Scatter, rest_grads)
        elif l < N_A:
            dq, dk, dv = _sb_bwd(sv["proj"], sv["merged"], dmerged, f"stickbreak_bwd_{l}")
        else:
            dq, dk_sh, dv_sh, dc_sh = _fox_bwd(sv["proj"], shared["kv"], shared["c_col"], shared["c_row"],
                                               sv["lse"], sv["merged"], dmerged, dk_sh, dv_sh, dc_sh,
                                               f"fox_bwd_{l}")
        dqm, dmk, dmv = _mem_bwd(sv["proj"], sv["qcol"], sv["mkv"], sv["mlse"], sv["merged"], dmerged,
                                 f"mem_attn_bwd_{l}")
        if l < N_A:
            flat = lambda t: t.transpose(1, 0, 2).reshape(S, MIX_WIDTH).astype(BF16)
            dproj = jnp.concatenate([dq, flat(dk), flat(dv), dqm], axis=1)
        else:
            dproj = jnp.concatenate([dq, dqm], axis=1)
        name_in = "w_in_a" if l < N_A else "w_in_b"
        gb[name_in][l if l < N_A else l - N_A] = _mm(sv["hn"], dproj, mode="tn", out_dtype=F32,
                                                      name=f"in_proj_dw_{l}")
        dhn = _mm(dproj, sv["w_in"], mode="nt", out_dtype=F32, name=f"in_proj_dx_{l}")
        dh, gs["norm1_g"][l] = _rms_bwd(sv["h"], vec(sm["norm1_g"][l]), dhn, dh, f"norm1_bwd_{l}")
        dmkv = jnp.concatenate([dmk, dmv], axis=1)
        gb["w_mem_kv"][l] = _mm(sv["mn"], dmkv, mode="tn", out_dtype=F32, name=f"mem_kv_dw_{l}")
        dmn = _mm(dmkv, wb["w_mem_kv"][l], mode="nt", out_dtype=F32, name=f"mem_kv_dx_{l}")
        _, gs["mem_norm_g"][l] = _rms_bwd(mem, vec(sm["mem_norm_g"][l]), dmn, None, f"mem_norm_bwd_{l}")
        if l == N_A:
            dfl3, db8 = _gate_bwd(dc_sh.reshape(4, nb, 8, LANES)[:, :, :2].transpose(1, 0, 2, 3).reshape(nb, 8, LANES),
                                  shared["fl3"], b8)
            dfl = dfl3.transpose(1, 0, 2).reshape(8, S).T
            flat = lambda t: t.transpose(1, 0, 2).reshape(S, MIX_WIDTH).astype(BF16)
            dkvf = jnp.concatenate([flat(dk_sh), flat(dv_sh),
                                    jnp.pad(dfl, ((0, 0), (0, LANES - 8))).astype(BF16)], axis=1)
            g_kvf = _mm(shared["hs"], dkvf, mode="tn", out_dtype=F32, name="kv_shared_dw")
            dhs = _mm(dkvf, w_kvf, mode="nt", out_dtype=F32, name="kv_shared_dx")
            dh, g_kvn = _rms_bwd(shared["h"], vec(sm["kv_norm_g"]), dhs, dh, "kv_norm_bwd")
            g_bf = db8[:, 0]

    gsmall = {n: jnp.concatenate(v, axis=0) for n, v in gs.items()}
    gsmall["kv_norm_g"] = g_kvn
    gsmall["final_norm_g"] = dg_final
    gsmall["b_f"] = g_bf
    return loss, dh, gb, gsmall, rest_received


def kernel(x, mem, norm1_g, w_in_a, w_in_b, w_mem_kv, mem_norm_g, w_o, norm2_g, w_mlp1, w_mlp2, kv_norm_g, w_kv_shared, b_f, final_norm_g, loss_target, m_norm1_g, m_w_in_a, m_w_in_b, m_w_mem_kv, m_mem_norm_g, m_w_o, m_norm2_g, m_w_mlp1, m_w_mlp2, m_kv_norm_g, m_w_kv_shared, m_b_f, m_final_norm_g, v_norm1_g, v_w_in_a, v_w_in_b, v_w_mem_kv, v_mem_norm_g, v_w_o, v_norm2_g, v_w_mlp1, v_w_mlp2, v_kv_norm_g, v_w_kv_shared, v_b_f, v_final_norm_g):
    big_w = dict(w_in_a=w_in_a, w_in_b=w_in_b, w_mem_kv=w_mem_kv, w_o=w_o, w_mlp1=w_mlp1, w_mlp2=w_mlp2,
                 w_kv_shared=w_kv_shared)
    small_w = dict(norm1_g=norm1_g, mem_norm_g=mem_norm_g, norm2_g=norm2_g, kv_norm_g=kv_norm_g,
                   final_norm_g=final_norm_g, b_f=b_f)
    big_m = dict(w_in_a=m_w_in_a, w_in_b=m_w_in_b, w_mem_kv=m_w_mem_kv, w_o=m_w_o, w_mlp1=m_w_mlp1,
                 w_mlp2=m_w_mlp2, w_kv_shared=m_w_kv_shared)
    small_m = dict(norm1_g=m_norm1_g, mem_norm_g=m_mem_norm_g, norm2_g=m_norm2_g, kv_norm_g=m_kv_norm_g,
                   final_norm_g=m_final_norm_g, b_f=m_b_f)
    big_v = dict(w_in_a=v_w_in_a, w_in_b=v_w_in_b, w_mem_kv=v_w_mem_kv, w_o=v_w_o, w_mlp1=v_w_mlp1,
                 w_mlp2=v_w_mlp2, w_kv_shared=v_w_kv_shared)
    small_v = dict(norm1_g=v_norm1_g, mem_norm_g=v_mem_norm_g, norm2_g=v_norm2_g, kv_norm_g=v_kv_norm_g,
                   final_norm_g=v_final_norm_g, b_f=v_b_f)

    def pack_all(big, small, dtype):
        return jnp.concatenate([_pack_local(PART_LAYER0, ROWS_LAYER0, big, small, dtype),
                                _pack_local(PART_REST, ROWS_REST, big, None, dtype)], axis=0)

    wb = {n: {} for n in BIG_NAMES}
    _unpack_gathered(PART_LAYER0, _allgather_chips(_pack_local(PART_LAYER0, ROWS_LAYER0, big_w, small_w, BF16)), wb)
    rest_shard = _pack_local(PART_REST, ROWS_REST, big_w, None, BF16)

    loss, dx, gb, gsmall, rest_received = _local_step(x[0], mem[0], wb, rest_shard, small_w, loss_target[0])

    layer0_received = _scatter_chips(_pack_grads(PART_LAYER0, ROWS_LAYER0, gb, gsmall))
    part = jnp.concatenate([_sum4(layer0_received), _sum4(rest_received)], axis=0)
    other = _swap_cores(part)
    g, delta, new_m, new_v = _adamw(part, other, pack_all(big_w, small_w, F32), pack_all(big_m, small_m, F32),
                                    pack_all(big_v, small_v, F32))

    outs = [lax.psum(loss[0, 0], ("x", "y", "c")), dx[None]]
    for packed in (g, delta, new_m, new_v):
        pieces, d = {n: [] for n in BIG_NAMES}, {}
        _unpack_local(PART_LAYER0, packed[:ROWS_LAYER0], True, pieces, d)
        _unpack_local(PART_REST, packed[ROWS_LAYER0:], False, pieces, d)
        d.update(_join_layers(pieces))
        outs.extend(d[n] for n in WEIGHT_ORDER)
    return tuple(outs)
```

```python
import functools
import math

import jax
import jax.numpy as jnp
from jax import lax
from jax.experimental import pallas as pl
from jax.experimental.pallas import tpu as pltpu

F32 = jnp.float32
BF16 = jnp.bfloat16

D_MODEL = 1024
HEAD_DIM = 64
MIX_WIDTH = 512
MEM_WIDTH = 256
DEPTH = 4
N_A = 2
D_FF = 4096
EPS = 1e-6
NEG_INF = -1e30
SCALE = 1.0 / math.sqrt(HEAD_DIM)

ADAM_LR = 0.001
ADAM_B1 = 0.9
ADAM_B2 = 0.999
ADAM_EPS = 1e-08
ADAM_WD = 0.01
ADAM_STEP = 10

LANES = 128
BQ = 256
BK = 128
DIAG_TILES = BQ // BK
CHAINS = 2
UNDERFLOW_BOUND = -110.0
VMEM_LIMIT = 56 * 1024 * 1024

MESH = pl.DeviceIdType.MESH
N_CHIPS = 4

PART_LAYER0 = (
    ("w_in_a", 0, 1, (1024, 448), 1),
    ("w_mem_kv", 0, 1, (256, 512), 0),
    ("w_o", 0, 1, (768, 256), 1),
    ("w_mlp1", 0, 1, (1024, 1024), 1),
    ("w_mlp2", 0, 1, (1024, 1024), 0),
)
PART_REST = (
    ("w_in_a", 1, 2, (1024, 448), 1),
    ("w_in_b", 0, 2, (256, 768), 0),
    ("w_mem_kv", 1, 4, (256, 512), 0),
    ("w_o", 1, 4, (768, 256), 1),
    ("w_mlp1", 1, 4, (1024, 1024), 1),
    ("w_mlp2", 1, 4, (1024, 1024), 0),
    ("w_kv_shared", None, None, (1024, 258), 1),
)
BIG_NAMES = ("w_in_a", "w_in_b", "w_mem_kv", "w_o", "w_mlp1", "w_mlp2", "w_kv_shared")
SMALL = (
    ("norm1_g", (4, 1024)),
    ("mem_norm_g", (4, 1024)),
    ("norm2_g", (4, 1024)),
    ("kv_norm_g", (1, 1024)),
    ("final_norm_g", (1, 1024)),
    ("b_f", (1, 1024)),
)
WEIGHT_ORDER = ("norm1_g", "w_in_a", "w_in_b", "w_mem_kv", "mem_norm_g", "w_o", "norm2_g", "w_mlp1",
                "w_mlp2", "kv_norm_g", "w_kv_shared", "b_f", "final_norm_g")


ROW_ALIGN = 16
PACK_TILE = 256
SMALL_ROWS = ROW_ALIGN
assert sum(s[0] for _, s in SMALL) <= SMALL_ROWS


def _section_rows(entry):
    _, lo, hi, shape, _ = entry
    rows = (1 if lo is None else hi - lo) * math.prod(shape) // D_MODEL
    return rows, -(-rows // ROW_ALIGN) * ROW_ALIGN


def _part_rows(part, extra):
    rows = sum(_section_rows(e)[1] for e in part) + extra
    return -(-rows // PACK_TILE) * PACK_TILE


ROWS_LAYER0 = _part_rows(PART_LAYER0, SMALL_ROWS)
ROWS_REST = _part_rows(PART_REST, 0)


def _params(sem=None):
    return pltpu.CompilerParams(dimension_semantics=sem, vmem_limit_bytes=VMEM_LIMIT)


def _pick(n, cands):
    for c in cands:
        if n % c == 0:
            return c
    raise ValueError(f"no tile for {n}")


def _section(a, entry):
    a = a.reshape(-1, D_MODEL)
    return jnp.pad(a, ((0, _section_rows(entry)[1] - a.shape[0]), (0, 0)))


def _small_block(small, dtype):
    blk = jnp.zeros((SMALL_ROWS, D_MODEL), dtype)
    off = 0
    for n, shp in SMALL:
        a = small[n].astype(dtype)
        if n == "b_f":
            blk = blk.at[off, :a.size].set(a.reshape(-1))
        else:
            blk = blk.at[off:off + shp[0]].set(a.reshape(shp))
        off += shp[0]
    return blk


def _fill(parts, rows, dtype):
    used = sum(p.shape[0] for p in parts)
    return jnp.concatenate(parts + [jnp.zeros((rows - used, D_MODEL), dtype)], axis=0)


def _pack_local(part, rows, big, small, dtype):
    parts = [_section((big[e[0]] if e[1] is None else big[e[0]][e[1]:e[2]]).astype(dtype), e) for e in part]
    if small is not None:
        parts.append(_small_block(small, dtype))
    return _fill(parts, rows, dtype)


def _unpack_local(part, p, with_small, pieces, small):
    off = 0
    for e in part:
        n, lo, hi, shp, _ = e
        rows, reserved = _section_rows(e)
        pieces[n].append((lo, p[off:off + rows].reshape(shp if lo is None else (hi - lo,) + shp)))
        off += reserved
    if with_small:
        for n, shp in SMALL:
            a = p[off:off + shp[0]]
            small[n] = a[0, :8] if n == "b_f" else (a.reshape(D_MODEL) if shp[0] == 1 else a)
            off += shp[0]


def _join_layers(pieces):
    out = {}
    for n, ps in pieces.items():
        ps = sorted(ps, key=lambda t: -1 if t[0] is None else t[0])
        out[n] = ps[0][1] if len(ps) == 1 else jnp.concatenate([a for _, a in ps], axis=0)
    return out


def _unpack_gathered(part, g, weights):
    off = 0
    for e in part:
        n, lo, hi, shp, ax = e
        rows, reserved = _section_rows(e)
        if lo is None:
            sec = g[:, off:off + rows].reshape((N_CHIPS,) + shp)
            weights[n] = jnp.concatenate([sec[j] for j in range(N_CHIPS)], axis=ax)
        else:
            sec = g[:, off:off + rows].reshape((N_CHIPS, hi - lo) + shp)
            for l in range(lo, hi):
                weights[n][l] = jnp.concatenate([sec[j, l - lo] for j in range(N_CHIPS)], axis=ax)
        off += reserved


def _pack_grads(part, rows, gbig, gsmall):
    small = None if gsmall is None else _small_block(gsmall, BF16)
    chunks = []
    for j in range(N_CHIPS):
        parts = []
        for e in part:
            n, lo, hi, shp, ax = e
            w = shp[ax]
            layers = [gbig[n]] if lo is None else [gbig[n][l] for l in range(lo, hi)]
            cut = [lax.slice_in_dim(g, j * w, (j + 1) * w, axis=ax).astype(BF16).reshape(-1, D_MODEL) for g in layers]
            parts.append(_section(cut[0] if len(cut) == 1 else jnp.concatenate(cut, axis=0), e))
        if small is not None:
            parts.append(small)
        chunks.append(_fill(parts, rows, BF16))
    return jnp.stack(chunks, axis=0)


ANY = pl.BlockSpec(memory_space=pl.ANY)


def _other_chips(x, y):
    return [(1 - x, y), (x, 1 - y), (1 - x, 1 - y)]


class _AllGather:
    SCRATCH = [pltpu.SemaphoreType.DMA((3,)), pltpu.SemaphoreType.DMA((3,)), pltpu.SemaphoreType.DMA((3,)),
               pltpu.SemaphoreType.DMA((3,)), pltpu.SemaphoreType.DMA]

    def __init__(self, w_ref, o_ref, send_sems, recv_sems, pass_send, pass_recv, local_sem):
        self.w_ref, self.o_ref = w_ref, o_ref
        self.sems = (send_sems, recv_sems, pass_send, pass_recv, local_sem)
        x, y, c = lax.axis_index("x"), lax.axis_index("y"), lax.axis_index("c")
        half = w_ref.shape[0] // 2
        self.c, self.me, self.sibling = c, 2 * x + y, (x, y, 1 - c)
        self.mine = pl.ds(pl.multiple_of(c * half, ROW_ALIGN), half)
        self.other = pl.ds(pl.multiple_of((1 - c) * half, ROW_ALIGN), half)
        self.chips = _other_chips(x, y)

    def _over_ici(self, j, rows_of):
        chip = self.chips[j]
        return pltpu.make_async_remote_copy(
            src_ref=self.w_ref.at[self.mine], dst_ref=self.o_ref.at[rows_of, self.mine],
            send_sem=self.sems[0].at[j], recv_sem=self.sems[1].at[j],
            device_id=(chip[0], chip[1], self.c), device_id_type=MESH)

    def _over_d2d(self, j, rows):
        where = self.o_ref.at[2 * self.chips[j][0] + self.chips[j][1], rows]
        return pltpu.make_async_remote_copy(src_ref=where, dst_ref=where, send_sem=self.sems[2].at[j],
                                            recv_sem=self.sems[3].at[j], device_id=self.sibling,
                                            device_id_type=MESH)

    def _local(self):
        return pltpu.make_async_copy(self.w_ref, self.o_ref.at[self.me], self.sems[4])

    def start(self):
        self._local().start()
        for j in range(3):
            self._over_ici(j, self.me).start()

    def finish(self):
        for j in range(3):
            self._over_ici(j, 2 * self.chips[j][0] + self.chips[j][1]).wait_recv()
            self._over_d2d(j, self.mine).start()
        for j in range(3):
            self._over_d2d(j, self.other).wait_recv()
        for j in range(3):
            self._over_ici(j, self.me).wait_send()
            self._over_d2d(j, self.mine).wait_send()
        self._local().wait()


class _Scatter:
    SCRATCH = [pltpu.SemaphoreType.DMA((3,)), pltpu.SemaphoreType.DMA((3,)), pltpu.SemaphoreType.DMA]

    def __init__(self, g_ref, o_ref, send_sems, recv_sems, local_sem):
        self.g_ref, self.o_ref, self.sems = g_ref, o_ref, (send_sems, recv_sems, local_sem)
        x, y, c = lax.axis_index("x"), lax.axis_index("y"), lax.axis_index("c")
        self.c, self.me, self.chips = c, 2 * x + y, _other_chips(x, y)

    def _copy(self, j):
        chip = self.chips[j]
        return pltpu.make_async_remote_copy(
            src_ref=self.g_ref.at[2 * chip[0] + chip[1]], dst_ref=self.o_ref.at[self.me],
            send_sem=self.sems[0].at[j], recv_sem=self.sems[1].at[j],
            device_id=(chip[0], chip[1], self.c), device_id_type=MESH)

    def _local(self):
        return pltpu.make_async_copy(self.g_ref.at[self.me], self.o_ref.at[self.me], self.sems[2])

    def start(self):
        self._local().start()
        for j in range(3):
            self._copy(j).start()

    def finish(self):
        for j in range(3):
            self._copy(j).wait()
        self._local().wait()


def _allgather_chips(w):
    def body(w_ref, o_ref, *sems):
        ag = _AllGather(w_ref, o_ref, *sems)
        ag.start()
        ag.finish()

    return pl.pallas_call(
        body, name="allgather_weights",
        out_shape=jax.ShapeDtypeStruct((N_CHIPS,) + w.shape, w.dtype),
        in_specs=[ANY], out_specs=ANY, scratch_shapes=_AllGather.SCRATCH,
    )(w)


def _scatter_chips(g4):
    def body(g_ref, o_ref, *sems):
        sc = _Scatter(g_ref, o_ref, *sems)
        sc.start()
        sc.finish()

    return pl.pallas_call(
        body, name="scatter_grads",
        out_shape=jax.ShapeDtypeStruct(g4.shape, g4.dtype),
        in_specs=[ANY], out_specs=ANY, scratch_shapes=_Scatter.SCRATCH,
    )(g4)


def _swap_cores(p):
    def body(p_ref, o_ref, send_sem, recv_sem):
        x, y, c = lax.axis_index("x"), lax.axis_index("y"), lax.axis_index("c")
        cp = pltpu.make_async_remote_copy(src_ref=p_ref, dst_ref=o_ref, send_sem=send_sem, recv_sem=recv_sem,
                                          device_id=(x, y, 1 - c), device_id_type=MESH)
        cp.start()
        cp.wait()

    return pl.pallas_call(
        body, name="swap_cores",
        out_shape=jax.ShapeDtypeStruct(p.shape, p.dtype),
        in_specs=[ANY], out_specs=ANY,
        scratch_shapes=[pltpu.SemaphoreType.DMA, pltpu.SemaphoreType.DMA],
    )(p)


def _sum4(r4):
    _, R, C = r4.shape

    def body(r_ref, o_ref):
        o_ref[...] = ((r_ref[0].astype(F32) + r_ref[1].astype(F32)) + r_ref[2].astype(F32)) + r_ref[3].astype(F32)

    return pl.pallas_call(
        body, name="sum_chips", grid=(R // PACK_TILE,),
        in_specs=[pl.BlockSpec((N_CHIPS, PACK_TILE, C), lambda i: (0, i, 0))],
        out_specs=pl.BlockSpec((PACK_TILE, C), lambda i: (i, 0)),
        out_shape=jax.ShapeDtypeStruct((R, C), F32),
        compiler_params=_params(("parallel",)),
    )(r4)


def _adamw(pa, pb, w, m, v):
    R, C = w.shape
    c1 = 1.0 - ADAM_B1
    c2 = 1.0 - ADAM_B2
    bc1 = 1.0 - ADAM_B1 ** ADAM_STEP
    bc2 = 1.0 - ADAM_B2 ** ADAM_STEP

    def body(pa_ref, pb_ref, w_ref, m_ref, v_ref, g_ref, d_ref, mo_ref, vo_ref):
        g = pa_ref[...] + pb_ref[...]
        mn = ADAM_B1 * m_ref[...] + c1 * g
        vn = ADAM_B2 * v_ref[...] + c2 * (g * g)
        m_hat = mn / bc1
        v_hat = vn / bc2
        g_ref[...] = g
        d_ref[...] = -ADAM_LR * (m_hat / (jnp.sqrt(v_hat) + ADAM_EPS) + ADAM_WD * w_ref[...])
        mo_ref[...] = mn
        vo_ref[...] = vn

    spec = pl.BlockSpec((PACK_TILE, C), lambda i: (i, 0))
    shp = jax.ShapeDtypeStruct((R, C), F32)
    return pl.pallas_call(
        body, name="adamw", grid=(R // PACK_TILE,),
        in_specs=[spec] * 5, out_specs=[spec] * 4, out_shape=[shp] * 4,
        compiler_params=_params(("parallel",)),
    )(pa, pb, w, m, v)


def _rms_fwd(x, g, name):
    R, Dm = x.shape
    tr = _pick(R, (512, 256, 128))

    def body(x_ref, g_ref, o_ref):
        xf = x_ref[...]
        r = lax.rsqrt(jnp.mean(xf * xf, axis=-1, keepdims=True) + EPS)
        o_ref[...] = (xf * r * g_ref[...]).astype(o_ref.dtype)

    return pl.pallas_call(
        body, name=name, grid=(R // tr,),
        in_specs=[pl.BlockSpec((tr, Dm), lambda i: (i, 0)), pl.BlockSpec((1, Dm), lambda i: (0, 0))],
        out_specs=pl.BlockSpec((tr, Dm), lambda i: (i, 0)),
        out_shape=jax.ShapeDtypeStruct((R, Dm), BF16),
        compiler_params=_params(("parallel",)),
    )(x, g)


def _rms_bwd(x, g, dy, dres, name):
    R, Dm = x.shape
    tr = _pick(R, (256, 128))
    has_res = dres is not None

    def body(*refs):
        if has_res:
            x_ref, g_ref, dy_ref, dres_ref, dx_ref, dg_ref = refs
        else:
            x_ref, g_ref, dy_ref, dx_ref, dg_ref = refs
        xf = x_ref[...]
        dy_ = dy_ref[...].astype(F32)
        r = lax.rsqrt(jnp.mean(xf * xf, axis=-1, keepdims=True) + EPS)
        gdy = dy_ * g_ref[...]
        mdot = jnp.mean(xf * gdy, axis=-1, keepdims=True)
        dx = r * gdy - xf * ((r * r * r) * mdot)
        if has_res:
            dx = dres_ref[...] + dx
        dx_ref[...] = dx

        @pl.when(pl.program_id(0) == 0)
        def _():
            dg_ref[...] = jnp.zeros_like(dg_ref)

        dg_ref[...] += jnp.sum(dy_ * (xf * r), axis=0, keepdims=True)

    row = pl.BlockSpec((tr, Dm), lambda i: (i, 0))
    vec = pl.BlockSpec((1, Dm), lambda i: (0, 0))
    ins = [x, g, dy] + ([dres] if has_res else [])
    return pl.pallas_call(
        body, name=name, grid=(R // tr,),
        in_specs=[row, vec, row] + ([row] if has_res else []),
        out_specs=[row, vec],
        out_shape=[jax.ShapeDtypeStruct((R, Dm), F32), jax.ShapeDtypeStruct((1, Dm), F32)],
        compiler_params=_params(("arbitrary",)),
    )(*ins)


def _final_loss(x, g, tgt):
    R, Dm = x.shape
    tr = _pick(R, (256, 128))

    def body(x_ref, g_ref, t_ref, l_ref, dx_ref, dg_ref):
        xf = x_ref[...]
        gv = g_ref[...]
        r = lax.rsqrt(jnp.mean(xf * xf, axis=-1, keepdims=True) + EPS)
        xr = xf * r
        err = xr * gv - t_ref[...]
        dy_ = err * (1.0 / Dm)
        gdy = dy_ * gv
        mdot = jnp.mean(xf * gdy, axis=-1, keepdims=True)
        dx_ref[...] = r * gdy - xf * ((r * r * r) * mdot)

        @pl.when(pl.program_id(0) == 0)
        def _():
            dg_ref[...] = jnp.zeros_like(dg_ref)
            l_ref[...] = jnp.zeros_like(l_ref)

        dg_ref[...] += jnp.sum(dy_ * xr, axis=0, keepdims=True)
        sq = jnp.sum(err * err, axis=1, keepdims=True)
        l_ref[...] += jnp.sum(sq, axis=0, keepdims=True) * (0.5 / Dm)

    row = pl.BlockSpec((tr, Dm), lambda i: (i, 0))
    vec = pl.BlockSpec((1, Dm), lambda i: (0, 0))
    return pl.pallas_call(
        body, name="final_norm_loss", grid=(R // tr,),
        in_specs=[row, vec, row],
        out_specs=[pl.BlockSpec((1, 1), lambda i: (0, 0)), row, vec],
        out_shape=[jax.ShapeDtypeStruct((1, 1), F32), jax.ShapeDtypeStruct((R, Dm), F32),
                   jax.ShapeDtypeStruct((1, Dm), F32)],
        compiler_params=_params(("arbitrary",)),
    )(x, g, tgt)


MAX_TK = 2048

_DIMS = {"nn": (((1,), (0,)), ((), ())), "nt": (((1,), (1,)), ((), ())), "tn": (((0,), (0,)), ((), ()))}


def _mm(a, b, *, mode="nn", out_dtype=BF16, epi=None, extra=None, name):
    if mode == "nn":
        (M, K), N = a.shape, b.shape[1]
    elif mode == "nt":
        (M, K), N = a.shape, b.shape[0]
    else:
        (K, M), N = a.shape, b.shape[1]
    tm = _pick(M, (1024, 768, 512, 256, 128))
    tn = _pick(N, (1024, 896, 768, 640, 512, 384, 256, 128))
    tk = K if K <= MAX_TK else _pick(K, (MAX_TK, 1024, 512, 256, 128))
    nk = K // tk

    def body(*refs):
        n_in = 3 if extra is not None else 2
        a_ref, b_ref = refs[:2]
        e_ref = refs[2] if extra is not None else None
        outs = refs[n_in:n_in + (2 if epi == "relu2" else 1)]
        k = pl.program_id(2)
        part = lax.dot_general(a_ref[...].astype(BF16), b_ref[...].astype(BF16), _DIMS[mode],
                               preferred_element_type=F32)

        def finish(acc):
            if epi is None:
                outs[0][...] = acc.astype(outs[0].dtype)
            elif epi == "add":
                outs[0][...] = (e_ref[...] + acc).astype(outs[0].dtype)
            elif epi == "relu2":
                outs[0][...] = acc.astype(BF16)
                rl = jnp.maximum(acc, 0.0)
                outs[1][...] = (rl * rl).astype(BF16)
            elif epi == "drelu2":
                u = e_ref[...].astype(F32)
                outs[0][...] = (acc * (2.0 * jnp.maximum(u, 0.0))).astype(outs[0].dtype)

        if nk == 1:
            finish(part)
        else:
            acc_ref = refs[-1]

            @pl.when(k == 0)
            def _():
                acc_ref[...] = part

            @pl.when(jnp.logical_and(k > 0, k < nk - 1))
            def _():
                acc_ref[...] += part

            @pl.when(k == nk - 1)
            def _():
                finish(acc_ref[...] + part)

    if mode == "tn":
        a_spec = pl.BlockSpec((tk, tm), lambda i, j, k: (k, i))
    else:
        a_spec = pl.BlockSpec((tm, tk), lambda i, j, k: (i, k))
    if mode == "nt":
        b_spec = pl.BlockSpec((tn, tk), lambda i, j, k: (j, k))
    else:
        b_spec = pl.BlockSpec((tk, tn), lambda i, j, k: (k, j))
    o_spec = pl.BlockSpec((tm, tn), lambda i, j, k: (i, j))
    ins, in_specs = [a, b], [a_spec, b_spec]
    if extra is not None:
        ins.append(extra)
        in_specs.append(o_spec)
    if epi == "relu2":
        out_shape = [jax.ShapeDtypeStruct((M, N), BF16)] * 2
        out_specs = [o_spec] * 2
    else:
        out_shape = [jax.ShapeDtypeStruct((M, N), out_dtype)]
        out_specs = [o_spec]
    res = pl.pallas_call(
        body, name=name, grid=(M // tm, N // tn, nk),
        in_specs=in_specs, out_specs=out_specs, out_shape=out_shape,
        scratch_shapes=[pltpu.VMEM((tm, tn), F32)] if nk > 1 else [],
        compiler_params=_params(("parallel", "parallel", "arbitrary")),
    )(*ins)
    return res if epi == "relu2" else res[0]


def _dot(a, b):
    return lax.dot_general(a, b, _DIMS["nn"], preferred_element_type=F32)


def _dot_nt(a, b):
    return lax.dot_general(a, b, _DIMS["nt"], preferred_element_type=F32)


def _dot_tn(a, b):
    return lax.dot_general(a, b, _DIMS["tn"], preferred_element_type=F32)


def _split_dot(x, t):
    hi = x.astype(BF16)
    lo = (x - hi.astype(F32)).astype(BF16)
    return _dot(jnp.concatenate([hi, lo], axis=1), jnp.concatenate([t, t], axis=0))


def _head_pair(ref, scale=None):
    xf = ref[...].astype(F32)
    if scale is not None:
        xf = xf * scale
    is_a = lax.broadcasted_iota(jnp.int32, xf.shape, 1) < HEAD_DIM
    return jnp.where(is_a, xf, 0.0).astype(BF16), jnp.where(is_a, 0.0, xf).astype(BF16)


def _stack(a, b):
    return jnp.concatenate([a, b], axis=0)


def _head_rows(ref, scale=None):
    return _stack(*_head_pair(ref, scale))


def _unstack_heads(x):
    rows = x.shape[0] // 2
    return _select_pair(x[:rows], x[rows:])


def _pair_rowsum(x):
    is_a = lax.broadcasted_iota(jnp.int32, x.shape, 1) < HEAD_DIM
    return (jnp.sum(jnp.where(is_a, x, 0.0), axis=1, keepdims=True),
            jnp.sum(jnp.where(is_a, 0.0, x), axis=1, keepdims=True))


def _select_pair(xa, xb):
    is_a = lax.broadcasted_iota(jnp.int32, xa.shape, 1) < HEAD_DIM
    return jnp.where(is_a, xa, xb)


def _two_cols(xa, xb):
    rows = xa.shape[0]
    first = lax.broadcasted_iota(jnp.int32, (rows, 2), 1) == 0
    return jnp.where(first, xa, xb)


def _softplus_parts(z):
    e = jnp.exp(-jnp.abs(z))
    return jnp.maximum(z, 0.0) + jnp.log(1.0 + e), e


def _tile_iotas():
    row = lax.broadcasted_iota(jnp.int32, (BK, BK), 0)
    col = lax.broadcasted_iota(jnp.int32, (BK, BK), 1)
    return row, col


def _stacked_iotas(bq, nk):
    row = lax.broadcasted_iota(jnp.int32, (2 * bq, nk), 0) & (bq - 1)
    col = lax.broadcasted_iota(jnp.int32, (2 * bq, nk), 1)
    return row, col


def _side_exchange(exchange, operand, n_in, n_out):
    if exchange is None:
        return [], [], [], [], lambda refs: (lambda: None, lambda: None)
    out_shape = jax.ShapeDtypeStruct(((N_CHIPS,) + operand.shape) if exchange is _AllGather else operand.shape,
                                     operand.dtype)
    n_sem = len(exchange.SCRATCH)

    def pick(refs):
        def make():
            return exchange(refs[n_in], refs[n_in + 1 + n_out], *refs[len(refs) - n_sem:])

        return (lambda: make().start()), (lambda: make().finish())

    return [operand], [ANY], [out_shape], [ANY], pick


def _sb_fwd(proj, name, exchange=None, operand=None):
    S = proj.shape[0]
    nqb = S // (CHAINS * BQ)
    x_in, x_in_specs, x_out, x_out_specs, pick = _side_exchange(exchange, operand, 3, 1)

    def body(*refs):
        q_ref, k_ref, v_ref = refs[:3]
        o_ref = refs[3 + len(x_in)]
        acc_ref = refs[3 + len(x_in) + 1 + len(x_out)]
        start, finish = pick(refs)
        p = pl.program_id(0)
        i = pl.program_id(1)

        @pl.when(jnp.logical_and(p == 0, i == 0))
        def _():
            start()

        q2 = [_head_rows(q_ref.at[pl.ds(ch * BQ, BQ)], SCALE) for ch in range(CHAINS)]
        row, col = _tile_iotas()
        tri = (row > col).astype(BF16)
        srow, scol = _stacked_iotas(BQ, BK)
        acc_ref[...] = jnp.zeros_like(acc_ref)

        def tile(ch, kb, c, dmask=None, valid=None):
            r0 = pl.multiple_of(kb * BK, BK)
            kblk = k_ref[pl.ds(r0, BK), :]
            vblk = v_ref[pl.ds(r0, BK), :]
            z = _dot_nt(q2[ch], kblk)
            sp, _ = _softplus_parts(z)
            lm = -sp
            if dmask is not None:
                lm = jnp.where(dmask, lm, 0.0)
            btw = _split_dot(lm, tri)
            w = jnp.exp((z - sp) + btw + c)
            if dmask is not None:
                w = jnp.where(dmask, w, 0.0)
            if valid is not None:
                w = w * valid
            acc_ref[ch] += _dot(w.astype(BF16), vblk)
            return c + btw[:, 0:1] + lm[:, 0:1]

        def alive(c):
            return jnp.max(c) > UNDERFLOW_BOUND

        cs = [jnp.zeros((2 * BQ, 1), F32)] * CHAINS
        for d in reversed(range(DIAG_TILES)):
            cs = [tile(ch, (CHAINS * i + ch) * DIAG_TILES + d, cs[ch], dmask=scol < srow - d * BK)
                  for ch in range(CHAINS)]

        def tile_of(ch, t):
            return (CHAINS * i + ch) * DIAG_TILES - 1 - t

        def more(cs, t):
            go = [jnp.logical_and(alive(cs[ch]), tile_of(ch, t) >= 0) for ch in range(CHAINS)]
            return functools.reduce(jnp.logical_or, go).astype(jnp.int32)

        def step(st):
            t, _, cs = st
            new = []
            for ch in range(CHAINS):
                kb = tile_of(ch, t)
                if ch == CHAINS - 1:
                    new.append(tile(ch, kb, cs[ch]))
                else:
                    new.append(tile(ch, jnp.maximum(kb, 0), cs[ch], valid=(kb >= 0).astype(F32)))
            return t + 1, more(new, t + 1), new

        lax.while_loop(lambda st: st[1] > 0, step, (0, more(cs, 0), cs))
        for ch in range(CHAINS):
            o_ref[pl.ds(ch * BQ, BQ), :] = _unstack_heads(acc_ref[ch])

        @pl.when(jnp.logical_and(p == 3, i == nqb - 1))
        def _():
            finish()

    blk = pl.BlockSpec((CHAINS * BQ, LANES), lambda p, i: (i, p))
    res = pl.pallas_call(
        body, name=name, grid=(4, nqb),
        in_specs=[blk, pl.BlockSpec((S, LANES), lambda p, i: (0, 4 + p)),
                  pl.BlockSpec((S, LANES), lambda p, i: (0, 8 + p))] + x_in_specs,
        out_specs=[blk] + x_out_specs,
        out_shape=[jax.ShapeDtypeStruct((S, MIX_WIDTH), F32)] + x_out,
        scratch_shapes=[pltpu.VMEM((CHAINS, 2 * BQ, LANES), F32)] + ([] if exchange is None else exchange.SCRATCH),
        compiler_params=_params(("arbitrary", "arbitrary")),
    )(proj, proj, proj, *x_in)
    return res if exchange is not None else res[0]


def _sb_bwd(proj, merged, dmerged, name, exchange=None, operand=None):
    S = proj.shape[0]
    nqb = S // (CHAINS * BQ)
    x_in, x_in_specs, x_out, x_out_specs, pick = _side_exchange(exchange, operand, 5, 3)

    def body(*refs):
        q_ref, k_ref, v_ref, o_ref, do_ref = refs[:5]
        dq_ref, dk_hbm, dv_hbm = refs[5 + len(x_in):8 + len(x_in)]
        dq_acc, dk_acc, dv_acc, sem = refs[8 + len(x_in) + len(x_out):12 + len(x_in) + len(x_out)]
        start, finish = pick(refs)
        p = pl.program_id(0)
        i = pl.program_id(1)

        @pl.when(jnp.logical_and(p == 0, i == 0))
        def _():
            start()

        @pl.when(i == 0)
        def _():
            dk_acc[...] = jnp.zeros_like(dk_acc)
            dv_acc[...] = jnp.zeros_like(dv_acc)

        rows = [pl.ds(ch * BQ, BQ) for ch in range(CHAINS)]
        q2 = [_head_rows(q_ref.at[rw], SCALE) for rw in rows]
        do2 = [_head_rows(do_ref.at[rw]) for rw in rows]
        tot = [_stack(*_pair_rowsum(do_ref[rw, :].astype(F32) * o_ref[rw, :])) for rw in rows]
        row, col = _tile_iotas()
        tri_gt = (row > col).astype(BF16)
        tri_ge = (row >= col).astype(BF16)
        srow, scol = _stacked_iotas(BQ, BK)
        dq_acc[...] = jnp.zeros_like(dq_acc)

        def tile(ch, kb, st, dmask=None, valid=None):
            masked = dmask is not None
            c, r = st
            r0 = pl.multiple_of(kb * BK, BK)
            kblk = k_ref[pl.ds(r0, BK), :]
            vblk = v_ref[pl.ds(r0, BK), :]
            z = _dot_nt(q2[ch], kblk)
            sp, e = _softplus_parts(z)
            lm = -sp
            if masked:
                lm = jnp.where(dmask, lm, 0.0)
            btw = _split_dot(lm, tri_gt)
            w = jnp.exp((z - sp) + btw + c)
            if masked:
                w = jnp.where(dmask, w, 0.0)
            if valid is not None:
                w = w * valid
            wb = w.astype(BF16)
            a = wb.astype(F32) * _dot_nt(do2[ch], vblk)
            suffix = _split_dot(a, tri_ge) + r
            rcp = 1.0 / (1.0 + e)
            pos = z >= 0.0
            sig = jnp.where(pos, rcp, e * rcp)
            sig_neg = jnp.where(pos, e * rcp, rcp)
            dz = a * sig_neg - (tot[ch] - suffix) * sig
            if masked:
                dz = jnp.where(dmask, dz, 0.0)
            if valid is not None:
                dz = dz * valid
            dzb = dz.astype(BF16)
            dq_acc[ch] += _dot(dzb, kblk)
            dk_acc[pl.ds(r0, BK), :] += _dot_tn(dzb, q2[ch])
            dv_acc[pl.ds(r0, BK), :] += _dot_tn(wb, do2[ch])
            return c + btw[:, 0:1] + lm[:, 0:1], suffix[:, 0:1]

        def alive(st):
            return jnp.max(st[0]) > UNDERFLOW_BOUND

        zero = jnp.zeros((2 * BQ, 1), F32)
        sts = [(zero, zero)] * CHAINS
        for d in reversed(range(DIAG_TILES)):
            sts = [tile(ch, (CHAINS * i + ch) * DIAG_TILES + d, sts[ch], dmask=scol < srow - d * BK)
                   for ch in range(CHAINS)]

        def tile_of(ch, t):
            return (CHAINS * i + ch) * DIAG_TILES - 1 - t

        def more(sts, t):
            go = [jnp.logical_and(alive(sts[ch]), tile_of(ch, t) >= 0) for ch in range(CHAINS)]
            return functools.reduce(jnp.logical_or, go).astype(jnp.int32)

        def step(s):
            t, _, sts = s
            new = []
            for ch in range(CHAINS):
                kb = tile_of(ch, t)
                if ch == CHAINS - 1:
                    new.append(tile(ch, kb, sts[ch]))
                else:
                    new.append(tile(ch, jnp.maximum(kb, 0), sts[ch], valid=(kb >= 0).astype(F32)))
            return t + 1, more(new, t + 1), new

        lax.while_loop(lambda s: s[1] > 0, step, (0, more(sts, 0), sts))
        for ch in range(CHAINS):
            dq_ref[rows[ch], :] = (_unstack_heads(dq_acc[ch]) * SCALE).astype(dq_ref.dtype)

        @pl.when(i == nqb - 1)
        def _():
            ck = pltpu.make_async_copy(dk_acc, dk_hbm.at[p], sem.at[0])
            cv = pltpu.make_async_copy(dv_acc, dv_hbm.at[p], sem.at[1])
            ck.start()
            cv.start()
            ck.wait()
            cv.wait()

        @pl.when(jnp.logical_and(p == 3, i == nqb - 1))
        def _():
            finish()

    blk = lambda off: pl.BlockSpec((CHAINS * BQ, LANES), lambda p, i: (i, off + p))
    slab = lambda off: pl.BlockSpec((S, LANES), lambda p, i: (0, off + p))
    return pl.pallas_call(
        body, name=name, grid=(4, nqb),
        in_specs=[blk(0), slab(4), slab(8), blk(0), blk(0)] + x_in_specs,
        out_specs=[blk(0), ANY, ANY] + x_out_specs,
        out_shape=[jax.ShapeDtypeStruct((S, MIX_WIDTH), BF16),
                   jax.ShapeDtypeStruct((4, S, LANES), F32), jax.ShapeDtypeStruct((4, S, LANES), F32)] + x_out,
        scratch_shapes=[pltpu.VMEM((CHAINS, 2 * BQ, LANES), F32), pltpu.VMEM((S, LANES), F32),
                        pltpu.VMEM((S, LANES), F32), pltpu.SemaphoreType.DMA((2,))]
        + ([] if exchange is None else exchange.SCRATCH),
        compiler_params=_params(("arbitrary", "arbitrary")),
    )(proj, proj, proj, merged, dmerged, *x_in)


def _key_norm_max(k_ref, knorm_ref, nkb):
    def step(kb, m):
        r0 = pl.multiple_of(kb * BK, BK)
        blk = k_ref[pl.ds(r0, BK), :].astype(F32)
        sa, sb = _pair_rowsum(blk * blk)
        return (jnp.maximum(m[0], jnp.max(sa, axis=0, keepdims=True)),
                jnp.maximum(m[1], jnp.max(sb, axis=0, keepdims=True)))

    zero = jnp.zeros((1, 1), F32)
    ma, mb = lax.fori_loop(0, nkb, step, (zero, zero))
    knorm_ref[...] = _select_pair(jnp.broadcast_to(ma, (1, LANES)), jnp.broadcast_to(mb, (1, LANES)))


FQ = 512
FK = FQ
GATE_BLOCKS = FK // BK


def _key_gates(cr_ref, kb):
    blocks = [cr_ref[0, GATE_BLOCKS * kb + j] for j in range(GATE_BLOCKS)]
    per_head = [jnp.broadcast_to(jnp.concatenate([b[h:h + 1] for b in blocks], axis=1), (FQ, FK)) for h in range(2)]
    return _stack(*per_head)


def _last_gate(cr_ref, kb):
    last = cr_ref[0, GATE_BLOCKS * jnp.maximum(kb, 0) + GATE_BLOCKS - 1]
    return _stack(*[jnp.broadcast_to(last[h:h + 1, BK - 1:BK], (FQ, 1)) for h in range(2)])


def _logit_bound(q_ref, knorm_ref):
    qf = q_ref[...].astype(F32) * SCALE
    qa, qb = _pair_rowsum(qf * qf)
    kn = knorm_ref[...]
    return _stack(jnp.sqrt(qa * kn[:, 0:1]), jnp.sqrt(qb * kn[:, HEAD_DIM:HEAD_DIM + 1]))


def _causal_bias(bias_ref):
    srow, scol = _stacked_iotas(FQ, FK)
    bias_ref[...] = jnp.where(scol <= srow, 0.0, NEG_INF)


def _fox_fwd(proj, kv, c_col, c_row, name):
    S = proj.shape[0]
    nqb = S // FQ

    def body(q_ref, k_ref, v_ref, cc_ref, cr_ref, o_ref, lse_ref, acc_ref, knorm_ref, bias_ref):
        i = pl.program_id(1)

        @pl.when(i == 0)
        def _():
            _key_norm_max(k_ref, knorm_ref, S // BK)
            _causal_bias(bias_ref)

        q2 = _head_rows(q_ref, SCALE)
        bound = _logit_bound(q_ref, knorm_ref)
        cc = cc_ref[0]
        ct = _stack(cc[:, 0:1], cc[:, 1:2])
        acc_ref[...] = jnp.zeros_like(acc_ref)

        def tile(kb, st, masked):
            m, l = st
            r0 = pl.multiple_of(kb * FK, FK)
            kblk = k_ref[pl.ds(r0, FK), :]
            vblk = v_ref[pl.ds(r0, FK), :]
            z = _dot_nt(q2, kblk) + ct - _key_gates(cr_ref, kb)
            if masked:
                z = z + bias_ref[...]
            m_new = jnp.maximum(m, jnp.max(z, axis=1, keepdims=True))
            alpha = jnp.exp(m - m_new)
            pr = jnp.exp(z - m_new)
            acc_ref[...] = alpha * acc_ref[...] + _split_dot(pr, vblk)
            return m_new, alpha * l + jnp.sum(pr, axis=1, keepdims=True)

        def alive(kb, st):
            reach = bound + ct - _last_gate(cr_ref, kb) - st[0]
            return (jnp.max(reach) > UNDERFLOW_BOUND).astype(jnp.int32)

        neg = jnp.full((2 * FQ, 1), NEG_INF, F32)
        zero = jnp.zeros((2 * FQ, 1), F32)
        st0 = tile(i, (neg, zero), True)

        def cond(s):
            return jnp.logical_and(s[0] >= 0, s[1] > 0)

        def step(s):
            kb, _, st = s
            st = tile(kb, st, False)
            return kb - 1, alive(kb - 1, st), st

        _, _, (m, l) = lax.while_loop(cond, step, (i - 1, alive(i - 1, st0), st0))
        o_ref[...] = _unstack_heads(acc_ref[...] / l)
        lse = m + jnp.log(l)
        lse_ref[0] = _two_cols(lse[:FQ], lse[FQ:])

    return pl.pallas_call(
        body, name=name, grid=(4, nqb),
        in_specs=[pl.BlockSpec((FQ, LANES), lambda p, i: (i, p)),
                  pl.BlockSpec((S, LANES), lambda p, i: (0, p)),
                  pl.BlockSpec((S, LANES), lambda p, i: (0, 4 + p)),
                  pl.BlockSpec((1, FQ, 2), lambda p, i: (p, i, 0)),
                  pl.BlockSpec((1, S // BK, 8, LANES), lambda p, i: (p, 0, 0, 0))],
        out_specs=[pl.BlockSpec((FQ, LANES), lambda p, i: (i, p)),
                   pl.BlockSpec((1, FQ, 2), lambda p, i: (p, i, 0))],
        out_shape=[jax.ShapeDtypeStruct((S, MIX_WIDTH), F32), jax.ShapeDtypeStruct((4, S, 2), F32)],
        scratch_shapes=[pltpu.VMEM((2 * FQ, LANES), F32), pltpu.VMEM((1, LANES), F32),
                        pltpu.VMEM((2 * FQ, FK), F32)],
        compiler_params=_params(("arbitrary", "arbitrary")),
    )(proj, kv, kv, c_col, c_row)


def _fox_bwd(proj, kv, c_col, c_row, lse, merged, dmerged, dk_prev, dv_prev, dc_prev, name):
    S = proj.shape[0]
    nqb = S // FQ

    def body(q_ref, k_ref, v_ref, cc_ref, cr_ref, lse_ref, o_ref, do_ref, dkp_hbm, dvp_hbm, dcp_ref,
             dq_ref, dk_hbm, dv_hbm, dc_ref, dq_acc, dk_acc, dv_acc, knorm_ref, bias_ref, sem):
        p = pl.program_id(0)
        i = pl.program_id(1)

        @pl.when(i == 0)
        def _():
            ck = pltpu.make_async_copy(dkp_hbm.at[p], dk_acc, sem.at[0])
            cv = pltpu.make_async_copy(dvp_hbm.at[p], dv_acc, sem.at[1])
            ck.start()
            cv.start()
            dc_ref[...] = dcp_ref[...]
            _key_norm_max(k_ref, knorm_ref, S // BK)
            _causal_bias(bias_ref)
            ck.wait()
            cv.wait()

        q2 = _head_rows(q_ref, SCALE)
        do2 = _head_rows(do_ref)
        tot = _stack(*_pair_rowsum(do_ref[...].astype(F32) * o_ref[...]))
        bound = _logit_bound(q_ref, knorm_ref)
        cc = cc_ref[0]
        ct = _stack(cc[:, 0:1], cc[:, 1:2])
        ls = lse_ref[0]
        lse = _stack(ls[:, 0:1], ls[:, 1:2])
        sub = lax.broadcasted_iota(jnp.int32, (8, LANES), 0)
        dq_acc[...] = jnp.zeros_like(dq_acc)

        def tile(kb, masked):
            r0 = pl.multiple_of(kb * FK, FK)
            kblk = k_ref[pl.ds(r0, FK), :]
            vblk = v_ref[pl.ds(r0, FK), :]
            z = _dot_nt(q2, kblk) + ct - _key_gates(cr_ref, kb)
            if masked:
                z = z + bias_ref[...]
            pr = jnp.exp(z - lse)
            ds = pr * (_dot_nt(do2, vblk) - tot)
            dsb = ds.astype(BF16)
            dq_acc[...] += _dot(dsb, kblk)
            dk_acc[pl.ds(r0, FK), :] += _dot_tn(dsb, q2)
            dv_acc[pl.ds(r0, FK), :] += _dot_tn(pr.astype(BF16), do2)
            dca = jnp.sum(ds[:FQ], axis=0, keepdims=True)
            dcb = jnp.sum(ds[FQ:], axis=0, keepdims=True)
            for j in range(GATE_BLOCKS):
                cols = slice(j * BK, (j + 1) * BK)
                old = dc_ref[0, GATE_BLOCKS * kb + j]
                dc_ref[0, GATE_BLOCKS * kb + j] = jnp.where(sub == 0, old - dca[:, cols],
                                                            jnp.where(sub == 1, old - dcb[:, cols], old))

        def alive(kb):
            reach = bound + ct - _last_gate(cr_ref, kb) - lse
            return (jnp.max(reach) > UNDERFLOW_BOUND).astype(jnp.int32)

        tile(i, True)

        def cond(s):
            return jnp.logical_and(s[0] >= 0, s[1] > 0)

        def step(s):
            kb, _ = s
            tile(kb, False)
            return kb - 1, alive(kb - 1)

        lax.while_loop(cond, step, (i - 1, alive(i - 1)))
        dq_ref[...] = (_unstack_heads(dq_acc[...]) * SCALE).astype(dq_ref.dtype)

        @pl.when(i == nqb - 1)
        def _():
            ck = pltpu.make_async_copy(dk_acc, dk_hbm.at[p], sem.at[0])
            cv = pltpu.make_async_copy(dv_acc, dv_hbm.at[p], sem.at[1])
            ck.start()
            cv.start()
            ck.wait()
            cv.wait()

    blk = lambda off: pl.BlockSpec((FQ, LANES), lambda p, i: (i, off + p))
    slab = lambda off: pl.BlockSpec((S, LANES), lambda p, i: (0, off + p))
    cols = pl.BlockSpec((1, FQ, 2), lambda p, i: (p, i, 0))
    rows = pl.BlockSpec((1, S // BK, 8, LANES), lambda p, i: (p, 0, 0, 0))
    return pl.pallas_call(
        body, name=name, grid=(4, nqb),
        in_specs=[blk(0), slab(0), slab(4), cols, rows, cols, blk(0), blk(0), ANY, ANY, rows],
        out_specs=[blk(0), ANY, ANY, rows],
        out_shape=[jax.ShapeDtypeStruct((S, MIX_WIDTH), BF16),
                   jax.ShapeDtypeStruct((4, S, LANES), F32), jax.ShapeDtypeStruct((4, S, LANES), F32),
                   jax.ShapeDtypeStruct((4, S // BK, 8, LANES), F32)],
        scratch_shapes=[pltpu.VMEM((2 * FQ, LANES), F32), pltpu.VMEM((S, LANES), F32),
                        pltpu.VMEM((S, LANES), F32), pltpu.VMEM((1, LANES), F32),
                        pltpu.VMEM((2 * FQ, FK), F32), pltpu.SemaphoreType.DMA((2,))],
        compiler_params=_params(("arbitrary", "arbitrary")),
    )(proj, kv, kv, c_col, c_row, lse, merged, dmerged, dk_prev, dv_prev, dc_prev)


def _lane_scan(x, reverse):
    lane = lax.broadcasted_iota(jnp.int32, x.shape, 1)
    d = 1
    while d < LANES:
        if reverse:
            x = x + jnp.where(lane < LANES - d, pltpu.roll(x, LANES - d, 1), 0.0)
        else:
            x = x + jnp.where(lane >= d, pltpu.roll(x, d, 1), 0.0)
        d *= 2
    return x


def _gate_fwd(fl3, b8):
    nb = fl3.shape[0]

    def body(fl_ref, b_ref, c_ref):
        def step(kb, carry):
            x = fl_ref[kb] + b_ref[...]
            sp, _ = _softplus_parts(-x)
            c = _lane_scan(-sp, False) + carry
            c_ref[kb] = c
            return c[:, LANES - 1:LANES]

        lax.fori_loop(0, nb, step, jnp.zeros((8, 1), F32))

    return pl.pallas_call(body, name="forget_gate_cumsum",
                          out_shape=jax.ShapeDtypeStruct(fl3.shape, F32),
                          compiler_params=_params())(fl3, b8)


def _gate_bwd(dc3, fl3, b8):
    nb = fl3.shape[0]

    def body(dc_ref, fl_ref, b_ref, dfl_ref, db_ref):
        def step(t, st):
            carry, dbs = st
            kb = nb - 1 - t
            g = _lane_scan(dc_ref[kb], True) + carry
            x = fl_ref[kb] + b_ref[...]
            e = jnp.exp(-jnp.abs(x))
            rcp = 1.0 / (1.0 + e)
            dfl = g * jnp.where(x >= 0.0, e * rcp, rcp)
            dfl_ref[kb] = dfl
            return g[:, 0:1], dbs + dfl

        _, dbs = lax.fori_loop(0, nb, step, (jnp.zeros((8, 1), F32), jnp.zeros((8, LANES), F32)))
        db_ref[...] = jnp.broadcast_to(jnp.sum(dbs, axis=1, keepdims=True), (8, LANES))

    return pl.pallas_call(body, name="forget_gate_bwd",
                          out_shape=[jax.ShapeDtypeStruct(fl3.shape, F32), jax.ShapeDtypeStruct((8, LANES), F32)],
                          compiler_params=_params())(dc3, fl3, b8)


MEM_TQ = 512


def _mem_fwd(proj, qcol, mkv, name):
    S = proj.shape[0]
    M = mkv.shape[0]

    def body(q_ref, mk_ref, mv_ref, o_ref, lse_ref):
        q2 = _head_rows(q_ref, SCALE)
        s = _dot_nt(q2, mk_ref[...])
        m = jnp.max(s, axis=1, keepdims=True)
        pr = jnp.exp(s - m)
        l = jnp.sum(pr, axis=1, keepdims=True)
        o_ref[...] = _unstack_heads(_dot(pr.astype(BF16), mv_ref[...]) / l)
        lse = m + jnp.log(l)
        lse_ref[0] = _two_cols(lse[:MEM_TQ], lse[MEM_TQ:])

    return pl.pallas_call(
        body, name=name, grid=(2, S // MEM_TQ),
        in_specs=[pl.BlockSpec((MEM_TQ, LANES), lambda p, i: (i, qcol + p)),
                  pl.BlockSpec((M, LANES), lambda p, i: (0, p)),
                  pl.BlockSpec((M, LANES), lambda p, i: (0, 2 + p))],
        out_specs=[pl.BlockSpec((MEM_TQ, LANES), lambda p, i: (i, p)),
                   pl.BlockSpec((1, MEM_TQ, 2), lambda p, i: (p, i, 0))],
        out_shape=[jax.ShapeDtypeStruct((S, MEM_WIDTH), F32), jax.ShapeDtypeStruct((2, S, 2), F32)],
        compiler_params=_params(("parallel", "parallel")),
    )(proj, mkv, mkv)


def _mem_bwd(proj, qcol, mkv, lse, merged, dmerged, name):
    S = proj.shape[0]
    M = mkv.shape[0]

    def body(q_ref, mk_ref, mv_ref, lse_ref, o_ref, do_ref, dq_ref, dmk_ref, dmv_ref):
        @pl.when(pl.program_id(1) == 0)
        def _():
            dmk_ref[...] = jnp.zeros_like(dmk_ref)
            dmv_ref[...] = jnp.zeros_like(dmv_ref)

        q2 = _head_rows(q_ref, SCALE)
        do2 = _head_rows(do_ref)
        tot = _stack(*_pair_rowsum(do_ref[...].astype(F32) * o_ref[...]))
        ls = lse_ref[0]
        pr = jnp.exp(_dot_nt(q2, mk_ref[...]) - _stack(ls[:, 0:1], ls[:, 1:2]))
        ds = pr * (_dot_nt(do2, mv_ref[...]) - tot)
        dsb = ds.astype(BF16)
        dmk_ref[...] += _dot_tn(dsb, q2)
        dmv_ref[...] += _dot_tn(pr.astype(BF16), do2)
        dq_ref[...] = (_unstack_heads(_dot(dsb, mk_ref[...])) * SCALE).astype(dq_ref.dtype)

    blk = lambda off: pl.BlockSpec((MEM_TQ, LANES), lambda p, i: (i, off + p))
    acc = pl.BlockSpec((M, LANES), lambda p, i: (0, p))
    return pl.pallas_call(
        body, name=name, grid=(2, S // MEM_TQ),
        in_specs=[blk(qcol), pl.BlockSpec((M, LANES), lambda p, i: (0, p)),
                  pl.BlockSpec((M, LANES), lambda p, i: (0, 2 + p)),
                  pl.BlockSpec((1, MEM_TQ, 2), lambda p, i: (p, i, 0)), blk(4), blk(4)],
        out_specs=[blk(0), acc, acc],
        out_shape=[jax.ShapeDtypeStruct((S, MEM_WIDTH), BF16), jax.ShapeDtypeStruct((M, MEM_WIDTH), F32),
                   jax.ShapeDtypeStruct((M, MEM_WIDTH), F32)],
        compiler_params=_params(("parallel", "arbitrary")),
    )(proj, mkv, mkv, lse, merged, dmerged)


def _c_layouts(c3):
    nb = c3.shape[0]
    pairs = c3.reshape(nb, 4, 2, LANES).transpose(1, 0, 2, 3)
    c_row = jnp.pad(pairs, ((0, 0), (0, 0), (0, 6), (0, 0)))
    c_col = pairs.transpose(0, 1, 3, 2).reshape(4, nb * LANES, 2)
    return c_col, c_row


def _local_step(x, mem, wb, rest_shard, sm, loss_target):
    S = x.shape[0]
    nb = S // BK
    vec = lambda a: a.reshape(1, D_MODEL)
    b8 = jnp.broadcast_to(sm["b_f"].reshape(8, 1), (8, LANES))

    saved = []
    shared = None
    h = x
    for l in range(DEPTH):
        if l == N_A:
            w_kvf = jnp.pad(wb["w_kv_shared"], ((0, 0), (0, 1152 - 1032)))
            hs = _rms_fwd(h, vec(sm["kv_norm_g"]), "kv_norm")
            kvf = _mm(hs, w_kvf, out_dtype=F32, name="kv_shared_proj")
            kv = kvf[:, :2 * MIX_WIDTH].astype(BF16)
            fl3 = kvf[:, 2 * MIX_WIDTH:2 * MIX_WIDTH + 8].T.reshape(8, nb, LANES).transpose(1, 0, 2)
            c3 = _gate_fwd(fl3, b8)
            c_col, c_row = _c_layouts(c3)
            shared = dict(h=h, hs=hs, kv=kv, fl3=fl3, c_col=c_col, c_row=c_row)
        hn = _rms_fwd(h, vec(sm["norm1_g"][l]), f"norm1_{l}")
        mn = _rms_fwd(mem, vec(sm["mem_norm_g"][l]), f"mem_norm_{l}")
        mkv = _mm(mn, wb["w_mem_kv"][l], name=f"mem_kv_proj_{l}")
        if l < N_A:
            w_in = wb["w_in_a"][l]
            proj = _mm(hn, w_in, name=f"in_proj_{l}")
            if l == 0:
                mix, gathered = _sb_fwd(proj, f"stickbreak_fwd_{l}", _AllGather, rest_shard)
                _unpack_gathered(PART_REST, gathered, wb)
            else:
                mix = _sb_fwd(proj, f"stickbreak_fwd_{l}")
            lse, qcol = None, 12
        else:
            w_in = wb["w_in_b"][l - N_A]
            proj = _mm(hn, w_in, name=f"in_proj_{l}")
            mix, lse = _fox_fwd(proj, shared["kv"], shared["c_col"], shared["c_row"], f"fox_fwd_{l}")
            qcol = 4
        mo, mlse = _mem_fwd(proj, qcol, mkv, f"mem_attn_fwd_{l}")
        merged = jnp.concatenate([mix, mo], axis=1)
        h_mid = _mm(merged, wb["w_o"][l], out_dtype=F32, epi="add", extra=h, name=f"out_proj_{l}")
        hn2 = _rms_fwd(h_mid, vec(sm["norm2_g"][l]), f"norm2_{l}")
        u, act = _mm(hn2, wb["w_mlp1"][l], epi="relu2", name=f"mlp1_{l}")
        h_out = _mm(act, wb["w_mlp2"][l], out_dtype=F32, epi="add", extra=h_mid, name=f"mlp2_{l}")
        saved.append(dict(h=h, hn=hn, mn=mn, mkv=mkv, proj=proj, lse=lse, mlse=mlse, qcol=qcol, merged=merged,
                          h_mid=h_mid, hn2=hn2, u=u, act=act, w_in=w_in))
        h = h_out

    loss, dh, dg_final = _final_loss(h, vec(sm["final_norm_g"]), loss_target)

    gb = {n: [None] * (DEPTH if n not in ("w_in_a", "w_in_b") else 2) for n in
          ("w_in_a", "w_in_b", "w_mem_kv", "w_o", "w_mlp1", "w_mlp2")}
    gs = {n: [None] * DEPTH for n in ("norm1_g", "mem_norm_g", "norm2_g")}
    dk_sh = jnp.zeros((4, S, LANES), F32)
    dv_sh = jnp.zeros((4, S, LANES), F32)
    dc_sh = jnp.zeros((4, nb, 8, LANES), F32)
    for l in reversed(range(DEPTH)):
        sv = saved[l]
        du = _mm(dh, wb["w_mlp2"][l], mode="nt", epi="drelu2", extra=sv["u"], name=f"mlp2_dx_{l}")
        gb["w_mlp2"][l] = _mm(sv["act"], dh, mode="tn", out_dtype=F32, name=f"mlp2_dw_{l}")
        gb["w_mlp1"][l] = _mm(sv["hn2"], du, mode="tn", out_dtype=F32, name=f"mlp1_dw_{l}")
        dhn2 = _mm(du, wb["w_mlp1"][l], mode="nt", out_dtype=F32, name=f"mlp1_dx_{l}")
        dh, gs["norm2_g"][l] = _rms_bwd(sv["h_mid"], vec(sm["norm2_g"][l]), dhn2, dh, f"norm2_bwd_{l}")
        dmerged = _mm(dh, wb["w_o"][l], mode="nt", name=f"out_proj_dx_{l}")
        gb["w_o"][l] = _mm(sv["merged"], dh, mode="tn", out_dtype=F32, name=f"out_proj_dw_{l}")
        if l == 0:
            gb["w_kv_shared"] = g_kvf[:, :1032]
            rest_grads = _pack_grads(PART_REST, ROWS_REST, gb, None)
            dq, dk, dv, rest_received = _sb_bwd(sv["proj"], sv["merged"], dmerged, f"stickbreak_bwd_{l}",
                                                _Scatter, rest_grads)
        elif l < N_A:
            dq, dk, dv = _sb_bwd(sv["proj"], sv["merged"], dmerged, f"stickbreak_bwd_{l}")
        else:
            dq, dk_sh, dv_sh, dc_sh = _fox_bwd(sv["proj"], shared["kv"], shared["c_col"], shared["c_row"],
                                               sv["lse"], sv["merged"], dmerged, dk_sh, dv_sh, dc_sh,
                                               f"fox_bwd_{l}")
        dqm, dmk, dmv = _mem_bwd(sv["proj"], sv["qcol"], sv["mkv"], sv["mlse"], sv["merged"], dmerged,
                                 f"mem_attn_bwd_{l}")
        if l < N_A:
            flat = lambda t: t.transpose(1, 0, 2).reshape(S, MIX_WIDTH).astype(BF16)
            dproj = jnp.concatenate([dq, flat(dk), flat(dv), dqm], axis=1)
        else:
            dproj = jnp.concatenate([dq, dqm], axis=1)
        name_in = "w_in_a" if l < N_A else "w_in_b"
        gb[name_in][l if l < N_A else l - N_A] = _mm(sv["hn"], dproj, mode="tn", out_dtype=F32,
                                                      name=f"in_proj_dw_{l}")
        dhn = _mm(dproj, sv["w_in"], mode="nt", out_dtype=F32, name=f"in_proj_dx_{l}")
        dh, gs["norm1_g"][l] = _rms_bwd(sv["h"], vec(sm["norm1_g"][l]), dhn, dh, f"norm1_bwd_{l}")
        dmkv = jnp.concatenate([dmk, dmv], axis=1)
        gb["w_mem_kv"][l] = _mm(sv["mn"], dmkv, mode="tn", out_dtype=F32, name=f"mem_kv_dw_{l}")
        dmn = _mm(dmkv, wb["w_mem_kv"][l], mode="nt", out_dtype=F32, name=f"mem_kv_dx_{l}")
        _, gs["mem_norm_g"][l] = _rms_bwd(mem, vec(sm["mem_norm_g"][l]), dmn, None, f"mem_norm_bwd_{l}")
        if l == N_A:
            dfl3, db8 = _gate_bwd(dc_sh.reshape(4, nb, 8, LANES)[:, :, :2].transpose(1, 0, 2, 3).reshape(nb, 8, LANES),
                                  shared["fl3"], b8)
            dfl = dfl3.transpose(1, 0, 2).reshape(8, S).T
            flat = lambda t: t.transpose(1, 0, 2).reshape(S, MIX_WIDTH).astype(BF16)
            dkvf = jnp.concatenate([flat(dk_sh), flat(dv_sh),
                                    jnp.pad(dfl, ((0, 0), (0, LANES - 8))).astype(BF16)], axis=1)
            g_kvf = _mm(shared["hs"], dkvf, mode="tn", out_dtype=F32, name="kv_shared_dw")
            dhs = _mm(dkvf, w_kvf, mode="nt", out_dtype=F32, name="kv_shared_dx")
            dh, g_kvn = _rms_bwd(shared["h"], vec(sm["kv_norm_g"]), dhs, dh, "kv_norm_bwd")
            g_bf = db8[:, 0]

    gsmall = {n: jnp.concatenate(v, axis=0) for n, v in gs.items()}
    gsmall["kv_norm_g"] = g_kvn
    gsmall["final_norm_g"] = dg_final
    gsmall["b_f"] = g_bf
    return loss, dh, gb, gsmall, rest_received


def kernel(x, mem, norm1_g, w_in_a, w_in_b, w_mem_kv, mem_norm_g, w_o, norm2_g, w_mlp1, w_mlp2, kv_norm_g, w_kv_shared, b_f, final_norm_g, loss_target, m_norm1_g, m_w_in_a, m_w_in_b, m_w_mem_kv, m_mem_norm_g, m_w_o, m_norm2_g, m_w_mlp1, m_w_mlp2, m_kv_norm_g, m_w_kv_shared, m_b_f, m_final_norm_g, v_norm1_g, v_w_in_a, v_w_in_b, v_w_mem_kv, v_mem_norm_g, v_w_o, v_norm2_g, v_w_mlp1, v_w_mlp2, v_kv_norm_g, v_w_kv_shared, v_b_f, v_final_norm_g):
    big_w = dict(w_in_a=w_in_a, w_in_b=w_in_b, w_mem_kv=w_mem_kv, w_o=w_o, w_mlp1=w_mlp1, w_mlp2=w_mlp2,
                 w_kv_shared=w_kv_shared)
    small_w = dict(norm1_g=norm1_g, mem_norm_g=mem_norm_g, norm2_g=norm2_g, kv_norm_g=kv_norm_g,
                   final_norm_g=final_norm_g, b_f=b_f)
    big_m = dict(w_in_a=m_w_in_a, w_in_b=m_w_in_b, w_mem_kv=m_w_mem_kv, w_o=m_w_o, w_mlp1=m_w_mlp1,
                 w_mlp2=m_w_mlp2, w_kv_shared=m_w_kv_shared)
    small_m = dict(norm1_g=m_norm1_g, mem_norm_g=m_mem_norm_g, norm2_g=m_norm2_g, kv_norm_g=m_kv_norm_g,
                   final_norm_g=m_final_norm_g, b_f=m_b_f)
    big_v = dict(w_in_a=v_w_in_a, w_in_b=v_w_in_b, w_mem_kv=v_w_mem_kv, w_o=v_w_o, w_mlp1=v_w_mlp1,
                 w_mlp2=v_w_mlp2, w_kv_shared=v_w_kv_shared)
    small_v = dict(norm1_g=v_norm1_g, mem_norm_g=v_mem_norm_g, norm2_g=v_norm2_g, kv_norm_g=v_kv_norm_g,
                   final_norm_g=v_final_norm_g, b_f=v_b_f)

    def pack_all(big, small, dtype):
        return jnp.concatenate([_pack_local(PART_LAYER0, ROWS_LAYER0, big, small, dtype),
                                _pack_local(PART_REST, ROWS_REST, big, None, dtype)], axis=0)

    wb = {n: {} for n in BIG_NAMES}
    _unpack_gathered(PART_LAYER0, _allgather_chips(_pack_local(PART_LAYER0, ROWS_LAYER0, big_w, small_w, BF16)), wb)
    rest_shard = _pack_local(PART_REST, ROWS_REST, big_w, None, BF16)

    loss, dx, gb, gsmall, rest_received = _local_step(x[0], mem[0], wb, rest_shard, small_w, loss_target[0])

    layer0_received = _scatter_chips(_pack_grads(PART_LAYER0, ROWS_LAYER0, gb, gsmall))
    part = jnp.concatenate([_sum4(layer0_received), _sum4(rest_received)], axis=0)
    other = _swap_cores(part)
    g, delta, new_m, new_v = _adamw(part, other, pack_all(big_w, small_w, F32), pack_all(big_m, small_m, F32),
                                    pack_all(big_v, small_v, F32))

    outs = [lax.psum(loss[0, 0], ("x", "y", "c")), dx[None]]
    for packed in (g, delta, new_m, new_v):
        pieces, d = {n: [] for n in BIG_NAMES}, {}
        _unpack_local(PART_LAYER0, packed[:ROWS_LAYER0], True, pieces, d)
        _unpack_local(PART_REST, packed[ROWS_LAYER0:], False, pieces, d)
        d.update(_join_layers(pieces))
        outs.extend(d[n] for n in WEIGHT_ORDER)
    return tuple(outs)
```

```python
import functools
import math

import jax
import jax.numpy as jnp
from jax import lax
from jax.experimental import pallas as pl
from jax.experimental.pallas import tpu as pltpu

F32 = jnp.float32
BF16 = jnp.bfloat16

D_MODEL = 1024
HEAD_DIM = 64
MIX_WIDTH = 512
MEM_WIDTH = 256
MERGED_WIDTH = MIX_WIDTH + MEM_WIDTH
DEPTH = 4
N_A = 2
D_FF = 4096
EPS = 1e-6
NEG_INF = -1e30
SCALE = 1.0 / math.sqrt(HEAD_DIM)

ADAM_LR = 0.001
ADAM_B1 = 0.9
ADAM_B2 = 0.999
ADAM_EPS = 1e-08
ADAM_WD = 0.01
ADAM_STEP = 10

LANES = 128
BQ = 256
BK = 128
DIAG_TILES = BQ // BK
CHAINS = 2
UNDERFLOW_BOUND = -110.0
VMEM_LIMIT = 56 * 1024 * 1024

MESH = pl.DeviceIdType.MESH
N_CHIPS = 4

PART_LAYER0 = (
    ("w_in_a", 0, 1, (1024, 448), 1),
    ("w_mem_kv", 0, 1, (256, 512), 0),
    ("w_o", 0, 1, (768, 256), 1),
    ("w_mlp1", 0, 1, (1024, 1024), 1),
    ("w_mlp2", 0, 1, (1024, 1024), 0),
)
PART_REST = (
    ("w_in_a", 1, 2, (1024, 448), 1),
    ("w_in_b", 0, 2, (256, 768), 0),
    ("w_mem_kv", 1, 4, (256, 512), 0),
    ("w_o", 1, 4, (768, 256), 1),
    ("w_mlp1", 1, 4, (1024, 1024), 1),
    ("w_mlp2", 1, 4, (1024, 1024), 0),
    ("w_kv_shared", None, None, (1024, 258), 1),
)
BIG_NAMES = ("w_in_a", "w_in_b", "w_mem_kv", "w_o", "w_mlp1", "w_mlp2", "w_kv_shared")
SMALL = (
    ("norm1_g", (4, 1024)),
    ("mem_norm_g", (4, 1024)),
    ("norm2_g", (4, 1024)),
    ("kv_norm_g", (1, 1024)),
    ("final_norm_g", (1, 1024)),
    ("b_f", (1, 1024)),
)
WEIGHT_ORDER = ("norm1_g", "w_in_a", "w_in_b", "w_mem_kv", "mem_norm_g", "w_o", "norm2_g", "w_mlp1",
                "w_mlp2", "kv_norm_g", "w_kv_shared", "b_f", "final_norm_g")


ROW_ALIGN = 16
PACK_TILE = 256
SMALL_ROWS = ROW_ALIGN
assert sum(s[0] for _, s in SMALL) <= SMALL_ROWS


def _section_rows(entry):
    _, lo, hi, shape, _ = entry
    rows = (1 if lo is None else hi - lo) * math.prod(shape) // D_MODEL
    return rows, -(-rows // ROW_ALIGN) * ROW_ALIGN


def _part_rows(part, extra):
    rows = sum(_section_rows(e)[1] for e in part) + extra
    return -(-rows // PACK_TILE) * PACK_TILE


ROWS_LAYER0 = _part_rows(PART_LAYER0, SMALL_ROWS)
ROWS_REST = _part_rows(PART_REST, 0)


def _params(sem=None):
    return pltpu.CompilerParams(dimension_semantics=sem, vmem_limit_bytes=VMEM_LIMIT)


def _pick(n, cands):
    for c in cands:
        if n % c == 0:
            return c
    raise ValueError(f"no tile for {n}")


def _section(a, entry):
    a = a.reshape(-1, D_MODEL)
    return jnp.pad(a, ((0, _section_rows(entry)[1] - a.shape[0]), (0, 0)))


def _small_block(small, dtype):
    blk = jnp.zeros((SMALL_ROWS, D_MODEL), dtype)
    off = 0
    for n, shp in SMALL:
        a = small[n].astype(dtype)
        if n == "b_f":
            blk = blk.at[off, :a.size].set(a.reshape(-1))
        else:
            blk = blk.at[off:off + shp[0]].set(a.reshape(shp))
        off += shp[0]
    return blk


def _fill(parts, rows, dtype):
    used = sum(p.shape[0] for p in parts)
    return jnp.concatenate(parts + [jnp.zeros((rows - used, D_MODEL), dtype)], axis=0)


def _pack_local(part, rows, big, small, dtype):
    parts = [_section((big[e[0]] if e[1] is None else big[e[0]][e[1]:e[2]]).astype(dtype), e) for e in part]
    if small is not None:
        parts.append(_small_block(small, dtype))
    return _fill(parts, rows, dtype)


def _unpack_local(part, p, with_small, pieces, small):
    off = 0
    for e in part:
        n, lo, hi, shp, _ = e
        rows, reserved = _section_rows(e)
        pieces[n].append((lo, p[off:off + rows].reshape(shp if lo is None else (hi - lo,) + shp)))
        off += reserved
    if with_small:
        for n, shp in SMALL:
            a = p[off:off + shp[0]]
            small[n] = a[0, :8] if n == "b_f" else (a.reshape(D_MODEL) if shp[0] == 1 else a)
            off += shp[0]


def _join_layers(pieces):
    out = {}
    for n, ps in pieces.items():
        ps = sorted(ps, key=lambda t: -1 if t[0] is None else t[0])
        out[n] = ps[0][1] if len(ps) == 1 else jnp.concatenate([a for _, a in ps], axis=0)
    return out


def _unpack_gathered(part, g, weights):
    off = 0
    for e in part:
        n, lo, hi, shp, ax = e
        rows, reserved = _section_rows(e)
        if lo is None:
            sec = g[:, off:off + rows].reshape((N_CHIPS,) + shp)
            weights[n] = jnp.concatenate([sec[j] for j in range(N_CHIPS)], axis=ax)
        else:
            sec = g[:, off:off + rows].reshape((N_CHIPS, hi - lo) + shp)
            for l in range(lo, hi):
                weights[n][l] = jnp.concatenate([sec[j, l - lo] for j in range(N_CHIPS)], axis=ax)
        off += reserved


def _pack_grads(part, rows, gbig, gsmall):
    small = None if gsmall is None else _small_block(gsmall, BF16)
    chunks = []
    for j in range(N_CHIPS):
        parts = []
        for e in part:
            n, lo, hi, shp, ax = e
            w = shp[ax]
            layers = [gbig[n]] if lo is None else [gbig[n][l] for l in range(lo, hi)]
            cut = [lax.slice_in_dim(g, j * w, (j + 1) * w, axis=ax).astype(BF16).reshape(-1, D_MODEL) for g in layers]
            parts.append(_section(cut[0] if len(cut) == 1 else jnp.concatenate(cut, axis=0), e))
        if small is not None:
            parts.append(small)
        chunks.append(_fill(parts, rows, BF16))
    return jnp.stack(chunks, axis=0)


ANY = pl.BlockSpec(memory_space=pl.ANY)


def _other_chips(x, y):
    return [(1 - x, y), (x, 1 - y), (1 - x, 1 - y)]


class _AllGather:
    SCRATCH = [pltpu.SemaphoreType.DMA((3,)), pltpu.SemaphoreType.DMA((3,)), pltpu.SemaphoreType.DMA((3,)),
               pltpu.SemaphoreType.DMA((3,)), pltpu.SemaphoreType.DMA]

    def __init__(self, w_ref, o_ref, send_sems, recv_sems, pass_send, pass_recv, local_sem):
        self.w_ref, self.o_ref = w_ref, o_ref
        self.sems = (send_sems, recv_sems, pass_send, pass_recv, local_sem)
        x, y, c = lax.axis_index("x"), lax.axis_index("y"), lax.axis_index("c")
        half = w_ref.shape[0] // 2
        self.c, self.me, self.sibling = c, 2 * x + y, (x, y, 1 - c)
        self.mine = pl.ds(pl.multiple_of(c * half, ROW_ALIGN), half)
        self.other = pl.ds(pl.multiple_of((1 - c) * half, ROW_ALIGN), half)
        self.chips = _other_chips(x, y)

    def _over_ici(self, j, rows_of):
        chip = self.chips[j]
        return pltpu.make_async_remote_copy(
            src_ref=self.w_ref.at[self.mine], dst_ref=self.o_ref.at[rows_of, self.mine],
            send_sem=self.sems[0].at[j], recv_sem=self.sems[1].at[j],
            device_id=(chip[0], chip[1], self.c), device_id_type=MESH)

    def _over_d2d(self, j, rows):
        where = self.o_ref.at[2 * self.chips[j][0] + self.chips[j][1], rows]
        return pltpu.make_async_remote_copy(src_ref=where, dst_ref=where, send_sem=self.sems[2].at[j],
                                            recv_sem=self.sems[3].at[j], device_id=self.sibling,
                                            device_id_type=MESH)

    def _local(self):
        return pltpu.make_async_copy(self.w_ref, self.o_ref.at[self.me], self.sems[4])

    def start(self):
        self._local().start()
        for j in range(3):
            self._over_ici(j, self.me).start()

    def finish(self):
        for j in range(3):
            self._over_ici(j, 2 * self.chips[j][0] + self.chips[j][1]).wait_recv()
            self._over_d2d(j, self.mine).start()
        for j in range(3):
            self._over_d2d(j, self.other).wait_recv()
        for j in range(3):
            self._over_ici(j, self.me).wait_send()
            self._over_d2d(j, self.mine).wait_send()
        self._local().wait()


class _Scatter:
    SCRATCH = [pltpu.SemaphoreType.DMA((3,)), pltpu.SemaphoreType.DMA((3,)), pltpu.SemaphoreType.DMA]

    def __init__(self, g_ref, o_ref, send_sems, recv_sems, local_sem):
        self.g_ref, self.o_ref, self.sems = g_ref, o_ref, (send_sems, recv_sems, local_sem)
        x, y, c = lax.axis_index("x"), lax.axis_index("y"), lax.axis_index("c")
        self.c, self.me, self.chips = c, 2 * x + y, _other_chips(x, y)

    def _copy(self, j):
        chip = self.chips[j]
        return pltpu.make_async_remote_copy(
            src_ref=self.g_ref.at[2 * chip[0] + chip[1]], dst_ref=self.o_ref.at[self.me],
            send_sem=self.sems[0].at[j], recv_sem=self.sems[1].at[j],
            device_id=(chip[0], chip[1], self.c), device_id_type=MESH)

    def _local(self):
        return pltpu.make_async_copy(self.g_ref.at[self.me], self.o_ref.at[self.me], self.sems[2])

    def start(self):
        self._local().start()
        for j in range(3):
            self._copy(j).start()

    def finish(self):
        for j in range(3):
            self._copy(j).wait()
        self._local().wait()


def _allgather_chips(w):
    def body(w_ref, o_ref, *sems):
        ag = _AllGather(w_ref, o_ref, *sems)
        ag.start()
        ag.finish()

    return pl.pallas_call(
        body, name="allgather_weights",
        out_shape=jax.ShapeDtypeStruct((N_CHIPS,) + w.shape, w.dtype),
        in_specs=[ANY], out_specs=ANY, scratch_shapes=_AllGather.SCRATCH,
    )(w)


def _scatter_chips(g4):
    def body(g_ref, o_ref, *sems):
        sc = _Scatter(g_ref, o_ref, *sems)
        sc.start()
        sc.finish()

    return pl.pallas_call(
        body, name="scatter_grads",
        out_shape=jax.ShapeDtypeStruct(g4.shape, g4.dtype),
        in_specs=[ANY], out_specs=ANY, scratch_shapes=_Scatter.SCRATCH,
    )(g4)


def _swap_cores(p):
    def body(p_ref, o_ref, send_sem, recv_sem):
        x, y, c = lax.axis_index("x"), lax.axis_index("y"), lax.axis_index("c")
        cp = pltpu.make_async_remote_copy(src_ref=p_ref, dst_ref=o_ref, send_sem=send_sem, recv_sem=recv_sem,
                                          device_id=(x, y, 1 - c), device_id_type=MESH)
        cp.start()
        cp.wait()

    return pl.pallas_call(
        body, name="swap_cores",
        out_shape=jax.ShapeDtypeStruct(p.shape, p.dtype),
        in_specs=[ANY], out_specs=ANY,
        scratch_shapes=[pltpu.SemaphoreType.DMA, pltpu.SemaphoreType.DMA],
    )(p)


def _sum4(r4):
    _, R, C = r4.shape

    def body(r_ref, o_ref):
        o_ref[...] = ((r_ref[0].astype(F32) + r_ref[1].astype(F32)) + r_ref[2].astype(F32)) + r_ref[3].astype(F32)

    return pl.pallas_call(
        body, name="sum_chips", grid=(R // PACK_TILE,),
        in_specs=[pl.BlockSpec((N_CHIPS, PACK_TILE, C), lambda i: (0, i, 0))],
        out_specs=pl.BlockSpec((PACK_TILE, C), lambda i: (i, 0)),
        out_shape=jax.ShapeDtypeStruct((R, C), F32),
        compiler_params=_params(("parallel",)),
    )(r4)


def _adamw(pa, pb, w, m, v):
    R, C = w.shape
    c1 = 1.0 - ADAM_B1
    c2 = 1.0 - ADAM_B2
    bc1 = 1.0 - ADAM_B1 ** ADAM_STEP
    bc2 = 1.0 - ADAM_B2 ** ADAM_STEP

    def body(pa_ref, pb_ref, w_ref, m_ref, v_ref, g_ref, d_ref, mo_ref, vo_ref):
        g = pa_ref[...] + pb_ref[...]
        mn = ADAM_B1 * m_ref[...] + c1 * g
        vn = ADAM_B2 * v_ref[...] + c2 * (g * g)
        m_hat = mn / bc1
        v_hat = vn / bc2
        g_ref[...] = g
        d_ref[...] = -ADAM_LR * (m_hat / (jnp.sqrt(v_hat) + ADAM_EPS) + ADAM_WD * w_ref[...])
        mo_ref[...] = mn
        vo_ref[...] = vn

    spec = pl.BlockSpec((PACK_TILE, C), lambda i: (i, 0))
    shp = jax.ShapeDtypeStruct((R, C), F32)
    return pl.pallas_call(
        body, name="adamw", grid=(R // PACK_TILE,),
        in_specs=[spec] * 5, out_specs=[spec] * 4, out_shape=[shp] * 4,
        compiler_params=_params(("parallel",)),
    )(pa, pb, w, m, v)


def _rms_fwd(x, g, name):
    R, Dm = x.shape
    tr = _pick(R, (512, 256, 128))

    def body(x_ref, g_ref, o_ref):
        xf = x_ref[...]
        r = lax.rsqrt(jnp.mean(xf * xf, axis=-1, keepdims=True) + EPS)
        o_ref[...] = (xf * r * g_ref[...]).astype(o_ref.dtype)

    return pl.pallas_call(
        body, name=name, grid=(R // tr,),
        in_specs=[pl.BlockSpec((tr, Dm), lambda i: (i, 0)), pl.BlockSpec((1, Dm), lambda i: (0, 0))],
        out_specs=pl.BlockSpec((tr, Dm), lambda i: (i, 0)),
        out_shape=jax.ShapeDtypeStruct((R, Dm), BF16),
        compiler_params=_params(("parallel",)),
    )(x, g)


def _rms_bwd(x, g, dy, dres, name):
    R, Dm = x.shape
    tr = _pick(R, (256, 128))
    has_res = dres is not None

    def body(*refs):
        if has_res:
            x_ref, g_ref, dy_ref, dres_ref, dx_ref, dg_ref = refs
        else:
            x_ref, g_ref, dy_ref, dx_ref, dg_ref = refs
        xf = x_ref[...]
        dy_ = dy_ref[...].astype(F32)
        r = lax.rsqrt(jnp.mean(xf * xf, axis=-1, keepdims=True) + EPS)
        gdy = dy_ * g_ref[...]
        mdot = jnp.mean(xf * gdy, axis=-1, keepdims=True)
        dx = r * gdy - xf * ((r * r * r) * mdot)
        if has_res:
            dx = dres_ref[...] + dx
        dx_ref[...] = dx

        @pl.when(pl.program_id(0) == 0)
        def _():
            dg_ref[...] = jnp.zeros_like(dg_ref)

        dg_ref[...] += jnp.sum(dy_ * (xf * r), axis=0, keepdims=True)

    row = pl.BlockSpec((tr, Dm), lambda i: (i, 0))
    vec = pl.BlockSpec((1, Dm), lambda i: (0, 0))
    ins = [x, g, dy] + ([dres] if has_res else [])
    return pl.pallas_call(
        body, name=name, grid=(R // tr,),
        in_specs=[row, vec, row] + ([row] if has_res else []),
        out_specs=[row, vec],
        out_shape=[jax.ShapeDtypeStruct((R, Dm), F32), jax.ShapeDtypeStruct((1, Dm), F32)],
        compiler_params=_params(("arbitrary",)),
    )(*ins)


def _final_loss(x, g, tgt):
    R, Dm = x.shape
    tr = _pick(R, (256, 128))

    def body(x_ref, g_ref, t_ref, l_ref, dx_ref, dg_ref):
        xf = x_ref[...]
        gv = g_ref[...]
        r = lax.rsqrt(jnp.mean(xf * xf, axis=-1, keepdims=True) + EPS)
        xr = xf * r
        err = xr * gv - t_ref[...]
        dy_ = err * (1.0 / Dm)
        gdy = dy_ * gv
        mdot = jnp.mean(xf * gdy, axis=-1, keepdims=True)
        dx_ref[...] = r * gdy - xf * ((r * r * r) * mdot)

        @pl.when(pl.program_id(0) == 0)
        def _():
            dg_ref[...] = jnp.zeros_like(dg_ref)
            l_ref[...] = jnp.zeros_like(l_ref)

        dg_ref[...] += jnp.sum(dy_ * xr, axis=0, keepdims=True)
        sq = jnp.sum(err * err, axis=1, keepdims=True)
        l_ref[...] += jnp.sum(sq, axis=0, keepdims=True) * (0.5 / Dm)

    row = pl.BlockSpec((tr, Dm), lambda i: (i, 0))
    vec = pl.BlockSpec((1, Dm), lambda i: (0, 0))
    return pl.pallas_call(
        body, name="final_norm_loss", grid=(R // tr,),
        in_specs=[row, vec, row],
        out_specs=[pl.BlockSpec((1, 1), lambda i: (0, 0)), row, vec],
        out_shape=[jax.ShapeDtypeStruct((1, 1), F32), jax.ShapeDtypeStruct((R, Dm), F32),
                   jax.ShapeDtypeStruct((1, Dm), F32)],
        compiler_params=_params(("arbitrary",)),
    )(x, g, tgt)


MAX_TK = 2048

_DIMS = {"nn": (((1,), (0,)), ((), ())), "nt": (((1,), (1,)), ((), ())), "tn": (((0,), (0,)), ((), ()))}


def _mm(a, b, *, mode="nn", out_dtype=BF16, epi=None, extra=None, name):
    if mode == "nn":
        (M, K), N = a.shape, b.shape[1]
    elif mode == "nt":
        (M, K), N = a.shape, b.shape[0]
    else:
        (K, M), N = a.shape, b.shape[1]
    tm = _pick(M, (512, 256, 128) if epi == "rms_bwd" else (1024, 768, 512, 256, 128))
    tn = _pick(N, (1024, 896, 768, 640, 512, 384, 256, 128))
    tk = K if K <= MAX_TK else _pick(K, (MAX_TK, 1024, 512, 256, 128))
    nk = K // tk
    extras = () if extra is None else (extra if isinstance(extra, tuple) else (extra,))
    n_out = 2 if epi in ("relu2", "rms_bwd") else 1
    assert epi != "rms_bwd" or tn == N

    def body(*refs):
        a_ref, b_ref = refs[:2]
        e_refs = refs[2:2 + len(extras)]
        e_ref = e_refs[0] if e_refs else None
        outs = refs[2 + len(extras):2 + len(extras) + n_out]
        k = pl.program_id(2)
        part = lax.dot_general(a_ref[...].astype(BF16), b_ref[...].astype(BF16), _DIMS[mode],
                               preferred_element_type=F32)

        def finish(acc):
            if epi is None:
                outs[0][...] = acc.astype(outs[0].dtype)
            elif epi == "add":
                outs[0][...] = (e_ref[...] + acc).astype(outs[0].dtype)
            elif epi == "relu2":
                outs[0][...] = acc.astype(BF16)
                rl = jnp.maximum(acc, 0.0)
                outs[1][...] = (rl * rl).astype(BF16)
            elif epi == "drelu2":
                u = e_ref[...].astype(F32)
                outs[0][...] = (acc * (2.0 * jnp.maximum(u, 0.0))).astype(outs[0].dtype)
            elif epi == "rms_bwd":
                x_ref, g_ref, dres_ref = e_refs
                xf = x_ref[...]
                r = lax.rsqrt(jnp.mean(xf * xf, axis=-1, keepdims=True) + EPS)
                gdy = acc * g_ref[...]
                mdot = jnp.mean(xf * gdy, axis=-1, keepdims=True)
                outs[0][...] = dres_ref[...] + (r * gdy - xf * ((r * r * r) * mdot))

                @pl.when(pl.program_id(0) == 0)
                def _():
                    outs[1][...] = jnp.zeros_like(outs[1])

                outs[1][...] += jnp.sum(acc * (xf * r), axis=0, keepdims=True)

        if nk == 1:
            finish(part)
        else:
            acc_ref = refs[-1]

            @pl.when(k == 0)
            def _():
                acc_ref[...] = part

            @pl.when(jnp.logical_and(k > 0, k < nk - 1))
            def _():
                acc_ref[...] += part

            @pl.when(k == nk - 1)
            def _():
                finish(acc_ref[...] + part)

    if mode == "tn":
        a_spec = pl.BlockSpec((tk, tm), lambda i, j, k: (k, i))
    else:
        a_spec = pl.BlockSpec((tm, tk), lambda i, j, k: (i, k))
    if mode == "nt":
        b_spec = pl.BlockSpec((tn, tk), lambda i, j, k: (j, k))
    else:
        b_spec = pl.BlockSpec((tk, tn), lambda i, j, k: (k, j))
    o_spec = pl.BlockSpec((tm, tn), lambda i, j, k: (i, j))
    vec_spec = pl.BlockSpec((1, tn), lambda i, j, k: (0, j))
    ins, in_specs = [a, b] + list(extras), [a_spec, b_spec]
    if epi == "rms_bwd":
        in_specs += [o_spec, vec_spec, o_spec]
        out_shape = [jax.ShapeDtypeStruct((M, N), F32), jax.ShapeDtypeStruct((1, N), F32)]
        out_specs = [o_spec, vec_spec]
    else:
        in_specs += [o_spec] * len(extras)
        out_shape = [jax.ShapeDtypeStruct((M, N), BF16 if epi == "relu2" else out_dtype)] * n_out
        out_specs = [o_spec] * n_out
    res = pl.pallas_call(
        body, name=name, grid=(M // tm, N // tn, nk),
        in_specs=in_specs, out_specs=out_specs, out_shape=out_shape,
        scratch_shapes=[pltpu.VMEM((tm, tn), F32)] if nk > 1 else [],
        compiler_params=_params(("arbitrary",) * 3 if epi == "rms_bwd" else ("parallel", "parallel", "arbitrary")),
    )(*ins)
    return res if n_out > 1 else res[0]


def _dot(a, b):
    return lax.dot_general(a, b, _DIMS["nn"], preferred_element_type=F32)


def _dot_nt(a, b):
    return lax.dot_general(a, b, _DIMS["nt"], preferred_element_type=F32)


def _dot_tn(a, b):
    return lax.dot_general(a, b, _DIMS["tn"], preferred_element_type=F32)


def _split_dot(x, t):
    hi = x.astype(BF16)
    lo = (x - hi.astype(F32)).astype(BF16)
    return _dot(jnp.concatenate([hi, lo], axis=1), jnp.concatenate([t, t], axis=0))


def _head_pair(ref, scale=None):
    xf = ref[...].astype(F32)
    if scale is not None:
        xf = xf * scale
    is_a = lax.broadcasted_iota(jnp.int32, xf.shape, 1) < HEAD_DIM
    return jnp.where(is_a, xf, 0.0).astype(BF16), jnp.where(is_a, 0.0, xf).astype(BF16)


def _stack(a, b):
    return jnp.concatenate([a, b], axis=0)


def _head_rows(ref, scale=None):
    return _stack(*_head_pair(ref, scale))


def _unstack_heads(x):
    rows = x.shape[0] // 2
    return _select_pair(x[:rows], x[rows:])


def _pair_rowsum(x):
    is_a = lax.broadcasted_iota(jnp.int32, x.shape, 1) < HEAD_DIM
    return (jnp.sum(jnp.where(is_a, x, 0.0), axis=1, keepdims=True),
            jnp.sum(jnp.where(is_a, 0.0, x), axis=1, keepdims=True))


def _select_pair(xa, xb):
    is_a = lax.broadcasted_iota(jnp.int32, xa.shape, 1) < HEAD_DIM
    return jnp.where(is_a, xa, xb)


def _two_cols(xa, xb):
    rows = xa.shape[0]
    first = lax.broadcasted_iota(jnp.int32, (rows, 2), 1) == 0
    return jnp.where(first, xa, xb)


def _softplus_parts(z):
    e = jnp.exp(-jnp.abs(z))
    return jnp.maximum(z, 0.0) + jnp.log(1.0 + e), e


def _tile_iotas():
    row = lax.broadcasted_iota(jnp.int32, (BK, BK), 0)
    col = lax.broadcasted_iota(jnp.int32, (BK, BK), 1)
    return row, col


def _stacked_iotas(bq, nk):
    row = lax.broadcasted_iota(jnp.int32, (2 * bq, nk), 0) & (bq - 1)
    col = lax.broadcasted_iota(jnp.int32, (2 * bq, nk), 1)
    return row, col


def _side_exchange(exchange, operand, n_in, n_out):
    if exchange is None:
        return [], [], [], [], lambda refs: (lambda: None, lambda: None)
    out_shape = jax.ShapeDtypeStruct(((N_CHIPS,) + operand.shape) if exchange is _AllGather else operand.shape,
                                     operand.dtype)
    n_sem = len(exchange.SCRATCH)

    def pick(refs):
        def make():
            return exchange(refs[n_in], refs[n_in + 1 + n_out], *refs[len(refs) - n_sem:])

        return (lambda: make().start()), (lambda: make().finish())

    return [operand], [ANY], [out_shape], [ANY], pick


def _sb_fwd(proj, name, exchange=None, operand=None):
    S = proj.shape[0]
    nqb = S // (CHAINS * BQ)
    x_in, x_in_specs, x_out, x_out_specs, pick = _side_exchange(exchange, operand, 3, 1)

    def body(*refs):
        q_ref, k_ref, v_ref = refs[:3]
        o_ref = refs[3 + len(x_in)]
        acc_ref = refs[3 + len(x_in) + 1 + len(x_out)]
        start, finish = pick(refs)
        p = pl.program_id(0)
        i = pl.program_id(1)

        @pl.when(jnp.logical_and(p == 0, i == 0))
        def _():
            start()

        q2 = [_head_rows(q_ref.at[pl.ds(ch * BQ, BQ)], SCALE) for ch in range(CHAINS)]
        row, col = _tile_iotas()
        tri = (row > col).astype(BF16)
        srow, scol = _stacked_iotas(BQ, BK)
        acc_ref[...] = jnp.zeros_like(acc_ref)

        def tile(ch, kb, c, dmask=None, valid=None):
            r0 = pl.multiple_of(kb * BK, BK)
            kblk = k_ref[pl.ds(r0, BK), :]
            vblk = v_ref[pl.ds(r0, BK), :]
            z = _dot_nt(q2[ch], kblk)
            sp, _ = _softplus_parts(z)
            lm = -sp
            if dmask is not None:
                lm = jnp.where(dmask, lm, 0.0)
            btw = _split_dot(lm, tri)
            w = jnp.exp((z - sp) + btw + c)
            if dmask is not None:
                w = jnp.where(dmask, w, 0.0)
            if valid is not None:
                w = w * valid
            acc_ref[ch] += _dot(w.astype(BF16), vblk)
            return c + btw[:, 0:1] + lm[:, 0:1]

        def alive(c):
            return jnp.max(c) > UNDERFLOW_BOUND

        cs = [jnp.zeros((2 * BQ, 1), F32)] * CHAINS
        for d in reversed(range(DIAG_TILES)):
            cs = [tile(ch, (CHAINS * i + ch) * DIAG_TILES + d, cs[ch], dmask=scol < srow - d * BK)
                  for ch in range(CHAINS)]

        def tile_of(ch, t):
            return (CHAINS * i + ch) * DIAG_TILES - 1 - t

        def more(cs, t):
            go = [jnp.logical_and(alive(cs[ch]), tile_of(ch, t) >= 0) for ch in range(CHAINS)]
            return functools.reduce(jnp.logical_or, go).astype(jnp.int32)

        def step(st):
            t, _, cs = st
            new = []
            for ch in range(CHAINS):
                kb = tile_of(ch, t)
                if ch == CHAINS - 1:
                    new.append(tile(ch, kb, cs[ch]))
                else:
                    new.append(tile(ch, jnp.maximum(kb, 0), cs[ch], valid=(kb >= 0).astype(F32)))
            return t + 1, more(new, t + 1), new

        lax.while_loop(lambda st: st[1] > 0, step, (0, more(cs, 0), cs))
        for ch in range(CHAINS):
            o_ref[pl.ds(ch * BQ, BQ), :] = _unstack_heads(acc_ref[ch])

        @pl.when(jnp.logical_and(p == 3, i == nqb - 1))
        def _():
            finish()

    blk = pl.BlockSpec((CHAINS * BQ, LANES), lambda p, i: (i, p))
    res = pl.pallas_call(
        body, name=name, grid=(4, nqb),
        in_specs=[blk, pl.BlockSpec((S, LANES), lambda p, i: (0, 4 + p)),
                  pl.BlockSpec((S, LANES), lambda p, i: (0, 8 + p))] + x_in_specs,
        out_specs=[blk] + x_out_specs,
        out_shape=[jax.ShapeDtypeStruct((S, MERGED_WIDTH), F32)] + x_out,
        scratch_shapes=[pltpu.VMEM((CHAINS, 2 * BQ, LANES), F32)] + ([] if exchange is None else exchange.SCRATCH),
        compiler_params=_params(("arbitrary", "arbitrary")),
    )(proj, proj, proj, *x_in)
    return res if exchange is not None else res[0]


def _sb_bwd(proj, merged, dmerged, name, exchange=None, operand=None):
    S = proj.shape[0]
    nqb = S // (CHAINS * BQ)
    x_in, x_in_specs, x_out, x_out_specs, pick = _side_exchange(exchange, operand, 5, 3)

    def body(*refs):
        q_ref, k_ref, v_ref, o_ref, do_ref = refs[:5]
        dq_ref, dk_hbm, dv_hbm = refs[5 + len(x_in):8 + len(x_in)]
        dq_acc, dk_acc, dv_acc, sem = refs[8 + len(x_in) + len(x_out):12 + len(x_in) + len(x_out)]
        start, finish = pick(refs)
        p = pl.program_id(0)
        i = pl.program_id(1)

        @pl.when(jnp.logical_and(p == 0, i == 0))
        def _():
            start()

        @pl.when(i == 0)
        def _():
            dk_acc[...] = jnp.zeros_like(dk_acc)
            dv_acc[...] = jnp.zeros_like(dv_acc)

        rows = [pl.ds(ch * BQ, BQ) for ch in range(CHAINS)]
        q2 = [_head_rows(q_ref.at[rw], SCALE) for rw in rows]
        do2 = [_head_rows(do_ref.at[rw]) for rw in rows]
        tot = [_stack(*_pair_rowsum(do_ref[rw, :].astype(F32) * o_ref[rw, :])) for rw in rows]
        row, col = _tile_iotas()
        tri_gt = (row > col).astype(BF16)
        tri_ge = (row >= col).astype(BF16)
        srow, scol = _stacked_iotas(BQ, BK)
        dq_acc[...] = jnp.zeros_like(dq_acc)

        def tile(ch, kb, st, dmask=None, valid=None):
            masked = dmask is not None
            c, r = st
            r0 = pl.multiple_of(kb * BK, BK)
            kblk = k_ref[pl.ds(r0, BK), :]
            vblk = v_ref[pl.ds(r0, BK), :]
            z = _dot_nt(q2[ch], kblk)
            sp, e = _softplus_parts(z)
            lm = -sp
            if masked:
                lm = jnp.where(dmask, lm, 0.0)
            btw = _split_dot(lm, tri_gt)
            w = jnp.exp((z - sp) + btw + c)
            if masked:
                w = jnp.where(dmask, w, 0.0)
            if valid is not None:
                w = w * valid
            wb = w.astype(BF16)
            a = wb.astype(F32) * _dot_nt(do2[ch], vblk)
            suffix = _split_dot(a, tri_ge) + r
            rcp = 1.0 / (1.0 + e)
            pos = z >= 0.0
            sig = jnp.where(pos, rcp, e * rcp)
            sig_neg = jnp.where(pos, e * rcp, rcp)
            dz = a * sig_neg - (tot[ch] - suffix) * sig
            if masked:
                dz = jnp.where(dmask, dz, 0.0)
            if valid is not None:
                dz = dz * valid
            dzb = dz.astype(BF16)
            dq_acc[ch] += _dot(dzb, kblk)
            dk_acc[pl.ds(r0, BK), :] += _dot_tn(dzb, q2[ch])
            dv_acc[pl.ds(r0, BK), :] += _dot_tn(wb, do2[ch])
            return c + btw[:, 0:1] + lm[:, 0:1], suffix[:, 0:1]

        def alive(st):
            return jnp.max(st[0]) > UNDERFLOW_BOUND

        zero = jnp.zeros((2 * BQ, 1), F32)
        sts = [(zero, zero)] * CHAINS
        for d in reversed(range(DIAG_TILES)):
            sts = [tile(ch, (CHAINS * i + ch) * DIAG_TILES + d, sts[ch], dmask=scol < srow - d * BK)
                   for ch in range(CHAINS)]

        def tile_of(ch, t):
            return (CHAINS * i + ch) * DIAG_TILES - 1 - t

        def more(sts, t):
            go = [jnp.logical_and(alive(sts[ch]), tile_of(ch, t) >= 0) for ch in range(CHAINS)]
            return functools.reduce(jnp.logical_or, go).astype(jnp.int32)

        def step(s):
            t, _, sts = s
            new = []
            for ch in range(CHAINS):
                kb = tile_of(ch, t)
                if ch == CHAINS - 1:
                    new.append(tile(ch, kb, sts[ch]))
                else:
                    new.append(tile(ch, jnp.maximum(kb, 0), sts[ch], valid=(kb >= 0).astype(F32)))
            return t + 1, more(new, t + 1), new

        lax.while_loop(lambda s: s[1] > 0, step, (0, more(sts, 0), sts))
        for ch in range(CHAINS):
            dq_ref[rows[ch], :] = (_unstack_heads(dq_acc[ch]) * SCALE).astype(dq_ref.dtype)

        @pl.when(i == nqb - 1)
        def _():
            ck = pltpu.make_async_copy(dk_acc, dk_hbm.at[p], sem.at[0])
            cv = pltpu.make_async_copy(dv_acc, dv_hbm.at[p], sem.at[1])
            ck.start()
            cv.start()
            ck.wait()
            cv.wait()

        @pl.when(jnp.logical_and(p == 3, i == nqb - 1))
        def _():
            finish()

    blk = lambda off: pl.BlockSpec((CHAINS * BQ, LANES), lambda p, i: (i, off + p))
    slab = lambda off: pl.BlockSpec((S, LANES), lambda p, i: (0, off + p))
    return pl.pallas_call(
        body, name=name, grid=(4, nqb),
        in_specs=[blk(0), slab(4), slab(8), blk(0), blk(0)] + x_in_specs,
        out_specs=[blk(0), ANY, ANY] + x_out_specs,
        out_shape=[jax.ShapeDtypeStruct((S, MIX_WIDTH), BF16),
                   jax.ShapeDtypeStruct((4, S, LANES), F32), jax.ShapeDtypeStruct((4, S, LANES), F32)] + x_out,
        scratch_shapes=[pltpu.VMEM((CHAINS, 2 * BQ, LANES), F32), pltpu.VMEM((S, LANES), F32),
                        pltpu.VMEM((S, LANES), F32), pltpu.SemaphoreType.DMA((2,))]
        + ([] if exchange is None else exchange.SCRATCH),
        compiler_params=_params(("arbitrary", "arbitrary")),
    )(proj, proj, proj, merged, dmerged, *x_in)


def _key_norm_max(k_ref, knorm_ref, nkb):
    def step(kb, m):
        r0 = pl.multiple_of(kb * BK, BK)
        blk = k_ref[pl.ds(r0, BK), :].astype(F32)
        sa, sb = _pair_rowsum(blk * blk)
        return (jnp.maximum(m[0], jnp.max(sa, axis=0, keepdims=True)),
                jnp.maximum(m[1], jnp.max(sb, axis=0, keepdims=True)))

    zero = jnp.zeros((1, 1), F32)
    ma, mb = lax.fori_loop(0, nkb, step, (zero, zero))
    knorm_ref[...] = _select_pair(jnp.broadcast_to(ma, (1, LANES)), jnp.broadcast_to(mb, (1, LANES)))


FQ = 512
FK = FQ
GATE_BLOCKS = FK // BK


def _key_gates(cr_ref, kb):
    blocks = [cr_ref[0, GATE_BLOCKS * kb + j] for j in range(GATE_BLOCKS)]
    per_head = [jnp.broadcast_to(jnp.concatenate([b[h:h + 1] for b in blocks], axis=1), (FQ, FK)) for h in range(2)]
    return _stack(*per_head)


def _last_gate(cr_ref, kb):
    last = cr_ref[0, GATE_BLOCKS * jnp.maximum(kb, 0) + GATE_BLOCKS - 1]
    return _stack(*[jnp.broadcast_to(last[h:h + 1, BK - 1:BK], (FQ, 1)) for h in range(2)])


def _logit_bound(q_ref, knorm_ref):
    qf = q_ref[...].astype(F32) * SCALE
    qa, qb = _pair_rowsum(qf * qf)
    kn = knorm_ref[...]
    return _stack(jnp.sqrt(qa * kn[:, 0:1]), jnp.sqrt(qb * kn[:, HEAD_DIM:HEAD_DIM + 1]))


def _causal_bias(bias_ref):
    srow, scol = _stacked_iotas(FQ, FK)
    bias_ref[...] = jnp.where(scol <= srow, 0.0, NEG_INF)


def _fox_fwd(proj, kv, c_col, c_row, name):
    S = proj.shape[0]
    nqb = S // FQ

    def body(q_ref, k_ref, v_ref, cc_ref, cr_ref, o_ref, lse_ref, acc_ref, knorm_ref, bias_ref):
        i = pl.program_id(1)

        @pl.when(i == 0)
        def _():
            _key_norm_max(k_ref, knorm_ref, S // BK)
            _causal_bias(bias_ref)

        q2 = _head_rows(q_ref, SCALE)
        bound = _logit_bound(q_ref, knorm_ref)
        cc = cc_ref[0]
        ct = _stack(cc[:, 0:1], cc[:, 1:2])
        acc_ref[...] = jnp.zeros_like(acc_ref)

        def tile(kb, st, masked):
            m, l = st
            r0 = pl.multiple_of(kb * FK, FK)
            kblk = k_ref[pl.ds(r0, FK), :]
            vblk = v_ref[pl.ds(r0, FK), :]
            z = _dot_nt(q2, kblk) + ct - _key_gates(cr_ref, kb)
            if masked:
                z = z + bias_ref[...]
            m_new = jnp.maximum(m, jnp.max(z, axis=1, keepdims=True))
            alpha = jnp.exp(m - m_new)
            pr = jnp.exp(z - m_new)
            acc_ref[...] = alpha * acc_ref[...] + _split_dot(pr, vblk)
            return m_new, alpha * l + jnp.sum(pr, axis=1, keepdims=True)

        def alive(kb, st):
            reach = bound + ct - _last_gate(cr_ref, kb) - st[0]
            return (jnp.max(reach) > UNDERFLOW_BOUND).astype(jnp.int32)

        neg = jnp.full((2 * FQ, 1), NEG_INF, F32)
        zero = jnp.zeros((2 * FQ, 1), F32)
        st0 = tile(i, (neg, zero), True)

        def cond(s):
            return jnp.logical_and(s[0] >= 0, s[1] > 0)

        def step(s):
            kb, _, st = s
            st = tile(kb, st, False)
            return kb - 1, alive(kb - 1, st), st

        _, _, (m, l) = lax.while_loop(cond, step, (i - 1, alive(i - 1, st0), st0))
        o_ref[...] = _unstack_heads(acc_ref[...] / l)
        lse = m + jnp.log(l)
        lse_ref[0] = _two_cols(lse[:FQ], lse[FQ:])

    return pl.pallas_call(
        body, name=name, grid=(4, nqb),
        in_specs=[pl.BlockSpec((FQ, LANES), lambda p, i: (i, p)),
                  pl.BlockSpec((S, LANES), lambda p, i: (0, p)),
                  pl.BlockSpec((S, LANES), lambda p, i: (0, 4 + p)),
                  pl.BlockSpec((1, FQ, 2), lambda p, i: (p, i, 0)),
                  pl.BlockSpec((1, S // BK, 8, LANES), lambda p, i: (p, 0, 0, 0))],
        out_specs=[pl.BlockSpec((FQ, LANES), lambda p, i: (i, p)),
                   pl.BlockSpec((1, FQ, 2), lambda p, i: (p, i, 0))],
        out_shape=[jax.ShapeDtypeStruct((S, MERGED_WIDTH), F32), jax.ShapeDtypeStruct((4, S, 2), F32)],
        scratch_shapes=[pltpu.VMEM((2 * FQ, LANES), F32), pltpu.VMEM((1, LANES), F32),
                        pltpu.VMEM((2 * FQ, FK), F32)],
        compiler_params=_params(("arbitrary", "arbitrary")),
    )(proj, kv, kv, c_col, c_row)


def _fox_bwd(proj, kv, c_col, c_row, lse, merged, dmerged, dk_prev, dv_prev, dc_prev, name):
    S = proj.shape[0]
    nqb = S // FQ

    def body(q_ref, k_ref, v_ref, cc_ref, cr_ref, lse_ref, o_ref, do_ref, dkp_hbm, dvp_hbm, dcp_ref,
             dq_ref, dk_hbm, dv_hbm, dc_ref, dq_acc, dk_acc, dv_acc, knorm_ref, bias_ref, sem):
        p = pl.program_id(0)
        i = pl.program_id(1)

        @pl.when(i == 0)
        def _():
            ck = pltpu.make_async_copy(dkp_hbm.at[p], dk_acc, sem.at[0])
            cv = pltpu.make_async_copy(dvp_hbm.at[p], dv_acc, sem.at[1])
            ck.start()
            cv.start()
            dc_ref[...] = dcp_ref[...]
            _key_norm_max(k_ref, knorm_ref, S // BK)
            _causal_bias(bias_ref)
            ck.wait()
            cv.wait()

        q2 = _head_rows(q_ref, SCALE)
        do2 = _head_rows(do_ref)
        tot = _stack(*_pair_rowsum(do_ref[...].astype(F32) * o_ref[...]))
        bound = _logit_bound(q_ref, knorm_ref)
        cc = cc_ref[0]
        ct = _stack(cc[:, 0:1], cc[:, 1:2])
        ls = lse_ref[0]
        lse = _stack(ls[:, 0:1], ls[:, 1:2])
        sub = lax.broadcasted_iota(jnp.int32, (8, LANES), 0)
        dq_acc[...] = jnp.zeros_like(dq_acc)

        def tile(kb, masked):
            r0 = pl.multiple_of(kb * FK, FK)
            kblk = k_ref[pl.ds(r0, FK), :]
            vblk = v_ref[pl.ds(r0, FK), :]
            z = _dot_nt(q2, kblk) + ct - _key_gates(cr_ref, kb)
            if masked:
                z = z + bias_ref[...]
            pr = jnp.exp(z - lse)
            ds = pr * (_dot_nt(do2, vblk) - tot)
            dsb = ds.astype(BF16)
            dq_acc[...] += _dot(dsb, kblk)
            dk_acc[pl.ds(r0, FK), :] += _dot_tn(dsb, q2)
            dv_acc[pl.ds(r0, FK), :] += _dot_tn(pr.astype(BF16), do2)
            dca = jnp.sum(ds[:FQ], axis=0, keepdims=True)
            dcb = jnp.sum(ds[FQ:], axis=0, keepdims=True)
            for j in range(GATE_BLOCKS):
                cols = slice(j * BK, (j + 1) * BK)
                old = dc_ref[0, GATE_BLOCKS * kb + j]
                dc_ref[0, GATE_BLOCKS * kb + j] = jnp.where(sub == 0, old - dca[:, cols],
                                                            jnp.where(sub == 1, old - dcb[:, cols], old))

        def alive(kb):
            reach = bound + ct - _last_gate(cr_ref, kb) - lse
            return (jnp.max(reach) > UNDERFLOW_BOUND).astype(jnp.int32)

        tile(i, True)

        def cond(s):
            return jnp.logical_and(s[0] >= 0, s[1] > 0)

        def step(s):
            kb, _ = s
            tile(kb, False)
            return kb - 1, alive(kb - 1)

        lax.while_loop(cond, step, (i - 1, alive(i - 1)))
        dq_ref[...] = (_unstack_heads(dq_acc[...]) * SCALE).astype(dq_ref.dtype)

        @pl.when(i == nqb - 1)
        def _():
            ck = pltpu.make_async_copy(dk_acc, dk_hbm.at[p], sem.at[0])
            cv = pltpu.make_async_copy(dv_acc, dv_hbm.at[p], sem.at[1])
            ck.start()
            cv.start()
            ck.wait()
            cv.wait()

    blk = lambda off: pl.BlockSpec((FQ, LANES), lambda p, i: (i, off + p))
    slab = lambda off: pl.BlockSpec((S, LANES), lambda p, i: (0, off + p))
    cols = pl.BlockSpec((1, FQ, 2), lambda p, i: (p, i, 0))
    rows = pl.BlockSpec((1, S // BK, 8, LANES), lambda p, i: (p, 0, 0, 0))
    return pl.pallas_call(
        body, name=name, grid=(4, nqb),
        in_specs=[blk(0), slab(0), slab(4), cols, rows, cols, blk(0), blk(0), ANY, ANY, rows],
        out_specs=[blk(0), ANY, ANY, rows],
        out_shape=[jax.ShapeDtypeStruct((S, MIX_WIDTH), BF16),
                   jax.ShapeDtypeStruct((4, S, LANES), F32), jax.ShapeDtypeStruct((4, S, LANES), F32),
                   jax.ShapeDtypeStruct((4, S // BK, 8, LANES), F32)],
        scratch_shapes=[pltpu.VMEM((2 * FQ, LANES), F32), pltpu.VMEM((S, LANES), F32),
                        pltpu.VMEM((S, LANES), F32), pltpu.VMEM((1, LANES), F32),
                        pltpu.VMEM((2 * FQ, FK), F32), pltpu.SemaphoreType.DMA((2,))],
        compiler_params=_params(("arbitrary", "arbitrary")),
    )(proj, kv, kv, c_col, c_row, lse, merged, dmerged, dk_prev, dv_prev, dc_prev)


def _lane_scan(x, reverse):
    lane = lax.broadcasted_iota(jnp.int32, x.shape, 1)
    d = 1
    while d < LANES:
        if reverse:
            x = x + jnp.where(lane < LANES - d, pltpu.roll(x, LANES - d, 1), 0.0)
        else:
            x = x + jnp.where(lane >= d, pltpu.roll(x, d, 1), 0.0)
        d *= 2
    return x


def _gate_fwd(fl3, b8):
    nb = fl3.shape[0]

    def body(fl_ref, b_ref, c_ref):
        def step(kb, carry):
            x = fl_ref[kb] + b_ref[...]
            sp, _ = _softplus_parts(-x)
            c = _lane_scan(-sp, False) + carry
            c_ref[kb] = c
            return c[:, LANES - 1:LANES]

        lax.fori_loop(0, nb, step, jnp.zeros((8, 1), F32))

    return pl.pallas_call(body, name="forget_gate_cumsum",
                          out_shape=jax.ShapeDtypeStruct(fl3.shape, F32),
                          compiler_params=_params())(fl3, b8)


def _gate_bwd(dc3, fl3, b8):
    nb = fl3.shape[0]

    def body(dc_ref, fl_ref, b_ref, dfl_ref, db_ref):
        def step(t, st):
            carry, dbs = st
            kb = nb - 1 - t
            g = _lane_scan(dc_ref[kb], True) + carry
            x = fl_ref[kb] + b_ref[...]
            e = jnp.exp(-jnp.abs(x))
            rcp = 1.0 / (1.0 + e)
            dfl = g * jnp.where(x >= 0.0, e * rcp, rcp)
            dfl_ref[kb] = dfl
            return g[:, 0:1], dbs + dfl

        _, dbs = lax.fori_loop(0, nb, step, (jnp.zeros((8, 1), F32), jnp.zeros((8, LANES), F32)))
        db_ref[...] = jnp.broadcast_to(jnp.sum(dbs, axis=1, keepdims=True), (8, LANES))

    return pl.pallas_call(body, name="forget_gate_bwd",
                          out_shape=[jax.ShapeDtypeStruct(fl3.shape, F32), jax.ShapeDtypeStruct((8, LANES), F32)],
                          compiler_params=_params())(dc3, fl3, b8)


MEM_TQ = 512


def _mem_fwd(proj, qcol, mkv, mix, name):
    S = proj.shape[0]
    M = mkv.shape[0]

    def body(q_ref, mk_ref, mv_ref, mix_ref, o_ref, lse_ref):
        q2 = _head_rows(q_ref, SCALE)
        s = _dot_nt(q2, mk_ref[...])
        m = jnp.max(s, axis=1, keepdims=True)
        pr = jnp.exp(s - m)
        l = jnp.sum(pr, axis=1, keepdims=True)
        o_ref[...] = _unstack_heads(_dot(pr.astype(BF16), mv_ref[...]) / l)
        lse = m + jnp.log(l)
        lse_ref[0] = _two_cols(lse[:MEM_TQ], lse[MEM_TQ:])

    return pl.pallas_call(
        body, name=name, grid=(2, S // MEM_TQ),
        in_specs=[pl.BlockSpec((MEM_TQ, LANES), lambda p, i: (i, qcol + p)),
                  pl.BlockSpec((M, LANES), lambda p, i: (0, p)),
                  pl.BlockSpec((M, LANES), lambda p, i: (0, 2 + p)), ANY],
        out_specs=[pl.BlockSpec((MEM_TQ, LANES), lambda p, i: (i, 4 + p)),
                   pl.BlockSpec((1, MEM_TQ, 2), lambda p, i: (p, i, 0))],
        out_shape=[jax.ShapeDtypeStruct((S, MERGED_WIDTH), F32), jax.ShapeDtypeStruct((2, S, 2), F32)],
        input_output_aliases={3: 0},
        compiler_params=_params(("parallel", "parallel")),
    )(proj, mkv, mkv, mix)


def _mem_bwd(proj, qcol, mkv, lse, merged, dmerged, name):
    S = proj.shape[0]
    M = mkv.shape[0]

    def body(q_ref, mk_ref, mv_ref, lse_ref, o_ref, do_ref, dq_ref, dmk_ref, dmv_ref):
        @pl.when(pl.program_id(1) == 0)
        def _():
            dmk_ref[...] = jnp.zeros_like(dmk_ref)
            dmv_ref[...] = jnp.zeros_like(dmv_ref)

        q2 = _head_rows(q_ref, SCALE)
        do2 = _head_rows(do_ref)
        tot = _stack(*_pair_rowsum(do_ref[...].astype(F32) * o_ref[...]))
        ls = lse_ref[0]
        pr = jnp.exp(_dot_nt(q2, mk_ref[...]) - _stack(ls[:, 0:1], ls[:, 1:2]))
        ds = pr * (_dot_nt(do2, mv_ref[...]) - tot)
        dsb = ds.astype(BF16)
        dmk_ref[...] += _dot_tn(dsb, q2)
        dmv_ref[...] += _dot_tn(pr.astype(BF16), do2)
        dq_ref[...] = (_unstack_heads(_dot(dsb, mk_ref[...])) * SCALE).astype(dq_ref.dtype)

    blk = lambda off: pl.BlockSpec((MEM_TQ, LANES), lambda p, i: (i, off + p))
    acc = pl.BlockSpec((M, LANES), lambda p, i: (0, p))
    return pl.pallas_call(
        body, name=name, grid=(2, S // MEM_TQ),
        in_specs=[blk(qcol), pl.BlockSpec((M, LANES), lambda p, i: (0, p)),
                  pl.BlockSpec((M, LANES), lambda p, i: (0, 2 + p)),
                  pl.BlockSpec((1, MEM_TQ, 2), lambda p, i: (p, i, 0)), blk(4), blk(4)],
        out_specs=[blk(0), acc, acc],
        out_shape=[jax.ShapeDtypeStruct((S, MEM_WIDTH), BF16), jax.ShapeDtypeStruct((M, MEM_WIDTH), F32),
                   jax.ShapeDtypeStruct((M, MEM_WIDTH), F32)],
        compiler_params=_params(("parallel", "arbitrary")),
    )(proj, mkv, mkv, lse, merged, dmerged)


def _c_layouts(c3):
    nb = c3.shape[0]
    pairs = c3.reshape(nb, 4, 2, LANES).transpose(1, 0, 2, 3)
    c_row = jnp.pad(pairs, ((0, 0), (0, 0), (0, 6), (0, 0)))
    c_col = pairs.transpose(0, 1, 3, 2).reshape(4, nb * LANES, 2)
    return c_col, c_row


def _local_step(x, mem, wb, rest_shard, sm, loss_target):
    S = x.shape[0]
    nb = S // BK
    vec = lambda a: a.reshape(1, D_MODEL)
    b8 = jnp.broadcast_to(sm["b_f"].reshape(8, 1), (8, LANES))

    saved = []
    shared = None
    h = x
    for l in range(DEPTH):
        if l == N_A:
            w_kvf = jnp.pad(wb["w_kv_shared"], ((0, 0), (0, 1152 - 1032)))
            hs = _rms_fwd(h, vec(sm["kv_norm_g"]), "kv_norm")
            kvf = _mm(hs, w_kvf, out_dtype=F32, name="kv_shared_proj")
            kv = kvf[:, :2 * MIX_WIDTH].astype(BF16)
            fl3 = kvf[:, 2 * MIX_WIDTH:2 * MIX_WIDTH + 8].T.reshape(8, nb, LANES).transpose(1, 0, 2)
            c3 = _gate_fwd(fl3, b8)
            c_col, c_row = _c_layouts(c3)
            shared = dict(h=h, hs=hs, kv=kv, fl3=fl3, c_col=c_col, c_row=c_row)
        hn = _rms_fwd(h, vec(sm["norm1_g"][l]), f"norm1_{l}")
        mn = _rms_fwd(mem, vec(sm["mem_norm_g"][l]), f"mem_norm_{l}")
        mkv = _mm(mn, wb["w_mem_kv"][l], name=f"mem_kv_proj_{l}")
        if l < N_A:
            w_in = wb["w_in_a"][l]
            proj = _mm(hn, w_in, name=f"in_proj_{l}")
            if l == 0:
                mix, gathered = _sb_fwd(proj, f"stickbreak_fwd_{l}", _AllGather, rest_shard)
                _unpack_gathered(PART_REST, gathered, wb)
            else:
                mix = _sb_fwd(proj, f"stickbreak_fwd_{l}")
            lse, qcol = None, 12
        else:
            w_in = wb["w_in_b"][l - N_A]
            proj = _mm(hn, w_in, name=f"in_proj_{l}")
            mix, lse = _fox_fwd(proj, shared["kv"], shared["c_col"], shared["c_row"], f"fox_fwd_{l}")
            qcol = 4
        merged, mlse = _mem_fwd(proj, qcol, mkv, mix, f"mem_attn_fwd_{l}")
        h_mid = _mm(merged, wb["w_o"][l], out_dtype=F32, epi="add", extra=h, name=f"out_proj_{l}")
        hn2 = _rms_fwd(h_mid, vec(sm["norm2_g"][l]), f"norm2_{l}")
        u, act = _mm(hn2, wb["w_mlp1"][l], epi="relu2", name=f"mlp1_{l}")
        h_out = _mm(act, wb["w_mlp2"][l], out_dtype=F32, epi="add", extra=h_mid, name=f"mlp2_{l}")
        saved.append(dict(h=h, hn=hn, mn=mn, mkv=mkv, proj=proj, lse=lse, mlse=mlse, qcol=qcol, merged=merged,
                          h_mid=h_mid, hn2=hn2, u=u, act=act, w_in=w_in))
        h = h_out

    loss, dh, dg_final = _final_loss(h, vec(sm["final_norm_g"]), loss_target)

    gb = {n: [None] * (DEPTH if n not in ("w_in_a", "w_in_b") else 2) for n in
          ("w_in_a", "w_in_b", "w_mem_kv", "w_o", "w_mlp1", "w_mlp2")}
    gs = {n: [None] * DEPTH for n in ("norm1_g", "mem_norm_g", "norm2_g")}
    dk_sh = jnp.zeros((4, S, LANES), F32)
    dv_sh = jnp.zeros((4, S, LANES), F32)
    dc_sh = jnp.zeros((4, nb, 8, LANES), F32)
    for l in reversed(range(DEPTH)):
        sv = saved[l]
        du = _mm(dh, wb["w_mlp2"][l], mode="nt", epi="drelu2", extra=sv["u"], name=f"mlp2_dx_{l}")
        gb["w_mlp2"][l] = _mm(sv["act"], dh, mode="tn", out_dtype=F32, name=f"mlp2_dw_{l}")
        gb["w_mlp1"][l] = _mm(sv["hn2"], du, mode="tn", out_dtype=F32, name=f"mlp1_dw_{l}")
        dh, gs["norm2_g"][l] = _mm(du, wb["w_mlp1"][l], mode="nt", epi="rms_bwd",
                                   extra=(sv["h_mid"], vec(sm["norm2_g"][l]), dh), name=f"mlp1_dx_norm2_bwd_{l}")
        dmerged = _mm(dh, wb["w_o"][l], mode="nt", name=f"out_proj_dx_{l}")
        gb["w_o"][l] = _mm(sv["merged"], dh, mode="tn", out_dtype=F32, name=f"out_proj_dw_{l}")
        if l == 0:
            gb["w_kv_shared"] = g_kvf[:, :1032]
            rest_grads = _pack_grads(PART_REST, ROWS_REST, gb, None)
            dq, dk, dv, rest_received = _sb_bwd(sv["proj"], sv["merged"], dmerged, f"stickbreak_bwd_{l}",
                                                _Scatter, rest_grads)
        elif l < N_A:
            dq, dk, dv = _sb_bwd(sv["proj"], sv["merged"], dmerged, f"stickbreak_bwd_{l}")
        else:
            dq, dk_sh, dv_sh, dc_sh = _fox_bwd(sv["proj"], shared["kv"], shared["c_col"], shared["c_row"],
                                               sv["lse"], sv["merged"], dmerged, dk_sh, dv_sh, dc_sh,
                                               f"fox_bwd_{l}")
        dqm, dmk, dmv = _mem_bwd(sv["proj"], sv["qcol"], sv["mkv"], sv["mlse"], sv["merged"], dmerged,
                                 f"mem_attn_bwd_{l}")
        if l < N_A:
            flat = lambda t: t.transpose(1, 0, 2).reshape(S, MIX_WIDTH).astype(BF16)
            dproj = jnp.concatenate([dq, flat(dk), flat(dv), dqm], axis=1)
        else:
            dproj = jnp.concatenate([dq, dqm], axis=1)
        name_in = "w_in_a" if l < N_A else "w_in_b"
        gb[name_in][l if l < N_A else l - N_A] = _mm(sv["hn"], dproj, mode="tn", out_dtype=F32,
                                                      name=f"in_proj_dw_{l}")
        dh, gs["norm1_g"][l] = _mm(dproj, sv["w_in"], mode="nt", epi="rms_bwd",
                                   extra=(sv["h"], vec(sm["norm1_g"][l]), dh), name=f"in_proj_dx_norm1_bwd_{l}")
        dmkv = jnp.concatenate([dmk, dmv], axis=1)
        gb["w_mem_kv"][l] = _mm(sv["mn"], dmkv, mode="tn", out_dtype=F32, name=f"mem_kv_dw_{l}")
        dmn = _mm(dmkv, wb["w_mem_kv"][l], mode="nt", out_dtype=F32, name=f"mem_kv_dx_{l}")
        _, gs["mem_norm_g"][l] = _rms_bwd(mem, vec(sm["mem_norm_g"][l]), dmn, None, f"mem_norm_bwd_{l}")
        if l == N_A:
            dfl3, db8 = _gate_bwd(dc_sh.reshape(4, nb, 8, LANES)[:, :, :2].transpose(1, 0, 2, 3).reshape(nb, 8, LANES),
                                  shared["fl3"], b8)
            dfl = dfl3.transpose(1, 0, 2).reshape(8, S).T
            flat = lambda t: t.transpose(1, 0, 2).reshape(S, MIX_WIDTH).astype(BF16)
            dkvf = jnp.concatenate([flat(dk_sh), flat(dv_sh),
                                    jnp.pad(dfl, ((0, 0), (0, LANES - 8))).astype(BF16)], axis=1)
            g_kvf = _mm(shared["hs"], dkvf, mode="tn", out_dtype=F32, name="kv_shared_dw")
            dh, g_kvn = _mm(dkvf, w_kvf, mode="nt", epi="rms_bwd",
                            extra=(shared["h"], vec(sm["kv_norm_g"]), dh), name="kv_shared_dx_norm_bwd")
            g_bf = db8[:, 0]

    gsmall = {n: jnp.concatenate(v, axis=0) for n, v in gs.items()}
    gsmall["kv_norm_g"] = g_kvn
    gsmall["final_norm_g"] = dg_final
    gsmall["b_f"] = g_bf
    return loss, dh, gb, gsmall, rest_received


def kernel(x, mem, norm1_g, w_in_a, w_in_b, w_mem_kv, mem_norm_g, w_o, norm2_g, w_mlp1, w_mlp2, kv_norm_g, w_kv_shared, b_f, final_norm_g, loss_target, m_norm1_g, m_w_in_a, m_w_in_b, m_w_mem_kv, m_mem_norm_g, m_w_o, m_norm2_g, m_w_mlp1, m_w_mlp2, m_kv_norm_g, m_w_kv_shared, m_b_f, m_final_norm_g, v_norm1_g, v_w_in_a, v_w_in_b, v_w_mem_kv, v_mem_norm_g, v_w_o, v_norm2_g, v_w_mlp1, v_w_mlp2, v_kv_norm_g, v_w_kv_shared, v_b_f, v_final_norm_g):
    big_w = dict(w_in_a=w_in_a, w_in_b=w_in_b, w_mem_kv=w_mem_kv, w_o=w_o, w_mlp1=w_mlp1, w_mlp2=w_mlp2,
                 w_kv_shared=w_kv_shared)
    small_w = dict(norm1_g=norm1_g, mem_norm_g=mem_norm_g, norm2_g=norm2_g, kv_norm_g=kv_norm_g,
                   final_norm_g=final_norm_g, b_f=b_f)
    big_m = dict(w_in_a=m_w_in_a, w_in_b=m_w_in_b, w_mem_kv=m_w_mem_kv, w_o=m_w_o, w_mlp1=m_w_mlp1,
                 w_mlp2=m_w_mlp2, w_kv_shared=m_w_kv_shared)
    small_m = dict(norm1_g=m_norm1_g, mem_norm_g=m_mem_norm_g, norm2_g=m_norm2_g, kv_norm_g=m_kv_norm_g,
                   final_norm_g=m_final_norm_g, b_f=m_b_f)
    big_v = dict(w_in_a=v_w_in_a, w_in_b=v_w_in_b, w_mem_kv=v_w_mem_kv, w_o=v_w_o, w_mlp1=v_w_mlp1,
                 w_mlp2=v_w_mlp2, w_kv_shared=v_w_kv_shared)
    small_v = dict(norm1_g=v_norm1_g, mem_norm_g=v_mem_norm_g, norm2_g=v_norm2_g, kv_norm_g=v_kv_norm_g,
                   final_norm_g=v_final_norm_g, b_f=v_b_f)

    def pack_all(big, small, dtype):
        return jnp.concatenate([_pack_local(PART_LAYER0, ROWS_LAYER0, big, small, dtype),
                                _pack_local(PART_REST, ROWS_REST, big, None, dtype)], axis=0)

    wb = {n: {} for n in BIG_NAMES}
    _unpack_gathered(PART_LAYER0, _allgather_chips(_pack_local(PART_LAYER0, ROWS_LAYER0, big_w, small_w, BF16)), wb)
    rest_shard = _pack_local(PART_REST, ROWS_REST, big_w, None, BF16)

    loss, dx, gb, gsmall, rest_received = _local_step(x[0], mem[0], wb, rest_shard, small_w, loss_target[0])

    layer0_received = _scatter_chips(_pack_grads(PART_LAYER0, ROWS_LAYER0, gb, gsmall))
    part = jnp.concatenate([_sum4(layer0_received), _sum4(rest_received)], axis=0)
    other = _swap_cores(part)
    g, delta, new_m, new_v = _adamw(part, other, pack_all(big_w, small_w, F32), pack_all(big_m, small_m, F32),
                                    pack_all(big_v, small_v, F32))

    outs = [lax.psum(loss[0, 0], ("x", "y", "c")), dx[None]]
    for packed in (g, delta, new_m, new_v):
        pieces, d = {n: [] for n in BIG_NAMES}, {}
        _unpack_local(PART_LAYER0, packed[:ROWS_LAYER0], True, pieces, d)
        _unpack_local(PART_REST, packed[ROWS_LAYER0:], False, pieces, d)
        d.update(_join_layers(pieces))
        outs.extend(d[n] for n in WEIGHT_ORDER)
    return tuple(outs)
```

```python
import functools
import math

import jax
import jax.numpy as jnp
from jax import lax
from jax.experimental import pallas as pl
from jax.experimental.pallas import tpu as pltpu

F32 = jnp.float32
BF16 = jnp.bfloat16

D_MODEL = 1024
HEAD_DIM = 64
MIX_WIDTH = 512
MEM_WIDTH = 256
MERGED_WIDTH = MIX_WIDTH + MEM_WIDTH
DEPTH = 4
N_A = 2
D_FF = 4096
EPS = 1e-6
NEG_INF = -1e30
SCALE = 1.0 / math.sqrt(HEAD_DIM)

ADAM_LR = 0.001
ADAM_B1 = 0.9
ADAM_B2 = 0.999
ADAM_EPS = 1e-08
ADAM_WD = 0.01
ADAM_STEP = 10

LANES = 128
BQ = 256
BK = 128
DIAG_TILES = BQ // BK
CHAINS = 2
UNDERFLOW_BOUND = -110.0
VMEM_LIMIT = 56 * 1024 * 1024

MESH = pl.DeviceIdType.MESH
N_CHIPS = 4

PART_LAYER0 = (
    ("w_in_a", 0, 1, (1024, 448), 1),
    ("w_mem_kv", 0, 1, (256, 512), 0),
)
PART_REST = (
    ("w_o", 0, 4, (768, 256), 1),
    ("w_mlp1", 0, 4, (1024, 1024), 1),
    ("w_mlp2", 0, 4, (1024, 1024), 0),
    ("w_in_a", 1, 2, (1024, 448), 1),
    ("w_in_b", 0, 2, (256, 768), 0),
    ("w_mem_kv", 1, 4, (256, 512), 0),
    ("w_kv_shared", None, None, (1024, 258), 1),
)
BIG_NAMES = ("w_in_a", "w_in_b", "w_mem_kv", "w_o", "w_mlp1", "w_mlp2", "w_kv_shared")
SMALL = (
    ("norm1_g", (4, 1024)),
    ("mem_norm_g", (4, 1024)),
    ("norm2_g", (4, 1024)),
    ("kv_norm_g", (1, 1024)),
    ("final_norm_g", (1, 1024)),
    ("b_f", (1, 1024)),
)
WEIGHT_ORDER = ("norm1_g", "w_in_a", "w_in_b", "w_mem_kv", "mem_norm_g", "w_o", "norm2_g", "w_mlp1",
                "w_mlp2", "kv_norm_g", "w_kv_shared", "b_f", "final_norm_g")


ROW_ALIGN = 16
PACK_TILE = 256
SMALL_ROWS = ROW_ALIGN
assert sum(s[0] for _, s in SMALL) <= SMALL_ROWS


def _section_rows(entry):
    _, lo, hi, shape, _ = entry
    rows = (1 if lo is None else hi - lo) * math.prod(shape) // D_MODEL
    return rows, -(-rows // ROW_ALIGN) * ROW_ALIGN


def _round_up(n, m):
    return -(-n // m) * m


SUM_TILE = 128
ROWS_LAYER0 = _round_up(sum(_section_rows(e)[1] for e in PART_LAYER0) + SMALL_ROWS, SUM_TILE)
ROWS_REST = _round_up(sum(_section_rows(e)[1] for e in PART_REST) + ROWS_LAYER0, PACK_TILE) - ROWS_LAYER0
assert ROWS_REST % SUM_TILE == 0


def _params(sem=None):
    return pltpu.CompilerParams(dimension_semantics=sem, vmem_limit_bytes=VMEM_LIMIT)


def _pick(n, cands):
    for c in cands:
        if n % c == 0:
            return c
    raise ValueError(f"no tile for {n}")


def _section(a, entry):
    a = a.reshape(-1, D_MODEL)
    return jnp.pad(a, ((0, _section_rows(entry)[1] - a.shape[0]), (0, 0)))


def _small_block(small, dtype):
    blk = jnp.zeros((SMALL_ROWS, D_MODEL), dtype)
    off = 0
    for n, shp in SMALL:
        a = small[n].astype(dtype)
        if n == "b_f":
            blk = blk.at[off, :a.size].set(a.reshape(-1))
        else:
            blk = blk.at[off:off + shp[0]].set(a.reshape(shp))
        off += shp[0]
    return blk


def _fill(parts, rows, dtype):
    used = sum(p.shape[0] for p in parts)
    return jnp.concatenate(parts + [jnp.zeros((rows - used, D_MODEL), dtype)], axis=0)


def _pack_local(part, rows, big, small, dtype):
    parts = [_section((big[e[0]] if e[1] is None else big[e[0]][e[1]:e[2]]).astype(dtype), e) for e in part]
    if small is not None:
        parts.append(_small_block(small, dtype))
    return _fill(parts, rows, dtype)


def _unpack_local(part, p, with_small, pieces, small):
    off = 0
    for e in part:
        n, lo, hi, shp, _ = e
        rows, reserved = _section_rows(e)
        pieces[n].append((lo, p[off:off + rows].reshape(shp if lo is None else (hi - lo,) + shp)))
        off += reserved
    if with_small:
        for n, shp in SMALL:
            a = p[off:off + shp[0]]
            small[n] = a[0, :8] if n == "b_f" else (a.reshape(D_MODEL) if shp[0] == 1 else a)
            off += shp[0]


def _join_layers(pieces):
    out = {}
    for n, ps in pieces.items():
        ps = sorted(ps, key=lambda t: -1 if t[0] is None else t[0])
        out[n] = ps[0][1] if len(ps) == 1 else jnp.concatenate([a for _, a in ps], axis=0)
    return out


def _unpack_gathered(part, g, weights):
    off = 0
    for e in part:
        n, lo, hi, shp, ax = e
        rows, reserved = _section_rows(e)
        if lo is None:
            sec = g[:, off:off + rows].reshape((N_CHIPS,) + shp)
            weights[n] = jnp.concatenate([sec[j] for j in range(N_CHIPS)], axis=ax)
        else:
            sec = g[:, off:off + rows].reshape((N_CHIPS, hi - lo) + shp)
            for l in range(lo, hi):
                weights[n][l] = jnp.concatenate([sec[j, l - lo] for j in range(N_CHIPS)], axis=ax)
        off += reserved


def _pack_grads(part, rows, gbig, gsmall):
    small = None if gsmall is None else _small_block(gsmall, BF16)
    chunks = []
    for j in range(N_CHIPS):
        parts = []
        for e in part:
            n, lo, hi, shp, ax = e
            w = shp[ax]
            layers = [gbig[n]] if lo is None else [gbig[n][l] for l in range(lo, hi)]
            cut = [lax.slice_in_dim(g, j * w, (j + 1) * w, axis=ax).astype(BF16).reshape(-1, D_MODEL) for g in layers]
            parts.append(_section(cut[0] if len(cut) == 1 else jnp.concatenate(cut, axis=0), e))
        if small is not None:
            parts.append(small)
        chunks.append(_fill(parts, rows, BF16))
    return jnp.stack(chunks, axis=0)


ANY = pl.BlockSpec(memory_space=pl.ANY)


def _other_chips(x, y):
    return [(1 - x, y), (x, 1 - y), (1 - x, 1 - y)]


class _AllGather:
    SCRATCH = [pltpu.SemaphoreType.DMA((3,)), pltpu.SemaphoreType.DMA((3,)), pltpu.SemaphoreType.DMA((3,)),
               pltpu.SemaphoreType.DMA((3,)), pltpu.SemaphoreType.DMA]

    def __init__(self, w_ref, o_ref, send_sems, recv_sems, pass_send, pass_recv, local_sem):
        self.w_ref, self.o_ref = w_ref, o_ref
        self.sems = (send_sems, recv_sems, pass_send, pass_recv, local_sem)
        x, y, c = lax.axis_index("x"), lax.axis_index("y"), lax.axis_index("c")
        half = w_ref.shape[0] // 2
        self.c, self.me, self.sibling = c, 2 * x + y, (x, y, 1 - c)
        self.mine = pl.ds(pl.multiple_of(c * half, ROW_ALIGN), half)
        self.other = pl.ds(pl.multiple_of((1 - c) * half, ROW_ALIGN), half)
        self.chips = _other_chips(x, y)

    def _over_ici(self, j, rows_of):
        chip = self.chips[j]
        return pltpu.make_async_remote_copy(
            src_ref=self.w_ref.at[self.mine], dst_ref=self.o_ref.at[rows_of, self.mine],
            send_sem=self.sems[0].at[j], recv_sem=self.sems[1].at[j],
            device_id=(chip[0], chip[1], self.c), device_id_type=MESH)

    def _over_d2d(self, j, rows):
        where = self.o_ref.at[2 * self.chips[j][0] + self.chips[j][1], rows]
        return pltpu.make_async_remote_copy(src_ref=where, dst_ref=where, send_sem=self.sems[2].at[j],
                                            recv_sem=self.sems[3].at[j], device_id=self.sibling,
                                            device_id_type=MESH)

    def _local(self):
        return pltpu.make_async_copy(self.w_ref, self.o_ref.at[self.me], self.sems[4])

    def start(self):
        self._local().start()
        for j in range(3):
            self._over_ici(j, self.me).start()

    def finish(self):
        for j in range(3):
            self._over_ici(j, 2 * self.chips[j][0] + self.chips[j][1]).wait_recv()
            self._over_d2d(j, self.mine).start()
        for j in range(3):
            self._over_d2d(j, self.other).wait_recv()
        for j in range(3):
            self._over_ici(j, self.me).wait_send()
            self._over_d2d(j, self.mine).wait_send()
        self._local().wait()


class _Scatter:
    SCRATCH = [pltpu.SemaphoreType.DMA((3,)), pltpu.SemaphoreType.DMA((3,)), pltpu.SemaphoreType.DMA]

    def __init__(self, g_ref, o_ref, send_sems, recv_sems, local_sem):
        self.g_ref, self.o_ref, self.sems = g_ref, o_ref, (send_sems, recv_sems, local_sem)
        x, y, c = lax.axis_index("x"), lax.axis_index("y"), lax.axis_index("c")
        self.c, self.me, self.chips = c, 2 * x + y, _other_chips(x, y)

    def _copy(self, j):
        chip = self.chips[j]
        return pltpu.make_async_remote_copy(
            src_ref=self.g_ref.at[2 * chip[0] + chip[1]], dst_ref=self.o_ref.at[self.me],
            send_sem=self.sems[0].at[j], recv_sem=self.sems[1].at[j],
            device_id=(chip[0], chip[1], self.c), device_id_type=MESH)

    def _local(self):
        return pltpu.make_async_copy(self.g_ref.at[self.me], self.o_ref.at[self.me], self.sems[2])

    def start(self):
        self._local().start()
        for j in range(3):
            self._copy(j).start()

    def finish(self):
        for j in range(3):
            self._copy(j).wait()
        self._local().wait()


def _allgather_chips(w):
    def body(w_ref, o_ref, *sems):
        ag = _AllGather(w_ref, o_ref, *sems)
        ag.start()
        ag.finish()

    return pl.pallas_call(
        body, name="allgather_weights",
        out_shape=jax.ShapeDtypeStruct((N_CHIPS,) + w.shape, w.dtype),
        in_specs=[ANY], out_specs=ANY, scratch_shapes=_AllGather.SCRATCH,
    )(w)


def _scatter_chips(g4):
    def body(g_ref, o_ref, *sems):
        sc = _Scatter(g_ref, o_ref, *sems)
        sc.start()
        sc.finish()

    return pl.pallas_call(
        body, name="scatter_grads",
        out_shape=jax.ShapeDtypeStruct(g4.shape, g4.dtype),
        in_specs=[ANY], out_specs=ANY, scratch_shapes=_Scatter.SCRATCH,
    )(g4)


def _swap_cores(p):
    def body(p_ref, o_ref, send_sem, recv_sem):
        x, y, c = lax.axis_index("x"), lax.axis_index("y"), lax.axis_index("c")
        cp = pltpu.make_async_remote_copy(src_ref=p_ref, dst_ref=o_ref, send_sem=send_sem, recv_sem=recv_sem,
                                          device_id=(x, y, 1 - c), device_id_type=MESH)
        cp.start()
        cp.wait()

    return pl.pallas_call(
        body, name="swap_cores",
        out_shape=jax.ShapeDtypeStruct(p.shape, p.dtype),
        in_specs=[ANY], out_specs=ANY,
        scratch_shapes=[pltpu.SemaphoreType.DMA, pltpu.SemaphoreType.DMA],
    )(p)


def _sum4(r4):
    _, R, C = r4.shape

    def body(r_ref, o_ref):
        o_ref[...] = ((r_ref[0].astype(F32) + r_ref[1].astype(F32)) + r_ref[2].astype(F32)) + r_ref[3].astype(F32)

    return pl.pallas_call(
        body, name="sum_chips", grid=(R // SUM_TILE,),
        in_specs=[pl.BlockSpec((N_CHIPS, SUM_TILE, C), lambda i: (0, i, 0))],
        out_specs=pl.BlockSpec((SUM_TILE, C), lambda i: (i, 0)),
        out_shape=jax.ShapeDtypeStruct((R, C), F32),
        compiler_params=_params(("parallel",)),
    )(r4)


def _adamw(pa, pb, w, m, v):
    R, C = w.shape
    c1 = 1.0 - ADAM_B1
    c2 = 1.0 - ADAM_B2
    bc1 = 1.0 - ADAM_B1 ** ADAM_STEP
    bc2 = 1.0 - ADAM_B2 ** ADAM_STEP

    def body(pa_ref, pb_ref, w_ref, m_ref, v_ref, g_ref, d_ref, mo_ref, vo_ref):
        g = pa_ref[...] + pb_ref[...]
        mn = ADAM_B1 * m_ref[...] + c1 * g
        vn = ADAM_B2 * v_ref[...] + c2 * (g * g)
        m_hat = mn / bc1
        v_hat = vn / bc2
        g_ref[...] = g
        d_ref[...] = -ADAM_LR * (m_hat / (jnp.sqrt(v_hat) + ADAM_EPS) + ADAM_WD * w_ref[...])
        mo_ref[...] = mn
        vo_ref[...] = vn

    spec = pl.BlockSpec((PACK_TILE, C), lambda i: (i, 0))
    shp = jax.ShapeDtypeStruct((R, C), F32)
    return pl.pallas_call(
        body, name="adamw", grid=(R // PACK_TILE,),
        in_specs=[spec] * 5, out_specs=[spec] * 4, out_shape=[shp] * 4,
        compiler_params=_params(("parallel",)),
    )(pa, pb, w, m, v)


def _rms_fwd(x, g, name):
    R, Dm = x.shape
    tr = _pick(R, (512, 256, 128))

    def body(x_ref, g_ref, o_ref):
        xf = x_ref[...]
        r = lax.rsqrt(jnp.mean(xf * xf, axis=-1, keepdims=True) + EPS)
        o_ref[...] = (xf * r * g_ref[...]).astype(o_ref.dtype)

    return pl.pallas_call(
        body, name=name, grid=(R // tr,),
        in_specs=[pl.BlockSpec((tr, Dm), lambda i: (i, 0)), pl.BlockSpec((1, Dm), lambda i: (0, 0))],
        out_specs=pl.BlockSpec((tr, Dm), lambda i: (i, 0)),
        out_shape=jax.ShapeDtypeStruct((R, Dm), BF16),
        compiler_params=_params(("parallel",)),
    )(x, g)


def _rms_bwd(x, g, dy, dres, name):
    R, Dm = x.shape
    tr = _pick(R, (256, 128))
    has_res = dres is not None

    def body(*refs):
        if has_res:
            x_ref, g_ref, dy_ref, dres_ref, dx_ref, dg_ref = refs
        else:
            x_ref, g_ref, dy_ref, dx_ref, dg_ref = refs
        xf = x_ref[...]
        dy_ = dy_ref[...].astype(F32)
        r = lax.rsqrt(jnp.mean(xf * xf, axis=-1, keepdims=True) + EPS)
        gdy = dy_ * g_ref[...]
        mdot = jnp.mean(xf * gdy, axis=-1, keepdims=True)
        dx = r * gdy - xf * ((r * r * r) * mdot)
        if has_res:
            dx = dres_ref[...] + dx
        dx_ref[...] = dx

        @pl.when(pl.program_id(0) == 0)
        def _():
            dg_ref[...] = jnp.zeros_like(dg_ref)

        dg_ref[...] += jnp.sum(dy_ * (xf * r), axis=0, keepdims=True)

    row = pl.BlockSpec((tr, Dm), lambda i: (i, 0))
    vec = pl.BlockSpec((1, Dm), lambda i: (0, 0))
    ins = [x, g, dy] + ([dres] if has_res else [])
    return pl.pallas_call(
        body, name=name, grid=(R // tr,),
        in_specs=[row, vec, row] + ([row] if has_res else []),
        out_specs=[row, vec],
        out_shape=[jax.ShapeDtypeStruct((R, Dm), F32), jax.ShapeDtypeStruct((1, Dm), F32)],
        compiler_params=_params(("arbitrary",)),
    )(*ins)


def _final_loss(x, g, tgt):
    R, Dm = x.shape
    tr = _pick(R, (256, 128))

    def body(x_ref, g_ref, t_ref, l_ref, dx_ref, dxb_ref, dg_ref):
        xf = x_ref[...]
        gv = g_ref[...]
        r = lax.rsqrt(jnp.mean(xf * xf, axis=-1, keepdims=True) + EPS)
        xr = xf * r
        err = xr * gv - t_ref[...]
        dy_ = err * (1.0 / Dm)
        gdy = dy_ * gv
        mdot = jnp.mean(xf * gdy, axis=-1, keepdims=True)
        dx = r * gdy - xf * ((r * r * r) * mdot)
        dx_ref[...] = dx
        dxb_ref[...] = dx.astype(BF16)

        @pl.when(pl.program_id(0) == 0)
        def _():
            dg_ref[...] = jnp.zeros_like(dg_ref)
            l_ref[...] = jnp.zeros_like(l_ref)

        dg_ref[...] += jnp.sum(dy_ * xr, axis=0, keepdims=True)
        sq = jnp.sum(err * err, axis=1, keepdims=True)
        l_ref[...] += jnp.sum(sq, axis=0, keepdims=True) * (0.5 / Dm)

    row = pl.BlockSpec((tr, Dm), lambda i: (i, 0))
    vec = pl.BlockSpec((1, Dm), lambda i: (0, 0))
    return pl.pallas_call(
        body, name="final_norm_loss", grid=(R // tr,),
        in_specs=[row, vec, row],
        out_specs=[pl.BlockSpec((1, 1), lambda i: (0, 0)), row, row, vec],
        out_shape=[jax.ShapeDtypeStruct((1, 1), F32), jax.ShapeDtypeStruct((R, Dm), F32),
                   jax.ShapeDtypeStruct((R, Dm), BF16), jax.ShapeDtypeStruct((1, Dm), F32)],
        compiler_params=_params(("arbitrary",)),
    )(x, g, tgt)


MAX_TK = 2048

_DIMS = {"nn": (((1,), (0,)), ((), ())), "nt": (((1,), (1,)), ((), ())), "tn": (((0,), (0,)), ((), ()))}


def _mm(a, b, *, mode="nn", out_dtype=BF16, epi=None, extra=None, name):
    if mode == "nn":
        (M, K), N = a.shape, b.shape[1]
    elif mode == "nt":
        (M, K), N = a.shape, b.shape[0]
    else:
        (K, M), N = a.shape, b.shape[1]
    tm = _pick(M, (512, 256, 128) if epi == "rms_bwd" else (1024, 768, 512, 256, 128))
    tn = _pick(N, (1024, 896, 768, 640, 512, 384, 256, 128))
    tk = K if K <= MAX_TK else _pick(K, (MAX_TK, 1024, 512, 256, 128))
    nk = K // tk
    extras = () if extra is None else (extra if isinstance(extra, tuple) else (extra,))
    n_out = {"relu2": 2, "rms_bwd": 3}.get(epi, 1)
    assert epi != "rms_bwd" or tn == N

    def body(*refs):
        a_ref, b_ref = refs[:2]
        e_refs = refs[2:2 + len(extras)]
        e_ref = e_refs[0] if e_refs else None
        outs = refs[2 + len(extras):2 + len(extras) + n_out]
        k = pl.program_id(2)
        part = lax.dot_general(a_ref[...].astype(BF16), b_ref[...].astype(BF16), _DIMS[mode],
                               preferred_element_type=F32)

        def finish(acc):
            if epi is None:
                outs[0][...] = acc.astype(outs[0].dtype)
            elif epi == "add":
                outs[0][...] = (e_ref[...] + acc).astype(outs[0].dtype)
            elif epi == "relu2":
                outs[0][...] = acc.astype(BF16)
                rl = jnp.maximum(acc, 0.0)
                outs[1][...] = (rl * rl).astype(BF16)
            elif epi == "drelu2":
                u = e_ref[...].astype(F32)
                outs[0][...] = (acc * (2.0 * jnp.maximum(u, 0.0))).astype(outs[0].dtype)
            elif epi == "rms_bwd":
                x_ref, g_ref, dres_ref = e_refs
                xf = x_ref[...]
                r = lax.rsqrt(jnp.mean(xf * xf, axis=-1, keepdims=True) + EPS)
                gdy = acc * g_ref[...]
                mdot = jnp.mean(xf * gdy, axis=-1, keepdims=True)
                dx = dres_ref[...] + (r * gdy - xf * ((r * r * r) * mdot))
                outs[0][...] = dx
                outs[1][...] = dx.astype(BF16)

                @pl.when(pl.program_id(0) == 0)
                def _():
                    outs[2][...] = jnp.zeros_like(outs[2])

                outs[2][...] += jnp.sum(acc * (xf * r), axis=0, keepdims=True)

        if nk == 1:
            finish(part)
        else:
            acc_ref = refs[-1]

            @pl.when(k == 0)
            def _():
                acc_ref[...] = part

            @pl.when(jnp.logical_and(k > 0, k < nk - 1))
            def _():
                acc_ref[...] += part

            @pl.when(k == nk - 1)
            def _():
                finish(acc_ref[...] + part)

    if mode == "tn":
        a_spec = pl.BlockSpec((tk, tm), lambda i, j, k: (k, i))
    else:
        a_spec = pl.BlockSpec((tm, tk), lambda i, j, k: (i, k))
    if mode == "nt":
        b_spec = pl.BlockSpec((tn, tk), lambda i, j, k: (j, k))
    else:
        b_spec = pl.BlockSpec((tk, tn), lambda i, j, k: (k, j))
    o_spec = pl.BlockSpec((tm, tn), lambda i, j, k: (i, j))
    vec_spec = pl.BlockSpec((1, tn), lambda i, j, k: (0, j))
    ins, in_specs = [a, b] + list(extras), [a_spec, b_spec]
    if epi == "rms_bwd":
        in_specs += [o_spec, vec_spec, o_spec]
        out_shape = [jax.ShapeDtypeStruct((M, N), F32), jax.ShapeDtypeStruct((M, N), BF16),
                     jax.ShapeDtypeStruct((1, N), F32)]
        out_specs = [o_spec, o_spec, vec_spec]
    else:
        in_specs += [o_spec] * len(extras)
        out_shape = [jax.ShapeDtypeStruct((M, N), BF16 if epi == "relu2" else out_dtype)] * n_out
        out_specs = [o_spec] * n_out
    res = pl.pallas_call(
        body, name=name, grid=(M // tm, N // tn, nk),
        in_specs=in_specs, out_specs=out_specs, out_shape=out_shape,
        scratch_shapes=[pltpu.VMEM((tm, tn), F32)] if nk > 1 else [],
        compiler_params=_params(("arbitrary",) * 3 if epi == "rms_bwd" else ("parallel", "parallel", "arbitrary")),
    )(*ins)
    return res if n_out > 1 else res[0]


def _dot(a, b):
    return lax.dot_general(a, b, _DIMS["nn"], preferred_element_type=F32)


def _dot_nt(a, b):
    return lax.dot_general(a, b, _DIMS["nt"], preferred_element_type=F32)


def _dot_tn(a, b):
    return lax.dot_general(a, b, _DIMS["tn"], preferred_element_type=F32)


def _split_dot(x, t):
    hi = x.astype(BF16)
    lo = (x - hi.astype(F32)).astype(BF16)
    return _dot(jnp.concatenate([hi, lo], axis=1), jnp.concatenate([t, t], axis=0))


def _head_pair(ref, scale=None):
    xf = ref[...].astype(F32)
    if scale is not None:
        xf = xf * scale
    is_a = lax.broadcasted_iota(jnp.int32, xf.shape, 1) < HEAD_DIM
    return jnp.where(is_a, xf, 0.0).astype(BF16), jnp.where(is_a, 0.0, xf).astype(BF16)


def _stack(a, b):
    return jnp.concatenate([a, b], axis=0)


def _head_rows(ref, scale=None):
    return _stack(*_head_pair(ref, scale))


def _unstack_heads(x):
    rows = x.shape[0] // 2
    return _select_pair(x[:rows], x[rows:])


def _pair_rowsum(x):
    is_a = lax.broadcasted_iota(jnp.int32, x.shape, 1) < HEAD_DIM
    return (jnp.sum(jnp.where(is_a, x, 0.0), axis=1, keepdims=True),
            jnp.sum(jnp.where(is_a, 0.0, x), axis=1, keepdims=True))


def _select_pair(xa, xb):
    is_a = lax.broadcasted_iota(jnp.int32, xa.shape, 1) < HEAD_DIM
    return jnp.where(is_a, xa, xb)


def _two_cols(xa, xb):
    rows = xa.shape[0]
    first = lax.broadcasted_iota(jnp.int32, (rows, 2), 1) == 0
    return jnp.where(first, xa, xb)


def _softplus_parts(z):
    e = jnp.exp(-jnp.abs(z))
    return jnp.maximum(z, 0.0) + jnp.log(1.0 + e), e


def _tile_iotas():
    row = lax.broadcasted_iota(jnp.int32, (BK, BK), 0)
    col = lax.broadcasted_iota(jnp.int32, (BK, BK), 1)
    return row, col


def _stacked_iotas(bq, nk):
    row = lax.broadcasted_iota(jnp.int32, (2 * bq, nk), 0) & (bq - 1)
    col = lax.broadcasted_iota(jnp.int32, (2 * bq, nk), 1)
    return row, col


def _side_exchange(exchange, operand, n_in, n_out):
    if exchange is None:
        return [], [], [], [], lambda refs: (lambda: None, lambda: None)
    out_shape = jax.ShapeDtypeStruct(((N_CHIPS,) + operand.shape) if exchange is _AllGather else operand.shape,
                                     operand.dtype)
    n_sem = len(exchange.SCRATCH)

    def pick(refs):
        def make():
            return exchange(refs[n_in], refs[n_in + 1 + n_out], *refs[len(refs) - n_sem:])

        return (lambda: make().start()), (lambda: make().finish())

    return [operand], [ANY], [out_shape], [ANY], pick


def _sb_fwd(proj, name, exchange=None, operand=None):
    S = proj.shape[0]
    nqb = S // (CHAINS * BQ)
    x_in, x_in_specs, x_out, x_out_specs, pick = _side_exchange(exchange, operand, 3, 1)

    def body(*refs):
        q_ref, k_ref, v_ref = refs[:3]
        o_ref = refs[3 + len(x_in)]
        acc_ref = refs[3 + len(x_in) + 1 + len(x_out)]
        start, finish = pick(refs)
        p = pl.program_id(0)
        i = pl.program_id(1)

        @pl.when(jnp.logical_and(p == 0, i == 0))
        def _():
            start()

        q2 = [_head_rows(q_ref.at[pl.ds(ch * BQ, BQ)], SCALE) for ch in range(CHAINS)]
        row, col = _tile_iotas()
        tri = (row > col).astype(BF16)
        srow, scol = _stacked_iotas(BQ, BK)
        acc_ref[...] = jnp.zeros_like(acc_ref)

        def tile(ch, kb, c, dmask=None, valid=None):
            r0 = pl.multiple_of(kb * BK, BK)
            kblk = k_ref[pl.ds(r0, BK), :]
            vblk = v_ref[pl.ds(r0, BK), :]
            z = _dot_nt(q2[ch], kblk)
            sp, _ = _softplus_parts(z)
            lm = -sp
            if dmask is not None:
                lm = jnp.where(dmask, lm, 0.0)
            btw = _split_dot(lm, tri)
            w = jnp.exp((z - sp) + btw + c)
            if dmask is not None:
                w = jnp.where(dmask, w, 0.0)
            if valid is not None:
                w = w * valid
            acc_ref[ch] += _dot(w.astype(BF16), vblk)
            return c + btw[:, 0:1] + lm[:, 0:1]

        def alive(c):
            return jnp.max(c) > UNDERFLOW_BOUND

        cs = [jnp.zeros((2 * BQ, 1), F32)] * CHAINS
        for d in reversed(range(DIAG_TILES)):
            cs = [tile(ch, (CHAINS * i + ch) * DIAG_TILES + d, cs[ch], dmask=scol < srow - d * BK)
                  for ch in range(CHAINS)]

        def tile_of(ch, t):
            return (CHAINS * i + ch) * DIAG_TILES - 1 - t

        def more(cs, t):
            go = [jnp.logical_and(alive(cs[ch]), tile_of(ch, t) >= 0) for ch in range(CHAINS)]
            return functools.reduce(jnp.logical_or, go).astype(jnp.int32)

        def step(st):
            t, _, cs = st
            new = []
            for ch in range(CHAINS):
                kb = tile_of(ch, t)
                if ch == CHAINS - 1:
                    new.append(tile(ch, kb, cs[ch]))
                else:
                    new.append(tile(ch, jnp.maximum(kb, 0), cs[ch], valid=(kb >= 0).astype(F32)))
            return t + 1, more(new, t + 1), new

        lax.while_loop(lambda st: st[1] > 0, step, (0, more(cs, 0), cs))
        for ch in range(CHAINS):
            o_ref[pl.ds(ch * BQ, BQ), :] = _unstack_heads(acc_ref[ch])

        @pl.when(jnp.logical_and(p == 3, i == nqb - 1))
        def _():
            finish()

    blk = pl.BlockSpec((CHAINS * BQ, LANES), lambda p, i: (i, p))
    res = pl.pallas_call(
        body, name=name, grid=(4, nqb),
        in_specs=[blk, pl.BlockSpec((S, LANES), lambda p, i: (0, 4 + p)),
                  pl.BlockSpec((S, LANES), lambda p, i: (0, 8 + p))] + x_in_specs,
        out_specs=[blk] + x_out_specs,
        out_shape=[jax.ShapeDtypeStruct((S, MERGED_WIDTH), F32)] + x_out,
        scratch_shapes=[pltpu.VMEM((CHAINS, 2 * BQ, LANES), F32)] + ([] if exchange is None else exchange.SCRATCH),
        compiler_params=_params(("arbitrary", "arbitrary")),
    )(proj, proj, proj, *x_in)
    return res if exchange is not None else res[0]


def _sb_bwd(proj, merged, dmerged, name, exchange=None, operand=None):
    S = proj.shape[0]
    nqb = S // (CHAINS * BQ)
    x_in, x_in_specs, x_out, x_out_specs, pick = _side_exchange(exchange, operand, 5, 3)

    def body(*refs):
        q_ref, k_ref, v_ref, o_ref, do_ref = refs[:5]
        dq_ref, dk_hbm, dv_hbm = refs[5 + len(x_in):8 + len(x_in)]
        dq_acc, dk_acc, dv_acc, sem = refs[8 + len(x_in) + len(x_out):12 + len(x_in) + len(x_out)]
        start, finish = pick(refs)
        p = pl.program_id(0)
        i = pl.program_id(1)

        @pl.when(jnp.logical_and(p == 0, i == 0))
        def _():
            start()

        @pl.when(i == 0)
        def _():
            dk_acc[...] = jnp.zeros_like(dk_acc)
            dv_acc[...] = jnp.zeros_like(dv_acc)

        rows = [pl.ds(ch * BQ, BQ) for ch in range(CHAINS)]
        q2 = [_head_rows(q_ref.at[rw], SCALE) for rw in rows]
        do2 = [_head_rows(do_ref.at[rw]) for rw in rows]
        tot = [_stack(*_pair_rowsum(do_ref[rw, :].astype(F32) * o_ref[rw, :])) for rw in rows]
        row, col = _tile_iotas()
        tri_gt = (row > col).astype(BF16)
        tri_ge = (row >= col).astype(BF16)
        srow, scol = _stacked_iotas(BQ, BK)
        dq_acc[...] = jnp.zeros_like(dq_acc)

        def tile(ch, kb, st, dmask=None, valid=None):
            masked = dmask is not None
            c, r = st
            r0 = pl.multiple_of(kb * BK, BK)
            kblk = k_ref[pl.ds(r0, BK), :]
            vblk = v_ref[pl.ds(r0, BK), :]
            z = _dot_nt(q2[ch], kblk)
            sp, e = _softplus_parts(z)
            lm = -sp
            if masked:
                lm = jnp.where(dmask, lm, 0.0)
            btw = _split_dot(lm, tri_gt)
            w = jnp.exp((z - sp) + btw + c)
            if masked:
                w = jnp.where(dmask, w, 0.0)
            if valid is not None:
                w = w * valid
            wb = w.astype(BF16)
            a = wb.astype(F32) * _dot_nt(do2[ch], vblk)
            suffix = _split_dot(a, tri_ge) + r
            rcp = 1.0 / (1.0 + e)
            pos = z >= 0.0
            sig = jnp.where(pos, rcp, e * rcp)
            sig_neg = jnp.where(pos, e * rcp, rcp)
            dz = a * sig_neg - (tot[ch] - suffix) * sig
            if masked:
                dz = jnp.where(dmask, dz, 0.0)
            if valid is not None:
                dz = dz * valid
            dzb = dz.astype(BF16)
            dq_acc[ch] += _dot(dzb, kblk)
            dk_acc[pl.ds(r0, BK), :] += _dot_tn(dzb, q2[ch])
            dv_acc[pl.ds(r0, BK), :] += _dot_tn(wb, do2[ch])
            return c + btw[:, 0:1] + lm[:, 0:1], suffix[:, 0:1]

        def alive(st):
            return jnp.max(st[0]) > UNDERFLOW_BOUND

        zero = jnp.zeros((2 * BQ, 1), F32)
        sts = [(zero, zero)] * CHAINS
        for d in reversed(range(DIAG_TILES)):
            sts = [tile(ch, (CHAINS * i + ch) * DIAG_TILES + d, sts[ch], dmask=scol < srow - d * BK)
                   for ch in range(CHAINS)]

        def tile_of(ch, t):
            return (CHAINS * i + ch) * DIAG_TILES - 1 - t

        def more(sts, t):
            go = [jnp.logical_and(alive(sts[ch]), tile_of(ch, t) >= 0) for ch in range(CHAINS)]
            return functools.reduce(jnp.logical_or, go).astype(jnp.int32)

        def step(s):
            t, _, sts = s
            new = []
            for ch in range(CHAINS):
                kb = tile_of(ch, t)
                if ch == CHAINS - 1:
                    new.append(tile(ch, kb, sts[ch]))
                else:
                    new.append(tile(ch, jnp.maximum(kb, 0), sts[ch], valid=(kb >= 0).astype(F32)))
            return t + 1, more(new, t + 1), new

        lax.while_loop(lambda s: s[1] > 0, step, (0, more(sts, 0), sts))
        for ch in range(CHAINS):
            dq_ref[rows[ch], :] = (_unstack_heads(dq_acc[ch]) * SCALE).astype(dq_ref.dtype)

        @pl.when(i == nqb - 1)
        def _():
            ck = pltpu.make_async_copy(dk_acc, dk_hbm.at[p], sem.at[0])
            cv = pltpu.make_async_copy(dv_acc, dv_hbm.at[p], sem.at[1])
            ck.start()
            cv.start()
            ck.wait()
            cv.wait()

        @pl.when(jnp.logical_and(p == 3, i == nqb - 1))
        def _():
            finish()

    blk = lambda off: pl.BlockSpec((CHAINS * BQ, LANES), lambda p, i: (i, off + p))
    slab = lambda off: pl.BlockSpec((S, LANES), lambda p, i: (0, off + p))
    return pl.pallas_call(
        body, name=name, grid=(4, nqb),
        in_specs=[blk(0), slab(4), slab(8), blk(0), blk(0)] + x_in_specs,
        out_specs=[blk(0), ANY, ANY] + x_out_specs,
        out_shape=[jax.ShapeDtypeStruct((S, MIX_WIDTH), BF16),
                   jax.ShapeDtypeStruct((4, S, LANES), F32), jax.ShapeDtypeStruct((4, S, LANES), F32)] + x_out,
        scratch_shapes=[pltpu.VMEM((CHAINS, 2 * BQ, LANES), F32), pltpu.VMEM((S, LANES), F32),
                        pltpu.VMEM((S, LANES), F32), pltpu.SemaphoreType.DMA((2,))]
        + ([] if exchange is None else exchange.SCRATCH),
        compiler_params=_params(("arbitrary", "arbitrary")),
    )(proj, proj, proj, merged, dmerged, *x_in)


def _key_norm_max(k_ref, knorm_ref, nkb):
    def step(kb, m):
        r0 = pl.multiple_of(kb * BK, BK)
        blk = k_ref[pl.ds(r0, BK), :].astype(F32)
        sa, sb = _pair_rowsum(blk * blk)
        return (jnp.maximum(m[0], jnp.max(sa, axis=0, keepdims=True)),
                jnp.maximum(m[1], jnp.max(sb, axis=0, keepdims=True)))

    zero = jnp.zeros((1, 1), F32)
    ma, mb = lax.fori_loop(0, nkb, step, (zero, zero))
    knorm_ref[...] = _select_pair(jnp.broadcast_to(ma, (1, LANES)), jnp.broadcast_to(mb, (1, LANES)))


FQ = 512
FK = FQ
GATE_BLOCKS = FK // BK


def _key_gates(cr_ref, kb):
    blocks = [cr_ref[0, GATE_BLOCKS * kb + j] for j in range(GATE_BLOCKS)]
    per_head = [jnp.broadcast_to(jnp.concatenate([b[h:h + 1] for b in blocks], axis=1), (FQ, FK)) for h in range(2)]
    return _stack(*per_head)


def _last_gate(cr_ref, kb):
    last = cr_ref[0, GATE_BLOCKS * jnp.maximum(kb, 0) + GATE_BLOCKS - 1]
    return _stack(*[jnp.broadcast_to(last[h:h + 1, BK - 1:BK], (FQ, 1)) for h in range(2)])


def _logit_bound(q_ref, knorm_ref):
    qf = q_ref[...].astype(F32) * SCALE
    qa, qb = _pair_rowsum(qf * qf)
    kn = knorm_ref[...]
    return _stack(jnp.sqrt(qa * kn[:, 0:1]), jnp.sqrt(qb * kn[:, HEAD_DIM:HEAD_DIM + 1]))


def _causal_bias(bias_ref):
    srow, scol = _stacked_iotas(FQ, FK)
    bias_ref[...] = jnp.where(scol <= srow, 0.0, NEG_INF)


def _fox_fwd(proj, kv, c_col, c_row, name):
    S = proj.shape[0]
    nqb = S // FQ

    def body(q_ref, k_ref, v_ref, cc_ref, cr_ref, o_ref, lse_ref, acc_ref, knorm_ref, bias_ref):
        i = pl.program_id(1)

        @pl.when(i == 0)
        def _():
            _key_norm_max(k_ref, knorm_ref, S // BK)
            _causal_bias(bias_ref)

        q2 = _head_rows(q_ref, SCALE)
        bound = _logit_bound(q_ref, knorm_ref)
        cc = cc_ref[0]
        ct = _stack(cc[:, 0:1], cc[:, 1:2])
        acc_ref[...] = jnp.zeros_like(acc_ref)

        def tile(kb, st, masked):
            m, l = st
            r0 = pl.multiple_of(kb * FK, FK)
            kblk = k_ref[pl.ds(r0, FK), :]
            vblk = v_ref[pl.ds(r0, FK), :]
            z = _dot_nt(q2, kblk) + ct - _key_gates(cr_ref, kb)
            if masked:
                z = z + bias_ref[...]
            m_new = jnp.maximum(m, jnp.max(z, axis=1, keepdims=True))
            alpha = jnp.exp(m - m_new)
            pr = jnp.exp(z - m_new)
            acc_ref[...] = alpha * acc_ref[...] + _split_dot(pr, vblk)
            return m_new, alpha * l + jnp.sum(pr, axis=1, keepdims=True)

        def alive(kb, st):
            reach = bound + ct - _last_gate(cr_ref, kb) - st[0]
            return (jnp.max(reach) > UNDERFLOW_BOUND).astype(jnp.int32)

        neg = jnp.full((2 * FQ, 1), NEG_INF, F32)
        zero = jnp.zeros((2 * FQ, 1), F32)
        st0 = tile(i, (neg, zero), True)

        def cond(s):
            return jnp.logical_and(s[0] >= 0, s[1] > 0)

        def step(s):
            kb, _, st = s
            st = tile(kb, st, False)
            return kb - 1, alive(kb - 1, st), st

        _, _, (m, l) = lax.while_loop(cond, step, (i - 1, alive(i - 1, st0), st0))
        o_ref[...] = _unstack_heads(acc_ref[...] / l)
        lse = m + jnp.log(l)
        lse_ref[0] = _two_cols(lse[:FQ], lse[FQ:])

    return pl.pallas_call(
        body, name=name, grid=(4, nqb),
        in_specs=[pl.BlockSpec((FQ, LANES), lambda p, i: (i, p)),
                  pl.BlockSpec((S, LANES), lambda p, i: (0, p)),
                  pl.BlockSpec((S, LANES), lambda p, i: (0, 4 + p)),
                  pl.BlockSpec((1, FQ, 2), lambda p, i: (p, i, 0)),
                  pl.BlockSpec((1, S // BK, 8, LANES), lambda p, i: (p, 0, 0, 0))],
        out_specs=[pl.BlockSpec((FQ, LANES), lambda p, i: (i, p)),
                   pl.BlockSpec((1, FQ, 2), lambda p, i: (p, i, 0))],
        out_shape=[jax.ShapeDtypeStruct((S, MERGED_WIDTH), F32), jax.ShapeDtypeStruct((4, S, 2), F32)],
        scratch_shapes=[pltpu.VMEM((2 * FQ, LANES), F32), pltpu.VMEM((1, LANES), F32),
                        pltpu.VMEM((2 * FQ, FK), F32)],
        compiler_params=_params(("arbitrary", "arbitrary")),
    )(proj, kv, kv, c_col, c_row)


def _fox_bwd(proj, kv, c_col, c_row, lse, merged, dmerged, dk_prev, dv_prev, dc_prev, name):
    S = proj.shape[0]
    nqb = S // FQ

    def body(q_ref, k_ref, v_ref, cc_ref, cr_ref, lse_ref, o_ref, do_ref, dkp_hbm, dvp_hbm, dcp_ref,
             dq_ref, dk_hbm, dv_hbm, dc_ref, dq_acc, dk_acc, dv_acc, knorm_ref, bias_ref, sem):
        p = pl.program_id(0)
        i = pl.program_id(1)

        @pl.when(i == 0)
        def _():
            ck = pltpu.make_async_copy(dkp_hbm.at[p], dk_acc, sem.at[0])
            cv = pltpu.make_async_copy(dvp_hbm.at[p], dv_acc, sem.at[1])
            ck.start()
            cv.start()
            dc_ref[...] = dcp_ref[...]
            _key_norm_max(k_ref, knorm_ref, S // BK)
            _causal_bias(bias_ref)
            ck.wait()
            cv.wait()

        q2 = _head_rows(q_ref, SCALE)
        do2 = _head_rows(do_ref)
        tot = _stack(*_pair_rowsum(do_ref[...].astype(F32) * o_ref[...]))
        bound = _logit_bound(q_ref, knorm_ref)
        cc = cc_ref[0]
        ct = _stack(cc[:, 0:1], cc[:, 1:2])
        ls = lse_ref[0]
        lse = _stack(ls[:, 0:1], ls[:, 1:2])
        sub = lax.broadcasted_iota(jnp.int32, (8, LANES), 0)
        dq_acc[...] = jnp.zeros_like(dq_acc)

        def tile(kb, masked):
            r0 = pl.multiple_of(kb * FK, FK)
            kblk = k_ref[pl.ds(r0, FK), :]
            vblk = v_ref[pl.ds(r0, FK), :]
            z = _dot_nt(q2, kblk) + ct - _key_gates(cr_ref, kb)
            if masked:
                z = z + bias_ref[...]
            pr = jnp.exp(z - lse)
            ds = pr * (_dot_nt(do2, vblk) - tot)
            dsb = ds.astype(BF16)
            dq_acc[...] += _dot(dsb, kblk)
            dk_acc[pl.ds(r0, FK), :] += _dot_tn(dsb, q2)
            dv_acc[pl.ds(r0, FK), :] += _dot_tn(pr.astype(BF16), do2)
            dca = jnp.sum(ds[:FQ], axis=0, keepdims=True)
            dcb = jnp.sum(ds[FQ:], axis=0, keepdims=True)
            for j in range(GATE_BLOCKS):
                cols = slice(j * BK, (j + 1) * BK)
                old = dc_ref[0, GATE_BLOCKS * kb + j]
                dc_ref[0, GATE_BLOCKS * kb + j] = jnp.where(sub == 0, old - dca[:, cols],
                                                            jnp.where(sub == 1, old - dcb[:, cols], old))

        def alive(kb):
            reach = bound + ct - _last_gate(cr_ref, kb) - lse
            return (jnp.max(reach) > UNDERFLOW_BOUND).astype(jnp.int32)

        tile(i, True)

        def cond(s):
            return jnp.logical_and(s[0] >= 0, s[1] > 0)

        def step(s):
            kb, _ = s
            tile(kb, False)
            return kb - 1, alive(kb - 1)

        lax.while_loop(cond, step, (i - 1, alive(i - 1)))
        dq_ref[...] = (_unstack_heads(dq_acc[...]) * SCALE).astype(dq_ref.dtype)

        @pl.when(i == nqb - 1)
        def _():
            ck = pltpu.make_async_copy(dk_acc, dk_hbm.at[p], sem.at[0])
            cv = pltpu.make_async_copy(dv_acc, dv_hbm.at[p], sem.at[1])
            ck.start()
            cv.start()
            ck.wait()
            cv.wait()

    blk = lambda off: pl.BlockSpec((FQ, LANES), lambda p, i: (i, off + p))
    slab = lambda off: pl.BlockSpec((S, LANES), lambda p, i: (0, off + p))
    cols = pl.BlockSpec((1, FQ, 2), lambda p, i: (p, i, 0))
    rows = pl.BlockSpec((1, S // BK, 8, LANES), lambda p, i: (p, 0, 0, 0))
    return pl.pallas_call(
        body, name=name, grid=(4, nqb),
        in_specs=[blk(0), slab(0), slab(4), cols, rows, cols, blk(0), blk(0), ANY, ANY, rows],
        out_specs=[blk(0), ANY, ANY, rows],
        out_shape=[jax.ShapeDtypeStruct((S, MIX_WIDTH), BF16),
                   jax.ShapeDtypeStruct((4, S, LANES), F32), jax.ShapeDtypeStruct((4, S, LANES), F32),
                   jax.ShapeDtypeStruct((4, S // BK, 8, LANES), F32)],
        scratch_shapes=[pltpu.VMEM((2 * FQ, LANES), F32), pltpu.VMEM((S, LANES), F32),
                        pltpu.VMEM((S, LANES), F32), pltpu.VMEM((1, LANES), F32),
                        pltpu.VMEM((2 * FQ, FK), F32), pltpu.SemaphoreType.DMA((2,))],
        compiler_params=_params(("arbitrary", "arbitrary")),
    )(proj, kv, kv, c_col, c_row, lse, merged, dmerged, dk_prev, dv_prev, dc_prev)


def _lane_scan(x, reverse):
    lane = lax.broadcasted_iota(jnp.int32, x.shape, 1)
    d = 1
    while d < LANES:
        if reverse:
            x = x + jnp.where(lane < LANES - d, pltpu.roll(x, LANES - d, 1), 0.0)
        else:
            x = x + jnp.where(lane >= d, pltpu.roll(x, d, 1), 0.0)
        d *= 2
    return x


def _gate_fwd(fl3, b8):
    nb = fl3.shape[0]

    def body(fl_ref, b_ref, c_ref):
        def step(kb, carry):
            x = fl_ref[kb] + b_ref[...]
            sp, _ = _softplus_parts(-x)
            c = _lane_scan(-sp, False) + carry
            c_ref[kb] = c
            return c[:, LANES - 1:LANES]

        lax.fori_loop(0, nb, step, jnp.zeros((8, 1), F32))

    return pl.pallas_call(body, name="forget_gate_cumsum",
                          out_shape=jax.ShapeDtypeStruct(fl3.shape, F32),
                          compiler_params=_params())(fl3, b8)


def _gate_bwd(dc3, fl3, b8):
    nb = fl3.shape[0]

    def body(dc_ref, fl_ref, b_ref, dfl_ref, db_ref):
        def step(t, st):
            carry, dbs = st
            kb = nb - 1 - t
            g = _lane_scan(dc_ref[kb], True) + carry
            x = fl_ref[kb] + b_ref[...]
            e = jnp.exp(-jnp.abs(x))
            rcp = 1.0 / (1.0 + e)
            dfl = g * jnp.where(x >= 0.0, e * rcp, rcp)
            dfl_ref[kb] = dfl
            return g[:, 0:1], dbs + dfl

        _, dbs = lax.fori_loop(0, nb, step, (jnp.zeros((8, 1), F32), jnp.zeros((8, LANES), F32)))
        db_ref[...] = jnp.broadcast_to(jnp.sum(dbs, axis=1, keepdims=True), (8, LANES))

    return pl.pallas_call(body, name="forget_gate_bwd",
                          out_shape=[jax.ShapeDtypeStruct(fl3.shape, F32), jax.ShapeDtypeStruct((8, LANES), F32)],
                          compiler_params=_params())(dc3, fl3, b8)


MEM_TQ = 512


def _mem_fwd(proj, qcol, mkv, mix, name):
    S = proj.shape[0]
    M = mkv.shape[0]

    def body(q_ref, mk_ref, mv_ref, mix_ref, o_ref, lse_ref):
        q2 = _head_rows(q_ref, SCALE)
        s = _dot_nt(q2, mk_ref[...])
        m = jnp.max(s, axis=1, keepdims=True)
        pr = jnp.exp(s - m)
        l = jnp.sum(pr, axis=1, keepdims=True)
        o_ref[...] = _unstack_heads(_dot(pr.astype(BF16), mv_ref[...]) / l)
        lse = m + jnp.log(l)
        lse_ref[0] = _two_cols(lse[:MEM_TQ], lse[MEM_TQ:])

    return pl.pallas_call(
        body, name=name, grid=(2, S // MEM_TQ),
        in_specs=[pl.BlockSpec((MEM_TQ, LANES), lambda p, i: (i, qcol + p)),
                  pl.BlockSpec((M, LANES), lambda p, i: (0, p)),
                  pl.BlockSpec((M, LANES), lambda p, i: (0, 2 + p)), ANY],
        out_specs=[pl.BlockSpec((MEM_TQ, LANES), lambda p, i: (i, 4 + p)),
                   pl.BlockSpec((1, MEM_TQ, 2), lambda p, i: (p, i, 0))],
        out_shape=[jax.ShapeDtypeStruct((S, MERGED_WIDTH), F32), jax.ShapeDtypeStruct((2, S, 2), F32)],
        input_output_aliases={3: 0},
        compiler_params=_params(("parallel", "parallel")),
    )(proj, mkv, mkv, mix)


def _mem_bwd(proj, qcol, mkv, lse, merged, dmerged, name):
    S = proj.shape[0]
    M = mkv.shape[0]

    def body(q_ref, mk_ref, mv_ref, lse_ref, o_ref, do_ref, dq_ref, dmk_ref, dmv_ref):
        @pl.when(pl.program_id(1) == 0)
        def _():
            dmk_ref[...] = jnp.zeros_like(dmk_ref)
            dmv_ref[...] = jnp.zeros_like(dmv_ref)

        q2 = _head_rows(q_ref, SCALE)
        do2 = _head_rows(do_ref)
        tot = _stack(*_pair_rowsum(do_ref[...].astype(F32) * o_ref[...]))
        ls = lse_ref[0]
        pr = jnp.exp(_dot_nt(q2, mk_ref[...]) - _stack(ls[:, 0:1], ls[:, 1:2]))
        ds = pr * (_dot_nt(do2, mv_ref[...]) - tot)
        dsb = ds.astype(BF16)
        dmk_ref[...] += _dot_tn(dsb, q2)
        dmv_ref[...] += _dot_tn(pr.astype(BF16), do2)
        dq_ref[...] = (_unstack_heads(_dot(dsb, mk_ref[...])) * SCALE).astype(dq_ref.dtype)

    blk = lambda off: pl.BlockSpec((MEM_TQ, LANES), lambda p, i: (i, off + p))
    acc = pl.BlockSpec((M, LANES), lambda p, i: (0, p))
    return pl.pallas_call(
        body, name=name, grid=(2, S // MEM_TQ),
        in_specs=[blk(qcol), pl.BlockSpec((M, LANES), lambda p, i: (0, p)),
                  pl.BlockSpec((M, LANES), lambda p, i: (0, 2 + p)),
                  pl.BlockSpec((1, MEM_TQ, 2), lambda p, i: (p, i, 0)), blk(4), blk(4)],
        out_specs=[blk(0), acc, acc],
        out_shape=[jax.ShapeDtypeStruct((S, MEM_WIDTH), BF16), jax.ShapeDtypeStruct((M, MEM_WIDTH), F32),
                   jax.ShapeDtypeStruct((M, MEM_WIDTH), F32)],
        compiler_params=_params(("parallel", "arbitrary")),
    )(proj, mkv, mkv, lse, merged, dmerged)


def _c_layouts(c3):
    nb = c3.shape[0]
    pairs = c3.reshape(nb, 4, 2, LANES).transpose(1, 0, 2, 3)
    c_row = jnp.pad(pairs, ((0, 0), (0, 0), (0, 6), (0, 0)))
    c_col = pairs.transpose(0, 1, 3, 2).reshape(4, nb * LANES, 2)
    return c_col, c_row


def _local_step(x, mem, wb, rest_shard, sm, loss_target):
    S = x.shape[0]
    nb = S // BK
    vec = lambda a: a.reshape(1, D_MODEL)
    b8 = jnp.broadcast_to(sm["b_f"].reshape(8, 1), (8, LANES))

    saved = []
    shared = None
    h = x
    for l in range(DEPTH):
        if l == N_A:
            w_kvf = jnp.pad(wb["w_kv_shared"], ((0, 0), (0, 1152 - 1032)))
            hs = _rms_fwd(h, vec(sm["kv_norm_g"]), "kv_norm")
            kvf = _mm(hs, w_kvf, out_dtype=F32, name="kv_shared_proj")
            kv = kvf[:, :2 * MIX_WIDTH].astype(BF16)
            fl3 = kvf[:, 2 * MIX_WIDTH:2 * MIX_WIDTH + 8].T.reshape(8, nb, LANES).transpose(1, 0, 2)
            c3 = _gate_fwd(fl3, b8)
            c_col, c_row = _c_layouts(c3)
            shared = dict(h=h, hs=hs, kv=kv, fl3=fl3, c_col=c_col, c_row=c_row)
        hn = _rms_fwd(h, vec(sm["norm1_g"][l]), f"norm1_{l}")
        mn = _rms_fwd(mem, vec(sm["mem_norm_g"][l]), f"mem_norm_{l}")
        mkv = _mm(mn, wb["w_mem_kv"][l], name=f"mem_kv_proj_{l}")
        if l < N_A:
            w_in = wb["w_in_a"][l]
            proj = _mm(hn, w_in, name=f"in_proj_{l}")
            if l == 0:
                mix, gathered = _sb_fwd(proj, f"stickbreak_fwd_{l}", _AllGather, rest_shard)
                _unpack_gathered(PART_REST, gathered, wb)
            else:
                mix = _sb_fwd(proj, f"stickbreak_fwd_{l}")
            lse, qcol = None, 12
        else:
            w_in = wb["w_in_b"][l - N_A]
            proj = _mm(hn, w_in, name=f"in_proj_{l}")
            mix, lse = _fox_fwd(proj, shared["kv"], shared["c_col"], shared["c_row"], f"fox_fwd_{l}")
            qcol = 4
        merged, mlse = _mem_fwd(proj, qcol, mkv, mix, f"mem_attn_fwd_{l}")
        h_mid = _mm(merged, wb["w_o"][l], out_dtype=F32, epi="add", extra=h, name=f"out_proj_{l}")
        hn2 = _rms_fwd(h_mid, vec(sm["norm2_g"][l]), f"norm2_{l}")
        u, act = _mm(hn2, wb["w_mlp1"][l], epi="relu2", name=f"mlp1_{l}")
        h_out = _mm(act, wb["w_mlp2"][l], out_dtype=F32, epi="add", extra=h_mid, name=f"mlp2_{l}")
        saved.append(dict(h=h, hn=hn, mn=mn, mkv=mkv, proj=proj, lse=lse, mlse=mlse, qcol=qcol, merged=merged,
                          h_mid=h_mid, hn2=hn2, u=u, act=act, w_in=w_in))
        h = h_out

    loss, dh, dhb, dg_final = _final_loss(h, vec(sm["final_norm_g"]), loss_target)

    gb = {n: [None] * (DEPTH if n not in ("w_in_a", "w_in_b") else 2) for n in
          ("w_in_a", "w_in_b", "w_mem_kv", "w_o", "w_mlp1", "w_mlp2")}
    gs = {n: [None] * DEPTH for n in ("norm1_g", "mem_norm_g", "norm2_g")}
    dk_sh = jnp.zeros((4, S, LANES), F32)
    dv_sh = jnp.zeros((4, S, LANES), F32)
    dc_sh = jnp.zeros((4, nb, 8, LANES), F32)
    for l in reversed(range(DEPTH)):
        sv = saved[l]
        du = _mm(dhb, wb["w_mlp2"][l], mode="nt", epi="drelu2", extra=sv["u"], name=f"mlp2_dx_{l}")
        gb["w_mlp2"][l] = _mm(sv["act"], dhb, mode="tn", out_dtype=F32, name=f"mlp2_dw_{l}")
        gb["w_mlp1"][l] = _mm(sv["hn2"], du, mode="tn", out_dtype=F32, name=f"mlp1_dw_{l}")
        dh, dhb, gs["norm2_g"][l] = _mm(du, wb["w_mlp1"][l], mode="nt", epi="rms_bwd",
                                        extra=(sv["h_mid"], vec(sm["norm2_g"][l]), dh),
                                        name=f"mlp1_dx_norm2_bwd_{l}")
        dmerged = _mm(dhb, wb["w_o"][l], mode="nt", name=f"out_proj_dx_{l}")
        gb["w_o"][l] = _mm(sv["merged"], dhb, mode="tn", out_dtype=F32, name=f"out_proj_dw_{l}")
        if l == 0:
            gb["w_kv_shared"] = g_kvf[:, :1032]
            rest_grads = _pack_grads(PART_REST, ROWS_REST, gb, None)
            dq, dk, dv, rest_received = _sb_bwd(sv["proj"], sv["merged"], dmerged, f"stickbreak_bwd_{l}",
                                                _Scatter, rest_grads)
        elif l < N_A:
            dq, dk, dv = _sb_bwd(sv["proj"], sv["merged"], dmerged, f"stickbreak_bwd_{l}")
        else:
            dq, dk_sh, dv_sh, dc_sh = _fox_bwd(sv["proj"], shared["kv"], shared["c_col"], shared["c_row"],
                                               sv["lse"], sv["merged"], dmerged, dk_sh, dv_sh, dc_sh,
                                               f"fox_bwd_{l}")
        dqm, dmk, dmv = _mem_bwd(sv["proj"], sv["qcol"], sv["mkv"], sv["mlse"], sv["merged"], dmerged,
                                 f"mem_attn_bwd_{l}")
        if l < N_A:
            flat = lambda t: t.transpose(1, 0, 2).reshape(S, MIX_WIDTH).astype(BF16)
            dproj = jnp.concatenate([dq, flat(dk), flat(dv), dqm], axis=1)
        else:
            dproj = jnp.concatenate([dq, dqm], axis=1)
        name_in = "w_in_a" if l < N_A else "w_in_b"
        gb[name_in][l if l < N_A else l - N_A] = _mm(sv["hn"], dproj, mode="tn", out_dtype=F32,
                                                      name=f"in_proj_dw_{l}")
        dh, dhb, gs["norm1_g"][l] = _mm(dproj, sv["w_in"], mode="nt", epi="rms_bwd",
                                        extra=(sv["h"], vec(sm["norm1_g"][l]), dh),
                                        name=f"in_proj_dx_norm1_bwd_{l}")
        dmkv = jnp.concatenate([dmk, dmv], axis=1)
        gb["w_mem_kv"][l] = _mm(sv["mn"], dmkv, mode="tn", out_dtype=F32, name=f"mem_kv_dw_{l}")
        dmn = _mm(dmkv, wb["w_mem_kv"][l], mode="nt", out_dtype=F32, name=f"mem_kv_dx_{l}")
        _, gs["mem_norm_g"][l] = _rms_bwd(mem, vec(sm["mem_norm_g"][l]), dmn, None, f"mem_norm_bwd_{l}")
        if l == N_A:
            dfl3, db8 = _gate_bwd(dc_sh.reshape(4, nb, 8, LANES)[:, :, :2].transpose(1, 0, 2, 3).reshape(nb, 8, LANES),
                                  shared["fl3"], b8)
            dfl = dfl3.transpose(1, 0, 2).reshape(8, S).T
            flat = lambda t: t.transpose(1, 0, 2).reshape(S, MIX_WIDTH).astype(BF16)
            dkvf = jnp.concatenate([flat(dk_sh), flat(dv_sh),
                                    jnp.pad(dfl, ((0, 0), (0, LANES - 8))).astype(BF16)], axis=1)
            g_kvf = _mm(shared["hs"], dkvf, mode="tn", out_dtype=F32, name="kv_shared_dw")
            dh, dhb, g_kvn = _mm(dkvf, w_kvf, mode="nt", epi="rms_bwd",
                                 extra=(shared["h"], vec(sm["kv_norm_g"]), dh), name="kv_shared_dx_norm_bwd")
            g_bf = db8[:, 0]

    gsmall = {n: jnp.concatenate(v, axis=0) for n, v in gs.items()}
    gsmall["kv_norm_g"] = g_kvn
    gsmall["final_norm_g"] = dg_final
    gsmall["b_f"] = g_bf
    return loss, dh, gb, gsmall, rest_received


def kernel(x, mem, norm1_g, w_in_a, w_in_b, w_mem_kv, mem_norm_g, w_o, norm2_g, w_mlp1, w_mlp2, kv_norm_g, w_kv_shared, b_f, final_norm_g, loss_target, m_norm1_g, m_w_in_a, m_w_in_b, m_w_mem_kv, m_mem_norm_g, m_w_o, m_norm2_g, m_w_mlp1, m_w_mlp2, m_kv_norm_g, m_w_kv_shared, m_b_f, m_final_norm_g, v_norm1_g, v_w_in_a, v_w_in_b, v_w_mem_kv, v_mem_norm_g, v_w_o, v_norm2_g, v_w_mlp1, v_w_mlp2, v_kv_norm_g, v_w_kv_shared, v_b_f, v_final_norm_g):
    big_w = dict(w_in_a=w_in_a, w_in_b=w_in_b, w_mem_kv=w_mem_kv, w_o=w_o, w_mlp1=w_mlp1, w_mlp2=w_mlp2,
                 w_kv_shared=w_kv_shared)
    small_w = dict(norm1_g=norm1_g, mem_norm_g=mem_norm_g, norm2_g=norm2_g, kv_norm_g=kv_norm_g,
                   final_norm_g=final_norm_g, b_f=b_f)
    big_m = dict(w_in_a=m_w_in_a, w_in_b=m_w_in_b, w_mem_kv=m_w_mem_kv, w_o=m_w_o, w_mlp1=m_w_mlp1,
                 w_mlp2=m_w_mlp2, w_kv_shared=m_w_kv_shared)
    small_m = dict(norm1_g=m_norm1_g, mem_norm_g=m_mem_norm_g, norm2_g=m_norm2_g, kv_norm_g=m_kv_norm_g,
                   final_norm_g=m_final_norm_g, b_f=m_b_f)
    big_v = dict(w_in_a=v_w_in_a, w_in_b=v_w_in_b, w_mem_kv=v_w_mem_kv, w_o=v_w_o, w_mlp1=v_w_mlp1,
                 w_mlp2=v_w_mlp2, w_kv_shared=v_w_kv_shared)
    small_v = dict(norm1_g=v_norm1_g, mem_norm_g=v_mem_norm_g, norm2_g=v_norm2_g, kv_norm_g=v_kv_norm_g,
                   final_norm_g=v_final_norm_g, b_f=v_b_f)

    def pack_all(big, small, dtype):
        return jnp.concatenate([_pack_local(PART_LAYER0, ROWS_LAYER0, big, small, dtype),
                                _pack_local(PART_REST, ROWS_REST, big, None, dtype)], axis=0)

    wb = {n: {} for n in BIG_NAMES}
    _unpack_gathered(PART_LAYER0, _allgather_chips(_pack_local(PART_LAYER0, ROWS_LAYER0, big_w, small_w, BF16)), wb)
    rest_shard = _pack_local(PART_REST, ROWS_REST, big_w, None, BF16)

    loss, dx, gb, gsmall, rest_received = _local_step(x[0], mem[0], wb, rest_shard, small_w, loss_target[0])

    layer0_received = _scatter_chips(_pack_grads(PART_LAYER0, ROWS_LAYER0, gb, gsmall))
    part = jnp.concatenate([_sum4(layer0_received), _sum4(rest_received)], axis=0)
    other = _swap_cores(part)
    g, delta, new_m, new_v = _adamw(part, other, pack_all(big_w, small_w, F32), pack_all(big_m, small_m, F32),
                                    pack_all(big_v, small_v, F32))

    outs = [lax.psum(loss[0, 0], ("x", "y", "c")), dx[None]]
    for packed in (g, delta, new_m, new_v):
        pieces, d = {n: [] for n in BIG_NAMES}, {}
        _unpack_local(PART_LAYER0, packed[:ROWS_LAYER0], True, pieces, d)
        _unpack_local(PART_REST, packed[ROWS_LAYER0:], False, pieces, d)
        d.update(_join_layers(pieces))
        outs.extend(d[n] for n in WEIGHT_ORDER)
    return tuple(outs)
```

```python
import functools
import math

import jax
import jax.numpy as jnp
from jax import lax
from jax.experimental import pallas as pl
from jax.experimental.pallas import tpu as pltpu

F32 = jnp.float32
BF16 = jnp.bfloat16

D_MODEL = 1024
HEAD_DIM = 64
MIX_WIDTH = 512
MEM_WIDTH = 256
MERGED_WIDTH = MIX_WIDTH + MEM_WIDTH
DEPTH = 4
N_A = 2
D_FF = 4096
EPS = 1e-6
NEG_INF = -1e30
SCALE = 1.0 / math.sqrt(HEAD_DIM)

ADAM_LR = 0.001
ADAM_B1 = 0.9
ADAM_B2 = 0.999
ADAM_EPS = 1e-08
ADAM_WD = 0.01
ADAM_STEP = 10

LANES = 128
BQ = 256
BK = 128
DIAG_TILES = BQ // BK
CHAINS = 2
UNDERFLOW_BOUND = -110.0
VMEM_LIMIT = 56 * 1024 * 1024

MESH = pl.DeviceIdType.MESH
N_CHIPS = 4

PARTS = (
    (("w_in_a", 0, 1, (1024, 448), 1),
     ("w_mem_kv", 0, 1, (256, 512), 0)),
    (("w_o", 0, 1, (768, 256), 1),
     ("w_mlp1", 0, 1, (1024, 1024), 1),
     ("w_mlp2", 0, 1, (1024, 1024), 0),
     ("w_in_a", 1, 2, (1024, 448), 1),
     ("w_mem_kv", 1, 2, (256, 512), 0)),
    (("w_o", 1, 4, (768, 256), 1),
     ("w_mlp1", 1, 4, (1024, 1024), 1),
     ("w_mlp2", 1, 4, (1024, 1024), 0),
     ("w_in_b", 0, 2, (256, 768), 0),
     ("w_mem_kv", 2, 4, (256, 512), 0),
     ("w_kv_shared", None, None, (1024, 258), 1)),
)
BIG_NAMES = ("w_in_a", "w_in_b", "w_mem_kv", "w_o", "w_mlp1", "w_mlp2", "w_kv_shared")
SMALL = (
    ("norm1_g", (4, 1024)),
    ("mem_norm_g", (4, 1024)),
    ("norm2_g", (4, 1024)),
    ("kv_norm_g", (1, 1024)),
    ("final_norm_g", (1, 1024)),
    ("b_f", (1, 1024)),
)
WEIGHT_ORDER = ("norm1_g", "w_in_a", "w_in_b", "w_mem_kv", "mem_norm_g", "w_o", "norm2_g", "w_mlp1",
                "w_mlp2", "kv_norm_g", "w_kv_shared", "b_f", "final_norm_g")


ROW_ALIGN = 16
PACK_TILE = 256
SMALL_ROWS = ROW_ALIGN
assert sum(s[0] for _, s in SMALL) <= SMALL_ROWS


def _section_rows(entry):
    _, lo, hi, shape, _ = entry
    rows = (1 if lo is None else hi - lo) * math.prod(shape) // D_MODEL
    return rows, -(-rows // ROW_ALIGN) * ROW_ALIGN


def _round_up(n, m):
    return -(-n // m) * m


SUM_TILE = 128
_used = [sum(_section_rows(e)[1] for e in part) for part in PARTS]
PART_ROWS = [_round_up(_used[0] + SMALL_ROWS, SUM_TILE), _round_up(_used[1], SUM_TILE)]
PART_ROWS.append(_round_up(_used[2] + sum(PART_ROWS), PACK_TILE) - sum(PART_ROWS))
assert PART_ROWS[2] % SUM_TILE == 0


def _params(sem=None):
    return pltpu.CompilerParams(dimension_semantics=sem, vmem_limit_bytes=VMEM_LIMIT)


def _pick(n, cands):
    for c in cands:
        if n % c == 0:
            return c
    raise ValueError(f"no tile for {n}")


def _section(a, entry):
    a = a.reshape(-1, D_MODEL)
    return jnp.pad(a, ((0, _section_rows(entry)[1] - a.shape[0]), (0, 0)))


def _small_block(small, dtype):
    blk = jnp.zeros((SMALL_ROWS, D_MODEL), dtype)
    off = 0
    for n, shp in SMALL:
        a = small[n].astype(dtype)
        if n == "b_f":
            blk = blk.at[off, :a.size].set(a.reshape(-1))
        else:
            blk = blk.at[off:off + shp[0]].set(a.reshape(shp))
        off += shp[0]
    return blk


def _fill(parts, rows, dtype):
    used = sum(p.shape[0] for p in parts)
    return jnp.concatenate(parts + [jnp.zeros((rows - used, D_MODEL), dtype)], axis=0)


def _pack_local(part, rows, big, small, dtype):
    parts = [_section((big[e[0]] if e[1] is None else big[e[0]][e[1]:e[2]]).astype(dtype), e) for e in part]
    if small is not None:
        parts.append(_small_block(small, dtype))
    return _fill(parts, rows, dtype)


def _unpack_local(part, p, with_small, pieces, small):
    off = 0
    for e in part:
        n, lo, hi, shp, _ = e
        rows, reserved = _section_rows(e)
        pieces[n].append((lo, p[off:off + rows].reshape(shp if lo is None else (hi - lo,) + shp)))
        off += reserved
    if with_small:
        for n, shp in SMALL:
            a = p[off:off + shp[0]]
            small[n] = a[0, :8] if n == "b_f" else (a.reshape(D_MODEL) if shp[0] == 1 else a)
            off += shp[0]


def _join_layers(pieces):
    out = {}
    for n, ps in pieces.items():
        ps = sorted(ps, key=lambda t: -1 if t[0] is None else t[0])
        out[n] = ps[0][1] if len(ps) == 1 else jnp.concatenate([a for _, a in ps], axis=0)
    return out


def _unpack_gathered(part, g, weights):
    off = 0
    for e in part:
        n, lo, hi, shp, ax = e
        rows, reserved = _section_rows(e)
        if lo is None:
            sec = g[:, off:off + rows].reshape((N_CHIPS,) + shp)
            weights[n] = jnp.concatenate([sec[j] for j in range(N_CHIPS)], axis=ax)
        else:
            sec = g[:, off:off + rows].reshape((N_CHIPS, hi - lo) + shp)
            for l in range(lo, hi):
                weights[n][l] = jnp.concatenate([sec[j, l - lo] for j in range(N_CHIPS)], axis=ax)
        off += reserved


def _pack_grads(part, rows, gbig, gsmall):
    small = None if gsmall is None else _small_block(gsmall, BF16)
    chunks = []
    for j in range(N_CHIPS):
        parts = []
        for e in part:
            n, lo, hi, shp, ax = e
            w = shp[ax]
            layers = [gbig[n]] if lo is None else [gbig[n][l] for l in range(lo, hi)]
            cut = [lax.slice_in_dim(g, j * w, (j + 1) * w, axis=ax).astype(BF16).reshape(-1, D_MODEL) for g in layers]
            parts.append(_section(cut[0] if len(cut) == 1 else jnp.concatenate(cut, axis=0), e))
        if small is not None:
            parts.append(small)
        chunks.append(_fill(parts, rows, BF16))
    return jnp.stack(chunks, axis=0)


ANY = pl.BlockSpec(memory_space=pl.ANY)


def _other_chips(x, y):
    return [(1 - x, y), (x, 1 - y), (1 - x, 1 - y)]


class _AllGather:
    SCRATCH = [pltpu.SemaphoreType.DMA((3,)), pltpu.SemaphoreType.DMA((3,)), pltpu.SemaphoreType.DMA((3,)),
               pltpu.SemaphoreType.DMA((3,)), pltpu.SemaphoreType.DMA]

    def __init__(self, w_ref, o_ref, send_sems, recv_sems, pass_send, pass_recv, local_sem):
        self.w_ref, self.o_ref = w_ref, o_ref
        self.sems = (send_sems, recv_sems, pass_send, pass_recv, local_sem)
        x, y, c = lax.axis_index("x"), lax.axis_index("y"), lax.axis_index("c")
        half = w_ref.shape[0] // 2
        self.c, self.me, self.sibling = c, 2 * x + y, (x, y, 1 - c)
        self.mine = pl.ds(pl.multiple_of(c * half, ROW_ALIGN), half)
        self.other = pl.ds(pl.multiple_of((1 - c) * half, ROW_ALIGN), half)
        self.chips = _other_chips(x, y)

    def _over_ici(self, j, rows_of):
        chip = self.chips[j]
        return pltpu.make_async_remote_copy(
            src_ref=self.w_ref.at[self.mine], dst_ref=self.o_ref.at[rows_of, self.mine],
            send_sem=self.sems[0].at[j], recv_sem=self.sems[1].at[j],
            device_id=(chip[0], chip[1], self.c), device_id_type=MESH)

    def _over_d2d(self, j, rows):
        where = self.o_ref.at[2 * self.chips[j][0] + self.chips[j][1], rows]
        return pltpu.make_async_remote_copy(src_ref=where, dst_ref=where, send_sem=self.sems[2].at[j],
                                            recv_sem=self.sems[3].at[j], device_id=self.sibling,
                                            device_id_type=MESH)

    def _local(self):
        return pltpu.make_async_copy(self.w_ref, self.o_ref.at[self.me], self.sems[4])

    def start(self):
        self._local().start()
        for j in range(3):
            self._over_ici(j, self.me).start()

    def finish(self):
        for j in range(3):
            self._over_ici(j, 2 * self.chips[j][0] + self.chips[j][1]).wait_recv()
            self._over_d2d(j, self.mine).start()
        for j in range(3):
            self._over_d2d(j, self.other).wait_recv()
        for j in range(3):
            self._over_ici(j, self.me).wait_send()
            self._over_d2d(j, self.mine).wait_send()
        self._local().wait()


class _Scatter:
    SCRATCH = [pltpu.SemaphoreType.DMA((3,)), pltpu.SemaphoreType.DMA((3,)), pltpu.SemaphoreType.DMA]

    def __init__(self, g_ref, o_ref, send_sems, recv_sems, local_sem):
        self.g_ref, self.o_ref, self.sems = g_ref, o_ref, (send_sems, recv_sems, local_sem)
        x, y, c = lax.axis_index("x"), lax.axis_index("y"), lax.axis_index("c")
        self.c, self.me, self.chips = c, 2 * x + y, _other_chips(x, y)

    def _copy(self, j):
        chip = self.chips[j]
        return pltpu.make_async_remote_copy(
            src_ref=self.g_ref.at[2 * chip[0] + chip[1]], dst_ref=self.o_ref.at[self.me],
            send_sem=self.sems[0].at[j], recv_sem=self.sems[1].at[j],
            device_id=(chip[0], chip[1], self.c), device_id_type=MESH)

    def _local(self):
        return pltpu.make_async_copy(self.g_ref.at[self.me], self.o_ref.at[self.me], self.sems[2])

    def start(self):
        self._local().start()
        for j in range(3):
            self._copy(j).start()

    def finish(self):
        for j in range(3):
            self._copy(j).wait()
        self._local().wait()


def _allgather_chips(w):
    def body(w_ref, o_ref, *sems):
        ag = _AllGather(w_ref, o_ref, *sems)
        ag.start()
        ag.finish()

    return pl.pallas_call(
        body, name="allgather_weights",
        out_shape=jax.ShapeDtypeStruct((N_CHIPS,) + w.shape, w.dtype),
        in_specs=[ANY], out_specs=ANY, scratch_shapes=_AllGather.SCRATCH,
    )(w)


def _scatter_chips(g4):
    def body(g_ref, o_ref, *sems):
        sc = _Scatter(g_ref, o_ref, *sems)
        sc.start()
        sc.finish()

    return pl.pallas_call(
        body, name="scatter_grads",
        out_shape=jax.ShapeDtypeStruct(g4.shape, g4.dtype),
        in_specs=[ANY], out_specs=ANY, scratch_shapes=_Scatter.SCRATCH,
    )(g4)


def _swap_cores(p):
    def body(p_ref, o_ref, send_sem, recv_sem):
        x, y, c = lax.axis_index("x"), lax.axis_index("y"), lax.axis_index("c")
        cp = pltpu.make_async_remote_copy(src_ref=p_ref, dst_ref=o_ref, send_sem=send_sem, recv_sem=recv_sem,
                                          device_id=(x, y, 1 - c), device_id_type=MESH)
        cp.start()
        cp.wait()

    return pl.pallas_call(
        body, name="swap_cores",
        out_shape=jax.ShapeDtypeStruct(p.shape, p.dtype),
        in_specs=[ANY], out_specs=ANY,
        scratch_shapes=[pltpu.SemaphoreType.DMA, pltpu.SemaphoreType.DMA],
    )(p)


def _sum4(r4):
    _, R, C = r4.shape

    def body(r_ref, o_ref):
        o_ref[...] = ((r_ref[0].astype(F32) + r_ref[1].astype(F32)) + r_ref[2].astype(F32)) + r_ref[3].astype(F32)

    return pl.pallas_call(
        body, name="sum_chips", grid=(R // SUM_TILE,),
        in_specs=[pl.BlockSpec((N_CHIPS, SUM_TILE, C), lambda i: (0, i, 0))],
        out_specs=pl.BlockSpec((SUM_TILE, C), lambda i: (i, 0)),
        out_shape=jax.ShapeDtypeStruct((R, C), F32),
        compiler_params=_params(("parallel",)),
    )(r4)


def _adamw(pa, pb, w, m, v):
    R, C = w.shape
    c1 = 1.0 - ADAM_B1
    c2 = 1.0 - ADAM_B2
    bc1 = 1.0 - ADAM_B1 ** ADAM_STEP
    bc2 = 1.0 - ADAM_B2 ** ADAM_STEP

    def body(pa_ref, pb_ref, w_ref, m_ref, v_ref, g_ref, d_ref, mo_ref, vo_ref):
        g = pa_ref[...] + pb_ref[...]
        mn = ADAM_B1 * m_ref[...] + c1 * g
        vn = ADAM_B2 * v_ref[...] + c2 * (g * g)
        m_hat = mn / bc1
        v_hat = vn / bc2
        g_ref[...] = g
        d_ref[...] = -ADAM_LR * (m_hat / (jnp.sqrt(v_hat) + ADAM_EPS) + ADAM_WD * w_ref[...])
        mo_ref[...] = mn
        vo_ref[...] = vn

    spec = pl.BlockSpec((PACK_TILE, C), lambda i: (i, 0))
    shp = jax.ShapeDtypeStruct((R, C), F32)
    return pl.pallas_call(
        body, name="adamw", grid=(R // PACK_TILE,),
        in_specs=[spec] * 5, out_specs=[spec] * 4, out_shape=[shp] * 4,
        compiler_params=_params(("parallel",)),
    )(pa, pb, w, m, v)


def _rms_fwd(x, g, name):
    R, Dm = x.shape
    tr = _pick(R, (512, 256, 128))

    def body(x_ref, g_ref, o_ref):
        xf = x_ref[...]
        r = lax.rsqrt(jnp.mean(xf * xf, axis=-1, keepdims=True) + EPS)
        o_ref[...] = (xf * r * g_ref[...]).astype(o_ref.dtype)

    return pl.pallas_call(
        body, name=name, grid=(R // tr,),
        in_specs=[pl.BlockSpec((tr, Dm), lambda i: (i, 0)), pl.BlockSpec((1, Dm), lambda i: (0, 0))],
        out_specs=pl.BlockSpec((tr, Dm), lambda i: (i, 0)),
        out_shape=jax.ShapeDtypeStruct((R, Dm), BF16),
        compiler_params=_params(("parallel",)),
    )(x, g)


def _rms_bwd(x, g, dy, dres, name):
    R, Dm = x.shape
    tr = _pick(R, (256, 128))
    has_res = dres is not None

    def body(*refs):
        if has_res:
            x_ref, g_ref, dy_ref, dres_ref, dx_ref, dg_ref = refs
        else:
            x_ref, g_ref, dy_ref, dx_ref, dg_ref = refs
        xf = x_ref[...]
        dy_ = dy_ref[...].astype(F32)
        r = lax.rsqrt(jnp.mean(xf * xf, axis=-1, keepdims=True) + EPS)
        gdy = dy_ * g_ref[...]
        mdot = jnp.mean(xf * gdy, axis=-1, keepdims=True)
        dx = r * gdy - xf * ((r * r * r) * mdot)
        if has_res:
            dx = dres_ref[...] + dx
        dx_ref[...] = dx

        @pl.when(pl.program_id(0) == 0)
        def _():
            dg_ref[...] = jnp.zeros_like(dg_ref)

        dg_ref[...] += jnp.sum(dy_ * (xf * r), axis=0, keepdims=True)

    row = pl.BlockSpec((tr, Dm), lambda i: (i, 0))
    vec = pl.BlockSpec((1, Dm), lambda i: (0, 0))
    ins = [x, g, dy] + ([dres] if has_res else [])
    return pl.pallas_call(
        body, name=name, grid=(R // tr,),
        in_specs=[row, vec, row] + ([row] if has_res else []),
        out_specs=[row, vec],
        out_shape=[jax.ShapeDtypeStruct((R, Dm), F32), jax.ShapeDtypeStruct((1, Dm), F32)],
        compiler_params=_params(("arbitrary",)),
    )(*ins)


def _final_loss(x, g, tgt):
    R, Dm = x.shape
    tr = _pick(R, (256, 128))

    def body(x_ref, g_ref, t_ref, l_ref, dx_ref, dxb_ref, dg_ref):
        xf = x_ref[...]
        gv = g_ref[...]
        r = lax.rsqrt(jnp.mean(xf * xf, axis=-1, keepdims=True) + EPS)
        xr = xf * r
        err = xr * gv - t_ref[...]
        dy_ = err * (1.0 / Dm)
        gdy = dy_ * gv
        mdot = jnp.mean(xf * gdy, axis=-1, keepdims=True)
        dx = r * gdy - xf * ((r * r * r) * mdot)
        dx_ref[...] = dx
        dxb_ref[...] = dx.astype(BF16)

        @pl.when(pl.program_id(0) == 0)
        def _():
            dg_ref[...] = jnp.zeros_like(dg_ref)
            l_ref[...] = jnp.zeros_like(l_ref)

        dg_ref[...] += jnp.sum(dy_ * xr, axis=0, keepdims=True)
        sq = jnp.sum(err * err, axis=1, keepdims=True)
        l_ref[...] += jnp.sum(sq, axis=0, keepdims=True) * (0.5 / Dm)

    row = pl.BlockSpec((tr, Dm), lambda i: (i, 0))
    vec = pl.BlockSpec((1, Dm), lambda i: (0, 0))
    return pl.pallas_call(
        body, name="final_norm_loss", grid=(R // tr,),
        in_specs=[row, vec, row],
        out_specs=[pl.BlockSpec((1, 1), lambda i: (0, 0)), row, row, vec],
        out_shape=[jax.ShapeDtypeStruct((1, 1), F32), jax.ShapeDtypeStruct((R, Dm), F32),
                   jax.ShapeDtypeStruct((R, Dm), BF16), jax.ShapeDtypeStruct((1, Dm), F32)],
        compiler_params=_params(("arbitrary",)),
    )(x, g, tgt)


MAX_TK = 2048

_DIMS = {"nn": (((1,), (0,)), ((), ())), "nt": (((1,), (1,)), ((), ())), "tn": (((0,), (0,)), ((), ()))}


def _mm(a, b, *, mode="nn", out_dtype=BF16, epi=None, extra=None, name):
    if mode == "nn":
        (M, K), N = a.shape, b.shape[1]
    elif mode == "nt":
        (M, K), N = a.shape, b.shape[0]
    else:
        (K, M), N = a.shape, b.shape[1]
    tm = _pick(M, (512, 256, 128) if epi == "rms_bwd" else (1024, 768, 512, 256, 128))
    tn = _pick(N, (1024, 896, 768, 640, 512, 384, 256, 128))
    tk = K if K <= MAX_TK else _pick(K, (MAX_TK, 1024, 512, 256, 128))
    nk = K // tk
    extras = () if extra is None else (extra if isinstance(extra, tuple) else (extra,))
    n_out = {"relu2": 2, "add_rms": 2, "rms_bwd": 3}.get(epi, 1)
    assert epi not in ("rms_bwd", "add_rms") or tn == N

    def body(*refs):
        a_ref, b_ref = refs[:2]
        e_refs = refs[2:2 + len(extras)]
        e_ref = e_refs[0] if e_refs else None
        outs = refs[2 + len(extras):2 + len(extras) + n_out]
        k = pl.program_id(2)
        part = lax.dot_general(a_ref[...].astype(BF16), b_ref[...].astype(BF16), _DIMS[mode],
                               preferred_element_type=F32)

        def finish(acc):
            if epi is None:
                outs[0][...] = acc.astype(outs[0].dtype)
            elif epi == "add":
                outs[0][...] = (e_ref[...] + acc).astype(outs[0].dtype)
            elif epi == "add_rms":
                y = e_refs[0][...] + acc
                outs[0][...] = y
                r = lax.rsqrt(jnp.mean(y * y, axis=-1, keepdims=True) + EPS)
                outs[1][...] = (y * r * e_refs[1][...]).astype(BF16)
            elif epi == "relu2":
                outs[0][...] = acc.astype(BF16)
                rl = jnp.maximum(acc, 0.0)
                outs[1][...] = (rl * rl).astype(BF16)
            elif epi == "drelu2":
                u = e_ref[...].astype(F32)
                outs[0][...] = (acc * (2.0 * jnp.maximum(u, 0.0))).astype(outs[0].dtype)
            elif epi == "rms_bwd":
                x_ref, g_ref, dres_ref = e_refs
                xf = x_ref[...]
                r = lax.rsqrt(jnp.mean(xf * xf, axis=-1, keepdims=True) + EPS)
                gdy = acc * g_ref[...]
                mdot = jnp.mean(xf * gdy, axis=-1, keepdims=True)
                dx = dres_ref[...] + (r * gdy - xf * ((r * r * r) * mdot))
                outs[0][...] = dx
                outs[1][...] = dx.astype(BF16)

                @pl.when(pl.program_id(0) == 0)
                def _():
                    outs[2][...] = jnp.zeros_like(outs[2])

                outs[2][...] += jnp.sum(acc * (xf * r), axis=0, keepdims=True)

        if nk == 1:
            finish(part)
        else:
            acc_ref = refs[-1]

            @pl.when(k == 0)
            def _():
                acc_ref[...] = part

            @pl.when(jnp.logical_and(k > 0, k < nk - 1))
            def _():
                acc_ref[...] += part

            @pl.when(k == nk - 1)
            def _():
                finish(acc_ref[...] + part)

    if mode == "tn":
        a_spec = pl.BlockSpec((tk, tm), lambda i, j, k: (k, i))
    else:
        a_spec = pl.BlockSpec((tm, tk), lambda i, j, k: (i, k))
    if mode == "nt":
        b_spec = pl.BlockSpec((tn, tk), lambda i, j, k: (j, k))
    else:
        b_spec = pl.BlockSpec((tk, tn), lambda i, j, k: (k, j))
    o_spec = pl.BlockSpec((tm, tn), lambda i, j, k: (i, j))
    vec_spec = pl.BlockSpec((1, tn), lambda i, j, k: (0, j))
    ins, in_specs = [a, b] + list(extras), [a_spec, b_spec]
    if epi == "rms_bwd":
        in_specs += [o_spec, vec_spec, o_spec]
        out_shape = [jax.ShapeDtypeStruct((M, N), F32), jax.ShapeDtypeStruct((M, N), BF16),
                     jax.ShapeDtypeStruct((1, N), F32)]
        out_specs = [o_spec, o_spec, vec_spec]
    elif epi == "add_rms":
        in_specs += [o_spec, vec_spec]
        out_shape = [jax.ShapeDtypeStruct((M, N), F32), jax.ShapeDtypeStruct((M, N), BF16)]
        out_specs = [o_spec, o_spec]
    else:
        in_specs += [o_spec] * len(extras)
        out_shape = [jax.ShapeDtypeStruct((M, N), BF16 if epi == "relu2" else out_dtype)] * n_out
        out_specs = [o_spec] * n_out
    res = pl.pallas_call(
        body, name=name, grid=(M // tm, N // tn, nk),
        in_specs=in_specs, out_specs=out_specs, out_shape=out_shape,
        scratch_shapes=[pltpu.VMEM((tm, tn), F32)] if nk > 1 else [],
        compiler_params=_params(("arbitrary",) * 3 if epi == "rms_bwd" else ("parallel", "parallel", "arbitrary")),
    )(*ins)
    return res if n_out > 1 else res[0]


def _dot(a, b):
    return lax.dot_general(a, b, _DIMS["nn"], preferred_element_type=F32)


def _dot_nt(a, b):
    return lax.dot_general(a, b, _DIMS["nt"], preferred_element_type=F32)


def _dot_tn(a, b):
    return lax.dot_general(a, b, _DIMS["tn"], preferred_element_type=F32)


def _split_dot(x, t):
    hi = x.astype(BF16)
    lo = (x - hi.astype(F32)).astype(BF16)
    return _dot(jnp.concatenate([hi, lo], axis=1), jnp.concatenate([t, t], axis=0))


def _head_pair(ref, scale=None):
    xf = ref[...].astype(F32)
    if scale is not None:
        xf = xf * scale
    is_a = lax.broadcasted_iota(jnp.int32, xf.shape, 1) < HEAD_DIM
    return jnp.where(is_a, xf, 0.0).astype(BF16), jnp.where(is_a, 0.0, xf).astype(BF16)


def _stack(a, b):
    return jnp.concatenate([a, b], axis=0)


def _head_rows(ref, scale=None):
    return _stack(*_head_pair(ref, scale))


def _unstack_heads(x):
    rows = x.shape[0] // 2
    return _select_pair(x[:rows], x[rows:])


def _pair_rowsum(x):
    is_a = lax.broadcasted_iota(jnp.int32, x.shape, 1) < HEAD_DIM
    return (jnp.sum(jnp.where(is_a, x, 0.0), axis=1, keepdims=True),
            jnp.sum(jnp.where(is_a, 0.0, x), axis=1, keepdims=True))


def _select_pair(xa, xb):
    is_a = lax.broadcasted_iota(jnp.int32, xa.shape, 1) < HEAD_DIM
    return jnp.where(is_a, xa, xb)


def _two_cols(xa, xb):
    rows = xa.shape[0]
    first = lax.broadcasted_iota(jnp.int32, (rows, 2), 1) == 0
    return jnp.where(first, xa, xb)


def _softplus_parts(z):
    e = jnp.exp(-jnp.abs(z))
    return jnp.maximum(z, 0.0) + jnp.log(1.0 + e), e


def _tile_iotas():
    row = lax.broadcasted_iota(jnp.int32, (BK, BK), 0)
    col = lax.broadcasted_iota(jnp.int32, (BK, BK), 1)
    return row, col


def _stacked_iotas(bq, nk):
    row = lax.broadcasted_iota(jnp.int32, (2 * bq, nk), 0) & (bq - 1)
    col = lax.broadcasted_iota(jnp.int32, (2 * bq, nk), 1)
    return row, col


def _side_exchange(exchange, operand, n_in, n_out):
    if exchange is None:
        return [], [], [], [], lambda refs: (lambda: None, lambda: None)
    out_shape = jax.ShapeDtypeStruct(((N_CHIPS,) + operand.shape) if exchange is _AllGather else operand.shape,
                                     operand.dtype)
    n_sem = len(exchange.SCRATCH)

    def pick(refs):
        def make():
            return exchange(refs[n_in], refs[n_in + 1 + n_out], *refs[len(refs) - n_sem:])

        return (lambda: make().start()), (lambda: make().finish())

    return [operand], [ANY], [out_shape], [ANY], pick


def _sb_fwd(proj, name, exchange=None, operand=None):
    S = proj.shape[0]
    nqb = S // (CHAINS * BQ)
    x_in, x_in_specs, x_out, x_out_specs, pick = _side_exchange(exchange, operand, 3, 1)

    def body(*refs):
        q_ref, k_ref, v_ref = refs[:3]
        o_ref = refs[3 + len(x_in)]
        acc_ref = refs[3 + len(x_in) + 1 + len(x_out)]
        start, finish = pick(refs)
        p = pl.program_id(0)
        i = pl.program_id(1)

        @pl.when(jnp.logical_and(p == 0, i == 0))
        def _():
            start()

        q2 = [_head_rows(q_ref.at[pl.ds(ch * BQ, BQ)], SCALE) for ch in range(CHAINS)]
        row, col = _tile_iotas()
        tri = (row > col).astype(BF16)
        srow, scol = _stacked_iotas(BQ, BK)
        acc_ref[...] = jnp.zeros_like(acc_ref)

        def tile(ch, kb, c, dmask=None, valid=None):
            r0 = pl.multiple_of(kb * BK, BK)
            kblk = k_ref[pl.ds(r0, BK), :]
            vblk = v_ref[pl.ds(r0, BK), :]
            z = _dot_nt(q2[ch], kblk)
            sp, _ = _softplus_parts(z)
            lm = -sp
            if dmask is not None:
                lm = jnp.where(dmask, lm, 0.0)
            btw = _split_dot(lm, tri)
            w = jnp.exp((z - sp) + btw + c)
            if dmask is not None:
                w = jnp.where(dmask, w, 0.0)
            if valid is not None:
                w = w * valid
            acc_ref[ch] += _dot(w.astype(BF16), vblk)
            return c + btw[:, 0:1] + lm[:, 0:1]

        def alive(c):
            return jnp.max(c) > UNDERFLOW_BOUND

        cs = [jnp.zeros((2 * BQ, 1), F32)] * CHAINS
        for d in reversed(range(DIAG_TILES)):
            cs = [tile(ch, (CHAINS * i + ch) * DIAG_TILES + d, cs[ch], dmask=scol < srow - d * BK)
                  for ch in range(CHAINS)]

        def tile_of(ch, t):
            return (CHAINS * i + ch) * DIAG_TILES - 1 - t

        def more(cs, t):
            go = [jnp.logical_and(alive(cs[ch]), tile_of(ch, t) >= 0) for ch in range(CHAINS)]
            return functools.reduce(jnp.logical_or, go).astype(jnp.int32)

        def step(st):
            t, _, cs = st
            new = []
            for ch in range(CHAINS):
                kb = tile_of(ch, t)
                if ch == CHAINS - 1:
                    new.append(tile(ch, kb, cs[ch]))
                else:
                    new.append(tile(ch, jnp.maximum(kb, 0), cs[ch], valid=(kb >= 0).astype(F32)))
            return t + 1, more(new, t + 1), new

        lax.while_loop(lambda st: st[1] > 0, step, (0, more(cs, 0), cs))
        for ch in range(CHAINS):
            o_ref[pl.ds(ch * BQ, BQ), :] = _unstack_heads(acc_ref[ch])

        @pl.when(jnp.logical_and(p == 3, i == nqb - 1))
        def _():
            finish()

    blk = pl.BlockSpec((CHAINS * BQ, LANES), lambda p, i: (i, p))
    res = pl.pallas_call(
        body, name=name, grid=(4, nqb),
        in_specs=[blk, pl.BlockSpec((S, LANES), lambda p, i: (0, 4 + p)),
                  pl.BlockSpec((S, LANES), lambda p, i: (0, 8 + p))] + x_in_specs,
        out_specs=[blk] + x_out_specs,
        out_shape=[jax.ShapeDtypeStruct((S, MERGED_WIDTH), F32)] + x_out,
        scratch_shapes=[pltpu.VMEM((CHAINS, 2 * BQ, LANES), F32)] + ([] if exchange is None else exchange.SCRATCH),
        compiler_params=_params(("arbitrary", "arbitrary")),
    )(proj, proj, proj, *x_in)
    return res if exchange is not None else res[0]


def _sb_bwd(proj, merged, dmerged, name, exchange=None, operand=None):
    S = proj.shape[0]
    nqb = S // (CHAINS * BQ)
    x_in, x_in_specs, x_out, x_out_specs, pick = _side_exchange(exchange, operand, 5, 3)

    def body(*refs):
        q_ref, k_ref, v_ref, o_ref, do_ref = refs[:5]
        dq_ref, dk_hbm, dv_hbm = refs[5 + len(x_in):8 + len(x_in)]
        dq_acc, dk_acc, dv_acc, sem = refs[8 + len(x_in) + len(x_out):12 + len(x_in) + len(x_out)]
        start, finish = pick(refs)
        p = pl.program_id(0)
        i = pl.program_id(1)

        @pl.when(jnp.logical_and(p == 0, i == 0))
        def _():
            start()

        @pl.when(i == 0)
        def _():
            dk_acc[...] = jnp.zeros_like(dk_acc)
            dv_acc[...] = jnp.zeros_like(dv_acc)

        rows = [pl.ds(ch * BQ, BQ) for ch in range(CHAINS)]
        q2 = [_head_rows(q_ref.at[rw], SCALE) for rw in rows]
        do2 = [_head_rows(do_ref.at[rw]) for rw in rows]
        tot = [_stack(*_pair_rowsum(do_ref[rw, :].astype(F32) * o_ref[rw, :])) for rw in rows]
        row, col = _tile_iotas()
        tri_gt = (row > col).astype(BF16)
        tri_ge = (row >= col).astype(BF16)
        srow, scol = _stacked_iotas(BQ, BK)
        dq_acc[...] = jnp.zeros_like(dq_acc)

        def tile(ch, kb, st, dmask=None, valid=None):
            masked = dmask is not None
            c, r = st
            r0 = pl.multiple_of(kb * BK, BK)
            kblk = k_ref[pl.ds(r0, BK), :]
            vblk = v_ref[pl.ds(r0, BK), :]
            z = _dot_nt(q2[ch], kblk)
            sp, e = _softplus_parts(z)
            lm = -sp
            if masked:
                lm = jnp.where(dmask, lm, 0.0)
            btw = _split_dot(lm, tri_gt)
            w = jnp.exp((z - sp) + btw + c)
            if masked:
                w = jnp.where(dmask, w, 0.0)
            if valid is not None:
                w = w * valid
            wb = w.astype(BF16)
            a = wb.astype(F32) * _dot_nt(do2[ch], vblk)
            suffix = _split_dot(a, tri_ge) + r
            rcp = 1.0 / (1.0 + e)
            pos = z >= 0.0
            sig = jnp.where(pos, rcp, e * rcp)
            sig_neg = jnp.where(pos, e * rcp, rcp)
            dz = a * sig_neg - (tot[ch] - suffix) * sig
            if masked:
                dz = jnp.where(dmask, dz, 0.0)
            if valid is not None:
                dz = dz * valid
            dzb = dz.astype(BF16)
            dq_acc[ch] += _dot(dzb, kblk)
            dk_acc[pl.ds(r0, BK), :] += _dot_tn(dzb, q2[ch])
            dv_acc[pl.ds(r0, BK), :] += _dot_tn(wb, do2[ch])
            return c + btw[:, 0:1] + lm[:, 0:1], suffix[:, 0:1]

        def alive(st):
            return jnp.max(st[0]) > UNDERFLOW_BOUND

        zero = jnp.zeros((2 * BQ, 1), F32)
        sts = [(zero, zero)] * CHAINS
        for d in reversed(range(DIAG_TILES)):
            sts = [tile(ch, (CHAINS * i + ch) * DIAG_TILES + d, sts[ch], dmask=scol < srow - d * BK)
                   for ch in range(CHAINS)]

        def tile_of(ch, t):
            return (CHAINS * i + ch) * DIAG_TILES - 1 - t

        def more(sts, t):
            go = [jnp.logical_and(alive(sts[ch]), tile_of(ch, t) >= 0) for ch in range(CHAINS)]
            return functools.reduce(jnp.logical_or, go).astype(jnp.int32)

        def step(s):
            t, _, sts = s
            new = []
            for ch in range(CHAINS):
                kb = tile_of(ch, t)
                if ch == CHAINS - 1:
                    new.append(tile(ch, kb, sts[ch]))
                else:
                    new.append(tile(ch, jnp.maximum(kb, 0), sts[ch], valid=(kb >= 0).astype(F32)))
            return t + 1, more(new, t + 1), new

        lax.while_loop(lambda s: s[1] > 0, step, (0, more(sts, 0), sts))
        for ch in range(CHAINS):
            dq_ref[rows[ch], :] = (_unstack_heads(dq_acc[ch]) * SCALE).astype(dq_ref.dtype)

        @pl.when(i == nqb - 1)
        def _():
            ck = pltpu.make_async_copy(dk_acc, dk_hbm.at[p], sem.at[0])
            cv = pltpu.make_async_copy(dv_acc, dv_hbm.at[p], sem.at[1])
            ck.start()
            cv.start()
            ck.wait()
            cv.wait()

        @pl.when(jnp.logical_and(p == 3, i == nqb - 1))
        def _():
            finish()

    blk = lambda off: pl.BlockSpec((CHAINS * BQ, LANES), lambda p, i: (i, off + p))
    slab = lambda off: pl.BlockSpec((S, LANES), lambda p, i: (0, off + p))
    return pl.pallas_call(
        body, name=name, grid=(4, nqb),
        in_specs=[blk(0), slab(4), slab(8), blk(0), blk(0)] + x_in_specs,
        out_specs=[blk(0), ANY, ANY] + x_out_specs,
        out_shape=[jax.ShapeDtypeStruct((S, MIX_WIDTH), BF16),
                   jax.ShapeDtypeStruct((4, S, LANES), F32), jax.ShapeDtypeStruct((4, S, LANES), F32)] + x_out,
        scratch_shapes=[pltpu.VMEM((CHAINS, 2 * BQ, LANES), F32), pltpu.VMEM((S, LANES), F32),
                        pltpu.VMEM((S, LANES), F32), pltpu.SemaphoreType.DMA((2,))]
        + ([] if exchange is None else exchange.SCRATCH),
        compiler_params=_params(("arbitrary", "arbitrary")),
    )(proj, proj, proj, merged, dmerged, *x_in)


def _key_norm_max(k_ref, knorm_ref, nkb):
    def step(kb, m):
        r0 = pl.multiple_of(kb * BK, BK)
        blk = k_ref[pl.ds(r0, BK), :].astype(F32)
        sa, sb = _pair_rowsum(blk * blk)
        return (jnp.maximum(m[0], jnp.max(sa, axis=0, keepdims=True)),
                jnp.maximum(m[1], jnp.max(sb, axis=0, keepdims=True)))

    zero = jnp.zeros((1, 1), F32)
    ma, mb = lax.fori_loop(0, nkb, step, (zero, zero))
    knorm_ref[...] = _select_pair(jnp.broadcast_to(ma, (1, LANES)), jnp.broadcast_to(mb, (1, LANES)))


FQ = 512
FK = FQ
GATE_BLOCKS = FK // BK


def _key_gates(cr_ref, kb):
    blocks = [cr_ref[0, GATE_BLOCKS * kb + j] for j in range(GATE_BLOCKS)]
    per_head = [jnp.broadcast_to(jnp.concatenate([b[h:h + 1] for b in blocks], axis=1), (FQ, FK)) for h in range(2)]
    return _stack(*per_head)


def _last_gate(cr_ref, kb):
    last = cr_ref[0, GATE_BLOCKS * jnp.maximum(kb, 0) + GATE_BLOCKS - 1]
    return _stack(*[jnp.broadcast_to(last[h:h + 1, BK - 1:BK], (FQ, 1)) for h in range(2)])


def _logit_bound(q_ref, knorm_ref):
    qf = q_ref[...].astype(F32) * SCALE
    qa, qb = _pair_rowsum(qf * qf)
    kn = knorm_ref[...]
    return _stack(jnp.sqrt(qa * kn[:, 0:1]), jnp.sqrt(qb * kn[:, HEAD_DIM:HEAD_DIM + 1]))


def _causal_bias(bias_ref):
    srow, scol = _stacked_iotas(FQ, FK)
    bias_ref[...] = jnp.where(scol <= srow, 0.0, NEG_INF)


def _fox_fwd(proj, kv, c_col, c_row, name):
    S = proj.shape[0]
    nqb = S // FQ

    def body(q_ref, k_ref, v_ref, cc_ref, cr_ref, o_ref, lse_ref, acc_ref, knorm_ref, bias_ref):
        i = pl.program_id(1)

        @pl.when(i == 0)
        def _():
            _key_norm_max(k_ref, knorm_ref, S // BK)
            _causal_bias(bias_ref)

        q2 = _head_rows(q_ref, SCALE)
        bound = _logit_bound(q_ref, knorm_ref)
        cc = cc_ref[0]
        ct = _stack(cc[:, 0:1], cc[:, 1:2])
        acc_ref[...] = jnp.zeros_like(acc_ref)

        def tile(kb, st, masked):
            m, l = st
            r0 = pl.multiple_of(kb * FK, FK)
            kblk = k_ref[pl.ds(r0, FK), :]
            vblk = v_ref[pl.ds(r0, FK), :]
            z = _dot_nt(q2, kblk) + ct - _key_gates(cr_ref, kb)
            if masked:
                z = z + bias_ref[...]
            m_new = jnp.maximum(m, jnp.max(z, axis=1, keepdims=True))
            alpha = jnp.exp(m - m_new)
            pr = jnp.exp(z - m_new)
            acc_ref[...] = alpha * acc_ref[...] + _split_dot(pr, vblk)
            return m_new, alpha * l + jnp.sum(pr, axis=1, keepdims=True)

        def alive(kb, st):
            reach = bound + ct - _last_gate(cr_ref, kb) - st[0]
            return (jnp.max(reach) > UNDERFLOW_BOUND).astype(jnp.int32)

        neg = jnp.full((2 * FQ, 1), NEG_INF, F32)
        zero = jnp.zeros((2 * FQ, 1), F32)
        st0 = tile(i, (neg, zero), True)

        def cond(s):
            return jnp.logical_and(s[0] >= 0, s[1] > 0)

        def step(s):
            kb, _, st = s
            st = tile(kb, st, False)
            return kb - 1, alive(kb - 1, st), st

        _, _, (m, l) = lax.while_loop(cond, step, (i - 1, alive(i - 1, st0), st0))
        o_ref[...] = _unstack_heads(acc_ref[...] / l)
        lse = m + jnp.log(l)
        lse_ref[0] = _two_cols(lse[:FQ], lse[FQ:])

    return pl.pallas_call(
        body, name=name, grid=(4, nqb),
        in_specs=[pl.BlockSpec((FQ, LANES), lambda p, i: (i, p)),
                  pl.BlockSpec((S, LANES), lambda p, i: (0, p)),
                  pl.BlockSpec((S, LANES), lambda p, i: (0, 4 + p)),
                  pl.BlockSpec((1, FQ, 2), lambda p, i: (p, i, 0)),
                  pl.BlockSpec((1, S // BK, 8, LANES), lambda p, i: (p, 0, 0, 0))],
        out_specs=[pl.BlockSpec((FQ, LANES), lambda p, i: (i, p)),
                   pl.BlockSpec((1, FQ, 2), lambda p, i: (p, i, 0))],
        out_shape=[jax.ShapeDtypeStruct((S, MERGED_WIDTH), F32), jax.ShapeDtypeStruct((4, S, 2), F32)],
        scratch_shapes=[pltpu.VMEM((2 * FQ, LANES), F32), pltpu.VMEM((1, LANES), F32),
                        pltpu.VMEM((2 * FQ, FK), F32)],
        compiler_params=_params(("arbitrary", "arbitrary")),
    )(proj, kv, kv, c_col, c_row)


def _fox_bwd(proj, kv, c_col, c_row, lse, merged, dmerged, dk_prev, dv_prev, dc_prev, name):
    S = proj.shape[0]
    nqb = S // FQ

    def body(q_ref, k_ref, v_ref, cc_ref, cr_ref, lse_ref, o_ref, do_ref, dkp_hbm, dvp_hbm, dcp_ref,
             dq_ref, dk_hbm, dv_hbm, dc_ref, dq_acc, dk_acc, dv_acc, knorm_ref, bias_ref, sem):
        p = pl.program_id(0)
        i = pl.program_id(1)

        @pl.when(i == 0)
        def _():
            ck = pltpu.make_async_copy(dkp_hbm.at[p], dk_acc, sem.at[0])
            cv = pltpu.make_async_copy(dvp_hbm.at[p], dv_acc, sem.at[1])
            ck.start()
            cv.start()
            dc_ref[...] = dcp_ref[...]
            _key_norm_max(k_ref, knorm_ref, S // BK)
            _causal_bias(bias_ref)
            ck.wait()
            cv.wait()

        q2 = _head_rows(q_ref, SCALE)
        do2 = _head_rows(do_ref)
        tot = _stack(*_pair_rowsum(do_ref[...].astype(F32) * o_ref[...]))
        bound = _logit_bound(q_ref, knorm_ref)
        cc = cc_ref[0]
        ct = _stack(cc[:, 0:1], cc[:, 1:2])
        ls = lse_ref[0]
        lse = _stack(ls[:, 0:1], ls[:, 1:2])
        sub = lax.broadcasted_iota(jnp.int32, (8, LANES), 0)
        dq_acc[...] = jnp.zeros_like(dq_acc)

        def tile(kb, masked):
            r0 = pl.multiple_of(kb * FK, FK)
            kblk = k_ref[pl.ds(r0, FK), :]
            vblk = v_ref[pl.ds(r0, FK), :]
            z = _dot_nt(q2, kblk) + ct - _key_gates(cr_ref, kb)
            if masked:
                z = z + bias_ref[...]
            pr = jnp.exp(z - lse)
            ds = pr * (_dot_nt(do2, vblk) - tot)
            dsb = ds.astype(BF16)
            dq_acc[...] += _dot(dsb, kblk)
            dk_acc[pl.ds(r0, FK), :] += _dot_tn(dsb, q2)
            dv_acc[pl.ds(r0, FK), :] += _dot_tn(pr.astype(BF16), do2)
            dca = jnp.sum(ds[:FQ], axis=0, keepdims=True)
            dcb = jnp.sum(ds[FQ:], axis=0, keepdims=True)
            for j in range(GATE_BLOCKS):
                cols = slice(j * BK, (j + 1) * BK)
                old = dc_ref[0, GATE_BLOCKS * kb + j]
                dc_ref[0, GATE_BLOCKS * kb + j] = jnp.where(sub == 0, old - dca[:, cols],
                                                            jnp.where(sub == 1, old - dcb[:, cols], old))

        def alive(kb):
            reach = bound + ct - _last_gate(cr_ref, kb) - lse
            return (jnp.max(reach) > UNDERFLOW_BOUND).astype(jnp.int32)

        tile(i, True)

        def cond(s):
            return jnp.logical_and(s[0] >= 0, s[1] > 0)

        def step(s):
            kb, _ = s
            tile(kb, False)
            return kb - 1, alive(kb - 1)

        lax.while_loop(cond, step, (i - 1, alive(i - 1)))
        dq_ref[...] = (_unstack_heads(dq_acc[...]) * SCALE).astype(dq_ref.dtype)

        @pl.when(i == nqb - 1)
        def _():
            ck = pltpu.make_async_copy(dk_acc, dk_hbm.at[p], sem.at[0])
            cv = pltpu.make_async_copy(dv_acc, dv_hbm.at[p], sem.at[1])
            ck.start()
            cv.start()
            ck.wait()
            cv.wait()

    blk = lambda off: pl.BlockSpec((FQ, LANES), lambda p, i: (i, off + p))
    slab = lambda off: pl.BlockSpec((S, LANES), lambda p, i: (0, off + p))
    cols = pl.BlockSpec((1, FQ, 2), lambda p, i: (p, i, 0))
    rows = pl.BlockSpec((1, S // BK, 8, LANES), lambda p, i: (p, 0, 0, 0))
    return pl.pallas_call(
        body, name=name, grid=(4, nqb),
        in_specs=[blk(0), slab(0), slab(4), cols, rows, cols, blk(0), blk(0), ANY, ANY, rows],
        out_specs=[blk(0), ANY, ANY, rows],
        out_shape=[jax.ShapeDtypeStruct((S, MIX_WIDTH), BF16),
                   jax.ShapeDtypeStruct((4, S, LANES), F32), jax.ShapeDtypeStruct((4, S, LANES), F32),
                   jax.ShapeDtypeStruct((4, S // BK, 8, LANES), F32)],
        scratch_shapes=[pltpu.VMEM((2 * FQ, LANES), F32), pltpu.VMEM((S, LANES), F32),
                        pltpu.VMEM((S, LANES), F32), pltpu.VMEM((1, LANES), F32),
                        pltpu.VMEM((2 * FQ, FK), F32), pltpu.SemaphoreType.DMA((2,))],
        compiler_params=_params(("arbitrary", "arbitrary")),
    )(proj, kv, kv, c_col, c_row, lse, merged, dmerged, dk_prev, dv_prev, dc_prev)


def _lane_scan(x, reverse):
    lane = lax.broadcasted_iota(jnp.int32, x.shape, 1)
    d = 1
    while d < LANES:
        if reverse:
            x = x + jnp.where(lane < LANES - d, pltpu.roll(x, LANES - d, 1), 0.0)
        else:
            x = x + jnp.where(lane >= d, pltpu.roll(x, d, 1), 0.0)
        d *= 2
    return x


def _gate_fwd(fl3, b8):
    nb = fl3.shape[0]

    def body(fl_ref, b_ref, c_ref):
        def step(kb, carry):
            x = fl_ref[kb] + b_ref[...]
            sp, _ = _softplus_parts(-x)
            c = _lane_scan(-sp, False) + carry
            c_ref[kb] = c
            return c[:, LANES - 1:LANES]

        lax.fori_loop(0, nb, step, jnp.zeros((8, 1), F32))

    return pl.pallas_call(body, name="forget_gate_cumsum",
                          out_shape=jax.ShapeDtypeStruct(fl3.shape, F32),
                          compiler_params=_params())(fl3, b8)


def _gate_bwd(dc3, fl3, b8):
    nb = fl3.shape[0]

    def body(dc_ref, fl_ref, b_ref, dfl_ref, db_ref):
        def step(t, st):
            carry, dbs = st
            kb = nb - 1 - t
            g = _lane_scan(dc_ref[kb], True) + carry
            x = fl_ref[kb] + b_ref[...]
            e = jnp.exp(-jnp.abs(x))
            rcp = 1.0 / (1.0 + e)
            dfl = g * jnp.where(x >= 0.0, e * rcp, rcp)
            dfl_ref[kb] = dfl
            return g[:, 0:1], dbs + dfl

        _, dbs = lax.fori_loop(0, nb, step, (jnp.zeros((8, 1), F32), jnp.zeros((8, LANES), F32)))
        db_ref[...] = jnp.broadcast_to(jnp.sum(dbs, axis=1, keepdims=True), (8, LANES))

    return pl.pallas_call(body, name="forget_gate_bwd",
                          out_shape=[jax.ShapeDtypeStruct(fl3.shape, F32), jax.ShapeDtypeStruct((8, LANES), F32)],
                          compiler_params=_params())(dc3, fl3, b8)


MEM_TQ = 512


def _mem_fwd(proj, qcol, mkv, mix, name):
    S = proj.shape[0]
    M = mkv.shape[0]

    def body(q_ref, mk_ref, mv_ref, mix_ref, o_ref, lse_ref):
        q2 = _head_rows(q_ref, SCALE)
        s = _dot_nt(q2, mk_ref[...])
        m = jnp.max(s, axis=1, keepdims=True)
        pr = jnp.exp(s - m)
        l = jnp.sum(pr, axis=1, keepdims=True)
        o_ref[...] = _unstack_heads(_dot(pr.astype(BF16), mv_ref[...]) / l)
        lse = m + jnp.log(l)
        lse_ref[0] = _two_cols(lse[:MEM_TQ], lse[MEM_TQ:])

    return pl.pallas_call(
        body, name=name, grid=(2, S // MEM_TQ),
        in_specs=[pl.BlockSpec((MEM_TQ, LANES), lambda p, i: (i, qcol + p)),
                  pl.BlockSpec((M, LANES), lambda p, i: (0, p)),
                  pl.BlockSpec((M, LANES), lambda p, i: (0, 2 + p)), ANY],
        out_specs=[pl.BlockSpec((MEM_TQ, LANES), lambda p, i: (i, 4 + p)),
                   pl.BlockSpec((1, MEM_TQ, 2), lambda p, i: (p, i, 0))],
        out_shape=[jax.ShapeDtypeStruct((S, MERGED_WIDTH), F32), jax.ShapeDtypeStruct((2, S, 2), F32)],
        input_output_aliases={3: 0},
        compiler_params=_params(("parallel", "parallel")),
    )(proj, mkv, mkv, mix)


def _mem_bwd(proj, qcol, mkv, lse, merged, dmerged, name):
    S = proj.shape[0]
    M = mkv.shape[0]

    def body(q_ref, mk_ref, mv_ref, lse_ref, o_ref, do_ref, dq_ref, dmk_ref, dmv_ref):
        @pl.when(pl.program_id(1) == 0)
        def _():
            dmk_ref[...] = jnp.zeros_like(dmk_ref)
            dmv_ref[...] = jnp.zeros_like(dmv_ref)

        q2 = _head_rows(q_ref, SCALE)
        do2 = _head_rows(do_ref)
        tot = _stack(*_pair_rowsum(do_ref[...].astype(F32) * o_ref[...]))
        ls = lse_ref[0]
        pr = jnp.exp(_dot_nt(q2, mk_ref[...]) - _stack(ls[:, 0:1], ls[:, 1:2]))
        ds = pr * (_dot_nt(do2, mv_ref[...]) - tot)
        dsb = ds.astype(BF16)
        dmk_ref[...] += _dot_tn(dsb, q2)
        dmv_ref[...] += _dot_tn(pr.astype(BF16), do2)
        dq_ref[...] = (_unstack_heads(_dot(dsb, mk_ref[...])) * SCALE).astype(dq_ref.dtype)

    blk = lambda off: pl.BlockSpec((MEM_TQ, LANES), lambda p, i: (i, off + p))
    acc = pl.BlockSpec((M, LANES), lambda p, i: (0, p))
    return pl.pallas_call(
        body, name=name, grid=(2, S // MEM_TQ),
        in_specs=[blk(qcol), pl.BlockSpec((M, LANES), lambda p, i: (0, p)),
                  pl.BlockSpec((M, LANES), lambda p, i: (0, 2 + p)),
                  pl.BlockSpec((1, MEM_TQ, 2), lambda p, i: (p, i, 0)), blk(4), blk(4)],
        out_specs=[blk(0), acc, acc],
        out_shape=[jax.ShapeDtypeStruct((S, MEM_WIDTH), BF16), jax.ShapeDtypeStruct((M, MEM_WIDTH), F32),
                   jax.ShapeDtypeStruct((M, MEM_WIDTH), F32)],
        compiler_params=_params(("parallel", "arbitrary")),
    )(proj, mkv, mkv, lse, merged, dmerged)


def _c_layouts(c3):
    nb = c3.shape[0]
    pairs = c3.reshape(nb, 4, 2, LANES).transpose(1, 0, 2, 3)
    c_row = jnp.pad(pairs, ((0, 0), (0, 0), (0, 6), (0, 0)))
    c_col = pairs.transpose(0, 1, 3, 2).reshape(4, nb * LANES, 2)
    return c_col, c_row


def _local_step(x, mem, wb, shards, sm, loss_target):
    S = x.shape[0]
    nb = S // BK
    vec = lambda a: a.reshape(1, D_MODEL)
    b8 = jnp.broadcast_to(sm["b_f"].reshape(8, 1), (8, LANES))

    saved = []
    shared = None
    h = x
    hn = _rms_fwd(h, vec(sm["norm1_g"][0]), "norm1_0")
    for l in range(DEPTH):
        if l == N_A:
            w_kvf = jnp.pad(wb["w_kv_shared"], ((0, 0), (0, 1152 - 1032)))
            hs = _rms_fwd(h, vec(sm["kv_norm_g"]), "kv_norm")
            kvf = _mm(hs, w_kvf, out_dtype=F32, name="kv_shared_proj")
            kv = kvf[:, :2 * MIX_WIDTH].astype(BF16)
            fl3 = kvf[:, 2 * MIX_WIDTH:2 * MIX_WIDTH + 8].T.reshape(8, nb, LANES).transpose(1, 0, 2)
            c3 = _gate_fwd(fl3, b8)
            c_col, c_row = _c_layouts(c3)
            shared = dict(h=h, hs=hs, kv=kv, fl3=fl3, c_col=c_col, c_row=c_row)
        mn = _rms_fwd(mem, vec(sm["mem_norm_g"][l]), f"mem_norm_{l}")
        mkv = _mm(mn, wb["w_mem_kv"][l], name=f"mem_kv_proj_{l}")
        if l < N_A:
            w_in = wb["w_in_a"][l]
            proj = _mm(hn, w_in, name=f"in_proj_{l}")
            mix, gathered = _sb_fwd(proj, f"stickbreak_fwd_{l}", _AllGather, shards[l])
            _unpack_gathered(PARTS[1 + l], gathered, wb)
            lse, qcol = None, 12
        else:
            w_in = wb["w_in_b"][l - N_A]
            proj = _mm(hn, w_in, name=f"in_proj_{l}")
            mix, lse = _fox_fwd(proj, shared["kv"], shared["c_col"], shared["c_row"], f"fox_fwd_{l}")
            qcol = 4
        merged, mlse = _mem_fwd(proj, qcol, mkv, mix, f"mem_attn_fwd_{l}")
        h_mid, hn2 = _mm(merged, wb["w_o"][l], epi="add_rms", extra=(h, vec(sm["norm2_g"][l])),
                         name=f"out_proj_{l}")
        u, act = _mm(hn2, wb["w_mlp1"][l], epi="relu2", name=f"mlp1_{l}")
        saved.append(dict(h=h, hn=hn, mn=mn, mkv=mkv, proj=proj, lse=lse, mlse=mlse, qcol=qcol, merged=merged,
                          h_mid=h_mid, hn2=hn2, u=u, act=act, w_in=w_in))
        if l + 1 < DEPTH:
            h, hn = _mm(act, wb["w_mlp2"][l], epi="add_rms", extra=(h_mid, vec(sm["norm1_g"][l + 1])),
                        name=f"mlp2_{l}")
        else:
            h = _mm(act, wb["w_mlp2"][l], out_dtype=F32, epi="add", extra=h_mid, name=f"mlp2_{l}")

    loss, dh, dhb, dg_final = _final_loss(h, vec(sm["final_norm_g"]), loss_target)

    gb = {n: [None] * (DEPTH if n not in ("w_in_a", "w_in_b") else 2) for n in
          ("w_in_a", "w_in_b", "w_mem_kv", "w_o", "w_mlp1", "w_mlp2")}
    gs = {n: [None] * DEPTH for n in ("norm1_g", "mem_norm_g", "norm2_g")}
    received = [None] * N_A
    dk_sh = jnp.zeros((4, S, LANES), F32)
    dv_sh = jnp.zeros((4, S, LANES), F32)
    dc_sh = jnp.zeros((4, nb, 8, LANES), F32)
    for l in reversed(range(DEPTH)):
        sv = saved[l]
        du = _mm(dhb, wb["w_mlp2"][l], mode="nt", epi="drelu2", extra=sv["u"], name=f"mlp2_dx_{l}")
        gb["w_mlp2"][l] = _mm(sv["act"], dhb, mode="tn", out_dtype=F32, name=f"mlp2_dw_{l}")
        gb["w_mlp1"][l] = _mm(sv["hn2"], du, mode="tn", out_dtype=F32, name=f"mlp1_dw_{l}")
        dh, dhb, gs["norm2_g"][l] = _mm(du, wb["w_mlp1"][l], mode="nt", epi="rms_bwd",
                                        extra=(sv["h_mid"], vec(sm["norm2_g"][l]), dh),
                                        name=f"mlp1_dx_norm2_bwd_{l}")
        dmerged = _mm(dhb, wb["w_o"][l], mode="nt", name=f"out_proj_dx_{l}")
        gb["w_o"][l] = _mm(sv["merged"], dhb, mode="tn", out_dtype=F32, name=f"out_proj_dw_{l}")
        if l < N_A:
            ready = _pack_grads(PARTS[1 + l], PART_ROWS[1 + l], gb, None)
            dq, dk, dv, received[l] = _sb_bwd(sv["proj"], sv["merged"], dmerged, f"stickbreak_bwd_{l}",
                                              _Scatter, ready)
        else:
            dq, dk_sh, dv_sh, dc_sh = _fox_bwd(sv["proj"], shared["kv"], shared["c_col"], shared["c_row"],
                                               sv["lse"], sv["merged"], dmerged, dk_sh, dv_sh, dc_sh,
                                               f"fox_bwd_{l}")
        dqm, dmk, dmv = _mem_bwd(sv["proj"], sv["qcol"], sv["mkv"], sv["mlse"], sv["merged"], dmerged,
                                 f"mem_attn_bwd_{l}")
        if l < N_A:
            flat = lambda t: t.transpose(1, 0, 2).reshape(S, MIX_WIDTH).astype(BF16)
            dproj = jnp.concatenate([dq, flat(dk), flat(dv), dqm], axis=1)
        else:
            dproj = jnp.concatenate([dq, dqm], axis=1)
        name_in = "w_in_a" if l < N_A else "w_in_b"
        gb[name_in][l if l < N_A else l - N_A] = _mm(sv["hn"], dproj, mode="tn", out_dtype=F32,
                                                      name=f"in_proj_dw_{l}")
        dh, dhb, gs["norm1_g"][l] = _mm(dproj, sv["w_in"], mode="nt", epi="rms_bwd",
                                        extra=(sv["h"], vec(sm["norm1_g"][l]), dh),
                                        name=f"in_proj_dx_norm1_bwd_{l}")
        dmkv = jnp.concatenate([dmk, dmv], axis=1)
        gb["w_mem_kv"][l] = _mm(sv["mn"], dmkv, mode="tn", out_dtype=F32, name=f"mem_kv_dw_{l}")
        dmn = _mm(dmkv, wb["w_mem_kv"][l], mode="nt", out_dtype=F32, name=f"mem_kv_dx_{l}")
        _, gs["mem_norm_g"][l] = _rms_bwd(mem, vec(sm["mem_norm_g"][l]), dmn, None, f"mem_norm_bwd_{l}")
        if l == N_A:
            dfl3, db8 = _gate_bwd(dc_sh.reshape(4, nb, 8, LANES)[:, :, :2].transpose(1, 0, 2, 3).reshape(nb, 8, LANES),
                                  shared["fl3"], b8)
            dfl = dfl3.transpose(1, 0, 2).reshape(8, S).T
            flat = lambda t: t.transpose(1, 0, 2).reshape(S, MIX_WIDTH).astype(BF16)
            dkvf = jnp.concatenate([flat(dk_sh), flat(dv_sh),
                                    jnp.pad(dfl, ((0, 0), (0, LANES - 8))).astype(BF16)], axis=1)
            gb["w_kv_shared"] = _mm(shared["hs"], dkvf, mode="tn", out_dtype=F32, name="kv_shared_dw")[:, :1032]
            dh, dhb, g_kvn = _mm(dkvf, w_kvf, mode="nt", epi="rms_bwd",
                                 extra=(shared["h"], vec(sm["kv_norm_g"]), dh), name="kv_shared_dx_norm_bwd")
            g_bf = db8[:, 0]

    gsmall = {n: jnp.concatenate(v, axis=0) for n, v in gs.items()}
    gsmall["kv_norm_g"] = g_kvn
    gsmall["final_norm_g"] = dg_final
    gsmall["b_f"] = g_bf
    return loss, dh, gb, gsmall, received


def kernel(x, mem, norm1_g, w_in_a, w_in_b, w_mem_kv, mem_norm_g, w_o, norm2_g, w_mlp1, w_mlp2, kv_norm_g, w_kv_shared, b_f, final_norm_g, loss_target, m_norm1_g, m_w_in_a, m_w_in_b, m_w_mem_kv, m_mem_norm_g, m_w_o, m_norm2_g, m_w_mlp1, m_w_mlp2, m_kv_norm_g, m_w_kv_shared, m_b_f, m_final_norm_g, v_norm1_g, v_w_in_a, v_w_in_b, v_w_mem_kv, v_mem_norm_g, v_w_o, v_norm2_g, v_w_mlp1, v_w_mlp2, v_kv_norm_g, v_w_kv_shared, v_b_f, v_final_norm_g):
    big_w = dict(w_in_a=w_in_a, w_in_b=w_in_b, w_mem_kv=w_mem_kv, w_o=w_o, w_mlp1=w_mlp1, w_mlp2=w_mlp2,
                 w_kv_shared=w_kv_shared)
    small_w = dict(norm1_g=norm1_g, mem_norm_g=mem_norm_g, norm2_g=norm2_g, kv_norm_g=kv_norm_g,
                   final_norm_g=final_norm_g, b_f=b_f)
    big_m = dict(w_in_a=m_w_in_a, w_in_b=m_w_in_b, w_mem_kv=m_w_mem_kv, w_o=m_w_o, w_mlp1=m_w_mlp1,
                 w_mlp2=m_w_mlp2, w_kv_shared=m_w_kv_shared)
    small_m = dict(norm1_g=m_norm1_g, mem_norm_g=m_mem_norm_g, norm2_g=m_norm2_g, kv_norm_g=m_kv_norm_g,
                   final_norm_g=m_final_norm_g, b_f=m_b_f)
    big_v = dict(w_in_a=v_w_in_a, w_in_b=v_w_in_b, w_mem_kv=v_w_mem_kv, w_o=v_w_o, w_mlp1=v_w_mlp1,
                 w_mlp2=v_w_mlp2, w_kv_shared=v_w_kv_shared)
    small_v = dict(norm1_g=v_norm1_g, mem_norm_g=v_mem_norm_g, norm2_g=v_norm2_g, kv_norm_g=v_kv_norm_g,
                   final_norm_g=v_final_norm_g, b_f=v_b_f)

    def pack(k, big, small, dtype):
        return _pack_local(PARTS[k], PART_ROWS[k], big, small if k == 0 else None, dtype)

    def pack_all(big, small):
        return jnp.concatenate([pack(k, big, small, F32) for k in range(len(PARTS))], axis=0)

    wb = {n: {} for n in BIG_NAMES}
    _unpack_gathered(PARTS[0], _allgather_chips(pack(0, big_w, small_w, BF16)), wb)
    shards = [pack(1 + l, big_w, None, BF16) for l in range(N_A)]

    loss, dx, gb, gsmall, received = _local_step(x[0], mem[0], wb, shards, small_w, loss_target[0])

    received = [_scatter_chips(_pack_grads(PARTS[0], PART_ROWS[0], gb, gsmall))] + received
    part = jnp.concatenate([_sum4(r) for r in received], axis=0)
    other = _swap_cores(part)
    g, delta, new_m, new_v = _adamw(part, other, pack_all(big_w, small_w), pack_all(big_m, small_m),
                                    pack_all(big_v, small_v))

    outs = [lax.psum(loss[0, 0], ("x", "y", "c")), dx[None]]
    for packed in (g, delta, new_m, new_v):
        pieces, d, off = {n: [] for n in BIG_NAMES}, {}, 0
        for k, part_k in enumerate(PARTS):
            _unpack_local(part_k, packed[off:off + PART_ROWS[k]], k == 0, pieces, d)
            off += PART_ROWS[k]
        d.update(_join_layers(pieces))
        outs.extend(d[n] for n in WEIGHT_ORDER)
    return tuple(outs)
```

```python
import functools
import math

import jax
import jax.numpy as jnp
from jax import lax
from jax.experimental import pallas as pl
from jax.experimental.pallas import tpu as pltpu

F32 = jnp.float32
BF16 = jnp.bfloat16

D_MODEL = 1024
HEAD_DIM = 64
MIX_WIDTH = 512
MEM_WIDTH = 256
MERGED_WIDTH = MIX_WIDTH + MEM_WIDTH
DEPTH = 4
N_A = 2
D_FF = 4096
EPS = 1e-6
NEG_INF = -1e30
SCALE = 1.0 / math.sqrt(HEAD_DIM)

ADAM_LR = 0.001
ADAM_B1 = 0.9
ADAM_B2 = 0.999
ADAM_EPS = 1e-08
ADAM_WD = 0.01
ADAM_STEP = 10

LANES = 128
GROUP_COLS = MIX_WIDTH // LANES
Q_MEM_COL_A = 3 * GROUP_COLS
Q_MEM_COL_B = GROUP_COLS
W_KV_SHARED = 2 * MIX_WIDTH + 8
KVF_WIDTH = 1152
BQ = 256
BK = 128
DIAG_TILES = BQ // BK
CHAINS = 2
UNDERFLOW_BOUND = -110.0
VMEM_LIMIT = 56 * 1024 * 1024

MESH = pl.DeviceIdType.MESH
N_CHIPS = 4

PARTS = (
    (("w_in_a", 0, 1, (1024, 448), 1),
     ("w_mem_kv", 0, 1, (256, 512), 0)),
    (("w_o", 0, 1, (768, 256), 1),
     ("w_mlp1", 0, 1, (1024, 1024), 1),
     ("w_mlp2", 0, 1, (1024, 1024), 0),
     ("w_in_a", 1, 2, (1024, 448), 1),
     ("w_mem_kv", 1, 2, (256, 512), 0)),
    (("w_o", 1, 4, (768, 256), 1),
     ("w_mlp1", 1, 4, (1024, 1024), 1),
     ("w_mlp2", 1, 4, (1024, 1024), 0),
     ("w_in_b", 0, 2, (256, 768), 0),
     ("w_mem_kv", 2, 4, (256, 512), 0),
     ("w_kv_shared", None, None, (1024, 258), 1)),
)
BIG_NAMES = ("w_in_a", "w_in_b", "w_mem_kv", "w_o", "w_mlp1", "w_mlp2", "w_kv_shared")
SMALL = (
    ("norm1_g", (4, 1024)),
    ("mem_norm_g", (4, 1024)),
    ("norm2_g", (4, 1024)),
    ("kv_norm_g", (1, 1024)),
    ("final_norm_g", (1, 1024)),
    ("b_f", (1, 1024)),
)
WEIGHT_ORDER = ("norm1_g", "w_in_a", "w_in_b", "w_mem_kv", "mem_norm_g", "w_o", "norm2_g", "w_mlp1",
                "w_mlp2", "kv_norm_g", "w_kv_shared", "b_f", "final_norm_g")


ROW_ALIGN = 16
PACK_TILE = 256
SMALL_ROWS = ROW_ALIGN
assert sum(s[0] for _, s in SMALL) <= SMALL_ROWS


def _section_rows(entry):
    _, lo, hi, shape, _ = entry
    rows = (1 if lo is None else hi - lo) * math.prod(shape) // D_MODEL
    return rows, -(-rows // ROW_ALIGN) * ROW_ALIGN


def _round_up(n, m):
    return -(-n // m) * m


SUM_TILE = 128
_used = [sum(_section_rows(e)[1] for e in part) for part in PARTS]
PART_ROWS = [_round_up(_used[0] + SMALL_ROWS, SUM_TILE), _round_up(_used[1], SUM_TILE)]
PART_ROWS.append(_round_up(_used[2] + sum(PART_ROWS), PACK_TILE) - sum(PART_ROWS))
assert PART_ROWS[2] % SUM_TILE == 0


def _params(sem=None):
    return pltpu.CompilerParams(dimension_semantics=sem, vmem_limit_bytes=VMEM_LIMIT)


def _pick(n, cands):
    for c in cands:
        if n % c == 0:
            return c
    raise ValueError(f"no tile for {n}")


def _section(a, entry):
    a = a.reshape(-1, D_MODEL)
    return jnp.pad(a, ((0, _section_rows(entry)[1] - a.shape[0]), (0, 0)))


def _small_block(small, dtype):
    blk = jnp.zeros((SMALL_ROWS, D_MODEL), dtype)
    off = 0
    for n, shp in SMALL:
        a = small[n].astype(dtype)
        if n == "b_f":
            blk = blk.at[off, :a.size].set(a.reshape(-1))
        else:
            blk = blk.at[off:off + shp[0]].set(a.reshape(shp))
        off += shp[0]
    return blk


def _fill(parts, rows, dtype):
    used = sum(p.shape[0] for p in parts)
    return jnp.concatenate(parts + [jnp.zeros((rows - used, D_MODEL), dtype)], axis=0)


def _pack_local(part, rows, big, small, dtype):
    parts = [_section((big[e[0]] if e[1] is None else big[e[0]][e[1]:e[2]]).astype(dtype), e) for e in part]
    if small is not None:
        parts.append(_small_block(small, dtype))
    return _fill(parts, rows, dtype)


def _unpack_local(part, p, with_small, pieces, small):
    off = 0
    for e in part:
        n, lo, hi, shp, _ = e
        rows, reserved = _section_rows(e)
        pieces[n].append((lo, p[off:off + rows].reshape(shp if lo is None else (hi - lo,) + shp)))
        off += reserved
    if with_small:
        for n, shp in SMALL:
            a = p[off:off + shp[0]]
            small[n] = a[0, :8] if n == "b_f" else (a.reshape(D_MODEL) if shp[0] == 1 else a)
            off += shp[0]


def _join_layers(pieces):
    out = {}
    for n, ps in pieces.items():
        ps = sorted(ps, key=lambda t: -1 if t[0] is None else t[0])
        out[n] = ps[0][1] if len(ps) == 1 else jnp.concatenate([a for _, a in ps], axis=0)
    return out


def _unpack_gathered(part, g, weights):
    off = 0
    for e in part:
        n, lo, hi, shp, ax = e
        rows, reserved = _section_rows(e)
        if lo is None:
            sec = g[:, off:off + rows].reshape((N_CHIPS,) + shp)
            weights[n] = jnp.concatenate([sec[j] for j in range(N_CHIPS)], axis=ax)
        else:
            sec = g[:, off:off + rows].reshape((N_CHIPS, hi - lo) + shp)
            for l in range(lo, hi):
                weights[n][l] = jnp.concatenate([sec[j, l - lo] for j in range(N_CHIPS)], axis=ax)
        off += reserved


def _pack_grads(part, rows, gbig, gsmall):
    small = None if gsmall is None else _small_block(gsmall, BF16)
    chunks = []
    for j in range(N_CHIPS):
        parts = []
        for e in part:
            n, lo, hi, shp, ax = e
            w = shp[ax]
            layers = [gbig[n]] if lo is None else [gbig[n][l] for l in range(lo, hi)]
            cut = [lax.slice_in_dim(g, j * w, (j + 1) * w, axis=ax).astype(BF16).reshape(-1, D_MODEL) for g in layers]
            parts.append(_section(cut[0] if len(cut) == 1 else jnp.concatenate(cut, axis=0), e))
        if small is not None:
            parts.append(small)
        chunks.append(_fill(parts, rows, BF16))
    return jnp.stack(chunks, axis=0)


ANY = pl.BlockSpec(memory_space=pl.ANY)


def _other_chips(x, y):
    return [(1 - x, y), (x, 1 - y), (1 - x, 1 - y)]


class _AllGather:
    SCRATCH = [pltpu.SemaphoreType.DMA((3,)), pltpu.SemaphoreType.DMA((3,)), pltpu.SemaphoreType.DMA((3,)),
               pltpu.SemaphoreType.DMA((3,)), pltpu.SemaphoreType.DMA]

    def __init__(self, w_ref, o_ref, send_sems, recv_sems, pass_send, pass_recv, local_sem):
        self.w_ref, self.o_ref = w_ref, o_ref
        self.sems = (send_sems, recv_sems, pass_send, pass_recv, local_sem)
        x, y, c = lax.axis_index("x"), lax.axis_index("y"), lax.axis_index("c")
        half = w_ref.shape[0] // 2
        self.c, self.me, self.sibling = c, 2 * x + y, (x, y, 1 - c)
        self.mine = pl.ds(pl.multiple_of(c * half, ROW_ALIGN), half)
        self.other = pl.ds(pl.multiple_of((1 - c) * half, ROW_ALIGN), half)
        self.chips = _other_chips(x, y)

    def _over_ici(self, j, rows_of):
        chip = self.chips[j]
        return pltpu.make_async_remote_copy(
            src_ref=self.w_ref.at[self.mine], dst_ref=self.o_ref.at[rows_of, self.mine],
            send_sem=self.sems[0].at[j], recv_sem=self.sems[1].at[j],
            device_id=(chip[0], chip[1], self.c), device_id_type=MESH)

    def _over_d2d(self, j, rows):
        where = self.o_ref.at[2 * self.chips[j][0] + self.chips[j][1], rows]
        return pltpu.make_async_remote_copy(src_ref=where, dst_ref=where, send_sem=self.sems[2].at[j],
                                            recv_sem=self.sems[3].at[j], device_id=self.sibling,
                                            device_id_type=MESH)

    def _local(self):
        return pltpu.make_async_copy(self.w_ref, self.o_ref.at[self.me], self.sems[4])

    def start(self):
        self._local().start()
        for j in range(3):
            self._over_ici(j, self.me).start()

    def finish(self):
        for j in range(3):
            self._over_ici(j, 2 * self.chips[j][0] + self.chips[j][1]).wait_recv()
            self._over_d2d(j, self.mine).start()
        for j in range(3):
            self._over_d2d(j, self.other).wait_recv()
        for j in range(3):
            self._over_ici(j, self.me).wait_send()
            self._over_d2d(j, self.mine).wait_send()
        self._local().wait()


class _Scatter:
    SCRATCH = [pltpu.SemaphoreType.DMA((3,)), pltpu.SemaphoreType.DMA((3,)), pltpu.SemaphoreType.DMA]

    def __init__(self, g_ref, o_ref, send_sems, recv_sems, local_sem):
        self.g_ref, self.o_ref, self.sems = g_ref, o_ref, (send_sems, recv_sems, local_sem)
        x, y, c = lax.axis_index("x"), lax.axis_index("y"), lax.axis_index("c")
        self.c, self.me, self.chips = c, 2 * x + y, _other_chips(x, y)

    def _copy(self, j):
        chip = self.chips[j]
        return pltpu.make_async_remote_copy(
            src_ref=self.g_ref.at[2 * chip[0] + chip[1]], dst_ref=self.o_ref.at[self.me],
            send_sem=self.sems[0].at[j], recv_sem=self.sems[1].at[j],
            device_id=(chip[0], chip[1], self.c), device_id_type=MESH)

    def _local(self):
        return pltpu.make_async_copy(self.g_ref.at[self.me], self.o_ref.at[self.me], self.sems[2])

    def start(self):
        self._local().start()
        for j in range(3):
            self._copy(j).start()

    def finish(self):
        for j in range(3):
            self._copy(j).wait()
        self._local().wait()


def _allgather_chips(w):
    def body(w_ref, o_ref, *sems):
        ag = _AllGather(w_ref, o_ref, *sems)
        ag.start()
        ag.finish()

    return pl.pallas_call(
        body, name="allgather_weights",
        out_shape=jax.ShapeDtypeStruct((N_CHIPS,) + w.shape, w.dtype),
        in_specs=[ANY], out_specs=ANY, scratch_shapes=_AllGather.SCRATCH,
    )(w)


def _scatter_chips(g4):
    def body(g_ref, o_ref, *sems):
        sc = _Scatter(g_ref, o_ref, *sems)
        sc.start()
        sc.finish()

    return pl.pallas_call(
        body, name="scatter_grads",
        out_shape=jax.ShapeDtypeStruct(g4.shape, g4.dtype),
        in_specs=[ANY], out_specs=ANY, scratch_shapes=_Scatter.SCRATCH,
    )(g4)


def _swap_cores(p):
    def body(p_ref, o_ref, send_sem, recv_sem):
        x, y, c = lax.axis_index("x"), lax.axis_index("y"), lax.axis_index("c")
        cp = pltpu.make_async_remote_copy(src_ref=p_ref, dst_ref=o_ref, send_sem=send_sem, recv_sem=recv_sem,
                                          device_id=(x, y, 1 - c), device_id_type=MESH)
        cp.start()
        cp.wait()

    return pl.pallas_call(
        body, name="swap_cores",
        out_shape=jax.ShapeDtypeStruct(p.shape, p.dtype),
        in_specs=[ANY], out_specs=ANY,
        scratch_shapes=[pltpu.SemaphoreType.DMA, pltpu.SemaphoreType.DMA],
    )(p)


def _sum4(r4):
    _, R, C = r4.shape

    def body(r_ref, o_ref):
        o_ref[...] = ((r_ref[0].astype(F32) + r_ref[1].astype(F32)) + r_ref[2].astype(F32)) + r_ref[3].astype(F32)

    return pl.pallas_call(
        body, name="sum_chips", grid=(R // SUM_TILE,),
        in_specs=[pl.BlockSpec((N_CHIPS, SUM_TILE, C), lambda i: (0, i, 0))],
        out_specs=pl.BlockSpec((SUM_TILE, C), lambda i: (i, 0)),
        out_shape=jax.ShapeDtypeStruct((R, C), F32),
        compiler_params=_params(("parallel",)),
    )(r4)


def _adamw(pa, pb, w, m, v):
    R, C = w.shape
    c1 = 1.0 - ADAM_B1
    c2 = 1.0 - ADAM_B2
    bc1 = 1.0 - ADAM_B1 ** ADAM_STEP
    bc2 = 1.0 - ADAM_B2 ** ADAM_STEP

    def body(pa_ref, pb_ref, w_ref, m_ref, v_ref, g_ref, d_ref, mo_ref, vo_ref):
        g = pa_ref[...] + pb_ref[...]
        mn = ADAM_B1 * m_ref[...] + c1 * g
        vn = ADAM_B2 * v_ref[...] + c2 * (g * g)
        m_hat = mn / bc1
        v_hat = vn / bc2
        g_ref[...] = g
        d_ref[...] = -ADAM_LR * (m_hat / (jnp.sqrt(v_hat) + ADAM_EPS) + ADAM_WD * w_ref[...])
        mo_ref[...] = mn
        vo_ref[...] = vn

    spec = pl.BlockSpec((PACK_TILE, C), lambda i: (i, 0))
    shp = jax.ShapeDtypeStruct((R, C), F32)
    return pl.pallas_call(
        body, name="adamw", grid=(R // PACK_TILE,),
        in_specs=[spec] * 5, out_specs=[spec] * 4, out_shape=[shp] * 4,
        compiler_params=_params(("parallel",)),
    )(pa, pb, w, m, v)


def _rms_fwd(x, g, name):
    R, Dm = x.shape
    tr = _pick(R, (512, 256, 128))

    def body(x_ref, g_ref, o_ref):
        xf = x_ref[...]
        r = lax.rsqrt(jnp.mean(xf * xf, axis=-1, keepdims=True) + EPS)
        o_ref[...] = (xf * r * g_ref[...]).astype(o_ref.dtype)

    return pl.pallas_call(
        body, name=name, grid=(R // tr,),
        in_specs=[pl.BlockSpec((tr, Dm), lambda i: (i, 0)), pl.BlockSpec((1, Dm), lambda i: (0, 0))],
        out_specs=pl.BlockSpec((tr, Dm), lambda i: (i, 0)),
        out_shape=jax.ShapeDtypeStruct((R, Dm), BF16),
        compiler_params=_params(("parallel",)),
    )(x, g)


def _rms_gain_grad(x, dy, name):
    R, Dm = x.shape
    tr = _pick(R, (256, 128))

    def body(x_ref, dy_ref, dg_ref):
        xf = x_ref[...]
        r = lax.rsqrt(jnp.mean(xf * xf, axis=-1, keepdims=True) + EPS)

        @pl.when(pl.program_id(0) == 0)
        def _():
            dg_ref[...] = jnp.zeros_like(dg_ref)

        dg_ref[...] += jnp.sum(dy_ref[...] * (xf * r), axis=0, keepdims=True)

    row = pl.BlockSpec((tr, Dm), lambda i: (i, 0))
    return pl.pallas_call(
        body, name=name, grid=(R // tr,),
        in_specs=[row, row], out_specs=pl.BlockSpec((1, Dm), lambda i: (0, 0)),
        out_shape=jax.ShapeDtypeStruct((1, Dm), F32),
        compiler_params=_params(("arbitrary",)),
    )(x, dy)


def _final_loss(x, g, tgt):
    R, Dm = x.shape
    tr = _pick(R, (256, 128))

    def body(x_ref, g_ref, t_ref, l_ref, dx_ref, dxb_ref, dg_ref):
        xf = x_ref[...]
        gv = g_ref[...]
        r = lax.rsqrt(jnp.mean(xf * xf, axis=-1, keepdims=True) + EPS)
        xr = xf * r
        err = xr * gv - t_ref[...]
        dy_ = err * (1.0 / Dm)
        gdy = dy_ * gv
        mdot = jnp.mean(xf * gdy, axis=-1, keepdims=True)
        dx = r * gdy - xf * ((r * r * r) * mdot)
        dx_ref[...] = dx
        dxb_ref[...] = dx.astype(BF16)

        @pl.when(pl.program_id(0) == 0)
        def _():
            dg_ref[...] = jnp.zeros_like(dg_ref)
            l_ref[...] = jnp.zeros_like(l_ref)

        dg_ref[...] += jnp.sum(dy_ * xr, axis=0, keepdims=True)
        sq = jnp.sum(err * err, axis=1, keepdims=True)
        l_ref[...] += jnp.sum(sq, axis=0, keepdims=True) * (0.5 / Dm)

    row = pl.BlockSpec((tr, Dm), lambda i: (i, 0))
    vec = pl.BlockSpec((1, Dm), lambda i: (0, 0))
    return pl.pallas_call(
        body, name="final_norm_loss", grid=(R // tr,),
        in_specs=[row, vec, row],
        out_specs=[pl.BlockSpec((1, 1), lambda i: (0, 0)), row, row, vec],
        out_shape=[jax.ShapeDtypeStruct((1, 1), F32), jax.ShapeDtypeStruct((R, Dm), F32),
                   jax.ShapeDtypeStruct((R, Dm), BF16), jax.ShapeDtypeStruct((1, Dm), F32)],
        compiler_params=_params(("arbitrary",)),
    )(x, g, tgt)


MAX_TK = 2048

_DIMS = {"nn": (((1,), (0,)), ((), ())), "nt": (((1,), (1,)), ((), ())), "tn": (((0,), (0,)), ((), ()))}


def _mm(a, b, *, mode="nn", out_dtype=BF16, epi=None, extra=None, name):
    if mode == "nn":
        (M, K), N = a.shape, b.shape[1]
    elif mode == "nt":
        (M, K), N = a.shape, b.shape[0]
    else:
        (K, M), N = a.shape, b.shape[1]
    tm = _pick(M, (512, 256, 128) if epi == "rms_bwd" else (1024, 768, 512, 256, 128))
    tn = _pick(N, (1024, 896, 768, 640, 512, 384, 256, 128))
    tk = K if K <= MAX_TK else _pick(K, (MAX_TK, 1024, 512, 256, 128))
    nk = K // tk
    extras = () if extra is None else (extra if isinstance(extra, tuple) else (extra,))
    n_out = {"relu2": 2, "add_rms": 2, "rms_bwd": 3}.get(epi, 1)
    assert epi not in ("rms_bwd", "add_rms") or tn == N

    def body(*refs):
        a_ref, b_ref = refs[:2]
        e_refs = refs[2:2 + len(extras)]
        e_ref = e_refs[0] if e_refs else None
        outs = refs[2 + len(extras):2 + len(extras) + n_out]
        k = pl.program_id(2)
        part = lax.dot_general(a_ref[...].astype(BF16), b_ref[...].astype(BF16), _DIMS[mode],
                               preferred_element_type=F32)

        def finish(acc):
            if epi is None:
                outs[0][...] = acc.astype(outs[0].dtype)
            elif epi == "add":
                outs[0][...] = (e_ref[...] + acc).astype(outs[0].dtype)
            elif epi == "add_rms":
                y = e_refs[0][...] + acc
                outs[0][...] = y
                r = lax.rsqrt(jnp.mean(y * y, axis=-1, keepdims=True) + EPS)
                outs[1][...] = (y * r * e_refs[1][...]).astype(BF16)
            elif epi == "relu2":
                outs[0][...] = acc.astype(BF16)
                rl = jnp.maximum(acc, 0.0)
                outs[1][...] = (rl * rl).astype(BF16)
            elif epi == "drelu2":
                u = e_ref[...].astype(F32)
                outs[0][...] = (acc * (2.0 * jnp.maximum(u, 0.0))).astype(outs[0].dtype)
            elif epi == "rms_bwd":
                x_ref, g_ref, dres_ref = e_refs
                xf = x_ref[...]
                r = lax.rsqrt(jnp.mean(xf * xf, axis=-1, keepdims=True) + EPS)
                gdy = acc * g_ref[...]
                mdot = jnp.mean(xf * gdy, axis=-1, keepdims=True)
                dx = dres_ref[...] + (r * gdy - xf * ((r * r * r) * mdot))
                outs[0][...] = dx
                outs[1][...] = dx.astype(BF16)

                @pl.when(pl.program_id(0) == 0)
                def _():
                    outs[2][...] = jnp.zeros_like(outs[2])

                outs[2][...] += jnp.sum(acc * (xf * r), axis=0, keepdims=True)

        if nk == 1:
            finish(part)
        else:
            acc_ref = refs[-1]

            @pl.when(k == 0)
            def _():
                acc_ref[...] = part

            @pl.when(jnp.logical_and(k > 0, k < nk - 1))
            def _():
                acc_ref[...] += part

            @pl.when(k == nk - 1)
            def _():
                finish(acc_ref[...] + part)

    if mode == "tn":
        a_spec = pl.BlockSpec((tk, tm), lambda i, j, k: (k, i))
    else:
        a_spec = pl.BlockSpec((tm, tk), lambda i, j, k: (i, k))
    if mode == "nt":
        b_spec = pl.BlockSpec((tn, tk), lambda i, j, k: (j, k))
    else:
        b_spec = pl.BlockSpec((tk, tn), lambda i, j, k: (k, j))
    o_spec = pl.BlockSpec((tm, tn), lambda i, j, k: (i, j))
    vec_spec = pl.BlockSpec((1, tn), lambda i, j, k: (0, j))
    ins, in_specs = [a, b] + list(extras), [a_spec, b_spec]
    if epi == "rms_bwd":
        in_specs += [o_spec, vec_spec, o_spec]
        out_shape = [jax.ShapeDtypeStruct((M, N), F32), jax.ShapeDtypeStruct((M, N), BF16),
                     jax.ShapeDtypeStruct((1, N), F32)]
        out_specs = [o_spec, o_spec, vec_spec]
    elif epi == "add_rms":
        in_specs += [o_spec, vec_spec]
        out_shape = [jax.ShapeDtypeStruct((M, N), F32), jax.ShapeDtypeStruct((M, N), BF16)]
        out_specs = [o_spec, o_spec]
    else:
        in_specs += [o_spec] * len(extras)
        out_shape = [jax.ShapeDtypeStruct((M, N), BF16 if epi == "relu2" else out_dtype)] * n_out
        out_specs = [o_spec] * n_out
    res = pl.pallas_call(
        body, name=name, grid=(M // tm, N // tn, nk),
        in_specs=in_specs, out_specs=out_specs, out_shape=out_shape,
        scratch_shapes=[pltpu.VMEM((tm, tn), F32)] if nk > 1 else [],
        compiler_params=_params(("arbitrary",) * 3 if epi == "rms_bwd" else ("parallel", "parallel", "arbitrary")),
    )(*ins)
    return res if n_out > 1 else res[0]


def _dot(a, b):
    return lax.dot_general(a, b, _DIMS["nn"], preferred_element_type=F32)


def _dot_nt(a, b):
    return lax.dot_general(a, b, _DIMS["nt"], preferred_element_type=F32)


def _dot_tn(a, b):
    return lax.dot_general(a, b, _DIMS["tn"], preferred_element_type=F32)


def _split_dot(x, t):
    hi = x.astype(BF16)
    lo = (x - hi.astype(F32)).astype(BF16)
    return _dot(jnp.concatenate([hi, lo], axis=1), jnp.concatenate([t, t], axis=0))


def _head_pair(ref, scale=None):
    xf = ref[...].astype(F32)
    if scale is not None:
        xf = xf * scale
    is_a = lax.broadcasted_iota(jnp.int32, xf.shape, 1) < HEAD_DIM
    return jnp.where(is_a, xf, 0.0).astype(BF16), jnp.where(is_a, 0.0, xf).astype(BF16)


def _stack(a, b):
    return jnp.concatenate([a, b], axis=0)


def _head_rows(ref, scale=None):
    return _stack(*_head_pair(ref, scale))


def _unstack_heads(x):
    rows = x.shape[0] // 2
    return _select_pair(x[:rows], x[rows:])


def _pair_rowsum(x):
    is_a = lax.broadcasted_iota(jnp.int32, x.shape, 1) < HEAD_DIM
    return (jnp.sum(jnp.where(is_a, x, 0.0), axis=1, keepdims=True),
            jnp.sum(jnp.where(is_a, 0.0, x), axis=1, keepdims=True))


def _select_pair(xa, xb):
    is_a = lax.broadcasted_iota(jnp.int32, xa.shape, 1) < HEAD_DIM
    return jnp.where(is_a, xa, xb)


def _two_cols(xa, xb):
    rows = xa.shape[0]
    first = lax.broadcasted_iota(jnp.int32, (rows, 2), 1) == 0
    return jnp.where(first, xa, xb)


def _softplus_parts(z):
    e = jnp.exp(-jnp.abs(z))
    return jnp.maximum(z, 0.0) + jnp.log(1.0 + e), e


def _tile_iotas():
    row = lax.broadcasted_iota(jnp.int32, (BK, BK), 0)
    col = lax.broadcasted_iota(jnp.int32, (BK, BK), 1)
    return row, col


def _stacked_iotas(bq, nk):
    row = lax.broadcasted_iota(jnp.int32, (2 * bq, nk), 0) & (bq - 1)
    col = lax.broadcasted_iota(jnp.int32, (2 * bq, nk), 1)
    return row, col


def _side_exchange(exchange, operand, n_in, n_out):
    out_shape = jax.ShapeDtypeStruct(((N_CHIPS,) + operand.shape) if exchange is _AllGather else operand.shape,
                                     operand.dtype)
    n_sem = len(exchange.SCRATCH)

    def pick(refs):
        def make():
            return exchange(refs[n_in], refs[n_in + 1 + n_out], *refs[len(refs) - n_sem:])

        return (lambda: make().start()), (lambda: make().finish())

    return [operand], [ANY], [out_shape], [ANY], pick


def _sb_fwd(proj, name, exchange, operand):
    S = proj.shape[0]
    nqb = S // (CHAINS * BQ)
    x_in, x_in_specs, x_out, x_out_specs, pick = _side_exchange(exchange, operand, 3, 1)

    def body(*refs):
        q_ref, k_ref, v_ref = refs[:3]
        o_ref = refs[3 + len(x_in)]
        acc_ref = refs[3 + len(x_in) + 1 + len(x_out)]
        start, finish = pick(refs)
        p = pl.program_id(0)
        i = pl.program_id(1)

        @pl.when(jnp.logical_and(p == 0, i == 0))
        def _():
            start()

        q2 = [_head_rows(q_ref.at[pl.ds(ch * BQ, BQ)], SCALE) for ch in range(CHAINS)]
        row, col = _tile_iotas()
        tri = (row > col).astype(BF16)
        srow, scol = _stacked_iotas(BQ, BK)
        acc_ref[...] = jnp.zeros_like(acc_ref)

        def tile(ch, kb, c, dmask=None, valid=None):
            r0 = pl.multiple_of(kb * BK, BK)
            kblk = k_ref[pl.ds(r0, BK), :]
            vblk = v_ref[pl.ds(r0, BK), :]
            z = _dot_nt(q2[ch], kblk)
            sp, _ = _softplus_parts(z)
            lm = -sp
            if dmask is not None:
                lm = jnp.where(dmask, lm, 0.0)
            btw = _split_dot(lm, tri)
            w = jnp.exp((z - sp) + btw + c)
            if dmask is not None:
                w = jnp.where(dmask, w, 0.0)
            if valid is not None:
                w = w * valid
            acc_ref[ch] += _dot(w.astype(BF16), vblk)
            return c + btw[:, 0:1] + lm[:, 0:1]

        def alive(c):
            return jnp.max(c) > UNDERFLOW_BOUND

        cs = [jnp.zeros((2 * BQ, 1), F32)] * CHAINS
        for d in reversed(range(DIAG_TILES)):
            cs = [tile(ch, (CHAINS * i + ch) * DIAG_TILES + d, cs[ch], dmask=scol < srow - d * BK)
                  for ch in range(CHAINS)]

        def tile_of(ch, t):
            return (CHAINS * i + ch) * DIAG_TILES - 1 - t

        def more(cs, t):
            go = [jnp.logical_and(alive(cs[ch]), tile_of(ch, t) >= 0) for ch in range(CHAINS)]
            return functools.reduce(jnp.logical_or, go).astype(jnp.int32)

        def step(st):
            t, _, cs = st
            new = []
            for ch in range(CHAINS):
                kb = tile_of(ch, t)
                if ch == CHAINS - 1:
                    new.append(tile(ch, kb, cs[ch]))
                else:
                    new.append(tile(ch, jnp.maximum(kb, 0), cs[ch], valid=(kb >= 0).astype(F32)))
            return t + 1, more(new, t + 1), new

        lax.while_loop(lambda st: st[1] > 0, step, (0, more(cs, 0), cs))
        for ch in range(CHAINS):
            o_ref[pl.ds(ch * BQ, BQ), :] = _unstack_heads(acc_ref[ch])

        @pl.when(jnp.logical_and(p == 3, i == nqb - 1))
        def _():
            finish()

    blk = pl.BlockSpec((CHAINS * BQ, LANES), lambda p, i: (i, p))
    res = pl.pallas_call(
        body, name=name, grid=(4, nqb),
        in_specs=[blk, pl.BlockSpec((S, LANES), lambda p, i: (0, GROUP_COLS + p)),
                  pl.BlockSpec((S, LANES), lambda p, i: (0, 2 * GROUP_COLS + p))] + x_in_specs,
        out_specs=[blk] + x_out_specs,
        out_shape=[jax.ShapeDtypeStruct((S, MERGED_WIDTH), F32)] + x_out,
        scratch_shapes=[pltpu.VMEM((CHAINS, 2 * BQ, LANES), F32)] + exchange.SCRATCH,
        compiler_params=_params(("arbitrary", "arbitrary")),
    )(proj, proj, proj, *x_in)
    return res


def _sb_bwd(proj, merged, dmerged, name, exchange, operand):
    S = proj.shape[0]
    nqb = S // (CHAINS * BQ)
    x_in, x_in_specs, x_out, x_out_specs, pick = _side_exchange(exchange, operand, 5, 3)

    def body(*refs):
        q_ref, k_ref, v_ref, o_ref, do_ref = refs[:5]
        dq_ref, dk_hbm, dv_hbm = refs[5 + len(x_in):8 + len(x_in)]
        dq_acc, dk_acc, dv_acc, sem = refs[8 + len(x_in) + len(x_out):12 + len(x_in) + len(x_out)]
        start, finish = pick(refs)
        p = pl.program_id(0)
        i = pl.program_id(1)

        @pl.when(jnp.logical_and(p == 0, i == 0))
        def _():
            start()

        @pl.when(i == 0)
        def _():
            dk_acc[...] = jnp.zeros_like(dk_acc)
            dv_acc[...] = jnp.zeros_like(dv_acc)

        rows = [pl.ds(ch * BQ, BQ) for ch in range(CHAINS)]
        q2 = [_head_rows(q_ref.at[rw], SCALE) for rw in rows]
        do2 = [_head_rows(do_ref.at[rw]) for rw in rows]
        tot = [_stack(*_pair_rowsum(do_ref[rw, :].astype(F32) * o_ref[rw, :])) for rw in rows]
        row, col = _tile_iotas()
        tri_gt = (row > col).astype(BF16)
        tri_ge = (row >= col).astype(BF16)
        srow, scol = _stacked_iotas(BQ, BK)
        dq_acc[...] = jnp.zeros_like(dq_acc)

        def tile(ch, kb, st, dmask=None, valid=None):
            masked = dmask is not None
            c, r = st
            r0 = pl.multiple_of(kb * BK, BK)
            kblk = k_ref[pl.ds(r0, BK), :]
            vblk = v_ref[pl.ds(r0, BK), :]
            z = _dot_nt(q2[ch], kblk)
            sp, e = _softplus_parts(z)
            lm = -sp
            if masked:
                lm = jnp.where(dmask, lm, 0.0)
            btw = _split_dot(lm, tri_gt)
            w = jnp.exp((z - sp) + btw + c)
            if masked:
                w = jnp.where(dmask, w, 0.0)
            if valid is not None:
                w = w * valid
            wb = w.astype(BF16)
            a = wb.astype(F32) * _dot_nt(do2[ch], vblk)
            suffix = _split_dot(a, tri_ge) + r
            rcp = 1.0 / (1.0 + e)
            pos = z >= 0.0
            sig = jnp.where(pos, rcp, e * rcp)
            sig_neg = jnp.where(pos, e * rcp, rcp)
            dz = a * sig_neg - (tot[ch] - suffix) * sig
            if masked:
                dz = jnp.where(dmask, dz, 0.0)
            if valid is not None:
                dz = dz * valid
            dzb = dz.astype(BF16)
            dq_acc[ch] += _dot(dzb, kblk)
            dk_acc[pl.ds(r0, BK), :] += _dot_tn(dzb, q2[ch])
            dv_acc[pl.ds(r0, BK), :] += _dot_tn(wb, do2[ch])
            return c + btw[:, 0:1] + lm[:, 0:1], suffix[:, 0:1]

        def alive(st):
            return jnp.max(st[0]) > UNDERFLOW_BOUND

        zero = jnp.zeros((2 * BQ, 1), F32)
        sts = [(zero, zero)] * CHAINS
        for d in reversed(range(DIAG_TILES)):
            sts = [tile(ch, (CHAINS * i + ch) * DIAG_TILES + d, sts[ch], dmask=scol < srow - d * BK)
                   for ch in range(CHAINS)]

        def tile_of(ch, t):
            return (CHAINS * i + ch) * DIAG_TILES - 1 - t

        def more(sts, t):
            go = [jnp.logical_and(alive(sts[ch]), tile_of(ch, t) >= 0) for ch in range(CHAINS)]
            return functools.reduce(jnp.logical_or, go).astype(jnp.int32)

        def step(s):
            t, _, sts = s
            new = []
            for ch in range(CHAINS):
                kb = tile_of(ch, t)
                if ch == CHAINS - 1:
                    new.append(tile(ch, kb, sts[ch]))
                else:
                    new.append(tile(ch, jnp.maximum(kb, 0), sts[ch], valid=(kb >= 0).astype(F32)))
            return t + 1, more(new, t + 1), new

        lax.while_loop(lambda s: s[1] > 0, step, (0, more(sts, 0), sts))
        for ch in range(CHAINS):
            dq_ref[rows[ch], :] = (_unstack_heads(dq_acc[ch]) * SCALE).astype(dq_ref.dtype)

        @pl.when(i == nqb - 1)
        def _():
            ck = pltpu.make_async_copy(dk_acc, dk_hbm.at[p], sem.at[0])
            cv = pltpu.make_async_copy(dv_acc, dv_hbm.at[p], sem.at[1])
            ck.start()
            cv.start()
            ck.wait()
            cv.wait()

        @pl.when(jnp.logical_and(p == 3, i == nqb - 1))
        def _():
            finish()

    blk = lambda off: pl.BlockSpec((CHAINS * BQ, LANES), lambda p, i: (i, off + p))
    slab = lambda off: pl.BlockSpec((S, LANES), lambda p, i: (0, off + p))
    return pl.pallas_call(
        body, name=name, grid=(4, nqb),
        in_specs=[blk(0), slab(GROUP_COLS), slab(2 * GROUP_COLS), blk(0), blk(0)] + x_in_specs,
        out_specs=[blk(0), ANY, ANY] + x_out_specs,
        out_shape=[jax.ShapeDtypeStruct((S, MIX_WIDTH), BF16),
                   jax.ShapeDtypeStruct((4, S, LANES), F32), jax.ShapeDtypeStruct((4, S, LANES), F32)] + x_out,
        scratch_shapes=[pltpu.VMEM((CHAINS, 2 * BQ, LANES), F32), pltpu.VMEM((S, LANES), F32),
                        pltpu.VMEM((S, LANES), F32), pltpu.SemaphoreType.DMA((2,))]
        + exchange.SCRATCH,
        compiler_params=_params(("arbitrary", "arbitrary")),
    )(proj, proj, proj, merged, dmerged, *x_in)


def _key_norm_max(k_ref, knorm_ref, nkb):
    def step(kb, m):
        r0 = pl.multiple_of(kb * BK, BK)
        blk = k_ref[pl.ds(r0, BK), :].astype(F32)
        sa, sb = _pair_rowsum(blk * blk)
        return (jnp.maximum(m[0], jnp.max(sa, axis=0, keepdims=True)),
                jnp.maximum(m[1], jnp.max(sb, axis=0, keepdims=True)))

    zero = jnp.zeros((1, 1), F32)
    ma, mb = lax.fori_loop(0, nkb, step, (zero, zero))
    knorm_ref[...] = _select_pair(jnp.broadcast_to(ma, (1, LANES)), jnp.broadcast_to(mb, (1, LANES)))


FQ = 512
FK = FQ
GATE_BLOCKS = FK // BK


def _key_gates(cr_ref, kb):
    blocks = [cr_ref[0, GATE_BLOCKS * kb + j] for j in range(GATE_BLOCKS)]
    per_head = [jnp.broadcast_to(jnp.concatenate([b[h:h + 1] for b in blocks], axis=1), (FQ, FK)) for h in range(2)]
    return _stack(*per_head)


def _last_gate(cr_ref, kb):
    last = cr_ref[0, GATE_BLOCKS * jnp.maximum(kb, 0) + GATE_BLOCKS - 1]
    return _stack(*[jnp.broadcast_to(last[h:h + 1, BK - 1:BK], (FQ, 1)) for h in range(2)])


def _logit_bound(q_ref, knorm_ref):
    qf = q_ref[...].astype(F32) * SCALE
    qa, qb = _pair_rowsum(qf * qf)
    kn = knorm_ref[...]
    return _stack(jnp.sqrt(qa * kn[:, 0:1]), jnp.sqrt(qb * kn[:, HEAD_DIM:HEAD_DIM + 1]))


def _causal_bias(bias_ref):
    srow, scol = _stacked_iotas(FQ, FK)
    bias_ref[...] = jnp.where(scol <= srow, 0.0, NEG_INF)


def _fox_fwd(proj, kv, c_col, c_row, name):
    S = proj.shape[0]
    nqb = S // FQ

    def body(q_ref, k_ref, v_ref, cc_ref, cr_ref, o_ref, lse_ref, acc_ref, knorm_ref, bias_ref):
        i = pl.program_id(1)

        @pl.when(i == 0)
        def _():
            _key_norm_max(k_ref, knorm_ref, S // BK)
            _causal_bias(bias_ref)

        q2 = _head_rows(q_ref, SCALE)
        bound = _logit_bound(q_ref, knorm_ref)
        cc = cc_ref[0]
        ct = _stack(cc[:, 0:1], cc[:, 1:2])
        acc_ref[...] = jnp.zeros_like(acc_ref)

        def tile(kb, st, masked):
            m, l = st
            r0 = pl.multiple_of(kb * FK, FK)
            kblk = k_ref[pl.ds(r0, FK), :]
            vblk = v_ref[pl.ds(r0, FK), :]
            z = _dot_nt(q2, kblk) + ct - _key_gates(cr_ref, kb)
            if masked:
                z = z + bias_ref[...]
            m_new = jnp.maximum(m, jnp.max(z, axis=1, keepdims=True))
            alpha = jnp.exp(m - m_new)
            pr = jnp.exp(z - m_new)
            acc_ref[...] = alpha * acc_ref[...] + _split_dot(pr, vblk)
            return m_new, alpha * l + jnp.sum(pr, axis=1, keepdims=True)

        def alive(kb, st):
            reach = bound + ct - _last_gate(cr_ref, kb) - st[0]
            return (jnp.max(reach) > UNDERFLOW_BOUND).astype(jnp.int32)

        neg = jnp.full((2 * FQ, 1), NEG_INF, F32)
        zero = jnp.zeros((2 * FQ, 1), F32)
        st0 = tile(i, (neg, zero), True)

        def cond(s):
            return jnp.logical_and(s[0] >= 0, s[1] > 0)

        def step(s):
            kb, _, st = s
            st = tile(kb, st, False)
            return kb - 1, alive(kb - 1, st), st

        _, _, (m, l) = lax.while_loop(cond, step, (i - 1, alive(i - 1, st0), st0))
        o_ref[...] = _unstack_heads(acc_ref[...] / l)
        lse = m + jnp.log(l)
        lse_ref[0] = _two_cols(lse[:FQ], lse[FQ:])

    return pl.pallas_call(
        body, name=name, grid=(4, nqb),
        in_specs=[pl.BlockSpec((FQ, LANES), lambda p, i: (i, p)),
                  pl.BlockSpec((S, LANES), lambda p, i: (0, p)),
                  pl.BlockSpec((S, LANES), lambda p, i: (0, GROUP_COLS + p)),
                  pl.BlockSpec((1, FQ, 2), lambda p, i: (p, i, 0)),
                  pl.BlockSpec((1, S // BK, 8, LANES), lambda p, i: (p, 0, 0, 0))],
        out_specs=[pl.BlockSpec((FQ, LANES), lambda p, i: (i, p)),
                   pl.BlockSpec((1, FQ, 2), lambda p, i: (p, i, 0))],
        out_shape=[jax.ShapeDtypeStruct((S, MERGED_WIDTH), F32), jax.ShapeDtypeStruct((4, S, 2), F32)],
        scratch_shapes=[pltpu.VMEM((2 * FQ, LANES), F32), pltpu.VMEM((1, LANES), F32),
                        pltpu.VMEM((2 * FQ, FK), F32)],
        compiler_params=_params(("arbitrary", "arbitrary")),
    )(proj, kv, kv, c_col, c_row)


def _fox_bwd(proj, kv, c_col, c_row, lse, merged, dmerged, dk_prev, dv_prev, dc_prev, name):
    S = proj.shape[0]
    nqb = S // FQ

    def body(q_ref, k_ref, v_ref, cc_ref, cr_ref, lse_ref, o_ref, do_ref, dkp_hbm, dvp_hbm, dcp_ref,
             dq_ref, dk_hbm, dv_hbm, dc_ref, dq_acc, dk_acc, dv_acc, knorm_ref, bias_ref, sem):
        p = pl.program_id(0)
        i = pl.program_id(1)

        @pl.when(i == 0)
        def _():
            ck = pltpu.make_async_copy(dkp_hbm.at[p], dk_acc, sem.at[0])
            cv = pltpu.make_async_copy(dvp_hbm.at[p], dv_acc, sem.at[1])
            ck.start()
            cv.start()
            dc_ref[...] = dcp_ref[...]
            _key_norm_max(k_ref, knorm_ref, S // BK)
            _causal_bias(bias_ref)
            ck.wait()
            cv.wait()

        q2 = _head_rows(q_ref, SCALE)
        do2 = _head_rows(do_ref)
        tot = _stack(*_pair_rowsum(do_ref[...].astype(F32) * o_ref[...]))
        bound = _logit_bound(q_ref, knorm_ref)
        cc = cc_ref[0]
        ct = _stack(cc[:, 0:1], cc[:, 1:2])
        ls = lse_ref[0]
        lse = _stack(ls[:, 0:1], ls[:, 1:2])
        sub = lax.broadcasted_iota(jnp.int32, (8, LANES), 0)
        dq_acc[...] = jnp.zeros_like(dq_acc)

        def tile(kb, masked):
            r0 = pl.multiple_of(kb * FK, FK)
            kblk = k_ref[pl.ds(r0, FK), :]
            vblk = v_ref[pl.ds(r0, FK), :]
            z = _dot_nt(q2, kblk) + ct - _key_gates(cr_ref, kb)
            if masked:
                z = z + bias_ref[...]
            pr = jnp.exp(z - lse)
            ds = pr * (_dot_nt(do2, vblk) - tot)
            dsb = ds.astype(BF16)
            dq_acc[...] += _dot(dsb, kblk)
            dk_acc[pl.ds(r0, FK), :] += _dot_tn(dsb, q2)
            dv_acc[pl.ds(r0, FK), :] += _dot_tn(pr.astype(BF16), do2)
            dca = jnp.sum(ds[:FQ], axis=0, keepdims=True)
            dcb = jnp.sum(ds[FQ:], axis=0, keepdims=True)
            for j in range(GATE_BLOCKS):
                cols = slice(j * BK, (j + 1) * BK)
                old = dc_ref[0, GATE_BLOCKS * kb + j]
                dc_ref[0, GATE_BLOCKS * kb + j] = jnp.where(sub == 0, old - dca[:, cols],
                                                            jnp.where(sub == 1, old - dcb[:, cols], old))

        def alive(kb):
            reach = bound + ct - _last_gate(cr_ref, kb) - lse
            return (jnp.max(reach) > UNDERFLOW_BOUND).astype(jnp.int32)

        tile(i, True)

        def cond(s):
            return jnp.logical_and(s[0] >= 0, s[1] > 0)

        def step(s):
            kb, _ = s
            tile(kb, False)
            return kb - 1, alive(kb - 1)

        lax.while_loop(cond, step, (i - 1, alive(i - 1)))
        dq_ref[...] = (_unstack_heads(dq_acc[...]) * SCALE).astype(dq_ref.dtype)

        @pl.when(i == nqb - 1)
        def _():
            ck = pltpu.make_async_copy(dk_acc, dk_hbm.at[p], sem.at[0])
            cv = pltpu.make_async_copy(dv_acc, dv_hbm.at[p], sem.at[1])
            ck.start()
            cv.start()
            ck.wait()
            cv.wait()

    blk = lambda off: pl.BlockSpec((FQ, LANES), lambda p, i: (i, off + p))
    slab = lambda off: pl.BlockSpec((S, LANES), lambda p, i: (0, off + p))
    cols = pl.BlockSpec((1, FQ, 2), lambda p, i: (p, i, 0))
    rows = pl.BlockSpec((1, S // BK, 8, LANES), lambda p, i: (p, 0, 0, 0))
    return pl.pallas_call(
        body, name=name, grid=(4, nqb),
        in_specs=[blk(0), slab(0), slab(GROUP_COLS), cols, rows, cols, blk(0), blk(0), ANY, ANY, rows],
        out_specs=[blk(0), ANY, ANY, rows],
        out_shape=[jax.ShapeDtypeStruct((S, MIX_WIDTH), BF16),
                   jax.ShapeDtypeStruct((4, S, LANES), F32), jax.ShapeDtypeStruct((4, S, LANES), F32),
                   jax.ShapeDtypeStruct((4, S // BK, 8, LANES), F32)],
        scratch_shapes=[pltpu.VMEM((2 * FQ, LANES), F32), pltpu.VMEM((S, LANES), F32),
                        pltpu.VMEM((S, LANES), F32), pltpu.VMEM((1, LANES), F32),
                        pltpu.VMEM((2 * FQ, FK), F32), pltpu.SemaphoreType.DMA((2,))],
        compiler_params=_params(("arbitrary", "arbitrary")),
    )(proj, kv, kv, c_col, c_row, lse, merged, dmerged, dk_prev, dv_prev, dc_prev)


def _lane_scan(x, reverse):
    lane = lax.broadcasted_iota(jnp.int32, x.shape, 1)
    d = 1
    while d < LANES:
        if reverse:
            x = x + jnp.where(lane < LANES - d, pltpu.roll(x, LANES - d, 1), 0.0)
        else:
            x = x + jnp.where(lane >= d, pltpu.roll(x, d, 1), 0.0)
        d *= 2
    return x


def _gate_fwd(fl3, b8):
    nb = fl3.shape[0]

    def body(fl_ref, b_ref, c_ref):
        def step(kb, carry):
            x = fl_ref[kb] + b_ref[...]
            sp, _ = _softplus_parts(-x)
            c = _lane_scan(-sp, False) + carry
            c_ref[kb] = c
            return c[:, LANES - 1:LANES]

        lax.fori_loop(0, nb, step, jnp.zeros((8, 1), F32))

    return pl.pallas_call(body, name="forget_gate_cumsum",
                          out_shape=jax.ShapeDtypeStruct(fl3.shape, F32),
                          compiler_params=_params())(fl3, b8)


def _gate_bwd(dc3, fl3, b8):
    nb = fl3.shape[0]

    def body(dc_ref, fl_ref, b_ref, dfl_ref, db_ref):
        def step(t, st):
            carry, dbs = st
            kb = nb - 1 - t
            g = _lane_scan(dc_ref[kb], True) + carry
            x = fl_ref[kb] + b_ref[...]
            e = jnp.exp(-jnp.abs(x))
            rcp = 1.0 / (1.0 + e)
            dfl = g * jnp.where(x >= 0.0, e * rcp, rcp)
            dfl_ref[kb] = dfl
            return g[:, 0:1], dbs + dfl

        _, dbs = lax.fori_loop(0, nb, step, (jnp.zeros((8, 1), F32), jnp.zeros((8, LANES), F32)))
        db_ref[...] = jnp.broadcast_to(jnp.sum(dbs, axis=1, keepdims=True), (8, LANES))

    return pl.pallas_call(body, name="forget_gate_bwd",
                          out_shape=[jax.ShapeDtypeStruct(fl3.shape, F32), jax.ShapeDtypeStruct((8, LANES), F32)],
                          compiler_params=_params())(dc3, fl3, b8)


MEM_TQ = 512
MEM_COLS = MEM_WIDTH // LANES


def _mem_fwd(proj, qcol, mkv, mix, name):
    S = proj.shape[0]
    M = mkv.shape[0]

    def body(q_ref, mk_ref, mv_ref, mix_ref, o_ref, lse_ref):
        q2 = _head_rows(q_ref, SCALE)
        s = _dot_nt(q2, mk_ref[...])
        m = jnp.max(s, axis=1, keepdims=True)
        pr = jnp.exp(s - m)
        l = jnp.sum(pr, axis=1, keepdims=True)
        o_ref[...] = _unstack_heads(_dot(pr.astype(BF16), mv_ref[...]) / l)
        lse = m + jnp.log(l)
        lse_ref[0] = _two_cols(lse[:MEM_TQ], lse[MEM_TQ:])

    return pl.pallas_call(
        body, name=name, grid=(MEM_COLS, S // MEM_TQ),
        in_specs=[pl.BlockSpec((MEM_TQ, LANES), lambda p, i: (i, qcol + p)),
                  pl.BlockSpec((M, LANES), lambda p, i: (0, p)),
                  pl.BlockSpec((M, LANES), lambda p, i: (0, MEM_COLS + p)), ANY],
        out_specs=[pl.BlockSpec((MEM_TQ, LANES), lambda p, i: (i, GROUP_COLS + p)),
                   pl.BlockSpec((1, MEM_TQ, 2), lambda p, i: (p, i, 0))],
        out_shape=[jax.ShapeDtypeStruct((S, MERGED_WIDTH), F32), jax.ShapeDtypeStruct((2, S, 2), F32)],
        input_output_aliases={3: 0},
        compiler_params=_params(("parallel", "parallel")),
    )(proj, mkv, mkv, mix)


def _mem_bwd(proj, qcol, mkv, lse, merged, dmerged, name):
    S = proj.shape[0]
    M = mkv.shape[0]

    def body(q_ref, mk_ref, mv_ref, lse_ref, o_ref, do_ref, dq_ref, dmk_ref, dmv_ref):
        @pl.when(pl.program_id(1) == 0)
        def _():
            dmk_ref[...] = jnp.zeros_like(dmk_ref)
            dmv_ref[...] = jnp.zeros_like(dmv_ref)

        q2 = _head_rows(q_ref, SCALE)
        do2 = _head_rows(do_ref)
        tot = _stack(*_pair_rowsum(do_ref[...].astype(F32) * o_ref[...]))
        ls = lse_ref[0]
        pr = jnp.exp(_dot_nt(q2, mk_ref[...]) - _stack(ls[:, 0:1], ls[:, 1:2]))
        ds = pr * (_dot_nt(do2, mv_ref[...]) - tot)
        dsb = ds.astype(BF16)
        dmk_ref[...] += _dot_tn(dsb, q2)
        dmv_ref[...] += _dot_tn(pr.astype(BF16), do2)
        dq_ref[...] = (_unstack_heads(_dot(dsb, mk_ref[...])) * SCALE).astype(dq_ref.dtype)

    blk = lambda off: pl.BlockSpec((MEM_TQ, LANES), lambda p, i: (i, off + p))
    acc = pl.BlockSpec((M, LANES), lambda p, i: (0, p))
    return pl.pallas_call(
        body, name=name, grid=(MEM_COLS, S // MEM_TQ),
        in_specs=[blk(qcol), pl.BlockSpec((M, LANES), lambda p, i: (0, p)),
                  pl.BlockSpec((M, LANES), lambda p, i: (0, MEM_COLS + p)),
                  pl.BlockSpec((1, MEM_TQ, 2), lambda p, i: (p, i, 0)), blk(GROUP_COLS), blk(GROUP_COLS)],
        out_specs=[blk(0), acc, acc],
        out_shape=[jax.ShapeDtypeStruct((S, MEM_WIDTH), BF16), jax.ShapeDtypeStruct((M, MEM_WIDTH), F32),
                   jax.ShapeDtypeStruct((M, MEM_WIDTH), F32)],
        compiler_params=_params(("parallel", "arbitrary")),
    )(proj, mkv, mkv, lse, merged, dmerged)


def _c_layouts(c3):
    nb = c3.shape[0]
    pairs = c3.reshape(nb, 4, 2, LANES).transpose(1, 0, 2, 3)
    c_row = jnp.pad(pairs, ((0, 0), (0, 0), (0, 6), (0, 0)))
    c_col = pairs.transpose(0, 1, 3, 2).reshape(4, nb * LANES, 2)
    return c_col, c_row


def _local_step(x, mem, wb, shards, sm, loss_target):
    S = x.shape[0]
    nb = S // BK
    vec = lambda a: a.reshape(1, D_MODEL)
    b8 = jnp.broadcast_to(sm["b_f"].reshape(8, 1), (8, LANES))

    saved = []
    shared = None
    h = x
    hn = _rms_fwd(h, vec(sm["norm1_g"][0]), "norm1_0")
    for l in range(DEPTH):
        if l == N_A:
            w_kvf = jnp.pad(wb["w_kv_shared"], ((0, 0), (0, KVF_WIDTH - W_KV_SHARED)))
            hs = _rms_fwd(h, vec(sm["kv_norm_g"]), "kv_norm")
            kvf = _mm(hs, w_kvf, out_dtype=F32, name="kv_shared_proj")
            kv = kvf[:, :2 * MIX_WIDTH].astype(BF16)
            fl3 = kvf[:, 2 * MIX_WIDTH:2 * MIX_WIDTH + 8].T.reshape(8, nb, LANES).transpose(1, 0, 2)
            c3 = _gate_fwd(fl3, b8)
            c_col, c_row = _c_layouts(c3)
            shared = dict(h=h, hs=hs, kv=kv, fl3=fl3, c_col=c_col, c_row=c_row)
        mn = _rms_fwd(mem, vec(sm["mem_norm_g"][l]), f"mem_norm_{l}")
        mkv = _mm(mn, wb["w_mem_kv"][l], name=f"mem_kv_proj_{l}")
        if l < N_A:
            w_in = wb["w_in_a"][l]
            proj = _mm(hn, w_in, name=f"in_proj_{l}")
            mix, gathered = _sb_fwd(proj, f"stickbreak_fwd_{l}", _AllGather, shards[l])
            _unpack_gathered(PARTS[1 + l], gathered, wb)
            lse, qcol = None, Q_MEM_COL_A
        else:
            w_in = wb["w_in_b"][l - N_A]
            proj = _mm(hn, w_in, name=f"in_proj_{l}")
            mix, lse = _fox_fwd(proj, shared["kv"], shared["c_col"], shared["c_row"], f"fox_fwd_{l}")
            qcol = Q_MEM_COL_B
        merged, mlse = _mem_fwd(proj, qcol, mkv, mix, f"mem_attn_fwd_{l}")
        h_mid, hn2 = _mm(merged, wb["w_o"][l], epi="add_rms", extra=(h, vec(sm["norm2_g"][l])),
                         name=f"out_proj_{l}")
        u, act = _mm(hn2, wb["w_mlp1"][l], epi="relu2", name=f"mlp1_{l}")
        saved.append(dict(h=h, hn=hn, mn=mn, mkv=mkv, proj=proj, lse=lse, mlse=mlse, qcol=qcol, merged=merged,
                          h_mid=h_mid, hn2=hn2, u=u, act=act, w_in=w_in))
        if l + 1 < DEPTH:
            h, hn = _mm(act, wb["w_mlp2"][l], epi="add_rms", extra=(h_mid, vec(sm["norm1_g"][l + 1])),
                        name=f"mlp2_{l}")
        else:
            h = _mm(act, wb["w_mlp2"][l], out_dtype=F32, epi="add", extra=h_mid, name=f"mlp2_{l}")

    loss, dh, dhb, dg_final = _final_loss(h, vec(sm["final_norm_g"]), loss_target)

    gb = {n: [None] * (DEPTH if n not in ("w_in_a", "w_in_b") else 2) for n in
          ("w_in_a", "w_in_b", "w_mem_kv", "w_o", "w_mlp1", "w_mlp2")}
    gs = {n: [None] * DEPTH for n in ("norm1_g", "mem_norm_g", "norm2_g")}
    received = [None] * N_A
    dk_sh = jnp.zeros((4, S, LANES), F32)
    dv_sh = jnp.zeros((4, S, LANES), F32)
    dc_sh = jnp.zeros((4, nb, 8, LANES), F32)
    for l in reversed(range(DEPTH)):
        sv = saved[l]
        du = _mm(dhb, wb["w_mlp2"][l], mode="nt", epi="drelu2", extra=sv["u"], name=f"mlp2_dx_{l}")
        gb["w_mlp2"][l] = _mm(sv["act"], dhb, mode="tn", out_dtype=F32, name=f"mlp2_dw_{l}")
        gb["w_mlp1"][l] = _mm(sv["hn2"], du, mode="tn", out_dtype=F32, name=f"mlp1_dw_{l}")
        dh, dhb, gs["norm2_g"][l] = _mm(du, wb["w_mlp1"][l], mode="nt", epi="rms_bwd",
                                        extra=(sv["h_mid"], vec(sm["norm2_g"][l]), dh),
                                        name=f"mlp1_dx_norm2_bwd_{l}")
        dmerged = _mm(dhb, wb["w_o"][l], mode="nt", name=f"out_proj_dx_{l}")
        gb["w_o"][l] = _mm(sv["merged"], dhb, mode="tn", out_dtype=F32, name=f"out_proj_dw_{l}")
        if l < N_A:
            ready = _pack_grads(PARTS[1 + l], PART_ROWS[1 + l], gb, None)
            dq, dk, dv, received[l] = _sb_bwd(sv["proj"], sv["merged"], dmerged, f"stickbreak_bwd_{l}",
                                              _Scatter, ready)
        else:
            dq, dk_sh, dv_sh, dc_sh = _fox_bwd(sv["proj"], shared["kv"], shared["c_col"], shared["c_row"],
                                               sv["lse"], sv["merged"], dmerged, dk_sh, dv_sh, dc_sh,
                                               f"fox_bwd_{l}")
        dqm, dmk, dmv = _mem_bwd(sv["proj"], sv["qcol"], sv["mkv"], sv["mlse"], sv["merged"], dmerged,
                                 f"mem_attn_bwd_{l}")
        if l < N_A:
            flat = lambda t: t.transpose(1, 0, 2).reshape(S, MIX_WIDTH).astype(BF16)
            dproj = jnp.concatenate([dq, flat(dk), flat(dv), dqm], axis=1)
        else:
            dproj = jnp.concatenate([dq, dqm], axis=1)
        name_in = "w_in_a" if l < N_A else "w_in_b"
        gb[name_in][l if l < N_A else l - N_A] = _mm(sv["hn"], dproj, mode="tn", out_dtype=F32,
                                                      name=f"in_proj_dw_{l}")
        dh, dhb, gs["norm1_g"][l] = _mm(dproj, sv["w_in"], mode="nt", epi="rms_bwd",
                                        extra=(sv["h"], vec(sm["norm1_g"][l]), dh),
                                        name=f"in_proj_dx_norm1_bwd_{l}")
        dmkv = jnp.concatenate([dmk, dmv], axis=1)
        gb["w_mem_kv"][l] = _mm(sv["mn"], dmkv, mode="tn", out_dtype=F32, name=f"mem_kv_dw_{l}")
        dmn = _mm(dmkv, wb["w_mem_kv"][l], mode="nt", out_dtype=F32, name=f"mem_kv_dx_{l}")
        gs["mem_norm_g"][l] = _rms_gain_grad(mem, dmn, f"mem_norm_bwd_{l}")
        if l == N_A:
            dfl3, db8 = _gate_bwd(dc_sh.reshape(4, nb, 8, LANES)[:, :, :2].transpose(1, 0, 2, 3).reshape(nb, 8, LANES),
                                  shared["fl3"], b8)
            dfl = dfl3.transpose(1, 0, 2).reshape(8, S).T
            flat = lambda t: t.transpose(1, 0, 2).reshape(S, MIX_WIDTH).astype(BF16)
            dkvf = jnp.concatenate([flat(dk_sh), flat(dv_sh),
                                    jnp.pad(dfl, ((0, 0), (0, LANES - 8))).astype(BF16)], axis=1)
            gb["w_kv_shared"] = _mm(shared["hs"], dkvf, mode="tn", out_dtype=F32, name="kv_shared_dw")[:, :W_KV_SHARED]
            dh, dhb, g_kvn = _mm(dkvf, w_kvf, mode="nt", epi="rms_bwd",
                                 extra=(shared["h"], vec(sm["kv_norm_g"]), dh), name="kv_shared_dx_norm_bwd")
            g_bf = db8[:, 0]

    gsmall = {n: jnp.concatenate(v, axis=0) for n, v in gs.items()}
    gsmall["kv_norm_g"] = g_kvn
    gsmall["final_norm_g"] = dg_final
    gsmall["b_f"] = g_bf
    return loss, dh, gb, gsmall, received


def kernel(x, mem, norm1_g, w_in_a, w_in_b, w_mem_kv, mem_norm_g, w_o, norm2_g, w_mlp1, w_mlp2, kv_norm_g, w_kv_shared, b_f, final_norm_g, loss_target, m_norm1_g, m_w_in_a, m_w_in_b, m_w_mem_kv, m_mem_norm_g, m_w_o, m_norm2_g, m_w_mlp1, m_w_mlp2, m_kv_norm_g, m_w_kv_shared, m_b_f, m_final_norm_g, v_norm1_g, v_w_in_a, v_w_in_b, v_w_mem_kv, v_mem_norm_g, v_w_o, v_norm2_g, v_w_mlp1, v_w_mlp2, v_kv_norm_g, v_w_kv_shared, v_b_f, v_final_norm_g):
    big_w = dict(w_in_a=w_in_a, w_in_b=w_in_b, w_mem_kv=w_mem_kv, w_o=w_o, w_mlp1=w_mlp1, w_mlp2=w_mlp2,
                 w_kv_shared=w_kv_shared)
    small_w = dict(norm1_g=norm1_g, mem_norm_g=mem_norm_g, norm2_g=norm2_g, kv_norm_g=kv_norm_g,
                   final_norm_g=final_norm_g, b_f=b_f)
    big_m = dict(w_in_a=m_w_in_a, w_in_b=m_w_in_b, w_mem_kv=m_w_mem_kv, w_o=m_w_o, w_mlp1=m_w_mlp1,
                 w_mlp2=m_w_mlp2, w_kv_shared=m_w_kv_shared)
    small_m = dict(norm1_g=m_norm1_g, mem_norm_g=m_mem_norm_g, norm2_g=m_norm2_g, kv_norm_g=m_kv_norm_g,
                   final_norm_g=m_final_norm_g, b_f=m_b_f)
    big_v = dict(w_in_a=v_w_in_a, w_in_b=v_w_in_b, w_mem_kv=v_w_mem_kv, w_o=v_w_o, w_mlp1=v_w_mlp1,
                 w_mlp2=v_w_mlp2, w_kv_shared=v_w_kv_shared)
    small_v = dict(norm1_g=v_norm1_g, mem_norm_g=v_mem_norm_g, norm2_g=v_norm2_g, kv_norm_g=v_kv_norm_g,
                   final_norm_g=v_final_norm_g, b_f=v_b_f)

    def pack(k, big, small, dtype):
        return _pack_local(PARTS[k], PART_ROWS[k], big, small if k == 0 else None, dtype)

    def pack_all(big, small):
        return jnp.concatenate([pack(k, big, small, F32) for k in range(len(PARTS))], axis=0)

    wb = {n: {} for n in BIG_NAMES}
    _unpack_gathered(PARTS[0], _allgather_chips(pack(0, big_w, small_w, BF16)), wb)
    shards = [pack(1 + l, big_w, None, BF16) for l in range(N_A)]

    loss, dx, gb, gsmall, received = _local_step(x[0], mem[0], wb, shards, small_w, loss_target[0])

    received = [_scatter_chips(_pack_grads(PARTS[0], PART_ROWS[0], gb, gsmall))] + received
    part = jnp.concatenate([_sum4(r) for r in received], axis=0)
    other = _swap_cores(part)
    g, delta, new_m, new_v = _adamw(part, other, pack_all(big_w, small_w), pack_all(big_m, small_m),
                                    pack_all(big_v, small_v))

    outs = [lax.psum(loss[0, 0], ("x", "y", "c")), dx[None]]
    for packed in (g, delta, new_m, new_v):
        pieces, d, off = {n: [] for n in BIG_NAMES}, {}, 0
        for k, part_k in enumerate(PARTS):
            _unpack_local(part_k, packed[off:off + PART_ROWS[k]], k == 0, pieces, d)
            off += PART_ROWS[k]
        d.update(_join_layers(pieces))
        outs.extend(d[n] for n in WEIGHT_ORDER)
    return tuple(outs)
```

```python
import functools
import math

import jax
import jax.numpy as jnp
from jax import lax
from jax.experimental import pallas as pl
from jax.experimental.pallas import tpu as pltpu

F32 = jnp.float32
BF16 = jnp.bfloat16

D_MODEL = 1024
HEAD_DIM = 64
MIX_WIDTH = 512
MEM_WIDTH = 256
MERGED_WIDTH = MIX_WIDTH + MEM_WIDTH
DEPTH = 4
N_A = 2
D_FF = 4096
EPS = 1e-6
NEG_INF = -1e30
SCALE = 1.0 / math.sqrt(HEAD_DIM)

ADAM_LR = 0.001
ADAM_B1 = 0.9
ADAM_B2 = 0.999
ADAM_EPS = 1e-08
ADAM_WD = 0.01
ADAM_STEP = 10

LANES = 128
GROUP_COLS = MIX_WIDTH // LANES
Q_MEM_COL_A = 3 * GROUP_COLS
Q_MEM_COL_B = GROUP_COLS
W_KV_SHARED = 2 * MIX_WIDTH + 8
KVF_WIDTH = 1152
BQ = 256
BK = 128
DIAG_TILES = BQ // BK
CHAINS = 2
UNDERFLOW_BOUND = -110.0
VMEM_LIMIT = 56 * 1024 * 1024

MESH = pl.DeviceIdType.MESH
N_CHIPS = 4

PARTS = (
    (("w_in_a", 0, 1, (1024, 448), 1),
     ("w_mem_kv", 0, 1, (256, 512), 0)),
    (("w_o", 0, 1, (768, 256), 1),
     ("w_mlp1", 0, 1, (1024, 1024), 1),
     ("w_mlp2", 0, 1, (1024, 1024), 0),
     ("w_in_a", 1, 2, (1024, 448), 1),
     ("w_mem_kv", 1, 2, (256, 512), 0)),
    (("w_o", 1, 4, (768, 256), 1),
     ("w_mlp1", 1, 4, (1024, 1024), 1),
     ("w_mlp2", 1, 4, (1024, 1024), 0),
     ("w_in_b", 0, 2, (256, 768), 0),
     ("w_mem_kv", 2, 4, (256, 512), 0),
     ("w_kv_shared", None, None, (1024, 258), 1)),
)
BIG_NAMES = ("w_in_a", "w_in_b", "w_mem_kv", "w_o", "w_mlp1", "w_mlp2", "w_kv_shared")
SMALL = (
    ("norm1_g", (4, 1024)),
    ("mem_norm_g", (4, 1024)),
    ("norm2_g", (4, 1024)),
    ("kv_norm_g", (1, 1024)),
    ("final_norm_g", (1, 1024)),
    ("b_f", (1, 1024)),
)
WEIGHT_ORDER = ("norm1_g", "w_in_a", "w_in_b", "w_mem_kv", "mem_norm_g", "w_o", "norm2_g", "w_mlp1",
                "w_mlp2", "kv_norm_g", "w_kv_shared", "b_f", "final_norm_g")


ROW_ALIGN = 16
PACK_TILE = 256
SMALL_ROWS = ROW_ALIGN
assert sum(s[0] for _, s in SMALL) <= SMALL_ROWS


def _section_rows(entry):
    _, lo, hi, shape, _ = entry
    rows = (1 if lo is None else hi - lo) * math.prod(shape) // D_MODEL
    return rows, -(-rows // ROW_ALIGN) * ROW_ALIGN


def _round_up(n, m):
    return -(-n // m) * m


SUM_TILE = 128
_used = [sum(_section_rows(e)[1] for e in part) for part in PARTS]
PART_ROWS = [_round_up(_used[0] + SMALL_ROWS, SUM_TILE), _round_up(_used[1], SUM_TILE)]
PART_ROWS.append(_round_up(_used[2] + sum(PART_ROWS), PACK_TILE) - sum(PART_ROWS))
assert PART_ROWS[2] % SUM_TILE == 0


def _params(sem=None):
    return pltpu.CompilerParams(dimension_semantics=sem, vmem_limit_bytes=VMEM_LIMIT)


def _pick(n, cands):
    for c in cands:
        if n % c == 0:
            return c
    raise ValueError(f"no tile for {n}")


def _section(a, entry):
    a = a.reshape(-1, D_MODEL)
    return jnp.pad(a, ((0, _section_rows(entry)[1] - a.shape[0]), (0, 0)))


def _small_block(small, dtype):
    blk = jnp.zeros((SMALL_ROWS, D_MODEL), dtype)
    off = 0
    for n, shp in SMALL:
        a = small[n].astype(dtype)
        if n == "b_f":
            blk = blk.at[off, :a.size].set(a.reshape(-1))
        else:
            blk = blk.at[off:off + shp[0]].set(a.reshape(shp))
        off += shp[0]
    return blk


def _fill(parts, rows, dtype):
    used = sum(p.shape[0] for p in parts)
    return jnp.concatenate(parts + [jnp.zeros((rows - used, D_MODEL), dtype)], axis=0)


def _pack_local(part, rows, big, small, dtype):
    parts = [_section((big[e[0]] if e[1] is None else big[e[0]][e[1]:e[2]]).astype(dtype), e) for e in part]
    if small is not None:
        parts.append(_small_block(small, dtype))
    return _fill(parts, rows, dtype)


def _unpack_local(part, p, with_small, pieces, small):
    off = 0
    for e in part:
        n, lo, hi, shp, _ = e
        rows, reserved = _section_rows(e)
        pieces[n].append((lo, p[off:off + rows].reshape(shp if lo is None else (hi - lo,) + shp)))
        off += reserved
    if with_small:
        for n, shp in SMALL:
            a = p[off:off + shp[0]]
            small[n] = a[0, :8] if n == "b_f" else (a.reshape(D_MODEL) if shp[0] == 1 else a)
            off += shp[0]


def _join_layers(pieces):
    out = {}
    for n, ps in pieces.items():
        ps = sorted(ps, key=lambda t: -1 if t[0] is None else t[0])
        out[n] = ps[0][1] if len(ps) == 1 else jnp.concatenate([a for _, a in ps], axis=0)
    return out


def _unpack_gathered(part, g, weights):
    off = 0
    for e in part:
        n, lo, hi, shp, ax = e
        rows, reserved = _section_rows(e)
        if lo is None:
            sec = g[:, off:off + rows].reshape((N_CHIPS,) + shp)
            weights[n] = jnp.concatenate([sec[j] for j in range(N_CHIPS)], axis=ax)
        else:
            sec = g[:, off:off + rows].reshape((N_CHIPS, hi - lo) + shp)
            for l in range(lo, hi):
                weights[n][l] = jnp.concatenate([sec[j, l - lo] for j in range(N_CHIPS)], axis=ax)
        off += reserved


def _pack_grads(part, rows, gbig, gsmall):
    small = None if gsmall is None else _small_block(gsmall, BF16)
    chunks = []
    for j in range(N_CHIPS):
        parts = []
        for e in part:
            n, lo, hi, shp, ax = e
            w = shp[ax]
            layers = [gbig[n]] if lo is None else [gbig[n][l] for l in range(lo, hi)]
            cut = [lax.slice_in_dim(g, j * w, (j + 1) * w, axis=ax).astype(BF16).reshape(-1, D_MODEL) for g in layers]
            parts.append(_section(cut[0] if len(cut) == 1 else jnp.concatenate(cut, axis=0), e))
        if small is not None:
            parts.append(small)
        chunks.append(_fill(parts, rows, BF16))
    return jnp.stack(chunks, axis=0)


ANY = pl.BlockSpec(memory_space=pl.ANY)


def _other_chips(x, y):
    return [(1 - x, y), (x, 1 - y), (1 - x, 1 - y)]


class _AllGather:
    SCRATCH = [pltpu.SemaphoreType.DMA((3,)), pltpu.SemaphoreType.DMA((3,)), pltpu.SemaphoreType.DMA((3,)),
               pltpu.SemaphoreType.DMA((3,)), pltpu.SemaphoreType.DMA]

    def __init__(self, w_ref, o_ref, send_sems, recv_sems, pass_send, pass_recv, local_sem):
        self.w_ref, self.o_ref = w_ref, o_ref
        self.sems = (send_sems, recv_sems, pass_send, pass_recv, local_sem)
        x, y, c = lax.axis_index("x"), lax.axis_index("y"), lax.axis_index("c")
        half = w_ref.shape[0] // 2
        self.c, self.me, self.sibling = c, 2 * x + y, (x, y, 1 - c)
        self.mine = pl.ds(pl.multiple_of(c * half, ROW_ALIGN), half)
        self.other = pl.ds(pl.multiple_of((1 - c) * half, ROW_ALIGN), half)
        self.chips = _other_chips(x, y)

    def _over_ici(self, j, rows_of):
        chip = self.chips[j]
        return pltpu.make_async_remote_copy(
            src_ref=self.w_ref.at[self.mine], dst_ref=self.o_ref.at[rows_of, self.mine],
            send_sem=self.sems[0].at[j], recv_sem=self.sems[1].at[j],
            device_id=(chip[0], chip[1], self.c), device_id_type=MESH)

    def _over_d2d(self, j, rows):
        where = self.o_ref.at[2 * self.chips[j][0] + self.chips[j][1], rows]
        return pltpu.make_async_remote_copy(src_ref=where, dst_ref=where, send_sem=self.sems[2].at[j],
                                            recv_sem=self.sems[3].at[j], device_id=self.sibling,
                                            device_id_type=MESH)

    def _local(self):
        return pltpu.make_async_copy(self.w_ref, self.o_ref.at[self.me], self.sems[4])

    def start(self):
        self._local().start()
        for j in range(3):
            self._over_ici(j, self.me).start()

    def finish(self):
        for j in range(3):
            self._over_ici(j, 2 * self.chips[j][0] + self.chips[j][1]).wait_recv()
            self._over_d2d(j, self.mine).start()
        for j in range(3):
            self._over_d2d(j, self.other).wait_recv()
        for j in range(3):
            self._over_ici(j, self.me).wait_send()
            self._over_d2d(j, self.mine).wait_send()
        self._local().wait()


class _Scatter:
    SCRATCH = [pltpu.SemaphoreType.DMA((3,)), pltpu.SemaphoreType.DMA((3,)), pltpu.SemaphoreType.DMA]

    def __init__(self, g_ref, o_ref, send_sems, recv_sems, local_sem):
        self.g_ref, self.o_ref, self.sems = g_ref, o_ref, (send_sems, recv_sems, local_sem)
        x, y, c = lax.axis_index("x"), lax.axis_index("y"), lax.axis_index("c")
        self.c, self.me, self.chips = c, 2 * x + y, _other_chips(x, y)

    def _copy(self, j):
        chip = self.chips[j]
        return pltpu.make_async_remote_copy(
            src_ref=self.g_ref.at[2 * chip[0] + chip[1]], dst_ref=self.o_ref.at[self.me],
            send_sem=self.sems[0].at[j], recv_sem=self.sems[1].at[j],
            device_id=(chip[0], chip[1], self.c), device_id_type=MESH)

    def _local(self):
        return pltpu.make_async_copy(self.g_ref.at[self.me], self.o_ref.at[self.me], self.sems[2])

    def start(self):
        self._local().start()
        for j in range(3):
            self._copy(j).start()

    def finish(self):
        for j in range(3):
            self._copy(j).wait()
        self._local().wait()


def _allgather_chips(w):
    def body(w_ref, o_ref, *sems):
        ag = _AllGather(w_ref, o_ref, *sems)
        ag.start()
        ag.finish()

    return pl.pallas_call(
        body, name="allgather_weights",
        out_shape=jax.ShapeDtypeStruct((N_CHIPS,) + w.shape, w.dtype),
        in_specs=[ANY], out_specs=ANY, scratch_shapes=_AllGather.SCRATCH,
    )(w)


def _scatter_chips(g4):
    def body(g_ref, o_ref, *sems):
        sc = _Scatter(g_ref, o_ref, *sems)
        sc.start()
        sc.finish()

    return pl.pallas_call(
        body, name="scatter_grads",
        out_shape=jax.ShapeDtypeStruct(g4.shape, g4.dtype),
        in_specs=[ANY], out_specs=ANY, scratch_shapes=_Scatter.SCRATCH,
    )(g4)


def _swap_cores(p):
    def body(p_ref, o_ref, send_sem, recv_sem):
        x, y, c = lax.axis_index("x"), lax.axis_index("y"), lax.axis_index("c")
        cp = pltpu.make_async_remote_copy(src_ref=p_ref, dst_ref=o_ref, send_sem=send_sem, recv_sem=recv_sem,
                                          device_id=(x, y, 1 - c), device_id_type=MESH)
        cp.start()
        cp.wait()

    return pl.pallas_call(
        body, name="swap_cores",
        out_shape=jax.ShapeDtypeStruct(p.shape, p.dtype),
        in_specs=[ANY], out_specs=ANY,
        scratch_shapes=[pltpu.SemaphoreType.DMA, pltpu.SemaphoreType.DMA],
    )(p)


def _sum4(r4):
    _, R, C = r4.shape

    def body(r_ref, o_ref):
        o_ref[...] = ((r_ref[0].astype(F32) + r_ref[1].astype(F32)) + r_ref[2].astype(F32)) + r_ref[3].astype(F32)

    return pl.pallas_call(
        body, name="sum_chips", grid=(R // SUM_TILE,),
        in_specs=[pl.BlockSpec((N_CHIPS, SUM_TILE, C), lambda i: (0, i, 0))],
        out_specs=pl.BlockSpec((SUM_TILE, C), lambda i: (i, 0)),
        out_shape=jax.ShapeDtypeStruct((R, C), F32),
        compiler_params=_params(("parallel",)),
    )(r4)


def _adamw(pa, pb, w, m, v):
    R, C = w.shape
    c1 = 1.0 - ADAM_B1
    c2 = 1.0 - ADAM_B2
    bc1 = 1.0 - ADAM_B1 ** ADAM_STEP
    bc2 = 1.0 - ADAM_B2 ** ADAM_STEP

    def body(pa_ref, pb_ref, w_ref, m_ref, v_ref, g_ref, d_ref, mo_ref, vo_ref):
        g = pa_ref[...] + pb_ref[...]
        mn = ADAM_B1 * m_ref[...] + c1 * g
        vn = ADAM_B2 * v_ref[...] + c2 * (g * g)
        m_hat = mn / bc1
        v_hat = vn / bc2
        g_ref[...] = g
        d_ref[...] = -ADAM_LR * (m_hat / (jnp.sqrt(v_hat) + ADAM_EPS) + ADAM_WD * w_ref[...])
        mo_ref[...] = mn
        vo_ref[...] = vn

    spec = pl.BlockSpec((PACK_TILE, C), lambda i: (i, 0))
    shp = jax.ShapeDtypeStruct((R, C), F32)
    return pl.pallas_call(
        body, name="adamw", grid=(R // PACK_TILE,),
        in_specs=[spec] * 5, out_specs=[spec] * 4, out_shape=[shp] * 4,
        compiler_params=_params(("parallel",)),
    )(pa, pb, w, m, v)


def _rms_fwd(x, g, name):
    R, Dm = x.shape
    tr = _pick(R, (512, 256, 128))

    def body(x_ref, g_ref, o_ref):
        xf = x_ref[...]
        r = lax.rsqrt(jnp.mean(xf * xf, axis=-1, keepdims=True) + EPS)
        o_ref[...] = (xf * r * g_ref[...]).astype(o_ref.dtype)

    return pl.pallas_call(
        body, name=name, grid=(R // tr,),
        in_specs=[pl.BlockSpec((tr, Dm), lambda i: (i, 0)), pl.BlockSpec((1, Dm), lambda i: (0, 0))],
        out_specs=pl.BlockSpec((tr, Dm), lambda i: (i, 0)),
        out_shape=jax.ShapeDtypeStruct((R, Dm), BF16),
        compiler_params=_params(("parallel",)),
    )(x, g)


def _rms_gain_grad(x, dy, name):
    R, Dm = x.shape
    tr = _pick(R, (256, 128))

    def body(x_ref, dy_ref, dg_ref):
        xf = x_ref[...]
        r = lax.rsqrt(jnp.mean(xf * xf, axis=-1, keepdims=True) + EPS)

        @pl.when(pl.program_id(0) == 0)
        def _():
            dg_ref[...] = jnp.zeros_like(dg_ref)

        dg_ref[...] += jnp.sum(dy_ref[...] * (xf * r), axis=0, keepdims=True)

    row = pl.BlockSpec((tr, Dm), lambda i: (i, 0))
    return pl.pallas_call(
        body, name=name, grid=(R // tr,),
        in_specs=[row, row], out_specs=pl.BlockSpec((1, Dm), lambda i: (0, 0)),
        out_shape=jax.ShapeDtypeStruct((1, Dm), F32),
        compiler_params=_params(("arbitrary",)),
    )(x, dy)


def _final_loss(x, g, tgt):
    R, Dm = x.shape
    tr = _pick(R, (256, 128))

    def body(x_ref, g_ref, t_ref, l_ref, dx_ref, dxb_ref, dg_ref):
        xf = x_ref[...]
        gv = g_ref[...]
        r = lax.rsqrt(jnp.mean(xf * xf, axis=-1, keepdims=True) + EPS)
        xr = xf * r
        err = xr * gv - t_ref[...]
        dy_ = err * (1.0 / Dm)
        gdy = dy_ * gv
        mdot = jnp.mean(xf * gdy, axis=-1, keepdims=True)
        dx = r * gdy - xf * ((r * r * r) * mdot)
        dx_ref[...] = dx
        dxb_ref[...] = dx.astype(BF16)

        @pl.when(pl.program_id(0) == 0)
        def _():
            dg_ref[...] = jnp.zeros_like(dg_ref)
            l_ref[...] = jnp.zeros_like(l_ref)

        dg_ref[...] += jnp.sum(dy_ * xr, axis=0, keepdims=True)
        sq = jnp.sum(err * err, axis=1, keepdims=True)
        l_ref[...] += jnp.sum(sq, axis=0, keepdims=True) * (0.5 / Dm)

    row = pl.BlockSpec((tr, Dm), lambda i: (i, 0))
    vec = pl.BlockSpec((1, Dm), lambda i: (0, 0))
    return pl.pallas_call(
        body, name="final_norm_loss", grid=(R // tr,),
        in_specs=[row, vec, row],
        out_specs=[pl.BlockSpec((1, 1), lambda i: (0, 0)), row, row, vec],
        out_shape=[jax.ShapeDtypeStruct((1, 1), F32), jax.ShapeDtypeStruct((R, Dm), F32),
                   jax.ShapeDtypeStruct((R, Dm), BF16), jax.ShapeDtypeStruct((1, Dm), F32)],
        compiler_params=_params(("arbitrary",)),
    )(x, g, tgt)


MAX_TK = 2048

_DIMS = {"nn": (((1,), (0,)), ((), ())), "nt": (((1,), (1,)), ((), ())), "tn": (((0,), (0,)), ((), ()))}


def _mm(a, b, *, mode="nn", out_dtype=BF16, epi=None, extra=None, name):
    if mode == "nn":
        (M, K), N = a.shape, b.shape[1]
    elif mode == "nt":
        (M, K), N = a.shape, b.shape[0]
    else:
        (K, M), N = a.shape, b.shape[1]
    tm = _pick(M, (512, 256, 128) if epi == "rms_bwd" else (1024, 768, 512, 256, 128))
    tn = _pick(N, (1024, 896, 768, 640, 512, 384, 256, 128))
    tk = K if K <= MAX_TK else _pick(K, (MAX_TK, 1024, 512, 256, 128))
    nk = K // tk
    extras = () if extra is None else (extra if isinstance(extra, tuple) else (extra,))
    n_out = {"relu2": 2, "add_rms": 2, "rms_bwd": 3}.get(epi, 1)
    assert epi not in ("rms_bwd", "add_rms") or tn == N

    def body(*refs):
        a_ref, b_ref = refs[:2]
        e_refs = refs[2:2 + len(extras)]
        e_ref = e_refs[0] if e_refs else None
        outs = refs[2 + len(extras):2 + len(extras) + n_out]
        k = pl.program_id(2)
        part = lax.dot_general(a_ref[...].astype(BF16), b_ref[...].astype(BF16), _DIMS[mode],
                               preferred_element_type=F32)

        def finish(acc):
            if epi is None:
                outs[0][...] = acc.astype(outs[0].dtype)
            elif epi == "add":
                outs[0][...] = (e_ref[...] + acc).astype(outs[0].dtype)
            elif epi == "add_rms":
                y = e_refs[0][...] + acc
                outs[0][...] = y
                r = lax.rsqrt(jnp.mean(y * y, axis=-1, keepdims=True) + EPS)
                outs[1][...] = (y * r * e_refs[1][...]).astype(BF16)
            elif epi == "relu2":
                outs[0][...] = acc.astype(BF16)
                rl = jnp.maximum(acc, 0.0)
                outs[1][...] = (rl * rl).astype(BF16)
            elif epi == "drelu2":
                u = e_ref[...].astype(F32)
                outs[0][...] = (acc * (2.0 * jnp.maximum(u, 0.0))).astype(outs[0].dtype)
            elif epi == "rms_bwd":
                x_ref, g_ref, dres_ref = e_refs
                xf = x_ref[...]
                r = lax.rsqrt(jnp.mean(xf * xf, axis=-1, keepdims=True) + EPS)
                gdy = acc * g_ref[...]
                mdot = jnp.mean(xf * gdy, axis=-1, keepdims=True)
                dx = dres_ref[...] + (r * gdy - xf * ((r * r * r) * mdot))
                outs[0][...] = dx
                outs[1][...] = dx.astype(BF16)

                @pl.when(pl.program_id(0) == 0)
                def _():
                    outs[2][...] = jnp.zeros_like(outs[2])

                outs[2][...] += jnp.sum(acc * (xf * r), axis=0, keepdims=True)

        if nk == 1:
            finish(part)
        else:
            acc_ref = refs[-1]

            @pl.when(k == 0)
            def _():
                acc_ref[...] = part

            @pl.when(jnp.logical_and(k > 0, k < nk - 1))
            def _():
                acc_ref[...] += part

            @pl.when(k == nk - 1)
            def _():
                finish(acc_ref[...] + part)

    if mode == "tn":
        a_spec = pl.BlockSpec((tk, tm), lambda i, j, k: (k, i))
    else:
        a_spec = pl.BlockSpec((tm, tk), lambda i, j, k: (i, k))
    if mode == "nt":
        b_spec = pl.BlockSpec((tn, tk), lambda i, j, k: (j, k))
    else:
        b_spec = pl.BlockSpec((tk, tn), lambda i, j, k: (k, j))
    o_spec = pl.BlockSpec((tm, tn), lambda i, j, k: (i, j))
    vec_spec = pl.BlockSpec((1, tn), lambda i, j, k: (0, j))
    ins, in_specs = [a, b] + list(extras), [a_spec, b_spec]
    if epi == "rms_bwd":
        in_specs += [o_spec, vec_spec, o_spec]
        out_shape = [jax.ShapeDtypeStruct((M, N), F32), jax.ShapeDtypeStruct((M, N), BF16),
                     jax.ShapeDtypeStruct((1, N), F32)]
        out_specs = [o_spec, o_spec, vec_spec]
    elif epi == "add_rms":
        in_specs += [o_spec, vec_spec]
        out_shape = [jax.ShapeDtypeStruct((M, N), F32), jax.ShapeDtypeStruct((M, N), BF16)]
        out_specs = [o_spec, o_spec]
    else:
        in_specs += [o_spec] * len(extras)
        out_shape = [jax.ShapeDtypeStruct((M, N), BF16 if epi == "relu2" else out_dtype)] * n_out
        out_specs = [o_spec] * n_out
    res = pl.pallas_call(
        body, name=name, grid=(M // tm, N // tn, nk),
        in_specs=in_specs, out_specs=out_specs, out_shape=out_shape,
        scratch_shapes=[pltpu.VMEM((tm, tn), F32)] if nk > 1 else [],
        compiler_params=_params(("arbitrary",) * 3 if epi == "rms_bwd" else ("parallel", "parallel", "arbitrary")),
    )(*ins)
    return res if n_out > 1 else res[0]


def _dot(a, b):
    return lax.dot_general(a, b, _DIMS["nn"], preferred_element_type=F32)


def _dot_nt(a, b):
    return lax.dot_general(a, b, _DIMS["nt"], preferred_element_type=F32)


def _dot_tn(a, b):
    return lax.dot_general(a, b, _DIMS["tn"], preferred_element_type=F32)


def _split_dot(x, t):
    hi = x.astype(BF16)
    lo = (x - hi.astype(F32)).astype(BF16)
    return _dot(jnp.concatenate([hi, lo], axis=1), jnp.concatenate([t, t], axis=0))


def _head_pair(ref, scale=None):
    xf = ref[...].astype(F32)
    if scale is not None:
        xf = xf * scale
    is_a = lax.broadcasted_iota(jnp.int32, xf.shape, 1) < HEAD_DIM
    return jnp.where(is_a, xf, 0.0).astype(BF16), jnp.where(is_a, 0.0, xf).astype(BF16)


def _stack(a, b):
    return jnp.concatenate([a, b], axis=0)


def _head_rows(ref, scale=None):
    return _stack(*_head_pair(ref, scale))


def _unstack_heads(x):
    rows = x.shape[0] // 2
    return _select_pair(x[:rows], x[rows:])


def _pair_rowsum(x):
    is_a = lax.broadcasted_iota(jnp.int32, x.shape, 1) < HEAD_DIM
    return (jnp.sum(jnp.where(is_a, x, 0.0), axis=1, keepdims=True),
            jnp.sum(jnp.where(is_a, 0.0, x), axis=1, keepdims=True))


def _select_pair(xa, xb):
    is_a = lax.broadcasted_iota(jnp.int32, xa.shape, 1) < HEAD_DIM
    return jnp.where(is_a, xa, xb)


def _two_cols(xa, xb):
    rows = xa.shape[0]
    first = lax.broadcasted_iota(jnp.int32, (rows, 2), 1) == 0
    return jnp.where(first, xa, xb)


def _softplus_parts(z):
    e = jnp.exp(-jnp.abs(z))
    return jnp.maximum(z, 0.0) + jnp.log(1.0 + e), e


def _tile_iotas():
    row = lax.broadcasted_iota(jnp.int32, (BK, BK), 0)
    col = lax.broadcasted_iota(jnp.int32, (BK, BK), 1)
    return row, col


def _stacked_iotas(bq, nk):
    row = lax.broadcasted_iota(jnp.int32, (2 * bq, nk), 0) & (bq - 1)
    col = lax.broadcasted_iota(jnp.int32, (2 * bq, nk), 1)
    return row, col


def _side_exchange(exchange, operand, n_in, n_out):
    out_shape = jax.ShapeDtypeStruct(((N_CHIPS,) + operand.shape) if exchange is _AllGather else operand.shape,
                                     operand.dtype)
    n_sem = len(exchange.SCRATCH)

    def pick(refs):
        def make():
            return exchange(refs[n_in], refs[n_in + 1 + n_out], *refs[len(refs) - n_sem:])

        return (lambda: make().start()), (lambda: make().finish())

    return [operand], [ANY], [out_shape], [ANY], pick


def _sb_fwd(proj, name, exchange, operand):
    S = proj.shape[0]
    nqb = S // (CHAINS * BQ)
    x_in, x_in_specs, x_out, x_out_specs, pick = _side_exchange(exchange, operand, 3, 1)

    def body(*refs):
        q_ref, k_ref, v_ref = refs[:3]
        o_ref = refs[3 + len(x_in)]
        acc_ref = refs[3 + len(x_in) + 1 + len(x_out)]
        start, finish = pick(refs)
        p = pl.program_id(0)
        i = pl.program_id(1)

        @pl.when(jnp.logical_and(p == 0, i == 0))
        def _():
            start()

        q2 = [_head_rows(q_ref.at[pl.ds(ch * BQ, BQ)], SCALE) for ch in range(CHAINS)]
        row, col = _tile_iotas()
        tri = (row > col).astype(BF16)
        srow, scol = _stacked_iotas(BQ, BK)
        acc_ref[...] = jnp.zeros_like(acc_ref)

        def tile(ch, kb, c, dmask=None, valid=None):
            r0 = pl.multiple_of(kb * BK, BK)
            kblk = k_ref[pl.ds(r0, BK), :]
            vblk = v_ref[pl.ds(r0, BK), :]
            z = _dot_nt(q2[ch], kblk)
            sp, _ = _softplus_parts(z)
            lm = -sp
            if dmask is not None:
                lm = jnp.where(dmask, lm, 0.0)
            btw = _split_dot(lm, tri)
            w = jnp.exp((z - sp) + btw + c)
            if dmask is not None:
                w = jnp.where(dmask, w, 0.0)
            if valid is not None:
                w = w * valid
            acc_ref[ch] += _dot(w.astype(BF16), vblk)
            return c + btw[:, 0:1] + lm[:, 0:1]

        def alive(c):
            return jnp.max(c) > UNDERFLOW_BOUND

        cs = [jnp.zeros((2 * BQ, 1), F32)] * CHAINS
        for d in reversed(range(DIAG_TILES)):
            cs = [tile(ch, (CHAINS * i + ch) * DIAG_TILES + d, cs[ch], dmask=scol < srow - d * BK)
                  for ch in range(CHAINS)]

        def tile_of(ch, t):
            return (CHAINS * i + ch) * DIAG_TILES - 1 - t

        def more(cs, t):
            go = [jnp.logical_and(alive(cs[ch]), tile_of(ch, t) >= 0) for ch in range(CHAINS)]
            return functools.reduce(jnp.logical_or, go).astype(jnp.int32)

        def step(st):
            t, _, cs = st
            new = []
            for ch in range(CHAINS):
                kb = tile_of(ch, t)
                if ch == CHAINS - 1:
                    new.append(tile(ch, kb, cs[ch]))
                else:
                    new.append(tile(ch, jnp.maximum(kb, 0), cs[ch], valid=(kb >= 0).astype(F32)))
            return t + 1, more(new, t + 1), new

        lax.while_loop(lambda st: st[1] > 0, step, (0, more(cs, 0), cs))
        for ch in range(CHAINS):
            o_ref[pl.ds(ch * BQ, BQ), :] = _unstack_heads(acc_ref[ch])

        @pl.when(jnp.logical_and(p == 3, i == nqb - 1))
        def _():
            finish()

    blk = pl.BlockSpec((CHAINS * BQ, LANES), lambda p, i: (i, p))
    res = pl.pallas_call(
        body, name=name, grid=(4, nqb),
        in_specs=[blk, pl.BlockSpec((S, LANES), lambda p, i: (0, GROUP_COLS + p)),
                  pl.BlockSpec((S, LANES), lambda p, i: (0, 2 * GROUP_COLS + p))] + x_in_specs,
        out_specs=[blk] + x_out_specs,
        out_shape=[jax.ShapeDtypeStruct((S, MERGED_WIDTH), F32)] + x_out,
        scratch_shapes=[pltpu.VMEM((CHAINS, 2 * BQ, LANES), F32)] + exchange.SCRATCH,
        compiler_params=_params(("arbitrary", "arbitrary")),
    )(proj, proj, proj, *x_in)
    return res


def _sb_bwd(proj, merged, dmerged, name, exchange, operand):
    S = proj.shape[0]
    nqb = S // (CHAINS * BQ)
    x_in, x_in_specs, x_out, x_out_specs, pick = _side_exchange(exchange, operand, 5, 3)

    def body(*refs):
        q_ref, k_ref, v_ref, o_ref, do_ref = refs[:5]
        dq_ref, dk_hbm, dv_hbm = refs[5 + len(x_in):8 + len(x_in)]
        dq_acc, dk_acc, dv_acc, sem = refs[8 + len(x_in) + len(x_out):12 + len(x_in) + len(x_out)]
        start, finish = pick(refs)
        p = pl.program_id(0)
        i = pl.program_id(1)

        @pl.when(jnp.logical_and(p == 0, i == 0))
        def _():
            start()

        @pl.when(i == 0)
        def _():
            dk_acc[...] = jnp.zeros_like(dk_acc)
            dv_acc[...] = jnp.zeros_like(dv_acc)

        rows = [pl.ds(ch * BQ, BQ) for ch in range(CHAINS)]
        q2 = [_head_rows(q_ref.at[rw], SCALE) for rw in rows]
        do2 = [_head_rows(do_ref.at[rw]) for rw in rows]
        tot = [_stack(*_pair_rowsum(do_ref[rw, :].astype(F32) * o_ref[rw, :])) for rw in rows]
        row, col = _tile_iotas()
        tri_gt = (row > col).astype(BF16)
        tri_ge = (row >= col).astype(BF16)
        srow, scol = _stacked_iotas(BQ, BK)
        dq_acc[...] = jnp.zeros_like(dq_acc)

        def tile(ch, kb, st, dmask=None, valid=None):
            masked = dmask is not None
            c, r = st
            r0 = pl.multiple_of(kb * BK, BK)
            kblk = k_ref[pl.ds(r0, BK), :]
            vblk = v_ref[pl.ds(r0, BK), :]
            z = _dot_nt(q2[ch], kblk)
            sp, e = _softplus_parts(z)
            lm = -sp
            if masked:
                lm = jnp.where(dmask, lm, 0.0)
            btw = _split_dot(lm, tri_gt)
            w = jnp.exp((z - sp) + btw + c)
            if masked:
                w = jnp.where(dmask, w, 0.0)
            if valid is not None:
                w = w * valid
            wb = w.astype(BF16)
            a = wb.astype(F32) * _dot_nt(do2[ch], vblk)
            suffix = _split_dot(a, tri_ge) + r
            rcp = 1.0 / (1.0 + e)
            pos = z >= 0.0
            sig = jnp.where(pos, rcp, e * rcp)
            sig_neg = jnp.where(pos, e * rcp, rcp)
            dz = a * sig_neg - (tot[ch] - suffix) * sig
            if masked:
                dz = jnp.where(dmask, dz, 0.0)
            if valid is not None:
                dz = dz * valid
            dzb = dz.astype(BF16)
            dq_acc[ch] += _dot(dzb, kblk)
            dk_acc[pl.ds(r0, BK), :] += _dot_tn(dzb, q2[ch])
            dv_acc[pl.ds(r0, BK), :] += _dot_tn(wb, do2[ch])
            return c + btw[:, 0:1] + lm[:, 0:1], suffix[:, 0:1]

        def alive(st):
            return jnp.max(st[0]) > UNDERFLOW_BOUND

        zero = jnp.zeros((2 * BQ, 1), F32)
        sts = [(zero, zero)] * CHAINS
        for d in reversed(range(DIAG_TILES)):
            sts = [tile(ch, (CHAINS * i + ch) * DIAG_TILES + d, sts[ch], dmask=scol < srow - d * BK)
                   for ch in range(CHAINS)]

        def tile_of(ch, t):
            return (CHAINS * i + ch) * DIAG_TILES - 1 - t

        def more(sts, t):
            go = [jnp.logical_and(alive(sts[ch]), tile_of(ch, t) >= 0) for ch in range(CHAINS)]
            return functools.reduce(jnp.logical_or, go).astype(jnp.int32)

        def step(s):
            t, _, sts = s
            new = []
            for ch in range(CHAINS):
                kb = tile_of(ch, t)
                if ch == CHAINS - 1:
                    new.append(tile(ch, kb, sts[ch]))
                else:
                    new.append(tile(ch, jnp.maximum(kb, 0), sts[ch], valid=(kb >= 0).astype(F32)))
            return t + 1, more(new, t + 1), new

        lax.while_loop(lambda s: s[1] > 0, step, (0, more(sts, 0), sts))
        for ch in range(CHAINS):
            dq_ref[rows[ch], :] = (_unstack_heads(dq_acc[ch]) * SCALE).astype(dq_ref.dtype)

        @pl.when(i == nqb - 1)
        def _():
            ck = pltpu.make_async_copy(dk_acc, dk_hbm.at[p], sem.at[0])
            cv = pltpu.make_async_copy(dv_acc, dv_hbm.at[p], sem.at[1])
            ck.start()
            cv.start()
            ck.wait()
            cv.wait()

        @pl.when(jnp.logical_and(p == 3, i == nqb - 1))
        def _():
            finish()

    blk = lambda off: pl.BlockSpec((CHAINS * BQ, LANES), lambda p, i: (i, off + p))
    slab = lambda off: pl.BlockSpec((S, LANES), lambda p, i: (0, off + p))
    return pl.pallas_call(
        body, name=name, grid=(4, nqb),
        in_specs=[blk(0), slab(GROUP_COLS), slab(2 * GROUP_COLS), blk(0), blk(0)] + x_in_specs,
        out_specs=[blk(0), ANY, ANY] + x_out_specs,
        out_shape=[jax.ShapeDtypeStruct((S, MIX_WIDTH), BF16),
                   jax.ShapeDtypeStruct((4, S, LANES), F32), jax.ShapeDtypeStruct((4, S, LANES), F32)] + x_out,
        scratch_shapes=[pltpu.VMEM((CHAINS, 2 * BQ, LANES), F32), pltpu.VMEM((S, LANES), F32),
                        pltpu.VMEM((S, LANES), F32), pltpu.SemaphoreType.DMA((2,))]
        + exchange.SCRATCH,
        compiler_params=_params(("arbitrary", "arbitrary")),
    )(proj, proj, proj, merged, dmerged, *x_in)


def _key_norm_max(k_ref, knorm_ref, nkb):
    def step(kb, m):
        r0 = pl.multiple_of(kb * BK, BK)
        blk = k_ref[pl.ds(r0, BK), :].astype(F32)
        sa, sb = _pair_rowsum(blk * blk)
        return (jnp.maximum(m[0], jnp.max(sa, axis=0, keepdims=True)),
                jnp.maximum(m[1], jnp.max(sb, axis=0, keepdims=True)))

    zero = jnp.zeros((1, 1), F32)
    ma, mb = lax.fori_loop(0, nkb, step, (zero, zero))
    knorm_ref[...] = _select_pair(jnp.broadcast_to(ma, (1, LANES)), jnp.broadcast_to(mb, (1, LANES)))


FQ = 512
FK = FQ
GATE_BLOCKS = FK // BK


def _key_gates(cr_ref, kb):
    blocks = [cr_ref[0, GATE_BLOCKS * kb + j] for j in range(GATE_BLOCKS)]
    per_head = [jnp.broadcast_to(jnp.concatenate([b[h:h + 1] for b in blocks], axis=1), (FQ, FK)) for h in range(2)]
    return _stack(*per_head)


def _last_gate(cr_ref, kb):
    last = cr_ref[0, GATE_BLOCKS * jnp.maximum(kb, 0) + GATE_BLOCKS - 1]
    return _stack(*[jnp.broadcast_to(last[h:h + 1, BK - 1:BK], (FQ, 1)) for h in range(2)])


def _logit_bound(q_ref, knorm_ref):
    qf = q_ref[...].astype(F32) * SCALE
    qa, qb = _pair_rowsum(qf * qf)
    kn = knorm_ref[...]
    return _stack(jnp.sqrt(qa * kn[:, 0:1]), jnp.sqrt(qb * kn[:, HEAD_DIM:HEAD_DIM + 1]))


def _causal_bias(bias_ref):
    srow, scol = _stacked_iotas(FQ, FK)
    bias_ref[...] = jnp.where(scol <= srow, 0.0, NEG_INF)


def _fox_fwd(proj, kv, c_col, c_row, name):
    S = proj.shape[0]
    nqb = S // FQ

    def body(q_ref, k_ref, v_ref, cc_ref, cr_ref, o_ref, lse_ref, acc_ref, knorm_ref, bias_ref):
        i = pl.program_id(1)

        @pl.when(i == 0)
        def _():
            _key_norm_max(k_ref, knorm_ref, S // BK)
            _causal_bias(bias_ref)

        q2 = _head_rows(q_ref, SCALE)
        bound = _logit_bound(q_ref, knorm_ref)
        cc = cc_ref[0]
        ct = _stack(cc[:, 0:1], cc[:, 1:2])
        acc_ref[...] = jnp.zeros_like(acc_ref)

        def tile(kb, st, on_diagonal):
            m, l = st
            r0 = pl.multiple_of(kb * FK, FK)
            kblk = k_ref[pl.ds(r0, FK), :]
            vblk = v_ref[pl.ds(r0, FK), :]
            z = _dot_nt(q2, kblk) + ct - _key_gates(cr_ref, kb) + bias_ref[...] * on_diagonal
            m_new = jnp.maximum(m, jnp.max(z, axis=1, keepdims=True))
            alpha = jnp.exp(m - m_new)
            pr = jnp.exp(z - m_new)
            acc_ref[...] = alpha * acc_ref[...] + _split_dot(pr, vblk)
            return m_new, alpha * l + jnp.sum(pr, axis=1, keepdims=True)

        def alive(kb, st):
            reach = bound + ct - _last_gate(cr_ref, kb) - st[0]
            return (jnp.max(reach) > UNDERFLOW_BOUND).astype(jnp.int32)

        neg = jnp.full((2 * FQ, 1), NEG_INF, F32)
        zero = jnp.zeros((2 * FQ, 1), F32)
        def cond(s):
            return jnp.logical_and(s[0] >= 0, s[1] > 0)

        def step(s):
            kb, _, st = s
            st = tile(kb, st, (kb == i).astype(F32))
            return kb - 1, alive(kb - 1, st), st

        _, _, (m, l) = lax.while_loop(cond, step, (i, jnp.int32(1), (neg, zero)))
        o_ref[...] = _unstack_heads(acc_ref[...] / l)
        lse = m + jnp.log(l)
        lse_ref[0] = _two_cols(lse[:FQ], lse[FQ:])

    return pl.pallas_call(
        body, name=name, grid=(4, nqb),
        in_specs=[pl.BlockSpec((FQ, LANES), lambda p, i: (i, p)),
                  pl.BlockSpec((S, LANES), lambda p, i: (0, p)),
                  pl.BlockSpec((S, LANES), lambda p, i: (0, GROUP_COLS + p)),
                  pl.BlockSpec((1, FQ, 2), lambda p, i: (p, i, 0)),
                  pl.BlockSpec((1, S // BK, 8, LANES), lambda p, i: (p, 0, 0, 0))],
        out_specs=[pl.BlockSpec((FQ, LANES), lambda p, i: (i, p)),
                   pl.BlockSpec((1, FQ, 2), lambda p, i: (p, i, 0))],
        out_shape=[jax.ShapeDtypeStruct((S, MERGED_WIDTH), F32), jax.ShapeDtypeStruct((4, S, 2), F32)],
        scratch_shapes=[pltpu.VMEM((2 * FQ, LANES), F32), pltpu.VMEM((1, LANES), F32),
                        pltpu.VMEM((2 * FQ, FK), F32)],
        compiler_params=_params(("arbitrary", "arbitrary")),
    )(proj, kv, kv, c_col, c_row)


def _fox_bwd(proj, kv, c_col, c_row, lse, merged, dmerged, dk_prev, dv_prev, dc_prev, name):
    S = proj.shape[0]
    nqb = S // FQ

    def body(q_ref, k_ref, v_ref, cc_ref, cr_ref, lse_ref, o_ref, do_ref, dkp_hbm, dvp_hbm, dcp_ref,
             dq_ref, dk_hbm, dv_hbm, dc_ref, dq_acc, dk_acc, dv_acc, knorm_ref, bias_ref, sem):
        p = pl.program_id(0)
        i = pl.program_id(1)

        @pl.when(i == 0)
        def _():
            ck = pltpu.make_async_copy(dkp_hbm.at[p], dk_acc, sem.at[0])
            cv = pltpu.make_async_copy(dvp_hbm.at[p], dv_acc, sem.at[1])
            ck.start()
            cv.start()
            dc_ref[...] = dcp_ref[...]
            _key_norm_max(k_ref, knorm_ref, S // BK)
            _causal_bias(bias_ref)
            ck.wait()
            cv.wait()

        q2 = _head_rows(q_ref, SCALE)
        do2 = _head_rows(do_ref)
        tot = _stack(*_pair_rowsum(do_ref[...].astype(F32) * o_ref[...]))
        bound = _logit_bound(q_ref, knorm_ref)
        cc = cc_ref[0]
        ct = _stack(cc[:, 0:1], cc[:, 1:2])
        ls = lse_ref[0]
        lse = _stack(ls[:, 0:1], ls[:, 1:2])
        sub = lax.broadcasted_iota(jnp.int32, (8, LANES), 0)
        dq_acc[...] = jnp.zeros_like(dq_acc)

        def tile(kb, masked):
            r0 = pl.multiple_of(kb * FK, FK)
            kblk = k_ref[pl.ds(r0, FK), :]
            vblk = v_ref[pl.ds(r0, FK), :]
            z = _dot_nt(q2, kblk) + ct - _key_gates(cr_ref, kb)
            if masked:
                z = z + bias_ref[...]
            pr = jnp.exp(z - lse)
            ds = pr * (_dot_nt(do2, vblk) - tot)
            dsb = ds.astype(BF16)
            dq_acc[...] += _dot(dsb, kblk)
            dk_acc[pl.ds(r0, FK), :] += _dot_tn(dsb, q2)
            dv_acc[pl.ds(r0, FK), :] += _dot_tn(pr.astype(BF16), do2)
            dca = jnp.sum(ds[:FQ], axis=0, keepdims=True)
            dcb = jnp.sum(ds[FQ:], axis=0, keepdims=True)
            for j in range(GATE_BLOCKS):
                cols = slice(j * BK, (j + 1) * BK)
                old = dc_ref[0, GATE_BLOCKS * kb + j]
                dc_ref[0, GATE_BLOCKS * kb + j] = jnp.where(sub == 0, old - dca[:, cols],
                                                            jnp.where(sub == 1, old - dcb[:, cols], old))

        def alive(kb):
            reach = bound + ct - _last_gate(cr_ref, kb) - lse
            return (jnp.max(reach) > UNDERFLOW_BOUND).astype(jnp.int32)

        tile(i, True)

        def cond(s):
            return jnp.logical_and(s[0] >= 0, s[1] > 0)

        def step(s):
            kb, _ = s
            tile(kb, False)
            return kb - 1, alive(kb - 1)

        lax.while_loop(cond, step, (i - 1, alive(i - 1)))
        dq_ref[...] = (_unstack_heads(dq_acc[...]) * SCALE).astype(dq_ref.dtype)

        @pl.when(i == nqb - 1)
        def _():
            ck = pltpu.make_async_copy(dk_acc, dk_hbm.at[p], sem.at[0])
            cv = pltpu.make_async_copy(dv_acc, dv_hbm.at[p], sem.at[1])
            ck.start()
            cv.start()
            ck.wait()
            cv.wait()

    blk = lambda off: pl.BlockSpec((FQ, LANES), lambda p, i: (i, off + p))
    slab = lambda off: pl.BlockSpec((S, LANES), lambda p, i: (0, off + p))
    cols = pl.BlockSpec((1, FQ, 2), lambda p, i: (p, i, 0))
    rows = pl.BlockSpec((1, S // BK, 8, LANES), lambda p, i: (p, 0, 0, 0))
    return pl.pallas_call(
        body, name=name, grid=(4, nqb),
        in_specs=[blk(0), slab(0), slab(GROUP_COLS), cols, rows, cols, blk(0), blk(0), ANY, ANY, rows],
        out_specs=[blk(0), ANY, ANY, rows],
        out_shape=[jax.ShapeDtypeStruct((S, MIX_WIDTH), BF16),
                   jax.ShapeDtypeStruct((4, S, LANES), F32), jax.ShapeDtypeStruct((4, S, LANES), F32),
                   jax.ShapeDtypeStruct((4, S // BK, 8, LANES), F32)],
        scratch_shapes=[pltpu.VMEM((2 * FQ, LANES), F32), pltpu.VMEM((S, LANES), F32),
                        pltpu.VMEM((S, LANES), F32), pltpu.VMEM((1, LANES), F32),
                        pltpu.VMEM((2 * FQ, FK), F32), pltpu.SemaphoreType.DMA((2,))],
        compiler_params=_params(("arbitrary", "arbitrary")),
    )(proj, kv, kv, c_col, c_row, lse, merged, dmerged, dk_prev, dv_prev, dc_prev)


def _lane_scan(x, reverse):
    lane = lax.broadcasted_iota(jnp.int32, x.shape, 1)
    d = 1
    while d < LANES:
        if reverse:
            x = x + jnp.where(lane < LANES - d, pltpu.roll(x, LANES - d, 1), 0.0)
        else:
            x = x + jnp.where(lane >= d, pltpu.roll(x, d, 1), 0.0)
        d *= 2
    return x


def _gate_fwd(fl3, b8):
    nb = fl3.shape[0]

    def body(fl_ref, b_ref, c_ref):
        def step(kb, carry):
            x = fl_ref[kb] + b_ref[...]
            sp, _ = _softplus_parts(-x)
            c = _lane_scan(-sp, False) + carry
            c_ref[kb] = c
            return c[:, LANES - 1:LANES]

        lax.fori_loop(0, nb, step, jnp.zeros((8, 1), F32))

    return pl.pallas_call(body, name="forget_gate_cumsum",
                          out_shape=jax.ShapeDtypeStruct(fl3.shape, F32),
                          compiler_params=_params())(fl3, b8)


def _gate_bwd(dc3, fl3, b8):
    nb = fl3.shape[0]

    def body(dc_ref, fl_ref, b_ref, dfl_ref, db_ref):
        def step(t, st):
            carry, dbs = st
            kb = nb - 1 - t
            g = _lane_scan(dc_ref[kb], True) + carry
            x = fl_ref[kb] + b_ref[...]
            e = jnp.exp(-jnp.abs(x))
            rcp = 1.0 / (1.0 + e)
            dfl = g * jnp.where(x >= 0.0, e * rcp, rcp)
            dfl_ref[kb] = dfl
            return g[:, 0:1], dbs + dfl

        _, dbs = lax.fori_loop(0, nb, step, (jnp.zeros((8, 1), F32), jnp.zeros((8, LANES), F32)))
        db_ref[...] = jnp.broadcast_to(jnp.sum(dbs, axis=1, keepdims=True), (8, LANES))

    return pl.pallas_call(body, name="forget_gate_bwd",
                          out_shape=[jax.ShapeDtypeStruct(fl3.shape, F32), jax.ShapeDtypeStruct((8, LANES), F32)],
                          compiler_params=_params())(dc3, fl3, b8)


MEM_TQ = 512
MEM_COLS = MEM_WIDTH // LANES


def _mem_fwd(proj, qcol, mkv, mix, name):
    S = proj.shape[0]
    M = mkv.shape[0]

    def body(q_ref, mk_ref, mv_ref, mix_ref, o_ref, lse_ref):
        q2 = _head_rows(q_ref, SCALE)
        s = _dot_nt(q2, mk_ref[...])
        m = jnp.max(s, axis=1, keepdims=True)
        pr = jnp.exp(s - m)
        l = jnp.sum(pr, axis=1, keepdims=True)
        o_ref[...] = _unstack_heads(_dot(pr.astype(BF16), mv_ref[...]) / l)
        lse = m + jnp.log(l)
        lse_ref[0] = _two_cols(lse[:MEM_TQ], lse[MEM_TQ:])

    return pl.pallas_call(
        body, name=name, grid=(MEM_COLS, S // MEM_TQ),
        in_specs=[pl.BlockSpec((MEM_TQ, LANES), lambda p, i: (i, qcol + p)),
                  pl.BlockSpec((M, LANES), lambda p, i: (0, p)),
                  pl.BlockSpec((M, LANES), lambda p, i: (0, MEM_COLS + p)), ANY],
        out_specs=[pl.BlockSpec((MEM_TQ, LANES), lambda p, i: (i, GROUP_COLS + p)),
                   pl.BlockSpec((1, MEM_TQ, 2), lambda p, i: (p, i, 0))],
        out_shape=[jax.ShapeDtypeStruct((S, MERGED_WIDTH), F32), jax.ShapeDtypeStruct((2, S, 2), F32)],
        input_output_aliases={3: 0},
        compiler_params=_params(("parallel", "parallel")),
    )(proj, mkv, mkv, mix)


def _mem_bwd(proj, qcol, mkv, lse, merged, dmerged, name):
    S = proj.shape[0]
    M = mkv.shape[0]

    def body(q_ref, mk_ref, mv_ref, lse_ref, o_ref, do_ref, dq_ref, dmk_ref, dmv_ref):
        @pl.when(pl.program_id(1) == 0)
        def _():
            dmk_ref[...] = jnp.zeros_like(dmk_ref)
            dmv_ref[...] = jnp.zeros_like(dmv_ref)

        q2 = _head_rows(q_ref, SCALE)
        do2 = _head_rows(do_ref)
        tot = _stack(*_pair_rowsum(do_ref[...].astype(F32) * o_ref[...]))
        ls = lse_ref[0]
        pr = jnp.exp(_dot_nt(q2, mk_ref[...]) - _stack(ls[:, 0:1], ls[:, 1:2]))
        ds = pr * (_dot_nt(do2, mv_ref[...]) - tot)
        dsb = ds.astype(BF16)
        dmk_ref[...] += _dot_tn(dsb, q2)
        dmv_ref[...] += _dot_tn(pr.astype(BF16), do2)
        dq_ref[...] = (_unstack_heads(_dot(dsb, mk_ref[...])) * SCALE).astype(dq_ref.dtype)

    blk = lambda off: pl.BlockSpec((MEM_TQ, LANES), lambda p, i: (i, off + p))
    acc = pl.BlockSpec((M, LANES), lambda p, i: (0, p))
    return pl.pallas_call(
        body, name=name, grid=(MEM_COLS, S // MEM_TQ),
        in_specs=[blk(qcol), pl.BlockSpec((M, LANES), lambda p, i: (0, p)),
                  pl.BlockSpec((M, LANES), lambda p, i: (0, MEM_COLS + p)),
                  pl.BlockSpec((1, MEM_TQ, 2), lambda p, i: (p, i, 0)), blk(GROUP_COLS), blk(GROUP_COLS)],
        out_specs=[blk(0), acc, acc],
        out_shape=[jax.ShapeDtypeStruct((S, MEM_WIDTH), BF16), jax.ShapeDtypeStruct((M, MEM_WIDTH), F32),
                   jax.ShapeDtypeStruct((M, MEM_WIDTH), F32)],
        compiler_params=_params(("parallel", "arbitrary")),
    )(proj, mkv, mkv, lse, merged, dmerged)


def _c_layouts(c3):
    nb = c3.shape[0]
    pairs = c3.reshape(nb, 4, 2, LANES).transpose(1, 0, 2, 3)
    c_row = jnp.pad(pairs, ((0, 0), (0, 0), (0, 6), (0, 0)))
    c_col = pairs.transpose(0, 1, 3, 2).reshape(4, nb * LANES, 2)
    return c_col, c_row


def _local_step(x, mem, wb, shards, sm, loss_target):
    S = x.shape[0]
    nb = S // BK
    vec = lambda a: a.reshape(1, D_MODEL)
    b8 = jnp.broadcast_to(sm["b_f"].reshape(8, 1), (8, LANES))

    saved = []
    shared = None
    h = x
    hn = _rms_fwd(h, vec(sm["norm1_g"][0]), "norm1_0")
    for l in range(DEPTH):
        if l == N_A:
            w_kvf = jnp.pad(wb["w_kv_shared"], ((0, 0), (0, KVF_WIDTH - W_KV_SHARED)))
            hs = _rms_fwd(h, vec(sm["kv_norm_g"]), "kv_norm")
            kvf = _mm(hs, w_kvf, out_dtype=F32, name="kv_shared_proj")
            kv = kvf[:, :2 * MIX_WIDTH].astype(BF16)
            fl3 = kvf[:, 2 * MIX_WIDTH:2 * MIX_WIDTH + 8].T.reshape(8, nb, LANES).transpose(1, 0, 2)
            c3 = _gate_fwd(fl3, b8)
            c_col, c_row = _c_layouts(c3)
            shared = dict(h=h, hs=hs, kv=kv, fl3=fl3, c_col=c_col, c_row=c_row)
        mn = _rms_fwd(mem, vec(sm["mem_norm_g"][l]), f"mem_norm_{l}")
        mkv = _mm(mn, wb["w_mem_kv"][l], name=f"mem_kv_proj_{l}")
        if l < N_A:
            w_in = wb["w_in_a"][l]
            proj = _mm(hn, w_in, name=f"in_proj_{l}")
            mix, gathered = _sb_fwd(proj, f"stickbreak_fwd_{l}", _AllGather, shards[l])
            _unpack_gathered(PARTS[1 + l], gathered, wb)
            lse, qcol = None, Q_MEM_COL_A
        else:
            w_in = wb["w_in_b"][l - N_A]
            proj = _mm(hn, w_in, name=f"in_proj_{l}")
            mix, lse = _fox_fwd(proj, shared["kv"], shared["c_col"], shared["c_row"], f"fox_fwd_{l}")
            qcol = Q_MEM_COL_B
        merged, mlse = _mem_fwd(proj, qcol, mkv, mix, f"mem_attn_fwd_{l}")
        h_mid, hn2 = _mm(merged, wb["w_o"][l], epi="add_rms", extra=(h, vec(sm["norm2_g"][l])),
                         name=f"out_proj_{l}")
        u, act = _mm(hn2, wb["w_mlp1"][l], epi="relu2", name=f"mlp1_{l}")
        saved.append(dict(h=h, hn=hn, mn=mn, mkv=mkv, proj=proj, lse=lse, mlse=mlse, qcol=qcol, merged=merged,
                          h_mid=h_mid, hn2=hn2, u=u, act=act, w_in=w_in))
        if l + 1 < DEPTH:
            h, hn = _mm(act, wb["w_mlp2"][l], epi="add_rms", extra=(h_mid, vec(sm["norm1_g"][l + 1])),
                        name=f"mlp2_{l}")
        else:
            h = _mm(act, wb["w_mlp2"][l], out_dtype=F32, epi="add", extra=h_mid, name=f"mlp2_{l}")

    loss, dh, dhb, dg_final = _final_loss(h, vec(sm["final_norm_g"]), loss_target)

    gb = {n: [None] * (DEPTH if n not in ("w_in_a", "w_in_b") else 2) for n in
          ("w_in_a", "w_in_b", "w_mem_kv", "w_o", "w_mlp1", "w_mlp2")}
    gs = {n: [None] * DEPTH for n in ("norm1_g", "mem_norm_g", "norm2_g")}
    received = [None] * N_A
    dk_sh = jnp.zeros((4, S, LANES), F32)
    dv_sh = jnp.zeros((4, S, LANES), F32)
    dc_sh = jnp.zeros((4, nb, 8, LANES), F32)
    for l in reversed(range(DEPTH)):
        sv = saved[l]
        du = _mm(dhb, wb["w_mlp2"][l], mode="nt", epi="drelu2", extra=sv["u"], name=f"mlp2_dx_{l}")
        gb["w_mlp2"][l] = _mm(sv["act"], dhb, mode="tn", out_dtype=F32, name=f"mlp2_dw_{l}")
        gb["w_mlp1"][l] = _mm(sv["hn2"], du, mode="tn", out_dtype=F32, name=f"mlp1_dw_{l}")
        dh, dhb, gs["norm2_g"][l] = _mm(du, wb["w_mlp1"][l], mode="nt", epi="rms_bwd",
                                        extra=(sv["h_mid"], vec(sm["norm2_g"][l]), dh),
                                        name=f"mlp1_dx_norm2_bwd_{l}")
        dmerged = _mm(dhb, wb["w_o"][l], mode="nt", name=f"out_proj_dx_{l}")
        gb["w_o"][l] = _mm(sv["merged"], dhb, mode="tn", out_dtype=F32, name=f"out_proj_dw_{l}")
        if l < N_A:
            ready = _pack_grads(PARTS[1 + l], PART_ROWS[1 + l], gb, None)
            dq, dk, dv, received[l] = _sb_bwd(sv["proj"], sv["merged"], dmerged, f"stickbreak_bwd_{l}",
                                              _Scatter, ready)
        else:
            dq, dk_sh, dv_sh, dc_sh = _fox_bwd(sv["proj"], shared["kv"], shared["c_col"], shared["c_row"],
                                               sv["lse"], sv["merged"], dmerged, dk_sh, dv_sh, dc_sh,
                                               f"fox_bwd_{l}")
        dqm, dmk, dmv = _mem_bwd(sv["proj"], sv["qcol"], sv["mkv"], sv["mlse"], sv["merged"], dmerged,
                                 f"mem_attn_bwd_{l}")
        if l < N_A:
            flat = lambda t: t.transpose(1, 0, 2).reshape(S, MIX_WIDTH).astype(BF16)
            dproj = jnp.concatenate([dq, flat(dk), flat(dv), dqm], axis=1)
        else:
            dproj = jnp.concatenate([dq, dqm], axis=1)
        name_in = "w_in_a" if l < N_A else "w_in_b"
        gb[name_in][l if l < N_A else l - N_A] = _mm(sv["hn"], dproj, mode="tn", out_dtype=F32,
                                                      name=f"in_proj_dw_{l}")
        dh, dhb, gs["norm1_g"][l] = _mm(dproj, sv["w_in"], mode="nt", epi="rms_bwd",
                                        extra=(sv["h"], vec(sm["norm1_g"][l]), dh),
                                        name=f"in_proj_dx_norm1_bwd_{l}")
        dmkv = jnp.concatenate([dmk, dmv], axis=1)
        gb["w_mem_kv"][l] = _mm(sv["mn"], dmkv, mode="tn", out_dtype=F32, name=f"mem_kv_dw_{l}")
        dmn = _mm(dmkv, wb["w_mem_kv"][l], mode="nt", out_dtype=F32, name=f"mem_kv_dx_{l}")
        gs["mem_norm_g"][l] = _rms_gain_grad(mem, dmn, f"mem_norm_bwd_{l}")
        if l == N_A:
            dfl3, db8 = _gate_bwd(dc_sh.reshape(4, nb, 8, LANES)[:, :, :2].transpose(1, 0, 2, 3).reshape(nb, 8, LANES),
                                  shared["fl3"], b8)
            dfl = dfl3.transpose(1, 0, 2).reshape(8, S).T
            flat = lambda t: t.transpose(1, 0, 2).reshape(S, MIX_WIDTH).astype(BF16)
            dkvf = jnp.concatenate([flat(dk_sh), flat(dv_sh),
                                    jnp.pad(dfl, ((0, 0), (0, LANES - 8))).astype(BF16)], axis=1)
            gb["w_kv_shared"] = _mm(shared["hs"], dkvf, mode="tn", out_dtype=F32, name="kv_shared_dw")[:, :W_KV_SHARED]
            dh, dhb, g_kvn = _mm(dkvf, w_kvf, mode="nt", epi="rms_bwd",
                                 extra=(shared["h"], vec(sm["kv_norm_g"]), dh), name="kv_shared_dx_norm_bwd")
            g_bf = db8[:, 0]

    gsmall = {n: jnp.concatenate(v, axis=0) for n, v in gs.items()}
    gsmall["kv_norm_g"] = g_kvn
    gsmall["final_norm_g"] = dg_final
    gsmall["b_f"] = g_bf
    return loss, dh, gb, gsmall, received


def kernel(x, mem, norm1_g, w_in_a, w_in_b, w_mem_kv, mem_norm_g, w_o, norm2_g, w_mlp1, w_mlp2, kv_norm_g, w_kv_shared, b_f, final_norm_g, loss_target, m_norm1_g, m_w_in_a, m_w_in_b, m_w_mem_kv, m_mem_norm_g, m_w_o, m_norm2_g, m_w_mlp1, m_w_mlp2, m_kv_norm_g, m_w_kv_shared, m_b_f, m_final_norm_g, v_norm1_g, v_w_in_a, v_w_in_b, v_w_mem_kv, v_mem_norm_g, v_w_o, v_norm2_g, v_w_mlp1, v_w_mlp2, v_kv_norm_g, v_w_kv_shared, v_b_f, v_final_norm_g):
    big_w = dict(w_in_a=w_in_a, w_in_b=w_in_b, w_mem_kv=w_mem_kv, w_o=w_o, w_mlp1=w_mlp1, w_mlp2=w_mlp2,
                 w_kv_shared=w_kv_shared)
    small_w = dict(norm1_g=norm1_g, mem_norm_g=mem_norm_g, norm2_g=norm2_g, kv_norm_g=kv_norm_g,
                   final_norm_g=final_norm_g, b_f=b_f)
    big_m = dict(w_in_a=m_w_in_a, w_in_b=m_w_in_b, w_mem_kv=m_w_mem_kv, w_o=m_w_o, w_mlp1=m_w_mlp1,
                 w_mlp2=m_w_mlp2, w_kv_shared=m_w_kv_shared)
    small_m = dict(norm1_g=m_norm1_g, mem_norm_g=m_mem_norm_g, norm2_g=m_norm2_g, kv_norm_g=m_kv_norm_g,
                   final_norm_g=m_final_norm_g, b_f=m_b_f)
    big_v = dict(w_in_a=v_w_in_a, w_in_b=v_w_in_b, w_mem_kv=v_w_mem_kv, w_o=v_w_o, w_mlp1=v_w_mlp1,
                 w_mlp2=v_w_mlp2, w_kv_shared=v_w_kv_shared)
    small_v = dict(norm1_g=v_norm1_g, mem_norm_g=v_mem_norm_g, norm2_g=v_norm2_g, kv_norm_g=v_kv_norm_g,
                   final_norm_g=v_final_norm_g, b_f=v_b_f)

    def pack(k, big, small, dtype):
        return _pack_local(PARTS[k], PART_ROWS[k], big, small if k == 0 else None, dtype)

    def pack_all(big, small):
        return jnp.concatenate([pack(k, big, small, F32) for k in range(len(PARTS))], axis=0)

    wb = {n: {} for n in BIG_NAMES}
    _unpack_gathered(PARTS[0], _allgather_chips(pack(0, big_w, small_w, BF16)), wb)
    shards = [pack(1 + l, big_w, None, BF16) for l in range(N_A)]

    loss, dx, gb, gsmall, received = _local_step(x[0], mem[0], wb, shards, small_w, loss_target[0])

    received = [_scatter_chips(_pack_grads(PARTS[0], PART_ROWS[0], gb, gsmall))] + received
    part = jnp.concatenate([_sum4(r) for r in received], axis=0)
    other = _swap_cores(part)
    g, delta, new_m, new_v = _adamw(part, other, pack_all(big_w, small_w), pack_all(big_m, small_m),
                                    pack_all(big_v, small_v))

    outs = [lax.psum(loss[0, 0], ("x", "y", "c")), dx[None]]
    for packed in (g, delta, new_m, new_v):
        pieces, d, off = {n: [] for n in BIG_NAMES}, {}, 0
        for k, part_k in enumerate(PARTS):
            _unpack_local(part_k, packed[off:off + PART_ROWS[k]], k == 0, pieces, d)
            off += PART_ROWS[k]
        d.update(_join_layers(pieces))
        outs.extend(d[n] for n in WEIGHT_ORDER)
    return tuple(outs)
```

```python
import functools
import math

import jax
import jax.numpy as jnp
from jax import lax
from jax.experimental import pallas as pl
from jax.experimental.pallas import tpu as pltpu

F32 = jnp.float32
BF16 = jnp.bfloat16

D_MODEL = 1024
HEAD_DIM = 64
MIX_WIDTH = 512
MEM_WIDTH = 256
MERGED_WIDTH = MIX_WIDTH + MEM_WIDTH
DEPTH = 4
N_A = 2
D_FF = 4096
EPS = 1e-6
NEG_INF = -1e30
SCALE = 1.0 / math.sqrt(HEAD_DIM)

ADAM_LR = 0.001
ADAM_B1 = 0.9
ADAM_B2 = 0.999
ADAM_EPS = 1e-08
ADAM_WD = 0.01
ADAM_STEP = 10

LANES = 128
GROUP_COLS = MIX_WIDTH // LANES
Q_MEM_COL_A = 3 * GROUP_COLS
Q_MEM_COL_B = GROUP_COLS
W_KV_SHARED = 2 * MIX_WIDTH + 8
KVF_WIDTH = 1152
BQ = 256
BK = 128
DIAG_TILES = BQ // BK
CHAINS = 2
UNDERFLOW_BOUND = -110.0
VMEM_LIMIT = 56 * 1024 * 1024

MESH = pl.DeviceIdType.MESH
N_CHIPS = 4

PARTS = (
    (("w_in_a", 0, 1, (1024, 448), 1),
     ("w_mem_kv", 0, 1, (256, 512), 0)),
    (("w_o", 0, 1, (768, 256), 1),
     ("w_mlp1", 0, 1, (1024, 1024), 1),
     ("w_mlp2", 0, 1, (1024, 1024), 0),
     ("w_in_a", 1, 2, (1024, 448), 1),
     ("w_mem_kv", 1, 2, (256, 512), 0)),
    (("w_o", 1, 4, (768, 256), 1),
     ("w_mlp1", 1, 4, (1024, 1024), 1),
     ("w_mlp2", 1, 4, (1024, 1024), 0),
     ("w_in_b", 0, 2, (256, 768), 0),
     ("w_mem_kv", 2, 4, (256, 512), 0),
     ("w_kv_shared", None, None, (1024, 258), 1)),
)
BIG_NAMES = ("w_in_a", "w_in_b", "w_mem_kv", "w_o", "w_mlp1", "w_mlp2", "w_kv_shared")
SMALL = (
    ("norm1_g", (4, 1024)),
    ("mem_norm_g", (4, 1024)),
    ("norm2_g", (4, 1024)),
    ("kv_norm_g", (1, 1024)),
    ("final_norm_g", (1, 1024)),
    ("b_f", (1, 1024)),
)
WEIGHT_ORDER = ("norm1_g", "w_in_a", "w_in_b", "w_mem_kv", "mem_norm_g", "w_o", "norm2_g", "w_mlp1",
                "w_mlp2", "kv_norm_g", "w_kv_shared", "b_f", "final_norm_g")


ROW_ALIGN = 16
PACK_TILE = 256
SMALL_ROWS = ROW_ALIGN
assert sum(s[0] for _, s in SMALL) <= SMALL_ROWS


def _section_rows(entry):
    _, lo, hi, shape, _ = entry
    rows = (1 if lo is None else hi - lo) * math.prod(shape) // D_MODEL
    return rows, -(-rows // ROW_ALIGN) * ROW_ALIGN


def _round_up(n, m):
    return -(-n // m) * m


SUM_TILE = 128
_used = [sum(_section_rows(e)[1] for e in part) for part in PARTS]
PART_ROWS = [_round_up(_used[0] + SMALL_ROWS, SUM_TILE), _round_up(_used[1], SUM_TILE)]
PART_ROWS.append(_round_up(_used[2] + sum(PART_ROWS), PACK_TILE) - sum(PART_ROWS))
assert PART_ROWS[2] % SUM_TILE == 0


def _params(sem=None):
    return pltpu.CompilerParams(dimension_semantics=sem, vmem_limit_bytes=VMEM_LIMIT)


def _pick(n, cands):
    for c in cands:
        if n % c == 0:
            return c
    raise ValueError(f"no tile for {n}")


def _section(a, entry):
    a = a.reshape(-1, D_MODEL)
    return jnp.pad(a, ((0, _section_rows(entry)[1] - a.shape[0]), (0, 0)))


def _small_block(small, dtype):
    blk = jnp.zeros((SMALL_ROWS, D_MODEL), dtype)
    off = 0
    for n, shp in SMALL:
        a = small[n].astype(dtype)
        if n == "b_f":
            blk = blk.at[off, :a.size].set(a.reshape(-1))
        else:
            blk = blk.at[off:off + shp[0]].set(a.reshape(shp))
        off += shp[0]
    return blk


def _fill(parts, rows, dtype):
    used = sum(p.shape[0] for p in parts)
    return jnp.concatenate(parts + [jnp.zeros((rows - used, D_MODEL), dtype)], axis=0)


def _pack_local(part, rows, big, small, dtype):
    parts = [_section((big[e[0]] if e[1] is None else big[e[0]][e[1]:e[2]]).astype(dtype), e) for e in part]
    if small is not None:
        parts.append(_small_block(small, dtype))
    return _fill(parts, rows, dtype)


def _unpack_local(part, p, with_small, pieces, small):
    off = 0
    for e in part:
        n, lo, hi, shp, _ = e
        rows, reserved = _section_rows(e)
        pieces[n].append((lo, p[off:off + rows].reshape(shp if lo is None else (hi - lo,) + shp)))
        off += reserved
    if with_small:
        for n, shp in SMALL:
            a = p[off:off + shp[0]]
            small[n] = a[0, :8] if n == "b_f" else (a.reshape(D_MODEL) if shp[0] == 1 else a)
            off += shp[0]


def _join_layers(pieces):
    out = {}
    for n, ps in pieces.items():
        ps = sorted(ps, key=lambda t: -1 if t[0] is None else t[0])
        out[n] = ps[0][1] if len(ps) == 1 else jnp.concatenate([a for _, a in ps], axis=0)
    return out


def _unpack_gathered(part, g, weights):
    off = 0
    for e in part:
        n, lo, hi, shp, ax = e
        rows, reserved = _section_rows(e)
        if lo is None:
            sec = g[:, off:off + rows].reshape((N_CHIPS,) + shp)
            weights[n] = jnp.concatenate([sec[j] for j in range(N_CHIPS)], axis=ax)
        else:
            sec = g[:, off:off + rows].reshape((N_CHIPS, hi - lo) + shp)
            for l in range(lo, hi):
                weights[n][l] = jnp.concatenate([sec[j, l - lo] for j in range(N_CHIPS)], axis=ax)
        off += reserved


def _pack_grads(part, rows, gbig, gsmall):
    small = None if gsmall is None else _small_block(gsmall, BF16)
    chunks = []
    for j in range(N_CHIPS):
        parts = []
        for e in part:
            n, lo, hi, shp, ax = e
            w = shp[ax]
            layers = [gbig[n]] if lo is None else [gbig[n][l] for l in range(lo, hi)]
            cut = [lax.slice_in_dim(g, j * w, (j + 1) * w, axis=ax).astype(BF16).reshape(-1, D_MODEL) for g in layers]
            parts.append(_section(cut[0] if len(cut) == 1 else jnp.concatenate(cut, axis=0), e))
        if small is not None:
            parts.append(small)
        chunks.append(_fill(parts, rows, BF16))
    return jnp.stack(chunks, axis=0)


ANY = pl.BlockSpec(memory_space=pl.ANY)


def _other_chips(x, y):
    return [(1 - x, y), (x, 1 - y), (1 - x, 1 - y)]


class _AllGather:
    SCRATCH = [pltpu.SemaphoreType.DMA((3,)), pltpu.SemaphoreType.DMA((3,)), pltpu.SemaphoreType.DMA((3,)),
               pltpu.SemaphoreType.DMA((3,)), pltpu.SemaphoreType.DMA]

    def __init__(self, w_ref, o_ref, send_sems, recv_sems, pass_send, pass_recv, local_sem):
        self.w_ref, self.o_ref = w_ref, o_ref
        self.sems = (send_sems, recv_sems, pass_send, pass_recv, local_sem)
        x, y, c = lax.axis_index("x"), lax.axis_index("y"), lax.axis_index("c")
        half = w_ref.shape[0] // 2
        self.c, self.me, self.sibling = c, 2 * x + y, (x, y, 1 - c)
        self.mine = pl.ds(pl.multiple_of(c * half, ROW_ALIGN), half)
        self.other = pl.ds(pl.multiple_of((1 - c) * half, ROW_ALIGN), half)
        self.chips = _other_chips(x, y)

    def _over_ici(self, j, rows_of):
        chip = self.chips[j]
        return pltpu.make_async_remote_copy(
            src_ref=self.w_ref.at[self.mine], dst_ref=self.o_ref.at[rows_of, self.mine],
            send_sem=self.sems[0].at[j], recv_sem=self.sems[1].at[j],
            device_id=(chip[0], chip[1], self.c), device_id_type=MESH)

    def _over_d2d(self, j, rows):
        where = self.o_ref.at[2 * self.chips[j][0] + self.chips[j][1], rows]
        return pltpu.make_async_remote_copy(src_ref=where, dst_ref=where, send_sem=self.sems[2].at[j],
                                            recv_sem=self.sems[3].at[j], device_id=self.sibling,
                                            device_id_type=MESH)

    def _local(self):
        return pltpu.make_async_copy(self.w_ref, self.o_ref.at[self.me], self.sems[4])

    def start(self):
        self._local().start()
        for j in range(3):
            self._over_ici(j, self.me).start()

    def finish(self):
        for j in range(3):
            self._over_ici(j, 2 * self.chips[j][0] + self.chips[j][1]).wait_recv()
            self._over_d2d(j, self.mine).start()
        for j in range(3):
            self._over_d2d(j, self.other).wait_recv()
        for j in range(3):
            self._over_ici(j, self.me).wait_send()
            self._over_d2d(j, self.mine).wait_send()
        self._local().wait()


class _Scatter:
    SCRATCH = [pltpu.SemaphoreType.DMA((3,)), pltpu.SemaphoreType.DMA((3,)), pltpu.SemaphoreType.DMA]

    def __init__(self, g_ref, o_ref, send_sems, recv_sems, local_sem):
        self.g_ref, self.o_ref, self.sems = g_ref, o_ref, (send_sems, recv_sems, local_sem)
        x, y, c = lax.axis_index("x"), lax.axis_index("y"), lax.axis_index("c")
        self.c, self.me, self.chips = c, 2 * x + y, _other_chips(x, y)

    def _copy(self, j):
        chip = self.chips[j]
        return pltpu.make_async_remote_copy(
            src_ref=self.g_ref.at[2 * chip[0] + chip[1]], dst_ref=self.o_ref.at[self.me],
            send_sem=self.sems[0].at[j], recv_sem=self.sems[1].at[j],
            device_id=(chip[0], chip[1], self.c), device_id_type=MESH)

    def _local(self):
        return pltpu.make_async_copy(self.g_ref.at[self.me], self.o_ref.at[self.me], self.sems[2])

    def start(self):
        self._local().start()
        for j in range(3):
            self._copy(j).start()

    def finish(self):
        for j in range(3):
            self._copy(j).wait()
        self._local().wait()


def _allgather_chips(w):
    def body(w_ref, o_ref, *sems):
        ag = _AllGather(w_ref, o_ref, *sems)
        ag.start()
        ag.finish()

    return pl.pallas_call(
        body, name="allgather_weights",
        out_shape=jax.ShapeDtypeStruct((N_CHIPS,) + w.shape, w.dtype),
        in_specs=[ANY], out_specs=ANY, scratch_shapes=_AllGather.SCRATCH,
    )(w)


def _scatter_chips(g4):
    def body(g_ref, o_ref, *sems):
        sc = _Scatter(g_ref, o_ref, *sems)
        sc.start()
        sc.finish()

    return pl.pallas_call(
        body, name="scatter_grads",
        out_shape=jax.ShapeDtypeStruct(g4.shape, g4.dtype),
        in_specs=[ANY], out_specs=ANY, scratch_shapes=_Scatter.SCRATCH,
    )(g4)


def _swap_cores(p):
    def body(p_ref, o_ref, send_sem, recv_sem):
        x, y, c = lax.axis_index("x"), lax.axis_index("y"), lax.axis_index("c")
        cp = pltpu.make_async_remote_copy(src_ref=p_ref, dst_ref=o_ref, send_sem=send_sem, recv_sem=recv_sem,
                                          device_id=(x, y, 1 - c), device_id_type=MESH)
        cp.start()
        cp.wait()

    return pl.pallas_call(
        body, name="swap_cores",
        out_shape=jax.ShapeDtypeStruct(p.shape, p.dtype),
        in_specs=[ANY], out_specs=ANY,
        scratch_shapes=[pltpu.SemaphoreType.DMA, pltpu.SemaphoreType.DMA],
    )(p)


def _sum4(r4):
    _, R, C = r4.shape

    def body(r_ref, o_ref):
        o_ref[...] = ((r_ref[0].astype(F32) + r_ref[1].astype(F32)) + r_ref[2].astype(F32)) + r_ref[3].astype(F32)

    return pl.pallas_call(
        body, name="sum_chips", grid=(R // SUM_TILE,),
        in_specs=[pl.BlockSpec((N_CHIPS, SUM_TILE, C), lambda i: (0, i, 0))],
        out_specs=pl.BlockSpec((SUM_TILE, C), lambda i: (i, 0)),
        out_shape=jax.ShapeDtypeStruct((R, C), F32),
        compiler_params=_params(("parallel",)),
    )(r4)


def _adamw(pa, pb, w, m, v):
    R, C = w.shape
    c1 = 1.0 - ADAM_B1
    c2 = 1.0 - ADAM_B2
    bc1 = 1.0 - ADAM_B1 ** ADAM_STEP
    bc2 = 1.0 - ADAM_B2 ** ADAM_STEP

    def body(pa_ref, pb_ref, w_ref, m_ref, v_ref, g_ref, d_ref, mo_ref, vo_ref):
        g = pa_ref[...] + pb_ref[...]
        mn = ADAM_B1 * m_ref[...] + c1 * g
        vn = ADAM_B2 * v_ref[...] + c2 * (g * g)
        m_hat = mn / bc1
        v_hat = vn / bc2
        g_ref[...] = g
        d_ref[...] = -ADAM_LR * (m_hat / (jnp.sqrt(v_hat) + ADAM_EPS) + ADAM_WD * w_ref[...])
        mo_ref[...] = mn
        vo_ref[...] = vn

    spec = pl.BlockSpec((PACK_TILE, C), lambda i: (i, 0))
    shp = jax.ShapeDtypeStruct((R, C), F32)
    return pl.pallas_call(
        body, name="adamw", grid=(R // PACK_TILE,),
        in_specs=[spec] * 5, out_specs=[spec] * 4, out_shape=[shp] * 4,
        compiler_params=_params(("parallel",)),
    )(pa, pb, w, m, v)


def _rms_fwd(x, g, name):
    R, Dm = x.shape
    tr = _pick(R, (512, 256, 128))

    def body(x_ref, g_ref, o_ref):
        xf = x_ref[...]
        r = lax.rsqrt(jnp.mean(xf * xf, axis=-1, keepdims=True) + EPS)
        o_ref[...] = (xf * r * g_ref[...]).astype(o_ref.dtype)

    return pl.pallas_call(
        body, name=name, grid=(R // tr,),
        in_specs=[pl.BlockSpec((tr, Dm), lambda i: (i, 0)), pl.BlockSpec((1, Dm), lambda i: (0, 0))],
        out_specs=pl.BlockSpec((tr, Dm), lambda i: (i, 0)),
        out_shape=jax.ShapeDtypeStruct((R, Dm), BF16),
        compiler_params=_params(("parallel",)),
    )(x, g)


def _rms_gain_grad(x, dy, name):
    R, Dm = x.shape
    tr = _pick(R, (256, 128))

    def body(x_ref, dy_ref, dg_ref):
        xf = x_ref[...]
        r = lax.rsqrt(jnp.mean(xf * xf, axis=-1, keepdims=True) + EPS)

        @pl.when(pl.program_id(0) == 0)
        def _():
            dg_ref[...] = jnp.zeros_like(dg_ref)

        dg_ref[...] += jnp.sum(dy_ref[...] * (xf * r), axis=0, keepdims=True)

    row = pl.BlockSpec((tr, Dm), lambda i: (i, 0))
    return pl.pallas_call(
        body, name=name, grid=(R // tr,),
        in_specs=[row, row], out_specs=pl.BlockSpec((1, Dm), lambda i: (0, 0)),
        out_shape=jax.ShapeDtypeStruct((1, Dm), F32),
        compiler_params=_params(("arbitrary",)),
    )(x, dy)


def _final_loss(x, g, tgt):
    R, Dm = x.shape
    tr = _pick(R, (256, 128))

    def body(x_ref, g_ref, t_ref, l_ref, dx_ref, dxb_ref, dg_ref):
        xf = x_ref[...]
        gv = g_ref[...]
        r = lax.rsqrt(jnp.mean(xf * xf, axis=-1, keepdims=True) + EPS)
        xr = xf * r
        err = xr * gv - t_ref[...]
        dy_ = err * (1.0 / Dm)
        gdy = dy_ * gv
        mdot = jnp.mean(xf * gdy, axis=-1, keepdims=True)
        dx = r * gdy - xf * ((r * r * r) * mdot)
        dx_ref[...] = dx
        dxb_ref[...] = dx.astype(BF16)

        @pl.when(pl.program_id(0) == 0)
        def _():
            dg_ref[...] = jnp.zeros_like(dg_ref)
            l_ref[...] = jnp.zeros_like(l_ref)

        dg_ref[...] += jnp.sum(dy_ * xr, axis=0, keepdims=True)
        sq = jnp.sum(err * err, axis=1, keepdims=True)
        l_ref[...] += jnp.sum(sq, axis=0, keepdims=True) * (0.5 / Dm)

    row = pl.BlockSpec((tr, Dm), lambda i: (i, 0))
    vec = pl.BlockSpec((1, Dm), lambda i: (0, 0))
    return pl.pallas_call(
        body, name="final_norm_loss", grid=(R // tr,),
        in_specs=[row, vec, row],
        out_specs=[pl.BlockSpec((1, 1), lambda i: (0, 0)), row, row, vec],
        out_shape=[jax.ShapeDtypeStruct((1, 1), F32), jax.ShapeDtypeStruct((R, Dm), F32),
                   jax.ShapeDtypeStruct((R, Dm), BF16), jax.ShapeDtypeStruct((1, Dm), F32)],
        compiler_params=_params(("arbitrary",)),
    )(x, g, tgt)


MAX_TK = 2048

_DIMS = {"nn": (((1,), (0,)), ((), ())), "nt": (((1,), (1,)), ((), ())), "tn": (((0,), (0,)), ((), ()))}


def _mm(a, b, *, mode="nn", out_dtype=BF16, epi=None, extra=None, name):
    if mode == "nn":
        (M, K), N = a.shape, b.shape[1]
    elif mode == "nt":
        (M, K), N = a.shape, b.shape[0]
    else:
        (K, M), N = a.shape, b.shape[1]
    deep = K > MAX_TK
    tm = _pick(M, (512, 256, 128) if (epi == "rms_bwd" or deep) else (1024, 768, 512, 256, 128))
    tn = _pick(N, (1024, 896, 768, 640, 512, 384, 256, 128))
    tk = K if K <= (2 * MAX_TK if deep else MAX_TK) else _pick(K, (2 * MAX_TK, MAX_TK, 1024, 512, 256, 128))
    nk = K // tk
    extras = () if extra is None else (extra if isinstance(extra, tuple) else (extra,))
    n_out = {"relu2": 2, "add_rms": 2, "rms_bwd": 3}.get(epi, 1)
    assert epi not in ("rms_bwd", "add_rms") or tn == N

    def body(*refs):
        a_ref, b_ref = refs[:2]
        e_refs = refs[2:2 + len(extras)]
        e_ref = e_refs[0] if e_refs else None
        outs = refs[2 + len(extras):2 + len(extras) + n_out]
        k = pl.program_id(2)
        part = lax.dot_general(a_ref[...].astype(BF16), b_ref[...].astype(BF16), _DIMS[mode],
                               preferred_element_type=F32)

        def finish(acc):
            if epi is None:
                outs[0][...] = acc.astype(outs[0].dtype)
            elif epi == "add":
                outs[0][...] = (e_ref[...] + acc).astype(outs[0].dtype)
            elif epi == "add_rms":
                y = e_refs[0][...] + acc
                outs[0][...] = y
                r = lax.rsqrt(jnp.mean(y * y, axis=-1, keepdims=True) + EPS)
                outs[1][...] = (y * r * e_refs[1][...]).astype(BF16)
            elif epi == "relu2":
                outs[0][...] = acc.astype(BF16)
                rl = jnp.maximum(acc, 0.0)
                outs[1][...] = (rl * rl).astype(BF16)
            elif epi == "drelu2":
                u = e_ref[...].astype(F32)
                outs[0][...] = (acc * (2.0 * jnp.maximum(u, 0.0))).astype(outs[0].dtype)
            elif epi == "rms_bwd":
                x_ref, g_ref, dres_ref = e_refs
                xf = x_ref[...]
                r = lax.rsqrt(jnp.mean(xf * xf, axis=-1, keepdims=True) + EPS)
                gdy = acc * g_ref[...]
                mdot = jnp.mean(xf * gdy, axis=-1, keepdims=True)
                dx = dres_ref[...] + (r * gdy - xf * ((r * r * r) * mdot))
                outs[0][...] = dx
                outs[1][...] = dx.astype(BF16)

                @pl.when(pl.program_id(0) == 0)
                def _():
                    outs[2][...] = jnp.zeros_like(outs[2])

                outs[2][...] += jnp.sum(acc * (xf * r), axis=0, keepdims=True)

        if nk == 1:
            finish(part)
        else:
            acc_ref = refs[-1]

            @pl.when(k == 0)
            def _():
                acc_ref[...] = part

            @pl.when(jnp.logical_and(k > 0, k < nk - 1))
            def _():
                acc_ref[...] += part

            @pl.when(k == nk - 1)
            def _():
                finish(acc_ref[...] + part)

    if mode == "tn":
        a_spec = pl.BlockSpec((tk, tm), lambda i, j, k: (k, i))
    else:
        a_spec = pl.BlockSpec((tm, tk), lambda i, j, k: (i, k))
    if mode == "nt":
        b_spec = pl.BlockSpec((tn, tk), lambda i, j, k: (j, k))
    else:
        b_spec = pl.BlockSpec((tk, tn), lambda i, j, k: (k, j))
    o_spec = pl.BlockSpec((tm, tn), lambda i, j, k: (i, j))
    vec_spec = pl.BlockSpec((1, tn), lambda i, j, k: (0, j))
    ins, in_specs = [a, b] + list(extras), [a_spec, b_spec]
    if epi == "rms_bwd":
        in_specs += [o_spec, vec_spec, o_spec]
        out_shape = [jax.ShapeDtypeStruct((M, N), F32), jax.ShapeDtypeStruct((M, N), BF16),
                     jax.ShapeDtypeStruct((1, N), F32)]
        out_specs = [o_spec, o_spec, vec_spec]
    elif epi == "add_rms":
        in_specs += [o_spec, vec_spec]
        out_shape = [jax.ShapeDtypeStruct((M, N), F32), jax.ShapeDtypeStruct((M, N), BF16)]
        out_specs = [o_spec, o_spec]
    else:
        in_specs += [o_spec] * len(extras)
        out_shape = [jax.ShapeDtypeStruct((M, N), BF16 if epi == "relu2" else out_dtype)] * n_out
        out_specs = [o_spec] * n_out
    res = pl.pallas_call(
        body, name=name, grid=(M // tm, N // tn, nk),
        in_specs=in_specs, out_specs=out_specs, out_shape=out_shape,
        scratch_shapes=[pltpu.VMEM((tm, tn), F32)] if nk > 1 else [],
        compiler_params=_params(("arbitrary",) * 3 if epi == "rms_bwd" else ("parallel", "parallel", "arbitrary")),
    )(*ins)
    return res if n_out > 1 else res[0]


def _dot(a, b):
    return lax.dot_general(a, b, _DIMS["nn"], preferred_element_type=F32)


def _dot_nt(a, b):
    return lax.dot_general(a, b, _DIMS["nt"], preferred_element_type=F32)


def _dot_tn(a, b):
    return lax.dot_general(a, b, _DIMS["tn"], preferred_element_type=F32)


def _split_dot(x, t):
    hi = x.astype(BF16)
    lo = (x - hi.astype(F32)).astype(BF16)
    return _dot(jnp.concatenate([hi, lo], axis=1), jnp.concatenate([t, t], axis=0))


def _head_pair(ref, scale=None):
    xf = ref[...].astype(F32)
    if scale is not None:
        xf = xf * scale
    is_a = lax.broadcasted_iota(jnp.int32, xf.shape, 1) < HEAD_DIM
    return jnp.where(is_a, xf, 0.0).astype(BF16), jnp.where(is_a, 0.0, xf).astype(BF16)


def _stack(a, b):
    return jnp.concatenate([a, b], axis=0)


def _head_rows(ref, scale=None):
    return _stack(*_head_pair(ref, scale))


def _unstack_heads(x):
    rows = x.shape[0] // 2
    return _select_pair(x[:rows], x[rows:])


def _pair_rowsum(x):
    is_a = lax.broadcasted_iota(jnp.int32, x.shape, 1) < HEAD_DIM
    return (jnp.sum(jnp.where(is_a, x, 0.0), axis=1, keepdims=True),
            jnp.sum(jnp.where(is_a, 0.0, x), axis=1, keepdims=True))


def _select_pair(xa, xb):
    is_a = lax.broadcasted_iota(jnp.int32, xa.shape, 1) < HEAD_DIM
    return jnp.where(is_a, xa, xb)


def _two_cols(xa, xb):
    rows = xa.shape[0]
    first = lax.broadcasted_iota(jnp.int32, (rows, 2), 1) == 0
    return jnp.where(first, xa, xb)


def _softplus_parts(z):
    e = jnp.exp(-jnp.abs(z))
    return jnp.maximum(z, 0.0) + jnp.log(1.0 + e), e


def _tile_iotas():
    row = lax.broadcasted_iota(jnp.int32, (BK, BK), 0)
    col = lax.broadcasted_iota(jnp.int32, (BK, BK), 1)
    return row, col


def _stacked_iotas(bq, nk):
    row = lax.broadcasted_iota(jnp.int32, (2 * bq, nk), 0) & (bq - 1)
    col = lax.broadcasted_iota(jnp.int32, (2 * bq, nk), 1)
    return row, col


def _side_exchange(exchange, operand, n_in, n_out):
    out_shape = jax.ShapeDtypeStruct(((N_CHIPS,) + operand.shape) if exchange is _AllGather else operand.shape,
                                     operand.dtype)
    n_sem = len(exchange.SCRATCH)

    def pick(refs):
        def make():
            return exchange(refs[n_in], refs[n_in + 1 + n_out], *refs[len(refs) - n_sem:])

        return (lambda: make().start()), (lambda: make().finish())

    return [operand], [ANY], [out_shape], [ANY], pick


def _sb_fwd(proj, name, exchange, operand):
    S = proj.shape[0]
    nqb = S // (CHAINS * BQ)
    x_in, x_in_specs, x_out, x_out_specs, pick = _side_exchange(exchange, operand, 3, 1)

    def body(*refs):
        q_ref, k_ref, v_ref = refs[:3]
        o_ref = refs[3 + len(x_in)]
        acc_ref = refs[3 + len(x_in) + 1 + len(x_out)]
        start, finish = pick(refs)
        p = pl.program_id(0)
        i = pl.program_id(1)

        @pl.when(jnp.logical_and(p == 0, i == 0))
        def _():
            start()

        q2 = [_head_rows(q_ref.at[pl.ds(ch * BQ, BQ)], SCALE) for ch in range(CHAINS)]
        row, col = _tile_iotas()
        tri = (row > col).astype(BF16)
        srow, scol = _stacked_iotas(BQ, BK)
        acc_ref[...] = jnp.zeros_like(acc_ref)

        def tile(ch, kb, c, dmask=None, valid=None):
            r0 = pl.multiple_of(kb * BK, BK)
            kblk = k_ref[pl.ds(r0, BK), :]
            vblk = v_ref[pl.ds(r0, BK), :]
            z = _dot_nt(q2[ch], kblk)
            sp, _ = _softplus_parts(z)
            lm = -sp
            if dmask is not None:
                lm = jnp.where(dmask, lm, 0.0)
            btw = _split_dot(lm, tri)
            w = jnp.exp((z - sp) + btw + c)
            if dmask is not None:
                w = jnp.where(dmask, w, 0.0)
            if valid is not None:
                w = w * valid
            acc_ref[ch] += _dot(w.astype(BF16), vblk)
            return c + btw[:, 0:1] + lm[:, 0:1]

        def alive(c):
            return jnp.max(c) > UNDERFLOW_BOUND

        cs = [jnp.zeros((2 * BQ, 1), F32)] * CHAINS
        for d in reversed(range(DIAG_TILES)):
            cs = [tile(ch, (CHAINS * i + ch) * DIAG_TILES + d, cs[ch], dmask=scol < srow - d * BK)
                  for ch in range(CHAINS)]

        def tile_of(ch, t):
            return (CHAINS * i + ch) * DIAG_TILES - 1 - t

        def more(cs, t):
            go = [jnp.logical_and(alive(cs[ch]), tile_of(ch, t) >= 0) for ch in range(CHAINS)]
            return functools.reduce(jnp.logical_or, go).astype(jnp.int32)

        def step(st):
            t, _, cs = st
            new = []
            for ch in range(CHAINS):
                kb = tile_of(ch, t)
                if ch == CHAINS - 1:
                    new.append(tile(ch, kb, cs[ch]))
                else:
                    new.append(tile(ch, jnp.maximum(kb, 0), cs[ch], valid=(kb >= 0).astype(F32)))
            return t + 1, more(new, t + 1), new

        lax.while_loop(lambda st: st[1] > 0, step, (0, more(cs, 0), cs))
        for ch in range(CHAINS):
            o_ref[pl.ds(ch * BQ, BQ), :] = _unstack_heads(acc_ref[ch])

        @pl.when(jnp.logical_and(p == 3, i == nqb - 1))
        def _():
            finish()

    blk = pl.BlockSpec((CHAINS * BQ, LANES), lambda p, i: (i, p))
    res = pl.pallas_call(
        body, name=name, grid=(4, nqb),
        in_specs=[blk, pl.BlockSpec((S, LANES), lambda p, i: (0, GROUP_COLS + p)),
                  pl.BlockSpec((S, LANES), lambda p, i: (0, 2 * GROUP_COLS + p))] + x_in_specs,
        out_specs=[blk] + x_out_specs,
        out_shape=[jax.ShapeDtypeStruct((S, MERGED_WIDTH), F32)] + x_out,
        scratch_shapes=[pltpu.VMEM((CHAINS, 2 * BQ, LANES), F32)] + exchange.SCRATCH,
        compiler_params=_params(("arbitrary", "arbitrary")),
    )(proj, proj, proj, *x_in)
    return res


def _sb_bwd(proj, merged, dmerged, name, exchange, operand):
    S = proj.shape[0]
    nqb = S // (CHAINS * BQ)
    x_in, x_in_specs, x_out, x_out_specs, pick = _side_exchange(exchange, operand, 5, 3)

    def body(*refs):
        q_ref, k_ref, v_ref, o_ref, do_ref = refs[:5]
        dq_ref, dk_hbm, dv_hbm = refs[5 + len(x_in):8 + len(x_in)]
        dq_acc, dk_acc, dv_acc, sem = refs[8 + len(x_in) + len(x_out):12 + len(x_in) + len(x_out)]
        start, finish = pick(refs)
        p = pl.program_id(0)
        i = pl.program_id(1)

        @pl.when(jnp.logical_and(p == 0, i == 0))
        def _():
            start()

        @pl.when(i == 0)
        def _():
            dk_acc[...] = jnp.zeros_like(dk_acc)
            dv_acc[...] = jnp.zeros_like(dv_acc)

        rows = [pl.ds(ch * BQ, BQ) for ch in range(CHAINS)]
        q2 = [_head_rows(q_ref.at[rw], SCALE) for rw in rows]
        do2 = [_head_rows(do_ref.at[rw]) for rw in rows]
        tot = [_stack(*_pair_rowsum(do_ref[rw, :].astype(F32) * o_ref[rw, :])) for rw in rows]
        row, col = _tile_iotas()
        tri_gt = (row > col).astype(BF16)
        tri_ge = (row >= col).astype(BF16)
        srow, scol = _stacked_iotas(BQ, BK)
        dq_acc[...] = jnp.zeros_like(dq_acc)

        def tile(ch, kb, st, dmask=None, valid=None):
            masked = dmask is not None
            c, r = st
            r0 = pl.multiple_of(kb * BK, BK)
            kblk = k_ref[pl.ds(r0, BK), :]
            vblk = v_ref[pl.ds(r0, BK), :]
            z = _dot_nt(q2[ch], kblk)
            sp, e = _softplus_parts(z)
            lm = -sp
            if masked:
                lm = jnp.where(dmask, lm, 0.0)
            btw = _split_dot(lm, tri_gt)
            w = jnp.exp((z - sp) + btw + c)
            if masked:
                w = jnp.where(dmask, w, 0.0)
            if valid is not None:
                w = w * valid
            wb = w.astype(BF16)
            a = wb.astype(F32) * _dot_nt(do2[ch], vblk)
            suffix = _split_dot(a, tri_ge) + r
            rcp = 1.0 / (1.0 + e)
            pos = z >= 0.0
            sig = jnp.where(pos, rcp, e * rcp)
            sig_neg = jnp.where(pos, e * rcp, rcp)
            dz = a * sig_neg - (tot[ch] - suffix) * sig
            if masked:
                dz = jnp.where(dmask, dz, 0.0)
            if valid is not None:
                dz = dz * valid
            dzb = dz.astype(BF16)
            dq_acc[ch] += _dot(dzb, kblk)
            dk_acc[pl.ds(r0, BK), :] += _dot_tn(dzb, q2[ch])
            dv_acc[pl.ds(r0, BK), :] += _dot_tn(wb, do2[ch])
            return c + btw[:, 0:1] + lm[:, 0:1], suffix[:, 0:1]

        def alive(st):
            return jnp.max(st[0]) > UNDERFLOW_BOUND

        zero = jnp.zeros((2 * BQ, 1), F32)
        sts = [(zero, zero)] * CHAINS
        for d in reversed(range(DIAG_TILES)):
            sts = [tile(ch, (CHAINS * i + ch) * DIAG_TILES + d, sts[ch], dmask=scol < srow - d * BK)
                   for ch in range(CHAINS)]

        def tile_of(ch, t):
            return (CHAINS * i + ch) * DIAG_TILES - 1 - t

        def more(sts, t):
            go = [jnp.logical_and(alive(sts[ch]), tile_of(ch, t) >= 0) for ch in range(CHAINS)]
            return functools.reduce(jnp.logical_or, go).astype(jnp.int32)

        def step(s):
            t, _, sts = s
            new = []
            for ch in range(CHAINS):
                kb = tile_of(ch, t)
                if ch == CHAINS - 1:
                    new.append(tile(ch, kb, sts[ch]))
                else:
                    new.append(tile(ch, jnp.maximum(kb, 0), sts[ch], valid=(kb >= 0).astype(F32)))
            return t + 1, more(new, t + 1), new

        lax.while_loop(lambda s: s[1] > 0, step, (0, more(sts, 0), sts))
        for ch in range(CHAINS):
            dq_ref[rows[ch], :] = (_unstack_heads(dq_acc[ch]) * SCALE).astype(dq_ref.dtype)

        @pl.when(i == nqb - 1)
        def _():
            ck = pltpu.make_async_copy(dk_acc, dk_hbm.at[p], sem.at[0])
            cv = pltpu.make_async_copy(dv_acc, dv_hbm.at[p], sem.at[1])
            ck.start()
            cv.start()
            ck.wait()
            cv.wait()

        @pl.when(jnp.logical_and(p == 3, i == nqb - 1))
        def _():
            finish()

    blk = lambda off: pl.BlockSpec((CHAINS * BQ, LANES), lambda p, i: (i, off + p))
    slab = lambda off: pl.BlockSpec((S, LANES), lambda p, i: (0, off + p))
    return pl.pallas_call(
        body, name=name, grid=(4, nqb),
        in_specs=[blk(0), slab(GROUP_COLS), slab(2 * GROUP_COLS), blk(0), blk(0)] + x_in_specs,
        out_specs=[blk(0), ANY, ANY] + x_out_specs,
        out_shape=[jax.ShapeDtypeStruct((S, MIX_WIDTH), BF16),
                   jax.ShapeDtypeStruct((4, S, LANES), F32), jax.ShapeDtypeStruct((4, S, LANES), F32)] + x_out,
        scratch_shapes=[pltpu.VMEM((CHAINS, 2 * BQ, LANES), F32), pltpu.VMEM((S, LANES), F32),
                        pltpu.VMEM((S, LANES), F32), pltpu.SemaphoreType.DMA((2,))]
        + exchange.SCRATCH,
        compiler_params=_params(("arbitrary", "arbitrary")),
    )(proj, proj, proj, merged, dmerged, *x_in)


def _key_norm_max(k_ref, knorm_ref, nkb):
    def step(kb, m):
        r0 = pl.multiple_of(kb * BK, BK)
        blk = k_ref[pl.ds(r0, BK), :].astype(F32)
        sa, sb = _pair_rowsum(blk * blk)
        return (jnp.maximum(m[0], jnp.max(sa, axis=0, keepdims=True)),
                jnp.maximum(m[1], jnp.max(sb, axis=0, keepdims=True)))

    zero = jnp.zeros((1, 1), F32)
    ma, mb = lax.fori_loop(0, nkb, step, (zero, zero))
    knorm_ref[...] = _select_pair(jnp.broadcast_to(ma, (1, LANES)), jnp.broadcast_to(mb, (1, LANES)))


FQ = 512
FK = FQ
GATE_BLOCKS = FK // BK


def _key_gates(cr_ref, kb):
    blocks = [cr_ref[0, GATE_BLOCKS * kb + j] for j in range(GATE_BLOCKS)]
    per_head = [jnp.broadcast_to(jnp.concatenate([b[h:h + 1] for b in blocks], axis=1), (FQ, FK)) for h in range(2)]
    return _stack(*per_head)


def _last_gate(cr_ref, kb):
    last = cr_ref[0, GATE_BLOCKS * jnp.maximum(kb, 0) + GATE_BLOCKS - 1]
    return _stack(*[jnp.broadcast_to(last[h:h + 1, BK - 1:BK], (FQ, 1)) for h in range(2)])


def _logit_bound(q_ref, knorm_ref):
    qf = q_ref[...].astype(F32) * SCALE
    qa, qb = _pair_rowsum(qf * qf)
    kn = knorm_ref[...]
    return _stack(jnp.sqrt(qa * kn[:, 0:1]), jnp.sqrt(qb * kn[:, HEAD_DIM:HEAD_DIM + 1]))


def _causal_bias(bias_ref):
    srow, scol = _stacked_iotas(FQ, FK)
    bias_ref[...] = jnp.where(scol <= srow, 0.0, NEG_INF)


def _fox_fwd(proj, kv, c_col, c_row, name):
    S = proj.shape[0]
    nqb = S // FQ

    def body(q_ref, k_ref, v_ref, cc_ref, cr_ref, o_ref, lse_ref, acc_ref, knorm_ref, bias_ref):
        i = pl.program_id(1)

        @pl.when(i == 0)
        def _():
            _key_norm_max(k_ref, knorm_ref, S // BK)
            _causal_bias(bias_ref)

        q2 = _head_rows(q_ref, SCALE)
        bound = _logit_bound(q_ref, knorm_ref)
        cc = cc_ref[0]
        ct = _stack(cc[:, 0:1], cc[:, 1:2])
        acc_ref[...] = jnp.zeros_like(acc_ref)

        def tile(kb, st, on_diagonal):
            m, l = st
            r0 = pl.multiple_of(kb * FK, FK)
            kblk = k_ref[pl.ds(r0, FK), :]
            vblk = v_ref[pl.ds(r0, FK), :]
            z = _dot_nt(q2, kblk) + ct - _key_gates(cr_ref, kb) + bias_ref[...] * on_diagonal
            m_new = jnp.maximum(m, jnp.max(z, axis=1, keepdims=True))
            alpha = jnp.exp(m - m_new)
            pr = jnp.exp(z - m_new)
            acc_ref[...] = alpha * acc_ref[...] + _split_dot(pr, vblk)
            return m_new, alpha * l + jnp.sum(pr, axis=1, keepdims=True)

        def alive(kb, st):
            reach = bound + ct - _last_gate(cr_ref, kb) - st[0]
            return (jnp.max(reach) > UNDERFLOW_BOUND).astype(jnp.int32)

        neg = jnp.full((2 * FQ, 1), NEG_INF, F32)
        zero = jnp.zeros((2 * FQ, 1), F32)
        def cond(s):
            return jnp.logical_and(s[0] >= 0, s[1] > 0)

        def step(s):
            kb, _, st = s
            st = tile(kb, st, (kb == i).astype(F32))
            return kb - 1, alive(kb - 1, st), st

        _, _, (m, l) = lax.while_loop(cond, step, (i, jnp.int32(1), (neg, zero)))
        o_ref[...] = _unstack_heads(acc_ref[...] / l)
        lse = m + jnp.log(l)
        lse_ref[0] = _two_cols(lse[:FQ], lse[FQ:])

    return pl.pallas_call(
        body, name=name, grid=(4, nqb),
        in_specs=[pl.BlockSpec((FQ, LANES), lambda p, i: (i, p)),
                  pl.BlockSpec((S, LANES), lambda p, i: (0, p)),
                  pl.BlockSpec((S, LANES), lambda p, i: (0, GROUP_COLS + p)),
                  pl.BlockSpec((1, FQ, 2), lambda p, i: (p, i, 0)),
                  pl.BlockSpec((1, S // BK, 8, LANES), lambda p, i: (p, 0, 0, 0))],
        out_specs=[pl.BlockSpec((FQ, LANES), lambda p, i: (i, p)),
                   pl.BlockSpec((1, FQ, 2), lambda p, i: (p, i, 0))],
        out_shape=[jax.ShapeDtypeStruct((S, MERGED_WIDTH), F32), jax.ShapeDtypeStruct((4, S, 2), F32)],
        scratch_shapes=[pltpu.VMEM((2 * FQ, LANES), F32), pltpu.VMEM((1, LANES), F32),
                        pltpu.VMEM((2 * FQ, FK), F32)],
        compiler_params=_params(("arbitrary", "arbitrary")),
    )(proj, kv, kv, c_col, c_row)


def _fox_bwd(proj, kv, c_col, c_row, lse, merged, dmerged, dk_prev, dv_prev, dc_prev, name):
    S = proj.shape[0]
    nqb = S // FQ

    def body(q_ref, k_ref, v_ref, cc_ref, cr_ref, lse_ref, o_ref, do_ref, dkp_hbm, dvp_hbm, dcp_ref,
             dq_ref, dk_hbm, dv_hbm, dc_ref, dq_acc, dk_acc, dv_acc, knorm_ref, bias_ref, sem):
        p = pl.program_id(0)
        i = pl.program_id(1)

        @pl.when(i == 0)
        def _():
            ck = pltpu.make_async_copy(dkp_hbm.at[p], dk_acc, sem.at[0])
            cv = pltpu.make_async_copy(dvp_hbm.at[p], dv_acc, sem.at[1])
            ck.start()
            cv.start()
            dc_ref[...] = dcp_ref[...]
            _key_norm_max(k_ref, knorm_ref, S // BK)
            _causal_bias(bias_ref)
            ck.wait()
            cv.wait()

        q2 = _head_rows(q_ref, SCALE)
        do2 = _head_rows(do_ref)
        tot = _stack(*_pair_rowsum(do_ref[...].astype(F32) * o_ref[...]))
        bound = _logit_bound(q_ref, knorm_ref)
        cc = cc_ref[0]
        ct = _stack(cc[:, 0:1], cc[:, 1:2])
        ls = lse_ref[0]
        lse = _stack(ls[:, 0:1], ls[:, 1:2])
        sub = lax.broadcasted_iota(jnp.int32, (8, LANES), 0)
        dq_acc[...] = jnp.zeros_like(dq_acc)

        def tile(kb, masked):
            r0 = pl.multiple_of(kb * FK, FK)
            kblk = k_ref[pl.ds(r0, FK), :]
            vblk = v_ref[pl.ds(r0, FK), :]
            z = _dot_nt(q2, kblk) + ct - _key_gates(cr_ref, kb)
            if masked:
                z = z + bias_ref[...]
            pr = jnp.exp(z - lse)
            ds = pr * (_dot_nt(do2, vblk) - tot)
            dsb = ds.astype(BF16)
            dq_acc[...] += _dot(dsb, kblk)
            dk_acc[pl.ds(r0, FK), :] += _dot_tn(dsb, q2)
            dv_acc[pl.ds(r0, FK), :] += _dot_tn(pr.astype(BF16), do2)
            dca = jnp.sum(ds[:FQ], axis=0, keepdims=True)
            dcb = jnp.sum(ds[FQ:], axis=0, keepdims=True)
            for j in range(GATE_BLOCKS):
                cols = slice(j * BK, (j + 1) * BK)
                old = dc_ref[0, GATE_BLOCKS * kb + j]
                dc_ref[0, GATE_BLOCKS * kb + j] = jnp.where(sub == 0, old - dca[:, cols],
                                                            jnp.where(sub == 1, old - dcb[:, cols], old))

        def alive(kb):
            reach = bound + ct - _last_gate(cr_ref, kb) - lse
            return (jnp.max(reach) > UNDERFLOW_BOUND).astype(jnp.int32)

        tile(i, True)

        def cond(s):
            return jnp.logical_and(s[0] >= 0, s[1] > 0)

        def step(s):
            kb, _ = s
            tile(kb, False)
            return kb - 1, alive(kb - 1)

        lax.while_loop(cond, step, (i - 1, alive(i - 1)))
        dq_ref[...] = (_unstack_heads(dq_acc[...]) * SCALE).astype(dq_ref.dtype)

        @pl.when(i == nqb - 1)
        def _():
            ck = pltpu.make_async_copy(dk_acc, dk_hbm.at[p], sem.at[0])
            cv = pltpu.make_async_copy(dv_acc, dv_hbm.at[p], sem.at[1])
            ck.start()
            cv.start()
            ck.wait()
            cv.wait()

    blk = lambda off: pl.BlockSpec((FQ, LANES), lambda p, i: (i, off + p))
    slab = lambda off: pl.BlockSpec((S, LANES), lambda p, i: (0, off + p))
    cols = pl.BlockSpec((1, FQ, 2), lambda p, i: (p, i, 0))
    rows = pl.BlockSpec((1, S // BK, 8, LANES), lambda p, i: (p, 0, 0, 0))
    return pl.pallas_call(
        body, name=name, grid=(4, nqb),
        in_specs=[blk(0), slab(0), slab(GROUP_COLS), cols, rows, cols, blk(0), blk(0), ANY, ANY, rows],
        out_specs=[blk(0), ANY, ANY, rows],
        out_shape=[jax.ShapeDtypeStruct((S, MIX_WIDTH), BF16),
                   jax.ShapeDtypeStruct((4, S, LANES), F32), jax.ShapeDtypeStruct((4, S, LANES), F32),
                   jax.ShapeDtypeStruct((4, S // BK, 8, LANES), F32)],
        scratch_shapes=[pltpu.VMEM((2 * FQ, LANES), F32), pltpu.VMEM((S, LANES), F32),
                        pltpu.VMEM((S, LANES), F32), pltpu.VMEM((1, LANES), F32),
                        pltpu.VMEM((2 * FQ, FK), F32), pltpu.SemaphoreType.DMA((2,))],
        compiler_params=_params(("arbitrary", "arbitrary")),
    )(proj, kv, kv, c_col, c_row, lse, merged, dmerged, dk_prev, dv_prev, dc_prev)


def _lane_scan(x, reverse):
    lane = lax.broadcasted_iota(jnp.int32, x.shape, 1)
    d = 1
    while d < LANES:
        if reverse:
            x = x + jnp.where(lane < LANES - d, pltpu.roll(x, LANES - d, 1), 0.0)
        else:
            x = x + jnp.where(lane >= d, pltpu.roll(x, d, 1), 0.0)
        d *= 2
    return x


def _gate_fwd(fl3, b8):
    nb = fl3.shape[0]

    def body(fl_ref, b_ref, c_ref):
        def step(kb, carry):
            x = fl_ref[kb] + b_ref[...]
            sp, _ = _softplus_parts(-x)
            c = _lane_scan(-sp, False) + carry
            c_ref[kb] = c
            return c[:, LANES - 1:LANES]

        lax.fori_loop(0, nb, step, jnp.zeros((8, 1), F32))

    return pl.pallas_call(body, name="forget_gate_cumsum",
                          out_shape=jax.ShapeDtypeStruct(fl3.shape, F32),
                          compiler_params=_params())(fl3, b8)


def _gate_bwd(dc3, fl3, b8):
    nb = fl3.shape[0]

    def body(dc_ref, fl_ref, b_ref, dfl_ref, db_ref):
        def step(t, st):
            carry, dbs = st
            kb = nb - 1 - t
            g = _lane_scan(dc_ref[kb], True) + carry
            x = fl_ref[kb] + b_ref[...]
            e = jnp.exp(-jnp.abs(x))
            rcp = 1.0 / (1.0 + e)
            dfl = g * jnp.where(x >= 0.0, e * rcp, rcp)
            dfl_ref[kb] = dfl
            return g[:, 0:1], dbs + dfl

        _, dbs = lax.fori_loop(0, nb, step, (jnp.zeros((8, 1), F32), jnp.zeros((8, LANES), F32)))
        db_ref[...] = jnp.broadcast_to(jnp.sum(dbs, axis=1, keepdims=True), (8, LANES))

    return pl.pallas_call(body, name="forget_gate_bwd",
                          out_shape=[jax.ShapeDtypeStruct(fl3.shape, F32), jax.ShapeDtypeStruct((8, LANES), F32)],
                          compiler_params=_params())(dc3, fl3, b8)


MEM_TQ = 512
MEM_COLS = MEM_WIDTH // LANES


def _mem_fwd(proj, qcol, mkv, mix, name):
    S = proj.shape[0]
    M = mkv.shape[0]

    def body(q_ref, mk_ref, mv_ref, mix_ref, o_ref, lse_ref):
        q2 = _head_rows(q_ref, SCALE)
        s = _dot_nt(q2, mk_ref[...])
        m = jnp.max(s, axis=1, keepdims=True)
        pr = jnp.exp(s - m)
        l = jnp.sum(pr, axis=1, keepdims=True)
        o_ref[...] = _unstack_heads(_dot(pr.astype(BF16), mv_ref[...]) / l)
        lse = m + jnp.log(l)
        lse_ref[0] = _two_cols(lse[:MEM_TQ], lse[MEM_TQ:])

    return pl.pallas_call(
        body, name=name, grid=(MEM_COLS, S // MEM_TQ),
        in_specs=[pl.BlockSpec((MEM_TQ, LANES), lambda p, i: (i, qcol + p)),
                  pl.BlockSpec((M, LANES), lambda p, i: (0, p)),
                  pl.BlockSpec((M, LANES), lambda p, i: (0, MEM_COLS + p)), ANY],
        out_specs=[pl.BlockSpec((MEM_TQ, LANES), lambda p, i: (i, GROUP_COLS + p)),
                   pl.BlockSpec((1, MEM_TQ, 2), lambda p, i: (p, i, 0))],
        out_shape=[jax.ShapeDtypeStruct((S, MERGED_WIDTH), F32), jax.ShapeDtypeStruct((2, S, 2), F32)],
        input_output_aliases={3: 0},
        compiler_params=_params(("parallel", "parallel")),
    )(proj, mkv, mkv, mix)


def _mem_bwd(proj, qcol, mkv, lse, merged, dmerged, name):
    S = proj.shape[0]
    M = mkv.shape[0]

    def body(q_ref, mk_ref, mv_ref, lse_ref, o_ref, do_ref, dq_ref, dmk_ref, dmv_ref):
        @pl.when(pl.program_id(1) == 0)
        def _():
            dmk_ref[...] = jnp.zeros_like(dmk_ref)
            dmv_ref[...] = jnp.zeros_like(dmv_ref)

        q2 = _head_rows(q_ref, SCALE)
        do2 = _head_rows(do_ref)
        tot = _stack(*_pair_rowsum(do_ref[...].astype(F32) * o_ref[...]))
        ls = lse_ref[0]
        pr = jnp.exp(_dot_nt(q2, mk_ref[...]) - _stack(ls[:, 0:1], ls[:, 1:2]))
        ds = pr * (_dot_nt(do2, mv_ref[...]) - tot)
        dsb = ds.astype(BF16)
        dmk_ref[...] += _dot_tn(dsb, q2)
        dmv_ref[...] += _dot_tn(pr.astype(BF16), do2)
        dq_ref[...] = (_unstack_heads(_dot(dsb, mk_ref[...])) * SCALE).astype(dq_ref.dtype)

    blk = lambda off: pl.BlockSpec((MEM_TQ, LANES), lambda p, i: (i, off + p))
    acc = pl.BlockSpec((M, LANES), lambda p, i: (0, p))
    return pl.pallas_call(
        body, name=name, grid=(MEM_COLS, S // MEM_TQ),
        in_specs=[blk(qcol), pl.BlockSpec((M, LANES), lambda p, i: (0, p)),
                  pl.BlockSpec((M, LANES), lambda p, i: (0, MEM_COLS + p)),
                  pl.BlockSpec((1, MEM_TQ, 2), lambda p, i: (p, i, 0)), blk(GROUP_COLS), blk(GROUP_COLS)],
        out_specs=[blk(0), acc, acc],
        out_shape=[jax.ShapeDtypeStruct((S, MEM_WIDTH), BF16), jax.ShapeDtypeStruct((M, MEM_WIDTH), F32),
                   jax.ShapeDtypeStruct((M, MEM_WIDTH), F32)],
        compiler_params=_params(("parallel", "arbitrary")),
    )(proj, mkv, mkv, lse, merged, dmerged)


def _c_layouts(c3):
    nb = c3.shape[0]
    pairs = c3.reshape(nb, 4, 2, LANES).transpose(1, 0, 2, 3)
    c_row = jnp.pad(pairs, ((0, 0), (0, 0), (0, 6), (0, 0)))
    c_col = pairs.transpose(0, 1, 3, 2).reshape(4, nb * LANES, 2)
    return c_col, c_row


def _local_step(x, mem, wb, shards, sm, loss_target):
    S = x.shape[0]
    nb = S // BK
    vec = lambda a: a.reshape(1, D_MODEL)
    b8 = jnp.broadcast_to(sm["b_f"].reshape(8, 1), (8, LANES))

    saved = []
    shared = None
    h = x
    hn = _rms_fwd(h, vec(sm["norm1_g"][0]), "norm1_0")
    for l in range(DEPTH):
        if l == N_A:
            w_kvf = jnp.pad(wb["w_kv_shared"], ((0, 0), (0, KVF_WIDTH - W_KV_SHARED)))
            hs = _rms_fwd(h, vec(sm["kv_norm_g"]), "kv_norm")
            kvf = _mm(hs, w_kvf, out_dtype=F32, name="kv_shared_proj")
            kv = kvf[:, :2 * MIX_WIDTH].astype(BF16)
            fl3 = kvf[:, 2 * MIX_WIDTH:2 * MIX_WIDTH + 8].T.reshape(8, nb, LANES).transpose(1, 0, 2)
            c3 = _gate_fwd(fl3, b8)
            c_col, c_row = _c_layouts(c3)
            shared = dict(h=h, hs=hs, kv=kv, fl3=fl3, c_col=c_col, c_row=c_row)
        mn = _rms_fwd(mem, vec(sm["mem_norm_g"][l]), f"mem_norm_{l}")
        mkv = _mm(mn, wb["w_mem_kv"][l], name=f"mem_kv_proj_{l}")
        if l < N_A:
            w_in = wb["w_in_a"][l]
            proj = _mm(hn, w_in, name=f"in_proj_{l}")
            mix, gathered = _sb_fwd(proj, f"stickbreak_fwd_{l}", _AllGather, shards[l])
            _unpack_gathered(PARTS[1 + l], gathered, wb)
            lse, qcol = None, Q_MEM_COL_A
        else:
            w_in = wb["w_in_b"][l - N_A]
            proj = _mm(hn, w_in, name=f"in_proj_{l}")
            mix, lse = _fox_fwd(proj, shared["kv"], shared["c_col"], shared["c_row"], f"fox_fwd_{l}")
            qcol = Q_MEM_COL_B
        merged, mlse = _mem_fwd(proj, qcol, mkv, mix, f"mem_attn_fwd_{l}")
        h_mid, hn2 = _mm(merged, wb["w_o"][l], epi="add_rms", extra=(h, vec(sm["norm2_g"][l])),
                         name=f"out_proj_{l}")
        u, act = _mm(hn2, wb["w_mlp1"][l], epi="relu2", name=f"mlp1_{l}")
        saved.append(dict(h=h, hn=hn, mn=mn, mkv=mkv, proj=proj, lse=lse, mlse=mlse, qcol=qcol, merged=merged,
                          h_mid=h_mid, hn2=hn2, u=u, act=act, w_in=w_in))
        if l + 1 < DEPTH:
            h, hn = _mm(act, wb["w_mlp2"][l], epi="add_rms", extra=(h_mid, vec(sm["norm1_g"][l + 1])),
                        name=f"mlp2_{l}")
        else:
            h = _mm(act, wb["w_mlp2"][l], out_dtype=F32, epi="add", extra=h_mid, name=f"mlp2_{l}")

    loss, dh, dhb, dg_final = _final_loss(h, vec(sm["final_norm_g"]), loss_target)

    gb = {n: [None] * (DEPTH if n not in ("w_in_a", "w_in_b") else 2) for n in
          ("w_in_a", "w_in_b", "w_mem_kv", "w_o", "w_mlp1", "w_mlp2")}
    gs = {n: [None] * DEPTH for n in ("norm1_g", "mem_norm_g", "norm2_g")}
    received = [None] * N_A
    dk_sh = jnp.zeros((4, S, LANES), F32)
    dv_sh = jnp.zeros((4, S, LANES), F32)
    dc_sh = jnp.zeros((4, nb, 8, LANES), F32)
    for l in reversed(range(DEPTH)):
        sv = saved[l]
        du = _mm(dhb, wb["w_mlp2"][l], mode="nt", epi="drelu2", extra=sv["u"], name=f"mlp2_dx_{l}")
        gb["w_mlp2"][l] = _mm(sv["act"], dhb, mode="tn", out_dtype=F32, name=f"mlp2_dw_{l}")
        gb["w_mlp1"][l] = _mm(sv["hn2"], du, mode="tn", out_dtype=F32, name=f"mlp1_dw_{l}")
        dh, dhb, gs["norm2_g"][l] = _mm(du, wb["w_mlp1"][l], mode="nt", epi="rms_bwd",
                                        extra=(sv["h_mid"], vec(sm["norm2_g"][l]), dh),
                                        name=f"mlp1_dx_norm2_bwd_{l}")
        dmerged = _mm(dhb, wb["w_o"][l], mode="nt", name=f"out_proj_dx_{l}")
        gb["w_o"][l] = _mm(sv["merged"], dhb, mode="tn", out_dtype=F32, name=f"out_proj_dw_{l}")
        if l < N_A:
            ready = _pack_grads(PARTS[1 + l], PART_ROWS[1 + l], gb, None)
            dq, dk, dv, received[l] = _sb_bwd(sv["proj"], sv["merged"], dmerged, f"stickbreak_bwd_{l}",
                                              _Scatter, ready)
        else:
            dq, dk_sh, dv_sh, dc_sh = _fox_bwd(sv["proj"], shared["kv"], shared["c_col"], shared["c_row"],
                                               sv["lse"], sv["merged"], dmerged, dk_sh, dv_sh, dc_sh,
                                               f"fox_bwd_{l}")
        dqm, dmk, dmv = _mem_bwd(sv["proj"], sv["qcol"], sv["mkv"], sv["mlse"], sv["merged"], dmerged,
                                 f"mem_attn_bwd_{l}")
        if l < N_A:
            flat = lambda t: t.transpose(1, 0, 2).reshape(S, MIX_WIDTH).astype(BF16)
            dproj = jnp.concatenate([dq, flat(dk), flat(dv), dqm], axis=1)
        else:
            dproj = jnp.concatenate([dq, dqm], axis=1)
        name_in = "w_in_a" if l < N_A else "w_in_b"
        gb[name_in][l if l < N_A else l - N_A] = _mm(sv["hn"], dproj, mode="tn", out_dtype=F32,
                                                      name=f"in_proj_dw_{l}")
        dh, dhb, gs["norm1_g"][l] = _mm(dproj, sv["w_in"], mode="nt", epi="rms_bwd",
                                        extra=(sv["h"], vec(sm["norm1_g"][l]), dh),
                                        name=f"in_proj_dx_norm1_bwd_{l}")
        dmkv = jnp.concatenate([dmk, dmv], axis=1)
        gb["w_mem_kv"][l] = _mm(sv["mn"], dmkv, mode="tn", out_dtype=F32, name=f"mem_kv_dw_{l}")
        dmn = _mm(dmkv, wb["w_mem_kv"][l], mode="nt", out_dtype=F32, name=f"mem_kv_dx_{l}")
        gs["mem_norm_g"][l] = _rms_gain_grad(mem, dmn, f"mem_norm_bwd_{l}")
        if l == N_A:
            dfl3, db8 = _gate_bwd(dc_sh.reshape(4, nb, 8, LANES)[:, :, :2].transpose(1, 0, 2, 3).reshape(nb, 8, LANES),
                                  shared["fl3"], b8)
            dfl = dfl3.transpose(1, 0, 2).reshape(8, S).T
            flat = lambda t: t.transpose(1, 0, 2).reshape(S, MIX_WIDTH).astype(BF16)
            dkvf = jnp.concatenate([flat(dk_sh), flat(dv_sh),
                                    jnp.pad(dfl, ((0, 0), (0, LANES - 8))).astype(BF16)], axis=1)
            gb["w_kv_shared"] = _mm(shared["hs"], dkvf, mode="tn", out_dtype=F32, name="kv_shared_dw")[:, :W_KV_SHARED]
            dh, dhb, g_kvn = _mm(dkvf, w_kvf, mode="nt", epi="rms_bwd",
                                 extra=(shared["h"], vec(sm["kv_norm_g"]), dh), name="kv_shared_dx_norm_bwd")
            g_bf = db8[:, 0]

    gsmall = {n: jnp.concatenate(v, axis=0) for n, v in gs.items()}
    gsmall["kv_norm_g"] = g_kvn
    gsmall["final_norm_g"] = dg_final
    gsmall["b_f"] = g_bf
    return loss, dh, gb, gsmall, received


def kernel(x, mem, norm1_g, w_in_a, w_in_b, w_mem_kv, mem_norm_g, w_o, norm2_g, w_mlp1, w_mlp2, kv_norm_g, w_kv_shared, b_f, final_norm_g, loss_target, m_norm1_g, m_w_in_a, m_w_in_b, m_w_mem_kv, m_mem_norm_g, m_w_o, m_norm2_g, m_w_mlp1, m_w_mlp2, m_kv_norm_g, m_w_kv_shared, m_b_f, m_final_norm_g, v_norm1_g, v_w_in_a, v_w_in_b, v_w_mem_kv, v_mem_norm_g, v_w_o, v_norm2_g, v_w_mlp1, v_w_mlp2, v_kv_norm_g, v_w_kv_shared, v_b_f, v_final_norm_g):
    big_w = dict(w_in_a=w_in_a, w_in_b=w_in_b, w_mem_kv=w_mem_kv, w_o=w_o, w_mlp1=w_mlp1, w_mlp2=w_mlp2,
                 w_kv_shared=w_kv_shared)
    small_w = dict(norm1_g=norm1_g, mem_norm_g=mem_norm_g, norm2_g=norm2_g, kv_norm_g=kv_norm_g,
                   final_norm_g=final_norm_g, b_f=b_f)
    big_m = dict(w_in_a=m_w_in_a, w_in_b=m_w_in_b, w_mem_kv=m_w_mem_kv, w_o=m_w_o, w_mlp1=m_w_mlp1,
                 w_mlp2=m_w_mlp2, w_kv_shared=m_w_kv_shared)
    small_m = dict(norm1_g=m_norm1_g, mem_norm_g=m_mem_norm_g, norm2_g=m_norm2_g, kv_norm_g=m_kv_norm_g,
                   final_norm_g=m_final_norm_g, b_f=m_b_f)
    big_v = dict(w_in_a=v_w_in_a, w_in_b=v_w_in_b, w_mem_kv=v_w_mem_kv, w_o=v_w_o, w_mlp1=v_w_mlp1,
                 w_mlp2=v_w_mlp2, w_kv_shared=v_w_kv_shared)
    small_v = dict(norm1_g=v_norm1_g, mem_norm_g=v_mem_norm_g, norm2_g=v_norm2_g, kv_norm_g=v_kv_norm_g,
                   final_norm_g=v_final_norm_g, b_f=v_b_f)

    def pack(k, big, small, dtype):
        return _pack_local(PARTS[k], PART_ROWS[k], big, small if k == 0 else None, dtype)

    def pack_all(big, small):
        return jnp.concatenate([pack(k, big, small, F32) for k in range(len(PARTS))], axis=0)

    wb = {n: {} for n in BIG_NAMES}
    _unpack_gathered(PARTS[0], _allgather_chips(pack(0, big_w, small_w, BF16)), wb)
    shards = [pack(1 + l, big_w, None, BF16) for l in range(N_A)]

    loss, dx, gb, gsmall, received = _local_step(x[0], mem[0], wb, shards, small_w, loss_target[0])

    received = [_scatter_chips(_pack_grads(PARTS[0], PART_ROWS[0], gb, gsmall))] + received
    part = jnp.concatenate([_sum4(r) for r in received], axis=0)
    other = _swap_cores(part)
    g, delta, new_m, new_v = _adamw(part, other, pack_all(big_w, small_w), pack_all(big_m, small_m),
                                    pack_all(big_v, small_v))

    outs = [lax.psum(loss[0, 0], ("x", "y", "c")), dx[None]]
    for packed in (g, delta, new_m, new_v):
        pieces, d, off = {n: [] for n in BIG_NAMES}, {}, 0
        for k, part_k in enumerate(PARTS):
            _unpack_local(part_k, packed[off:off + PART_ROWS[k]], k == 0, pieces, d)
            off += PART_ROWS[k]
        d.update(_join_layers(pieces))
        outs.extend(d[n] for n in WEIGHT_ORDER)
    return tuple(outs)
```

```python
import functools
import math

import jax
import jax.numpy as jnp
from jax import lax
from jax.experimental import pallas as pl
from jax.experimental.pallas import tpu as pltpu

F32 = jnp.float32
BF16 = jnp.bfloat16

D_MODEL = 1024
HEAD_DIM = 64
MIX_WIDTH = 512
MEM_WIDTH = 256
MERGED_WIDTH = MIX_WIDTH + MEM_WIDTH
DEPTH = 4
N_A = 2
D_FF = 4096
EPS = 1e-6
NEG_INF = -1e30
SCALE = 1.0 / math.sqrt(HEAD_DIM)

ADAM_LR = 0.001
ADAM_B1 = 0.9
ADAM_B2 = 0.999
ADAM_EPS = 1e-08
ADAM_WD = 0.01
ADAM_STEP = 10

LANES = 128
GROUP_COLS = MIX_WIDTH // LANES
Q_MEM_COL_A = 3 * GROUP_COLS
Q_MEM_COL_B = GROUP_COLS
W_KV_SHARED = 2 * MIX_WIDTH + 8
KVF_WIDTH = 1152
BQ = 256
BK = 128
DIAG_TILES = BQ // BK
CHAINS = 2
UNDERFLOW_BOUND = -110.0
VMEM_LIMIT = 56 * 1024 * 1024

MESH = pl.DeviceIdType.MESH
N_CHIPS = 4

PARTS = (
    (("w_in_a", 0, 1, (1024, 448), 1),
     ("w_mem_kv", 0, 1, (256, 512), 0)),
    (("w_o", 0, 1, (768, 256), 1),
     ("w_mlp1", 0, 1, (1024, 1024), 1),
     ("w_mlp2", 0, 1, (1024, 1024), 0),
     ("w_in_a", 1, 2, (1024, 448), 1),
     ("w_mem_kv", 1, 2, (256, 512), 0)),
    (("w_o", 1, 4, (768, 256), 1),
     ("w_mlp1", 1, 4, (1024, 1024), 1),
     ("w_mlp2", 1, 4, (1024, 1024), 0),
     ("w_in_b", 0, 2, (256, 768), 0),
     ("w_mem_kv", 2, 4, (256, 512), 0),
     ("w_kv_shared", None, None, (1024, 258), 1)),
)
BIG_NAMES = ("w_in_a", "w_in_b", "w_mem_kv", "w_o", "w_mlp1", "w_mlp2", "w_kv_shared")
SMALL = (
    ("norm1_g", (4, 1024)),
    ("mem_norm_g", (4, 1024)),
    ("norm2_g", (4, 1024)),
    ("kv_norm_g", (1, 1024)),
    ("final_norm_g", (1, 1024)),
    ("b_f", (1, 1024)),
)
WEIGHT_ORDER = ("norm1_g", "w_in_a", "w_in_b", "w_mem_kv", "mem_norm_g", "w_o", "norm2_g", "w_mlp1",
                "w_mlp2", "kv_norm_g", "w_kv_shared", "b_f", "final_norm_g")


ROW_ALIGN = 16
PACK_TILE = 256
SMALL_ROWS = ROW_ALIGN
assert sum(s[0] for _, s in SMALL) <= SMALL_ROWS


def _section_rows(entry):
    _, lo, hi, shape, _ = entry
    rows = (1 if lo is None else hi - lo) * math.prod(shape) // D_MODEL
    return rows, -(-rows // ROW_ALIGN) * ROW_ALIGN


def _round_up(n, m):
    return -(-n // m) * m


SUM_TILE = 128
_used = [sum(_section_rows(e)[1] for e in part) for part in PARTS]
PART_ROWS = [_round_up(_used[0] + SMALL_ROWS, SUM_TILE), _round_up(_used[1], SUM_TILE)]
PART_ROWS.append(_round_up(_used[2] + sum(PART_ROWS), PACK_TILE) - sum(PART_ROWS))
assert PART_ROWS[2] % SUM_TILE == 0


def _params(sem=None):
    return pltpu.CompilerParams(dimension_semantics=sem, vmem_limit_bytes=VMEM_LIMIT)


def _pick(n, cands):
    for c in cands:
        if n % c == 0:
            return c
    raise ValueError(f"no tile for {n}")


def _section(a, entry):
    a = a.reshape(-1, D_MODEL)
    return jnp.pad(a, ((0, _section_rows(entry)[1] - a.shape[0]), (0, 0)))


def _small_block(small, dtype):
    blk = jnp.zeros((SMALL_ROWS, D_MODEL), dtype)
    off = 0
    for n, shp in SMALL:
        a = small[n].astype(dtype)
        if n == "b_f":
            blk = blk.at[off, :a.size].set(a.reshape(-1))
        else:
            blk = blk.at[off:off + shp[0]].set(a.reshape(shp))
        off += shp[0]
    return blk


def _fill(parts, rows, dtype):
    used = sum(p.shape[0] for p in parts)
    return jnp.concatenate(parts + [jnp.zeros((rows - used, D_MODEL), dtype)], axis=0)


def _pack_local(part, rows, big, small, dtype):
    parts = [_section((big[e[0]] if e[1] is None else big[e[0]][e[1]:e[2]]).astype(dtype), e) for e in part]
    if small is not None:
        parts.append(_small_block(small, dtype))
    return _fill(parts, rows, dtype)


def _unpack_local(part, p, with_small, pieces, small):
    off = 0
    for e in part:
        n, lo, hi, shp, _ = e
        rows, reserved = _section_rows(e)
        pieces[n].append((lo, p[off:off + rows].reshape(shp if lo is None else (hi - lo,) + shp)))
        off += reserved
    if with_small:
        for n, shp in SMALL:
            a = p[off:off + shp[0]]
            small[n] = a[0, :8] if n == "b_f" else (a.reshape(D_MODEL) if shp[0] == 1 else a)
            off += shp[0]


def _join_layers(pieces):
    out = {}
    for n, ps in pieces.items():
        ps = sorted(ps, key=lambda t: -1 if t[0] is None else t[0])
        out[n] = ps[0][1] if len(ps) == 1 else jnp.concatenate([a for _, a in ps], axis=0)
    return out


def _unpack_gathered(part, g, weights):
    off = 0
    for e in part:
        n, lo, hi, shp, ax = e
        rows, reserved = _section_rows(e)
        if lo is None:
            sec = g[:, off:off + rows].reshape((N_CHIPS,) + shp)
            weights[n] = jnp.concatenate([sec[j] for j in range(N_CHIPS)], axis=ax)
        else:
            sec = g[:, off:off + rows].reshape((N_CHIPS, hi - lo) + shp)
            for l in range(lo, hi):
                weights[n][l] = jnp.concatenate([sec[j, l - lo] for j in range(N_CHIPS)], axis=ax)
        off += reserved


def _pack_grads(part, rows, gbig, gsmall):
    small = None if gsmall is None else _small_block(gsmall, BF16)
    chunks = []
    for j in range(N_CHIPS):
        parts = []
        for e in part:
            n, lo, hi, shp, ax = e
            w = shp[ax]
            layers = [gbig[n]] if lo is None else [gbig[n][l] for l in range(lo, hi)]
            cut = [lax.slice_in_dim(g, j * w, (j + 1) * w, axis=ax).astype(BF16).reshape(-1, D_MODEL) for g in layers]
            parts.append(_section(cut[0] if len(cut) == 1 else jnp.concatenate(cut, axis=0), e))
        if small is not None:
            parts.append(small)
        chunks.append(_fill(parts, rows, BF16))
    return jnp.stack(chunks, axis=0)


ANY = pl.BlockSpec(memory_space=pl.ANY)


def _other_chips(x, y):
    return [(1 - x, y), (x, 1 - y), (1 - x, 1 - y)]


class _AllGather:
    SCRATCH = [pltpu.SemaphoreType.DMA((3,)), pltpu.SemaphoreType.DMA((3,)), pltpu.SemaphoreType.DMA((3,)),
               pltpu.SemaphoreType.DMA((3,)), pltpu.SemaphoreType.DMA]

    def __init__(self, w_ref, o_ref, send_sems, recv_sems, pass_send, pass_recv, local_sem):
        self.w_ref, self.o_ref = w_ref, o_ref
        self.sems = (send_sems, recv_sems, pass_send, pass_recv, local_sem)
        x, y, c = lax.axis_index("x"), lax.axis_index("y"), lax.axis_index("c")
        half = w_ref.shape[0] // 2
        self.c, self.me, self.sibling = c, 2 * x + y, (x, y, 1 - c)
        self.mine = pl.ds(pl.multiple_of(c * half, ROW_ALIGN), half)
        self.other = pl.ds(pl.multiple_of((1 - c) * half, ROW_ALIGN), half)
        self.chips = _other_chips(x, y)

    def _over_ici(self, j, rows_of):
        chip = self.chips[j]
        return pltpu.make_async_remote_copy(
            src_ref=self.w_ref.at[self.mine], dst_ref=self.o_ref.at[rows_of, self.mine],
            send_sem=self.sems[0].at[j], recv_sem=self.sems[1].at[j],
            device_id=(chip[0], chip[1], self.c), device_id_type=MESH)

    def _over_d2d(self, j, rows):
        where = self.o_ref.at[2 * self.chips[j][0] + self.chips[j][1], rows]
        return pltpu.make_async_remote_copy(src_ref=where, dst_ref=where, send_sem=self.sems[2].at[j],
                                            recv_sem=self.sems[3].at[j], device_id=self.sibling,
                                            device_id_type=MESH)

    def _local(self):
        return pltpu.make_async_copy(self.w_ref, self.o_ref.at[self.me], self.sems[4])

    def start(self):
        self._local().start()
        for j in range(3):
            self._over_ici(j, self.me).start()

    def finish(self):
        for j in range(3):
            self._over_ici(j, 2 * self.chips[j][0] + self.chips[j][1]).wait_recv()
            self._over_d2d(j, self.mine).start()
        for j in range(3):
            self._over_d2d(j, self.other).wait_recv()
        for j in range(3):
            self._over_ici(j, self.me).wait_send()
            self._over_d2d(j, self.mine).wait_send()
        self._local().wait()


class _Scatter:
    SCRATCH = [pltpu.SemaphoreType.DMA((3,)), pltpu.SemaphoreType.DMA((3,)), pltpu.SemaphoreType.DMA]

    def __init__(self, g_ref, o_ref, send_sems, recv_sems, local_sem):
        self.g_ref, self.o_ref, self.sems = g_ref, o_ref, (send_sems, recv_sems, local_sem)
        x, y, c = lax.axis_index("x"), lax.axis_index("y"), lax.axis_index("c")
        self.c, self.me, self.chips = c, 2 * x + y, _other_chips(x, y)

    def _copy(self, j):
        chip = self.chips[j]
        return pltpu.make_async_remote_copy(
            src_ref=self.g_ref.at[2 * chip[0] + chip[1]], dst_ref=self.o_ref.at[self.me],
            send_sem=self.sems[0].at[j], recv_sem=self.sems[1].at[j],
            device_id=(chip[0], chip[1], self.c), device_id_type=MESH)

    def _local(self):
        return pltpu.make_async_copy(self.g_ref.at[self.me], self.o_ref.at[self.me], self.sems[2])

    def start(self):
        self._local().start()
        for j in range(3):
            self._copy(j).start()

    def finish(self):
        for j in range(3):
            self._copy(j).wait()
        self._local().wait()


def _allgather_chips(w):
    def body(w_ref, o_ref, *sems):
        ag = _AllGather(w_ref, o_ref, *sems)
        ag.start()
        ag.finish()

    return pl.pallas_call(
        body, name="allgather_weights",
        out_shape=jax.ShapeDtypeStruct((N_CHIPS,) + w.shape, w.dtype),
        in_specs=[ANY], out_specs=ANY, scratch_shapes=_AllGather.SCRATCH,
    )(w)


def _scatter_chips(g4):
    def body(g_ref, o_ref, *sems):
        sc = _Scatter(g_ref, o_ref, *sems)
        sc.start()
        sc.finish()

    return pl.pallas_call(
        body, name="scatter_grads",
        out_shape=jax.ShapeDtypeStruct(g4.shape, g4.dtype),
        in_specs=[ANY], out_specs=ANY, scratch_shapes=_Scatter.SCRATCH,
    )(g4)


def _swap_cores(p):
    def body(p_ref, o_ref, send_sem, recv_sem):
        x, y, c = lax.axis_index("x"), lax.axis_index("y"), lax.axis_index("c")
        cp = pltpu.make_async_remote_copy(src_ref=p_ref, dst_ref=o_ref, send_sem=send_sem, recv_sem=recv_sem,
                                          device_id=(x, y, 1 - c), device_id_type=MESH)
        cp.start()
        cp.wait()

    return pl.pallas_call(
        body, name="swap_cores",
        out_shape=jax.ShapeDtypeStruct(p.shape, p.dtype),
        in_specs=[ANY], out_specs=ANY,
        scratch_shapes=[pltpu.SemaphoreType.DMA, pltpu.SemaphoreType.DMA],
    )(p)


def _sum4(r4):
    _, R, C = r4.shape

    def body(r_ref, o_ref):
        o_ref[...] = ((r_ref[0].astype(F32) + r_ref[1].astype(F32)) + r_ref[2].astype(F32)) + r_ref[3].astype(F32)

    return pl.pallas_call(
        body, name="sum_chips", grid=(R // SUM_TILE,),
        in_specs=[pl.BlockSpec((N_CHIPS, SUM_TILE, C), lambda i: (0, i, 0))],
        out_specs=pl.BlockSpec((SUM_TILE, C), lambda i: (i, 0)),
        out_shape=jax.ShapeDtypeStruct((R, C), F32),
        compiler_params=_params(("parallel",)),
    )(r4)


def _adamw(pa, pb, w, m, v):
    R, C = w.shape
    c1 = 1.0 - ADAM_B1
    c2 = 1.0 - ADAM_B2
    bc1 = 1.0 - ADAM_B1 ** ADAM_STEP
    bc2 = 1.0 - ADAM_B2 ** ADAM_STEP

    def body(pa_ref, pb_ref, w_ref, m_ref, v_ref, g_ref, d_ref, mo_ref, vo_ref):
        g = pa_ref[...] + pb_ref[...]
        mn = ADAM_B1 * m_ref[...] + c1 * g
        vn = ADAM_B2 * v_ref[...] + c2 * (g * g)
        m_hat = mn / bc1
        v_hat = vn / bc2
        g_ref[...] = g
        d_ref[...] = -ADAM_LR * (m_hat / (jnp.sqrt(v_hat) + ADAM_EPS) + ADAM_WD * w_ref[...])
        mo_ref[...] = mn
        vo_ref[...] = vn

    spec = pl.BlockSpec((PACK_TILE, C), lambda i: (i, 0))
    shp = jax.ShapeDtypeStruct((R, C), F32)
    return pl.pallas_call(
        body, name="adamw", grid=(R // PACK_TILE,),
        in_specs=[spec] * 5, out_specs=[spec] * 4, out_shape=[shp] * 4,
        compiler_params=_params(("parallel",)),
    )(pa, pb, w, m, v)


def _rms_fwd(x, g, name):
    R, Dm = x.shape
    tr = _pick(R, (512, 256, 128))

    def body(x_ref, g_ref, o_ref):
        xf = x_ref[...]
        r = lax.rsqrt(jnp.mean(xf * xf, axis=-1, keepdims=True) + EPS)
        o_ref[...] = (xf * r * g_ref[...]).astype(o_ref.dtype)

    return pl.pallas_call(
        body, name=name, grid=(R // tr,),
        in_specs=[pl.BlockSpec((tr, Dm), lambda i: (i, 0)), pl.BlockSpec((1, Dm), lambda i: (0, 0))],
        out_specs=pl.BlockSpec((tr, Dm), lambda i: (i, 0)),
        out_shape=jax.ShapeDtypeStruct((R, Dm), BF16),
        compiler_params=_params(("parallel",)),
    )(x, g)


def _rms_gain_grad(x, dy, name):
    R, Dm = x.shape
    tr = _pick(R, (256, 128))

    def body(x_ref, dy_ref, dg_ref):
        xf = x_ref[...]
        r = lax.rsqrt(jnp.mean(xf * xf, axis=-1, keepdims=True) + EPS)

        @pl.when(pl.program_id(0) == 0)
        def _():
            dg_ref[...] = jnp.zeros_like(dg_ref)

        dg_ref[...] += jnp.sum(dy_ref[...] * (xf * r), axis=0, keepdims=True)

    row = pl.BlockSpec((tr, Dm), lambda i: (i, 0))
    return pl.pallas_call(
        body, name=name, grid=(R // tr,),
        in_specs=[row, row], out_specs=pl.BlockSpec((1, Dm), lambda i: (0, 0)),
        out_shape=jax.ShapeDtypeStruct((1, Dm), F32),
        compiler_params=_params(("arbitrary",)),
    )(x, dy)


def _final_loss(x, g, tgt):
    R, Dm = x.shape
    tr = _pick(R, (256, 128))

    def body(x_ref, g_ref, t_ref, l_ref, dx_ref, dxb_ref, dg_ref):
        xf = x_ref[...]
        gv = g_ref[...]
        r = lax.rsqrt(jnp.mean(xf * xf, axis=-1, keepdims=True) + EPS)
        xr = xf * r
        err = xr * gv - t_ref[...]
        dy_ = err * (1.0 / Dm)
        gdy = dy_ * gv
        mdot = jnp.mean(xf * gdy, axis=-1, keepdims=True)
        dx = r * gdy - xf * ((r * r * r) * mdot)
        dx_ref[...] = dx
        dxb_ref[...] = dx.astype(BF16)

        @pl.when(pl.program_id(0) == 0)
        def _():
            dg_ref[...] = jnp.zeros_like(dg_ref)
            l_ref[...] = jnp.zeros_like(l_ref)

        dg_ref[...] += jnp.sum(dy_ * xr, axis=0, keepdims=True)
        sq = jnp.sum(err * err, axis=1, keepdims=True)
        l_ref[...] += jnp.sum(sq, axis=0, keepdims=True) * (0.5 / Dm)

    row = pl.BlockSpec((tr, Dm), lambda i: (i, 0))
    vec = pl.BlockSpec((1, Dm), lambda i: (0, 0))
    return pl.pallas_call(
        body, name="final_norm_loss", grid=(R // tr,),
        in_specs=[row, vec, row],
        out_specs=[pl.BlockSpec((1, 1), lambda i: (0, 0)), row, row, vec],
        out_shape=[jax.ShapeDtypeStruct((1, 1), F32), jax.ShapeDtypeStruct((R, Dm), F32),
                   jax.ShapeDtypeStruct((R, Dm), BF16), jax.ShapeDtypeStruct((1, Dm), F32)],
        compiler_params=_params(("arbitrary",)),
    )(x, g, tgt)


MAX_TK = 2048

_DIMS = {"nn": (((1,), (0,)), ((), ())), "nt": (((1,), (1,)), ((), ())), "tn": (((0,), (0,)), ((), ()))}


def _mm(a, b, *, mode="nn", out_dtype=BF16, epi=None, extra=None, name):
    if mode == "nn":
        (M, K), N = a.shape, b.shape[1]
    elif mode == "nt":
        (M, K), N = a.shape, b.shape[0]
    else:
        (K, M), N = a.shape, b.shape[1]
    deep = K > MAX_TK
    tm = _pick(M, (512, 256, 128) if (epi == "rms_bwd" or deep) else (1024, 768, 512, 256, 128))
    wide = (2048,) if (K <= MAX_TK // 2 and mode != "tn") else ()
    tn = _pick(N, wide + (1024, 896, 768, 640, 512, 384, 256, 128))
    tk = K if K <= (2 * MAX_TK if deep else MAX_TK) else _pick(K, (2 * MAX_TK, MAX_TK, 1024, 512, 256, 128))
    nk = K // tk
    extras = () if extra is None else (extra if isinstance(extra, tuple) else (extra,))
    n_out = {"relu2": 2, "add_rms": 2, "rms_bwd": 3}.get(epi, 1)
    assert epi not in ("rms_bwd", "add_rms") or tn == N

    def body(*refs):
        a_ref, b_ref = refs[:2]
        e_refs = refs[2:2 + len(extras)]
        e_ref = e_refs[0] if e_refs else None
        outs = refs[2 + len(extras):2 + len(extras) + n_out]
        k = pl.program_id(2)
        part = lax.dot_general(a_ref[...].astype(BF16), b_ref[...].astype(BF16), _DIMS[mode],
                               preferred_element_type=F32)

        def finish(acc):
            if epi is None:
                outs[0][...] = acc.astype(outs[0].dtype)
            elif epi == "add":
                outs[0][...] = (e_ref[...] + acc).astype(outs[0].dtype)
            elif epi == "add_rms":
                y = e_refs[0][...] + acc
                outs[0][...] = y
                r = lax.rsqrt(jnp.mean(y * y, axis=-1, keepdims=True) + EPS)
                outs[1][...] = (y * r * e_refs[1][...]).astype(BF16)
            elif epi == "relu2":
                outs[0][...] = acc.astype(BF16)
                rl = jnp.maximum(acc, 0.0)
                outs[1][...] = (rl * rl).astype(BF16)
            elif epi == "drelu2":
                u = e_ref[...].astype(F32)
                outs[0][...] = (acc * (2.0 * jnp.maximum(u, 0.0))).astype(outs[0].dtype)
            elif epi == "rms_bwd":
                x_ref, g_ref, dres_ref = e_refs
                xf = x_ref[...]
                r = lax.rsqrt(jnp.mean(xf * xf, axis=-1, keepdims=True) + EPS)
                gdy = acc * g_ref[...]
                mdot = jnp.mean(xf * gdy, axis=-1, keepdims=True)
                dx = dres_ref[...] + (r * gdy - xf * ((r * r * r) * mdot))
                outs[0][...] = dx
                outs[1][...] = dx.astype(BF16)

                @pl.when(pl.program_id(0) == 0)
                def _():
                    outs[2][...] = jnp.zeros_like(outs[2])

                outs[2][...] += jnp.sum(acc * (xf * r), axis=0, keepdims=True)

        if nk == 1:
            finish(part)
        else:
            acc_ref = refs[-1]

            @pl.when(k == 0)
            def _():
                acc_ref[...] = part

            @pl.when(jnp.logical_and(k > 0, k < nk - 1))
            def _():
                acc_ref[...] += part

            @pl.when(k == nk - 1)
            def _():
                finish(acc_ref[...] + part)

    if mode == "tn":
        a_spec = pl.BlockSpec((tk, tm), lambda i, j, k: (k, i))
    else:
        a_spec = pl.BlockSpec((tm, tk), lambda i, j, k: (i, k))
    if mode == "nt":
        b_spec = pl.BlockSpec((tn, tk), lambda i, j, k: (j, k))
    else:
        b_spec = pl.BlockSpec((tk, tn), lambda i, j, k: (k, j))
    o_spec = pl.BlockSpec((tm, tn), lambda i, j, k: (i, j))
    vec_spec = pl.BlockSpec((1, tn), lambda i, j, k: (0, j))
    ins, in_specs = [a, b] + list(extras), [a_spec, b_spec]
    if epi == "rms_bwd":
        in_specs += [o_spec, vec_spec, o_spec]
        out_shape = [jax.ShapeDtypeStruct((M, N), F32), jax.ShapeDtypeStruct((M, N), BF16),
                     jax.ShapeDtypeStruct((1, N), F32)]
        out_specs = [o_spec, o_spec, vec_spec]
    elif epi == "add_rms":
        in_specs += [o_spec, vec_spec]
        out_shape = [jax.ShapeDtypeStruct((M, N), F32), jax.ShapeDtypeStruct((M, N), BF16)]
        out_specs = [o_spec, o_spec]
    else:
        in_specs += [o_spec] * len(extras)
        out_shape = [jax.ShapeDtypeStruct((M, N), BF16 if epi == "relu2" else out_dtype)] * n_out
        out_specs = [o_spec] * n_out
    res = pl.pallas_call(
        body, name=name, grid=(M // tm, N // tn, nk),
        in_specs=in_specs, out_specs=out_specs, out_shape=out_shape,
        scratch_shapes=[pltpu.VMEM((tm, tn), F32)] if nk > 1 else [],
        compiler_params=_params(("arbitrary",) * 3 if epi == "rms_bwd" else ("parallel", "parallel", "arbitrary")),
    )(*ins)
    return res if n_out > 1 else res[0]


def _dot(a, b):
    return lax.dot_general(a, b, _DIMS["nn"], preferred_element_type=F32)


def _dot_nt(a, b):
    return lax.dot_general(a, b, _DIMS["nt"], preferred_element_type=F32)


def _dot_tn(a, b):
    return lax.dot_general(a, b, _DIMS["tn"], preferred_element_type=F32)


def _split_dot(x, t):
    hi = x.astype(BF16)
    lo = (x - hi.astype(F32)).astype(BF16)
    return _dot(jnp.concatenate([hi, lo], axis=1), jnp.concatenate([t, t], axis=0))


def _head_pair(ref, scale=None):
    xf = ref[...].astype(F32)
    if scale is not None:
        xf = xf * scale
    is_a = lax.broadcasted_iota(jnp.int32, xf.shape, 1) < HEAD_DIM
    return jnp.where(is_a, xf, 0.0).astype(BF16), jnp.where(is_a, 0.0, xf).astype(BF16)


def _stack(a, b):
    return jnp.concatenate([a, b], axis=0)


def _head_rows(ref, scale=None):
    return _stack(*_head_pair(ref, scale))


def _unstack_heads(x):
    rows = x.shape[0] // 2
    return _select_pair(x[:rows], x[rows:])


def _pair_rowsum(x):
    is_a = lax.broadcasted_iota(jnp.int32, x.shape, 1) < HEAD_DIM
    return (jnp.sum(jnp.where(is_a, x, 0.0), axis=1, keepdims=True),
            jnp.sum(jnp.where(is_a, 0.0, x), axis=1, keepdims=True))


def _select_pair(xa, xb):
    is_a = lax.broadcasted_iota(jnp.int32, xa.shape, 1) < HEAD_DIM
    return jnp.where(is_a, xa, xb)


def _two_cols(xa, xb):
    rows = xa.shape[0]
    first = lax.broadcasted_iota(jnp.int32, (rows, 2), 1) == 0
    return jnp.where(first, xa, xb)


def _softplus_parts(z):
    e = jnp.exp(-jnp.abs(z))
    return jnp.maximum(z, 0.0) + jnp.log(1.0 + e), e


def _tile_iotas():
    row = lax.broadcasted_iota(jnp.int32, (BK, BK), 0)
    col = lax.broadcasted_iota(jnp.int32, (BK, BK), 1)
    return row, col


def _stacked_iotas(bq, nk):
    row = lax.broadcasted_iota(jnp.int32, (2 * bq, nk), 0) & (bq - 1)
    col = lax.broadcasted_iota(jnp.int32, (2 * bq, nk), 1)
    return row, col


def _side_exchange(exchange, operand, n_in, n_out):
    out_shape = jax.ShapeDtypeStruct(((N_CHIPS,) + operand.shape) if exchange is _AllGather else operand.shape,
                                     operand.dtype)
    n_sem = len(exchange.SCRATCH)

    def pick(refs):
        def make():
            return exchange(refs[n_in], refs[n_in + 1 + n_out], *refs[len(refs) - n_sem:])

        return (lambda: make().start()), (lambda: make().finish())

    return [operand], [ANY], [out_shape], [ANY], pick


def _sb_fwd(proj, name, exchange, operand):
    S = proj.shape[0]
    nqb = S // (CHAINS * BQ)
    x_in, x_in_specs, x_out, x_out_specs, pick = _side_exchange(exchange, operand, 3, 1)

    def body(*refs):
        q_ref, k_ref, v_ref = refs[:3]
        o_ref = refs[3 + len(x_in)]
        acc_ref = refs[3 + len(x_in) + 1 + len(x_out)]
        start, finish = pick(refs)
        p = pl.program_id(0)
        i = pl.program_id(1)

        @pl.when(jnp.logical_and(p == 0, i == 0))
        def _():
            start()

        q2 = [_head_rows(q_ref.at[pl.ds(ch * BQ, BQ)], SCALE) for ch in range(CHAINS)]
        row, col = _tile_iotas()
        tri = (row > col).astype(BF16)
        srow, scol = _stacked_iotas(BQ, BK)
        acc_ref[...] = jnp.zeros_like(acc_ref)

        def tile(ch, kb, c, dmask=None, valid=None):
            r0 = pl.multiple_of(kb * BK, BK)
            kblk = k_ref[pl.ds(r0, BK), :]
            vblk = v_ref[pl.ds(r0, BK), :]
            z = _dot_nt(q2[ch], kblk)
            sp, _ = _softplus_parts(z)
            lm = -sp
            if dmask is not None:
                lm = jnp.where(dmask, lm, 0.0)
            btw = _split_dot(lm, tri)
            w = jnp.exp((z - sp) + btw + c)
            if dmask is not None:
                w = jnp.where(dmask, w, 0.0)
            if valid is not None:
                w = w * valid
            acc_ref[ch] += _dot(w.astype(BF16), vblk)
            return c + btw[:, 0:1] + lm[:, 0:1]

        def alive(c):
            return jnp.max(c) > UNDERFLOW_BOUND

        cs = [jnp.zeros((2 * BQ, 1), F32)] * CHAINS
        for d in reversed(range(DIAG_TILES)):
            cs = [tile(ch, (CHAINS * i + ch) * DIAG_TILES + d, cs[ch], dmask=scol < srow - d * BK)
                  for ch in range(CHAINS)]

        def tile_of(ch, t):
            return (CHAINS * i + ch) * DIAG_TILES - 1 - t

        def more(cs, t):
            go = [jnp.logical_and(alive(cs[ch]), tile_of(ch, t) >= 0) for ch in range(CHAINS)]
            return functools.reduce(jnp.logical_or, go).astype(jnp.int32)

        def step(st):
            t, _, cs = st
            new = []
            for ch in range(CHAINS):
                kb = tile_of(ch, t)
                if ch == CHAINS - 1:
                    new.append(tile(ch, kb, cs[ch]))
                else:
                    new.append(tile(ch, jnp.maximum(kb, 0), cs[ch], valid=(kb >= 0).astype(F32)))
            return t + 1, more(new, t + 1), new

        lax.while_loop(lambda st: st[1] > 0, step, (0, more(cs, 0), cs))
        for ch in range(CHAINS):
            o_ref[pl.ds(ch * BQ, BQ), :] = _unstack_heads(acc_ref[ch])

        @pl.when(jnp.logical_and(p == 3, i == nqb - 1))
        def _():
            finish()

    blk = pl.BlockSpec((CHAINS * BQ, LANES), lambda p, i: (i, p))
    res = pl.pallas_call(
        body, name=name, grid=(4, nqb),
        in_specs=[blk, pl.BlockSpec((S, LANES), lambda p, i: (0, GROUP_COLS + p)),
                  pl.BlockSpec((S, LANES), lambda p, i: (0, 2 * GROUP_COLS + p))] + x_in_specs,
        out_specs=[blk] + x_out_specs,
        out_shape=[jax.ShapeDtypeStruct((S, MERGED_WIDTH), F32)] + x_out,
        scratch_shapes=[pltpu.VMEM((CHAINS, 2 * BQ, LANES), F32)] + exchange.SCRATCH,
        compiler_params=_params(("arbitrary", "arbitrary")),
    )(proj, proj, proj, *x_in)
    return res


def _sb_bwd(proj, merged, dmerged, name, exchange, operand):
    S = proj.shape[0]
    nqb = S // (CHAINS * BQ)
    x_in, x_in_specs, x_out, x_out_specs, pick = _side_exchange(exchange, operand, 5, 3)

    def body(*refs):
        q_ref, k_ref, v_ref, o_ref, do_ref = refs[:5]
        dq_ref, dk_hbm, dv_hbm = refs[5 + len(x_in):8 + len(x_in)]
        dq_acc, dk_acc, dv_acc, sem = refs[8 + len(x_in) + len(x_out):12 + len(x_in) + len(x_out)]
        start, finish = pick(refs)
        p = pl.program_id(0)
        i = pl.program_id(1)

        @pl.when(jnp.logical_and(p == 0, i == 0))
        def _():
            start()

        @pl.when(i == 0)
        def _():
            dk_acc[...] = jnp.zeros_like(dk_acc)
            dv_acc[...] = jnp.zeros_like(dv_acc)

        rows = [pl.ds(ch * BQ, BQ) for ch in range(CHAINS)]
        q2 = [_head_rows(q_ref.at[rw], SCALE) for rw in rows]
        do2 = [_head_rows(do_ref.at[rw]) for rw in rows]
        tot = [_stack(*_pair_rowsum(do_ref[rw, :].astype(F32) * o_ref[rw, :])) for rw in rows]
        row, col = _tile_iotas()
        tri_gt = (row > col).astype(BF16)
        tri_ge = (row >= col).astype(BF16)
        srow, scol = _stacked_iotas(BQ, BK)
        dq_acc[...] = jnp.zeros_like(dq_acc)

        def tile(ch, kb, st, dmask=None, valid=None):
            masked = dmask is not None
            c, r = st
            r0 = pl.multiple_of(kb * BK, BK)
            kblk = k_ref[pl.ds(r0, BK), :]
            vblk = v_ref[pl.ds(r0, BK), :]
            z = _dot_nt(q2[ch], kblk)
            sp, e = _softplus_parts(z)
            lm = -sp
            if masked:
                lm = jnp.where(dmask, lm, 0.0)
            btw = _split_dot(lm, tri_gt)
            w = jnp.exp((z - sp) + btw + c)
            if masked:
                w = jnp.where(dmask, w, 0.0)
            if valid is not None:
                w = w * valid
            wb = w.astype(BF16)
            a = wb.astype(F32) * _dot_nt(do2[ch], vblk)
            suffix = _split_dot(a, tri_ge) + r
            rcp = 1.0 / (1.0 + e)
            pos = z >= 0.0
            sig = jnp.where(pos, rcp, e * rcp)
            sig_neg = jnp.where(pos, e * rcp, rcp)
            dz = a * sig_neg - (tot[ch] - suffix) * sig
            if masked:
                dz = jnp.where(dmask, dz, 0.0)
            if valid is not None:
                dz = dz * valid
            dzb = dz.astype(BF16)
            dq_acc[ch] += _dot(dzb, kblk)
            dk_acc[pl.ds(r0, BK), :] += _dot_tn(dzb, q2[ch])
            dv_acc[pl.ds(r0, BK), :] += _dot_tn(wb, do2[ch])
            return c + btw[:, 0:1] + lm[:, 0:1], suffix[:, 0:1]

        def alive(st):
            return jnp.max(st[0]) > UNDERFLOW_BOUND

        zero = jnp.zeros((2 * BQ, 1), F32)
        sts = [(zero, zero)] * CHAINS
        for d in reversed(range(DIAG_TILES)):
            sts = [tile(ch, (CHAINS * i + ch) * DIAG_TILES + d, sts[ch], dmask=scol < srow - d * BK)
                   for ch in range(CHAINS)]

        def tile_of(ch, t):
            return (CHAINS * i + ch) * DIAG_TILES - 1 - t

        def more(sts, t):
            go = [jnp.logical_and(alive(sts[ch]), tile_of(ch, t) >= 0) for ch in range(CHAINS)]
            return functools.reduce(jnp.logical_or, go).astype(jnp.int32)

        def step(s):
            t, _, sts = s
            new = []
            for ch in range(CHAINS):
                kb = tile_of(ch, t)
                if ch == CHAINS - 1:
                    new.append(tile(ch, kb, sts[ch]))
                else:
                    new.append(tile(ch, jnp.maximum(kb, 0), sts[ch], valid=(kb >= 0).astype(F32)))
            return t + 1, more(new, t + 1), new

        lax.while_loop(lambda s: s[1] > 0, step, (0, more(sts, 0), sts))
        for ch in range(CHAINS):
            dq_ref[rows[ch], :] = (_unstack_heads(dq_acc[ch]) * SCALE).astype(dq_ref.dtype)

        @pl.when(i == nqb - 1)
        def _():
            ck = pltpu.make_async_copy(dk_acc, dk_hbm.at[p], sem.at[0])
            cv = pltpu.make_async_copy(dv_acc, dv_hbm.at[p], sem.at[1])
            ck.start()
            cv.start()
            ck.wait()
            cv.wait()

        @pl.when(jnp.logical_and(p == 3, i == nqb - 1))
        def _():
            finish()

    blk = lambda off: pl.BlockSpec((CHAINS * BQ, LANES), lambda p, i: (i, off + p))
    slab = lambda off: pl.BlockSpec((S, LANES), lambda p, i: (0, off + p))
    return pl.pallas_call(
        body, name=name, grid=(4, nqb),
        in_specs=[blk(0), slab(GROUP_COLS), slab(2 * GROUP_COLS), blk(0), blk(0)] + x_in_specs,
        out_specs=[blk(0), ANY, ANY] + x_out_specs,
        out_shape=[jax.ShapeDtypeStruct((S, MIX_WIDTH), BF16),
                   jax.ShapeDtypeStruct((4, S, LANES), F32), jax.ShapeDtypeStruct((4, S, LANES), F32)] + x_out,
        scratch_shapes=[pltpu.VMEM((CHAINS, 2 * BQ, LANES), F32), pltpu.VMEM((S, LANES), F32),
                        pltpu.VMEM((S, LANES), F32), pltpu.SemaphoreType.DMA((2,))]
        + exchange.SCRATCH,
        compiler_params=_params(("arbitrary", "arbitrary")),
    )(proj, proj, proj, merged, dmerged, *x_in)


def _key_norm_max(k_ref, knorm_ref, nkb):
    def step(kb, m):
        r0 = pl.multiple_of(kb * BK, BK)
        blk = k_ref[pl.ds(r0, BK), :].astype(F32)
        sa, sb = _pair_rowsum(blk * blk)
        return (jnp.maximum(m[0], jnp.max(sa, axis=0, keepdims=True)),
                jnp.maximum(m[1], jnp.max(sb, axis=0, keepdims=True)))

    zero = jnp.zeros((1, 1), F32)
    ma, mb = lax.fori_loop(0, nkb, step, (zero, zero))
    knorm_ref[...] = _select_pair(jnp.broadcast_to(ma, (1, LANES)), jnp.broadcast_to(mb, (1, LANES)))


FQ = 512
FK = FQ
GATE_BLOCKS = FK // BK


def _key_gates(cr_ref, kb):
    blocks = [cr_ref[0, GATE_BLOCKS * kb + j] for j in range(GATE_BLOCKS)]
    per_head = [jnp.broadcast_to(jnp.concatenate([b[h:h + 1] for b in blocks], axis=1), (FQ, FK)) for h in range(2)]
    return _stack(*per_head)


def _last_gate(cr_ref, kb):
    last = cr_ref[0, GATE_BLOCKS * jnp.maximum(kb, 0) + GATE_BLOCKS - 1]
    return _stack(*[jnp.broadcast_to(last[h:h + 1, BK - 1:BK], (FQ, 1)) for h in range(2)])


def _logit_bound(q_ref, knorm_ref):
    qf = q_ref[...].astype(F32) * SCALE
    qa, qb = _pair_rowsum(qf * qf)
    kn = knorm_ref[...]
    return _stack(jnp.sqrt(qa * kn[:, 0:1]), jnp.sqrt(qb * kn[:, HEAD_DIM:HEAD_DIM + 1]))


def _causal_bias(bias_ref):
    srow, scol = _stacked_iotas(FQ, FK)
    bias_ref[...] = jnp.where(scol <= srow, 0.0, NEG_INF)


def _fox_fwd(proj, kv, c_col, c_row, name):
    S = proj.shape[0]
    nqb = S // FQ

    def body(q_ref, k_ref, v_ref, cc_ref, cr_ref, o_ref, lse_ref, acc_ref, knorm_ref, bias_ref):
        i = pl.program_id(1)

        @pl.when(i == 0)
        def _():
            _key_norm_max(k_ref, knorm_ref, S // BK)
            _causal_bias(bias_ref)

        q2 = _head_rows(q_ref, SCALE)
        bound = _logit_bound(q_ref, knorm_ref)
        cc = cc_ref[0]
        ct = _stack(cc[:, 0:1], cc[:, 1:2])
        acc_ref[...] = jnp.zeros_like(acc_ref)

        def tile(kb, st, on_diagonal):
            m, l = st
            r0 = pl.multiple_of(kb * FK, FK)
            kblk = k_ref[pl.ds(r0, FK), :]
            vblk = v_ref[pl.ds(r0, FK), :]
            z = _dot_nt(q2, kblk) + ct - _key_gates(cr_ref, kb) + bias_ref[...] * on_diagonal
            m_new = jnp.maximum(m, jnp.max(z, axis=1, keepdims=True))
            alpha = jnp.exp(m - m_new)
            pr = jnp.exp(z - m_new)
            acc_ref[...] = alpha * acc_ref[...] + _split_dot(pr, vblk)
            return m_new, alpha * l + jnp.sum(pr, axis=1, keepdims=True)

        def alive(kb, st):
            reach = bound + ct - _last_gate(cr_ref, kb) - st[0]
            return (jnp.max(reach) > UNDERFLOW_BOUND).astype(jnp.int32)

        neg = jnp.full((2 * FQ, 1), NEG_INF, F32)
        zero = jnp.zeros((2 * FQ, 1), F32)
        def cond(s):
            return jnp.logical_and(s[0] >= 0, s[1] > 0)

        def step(s):
            kb, _, st = s
            st = tile(kb, st, (kb == i).astype(F32))
            return kb - 1, alive(kb - 1, st), st

        _, _, (m, l) = lax.while_loop(cond, step, (i, jnp.int32(1), (neg, zero)))
        o_ref[...] = _unstack_heads(acc_ref[...] / l)
        lse = m + jnp.log(l)
        lse_ref[0] = _two_cols(lse[:FQ], lse[FQ:])

    return pl.pallas_call(
        body, name=name, grid=(4, nqb),
        in_specs=[pl.BlockSpec((FQ, LANES), lambda p, i: (i, p)),
                  pl.BlockSpec((S, LANES), lambda p, i: (0, p)),
                  pl.BlockSpec((S, LANES), lambda p, i: (0, GROUP_COLS + p)),
                  pl.BlockSpec((1, FQ, 2), lambda p, i: (p, i, 0)),
                  pl.BlockSpec((1, S // BK, 8, LANES), lambda p, i: (p, 0, 0, 0))],
        out_specs=[pl.BlockSpec((FQ, LANES), lambda p, i: (i, p)),
                   pl.BlockSpec((1, FQ, 2), lambda p, i: (p, i, 0))],
        out_shape=[jax.ShapeDtypeStruct((S, MERGED_WIDTH), F32), jax.ShapeDtypeStruct((4, S, 2), F32)],
        scratch_shapes=[pltpu.VMEM((2 * FQ, LANES), F32), pltpu.VMEM((1, LANES), F32),
                        pltpu.VMEM((2 * FQ, FK), F32)],
        compiler_params=_params(("arbitrary", "arbitrary")),
    )(proj, kv, kv, c_col, c_row)


def _fox_bwd(proj, kv, c_col, c_row, lse, merged, dmerged, dk_prev, dv_prev, dc_prev, name):
    S = proj.shape[0]
    nqb = S // FQ

    def body(q_ref, k_ref, v_ref, cc_ref, cr_ref, lse_ref, o_ref, do_ref, dkp_hbm, dvp_hbm, dcp_ref,
             dq_ref, dk_hbm, dv_hbm, dc_ref, dq_acc, dk_acc, dv_acc, knorm_ref, bias_ref, sem):
        p = pl.program_id(0)
        i = pl.program_id(1)

        @pl.when(i == 0)
        def _():
            ck = pltpu.make_async_copy(dkp_hbm.at[p], dk_acc, sem.at[0])
            cv = pltpu.make_async_copy(dvp_hbm.at[p], dv_acc, sem.at[1])
            ck.start()
            cv.start()
            dc_ref[...] = dcp_ref[...]
            _key_norm_max(k_ref, knorm_ref, S // BK)
            _causal_bias(bias_ref)
            ck.wait()
            cv.wait()

        q2 = _head_rows(q_ref, SCALE)
        do2 = _head_rows(do_ref)
        tot = _stack(*_pair_rowsum(do_ref[...].astype(F32) * o_ref[...]))
        bound = _logit_bound(q_ref, knorm_ref)
        cc = cc_ref[0]
        ct = _stack(cc[:, 0:1], cc[:, 1:2])
        ls = lse_ref[0]
        lse = _stack(ls[:, 0:1], ls[:, 1:2])
        sub = lax.broadcasted_iota(jnp.int32, (8, LANES), 0)
        dq_acc[...] = jnp.zeros_like(dq_acc)

        def tile(kb, masked):
            r0 = pl.multiple_of(kb * FK, FK)
            kblk = k_ref[pl.ds(r0, FK), :]
            vblk = v_ref[pl.ds(r0, FK), :]
            z = _dot_nt(q2, kblk) + ct - _key_gates(cr_ref, kb)
            if masked:
                z = z + bias_ref[...]
            pr = jnp.exp(z - lse)
            ds = pr * (_dot_nt(do2, vblk) - tot)
            dsb = ds.astype(BF16)
            dq_acc[...] += _dot(dsb, kblk)
            dk_acc[pl.ds(r0, FK), :] += _dot_tn(dsb, q2)
            dv_acc[pl.ds(r0, FK), :] += _dot_tn(pr.astype(BF16), do2)
            dca = jnp.sum(ds[:FQ], axis=0, keepdims=True)
            dcb = jnp.sum(ds[FQ:], axis=0, keepdims=True)
            for j in range(GATE_BLOCKS):
                cols = slice(j * BK, (j + 1) * BK)
                old = dc_ref[0, GATE_BLOCKS * kb + j]
                dc_ref[0, GATE_BLOCKS * kb + j] = jnp.where(sub == 0, old - dca[:, cols],
                                                            jnp.where(sub == 1, old - dcb[:, cols], old))

        def alive(kb):
            reach = bound + ct - _last_gate(cr_ref, kb) - lse
            return (jnp.max(reach) > UNDERFLOW_BOUND).astype(jnp.int32)

        tile(i, True)

        def cond(s):
            return jnp.logical_and(s[0] >= 0, s[1] > 0)

        def step(s):
            kb, _ = s
            tile(kb, False)
            return kb - 1, alive(kb - 1)

        lax.while_loop(cond, step, (i - 1, alive(i - 1)))
        dq_ref[...] = (_unstack_heads(dq_acc[...]) * SCALE).astype(dq_ref.dtype)

        @pl.when(i == nqb - 1)
        def _():
            ck = pltpu.make_async_copy(dk_acc, dk_hbm.at[p], sem.at[0])
            cv = pltpu.make_async_copy(dv_acc, dv_hbm.at[p], sem.at[1])
            ck.start()
            cv.start()
            ck.wait()
            cv.wait()

    blk = lambda off: pl.BlockSpec((FQ, LANES), lambda p, i: (i, off + p))
    slab = lambda off: pl.BlockSpec((S, LANES), lambda p, i: (0, off + p))
    cols = pl.BlockSpec((1, FQ, 2), lambda p, i: (p, i, 0))
    rows = pl.BlockSpec((1, S // BK, 8, LANES), lambda p, i: (p, 0, 0, 0))
    return pl.pallas_call(
        body, name=name, grid=(4, nqb),
        in_specs=[blk(0), slab(0), slab(GROUP_COLS), cols, rows, cols, blk(0), blk(0), ANY, ANY, rows],
        out_specs=[blk(0), ANY, ANY, rows],
        out_shape=[jax.ShapeDtypeStruct((S, MIX_WIDTH), BF16),
                   jax.ShapeDtypeStruct((4, S, LANES), F32), jax.ShapeDtypeStruct((4, S, LANES), F32),
                   jax.ShapeDtypeStruct((4, S // BK, 8, LANES), F32)],
        scratch_shapes=[pltpu.VMEM((2 * FQ, LANES), F32), pltpu.VMEM((S, LANES), F32),
                        pltpu.VMEM((S, LANES), F32), pltpu.VMEM((1, LANES), F32),
                        pltpu.VMEM((2 * FQ, FK), F32), pltpu.SemaphoreType.DMA((2,))],
        compiler_params=_params(("arbitrary", "arbitrary")),
    )(proj, kv, kv, c_col, c_row, lse, merged, dmerged, dk_prev, dv_prev, dc_prev)


def _lane_scan(x, reverse):
    lane = lax.broadcasted_iota(jnp.int32, x.shape, 1)
    d = 1
    while d < LANES:
        if reverse:
            x = x + jnp.where(lane < LANES - d, pltpu.roll(x, LANES - d, 1), 0.0)
        else:
            x = x + jnp.where(lane >= d, pltpu.roll(x, d, 1), 0.0)
        d *= 2
    return x


def _gate_fwd(fl3, b8):
    nb = fl3.shape[0]

    def body(fl_ref, b_ref, c_ref):
        def step(kb, carry):
            x = fl_ref[kb] + b_ref[...]
            sp, _ = _softplus_parts(-x)
            c = _lane_scan(-sp, False) + carry
            c_ref[kb] = c
            return c[:, LANES - 1:LANES]

        lax.fori_loop(0, nb, step, jnp.zeros((8, 1), F32))

    return pl.pallas_call(body, name="forget_gate_cumsum",
                          out_shape=jax.ShapeDtypeStruct(fl3.shape, F32),
                          compiler_params=_params())(fl3, b8)


def _gate_bwd(dc3, fl3, b8):
    nb = fl3.shape[0]

    def body(dc_ref, fl_ref, b_ref, dfl_ref, db_ref):
        def step(t, st):
            carry, dbs = st
            kb = nb - 1 - t
            g = _lane_scan(dc_ref[kb], True) + carry
            x = fl_ref[kb] + b_ref[...]
            e = jnp.exp(-jnp.abs(x))
            rcp = 1.0 / (1.0 + e)
            dfl = g * jnp.where(x >= 0.0, e * rcp, rcp)
            dfl_ref[kb] = dfl
            return g[:, 0:1], dbs + dfl

        _, dbs = lax.fori_loop(0, nb, step, (jnp.zeros((8, 1), F32), jnp.zeros((8, LANES), F32)))
        db_ref[...] = jnp.broadcast_to(jnp.sum(dbs, axis=1, keepdims=True), (8, LANES))

    return pl.pallas_call(body, name="forget_gate_bwd",
                          out_shape=[jax.ShapeDtypeStruct(fl3.shape, F32), jax.ShapeDtypeStruct((8, LANES), F32)],
                          compiler_params=_params())(dc3, fl3, b8)


MEM_TQ = 512
MEM_COLS = MEM_WIDTH // LANES


def _mem_fwd(proj, qcol, mkv, mix, name):
    S = proj.shape[0]
    M = mkv.shape[0]

    def body(q_ref, mk_ref, mv_ref, mix_ref, o_ref, lse_ref):
        q2 = _head_rows(q_ref, SCALE)
        s = _dot_nt(q2, mk_ref[...])
        m = jnp.max(s, axis=1, keepdims=True)
        pr = jnp.exp(s - m)
        l = jnp.sum(pr, axis=1, keepdims=True)
        o_ref[...] = _unstack_heads(_dot(pr.astype(BF16), mv_ref[...]) / l)
        lse = m + jnp.log(l)
        lse_ref[0] = _two_cols(lse[:MEM_TQ], lse[MEM_TQ:])

    return pl.pallas_call(
        body, name=name, grid=(MEM_COLS, S // MEM_TQ),
        in_specs=[pl.BlockSpec((MEM_TQ, LANES), lambda p, i: (i, qcol + p)),
                  pl.BlockSpec((M, LANES), lambda p, i: (0, p)),
                  pl.BlockSpec((M, LANES), lambda p, i: (0, MEM_COLS + p)), ANY],
        out_specs=[pl.BlockSpec((MEM_TQ, LANES), lambda p, i: (i, GROUP_COLS + p)),
                   pl.BlockSpec((1, MEM_TQ, 2), lambda p, i: (p, i, 0))],
        out_shape=[jax.ShapeDtypeStruct((S, MERGED_WIDTH), F32), jax.ShapeDtypeStruct((2, S, 2), F32)],
        input_output_aliases={3: 0},
        compiler_params=_params(("parallel", "parallel")),
    )(proj, mkv, mkv, mix)


def _mem_bwd(proj, qcol, mkv, lse, merged, dmerged, name):
    S = proj.shape[0]
    M = mkv.shape[0]

    def body(q_ref, mk_ref, mv_ref, lse_ref, o_ref, do_ref, dq_ref, dmk_ref, dmv_ref):
        @pl.when(pl.program_id(1) == 0)
        def _():
            dmk_ref[...] = jnp.zeros_like(dmk_ref)
            dmv_ref[...] = jnp.zeros_like(dmv_ref)

        q2 = _head_rows(q_ref, SCALE)
        do2 = _head_rows(do_ref)
        tot = _stack(*_pair_rowsum(do_ref[...].astype(F32) * o_ref[...]))
        ls = lse_ref[0]
        pr = jnp.exp(_dot_nt(q2, mk_ref[...]) - _stack(ls[:, 0:1], ls[:, 1:2]))
        ds = pr * (_dot_nt(do2, mv_ref[...]) - tot)
        dsb = ds.astype(BF16)
        dmk_ref[...] += _dot_tn(dsb, q2)
        dmv_ref[...] += _dot_tn(pr.astype(BF16), do2)
        dq_ref[...] = (_unstack_heads(_dot(dsb, mk_ref[...])) * SCALE).astype(dq_ref.dtype)

    blk = lambda off: pl.BlockSpec((MEM_TQ, LANES), lambda p, i: (i, off + p))
    acc = pl.BlockSpec((M, LANES), lambda p, i: (0, p))
    return pl.pallas_call(
        body, name=name, grid=(MEM_COLS, S // MEM_TQ),
        in_specs=[blk(qcol), pl.BlockSpec((M, LANES), lambda p, i: (0, p)),
                  pl.BlockSpec((M, LANES), lambda p, i: (0, MEM_COLS + p)),
                  pl.BlockSpec((1, MEM_TQ, 2), lambda p, i: (p, i, 0)), blk(GROUP_COLS), blk(GROUP_COLS)],
        out_specs=[blk(0), acc, acc],
        out_shape=[jax.ShapeDtypeStruct((S, MEM_WIDTH), BF16), jax.ShapeDtypeStruct((M, MEM_WIDTH), F32),
                   jax.ShapeDtypeStruct((M, MEM_WIDTH), F32)],
        compiler_params=_params(("parallel", "arbitrary")),
    )(proj, mkv, mkv, lse, merged, dmerged)


def _c_layouts(c3):
    nb = c3.shape[0]
    pairs = c3.reshape(nb, 4, 2, LANES).transpose(1, 0, 2, 3)
    c_row = jnp.pad(pairs, ((0, 0), (0, 0), (0, 6), (0, 0)))
    c_col = pairs.transpose(0, 1, 3, 2).reshape(4, nb * LANES, 2)
    return c_col, c_row


def _local_step(x, mem, wb, shards, sm, loss_target):
    S = x.shape[0]
    nb = S // BK
    vec = lambda a: a.reshape(1, D_MODEL)
    b8 = jnp.broadcast_to(sm["b_f"].reshape(8, 1), (8, LANES))

    saved = []
    shared = None
    h = x
    hn = _rms_fwd(h, vec(sm["norm1_g"][0]), "norm1_0")
    for l in range(DEPTH):
        if l == N_A:
            w_kvf = jnp.pad(wb["w_kv_shared"], ((0, 0), (0, KVF_WIDTH - W_KV_SHARED)))
            hs = _rms_fwd(h, vec(sm["kv_norm_g"]), "kv_norm")
            kvf = _mm(hs, w_kvf, out_dtype=F32, name="kv_shared_proj")
            kv = kvf[:, :2 * MIX_WIDTH].astype(BF16)
            fl3 = kvf[:, 2 * MIX_WIDTH:2 * MIX_WIDTH + 8].T.reshape(8, nb, LANES).transpose(1, 0, 2)
            c3 = _gate_fwd(fl3, b8)
            c_col, c_row = _c_layouts(c3)
            shared = dict(h=h, hs=hs, kv=kv, fl3=fl3, c_col=c_col, c_row=c_row)
        mn = _rms_fwd(mem, vec(sm["mem_norm_g"][l]), f"mem_norm_{l}")
        mkv = _mm(mn, wb["w_mem_kv"][l], name=f"mem_kv_proj_{l}")
        if l < N_A:
            w_in = wb["w_in_a"][l]
            proj = _mm(hn, w_in, name=f"in_proj_{l}")
            mix, gathered = _sb_fwd(proj, f"stickbreak_fwd_{l}", _AllGather, shards[l])
            _unpack_gathered(PARTS[1 + l], gathered, wb)
            lse, qcol = None, Q_MEM_COL_A
        else:
            w_in = wb["w_in_b"][l - N_A]
            proj = _mm(hn, w_in, name=f"in_proj_{l}")
            mix, lse = _fox_fwd(proj, shared["kv"], shared["c_col"], shared["c_row"], f"fox_fwd_{l}")
            qcol = Q_MEM_COL_B
        merged, mlse = _mem_fwd(proj, qcol, mkv, mix, f"mem_attn_fwd_{l}")
        h_mid, hn2 = _mm(merged, wb["w_o"][l], epi="add_rms", extra=(h, vec(sm["norm2_g"][l])),
                         name=f"out_proj_{l}")
        u, act = _mm(hn2, wb["w_mlp1"][l], epi="relu2", name=f"mlp1_{l}")
        saved.append(dict(h=h, hn=hn, mn=mn, mkv=mkv, proj=proj, lse=lse, mlse=mlse, qcol=qcol, merged=merged,
                          h_mid=h_mid, hn2=hn2, u=u, act=act, w_in=w_in))
        if l + 1 < DEPTH:
            h, hn = _mm(act, wb["w_mlp2"][l], epi="add_rms", extra=(h_mid, vec(sm["norm1_g"][l + 1])),
                        name=f"mlp2_{l}")
        else:
            h = _mm(act, wb["w_mlp2"][l], out_dtype=F32, epi="add", extra=h_mid, name=f"mlp2_{l}")

    loss, dh, dhb, dg_final = _final_loss(h, vec(sm["final_norm_g"]), loss_target)

    gb = {n: [None] * (DEPTH if n not in ("w_in_a", "w_in_b") else 2) for n in
          ("w_in_a", "w_in_b", "w_mem_kv", "w_o", "w_mlp1", "w_mlp2")}
    gs = {n: [None] * DEPTH for n in ("norm1_g", "mem_norm_g", "norm2_g")}
    received = [None] * N_A
    dk_sh = jnp.zeros((4, S, LANES), F32)
    dv_sh = jnp.zeros((4, S, LANES), F32)
    dc_sh = jnp.zeros((4, nb, 8, LANES), F32)
    for l in reversed(range(DEPTH)):
        sv = saved[l]
        du = _mm(dhb, wb["w_mlp2"][l], mode="nt", epi="drelu2", extra=sv["u"], name=f"mlp2_dx_{l}")
        gb["w_mlp2"][l] = _mm(sv["act"], dhb, mode="tn", out_dtype=F32, name=f"mlp2_dw_{l}")
        gb["w_mlp1"][l] = _mm(sv["hn2"], du, mode="tn", out_dtype=F32, name=f"mlp1_dw_{l}")
        dh, dhb, gs["norm2_g"][l] = _mm(du, wb["w_mlp1"][l], mode="nt", epi="rms_bwd",
                                        extra=(sv["h_mid"], vec(sm["norm2_g"][l]), dh),
                                        name=f"mlp1_dx_norm2_bwd_{l}")
        dmerged = _mm(dhb, wb["w_o"][l], mode="nt", name=f"out_proj_dx_{l}")
        gb["w_o"][l] = _mm(sv["merged"], dhb, mode="tn", out_dtype=F32, name=f"out_proj_dw_{l}")
        if l < N_A:
            ready = _pack_grads(PARTS[1 + l], PART_ROWS[1 + l], gb, None)
            dq, dk, dv, received[l] = _sb_bwd(sv["proj"], sv["merged"], dmerged, f"stickbreak_bwd_{l}",
                                              _Scatter, ready)
        else:
            dq, dk_sh, dv_sh, dc_sh = _fox_bwd(sv["proj"], shared["kv"], shared["c_col"], shared["c_row"],
                                               sv["lse"], sv["merged"], dmerged, dk_sh, dv_sh, dc_sh,
                                               f"fox_bwd_{l}")
        dqm, dmk, dmv = _mem_bwd(sv["proj"], sv["qcol"], sv["mkv"], sv["mlse"], sv["merged"], dmerged,
                                 f"mem_attn_bwd_{l}")
        if l < N_A:
            flat = lambda t: t.transpose(1, 0, 2).reshape(S, MIX_WIDTH).astype(BF16)
            dproj = jnp.concatenate([dq, flat(dk), flat(dv), dqm], axis=1)
        else:
            dproj = jnp.concatenate([dq, dqm], axis=1)
        name_in = "w_in_a" if l < N_A else "w_in_b"
        gb[name_in][l if l < N_A else l - N_A] = _mm(sv["hn"], dproj, mode="tn", out_dtype=F32,
                                                      name=f"in_proj_dw_{l}")
        dh, dhb, gs["norm1_g"][l] = _mm(dproj, sv["w_in"], mode="nt", epi="rms_bwd",
                                        extra=(sv["h"], vec(sm["norm1_g"][l]), dh),
                                        name=f"in_proj_dx_norm1_bwd_{l}")
        dmkv = jnp.concatenate([dmk, dmv], axis=1)
        gb["w_mem_kv"][l] = _mm(sv["mn"], dmkv, mode="tn", out_dtype=F32, name=f"mem_kv_dw_{l}")
        dmn = _mm(dmkv, wb["w_mem_kv"][l], mode="nt", out_dtype=F32, name=f"mem_kv_dx_{l}")
        gs["mem_norm_g"][l] = _rms_gain_grad(mem, dmn, f"mem_norm_bwd_{l}")
        if l == N_A:
            dfl3, db8 = _gate_bwd(dc_sh.reshape(4, nb, 8, LANES)[:, :, :2].transpose(1, 0, 2, 3).reshape(nb, 8, LANES),
                                  shared["fl3"], b8)
            dfl = dfl3.transpose(1, 0, 2).reshape(8, S).T
            flat = lambda t: t.transpose(1, 0, 2).reshape(S, MIX_WIDTH).astype(BF16)
            dkvf = jnp.concatenate([flat(dk_sh), flat(dv_sh),
                                    jnp.pad(dfl, ((0, 0), (0, LANES - 8))).astype(BF16)], axis=1)
            gb["w_kv_shared"] = _mm(shared["hs"], dkvf, mode="tn", out_dtype=F32, name="kv_shared_dw")[:, :W_KV_SHARED]
            dh, dhb, g_kvn = _mm(dkvf, w_kvf, mode="nt", epi="rms_bwd",
                                 extra=(shared["h"], vec(sm["kv_norm_g"]), dh), name="kv_shared_dx_norm_bwd")
            g_bf = db8[:, 0]

    gsmall = {n: jnp.concatenate(v, axis=0) for n, v in gs.items()}
    gsmall["kv_norm_g"] = g_kvn
    gsmall["final_norm_g"] = dg_final
    gsmall["b_f"] = g_bf
    return loss, dh, gb, gsmall, received


def kernel(x, mem, norm1_g, w_in_a, w_in_b, w_mem_kv, mem_norm_g, w_o, norm2_g, w_mlp1, w_mlp2, kv_norm_g, w_kv_shared, b_f, final_norm_g, loss_target, m_norm1_g, m_w_in_a, m_w_in_b, m_w_mem_kv, m_mem_norm_g, m_w_o, m_norm2_g, m_w_mlp1, m_w_mlp2, m_kv_norm_g, m_w_kv_shared, m_b_f, m_final_norm_g, v_norm1_g, v_w_in_a, v_w_in_b, v_w_mem_kv, v_mem_norm_g, v_w_o, v_norm2_g, v_w_mlp1, v_w_mlp2, v_kv_norm_g, v_w_kv_shared, v_b_f, v_final_norm_g):
    big_w = dict(w_in_a=w_in_a, w_in_b=w_in_b, w_mem_kv=w_mem_kv, w_o=w_o, w_mlp1=w_mlp1, w_mlp2=w_mlp2,
                 w_kv_shared=w_kv_shared)
    small_w = dict(norm1_g=norm1_g, mem_norm_g=mem_norm_g, norm2_g=norm2_g, kv_norm_g=kv_norm_g,
                   final_norm_g=final_norm_g, b_f=b_f)
    big_m = dict(w_in_a=m_w_in_a, w_in_b=m_w_in_b, w_mem_kv=m_w_mem_kv, w_o=m_w_o, w_mlp1=m_w_mlp1,
                 w_mlp2=m_w_mlp2, w_kv_shared=m_w_kv_shared)
    small_m = dict(norm1_g=m_norm1_g, mem_norm_g=m_mem_norm_g, norm2_g=m_norm2_g, kv_norm_g=m_kv_norm_g,
                   final_norm_g=m_final_norm_g, b_f=m_b_f)
    big_v = dict(w_in_a=v_w_in_a, w_in_b=v_w_in_b, w_mem_kv=v_w_mem_kv, w_o=v_w_o, w_mlp1=v_w_mlp1,
                 w_mlp2=v_w_mlp2, w_kv_shared=v_w_kv_shared)
    small_v = dict(norm1_g=v_norm1_g, mem_norm_g=v_mem_norm_g, norm2_g=v_norm2_g, kv_norm_g=v_kv_norm_g,
                   final_norm_g=v_final_norm_g, b_f=v_b_f)

    def pack(k, big, small, dtype):
        return _pack_local(PARTS[k], PART_ROWS[k], big, small if k == 0 else None, dtype)

    def pack_all(big, small):
        return jnp.concatenate([pack(k, big, small, F32) for k in range(len(PARTS))], axis=0)

    wb = {n: {} for n in BIG_NAMES}
    _unpack_gathered(PARTS[0], _allgather_chips(pack(0, big_w, small_w, BF16)), wb)
    shards = [pack(1 + l, big_w, None, BF16) for l in range(N_A)]

    loss, dx, gb, gsmall, received = _local_step(x[0], mem[0], wb, shards, small_w, loss_target[0])

    received = [_scatter_chips(_pack_grads(PARTS[0], PART_ROWS[0], gb, gsmall))] + received
    part = jnp.concatenate([_sum4(r) for r in received], axis=0)
    other = _swap_cores(part)
    g, delta, new_m, new_v = _adamw(part, other, pack_all(big_w, small_w), pack_all(big_m, small_m),
                                    pack_all(big_v, small_v))

    outs = [lax.psum(loss[0, 0], ("x", "y", "c")), dx[None]]
    for packed in (g, delta, new_m, new_v):
        pieces, d, off = {n: [] for n in BIG_NAMES}, {}, 0
        for k, part_k in enumerate(PARTS):
            _unpack_local(part_k, packed[off:off + PART_ROWS[k]], k == 0, pieces, d)
            off += PART_ROWS[k]
        d.update(_join_layers(pieces))
        outs.extend(d[n] for n in WEIGHT_ORDER)
    return tuple(outs)
```

```python
import functools
import math

import jax
import jax.numpy as jnp
from jax import lax
from jax.experimental import pallas as pl
from jax.experimental.pallas import tpu as pltpu

F32 = jnp.float32
BF16 = jnp.bfloat16

D_MODEL = 1024
HEAD_DIM = 64
MIX_WIDTH = 512
MEM_WIDTH = 256
MERGED_WIDTH = MIX_WIDTH + MEM_WIDTH
DEPTH = 4
N_A = 2
D_FF = 4096
EPS = 1e-6
NEG_INF = -1e30
SCALE = 1.0 / math.sqrt(HEAD_DIM)

ADAM_LR = 0.001
ADAM_B1 = 0.9
ADAM_B2 = 0.999
ADAM_EPS = 1e-08
ADAM_WD = 0.01
ADAM_STEP = 10

LANES = 128
GROUP_COLS = MIX_WIDTH // LANES
Q_MEM_COL_A = 3 * GROUP_COLS
Q_MEM_COL_B = GROUP_COLS
W_KV_SHARED = 2 * MIX_WIDTH + 8
KVF_WIDTH = 1152
BQ = 256
BK = 128
DIAG_TILES = BQ // BK
CHAINS = 4
UNDERFLOW_BOUND = -110.0
VMEM_LIMIT = 56 * 1024 * 1024

MESH = pl.DeviceIdType.MESH
N_CHIPS = 4

PARTS = (
    (("w_in_a", 0, 1, (1024, 448), 1),
     ("w_mem_kv", 0, 1, (256, 512), 0)),
    (("w_o", 0, 1, (768, 256), 1),
     ("w_mlp1", 0, 1, (1024, 1024), 1),
     ("w_mlp2", 0, 1, (1024, 1024), 0),
     ("w_in_a", 1, 2, (1024, 448), 1),
     ("w_mem_kv", 1, 2, (256, 512), 0)),
    (("w_o", 1, 4, (768, 256), 1),
     ("w_mlp1", 1, 4, (1024, 1024), 1),
     ("w_mlp2", 1, 4, (1024, 1024), 0),
     ("w_in_b", 0, 2, (256, 768), 0),
     ("w_mem_kv", 2, 4, (256, 512), 0),
     ("w_kv_shared", None, None, (1024, 258), 1)),
)
BIG_NAMES = ("w_in_a", "w_in_b", "w_mem_kv", "w_o", "w_mlp1", "w_mlp2", "w_kv_shared")
SMALL = (
    ("norm1_g", (4, 1024)),
    ("mem_norm_g", (4, 1024)),
    ("norm2_g", (4, 1024)),
    ("kv_norm_g", (1, 1024)),
    ("final_norm_g", (1, 1024)),
    ("b_f", (1, 1024)),
)
WEIGHT_ORDER = ("norm1_g", "w_in_a", "w_in_b", "w_mem_kv", "mem_norm_g", "w_o", "norm2_g", "w_mlp1",
                "w_mlp2", "kv_norm_g", "w_kv_shared", "b_f", "final_norm_g")


ROW_ALIGN = 16
PACK_TILE = 256
SMALL_ROWS = ROW_ALIGN
assert sum(s[0] for _, s in SMALL) <= SMALL_ROWS


def _section_rows(entry):
    _, lo, hi, shape, _ = entry
    rows = (1 if lo is None else hi - lo) * math.prod(shape) // D_MODEL
    return rows, -(-rows // ROW_ALIGN) * ROW_ALIGN


def _round_up(n, m):
    return -(-n // m) * m


SUM_TILE = 128
_used = [sum(_section_rows(e)[1] for e in part) for part in PARTS]
PART_ROWS = [_round_up(_used[0] + SMALL_ROWS, SUM_TILE), _round_up(_used[1], SUM_TILE)]
PART_ROWS.append(_round_up(_used[2] + sum(PART_ROWS), PACK_TILE) - sum(PART_ROWS))
assert PART_ROWS[2] % SUM_TILE == 0


def _params(sem=None):
    return pltpu.CompilerParams(dimension_semantics=sem, vmem_limit_bytes=VMEM_LIMIT)


def _pick(n, cands):
    for c in cands:
        if n % c == 0:
            return c
    raise ValueError(f"no tile for {n}")


def _section(a, entry):
    a = a.reshape(-1, D_MODEL)
    return jnp.pad(a, ((0, _section_rows(entry)[1] - a.shape[0]), (0, 0)))


def _small_block(small, dtype):
    blk = jnp.zeros((SMALL_ROWS, D_MODEL), dtype)
    off = 0
    for n, shp in SMALL:
        a = small[n].astype(dtype)
        if n == "b_f":
            blk = blk.at[off, :a.size].set(a.reshape(-1))
        else:
            blk = blk.at[off:off + shp[0]].set(a.reshape(shp))
        off += shp[0]
    return blk


def _fill(parts, rows, dtype):
    used = sum(p.shape[0] for p in parts)
    return jnp.concatenate(parts + [jnp.zeros((rows - used, D_MODEL), dtype)], axis=0)


def _pack_local(part, rows, big, small, dtype):
    parts = [_section((big[e[0]] if e[1] is None else big[e[0]][e[1]:e[2]]).astype(dtype), e) for e in part]
    if small is not None:
        parts.append(_small_block(small, dtype))
    return _fill(parts, rows, dtype)


def _unpack_local(part, p, with_small, pieces, small):
    off = 0
    for e in part:
        n, lo, hi, shp, _ = e
        rows, reserved = _section_rows(e)
        pieces[n].append((lo, p[off:off + rows].reshape(shp if lo is None else (hi - lo,) + shp)))
        off += reserved
    if with_small:
        for n, shp in SMALL:
            a = p[off:off + shp[0]]
            small[n] = a[0, :8] if n == "b_f" else (a.reshape(D_MODEL) if shp[0] == 1 else a)
            off += shp[0]


def _join_layers(pieces):
    out = {}
    for n, ps in pieces.items():
        ps = sorted(ps, key=lambda t: -1 if t[0] is None else t[0])
        out[n] = ps[0][1] if len(ps) == 1 else jnp.concatenate([a for _, a in ps], axis=0)
    return out


def _unpack_gathered(part, g, weights):
    off = 0
    for e in part:
        n, lo, hi, shp, ax = e
        rows, reserved = _section_rows(e)
        if lo is None:
            sec = g[:, off:off + rows].reshape((N_CHIPS,) + shp)
            weights[n] = jnp.concatenate([sec[j] for j in range(N_CHIPS)], axis=ax)
        else:
            sec = g[:, off:off + rows].reshape((N_CHIPS, hi - lo) + shp)
            for l in range(lo, hi):
                weights[n][l] = jnp.concatenate([sec[j, l - lo] for j in range(N_CHIPS)], axis=ax)
        off += reserved


def _pack_grads(part, rows, gbig, gsmall):
    small = None if gsmall is None else _small_block(gsmall, BF16)
    chunks = []
    for j in range(N_CHIPS):
        parts = []
        for e in part:
            n, lo, hi, shp, ax = e
            w = shp[ax]
            layers = [gbig[n]] if lo is None else [gbig[n][l] for l in range(lo, hi)]
            cut = [lax.slice_in_dim(g, j * w, (j + 1) * w, axis=ax).astype(BF16).reshape(-1, D_MODEL) for g in layers]
            parts.append(_section(cut[0] if len(cut) == 1 else jnp.concatenate(cut, axis=0), e))
        if small is not None:
            parts.append(small)
        chunks.append(_fill(parts, rows, BF16))
    return jnp.stack(chunks, axis=0)


ANY = pl.BlockSpec(memory_space=pl.ANY)


def _other_chips(x, y):
    return [(1 - x, y), (x, 1 - y), (1 - x, 1 - y)]


class _AllGather:
    SCRATCH = [pltpu.SemaphoreType.DMA((3,)), pltpu.SemaphoreType.DMA((3,)), pltpu.SemaphoreType.DMA((3,)),
               pltpu.SemaphoreType.DMA((3,)), pltpu.SemaphoreType.DMA]

    def __init__(self, w_ref, o_ref, send_sems, recv_sems, pass_send, pass_recv, local_sem):
        self.w_ref, self.o_ref = w_ref, o_ref
        self.sems = (send_sems, recv_sems, pass_send, pass_recv, local_sem)
        x, y, c = lax.axis_index("x"), lax.axis_index("y"), lax.axis_index("c")
        half = w_ref.shape[0] // 2
        self.c, self.me, self.sibling = c, 2 * x + y, (x, y, 1 - c)
        self.mine = pl.ds(pl.multiple_of(c * half, ROW_ALIGN), half)
        self.other = pl.ds(pl.multiple_of((1 - c) * half, ROW_ALIGN), half)
        self.chips = _other_chips(x, y)

    def _over_ici(self, j, rows_of):
        chip = self.chips[j]
        return pltpu.make_async_remote_copy(
            src_ref=self.w_ref.at[self.mine], dst_ref=self.o_ref.at[rows_of, self.mine],
            send_sem=self.sems[0].at[j], recv_sem=self.sems[1].at[j],
            device_id=(chip[0], chip[1], self.c), device_id_type=MESH)

    def _over_d2d(self, j, rows):
        where = self.o_ref.at[2 * self.chips[j][0] + self.chips[j][1], rows]
        return pltpu.make_async_remote_copy(src_ref=where, dst_ref=where, send_sem=self.sems[2].at[j],
                                            recv_sem=self.sems[3].at[j], device_id=self.sibling,
                                            device_id_type=MESH)

    def _local(self):
        return pltpu.make_async_copy(self.w_ref, self.o_ref.at[self.me], self.sems[4])

    def start(self):
        self._local().start()
        for j in range(3):
            self._over_ici(j, self.me).start()

    def finish(self):
        for j in range(3):
            self._over_ici(j, 2 * self.chips[j][0] + self.chips[j][1]).wait_recv()
            self._over_d2d(j, self.mine).start()
        for j in range(3):
            self._over_d2d(j, self.other).wait_recv()
        for j in range(3):
            self._over_ici(j, self.me).wait_send()
            self._over_d2d(j, self.mine).wait_send()
        self._local().wait()


class _Scatter:
    SCRATCH = [pltpu.SemaphoreType.DMA((3,)), pltpu.SemaphoreType.DMA((3,)), pltpu.SemaphoreType.DMA]

    def __init__(self, g_ref, o_ref, send_sems, recv_sems, local_sem):
        self.g_ref, self.o_ref, self.sems = g_ref, o_ref, (send_sems, recv_sems, local_sem)
        x, y, c = lax.axis_index("x"), lax.axis_index("y"), lax.axis_index("c")
        self.c, self.me, self.chips = c, 2 * x + y, _other_chips(x, y)

    def _copy(self, j):
        chip = self.chips[j]
        return pltpu.make_async_remote_copy(
            src_ref=self.g_ref.at[2 * chip[0] + chip[1]], dst_ref=self.o_ref.at[self.me],
            send_sem=self.sems[0].at[j], recv_sem=self.sems[1].at[j],
            device_id=(chip[0], chip[1], self.c), device_id_type=MESH)

    def _local(self):
        return pltpu.make_async_copy(self.g_ref.at[self.me], self.o_ref.at[self.me], self.sems[2])

    def start(self):
        self._local().start()
        for j in range(3):
            self._copy(j).start()

    def finish(self):
        for j in range(3):
            self._copy(j).wait()
        self._local().wait()


def _allgather_chips(w):
    def body(w_ref, o_ref, *sems):
        ag = _AllGather(w_ref, o_ref, *sems)
        ag.start()
        ag.finish()

    return pl.pallas_call(
        body, name="allgather_weights",
        out_shape=jax.ShapeDtypeStruct((N_CHIPS,) + w.shape, w.dtype),
        in_specs=[ANY], out_specs=ANY, scratch_shapes=_AllGather.SCRATCH,
    )(w)


def _scatter_chips(g4):
    def body(g_ref, o_ref, *sems):
        sc = _Scatter(g_ref, o_ref, *sems)
        sc.start()
        sc.finish()

    return pl.pallas_call(
        body, name="scatter_grads",
        out_shape=jax.ShapeDtypeStruct(g4.shape, g4.dtype),
        in_specs=[ANY], out_specs=ANY, scratch_shapes=_Scatter.SCRATCH,
    )(g4)


def _swap_cores(p):
    def body(p_ref, o_ref, send_sem, recv_sem):
        x, y, c = lax.axis_index("x"), lax.axis_index("y"), lax.axis_index("c")
        cp = pltpu.make_async_remote_copy(src_ref=p_ref, dst_ref=o_ref, send_sem=send_sem, recv_sem=recv_sem,
                                          device_id=(x, y, 1 - c), device_id_type=MESH)
        cp.start()
        cp.wait()

    return pl.pallas_call(
        body, name="swap_cores",
        out_shape=jax.ShapeDtypeStruct(p.shape, p.dtype),
        in_specs=[ANY], out_specs=ANY,
        scratch_shapes=[pltpu.SemaphoreType.DMA, pltpu.SemaphoreType.DMA],
    )(p)


def _sum4(r4):
    _, R, C = r4.shape

    def body(r_ref, o_ref):
        o_ref[...] = ((r_ref[0].astype(F32) + r_ref[1].astype(F32)) + r_ref[2].astype(F32)) + r_ref[3].astype(F32)

    return pl.pallas_call(
        body, name="sum_chips", grid=(R // SUM_TILE,),
        in_specs=[pl.BlockSpec((N_CHIPS, SUM_TILE, C), lambda i: (0, i, 0))],
        out_specs=pl.BlockSpec((SUM_TILE, C), lambda i: (i, 0)),
        out_shape=jax.ShapeDtypeStruct((R, C), F32),
        compiler_params=_params(("parallel",)),
    )(r4)


def _adamw(pa, pb, w, m, v):
    R, C = w.shape
    c1 = 1.0 - ADAM_B1
    c2 = 1.0 - ADAM_B2
    bc1 = 1.0 - ADAM_B1 ** ADAM_STEP
    bc2 = 1.0 - ADAM_B2 ** ADAM_STEP

    def body(pa_ref, pb_ref, w_ref, m_ref, v_ref, g_ref, d_ref, mo_ref, vo_ref):
        g = pa_ref[...] + pb_ref[...]
        mn = ADAM_B1 * m_ref[...] + c1 * g
        vn = ADAM_B2 * v_ref[...] + c2 * (g * g)
        m_hat = mn / bc1
        v_hat = vn / bc2
        g_ref[...] = g
        d_ref[...] = -ADAM_LR * (m_hat / (jnp.sqrt(v_hat) + ADAM_EPS) + ADAM_WD * w_ref[...])
        mo_ref[...] = mn
        vo_ref[...] = vn

    spec = pl.BlockSpec((PACK_TILE, C), lambda i: (i, 0))
    shp = jax.ShapeDtypeStruct((R, C), F32)
    return pl.pallas_call(
        body, name="adamw", grid=(R // PACK_TILE,),
        in_specs=[spec] * 5, out_specs=[spec] * 4, out_shape=[shp] * 4,
        compiler_params=_params(("parallel",)),
    )(pa, pb, w, m, v)


def _rms_fwd(x, g, name):
    R, Dm = x.shape
    tr = _pick(R, (512, 256, 128))

    def body(x_ref, g_ref, o_ref):
        xf = x_ref[...]
        r = lax.rsqrt(jnp.mean(xf * xf, axis=-1, keepdims=True) + EPS)
        o_ref[...] = (xf * r * g_ref[...]).astype(o_ref.dtype)

    return pl.pallas_call(
        body, name=name, grid=(R // tr,),
        in_specs=[pl.BlockSpec((tr, Dm), lambda i: (i, 0)), pl.BlockSpec((1, Dm), lambda i: (0, 0))],
        out_specs=pl.BlockSpec((tr, Dm), lambda i: (i, 0)),
        out_shape=jax.ShapeDtypeStruct((R, Dm), BF16),
        compiler_params=_params(("parallel",)),
    )(x, g)


def _rms_gain_grad(x, dy, name):
    R, Dm = x.shape
    tr = _pick(R, (256, 128))

    def body(x_ref, dy_ref, dg_ref):
        xf = x_ref[...]
        r = lax.rsqrt(jnp.mean(xf * xf, axis=-1, keepdims=True) + EPS)

        @pl.when(pl.program_id(0) == 0)
        def _():
            dg_ref[...] = jnp.zeros_like(dg_ref)

        dg_ref[...] += jnp.sum(dy_ref[...] * (xf * r), axis=0, keepdims=True)

    row = pl.BlockSpec((tr, Dm), lambda i: (i, 0))
    return pl.pallas_call(
        body, name=name, grid=(R // tr,),
        in_specs=[row, row], out_specs=pl.BlockSpec((1, Dm), lambda i: (0, 0)),
        out_shape=jax.ShapeDtypeStruct((1, Dm), F32),
        compiler_params=_params(("arbitrary",)),
    )(x, dy)


def _final_loss(x, g, tgt):
    R, Dm = x.shape
    tr = _pick(R, (256, 128))

    def body(x_ref, g_ref, t_ref, l_ref, dx_ref, dxb_ref, dg_ref):
        xf = x_ref[...]
        gv = g_ref[...]
        r = lax.rsqrt(jnp.mean(xf * xf, axis=-1, keepdims=True) + EPS)
        xr = xf * r
        err = xr * gv - t_ref[...]
        dy_ = err * (1.0 / Dm)
        gdy = dy_ * gv
        mdot = jnp.mean(xf * gdy, axis=-1, keepdims=True)
        dx = r * gdy - xf * ((r * r * r) * mdot)
        dx_ref[...] = dx
        dxb_ref[...] = dx.astype(BF16)

        @pl.when(pl.program_id(0) == 0)
        def _():
            dg_ref[...] = jnp.zeros_like(dg_ref)
            l_ref[...] = jnp.zeros_like(l_ref)

        dg_ref[...] += jnp.sum(dy_ * xr, axis=0, keepdims=True)
        sq = jnp.sum(err * err, axis=1, keepdims=True)
        l_ref[...] += jnp.sum(sq, axis=0, keepdims=True) * (0.5 / Dm)

    row = pl.BlockSpec((tr, Dm), lambda i: (i, 0))
    vec = pl.BlockSpec((1, Dm), lambda i: (0, 0))
    return pl.pallas_call(
        body, name="final_norm_loss", grid=(R // tr,),
        in_specs=[row, vec, row],
        out_specs=[pl.BlockSpec((1, 1), lambda i: (0, 0)), row, row, vec],
        out_shape=[jax.ShapeDtypeStruct((1, 1), F32), jax.ShapeDtypeStruct((R, Dm), F32),
                   jax.ShapeDtypeStruct((R, Dm), BF16), jax.ShapeDtypeStruct((1, Dm), F32)],
        compiler_params=_params(("arbitrary",)),
    )(x, g, tgt)


MAX_TK = 2048

_DIMS = {"nn": (((1,), (0,)), ((), ())), "nt": (((1,), (1,)), ((), ())), "tn": (((0,), (0,)), ((), ()))}


def _mm(a, b, *, mode="nn", out_dtype=BF16, epi=None, extra=None, name):
    if mode == "nn":
        (M, K), N = a.shape, b.shape[1]
    elif mode == "nt":
        (M, K), N = a.shape, b.shape[0]
    else:
        (K, M), N = a.shape, b.shape[1]
    deep = K > MAX_TK
    tm = _pick(M, (512, 256, 128) if (epi == "rms_bwd" or deep) else (1024, 768, 512, 256, 128))
    wide = (2048,) if (K <= MAX_TK // 2 and mode != "tn") else ()
    tn = _pick(N, wide + (1024, 896, 768, 640, 512, 384, 256, 128))
    tk = K if K <= (2 * MAX_TK if deep else MAX_TK) else _pick(K, (2 * MAX_TK, MAX_TK, 1024, 512, 256, 128))
    nk = K // tk
    extras = () if extra is None else (extra if isinstance(extra, tuple) else (extra,))
    n_out = {"relu2": 2, "add_rms": 2, "rms_bwd": 3}.get(epi, 1)
    assert epi not in ("rms_bwd", "add_rms") or tn == N

    def body(*refs):
        a_ref, b_ref = refs[:2]
        e_refs = refs[2:2 + len(extras)]
        e_ref = e_refs[0] if e_refs else None
        outs = refs[2 + len(extras):2 + len(extras) + n_out]
        k = pl.program_id(2)
        part = lax.dot_general(a_ref[...].astype(BF16), b_ref[...].astype(BF16), _DIMS[mode],
                               preferred_element_type=F32)

        def finish(acc):
            if epi is None:
                outs[0][...] = acc.astype(outs[0].dtype)
            elif epi == "add":
                outs[0][...] = (e_ref[...] + acc).astype(outs[0].dtype)
            elif epi == "add_rms":
                y = e_refs[0][...] + acc
                outs[0][...] = y
                r = lax.rsqrt(jnp.mean(y * y, axis=-1, keepdims=True) + EPS)
                outs[1][...] = (y * r * e_refs[1][...]).astype(BF16)
            elif epi == "relu2":
                outs[0][...] = acc.astype(BF16)
                rl = jnp.maximum(acc, 0.0)
                outs[1][...] = (rl * rl).astype(BF16)
            elif epi == "drelu2":
                u = e_ref[...].astype(F32)
                outs[0][...] = (acc * (2.0 * jnp.maximum(u, 0.0))).astype(outs[0].dtype)
            elif epi == "rms_bwd":
                x_ref, g_ref, dres_ref = e_refs
                xf = x_ref[...]
                r = lax.rsqrt(jnp.mean(xf * xf, axis=-1, keepdims=True) + EPS)
                gdy = acc * g_ref[...]
                mdot = jnp.mean(xf * gdy, axis=-1, keepdims=True)
                dx = dres_ref[...] + (r * gdy - xf * ((r * r * r) * mdot))
                outs[0][...] = dx
                outs[1][...] = dx.astype(BF16)

                @pl.when(pl.program_id(0) == 0)
                def _():
                    outs[2][...] = jnp.zeros_like(outs[2])

                outs[2][...] += jnp.sum(acc * (xf * r), axis=0, keepdims=True)

        if nk == 1:
            finish(part)
        else:
            acc_ref = refs[-1]

            @pl.when(k == 0)
            def _():
                acc_ref[...] = part

            @pl.when(jnp.logical_and(k > 0, k < nk - 1))
            def _():
                acc_ref[...] += part

            @pl.when(k == nk - 1)
            def _():
                finish(acc_ref[...] + part)

    if mode == "tn":
        a_spec = pl.BlockSpec((tk, tm), lambda i, j, k: (k, i))
    else:
        a_spec = pl.BlockSpec((tm, tk), lambda i, j, k: (i, k))
    if mode == "nt":
        b_spec = pl.BlockSpec((tn, tk), lambda i, j, k: (j, k))
    else:
        b_spec = pl.BlockSpec((tk, tn), lambda i, j, k: (k, j))
    o_spec = pl.BlockSpec((tm, tn), lambda i, j, k: (i, j))
    vec_spec = pl.BlockSpec((1, tn), lambda i, j, k: (0, j))
    ins, in_specs = [a, b] + list(extras), [a_spec, b_spec]
    if epi == "rms_bwd":
        in_specs += [o_spec, vec_spec, o_spec]
        out_shape = [jax.ShapeDtypeStruct((M, N), F32), jax.ShapeDtypeStruct((M, N), BF16),
                     jax.ShapeDtypeStruct((1, N), F32)]
        out_specs = [o_spec, o_spec, vec_spec]
    elif epi == "add_rms":
        in_specs += [o_spec, vec_spec]
        out_shape = [jax.ShapeDtypeStruct((M, N), F32), jax.ShapeDtypeStruct((M, N), BF16)]
        out_specs = [o_spec, o_spec]
    else:
        in_specs += [o_spec] * len(extras)
        out_shape = [jax.ShapeDtypeStruct((M, N), BF16 if epi == "relu2" else out_dtype)] * n_out
        out_specs = [o_spec] * n_out
    res = pl.pallas_call(
        body, name=name, grid=(M // tm, N // tn, nk),
        in_specs=in_specs, out_specs=out_specs, out_shape=out_shape,
        scratch_shapes=[pltpu.VMEM((tm, tn), F32)] if nk > 1 else [],
        compiler_params=_params(("arbitrary",) * 3 if epi == "rms_bwd" else ("parallel", "parallel", "arbitrary")),
    )(*ins)
    return res if n_out > 1 else res[0]


def _dot(a, b):
    return lax.dot_general(a, b, _DIMS["nn"], preferred_element_type=F32)


def _dot_nt(a, b):
    return lax.dot_general(a, b, _DIMS["nt"], preferred_element_type=F32)


def _dot_tn(a, b):
    return lax.dot_general(a, b, _DIMS["tn"], preferred_element_type=F32)


def _split_dot(x, t):
    hi = x.astype(BF16)
    lo = (x - hi.astype(F32)).astype(BF16)
    return _dot(jnp.concatenate([hi, lo], axis=1), jnp.concatenate([t, t], axis=0))


def _head_pair(ref, scale=None):
    xf = ref[...].astype(F32)
    if scale is not None:
        xf = xf * scale
    is_a = lax.broadcasted_iota(jnp.int32, xf.shape, 1) < HEAD_DIM
    return jnp.where(is_a, xf, 0.0).astype(BF16), jnp.where(is_a, 0.0, xf).astype(BF16)


def _stack(a, b):
    return jnp.concatenate([a, b], axis=0)


def _head_rows(ref, scale=None):
    return _stack(*_head_pair(ref, scale))


def _unstack_heads(x):
    rows = x.shape[0] // 2
    return _select_pair(x[:rows], x[rows:])


def _pair_rowsum(x):
    is_a = lax.broadcasted_iota(jnp.int32, x.shape, 1) < HEAD_DIM
    return (jnp.sum(jnp.where(is_a, x, 0.0), axis=1, keepdims=True),
            jnp.sum(jnp.where(is_a, 0.0, x), axis=1, keepdims=True))


def _select_pair(xa, xb):
    is_a = lax.broadcasted_iota(jnp.int32, xa.shape, 1) < HEAD_DIM
    return jnp.where(is_a, xa, xb)


def _two_cols(xa, xb):
    rows = xa.shape[0]
    first = lax.broadcasted_iota(jnp.int32, (rows, 2), 1) == 0
    return jnp.where(first, xa, xb)


def _softplus_parts(z):
    e = jnp.exp(-jnp.abs(z))
    return jnp.maximum(z, 0.0) + jnp.log(1.0 + e), e


def _tile_iotas():
    row = lax.broadcasted_iota(jnp.int32, (BK, BK), 0)
    col = lax.broadcasted_iota(jnp.int32, (BK, BK), 1)
    return row, col


def _stacked_iotas(bq, nk):
    row = lax.broadcasted_iota(jnp.int32, (2 * bq, nk), 0) & (bq - 1)
    col = lax.broadcasted_iota(jnp.int32, (2 * bq, nk), 1)
    return row, col


def _side_exchange(exchange, operand, n_in, n_out):
    out_shape = jax.ShapeDtypeStruct(((N_CHIPS,) + operand.shape) if exchange is _AllGather else operand.shape,
                                     operand.dtype)
    n_sem = len(exchange.SCRATCH)

    def pick(refs):
        def make():
            return exchange(refs[n_in], refs[n_in + 1 + n_out], *refs[len(refs) - n_sem:])

        return (lambda: make().start()), (lambda: make().finish())

    return [operand], [ANY], [out_shape], [ANY], pick


def _sb_fwd(proj, name, exchange, operand):
    S = proj.shape[0]
    nqb = S // (CHAINS * BQ)
    x_in, x_in_specs, x_out, x_out_specs, pick = _side_exchange(exchange, operand, 3, 1)

    def body(*refs):
        q_ref, k_ref, v_ref = refs[:3]
        o_ref = refs[3 + len(x_in)]
        acc_ref = refs[3 + len(x_in) + 1 + len(x_out)]
        start, finish = pick(refs)
        p = pl.program_id(0)
        i = pl.program_id(1)

        @pl.when(jnp.logical_and(p == 0, i == 0))
        def _():
            start()

        q2 = [_head_rows(q_ref.at[pl.ds(ch * BQ, BQ)], SCALE) for ch in range(CHAINS)]
        row, col = _tile_iotas()
        tri = (row > col).astype(BF16)
        srow, scol = _stacked_iotas(BQ, BK)
        acc_ref[...] = jnp.zeros_like(acc_ref)

        def tile(ch, kb, c, dmask=None, valid=None):
            r0 = pl.multiple_of(kb * BK, BK)
            kblk = k_ref[pl.ds(r0, BK), :]
            vblk = v_ref[pl.ds(r0, BK), :]
            z = _dot_nt(q2[ch], kblk)
            sp, _ = _softplus_parts(z)
            lm = -sp
            if dmask is not None:
                lm = jnp.where(dmask, lm, 0.0)
            btw = _split_dot(lm, tri)
            w = jnp.exp((z - sp) + btw + c)
            if dmask is not None:
                w = jnp.where(dmask, w, 0.0)
            if valid is not None:
                w = w * valid
            acc_ref[ch] += _dot(w.astype(BF16), vblk)
            return c + btw[:, 0:1] + lm[:, 0:1]

        def alive(c):
            return jnp.max(c) > UNDERFLOW_BOUND

        cs = [jnp.zeros((2 * BQ, 1), F32)] * CHAINS
        for d in reversed(range(DIAG_TILES)):
            cs = [tile(ch, (CHAINS * i + ch) * DIAG_TILES + d, cs[ch], dmask=scol < srow - d * BK)
                  for ch in range(CHAINS)]

        def tile_of(ch, t):
            return (CHAINS * i + ch) * DIAG_TILES - 1 - t

        def more(cs, t):
            go = [jnp.logical_and(alive(cs[ch]), tile_of(ch, t) >= 0) for ch in range(CHAINS)]
            return functools.reduce(jnp.logical_or, go).astype(jnp.int32)

        def step(st):
            t, _, cs = st
            new = []
            for ch in range(CHAINS):
                kb = tile_of(ch, t)
                if ch == CHAINS - 1:
                    new.append(tile(ch, kb, cs[ch]))
                else:
                    new.append(tile(ch, jnp.maximum(kb, 0), cs[ch], valid=(kb >= 0).astype(F32)))
            return t + 1, more(new, t + 1), new

        lax.while_loop(lambda st: st[1] > 0, step, (0, more(cs, 0), cs))
        for ch in range(CHAINS):
            o_ref[pl.ds(ch * BQ, BQ), :] = _unstack_heads(acc_ref[ch])

        @pl.when(jnp.logical_and(p == 3, i == nqb - 1))
        def _():
            finish()

    blk = pl.BlockSpec((CHAINS * BQ, LANES), lambda p, i: (i, p))
    res = pl.pallas_call(
        body, name=name, grid=(4, nqb),
        in_specs=[blk, pl.BlockSpec((S, LANES), lambda p, i: (0, GROUP_COLS + p)),
                  pl.BlockSpec((S, LANES), lambda p, i: (0, 2 * GROUP_COLS + p))] + x_in_specs,
        out_specs=[blk] + x_out_specs,
        out_shape=[jax.ShapeDtypeStruct((S, MERGED_WIDTH), F32)] + x_out,
        scratch_shapes=[pltpu.VMEM((CHAINS, 2 * BQ, LANES), F32)] + exchange.SCRATCH,
        compiler_params=_params(("arbitrary", "arbitrary")),
    )(proj, proj, proj, *x_in)
    return res


def _sb_bwd(proj, merged, dmerged, name, exchange, operand):
    S = proj.shape[0]
    nqb = S // (CHAINS * BQ)
    x_in, x_in_specs, x_out, x_out_specs, pick = _side_exchange(exchange, operand, 5, 3)

    def body(*refs):
        q_ref, k_ref, v_ref, o_ref, do_ref = refs[:5]
        dq_ref, dk_hbm, dv_hbm = refs[5 + len(x_in):8 + len(x_in)]
        dq_acc, dk_acc, dv_acc, sem = refs[8 + len(x_in) + len(x_out):12 + len(x_in) + len(x_out)]
        start, finish = pick(refs)
        p = pl.program_id(0)
        i = pl.program_id(1)

        @pl.when(jnp.logical_and(p == 0, i == 0))
        def _():
            start()

        @pl.when(i == 0)
        def _():
            dk_acc[...] = jnp.zeros_like(dk_acc)
            dv_acc[...] = jnp.zeros_like(dv_acc)

        rows = [pl.ds(ch * BQ, BQ) for ch in range(CHAINS)]
        q2 = [_head_rows(q_ref.at[rw], SCALE) for rw in rows]
        do2 = [_head_rows(do_ref.at[rw]) for rw in rows]
        tot = [_stack(*_pair_rowsum(do_ref[rw, :].astype(F32) * o_ref[rw, :])) for rw in rows]
        row, col = _tile_iotas()
        tri_gt = (row > col).astype(BF16)
        tri_ge = (row >= col).astype(BF16)
        srow, scol = _stacked_iotas(BQ, BK)
        dq_acc[...] = jnp.zeros_like(dq_acc)

        def tile(ch, kb, st, dmask=None, valid=None):
            masked = dmask is not None
            c, r = st
            r0 = pl.multiple_of(kb * BK, BK)
            kblk = k_ref[pl.ds(r0, BK), :]
            vblk = v_ref[pl.ds(r0, BK), :]
            z = _dot_nt(q2[ch], kblk)
            sp, e = _softplus_parts(z)
            lm = -sp
            if masked:
                lm = jnp.where(dmask, lm, 0.0)
            btw = _split_dot(lm, tri_gt)
            w = jnp.exp((z - sp) + btw + c)
            if masked:
                w = jnp.where(dmask, w, 0.0)
            if valid is not None:
                w = w * valid
            wb = w.astype(BF16)
            a = wb.astype(F32) * _dot_nt(do2[ch], vblk)
            suffix = _split_dot(a, tri_ge) + r
            rcp = 1.0 / (1.0 + e)
            pos = z >= 0.0
            sig = jnp.where(pos, rcp, e * rcp)
            sig_neg = jnp.where(pos, e * rcp, rcp)
            dz = a * sig_neg - (tot[ch] - suffix) * sig
            if masked:
                dz = jnp.where(dmask, dz, 0.0)
            if valid is not None:
                dz = dz * valid
            dzb = dz.astype(BF16)
            dq_acc[ch] += _dot(dzb, kblk)
            dk_acc[pl.ds(r0, BK), :] += _dot_tn(dzb, q2[ch])
            dv_acc[pl.ds(r0, BK), :] += _dot_tn(wb, do2[ch])
            return c + btw[:, 0:1] + lm[:, 0:1], suffix[:, 0:1]

        def alive(st):
            return jnp.max(st[0]) > UNDERFLOW_BOUND

        zero = jnp.zeros((2 * BQ, 1), F32)
        sts = [(zero, zero)] * CHAINS
        for d in reversed(range(DIAG_TILES)):
            sts = [tile(ch, (CHAINS * i + ch) * DIAG_TILES + d, sts[ch], dmask=scol < srow - d * BK)
                   for ch in range(CHAINS)]

        def tile_of(ch, t):
            return (CHAINS * i + ch) * DIAG_TILES - 1 - t

        def more(sts, t):
            go = [jnp.logical_and(alive(sts[ch]), tile_of(ch, t) >= 0) for ch in range(CHAINS)]
            return functools.reduce(jnp.logical_or, go).astype(jnp.int32)

        def step(s):
            t, _, sts = s
            new = []
            for ch in range(CHAINS):
                kb = tile_of(ch, t)
                if ch == CHAINS - 1:
                    new.append(tile(ch, kb, sts[ch]))
                else:
                    new.append(tile(ch, jnp.maximum(kb, 0), sts[ch], valid=(kb >= 0).astype(F32)))
            return t + 1, more(new, t + 1), new

        lax.while_loop(lambda s: s[1] > 0, step, (0, more(sts, 0), sts))
        for ch in range(CHAINS):
            dq_ref[rows[ch], :] = (_unstack_heads(dq_acc[ch]) * SCALE).astype(dq_ref.dtype)

        @pl.when(i == nqb - 1)
        def _():
            ck = pltpu.make_async_copy(dk_acc, dk_hbm.at[p], sem.at[0])
            cv = pltpu.make_async_copy(dv_acc, dv_hbm.at[p], sem.at[1])
            ck.start()
            cv.start()
            ck.wait()
            cv.wait()

        @pl.when(jnp.logical_and(p == 3, i == nqb - 1))
        def _():
            finish()

    blk = lambda off: pl.BlockSpec((CHAINS * BQ, LANES), lambda p, i: (i, off + p))
    slab = lambda off: pl.BlockSpec((S, LANES), lambda p, i: (0, off + p))
    return pl.pallas_call(
        body, name=name, grid=(4, nqb),
        in_specs=[blk(0), slab(GROUP_COLS), slab(2 * GROUP_COLS), blk(0), blk(0)] + x_in_specs,
        out_specs=[blk(0), ANY, ANY] + x_out_specs,
        out_shape=[jax.ShapeDtypeStruct((S, MIX_WIDTH), BF16),
                   jax.ShapeDtypeStruct((4, S, LANES), F32), jax.ShapeDtypeStruct((4, S, LANES), F32)] + x_out,
        scratch_shapes=[pltpu.VMEM((CHAINS, 2 * BQ, LANES), F32), pltpu.VMEM((S, LANES), F32),
                        pltpu.VMEM((S, LANES), F32), pltpu.SemaphoreType.DMA((2,))]
        + exchange.SCRATCH,
        compiler_params=_params(("arbitrary", "arbitrary")),
    )(proj, proj, proj, merged, dmerged, *x_in)


def _key_norm_max(k_ref, knorm_ref, nkb):
    def step(kb, m):
        r0 = pl.multiple_of(kb * BK, BK)
        blk = k_ref[pl.ds(r0, BK), :].astype(F32)
        sa, sb = _pair_rowsum(blk * blk)
        return (jnp.maximum(m[0], jnp.max(sa, axis=0, keepdims=True)),
                jnp.maximum(m[1], jnp.max(sb, axis=0, keepdims=True)))

    zero = jnp.zeros((1, 1), F32)
    ma, mb = lax.fori_loop(0, nkb, step, (zero, zero))
    knorm_ref[...] = _select_pair(jnp.broadcast_to(ma, (1, LANES)), jnp.broadcast_to(mb, (1, LANES)))


FQ = 512
FK = FQ
GATE_BLOCKS = FK // BK


def _key_gates(cr_ref, kb):
    blocks = [cr_ref[0, GATE_BLOCKS * kb + j] for j in range(GATE_BLOCKS)]
    per_head = [jnp.broadcast_to(jnp.concatenate([b[h:h + 1] for b in blocks], axis=1), (FQ, FK)) for h in range(2)]
    return _stack(*per_head)


def _last_gate(cr_ref, kb):
    last = cr_ref[0, GATE_BLOCKS * jnp.maximum(kb, 0) + GATE_BLOCKS - 1]
    return _stack(*[jnp.broadcast_to(last[h:h + 1, BK - 1:BK], (FQ, 1)) for h in range(2)])


def _logit_bound(q_ref, knorm_ref):
    qf = q_ref[...].astype(F32) * SCALE
    qa, qb = _pair_rowsum(qf * qf)
    kn = knorm_ref[...]
    return _stack(jnp.sqrt(qa * kn[:, 0:1]), jnp.sqrt(qb * kn[:, HEAD_DIM:HEAD_DIM + 1]))


def _causal_bias(bias_ref):
    srow, scol = _stacked_iotas(FQ, FK)
    bias_ref[...] = jnp.where(scol <= srow, 0.0, NEG_INF)


def _fox_fwd(proj, kv, c_col, c_row, name):
    S = proj.shape[0]
    nqb = S // FQ

    def body(q_ref, k_ref, v_ref, cc_ref, cr_ref, o_ref, lse_ref, acc_ref, knorm_ref, bias_ref):
        i = pl.program_id(1)

        @pl.when(i == 0)
        def _():
            _key_norm_max(k_ref, knorm_ref, S // BK)
            _causal_bias(bias_ref)

        q2 = _head_rows(q_ref, SCALE)
        bound = _logit_bound(q_ref, knorm_ref)
        cc = cc_ref[0]
        ct = _stack(cc[:, 0:1], cc[:, 1:2])
        acc_ref[...] = jnp.zeros_like(acc_ref)

        def tile(kb, st, on_diagonal):
            m, l = st
            r0 = pl.multiple_of(kb * FK, FK)
            kblk = k_ref[pl.ds(r0, FK), :]
            vblk = v_ref[pl.ds(r0, FK), :]
            z = _dot_nt(q2, kblk) + ct - _key_gates(cr_ref, kb) + bias_ref[...] * on_diagonal
            m_new = jnp.maximum(m, jnp.max(z, axis=1, keepdims=True))
            alpha = jnp.exp(m - m_new)
            pr = jnp.exp(z - m_new)
            acc_ref[...] = alpha * acc_ref[...] + _split_dot(pr, vblk)
            return m_new, alpha * l + jnp.sum(pr, axis=1, keepdims=True)

        def alive(kb, st):
            reach = bound + ct - _last_gate(cr_ref, kb) - st[0]
            return (jnp.max(reach) > UNDERFLOW_BOUND).astype(jnp.int32)

        neg = jnp.full((2 * FQ, 1), NEG_INF, F32)
        zero = jnp.zeros((2 * FQ, 1), F32)
        def cond(s):
            return jnp.logical_and(s[0] >= 0, s[1] > 0)

        def step(s):
            kb, _, st = s
            st = tile(kb, st, (kb == i).astype(F32))
            return kb - 1, alive(kb - 1, st), st

        _, _, (m, l) = lax.while_loop(cond, step, (i, jnp.int32(1), (neg, zero)))
        o_ref[...] = _unstack_heads(acc_ref[...] / l)
        lse = m + jnp.log(l)
        lse_ref[0] = _two_cols(lse[:FQ], lse[FQ:])

    return pl.pallas_call(
        body, name=name, grid=(4, nqb),
        in_specs=[pl.BlockSpec((FQ, LANES), lambda p, i: (i, p)),
                  pl.BlockSpec((S, LANES), lambda p, i: (0, p)),
                  pl.BlockSpec((S, LANES), lambda p, i: (0, GROUP_COLS + p)),
                  pl.BlockSpec((1, FQ, 2), lambda p, i: (p, i, 0)),
                  pl.BlockSpec((1, S // BK, 8, LANES), lambda p, i: (p, 0, 0, 0))],
        out_specs=[pl.BlockSpec((FQ, LANES), lambda p, i: (i, p)),
                   pl.BlockSpec((1, FQ, 2), lambda p, i: (p, i, 0))],
        out_shape=[jax.ShapeDtypeStruct((S, MERGED_WIDTH), F32), jax.ShapeDtypeStruct((4, S, 2), F32)],
        scratch_shapes=[pltpu.VMEM((2 * FQ, LANES), F32), pltpu.VMEM((1, LANES), F32),
                        pltpu.VMEM((2 * FQ, FK), F32)],
        compiler_params=_params(("arbitrary", "arbitrary")),
    )(proj, kv, kv, c_col, c_row)


def _fox_bwd(proj, kv, c_col, c_row, lse, merged, dmerged, dk_prev, dv_prev, dc_prev, name):
    S = proj.shape[0]
    nqb = S // FQ

    def body(q_ref, k_ref, v_ref, cc_ref, cr_ref, lse_ref, o_ref, do_ref, dkp_hbm, dvp_hbm, dcp_ref,
             dq_ref, dk_hbm, dv_hbm, dc_ref, dq_acc, dk_acc, dv_acc, knorm_ref, bias_ref, sem):
        p = pl.program_id(0)
        i = pl.program_id(1)

        @pl.when(i == 0)
        def _():
            ck = pltpu.make_async_copy(dkp_hbm.at[p], dk_acc, sem.at[0])
            cv = pltpu.make_async_copy(dvp_hbm.at[p], dv_acc, sem.at[1])
            ck.start()
            cv.start()
            dc_ref[...] = dcp_ref[...]
            _key_norm_max(k_ref, knorm_ref, S // BK)
            _causal_bias(bias_ref)
            ck.wait()
            cv.wait()

        q2 = _head_rows(q_ref, SCALE)
        do2 = _head_rows(do_ref)
        tot = _stack(*_pair_rowsum(do_ref[...].astype(F32) * o_ref[...]))
        bound = _logit_bound(q_ref, knorm_ref)
        cc = cc_ref[0]
        ct = _stack(cc[:, 0:1], cc[:, 1:2])
        ls = lse_ref[0]
        lse = _stack(ls[:, 0:1], ls[:, 1:2])
        sub = lax.broadcasted_iota(jnp.int32, (8, LANES), 0)
        dq_acc[...] = jnp.zeros_like(dq_acc)

        def tile(kb, masked):
            r0 = pl.multiple_of(kb * FK, FK)
            kblk = k_ref[pl.ds(r0, FK), :]
            vblk = v_ref[pl.ds(r0, FK), :]
            z = _dot_nt(q2, kblk) + ct - _key_gates(cr_ref, kb)
            if masked:
                z = z + bias_ref[...]
            pr = jnp.exp(z - lse)
            ds = pr * (_dot_nt(do2, vblk) - tot)
            dsb = ds.astype(BF16)
            dq_acc[...] += _dot(dsb, kblk)
            dk_acc[pl.ds(r0, FK), :] += _dot_tn(dsb, q2)
            dv_acc[pl.ds(r0, FK), :] += _dot_tn(pr.astype(BF16), do2)
            dca = jnp.sum(ds[:FQ], axis=0, keepdims=True)
            dcb = jnp.sum(ds[FQ:], axis=0, keepdims=True)
            for j in range(GATE_BLOCKS):
                cols = slice(j * BK, (j + 1) * BK)
                old = dc_ref[0, GATE_BLOCKS * kb + j]
                dc_ref[0, GATE_BLOCKS * kb + j] = jnp.where(sub == 0, old - dca[:, cols],
                                                            jnp.where(sub == 1, old - dcb[:, cols], old))

        def alive(kb):
            reach = bound + ct - _last_gate(cr_ref, kb) - lse
            return (jnp.max(reach) > UNDERFLOW_BOUND).astype(jnp.int32)

        tile(i, True)

        def cond(s):
            return jnp.logical_and(s[0] >= 0, s[1] > 0)

        def step(s):
            kb, _ = s
            tile(kb, False)
            return kb - 1, alive(kb - 1)

        lax.while_loop(cond, step, (i - 1, alive(i - 1)))
        dq_ref[...] = (_unstack_heads(dq_acc[...]) * SCALE).astype(dq_ref.dtype)

        @pl.when(i == nqb - 1)
        def _():
            ck = pltpu.make_async_copy(dk_acc, dk_hbm.at[p], sem.at[0])
            cv = pltpu.make_async_copy(dv_acc, dv_hbm.at[p], sem.at[1])
            ck.start()
            cv.start()
            ck.wait()
            cv.wait()

    blk = lambda off: pl.BlockSpec((FQ, LANES), lambda p, i: (i, off + p))
    slab = lambda off: pl.BlockSpec((S, LANES), lambda p, i: (0, off + p))
    cols = pl.BlockSpec((1, FQ, 2), lambda p, i: (p, i, 0))
    rows = pl.BlockSpec((1, S // BK, 8, LANES), lambda p, i: (p, 0, 0, 0))
    return pl.pallas_call(
        body, name=name, grid=(4, nqb),
        in_specs=[blk(0), slab(0), slab(GROUP_COLS), cols, rows, cols, blk(0), blk(0), ANY, ANY, rows],
        out_specs=[blk(0), ANY, ANY, rows],
        out_shape=[jax.ShapeDtypeStruct((S, MIX_WIDTH), BF16),
                   jax.ShapeDtypeStruct((4, S, LANES), F32), jax.ShapeDtypeStruct((4, S, LANES), F32),
                   jax.ShapeDtypeStruct((4, S // BK, 8, LANES), F32)],
        scratch_shapes=[pltpu.VMEM((2 * FQ, LANES), F32), pltpu.VMEM((S, LANES), F32),
                        pltpu.VMEM((S, LANES), F32), pltpu.VMEM((1, LANES), F32),
                        pltpu.VMEM((2 * FQ, FK), F32), pltpu.SemaphoreType.DMA((2,))],
        compiler_params=_params(("arbitrary", "arbitrary")),
    )(proj, kv, kv, c_col, c_row, lse, merged, dmerged, dk_prev, dv_prev, dc_prev)


def _lane_scan(x, reverse):
    lane = lax.broadcasted_iota(jnp.int32, x.shape, 1)
    d = 1
    while d < LANES:
        if reverse:
            x = x + jnp.where(lane < LANES - d, pltpu.roll(x, LANES - d, 1), 0.0)
        else:
            x = x + jnp.where(lane >= d, pltpu.roll(x, d, 1), 0.0)
        d *= 2
    return x


def _gate_fwd(fl3, b8):
    nb = fl3.shape[0]

    def body(fl_ref, b_ref, c_ref):
        def step(kb, carry):
            x = fl_ref[kb] + b_ref[...]
            sp, _ = _softplus_parts(-x)
            c = _lane_scan(-sp, False) + carry
            c_ref[kb] = c
            return c[:, LANES - 1:LANES]

        lax.fori_loop(0, nb, step, jnp.zeros((8, 1), F32))

    return pl.pallas_call(body, name="forget_gate_cumsum",
                          out_shape=jax.ShapeDtypeStruct(fl3.shape, F32),
                          compiler_params=_params())(fl3, b8)


def _gate_bwd(dc3, fl3, b8):
    nb = fl3.shape[0]

    def body(dc_ref, fl_ref, b_ref, dfl_ref, db_ref):
        def step(t, st):
            carry, dbs = st
            kb = nb - 1 - t
            g = _lane_scan(dc_ref[kb], True) + carry
            x = fl_ref[kb] + b_ref[...]
            e = jnp.exp(-jnp.abs(x))
            rcp = 1.0 / (1.0 + e)
            dfl = g * jnp.where(x >= 0.0, e * rcp, rcp)
            dfl_ref[kb] = dfl
            return g[:, 0:1], dbs + dfl

        _, dbs = lax.fori_loop(0, nb, step, (jnp.zeros((8, 1), F32), jnp.zeros((8, LANES), F32)))
        db_ref[...] = jnp.broadcast_to(jnp.sum(dbs, axis=1, keepdims=True), (8, LANES))

    return pl.pallas_call(body, name="forget_gate_bwd",
                          out_shape=[jax.ShapeDtypeStruct(fl3.shape, F32), jax.ShapeDtypeStruct((8, LANES), F32)],
                          compiler_params=_params())(dc3, fl3, b8)


MEM_TQ = 512
MEM_COLS = MEM_WIDTH // LANES


def _mem_fwd(proj, qcol, mkv, mix, name):
    S = proj.shape[0]
    M = mkv.shape[0]

    def body(q_ref, mk_ref, mv_ref, mix_ref, o_ref, lse_ref):
        q2 = _head_rows(q_ref, SCALE)
        s = _dot_nt(q2, mk_ref[...])
        m = jnp.max(s, axis=1, keepdims=True)
        pr = jnp.exp(s - m)
        l = jnp.sum(pr, axis=1, keepdims=True)
        o_ref[...] = _unstack_heads(_dot(pr.astype(BF16), mv_ref[...]) / l)
        lse = m + jnp.log(l)
        lse_ref[0] = _two_cols(lse[:MEM_TQ], lse[MEM_TQ:])

    return pl.pallas_call(
        body, name=name, grid=(MEM_COLS, S // MEM_TQ),
        in_specs=[pl.BlockSpec((MEM_TQ, LANES), lambda p, i: (i, qcol + p)),
                  pl.BlockSpec((M, LANES), lambda p, i: (0, p)),
                  pl.BlockSpec((M, LANES), lambda p, i: (0, MEM_COLS + p)), ANY],
        out_specs=[pl.BlockSpec((MEM_TQ, LANES), lambda p, i: (i, GROUP_COLS + p)),
                   pl.BlockSpec((1, MEM_TQ, 2), lambda p, i: (p, i, 0))],
        out_shape=[jax.ShapeDtypeStruct((S, MERGED_WIDTH), F32), jax.ShapeDtypeStruct((2, S, 2), F32)],
        input_output_aliases={3: 0},
        compiler_params=_params(("parallel", "parallel")),
    )(proj, mkv, mkv, mix)


def _mem_bwd(proj, qcol, mkv, lse, merged, dmerged, name):
    S = proj.shape[0]
    M = mkv.shape[0]

    def body(q_ref, mk_ref, mv_ref, lse_ref, o_ref, do_ref, dq_ref, dmk_ref, dmv_ref):
        @pl.when(pl.program_id(1) == 0)
        def _():
            dmk_ref[...] = jnp.zeros_like(dmk_ref)
            dmv_ref[...] = jnp.zeros_like(dmv_ref)

        q2 = _head_rows(q_ref, SCALE)
        do2 = _head_rows(do_ref)
        tot = _stack(*_pair_rowsum(do_ref[...].astype(F32) * o_ref[...]))
        ls = lse_ref[0]
        pr = jnp.exp(_dot_nt(q2, mk_ref[...]) - _stack(ls[:, 0:1], ls[:, 1:2]))
        ds = pr * (_dot_nt(do2, mv_ref[...]) - tot)
        dsb = ds.astype(BF16)
        dmk_ref[...] += _dot_tn(dsb, q2)
        dmv_ref[...] += _dot_tn(pr.astype(BF16), do2)
        dq_ref[...] = (_unstack_heads(_dot(dsb, mk_ref[...])) * SCALE).astype(dq_ref.dtype)

    blk = lambda off: pl.BlockSpec((MEM_TQ, LANES), lambda p, i: (i, off + p))
    acc = pl.BlockSpec((M, LANES), lambda p, i: (0, p))
    return pl.pallas_call(
        body, name=name, grid=(MEM_COLS, S // MEM_TQ),
        in_specs=[blk(qcol), pl.BlockSpec((M, LANES), lambda p, i: (0, p)),
                  pl.BlockSpec((M, LANES), lambda p, i: (0, MEM_COLS + p)),
                  pl.BlockSpec((1, MEM_TQ, 2), lambda p, i: (p, i, 0)), blk(GROUP_COLS), blk(GROUP_COLS)],
        out_specs=[blk(0), acc, acc],
        out_shape=[jax.ShapeDtypeStruct((S, MEM_WIDTH), BF16), jax.ShapeDtypeStruct((M, MEM_WIDTH), F32),
                   jax.ShapeDtypeStruct((M, MEM_WIDTH), F32)],
        compiler_params=_params(("parallel", "arbitrary")),
    )(proj, mkv, mkv, lse, merged, dmerged)


def _c_layouts(c3):
    nb = c3.shape[0]
    pairs = c3.reshape(nb, 4, 2, LANES).transpose(1, 0, 2, 3)
    c_row = jnp.pad(pairs, ((0, 0), (0, 0), (0, 6), (0, 0)))
    c_col = pairs.transpose(0, 1, 3, 2).reshape(4, nb * LANES, 2)
    return c_col, c_row


def _local_step(x, mem, wb, shards, sm, loss_target):
    S = x.shape[0]
    nb = S // BK
    vec = lambda a: a.reshape(1, D_MODEL)
    b8 = jnp.broadcast_to(sm["b_f"].reshape(8, 1), (8, LANES))

    saved = []
    shared = None
    h = x
    hn = _rms_fwd(h, vec(sm["norm1_g"][0]), "norm1_0")
    for l in range(DEPTH):
        if l == N_A:
            w_kvf = jnp.pad(wb["w_kv_shared"], ((0, 0), (0, KVF_WIDTH - W_KV_SHARED)))
            hs = _rms_fwd(h, vec(sm["kv_norm_g"]), "kv_norm")
            kvf = _mm(hs, w_kvf, out_dtype=F32, name="kv_shared_proj")
            kv = kvf[:, :2 * MIX_WIDTH].astype(BF16)
            fl3 = kvf[:, 2 * MIX_WIDTH:2 * MIX_WIDTH + 8].T.reshape(8, nb, LANES).transpose(1, 0, 2)
            c3 = _gate_fwd(fl3, b8)
            c_col, c_row = _c_layouts(c3)
            shared = dict(h=h, hs=hs, kv=kv, fl3=fl3, c_col=c_col, c_row=c_row)
        mn = _rms_fwd(mem, vec(sm["mem_norm_g"][l]), f"mem_norm_{l}")
        mkv = _mm(mn, wb["w_mem_kv"][l], name=f"mem_kv_proj_{l}")
        if l < N_A:
            w_in = wb["w_in_a"][l]
            proj = _mm(hn, w_in, name=f"in_proj_{l}")
            mix, gathered = _sb_fwd(proj, f"stickbreak_fwd_{l}", _AllGather, shards[l])
            _unpack_gathered(PARTS[1 + l], gathered, wb)
            lse, qcol = None, Q_MEM_COL_A
        else:
            w_in = wb["w_in_b"][l - N_A]
            proj = _mm(hn, w_in, name=f"in_proj_{l}")
            mix, lse = _fox_fwd(proj, shared["kv"], shared["c_col"], shared["c_row"], f"fox_fwd_{l}")
            qcol = Q_MEM_COL_B
        merged, mlse = _mem_fwd(proj, qcol, mkv, mix, f"mem_attn_fwd_{l}")
        h_mid, hn2 = _mm(merged, wb["w_o"][l], epi="add_rms", extra=(h, vec(sm["norm2_g"][l])),
                         name=f"out_proj_{l}")
        u, act = _mm(hn2, wb["w_mlp1"][l], epi="relu2", name=f"mlp1_{l}")
        saved.append(dict(h=h, hn=hn, mn=mn, mkv=mkv, proj=proj, lse=lse, mlse=mlse, qcol=qcol, merged=merged,
                          h_mid=h_mid, hn2=hn2, u=u, act=act, w_in=w_in))
        if l + 1 < DEPTH:
            h, hn = _mm(act, wb["w_mlp2"][l], epi="add_rms", extra=(h_mid, vec(sm["norm1_g"][l + 1])),
                        name=f"mlp2_{l}")
        else:
            h = _mm(act, wb["w_mlp2"][l], out_dtype=F32, epi="add", extra=h_mid, name=f"mlp2_{l}")

    loss, dh, dhb, dg_final = _final_loss(h, vec(sm["final_norm_g"]), loss_target)

    gb = {n: [None] * (DEPTH if n not in ("w_in_a", "w_in_b") else 2) for n in
          ("w_in_a", "w_in_b", "w_mem_kv", "w_o", "w_mlp1", "w_mlp2")}
    gs = {n: [None] * DEPTH for n in ("norm1_g", "mem_norm_g", "norm2_g")}
    received = [None] * N_A
    dk_sh = jnp.zeros((4, S, LANES), F32)
    dv_sh = jnp.zeros((4, S, LANES), F32)
    dc_sh = jnp.zeros((4, nb, 8, LANES), F32)
    for l in reversed(range(DEPTH)):
        sv = saved[l]
        du = _mm(dhb, wb["w_mlp2"][l], mode="nt", epi="drelu2", extra=sv["u"], name=f"mlp2_dx_{l}")
        gb["w_mlp2"][l] = _mm(sv["act"], dhb, mode="tn", out_dtype=F32, name=f"mlp2_dw_{l}")
        gb["w_mlp1"][l] = _mm(sv["hn2"], du, mode="tn", out_dtype=F32, name=f"mlp1_dw_{l}")
        dh, dhb, gs["norm2_g"][l] = _mm(du, wb["w_mlp1"][l], mode="nt", epi="rms_bwd",
                                        extra=(sv["h_mid"], vec(sm["norm2_g"][l]), dh),
                                        name=f"mlp1_dx_norm2_bwd_{l}")
        dmerged = _mm(dhb, wb["w_o"][l], mode="nt", name=f"out_proj_dx_{l}")
        gb["w_o"][l] = _mm(sv["merged"], dhb, mode="tn", out_dtype=F32, name=f"out_proj_dw_{l}")
        if l < N_A:
            ready = _pack_grads(PARTS[1 + l], PART_ROWS[1 + l], gb, None)
            dq, dk, dv, received[l] = _sb_bwd(sv["proj"], sv["merged"], dmerged, f"stickbreak_bwd_{l}",
                                              _Scatter, ready)
        else:
            dq, dk_sh, dv_sh, dc_sh = _fox_bwd(sv["proj"], shared["kv"], shared["c_col"], shared["c_row"],
                                               sv["lse"], sv["merged"], dmerged, dk_sh, dv_sh, dc_sh,
                                               f"fox_bwd_{l}")
        dqm, dmk, dmv = _mem_bwd(sv["proj"], sv["qcol"], sv["mkv"], sv["mlse"], sv["merged"], dmerged,
                                 f"mem_attn_bwd_{l}")
        if l < N_A:
            flat = lambda t: t.transpose(1, 0, 2).reshape(S, MIX_WIDTH).astype(BF16)
            dproj = jnp.concatenate([dq, flat(dk), flat(dv), dqm], axis=1)
        else:
            dproj = jnp.concatenate([dq, dqm], axis=1)
        name_in = "w_in_a" if l < N_A else "w_in_b"
        gb[name_in][l if l < N_A else l - N_A] = _mm(sv["hn"], dproj, mode="tn", out_dtype=F32,
                                                      name=f"in_proj_dw_{l}")
        dh, dhb, gs["norm1_g"][l] = _mm(dproj, sv["w_in"], mode="nt", epi="rms_bwd",
                                        extra=(sv["h"], vec(sm["norm1_g"][l]), dh),
                                        name=f"in_proj_dx_norm1_bwd_{l}")
        dmkv = jnp.concatenate([dmk, dmv], axis=1)
        gb["w_mem_kv"][l] = _mm(sv["mn"], dmkv, mode="tn", out_dtype=F32, name=f"mem_kv_dw_{l}")
        dmn = _mm(dmkv, wb["w_mem_kv"][l], mode="nt", out_dtype=F32, name=f"mem_kv_dx_{l}")
        gs["mem_norm_g"][l] = _rms_gain_grad(mem, dmn, f"mem_norm_bwd_{l}")
        if l == N_A:
            dfl3, db8 = _gate_bwd(dc_sh.reshape(4, nb, 8, LANES)[:, :, :2].transpose(1, 0, 2, 3).reshape(nb, 8, LANES),
                                  shared["fl3"], b8)
            dfl = dfl3.transpose(1, 0, 2).reshape(8, S).T
            flat = lambda t: t.transpose(1, 0, 2).reshape(S, MIX_WIDTH).astype(BF16)
            dkvf = jnp.concatenate([flat(dk_sh), flat(dv_sh),
                                    jnp.pad(dfl, ((0, 0), (0, LANES - 8))).astype(BF16)], axis=1)
            gb["w_kv_shared"] = _mm(shared["hs"], dkvf, mode="tn", out_dtype=F32, name="kv_shared_dw")[:, :W_KV_SHARED]
            dh, dhb, g_kvn = _mm(dkvf, w_kvf, mode="nt", epi="rms_bwd",
                                 extra=(shared["h"], vec(sm["kv_norm_g"]), dh), name="kv_shared_dx_norm_bwd")
            g_bf = db8[:, 0]

    gsmall = {n: jnp.concatenate(v, axis=0) for n, v in gs.items()}
    gsmall["kv_norm_g"] = g_kvn
    gsmall["final_norm_g"] = dg_final
    gsmall["b_f"] = g_bf
    return loss, dh, gb, gsmall, received


def kernel(x, mem, norm1_g, w_in_a, w_in_b, w_mem_kv, mem_norm_g, w_o, norm2_g, w_mlp1, w_mlp2, kv_norm_g, w_kv_shared, b_f, final_norm_g, loss_target, m_norm1_g, m_w_in_a, m_w_in_b, m_w_mem_kv, m_mem_norm_g, m_w_o, m_norm2_g, m_w_mlp1, m_w_mlp2, m_kv_norm_g, m_w_kv_shared, m_b_f, m_final_norm_g, v_norm1_g, v_w_in_a, v_w_in_b, v_w_mem_kv, v_mem_norm_g, v_w_o, v_norm2_g, v_w_mlp1, v_w_mlp2, v_kv_norm_g, v_w_kv_shared, v_b_f, v_final_norm_g):
    big_w = dict(w_in_a=w_in_a, w_in_b=w_in_b, w_mem_kv=w_mem_kv, w_o=w_o, w_mlp1=w_mlp1, w_mlp2=w_mlp2,
                 w_kv_shared=w_kv_shared)
    small_w = dict(norm1_g=norm1_g, mem_norm_g=mem_norm_g, norm2_g=norm2_g, kv_norm_g=kv_norm_g,
                   final_norm_g=final_norm_g, b_f=b_f)
    big_m = dict(w_in_a=m_w_in_a, w_in_b=m_w_in_b, w_mem_kv=m_w_mem_kv, w_o=m_w_o, w_mlp1=m_w_mlp1,
                 w_mlp2=m_w_mlp2, w_kv_shared=m_w_kv_shared)
    small_m = dict(norm1_g=m_norm1_g, mem_norm_g=m_mem_norm_g, norm2_g=m_norm2_g, kv_norm_g=m_kv_norm_g,
                   final_norm_g=m_final_norm_g, b_f=m_b_f)
    big_v = dict(w_in_a=v_w_in_a, w_in_b=v_w_in_b, w_mem_kv=v_w_mem_kv, w_o=v_w_o, w_mlp1=v_w_mlp1,
                 w_mlp2=v_w_mlp2, w_kv_shared=v_w_kv_shared)
    small_v = dict(norm1_g=v_norm1_g, mem_norm_g=v_mem_norm_g, norm2_g=v_norm2_g, kv_norm_g=v_kv_norm_g,
                   final_norm_g=v_final_norm_g, b_f=v_b_f)

    def pack(k, big, small, dtype):
        return _pack_local(PARTS[k], PART_ROWS[k], big, small if k == 0 else None, dtype)

    def pack_all(big, small):
        return jnp.concatenate([pack(k, big, small, F32) for k in range(len(PARTS))], axis=0)

    wb = {n: {} for n in BIG_NAMES}
    _unpack_gathered(PARTS[0], _allgather_chips(pack(0, big_w, small_w, BF16)), wb)
    shards = [pack(1 + l, big_w, None, BF16) for l in range(N_A)]

    loss, dx, gb, gsmall, received = _local_step(x[0], mem[0], wb, shards, small_w, loss_target[0])

    received = [_scatter_chips(_pack_grads(PARTS[0], PART_ROWS[0], gb, gsmall))] + received
    part = jnp.concatenate([_sum4(r) for r in received], axis=0)
    other = _swap_cores(part)
    g, delta, new_m, new_v = _adamw(part, other, pack_all(big_w, small_w), pack_all(big_m, small_m),
                                    pack_all(big_v, small_v))

    outs = [lax.psum(loss[0, 0], ("x", "y", "c")), dx[None]]
    for packed in (g, delta, new_m, new_v):
        pieces, d, off = {n: [] for n in BIG_NAMES}, {}, 0
        for k, part_k in enumerate(PARTS):
            _unpack_local(part_k, packed[off:off + PART_ROWS[k]], k == 0, pieces, d)
            off += PART_ROWS[k]
        d.update(_join_layers(pieces))
        outs.extend(d[n] for n in WEIGHT_ORDER)
    return tuple(outs)
```

```python
import functools
import math

import jax
import jax.numpy as jnp
from jax import lax
from jax.experimental import pallas as pl
from jax.experimental.pallas import tpu as pltpu

F32 = jnp.float32
BF16 = jnp.bfloat16

D_MODEL = 1024
HEAD_DIM = 64
MIX_WIDTH = 512
MEM_WIDTH = 256
MERGED_WIDTH = MIX_WIDTH + MEM_WIDTH
DEPTH = 4
N_A = 2
D_FF = 4096
EPS = 1e-6
NEG_INF = -1e30
SCALE = 1.0 / math.sqrt(HEAD_DIM)

ADAM_LR = 0.001
ADAM_B1 = 0.9
ADAM_B2 = 0.999
ADAM_EPS = 1e-08
ADAM_WD = 0.01
ADAM_STEP = 10

LANES = 128
GROUP_COLS = MIX_WIDTH // LANES
Q_MEM_COL_A = 3 * GROUP_COLS
Q_MEM_COL_B = GROUP_COLS
W_KV_SHARED = 2 * MIX_WIDTH + 8
KVF_WIDTH = 1152
BQ = 256
BK = 128
DIAG_TILES = BQ // BK
CHAINS = 8
UNDERFLOW_BOUND = -110.0
VMEM_LIMIT = 56 * 1024 * 1024

MESH = pl.DeviceIdType.MESH
N_CHIPS = 4

PARTS = (
    (("w_in_a", 0, 1, (1024, 448), 1),
     ("w_mem_kv", 0, 1, (256, 512), 0)),
    (("w_o", 0, 1, (768, 256), 1),
     ("w_mlp1", 0, 1, (1024, 1024), 1),
     ("w_mlp2", 0, 1, (1024, 1024), 0),
     ("w_in_a", 1, 2, (1024, 448), 1),
     ("w_mem_kv", 1, 2, (256, 512), 0)),
    (("w_o", 1, 4, (768, 256), 1),
     ("w_mlp1", 1, 4, (1024, 1024), 1),
     ("w_mlp2", 1, 4, (1024, 1024), 0),
     ("w_in_b", 0, 2, (256, 768), 0),
     ("w_mem_kv", 2, 4, (256, 512), 0),
     ("w_kv_shared", None, None, (1024, 258), 1)),
)
BIG_NAMES = ("w_in_a", "w_in_b", "w_mem_kv", "w_o", "w_mlp1", "w_mlp2", "w_kv_shared")
SMALL = (
    ("norm1_g", (4, 1024)),
    ("mem_norm_g", (4, 1024)),
    ("norm2_g", (4, 1024)),
    ("kv_norm_g", (1, 1024)),
    ("final_norm_g", (1, 1024)),
    ("b_f", (1, 1024)),
)
WEIGHT_ORDER = ("norm1_g", "w_in_a", "w_in_b", "w_mem_kv", "mem_norm_g", "w_o", "norm2_g", "w_mlp1",
                "w_mlp2", "kv_norm_g", "w_kv_shared", "b_f", "final_norm_g")


ROW_ALIGN = 16
PACK_TILE = 256
SMALL_ROWS = ROW_ALIGN
assert sum(s[0] for _, s in SMALL) <= SMALL_ROWS


def _section_rows(entry):
    _, lo, hi, shape, _ = entry
    rows = (1 if lo is None else hi - lo) * math.prod(shape) // D_MODEL
    return rows, -(-rows // ROW_ALIGN) * ROW_ALIGN


def _round_up(n, m):
    return -(-n // m) * m


SUM_TILE = 128
_used = [sum(_section_rows(e)[1] for e in part) for part in PARTS]
PART_ROWS = [_round_up(_used[0] + SMALL_ROWS, SUM_TILE), _round_up(_used[1], SUM_TILE)]
PART_ROWS.append(_round_up(_used[2] + sum(PART_ROWS), PACK_TILE) - sum(PART_ROWS))
assert PART_ROWS[2] % SUM_TILE == 0


def _params(sem=None):
    return pltpu.CompilerParams(dimension_semantics=sem, vmem_limit_bytes=VMEM_LIMIT)


def _pick(n, cands):
    for c in cands:
        if n % c == 0:
            return c
    raise ValueError(f"no tile for {n}")


def _section(a, entry):
    a = a.reshape(-1, D_MODEL)
    return jnp.pad(a, ((0, _section_rows(entry)[1] - a.shape[0]), (0, 0)))


def _small_block(small, dtype):
    blk = jnp.zeros((SMALL_ROWS, D_MODEL), dtype)
    off = 0
    for n, shp in SMALL:
        a = small[n].astype(dtype)
        if n == "b_f":
            blk = blk.at[off, :a.size].set(a.reshape(-1))
        else:
            blk = blk.at[off:off + shp[0]].set(a.reshape(shp))
        off += shp[0]
    return blk


def _fill(parts, rows, dtype):
    used = sum(p.shape[0] for p in parts)
    return jnp.concatenate(parts + [jnp.zeros((rows - used, D_MODEL), dtype)], axis=0)


def _pack_local(part, rows, big, small, dtype):
    parts = [_section((big[e[0]] if e[1] is None else big[e[0]][e[1]:e[2]]).astype(dtype), e) for e in part]
    if small is not None:
        parts.append(_small_block(small, dtype))
    return _fill(parts, rows, dtype)


def _unpack_local(part, p, with_small, pieces, small):
    off = 0
    for e in part:
        n, lo, hi, shp, _ = e
        rows, reserved = _section_rows(e)
        pieces[n].append((lo, p[off:off + rows].reshape(shp if lo is None else (hi - lo,) + shp)))
        off += reserved
    if with_small:
        for n, shp in SMALL:
            a = p[off:off + shp[0]]
            small[n] = a[0, :8] if n == "b_f" else (a.reshape(D_MODEL) if shp[0] == 1 else a)
            off += shp[0]


def _join_layers(pieces):
    out = {}
    for n, ps in pieces.items():
        ps = sorted(ps, key=lambda t: -1 if t[0] is None else t[0])
        out[n] = ps[0][1] if len(ps) == 1 else jnp.concatenate([a for _, a in ps], axis=0)
    return out


def _unpack_gathered(part, g, weights):
    off = 0
    for e in part:
        n, lo, hi, shp, ax = e
        rows, reserved = _section_rows(e)
        if lo is None:
            sec = g[:, off:off + rows].reshape((N_CHIPS,) + shp)
            weights[n] = jnp.concatenate([sec[j] for j in range(N_CHIPS)], axis=ax)
        else:
            sec = g[:, off:off + rows].reshape((N_CHIPS, hi - lo) + shp)
            for l in range(lo, hi):
                weights[n][l] = jnp.concatenate([sec[j, l - lo] for j in range(N_CHIPS)], axis=ax)
        off += reserved


def _pack_grads(part, rows, gbig, gsmall):
    small = None if gsmall is None else _small_block(gsmall, BF16)
    chunks = []
    for j in range(N_CHIPS):
        parts = []
        for e in part:
            n, lo, hi, shp, ax = e
            w = shp[ax]
            layers = [gbig[n]] if lo is None else [gbig[n][l] for l in range(lo, hi)]
            cut = [lax.slice_in_dim(g, j * w, (j + 1) * w, axis=ax).astype(BF16).reshape(-1, D_MODEL) for g in layers]
            parts.append(_section(cut[0] if len(cut) == 1 else jnp.concatenate(cut, axis=0), e))
        if small is not None:
            parts.append(small)
        chunks.append(_fill(parts, rows, BF16))
    return jnp.stack(chunks, axis=0)


ANY = pl.BlockSpec(memory_space=pl.ANY)


def _other_chips(x, y):
    return [(1 - x, y), (x, 1 - y), (1 - x, 1 - y)]


class _AllGather:
    SCRATCH = [pltpu.SemaphoreType.DMA((3,)), pltpu.SemaphoreType.DMA((3,)), pltpu.SemaphoreType.DMA((3,)),
               pltpu.SemaphoreType.DMA((3,)), pltpu.SemaphoreType.DMA]

    def __init__(self, w_ref, o_ref, send_sems, recv_sems, pass_send, pass_recv, local_sem):
        self.w_ref, self.o_ref = w_ref, o_ref
        self.sems = (send_sems, recv_sems, pass_send, pass_recv, local_sem)
        x, y, c = lax.axis_index("x"), lax.axis_index("y"), lax.axis_index("c")
        half = w_ref.shape[0] // 2
        self.c, self.me, self.sibling = c, 2 * x + y, (x, y, 1 - c)
        self.mine = pl.ds(pl.multiple_of(c * half, ROW_ALIGN), half)
        self.other = pl.ds(pl.multiple_of((1 - c) * half, ROW_ALIGN), half)
        self.chips = _other_chips(x, y)

    def _over_ici(self, j, rows_of):
        chip = self.chips[j]
        return pltpu.make_async_remote_copy(
            src_ref=self.w_ref.at[self.mine], dst_ref=self.o_ref.at[rows_of, self.mine],
            send_sem=self.sems[0].at[j], recv_sem=self.sems[1].at[j],
            device_id=(chip[0], chip[1], self.c), device_id_type=MESH)

    def _over_d2d(self, j, rows):
        where = self.o_ref.at[2 * self.chips[j][0] + self.chips[j][1], rows]
        return pltpu.make_async_remote_copy(src_ref=where, dst_ref=where, send_sem=self.sems[2].at[j],
                                            recv_sem=self.sems[3].at[j], device_id=self.sibling,
                                            device_id_type=MESH)

    def _local(self):
        return pltpu.make_async_copy(self.w_ref, self.o_ref.at[self.me], self.sems[4])

    def start(self):
        self._local().start()
        for j in range(3):
            self._over_ici(j, self.me).start()

    def finish(self):
        for j in range(3):
            self._over_ici(j, 2 * self.chips[j][0] + self.chips[j][1]).wait_recv()
            self._over_d2d(j, self.mine).start()
        for j in range(3):
            self._over_d2d(j, self.other).wait_recv()
        for j in range(3):
            self._over_ici(j, self.me).wait_send()
            self._over_d2d(j, self.mine).wait_send()
        self._local().wait()


class _Scatter:
    SCRATCH = [pltpu.SemaphoreType.DMA((3,)), pltpu.SemaphoreType.DMA((3,)), pltpu.SemaphoreType.DMA]

    def __init__(self, g_ref, o_ref, send_sems, recv_sems, local_sem):
        self.g_ref, self.o_ref, self.sems = g_ref, o_ref, (send_sems, recv_sems, local_sem)
        x, y, c = lax.axis_index("x"), lax.axis_index("y"), lax.axis_index("c")
        self.c, self.me, self.chips = c, 2 * x + y, _other_chips(x, y)

    def _copy(self, j):
        chip = self.chips[j]
        return pltpu.make_async_remote_copy(
            src_ref=self.g_ref.at[2 * chip[0] + chip[1]], dst_ref=self.o_ref.at[self.me],
            send_sem=self.sems[0].at[j], recv_sem=self.sems[1].at[j],
            device_id=(chip[0], chip[1], self.c), device_id_type=MESH)

    def _local(self):
        return pltpu.make_async_copy(self.g_ref.at[self.me], self.o_ref.at[self.me], self.sems[2])

    def start(self):
        self._local().start()
        for j in range(3):
            self._copy(j).start()

    def finish(self):
        for j in range(3):
            self._copy(j).wait()
        self._local().wait()


def _allgather_chips(w):
    def body(w_ref, o_ref, *sems):
        ag = _AllGather(w_ref, o_ref, *sems)
        ag.start()
        ag.finish()

    return pl.pallas_call(
        body, name="allgather_weights",
        out_shape=jax.ShapeDtypeStruct((N_CHIPS,) + w.shape, w.dtype),
        in_specs=[ANY], out_specs=ANY, scratch_shapes=_AllGather.SCRATCH,
    )(w)


def _scatter_chips(g4):
    def body(g_ref, o_ref, *sems):
        sc = _Scatter(g_ref, o_ref, *sems)
        sc.start()
        sc.finish()

    return pl.pallas_call(
        body, name="scatter_grads",
        out_shape=jax.ShapeDtypeStruct(g4.shape, g4.dtype),
        in_specs=[ANY], out_specs=ANY, scratch_shapes=_Scatter.SCRATCH,
    )(g4)


def _swap_cores(p):
    def body(p_ref, o_ref, send_sem, recv_sem):
        x, y, c = lax.axis_index("x"), lax.axis_index("y"), lax.axis_index("c")
        cp = pltpu.make_async_remote_copy(src_ref=p_ref, dst_ref=o_ref, send_sem=send_sem, recv_sem=recv_sem,
                                          device_id=(x, y, 1 - c), device_id_type=MESH)
        cp.start()
        cp.wait()

    return pl.pallas_call(
        body, name="swap_cores",
        out_shape=jax.ShapeDtypeStruct(p.shape, p.dtype),
        in_specs=[ANY], out_specs=ANY,
        scratch_shapes=[pltpu.SemaphoreType.DMA, pltpu.SemaphoreType.DMA],
    )(p)


def _sum4(r4):
    _, R, C = r4.shape

    def body(r_ref, o_ref):
        o_ref[...] = ((r_ref[0].astype(F32) + r_ref[1].astype(F32)) + r_ref[2].astype(F32)) + r_ref[3].astype(F32)

    return pl.pallas_call(
        body, name="sum_chips", grid=(R // SUM_TILE,),
        in_specs=[pl.BlockSpec((N_CHIPS, SUM_TILE, C), lambda i: (0, i, 0))],
        out_specs=pl.BlockSpec((SUM_TILE, C), lambda i: (i, 0)),
        out_shape=jax.ShapeDtypeStruct((R, C), F32),
        compiler_params=_params(("parallel",)),
    )(r4)


def _adamw(pa, pb, w, m, v):
    R, C = w.shape
    c1 = 1.0 - ADAM_B1
    c2 = 1.0 - ADAM_B2
    bc1 = 1.0 - ADAM_B1 ** ADAM_STEP
    bc2 = 1.0 - ADAM_B2 ** ADAM_STEP

    def body(pa_ref, pb_ref, w_ref, m_ref, v_ref, g_ref, d_ref, mo_ref, vo_ref):
        g = pa_ref[...] + pb_ref[...]
        mn = ADAM_B1 * m_ref[...] + c1 * g
        vn = ADAM_B2 * v_ref[...] + c2 * (g * g)
        m_hat = mn / bc1
        v_hat = vn / bc2
        g_ref[...] = g
        d_ref[...] = -ADAM_LR * (m_hat / (jnp.sqrt(v_hat) + ADAM_EPS) + ADAM_WD * w_ref[...])
        mo_ref[...] = mn
        vo_ref[...] = vn

    spec = pl.BlockSpec((PACK_TILE, C), lambda i: (i, 0))
    shp = jax.ShapeDtypeStruct((R, C), F32)
    return pl.pallas_call(
        body, name="adamw", grid=(R // PACK_TILE,),
        in_specs=[spec] * 5, out_specs=[spec] * 4, out_shape=[shp] * 4,
        compiler_params=_params(("parallel",)),
    )(pa, pb, w, m, v)


def _rms_fwd(x, g, name):
    R, Dm = x.shape
    tr = _pick(R, (512, 256, 128))

    def body(x_ref, g_ref, o_ref):
        xf = x_ref[...]
        r = lax.rsqrt(jnp.mean(xf * xf, axis=-1, keepdims=True) + EPS)
        o_ref[...] = (xf * r * g_ref[...]).astype(o_ref.dtype)

    return pl.pallas_call(
        body, name=name, grid=(R // tr,),
        in_specs=[pl.BlockSpec((tr, Dm), lambda i: (i, 0)), pl.BlockSpec((1, Dm), lambda i: (0, 0))],
        out_specs=pl.BlockSpec((tr, Dm), lambda i: (i, 0)),
        out_shape=jax.ShapeDtypeStruct((R, Dm), BF16),
        compiler_params=_params(("parallel",)),
    )(x, g)


def _rms_gain_grad(x, dy, name):
    R, Dm = x.shape
    tr = _pick(R, (256, 128))

    def body(x_ref, dy_ref, dg_ref):
        xf = x_ref[...]
        r = lax.rsqrt(jnp.mean(xf * xf, axis=-1, keepdims=True) + EPS)

        @pl.when(pl.program_id(0) == 0)
        def _():
            dg_ref[...] = jnp.zeros_like(dg_ref)

        dg_ref[...] += jnp.sum(dy_ref[...] * (xf * r), axis=0, keepdims=True)

    row = pl.BlockSpec((tr, Dm), lambda i: (i, 0))
    return pl.pallas_call(
        body, name=name, grid=(R // tr,),
        in_specs=[row, row], out_specs=pl.BlockSpec((1, Dm), lambda i: (0, 0)),
        out_shape=jax.ShapeDtypeStruct((1, Dm), F32),
        compiler_params=_params(("arbitrary",)),
    )(x, dy)


def _final_loss(x, g, tgt):
    R, Dm = x.shape
    tr = _pick(R, (256, 128))

    def body(x_ref, g_ref, t_ref, l_ref, dx_ref, dxb_ref, dg_ref):
        xf = x_ref[...]
        gv = g_ref[...]
        r = lax.rsqrt(jnp.mean(xf * xf, axis=-1, keepdims=True) + EPS)
        xr = xf * r
        err = xr * gv - t_ref[...]
        dy_ = err * (1.0 / Dm)
        gdy = dy_ * gv
        mdot = jnp.mean(xf * gdy, axis=-1, keepdims=True)
        dx = r * gdy - xf * ((r * r * r) * mdot)
        dx_ref[...] = dx
        dxb_ref[...] = dx.astype(BF16)

        @pl.when(pl.program_id(0) == 0)
        def _():
            dg_ref[...] = jnp.zeros_like(dg_ref)
            l_ref[...] = jnp.zeros_like(l_ref)

        dg_ref[...] += jnp.sum(dy_ * xr, axis=0, keepdims=True)
        sq = jnp.sum(err * err, axis=1, keepdims=True)
        l_ref[...] += jnp.sum(sq, axis=0, keepdims=True) * (0.5 / Dm)

    row = pl.BlockSpec((tr, Dm), lambda i: (i, 0))
    vec = pl.BlockSpec((1, Dm), lambda i: (0, 0))
    return pl.pallas_call(
        body, name="final_norm_loss", grid=(R // tr,),
        in_specs=[row, vec, row],
        out_specs=[pl.BlockSpec((1, 1), lambda i: (0, 0)), row, row, vec],
        out_shape=[jax.ShapeDtypeStruct((1, 1), F32), jax.ShapeDtypeStruct((R, Dm), F32),
                   jax.ShapeDtypeStruct((R, Dm), BF16), jax.ShapeDtypeStruct((1, Dm), F32)],
        compiler_params=_params(("arbitrary",)),
    )(x, g, tgt)


MAX_TK = 2048

_DIMS = {"nn": (((1,), (0,)), ((), ())), "nt": (((1,), (1,)), ((), ())), "tn": (((0,), (0,)), ((), ()))}


def _mm(a, b, *, mode="nn", out_dtype=BF16, epi=None, extra=None, name):
    if mode == "nn":
        (M, K), N = a.shape, b.shape[1]
    elif mode == "nt":
        (M, K), N = a.shape, b.shape[0]
    else:
        (K, M), N = a.shape, b.shape[1]
    deep = K > MAX_TK
    tm = _pick(M, (512, 256, 128) if (epi == "rms_bwd" or deep) else (1024, 768, 512, 256, 128))
    wide = (2048,) if (K <= MAX_TK // 2 and mode != "tn") else ()
    tn = _pick(N, wide + (1024, 896, 768, 640, 512, 384, 256, 128))
    tk = K if K <= (2 * MAX_TK if deep else MAX_TK) else _pick(K, (2 * MAX_TK, MAX_TK, 1024, 512, 256, 128))
    nk = K // tk
    extras = () if extra is None else (extra if isinstance(extra, tuple) else (extra,))
    n_out = {"relu2": 2, "add_rms": 2, "rms_bwd": 3}.get(epi, 1)
    assert epi not in ("rms_bwd", "add_rms") or tn == N

    def body(*refs):
        a_ref, b_ref = refs[:2]
        e_refs = refs[2:2 + len(extras)]
        e_ref = e_refs[0] if e_refs else None
        outs = refs[2 + len(extras):2 + len(extras) + n_out]
        k = pl.program_id(2)
        part = lax.dot_general(a_ref[...].astype(BF16), b_ref[...].astype(BF16), _DIMS[mode],
                               preferred_element_type=F32)

        def finish(acc):
            if epi is None:
                outs[0][...] = acc.astype(outs[0].dtype)
            elif epi == "add":
                outs[0][...] = (e_ref[...] + acc).astype(outs[0].dtype)
            elif epi == "add_rms":
                y = e_refs[0][...] + acc
                outs[0][...] = y
                r = lax.rsqrt(jnp.mean(y * y, axis=-1, keepdims=True) + EPS)
                outs[1][...] = (y * r * e_refs[1][...]).astype(BF16)
            elif epi == "relu2":
                outs[0][...] = acc.astype(BF16)
                rl = jnp.maximum(acc, 0.0)
                outs[1][...] = (rl * rl).astype(BF16)
            elif epi == "drelu2":
                u = e_ref[...].astype(F32)
                outs[0][...] = (acc * (2.0 * jnp.maximum(u, 0.0))).astype(outs[0].dtype)
            elif epi == "rms_bwd":
                x_ref, g_ref, dres_ref = e_refs
                xf = x_ref[...]
                r = lax.rsqrt(jnp.mean(xf * xf, axis=-1, keepdims=True) + EPS)
                gdy = acc * g_ref[...]
                mdot = jnp.mean(xf * gdy, axis=-1, keepdims=True)
                dx = dres_ref[...] + (r * gdy - xf * ((r * r * r) * mdot))
                outs[0][...] = dx
                outs[1][...] = dx.astype(BF16)

                @pl.when(pl.program_id(0) == 0)
                def _():
                    outs[2][...] = jnp.zeros_like(outs[2])

                outs[2][...] += jnp.sum(acc * (xf * r), axis=0, keepdims=True)

        if nk == 1:
            finish(part)
        else:
            acc_ref = refs[-1]

            @pl.when(k == 0)
            def _():
                acc_ref[...] = part

            @pl.when(jnp.logical_and(k > 0, k < nk - 1))
            def _():
                acc_ref[...] += part

            @pl.when(k == nk - 1)
            def _():
                finish(acc_ref[...] + part)

    if mode == "tn":
        a_spec = pl.BlockSpec((tk, tm), lambda i, j, k: (k, i))
    else:
        a_spec = pl.BlockSpec((tm, tk), lambda i, j, k: (i, k))
    if mode == "nt":
        b_spec = pl.BlockSpec((tn, tk), lambda i, j, k: (j, k))
    else:
        b_spec = pl.BlockSpec((tk, tn), lambda i, j, k: (k, j))
    o_spec = pl.BlockSpec((tm, tn), lambda i, j, k: (i, j))
    vec_spec = pl.BlockSpec((1, tn), lambda i, j, k: (0, j))
    ins, in_specs = [a, b] + list(extras), [a_spec, b_spec]
    if epi == "rms_bwd":
        in_specs += [o_spec, vec_spec, o_spec]
        out_shape = [jax.ShapeDtypeStruct((M, N), F32), jax.ShapeDtypeStruct((M, N), BF16),
                     jax.ShapeDtypeStruct((1, N), F32)]
        out_specs = [o_spec, o_spec, vec_spec]
    elif epi == "add_rms":
        in_specs += [o_spec, vec_spec]
        out_shape = [jax.ShapeDtypeStruct((M, N), F32), jax.ShapeDtypeStruct((M, N), BF16)]
        out_specs = [o_spec, o_spec]
    else:
        in_specs += [o_spec] * len(extras)
        out_shape = [jax.ShapeDtypeStruct((M, N), BF16 if epi == "relu2" else out_dtype)] * n_out
        out_specs = [o_spec] * n_out
    res = pl.pallas_call(
        body, name=name, grid=(M // tm, N // tn, nk),
        in_specs=in_specs, out_specs=out_specs, out_shape=out_shape,
        scratch_shapes=[pltpu.VMEM((tm, tn), F32)] if nk > 1 else [],
        compiler_params=_params(("arbitrary",) * 3 if epi == "rms_bwd" else ("parallel", "parallel", "arbitrary")),
    )(*ins)
    return res if n_out > 1 else res[0]


def _dot(a, b):
    return lax.dot_general(a, b, _DIMS["nn"], preferred_element_type=F32)


def _dot_nt(a, b):
    return lax.dot_general(a, b, _DIMS["nt"], preferred_element_type=F32)


def _dot_tn(a, b):
    return lax.dot_general(a, b, _DIMS["tn"], preferred_element_type=F32)


def _split_dot(x, t):
    hi = x.astype(BF16)
    lo = (x - hi.astype(F32)).astype(BF16)
    return _dot(jnp.concatenate([hi, lo], axis=1), jnp.concatenate([t, t], axis=0))


def _head_pair(ref, scale=None):
    xf = ref[...].astype(F32)
    if scale is not None:
        xf = xf * scale
    is_a = lax.broadcasted_iota(jnp.int32, xf.shape, 1) < HEAD_DIM
    return jnp.where(is_a, xf, 0.0).astype(BF16), jnp.where(is_a, 0.0, xf).astype(BF16)


def _stack(a, b):
    return jnp.concatenate([a, b], axis=0)


def _head_rows(ref, scale=None):
    return _stack(*_head_pair(ref, scale))


def _unstack_heads(x):
    rows = x.shape[0] // 2
    return _select_pair(x[:rows], x[rows:])


def _pair_rowsum(x):
    is_a = lax.broadcasted_iota(jnp.int32, x.shape, 1) < HEAD_DIM
    return (jnp.sum(jnp.where(is_a, x, 0.0), axis=1, keepdims=True),
            jnp.sum(jnp.where(is_a, 0.0, x), axis=1, keepdims=True))


def _select_pair(xa, xb):
    is_a = lax.broadcasted_iota(jnp.int32, xa.shape, 1) < HEAD_DIM
    return jnp.where(is_a, xa, xb)


def _two_cols(xa, xb):
    rows = xa.shape[0]
    first = lax.broadcasted_iota(jnp.int32, (rows, 2), 1) == 0
    return jnp.where(first, xa, xb)


def _softplus_parts(z):
    e = jnp.exp(-jnp.abs(z))
    return jnp.maximum(z, 0.0) + jnp.log(1.0 + e), e


def _tile_iotas():
    row = lax.broadcasted_iota(jnp.int32, (BK, BK), 0)
    col = lax.broadcasted_iota(jnp.int32, (BK, BK), 1)
    return row, col


def _stacked_iotas(bq, nk):
    row = lax.broadcasted_iota(jnp.int32, (2 * bq, nk), 0) & (bq - 1)
    col = lax.broadcasted_iota(jnp.int32, (2 * bq, nk), 1)
    return row, col


def _side_exchange(exchange, operand, n_in, n_out):
    out_shape = jax.ShapeDtypeStruct(((N_CHIPS,) + operand.shape) if exchange is _AllGather else operand.shape,
                                     operand.dtype)
    n_sem = len(exchange.SCRATCH)

    def pick(refs):
        def make():
            return exchange(refs[n_in], refs[n_in + 1 + n_out], *refs[len(refs) - n_sem:])

        return (lambda: make().start()), (lambda: make().finish())

    return [operand], [ANY], [out_shape], [ANY], pick


def _sb_fwd(proj, name, exchange, operand):
    S = proj.shape[0]
    nqb = S // (CHAINS * BQ)
    x_in, x_in_specs, x_out, x_out_specs, pick = _side_exchange(exchange, operand, 3, 1)

    def body(*refs):
        q_ref, k_ref, v_ref = refs[:3]
        o_ref = refs[3 + len(x_in)]
        acc_ref = refs[3 + len(x_in) + 1 + len(x_out)]
        start, finish = pick(refs)
        p = pl.program_id(0)
        i = pl.program_id(1)

        @pl.when(jnp.logical_and(p == 0, i == 0))
        def _():
            start()

        q2 = [_head_rows(q_ref.at[pl.ds(ch * BQ, BQ)], SCALE) for ch in range(CHAINS)]
        row, col = _tile_iotas()
        tri = (row > col).astype(BF16)
        srow, scol = _stacked_iotas(BQ, BK)
        acc_ref[...] = jnp.zeros_like(acc_ref)

        def tile(ch, kb, c, dmask=None, valid=None):
            r0 = pl.multiple_of(kb * BK, BK)
            kblk = k_ref[pl.ds(r0, BK), :]
            vblk = v_ref[pl.ds(r0, BK), :]
            z = _dot_nt(q2[ch], kblk)
            sp, _ = _softplus_parts(z)
            lm = -sp
            if dmask is not None:
                lm = jnp.where(dmask, lm, 0.0)
            btw = _split_dot(lm, tri)
            w = jnp.exp((z - sp) + btw + c)
            if dmask is not None:
                w = jnp.where(dmask, w, 0.0)
            if valid is not None:
                w = w * valid
            acc_ref[ch] += _dot(w.astype(BF16), vblk)
            return c + btw[:, 0:1] + lm[:, 0:1]

        def alive(c):
            return jnp.max(c) > UNDERFLOW_BOUND

        cs = [jnp.zeros((2 * BQ, 1), F32)] * CHAINS
        for d in reversed(range(DIAG_TILES)):
            cs = [tile(ch, (CHAINS * i + ch) * DIAG_TILES + d, cs[ch], dmask=scol < srow - d * BK)
                  for ch in range(CHAINS)]

        def tile_of(ch, t):
            return (CHAINS * i + ch) * DIAG_TILES - 1 - t

        def more(cs, t):
            go = [jnp.logical_and(alive(cs[ch]), tile_of(ch, t) >= 0) for ch in range(CHAINS)]
            return functools.reduce(jnp.logical_or, go).astype(jnp.int32)

        def step(st):
            t, _, cs = st
            new = []
            for ch in range(CHAINS):
                kb = tile_of(ch, t)
                if ch == CHAINS - 1:
                    new.append(tile(ch, kb, cs[ch]))
                else:
                    new.append(tile(ch, jnp.maximum(kb, 0), cs[ch], valid=(kb >= 0).astype(F32)))
            return t + 1, more(new, t + 1), new

        lax.while_loop(lambda st: st[1] > 0, step, (0, more(cs, 0), cs))
        for ch in range(CHAINS):
            o_ref[pl.ds(ch * BQ, BQ), :] = _unstack_heads(acc_ref[ch])

        @pl.when(jnp.logical_and(p == 3, i == nqb - 1))
        def _():
            finish()

    blk = pl.BlockSpec((CHAINS * BQ, LANES), lambda p, i: (i, p))
    res = pl.pallas_call(
        body, name=name, grid=(4, nqb),
        in_specs=[blk, pl.BlockSpec((S, LANES), lambda p, i: (0, GROUP_COLS + p)),
                  pl.BlockSpec((S, LANES), lambda p, i: (0, 2 * GROUP_COLS + p))] + x_in_specs,
        out_specs=[blk] + x_out_specs,
        out_shape=[jax.ShapeDtypeStruct((S, MERGED_WIDTH), F32)] + x_out,
        scratch_shapes=[pltpu.VMEM((CHAINS, 2 * BQ, LANES), F32)] + exchange.SCRATCH,
        compiler_params=_params(("arbitrary", "arbitrary")),
    )(proj, proj, proj, *x_in)
    return res


def _sb_bwd(proj, merged, dmerged, name, exchange, operand):
    S = proj.shape[0]
    nqb = S // (CHAINS * BQ)
    x_in, x_in_specs, x_out, x_out_specs, pick = _side_exchange(exchange, operand, 5, 3)

    def body(*refs):
        q_ref, k_ref, v_ref, o_ref, do_ref = refs[:5]
        dq_ref, dk_hbm, dv_hbm = refs[5 + len(x_in):8 + len(x_in)]
        dq_acc, dk_acc, dv_acc, sem = refs[8 + len(x_in) + len(x_out):12 + len(x_in) + len(x_out)]
        start, finish = pick(refs)
        p = pl.program_id(0)
        i = pl.program_id(1)

        @pl.when(jnp.logical_and(p == 0, i == 0))
        def _():
            start()

        @pl.when(i == 0)
        def _():
            dk_acc[...] = jnp.zeros_like(dk_acc)
            dv_acc[...] = jnp.zeros_like(dv_acc)

        rows = [pl.ds(ch * BQ, BQ) for ch in range(CHAINS)]
        q2 = [_head_rows(q_ref.at[rw], SCALE) for rw in rows]
        do2 = [_head_rows(do_ref.at[rw]) for rw in rows]
        tot = [_stack(*_pair_rowsum(do_ref[rw, :].astype(F32) * o_ref[rw, :])) for rw in rows]
        row, col = _tile_iotas()
        tri_gt = (row > col).astype(BF16)
        tri_ge = (row >= col).astype(BF16)
        srow, scol = _stacked_iotas(BQ, BK)
        dq_acc[...] = jnp.zeros_like(dq_acc)

        def tile(ch, kb, st, dmask=None, valid=None):
            masked = dmask is not None
            c, r = st
            r0 = pl.multiple_of(kb * BK, BK)
            kblk = k_ref[pl.ds(r0, BK), :]
            vblk = v_ref[pl.ds(r0, BK), :]
            z = _dot_nt(q2[ch], kblk)
            sp, e = _softplus_parts(z)
            lm = -sp
            if masked:
                lm = jnp.where(dmask, lm, 0.0)
            btw = _split_dot(lm, tri_gt)
            w = jnp.exp((z - sp) + btw + c)
            if masked:
                w = jnp.where(dmask, w, 0.0)
            if valid is not None:
                w = w * valid
            wb = w.astype(BF16)
            a = wb.astype(F32) * _dot_nt(do2[ch], vblk)
            suffix = _split_dot(a, tri_ge) + r
            rcp = 1.0 / (1.0 + e)
            pos = z >= 0.0
            sig = jnp.where(pos, rcp, e * rcp)
            sig_neg = jnp.where(pos, e * rcp, rcp)
            dz = a * sig_neg - (tot[ch] - suffix) * sig
            if masked:
                dz = jnp.where(dmask, dz, 0.0)
            if valid is not None:
                dz = dz * valid
            dzb = dz.astype(BF16)
            dq_acc[ch] += _dot(dzb, kblk)
            dk_acc[pl.ds(r0, BK), :] += _dot_tn(dzb, q2[ch])
            dv_acc[pl.ds(r0, BK), :] += _dot_tn(wb, do2[ch])
            return c + btw[:, 0:1] + lm[:, 0:1], suffix[:, 0:1]

        def alive(st):
            return jnp.max(st[0]) > UNDERFLOW_BOUND

        zero = jnp.zeros((2 * BQ, 1), F32)
        sts = [(zero, zero)] * CHAINS
        for d in reversed(range(DIAG_TILES)):
            sts = [tile(ch, (CHAINS * i + ch) * DIAG_TILES + d, sts[ch], dmask=scol < srow - d * BK)
                   for ch in range(CHAINS)]

        def tile_of(ch, t):
            return (CHAINS * i + ch) * DIAG_TILES - 1 - t

        def more(sts, t):
            go = [jnp.logical_and(alive(sts[ch]), tile_of(ch, t) >= 0) for ch in range(CHAINS)]
            return functools.reduce(jnp.logical_or, go).astype(jnp.int32)

        def step(s):
            t, _, sts = s
            new = []
            for ch in range(CHAINS):
                kb = tile_of(ch, t)
                if ch == CHAINS - 1:
                    new.append(tile(ch, kb, sts[ch]))
                else:
                    new.append(tile(ch, jnp.maximum(kb, 0), sts[ch], valid=(kb >= 0).astype(F32)))
            return t + 1, more(new, t + 1), new

        lax.while_loop(lambda s: s[1] > 0, step, (0, more(sts, 0), sts))
        for ch in range(CHAINS):
            dq_ref[rows[ch], :] = (_unstack_heads(dq_acc[ch]) * SCALE).astype(dq_ref.dtype)

        @pl.when(i == nqb - 1)
        def _():
            ck = pltpu.make_async_copy(dk_acc, dk_hbm.at[p], sem.at[0])
            cv = pltpu.make_async_copy(dv_acc, dv_hbm.at[p], sem.at[1])
            ck.start()
            cv.start()
            ck.wait()
            cv.wait()

        @pl.when(jnp.logical_and(p == 3, i == nqb - 1))
        def _():
            finish()

    blk = lambda off: pl.BlockSpec((CHAINS * BQ, LANES), lambda p, i: (i, off + p))
    slab = lambda off: pl.BlockSpec((S, LANES), lambda p, i: (0, off + p))
    return pl.pallas_call(
        body, name=name, grid=(4, nqb),
        in_specs=[blk(0), slab(GROUP_COLS), slab(2 * GROUP_COLS), blk(0), blk(0)] + x_in_specs,
        out_specs=[blk(0), ANY, ANY] + x_out_specs,
        out_shape=[jax.ShapeDtypeStruct((S, MIX_WIDTH), BF16),
                   jax.ShapeDtypeStruct((4, S, LANES), F32), jax.ShapeDtypeStruct((4, S, LANES), F32)] + x_out,
        scratch_shapes=[pltpu.VMEM((CHAINS, 2 * BQ, LANES), F32), pltpu.VMEM((S, LANES), F32),
                        pltpu.VMEM((S, LANES), F32), pltpu.SemaphoreType.DMA((2,))]
        + exchange.SCRATCH,
        compiler_params=_params(("arbitrary", "arbitrary")),
    )(proj, proj, proj, merged, dmerged, *x_in)


def _key_norm_max(k_ref, knorm_ref, nkb):
    def step(kb, m):
        r0 = pl.multiple_of(kb * BK, BK)
        blk = k_ref[pl.ds(r0, BK), :].astype(F32)
        sa, sb = _pair_rowsum(blk * blk)
        return (jnp.maximum(m[0], jnp.max(sa, axis=0, keepdims=True)),
                jnp.maximum(m[1], jnp.max(sb, axis=0, keepdims=True)))

    zero = jnp.zeros((1, 1), F32)
    ma, mb = lax.fori_loop(0, nkb, step, (zero, zero))
    knorm_ref[...] = _select_pair(jnp.broadcast_to(ma, (1, LANES)), jnp.broadcast_to(mb, (1, LANES)))


FQ = 512
FK = FQ
GATE_BLOCKS = FK // BK


def _key_gates(cr_ref, kb):
    blocks = [cr_ref[0, GATE_BLOCKS * kb + j] for j in range(GATE_BLOCKS)]
    per_head = [jnp.broadcast_to(jnp.concatenate([b[h:h + 1] for b in blocks], axis=1), (FQ, FK)) for h in range(2)]
    return _stack(*per_head)


def _last_gate(cr_ref, kb):
    last = cr_ref[0, GATE_BLOCKS * jnp.maximum(kb, 0) + GATE_BLOCKS - 1]
    return _stack(*[jnp.broadcast_to(last[h:h + 1, BK - 1:BK], (FQ, 1)) for h in range(2)])


def _logit_bound(q_ref, knorm_ref):
    qf = q_ref[...].astype(F32) * SCALE
    qa, qb = _pair_rowsum(qf * qf)
    kn = knorm_ref[...]
    return _stack(jnp.sqrt(qa * kn[:, 0:1]), jnp.sqrt(qb * kn[:, HEAD_DIM:HEAD_DIM + 1]))


def _causal_bias(bias_ref):
    srow, scol = _stacked_iotas(FQ, FK)
    bias_ref[...] = jnp.where(scol <= srow, 0.0, NEG_INF)


def _fox_fwd(proj, kv, c_col, c_row, name):
    S = proj.shape[0]
    nqb = S // FQ

    def body(q_ref, k_ref, v_ref, cc_ref, cr_ref, o_ref, lse_ref, acc_ref, knorm_ref, bias_ref):
        i = pl.program_id(1)

        @pl.when(i == 0)
        def _():
            _key_norm_max(k_ref, knorm_ref, S // BK)
            _causal_bias(bias_ref)

        q2 = _head_rows(q_ref, SCALE)
        bound = _logit_bound(q_ref, knorm_ref)
        cc = cc_ref[0]
        ct = _stack(cc[:, 0:1], cc[:, 1:2])
        acc_ref[...] = jnp.zeros_like(acc_ref)

        def tile(kb, st, on_diagonal):
            m, l = st
            r0 = pl.multiple_of(kb * FK, FK)
            kblk = k_ref[pl.ds(r0, FK), :]
            vblk = v_ref[pl.ds(r0, FK), :]
            z = _dot_nt(q2, kblk) + ct - _key_gates(cr_ref, kb) + bias_ref[...] * on_diagonal
            m_new = jnp.maximum(m, jnp.max(z, axis=1, keepdims=True))
            alpha = jnp.exp(m - m_new)
            pr = jnp.exp(z - m_new)
            acc_ref[...] = alpha * acc_ref[...] + _split_dot(pr, vblk)
            return m_new, alpha * l + jnp.sum(pr, axis=1, keepdims=True)

        def alive(kb, st):
            reach = bound + ct - _last_gate(cr_ref, kb) - st[0]
            return (jnp.max(reach) > UNDERFLOW_BOUND).astype(jnp.int32)

        neg = jnp.full((2 * FQ, 1), NEG_INF, F32)
        zero = jnp.zeros((2 * FQ, 1), F32)
        def cond(s):
            return jnp.logical_and(s[0] >= 0, s[1] > 0)

        def step(s):
            kb, _, st = s
            st = tile(kb, st, (kb == i).astype(F32))
            return kb - 1, alive(kb - 1, st), st

        _, _, (m, l) = lax.while_loop(cond, step, (i, jnp.int32(1), (neg, zero)))
        o_ref[...] = _unstack_heads(acc_ref[...] / l)
        lse = m + jnp.log(l)
        lse_ref[0] = _two_cols(lse[:FQ], lse[FQ:])

    return pl.pallas_call(
        body, name=name, grid=(4, nqb),
        in_specs=[pl.BlockSpec((FQ, LANES), lambda p, i: (i, p)),
                  pl.BlockSpec((S, LANES), lambda p, i: (0, p)),
                  pl.BlockSpec((S, LANES), lambda p, i: (0, GROUP_COLS + p)),
                  pl.BlockSpec((1, FQ, 2), lambda p, i: (p, i, 0)),
                  pl.BlockSpec((1, S // BK, 8, LANES), lambda p, i: (p, 0, 0, 0))],
        out_specs=[pl.BlockSpec((FQ, LANES), lambda p, i: (i, p)),
                   pl.BlockSpec((1, FQ, 2), lambda p, i: (p, i, 0))],
        out_shape=[jax.ShapeDtypeStruct((S, MERGED_WIDTH), F32), jax.ShapeDtypeStruct((4, S, 2), F32)],
        scratch_shapes=[pltpu.VMEM((2 * FQ, LANES), F32), pltpu.VMEM((1, LANES), F32),
                        pltpu.VMEM((2 * FQ, FK), F32)],
        compiler_params=_params(("arbitrary", "arbitrary")),
    )(proj, kv, kv, c_col, c_row)


def _fox_bwd(proj, kv, c_col, c_row, lse, merged, dmerged, dk_prev, dv_prev, dc_prev, name):
    S = proj.shape[0]
    nqb = S // FQ

    def body(q_ref, k_ref, v_ref, cc_ref, cr_ref, lse_ref, o_ref, do_ref, dkp_hbm, dvp_hbm, dcp_ref,
             dq_ref, dk_hbm, dv_hbm, dc_ref, dq_acc, dk_acc, dv_acc, knorm_ref, bias_ref, sem):
        p = pl.program_id(0)
        i = pl.program_id(1)

        @pl.when(i == 0)
        def _():
            ck = pltpu.make_async_copy(dkp_hbm.at[p], dk_acc, sem.at[0])
            cv = pltpu.make_async_copy(dvp_hbm.at[p], dv_acc, sem.at[1])
            ck.start()
            cv.start()
            dc_ref[...] = dcp_ref[...]
            _key_norm_max(k_ref, knorm_ref, S // BK)
            _causal_bias(bias_ref)
            ck.wait()
            cv.wait()

        q2 = _head_rows(q_ref, SCALE)
        do2 = _head_rows(do_ref)
        tot = _stack(*_pair_rowsum(do_ref[...].astype(F32) * o_ref[...]))
        bound = _logit_bound(q_ref, knorm_ref)
        cc = cc_ref[0]
        ct = _stack(cc[:, 0:1], cc[:, 1:2])
        ls = lse_ref[0]
        lse = _stack(ls[:, 0:1], ls[:, 1:2])
        sub = lax.broadcasted_iota(jnp.int32, (8, LANES), 0)
        dq_acc[...] = jnp.zeros_like(dq_acc)

        def tile(kb, masked):
            r0 = pl.multiple_of(kb * FK, FK)
            kblk = k_ref[pl.ds(r0, FK), :]
            vblk = v_ref[pl.ds(r0, FK), :]
            z = _dot_nt(q2, kblk) + ct - _key_gates(cr_ref, kb)
            if masked:
                z = z + bias_ref[...]
            pr = jnp.exp(z - lse)
            ds = pr * (_dot_nt(do2, vblk) - tot)
            dsb = ds.astype(BF16)
            dq_acc[...] += _dot(dsb, kblk)
            dk_acc[pl.ds(r0, FK), :] += _dot_tn(dsb, q2)
            dv_acc[pl.ds(r0, FK), :] += _dot_tn(pr.astype(BF16), do2)
            dca = jnp.sum(ds[:FQ], axis=0, keepdims=True)
            dcb = jnp.sum(ds[FQ:], axis=0, keepdims=True)
            for j in range(GATE_BLOCKS):
                cols = slice(j * BK, (j + 1) * BK)
                old = dc_ref[0, GATE_BLOCKS * kb + j]
                dc_ref[0, GATE_BLOCKS * kb + j] = jnp.where(sub == 0, old - dca[:, cols],
                                                            jnp.where(sub == 1, old - dcb[:, cols], old))

        def alive(kb):
            reach = bound + ct - _last_gate(cr_ref, kb) - lse
            return (jnp.max(reach) > UNDERFLOW_BOUND).astype(jnp.int32)

        tile(i, True)

        def cond(s):
            return jnp.logical_and(s[0] >= 0, s[1] > 0)

        def step(s):
            kb, _ = s
            tile(kb, False)
            return kb - 1, alive(kb - 1)

        lax.while_loop(cond, step, (i - 1, alive(i - 1)))
        dq_ref[...] = (_unstack_heads(dq_acc[...]) * SCALE).astype(dq_ref.dtype)

        @pl.when(i == nqb - 1)
        def _():
            ck = pltpu.make_async_copy(dk_acc, dk_hbm.at[p], sem.at[0])
            cv = pltpu.make_async_copy(dv_acc, dv_hbm.at[p], sem.at[1])
            ck.start()
            cv.start()
            ck.wait()
            cv.wait()

    blk = lambda off: pl.BlockSpec((FQ, LANES), lambda p, i: (i, off + p))
    slab = lambda off: pl.BlockSpec((S, LANES), lambda p, i: (0, off + p))
    cols = pl.BlockSpec((1, FQ, 2), lambda p, i: (p, i, 0))
    rows = pl.BlockSpec((1, S // BK, 8, LANES), lambda p, i: (p, 0, 0, 0))
    return pl.pallas_call(
        body, name=name, grid=(4, nqb),
        in_specs=[blk(0), slab(0), slab(GROUP_COLS), cols, rows, cols, blk(0), blk(0), ANY, ANY, rows],
        out_specs=[blk(0), ANY, ANY, rows],
        out_shape=[jax.ShapeDtypeStruct((S, MIX_WIDTH), BF16),
                   jax.ShapeDtypeStruct((4, S, LANES), F32), jax.ShapeDtypeStruct((4, S, LANES), F32),
                   jax.ShapeDtypeStruct((4, S // BK, 8, LANES), F32)],
        scratch_shapes=[pltpu.VMEM((2 * FQ, LANES), F32), pltpu.VMEM((S, LANES), F32),
                        pltpu.VMEM((S, LANES), F32), pltpu.VMEM((1, LANES), F32),
                        pltpu.VMEM((2 * FQ, FK), F32), pltpu.SemaphoreType.DMA((2,))],
        compiler_params=_params(("arbitrary", "arbitrary")),
    )(proj, kv, kv, c_col, c_row, lse, merged, dmerged, dk_prev, dv_prev, dc_prev)


def _lane_scan(x, reverse):
    lane = lax.broadcasted_iota(jnp.int32, x.shape, 1)
    d = 1
    while d < LANES:
        if reverse:
            x = x + jnp.where(lane < LANES - d, pltpu.roll(x, LANES - d, 1), 0.0)
        else:
            x = x + jnp.where(lane >= d, pltpu.roll(x, d, 1), 0.0)
        d *= 2
    return x


def _gate_fwd(fl3, b8):
    nb = fl3.shape[0]

    def body(fl_ref, b_ref, c_ref):
        def step(kb, carry):
            x = fl_ref[kb] + b_ref[...]
            sp, _ = _softplus_parts(-x)
            c = _lane_scan(-sp, False) + carry
            c_ref[kb] = c
            return c[:, LANES - 1:LANES]

        lax.fori_loop(0, nb, step, jnp.zeros((8, 1), F32))

    return pl.pallas_call(body, name="forget_gate_cumsum",
                          out_shape=jax.ShapeDtypeStruct(fl3.shape, F32),
                          compiler_params=_params())(fl3, b8)


def _gate_bwd(dc3, fl3, b8):
    nb = fl3.shape[0]

    def body(dc_ref, fl_ref, b_ref, dfl_ref, db_ref):
        def step(t, st):
            carry, dbs = st
            kb = nb - 1 - t
            g = _lane_scan(dc_ref[kb], True) + carry
            x = fl_ref[kb] + b_ref[...]
            e = jnp.exp(-jnp.abs(x))
            rcp = 1.0 / (1.0 + e)
            dfl = g * jnp.where(x >= 0.0, e * rcp, rcp)
            dfl_ref[kb] = dfl
            return g[:, 0:1], dbs + dfl

        _, dbs = lax.fori_loop(0, nb, step, (jnp.zeros((8, 1), F32), jnp.zeros((8, LANES), F32)))
        db_ref[...] = jnp.broadcast_to(jnp.sum(dbs, axis=1, keepdims=True), (8, LANES))

    return pl.pallas_call(body, name="forget_gate_bwd",
                          out_shape=[jax.ShapeDtypeStruct(fl3.shape, F32), jax.ShapeDtypeStruct((8, LANES), F32)],
                          compiler_params=_params())(dc3, fl3, b8)


MEM_TQ = 512
MEM_COLS = MEM_WIDTH // LANES


def _mem_fwd(proj, qcol, mkv, mix, name):
    S = proj.shape[0]
    M = mkv.shape[0]

    def body(q_ref, mk_ref, mv_ref, mix_ref, o_ref, lse_ref):
        q2 = _head_rows(q_ref, SCALE)
        s = _dot_nt(q2, mk_ref[...])
        m = jnp.max(s, axis=1, keepdims=True)
        pr = jnp.exp(s - m)
        l = jnp.sum(pr, axis=1, keepdims=True)
        o_ref[...] = _unstack_heads(_dot(pr.astype(BF16), mv_ref[...]) / l)
        lse = m + jnp.log(l)
        lse_ref[0] = _two_cols(lse[:MEM_TQ], lse[MEM_TQ:])

    return pl.pallas_call(
        body, name=name, grid=(MEM_COLS, S // MEM_TQ),
        in_specs=[pl.BlockSpec((MEM_TQ, LANES), lambda p, i: (i, qcol + p)),
                  pl.BlockSpec((M, LANES), lambda p, i: (0, p)),
                  pl.BlockSpec((M, LANES), lambda p, i: (0, MEM_COLS + p)), ANY],
        out_specs=[pl.BlockSpec((MEM_TQ, LANES), lambda p, i: (i, GROUP_COLS + p)),
                   pl.BlockSpec((1, MEM_TQ, 2), lambda p, i: (p, i, 0))],
        out_shape=[jax.ShapeDtypeStruct((S, MERGED_WIDTH), F32), jax.ShapeDtypeStruct((2, S, 2), F32)],
        input_output_aliases={3: 0},
        compiler_params=_params(("parallel", "parallel")),
    )(proj, mkv, mkv, mix)


def _mem_bwd(proj, qcol, mkv, lse, merged, dmerged, name):
    S = proj.shape[0]
    M = mkv.shape[0]

    def body(q_ref, mk_ref, mv_ref, lse_ref, o_ref, do_ref, dq_ref, dmk_ref, dmv_ref):
        @pl.when(pl.program_id(1) == 0)
        def _():
            dmk_ref[...] = jnp.zeros_like(dmk_ref)
            dmv_ref[...] = jnp.zeros_like(dmv_ref)

        q2 = _head_rows(q_ref, SCALE)
        do2 = _head_rows(do_ref)
        tot = _stack(*_pair_rowsum(do_ref[...].astype(F32) * o_ref[...]))
        ls = lse_ref[0]
        pr = jnp.exp(_dot_nt(q2, mk_ref[...]) - _stack(ls[:, 0:1], ls[:, 1:2]))
        ds = pr * (_dot_nt(do2, mv_ref[...]) - tot)
        dsb = ds.astype(BF16)
        dmk_ref[...] += _dot_tn(dsb, q2)
        dmv_ref[...] += _dot_tn(pr.astype(BF16), do2)
        dq_ref[...] = (_unstack_heads(_dot(dsb, mk_ref[...])) * SCALE).astype(dq_ref.dtype)

    blk = lambda off: pl.BlockSpec((MEM_TQ, LANES), lambda p, i: (i, off + p))
    acc = pl.BlockSpec((M, LANES), lambda p, i: (0, p))
    return pl.pallas_call(
        body, name=name, grid=(MEM_COLS, S // MEM_TQ),
        in_specs=[blk(qcol), pl.BlockSpec((M, LANES), lambda p, i: (0, p)),
                  pl.BlockSpec((M, LANES), lambda p, i: (0, MEM_COLS + p)),
                  pl.BlockSpec((1, MEM_TQ, 2), lambda p, i: (p, i, 0)), blk(GROUP_COLS), blk(GROUP_COLS)],
        out_specs=[blk(0), acc, acc],
        out_shape=[jax.ShapeDtypeStruct((S, MEM_WIDTH), BF16), jax.ShapeDtypeStruct((M, MEM_WIDTH), F32),
                   jax.ShapeDtypeStruct((M, MEM_WIDTH), F32)],
        compiler_params=_params(("parallel", "arbitrary")),
    )(proj, mkv, mkv, lse, merged, dmerged)


def _c_layouts(c3):
    nb = c3.shape[0]
    pairs = c3.reshape(nb, 4, 2, LANES).transpose(1, 0, 2, 3)
    c_row = jnp.pad(pairs, ((0, 0), (0, 0), (0, 6), (0, 0)))
    c_col = pairs.transpose(0, 1, 3, 2).reshape(4, nb * LANES, 2)
    return c_col, c_row


def _local_step(x, mem, wb, shards, sm, loss_target):
    S = x.shape[0]
    nb = S // BK
    vec = lambda a: a.reshape(1, D_MODEL)
    b8 = jnp.broadcast_to(sm["b_f"].reshape(8, 1), (8, LANES))

    saved = []
    shared = None
    h = x
    hn = _rms_fwd(h, vec(sm["norm1_g"][0]), "norm1_0")
    for l in range(DEPTH):
        if l == N_A:
            w_kvf = jnp.pad(wb["w_kv_shared"], ((0, 0), (0, KVF_WIDTH - W_KV_SHARED)))
            hs = _rms_fwd(h, vec(sm["kv_norm_g"]), "kv_norm")
            kvf = _mm(hs, w_kvf, out_dtype=F32, name="kv_shared_proj")
            kv = kvf[:, :2 * MIX_WIDTH].astype(BF16)
            fl3 = kvf[:, 2 * MIX_WIDTH:2 * MIX_WIDTH + 8].T.reshape(8, nb, LANES).transpose(1, 0, 2)
            c3 = _gate_fwd(fl3, b8)
            c_col, c_row = _c_layouts(c3)
            shared = dict(h=h, hs=hs, kv=kv, fl3=fl3, c_col=c_col, c_row=c_row)
        mn = _rms_fwd(mem, vec(sm["mem_norm_g"][l]), f"mem_norm_{l}")
        mkv = _mm(mn, wb["w_mem_kv"][l], name=f"mem_kv_proj_{l}")
        if l < N_A:
            w_in = wb["w_in_a"][l]
            proj = _mm(hn, w_in, name=f"in_proj_{l}")
            mix, gathered = _sb_fwd(proj, f"stickbreak_fwd_{l}", _AllGather, shards[l])
            _unpack_gathered(PARTS[1 + l], gathered, wb)
            lse, qcol = None, Q_MEM_COL_A
        else:
            w_in = wb["w_in_b"][l - N_A]
            proj = _mm(hn, w_in, name=f"in_proj_{l}")
            mix, lse = _fox_fwd(proj, shared["kv"], shared["c_col"], shared["c_row"], f"fox_fwd_{l}")
            qcol = Q_MEM_COL_B
        merged, mlse = _mem_fwd(proj, qcol, mkv, mix, f"mem_attn_fwd_{l}")
        h_mid, hn2 = _mm(merged, wb["w_o"][l], epi="add_rms", extra=(h, vec(sm["norm2_g"][l])),
                         name=f"out_proj_{l}")
        u, act = _mm(hn2, wb["w_mlp1"][l], epi="relu2", name=f"mlp1_{l}")
        saved.append(dict(h=h, hn=hn, mn=mn, mkv=mkv, proj=proj, lse=lse, mlse=mlse, qcol=qcol, merged=merged,
                          h_mid=h_mid, hn2=hn2, u=u, act=act, w_in=w_in))
        if l + 1 < DEPTH:
            h, hn = _mm(act, wb["w_mlp2"][l], epi="add_rms", extra=(h_mid, vec(sm["norm1_g"][l + 1])),
                        name=f"mlp2_{l}")
        else:
            h = _mm(act, wb["w_mlp2"][l], out_dtype=F32, epi="add", extra=h_mid, name=f"mlp2_{l}")

    loss, dh, dhb, dg_final = _final_loss(h, vec(sm["final_norm_g"]), loss_target)

    gb = {n: [None] * (DEPTH if n not in ("w_in_a", "w_in_b") else 2) for n in
          ("w_in_a", "w_in_b", "w_mem_kv", "w_o", "w_mlp1", "w_mlp2")}
    gs = {n: [None] * DEPTH for n in ("norm1_g", "mem_norm_g", "norm2_g")}
    received = [None] * N_A
    dk_sh = jnp.zeros((4, S, LANES), F32)
    dv_sh = jnp.zeros((4, S, LANES), F32)
    dc_sh = jnp.zeros((4, nb, 8, LANES), F32)
    for l in reversed(range(DEPTH)):
        sv = saved[l]
        du = _mm(dhb, wb["w_mlp2"][l], mode="nt", epi="drelu2", extra=sv["u"], name=f"mlp2_dx_{l}")
        gb["w_mlp2"][l] = _mm(sv["act"], dhb, mode="tn", out_dtype=F32, name=f"mlp2_dw_{l}")
        gb["w_mlp1"][l] = _mm(sv["hn2"], du, mode="tn", out_dtype=F32, name=f"mlp1_dw_{l}")
        dh, dhb, gs["norm2_g"][l] = _mm(du, wb["w_mlp1"][l], mode="nt", epi="rms_bwd",
                                        extra=(sv["h_mid"], vec(sm["norm2_g"][l]), dh),
                                        name=f"mlp1_dx_norm2_bwd_{l}")
        dmerged = _mm(dhb, wb["w_o"][l], mode="nt", name=f"out_proj_dx_{l}")
        gb["w_o"][l] = _mm(sv["merged"], dhb, mode="tn", out_dtype=F32, name=f"out_proj_dw_{l}")
        if l < N_A:
            ready = _pack_grads(PARTS[1 + l], PART_ROWS[1 + l], gb, None)
            dq, dk, dv, received[l] = _sb_bwd(sv["proj"], sv["merged"], dmerged, f"stickbreak_bwd_{l}",
                                              _Scatter, ready)
        else:
            dq, dk_sh, dv_sh, dc_sh = _fox_bwd(sv["proj"], shared["kv"], shared["c_col"], shared["c_row"],
                                               sv["lse"], sv["merged"], dmerged, dk_sh, dv_sh, dc_sh,
                                               f"fox_bwd_{l}")
        dqm, dmk, dmv = _mem_bwd(sv["proj"], sv["qcol"], sv["mkv"], sv["mlse"], sv["merged"], dmerged,
                                 f"mem_attn_bwd_{l}")
        if l < N_A:
            flat = lambda t: t.transpose(1, 0, 2).reshape(S, MIX_WIDTH).astype(BF16)
            dproj = jnp.concatenate([dq, flat(dk), flat(dv), dqm], axis=1)
        else:
            dproj = jnp.concatenate([dq, dqm], axis=1)
        name_in = "w_in_a" if l < N_A else "w_in_b"
        gb[name_in][l if l < N_A else l - N_A] = _mm(sv["hn"], dproj, mode="tn", out_dtype=F32,
                                                      name=f"in_proj_dw_{l}")
        dh, dhb, gs["norm1_g"][l] = _mm(dproj, sv["w_in"], mode="nt", epi="rms_bwd",
                                        extra=(sv["h"], vec(sm["norm1_g"][l]), dh),
                                        name=f"in_proj_dx_norm1_bwd_{l}")
        dmkv = jnp.concatenate([dmk, dmv], axis=1)
        gb["w_mem_kv"][l] = _mm(sv["mn"], dmkv, mode="tn", out_dtype=F32, name=f"mem_kv_dw_{l}")
        dmn = _mm(dmkv, wb["w_mem_kv"][l], mode="nt", out_dtype=F32, name=f"mem_kv_dx_{l}")
        gs["mem_norm_g"][l] = _rms_gain_grad(mem, dmn, f"mem_norm_bwd_{l}")
        if l == N_A:
            dfl3, db8 = _gate_bwd(dc_sh.reshape(4, nb, 8, LANES)[:, :, :2].transpose(1, 0, 2, 3).reshape(nb, 8, LANES),
                                  shared["fl3"], b8)
            dfl = dfl3.transpose(1, 0, 2).reshape(8, S).T
            flat = lambda t: t.transpose(1, 0, 2).reshape(S, MIX_WIDTH).astype(BF16)
            dkvf = jnp.concatenate([flat(dk_sh), flat(dv_sh),
                                    jnp.pad(dfl, ((0, 0), (0, LANES - 8))).astype(BF16)], axis=1)
            gb["w_kv_shared"] = _mm(shared["hs"], dkvf, mode="tn", out_dtype=F32, name="kv_shared_dw")[:, :W_KV_SHARED]
            dh, dhb, g_kvn = _mm(dkvf, w_kvf, mode="nt", epi="rms_bwd",
                                 extra=(shared["h"], vec(sm["kv_norm_g"]), dh), name="kv_shared_dx_norm_bwd")
            g_bf = db8[:, 0]

    gsmall = {n: jnp.concatenate(v, axis=0) for n, v in gs.items()}
    gsmall["kv_norm_g"] = g_kvn
    gsmall["final_norm_g"] = dg_final
    gsmall["b_f"] = g_bf
    return loss, dh, gb, gsmall, received


def kernel(x, mem, norm1_g, w_in_a, w_in_b, w_mem_kv, mem_norm_g, w_o, norm2_g, w_mlp1, w_mlp2, kv_norm_g, w_kv_shared, b_f, final_norm_g, loss_target, m_norm1_g, m_w_in_a, m_w_in_b, m_w_mem_kv, m_mem_norm_g, m_w_o, m_norm2_g, m_w_mlp1, m_w_mlp2, m_kv_norm_g, m_w_kv_shared, m_b_f, m_final_norm_g, v_norm1_g, v_w_in_a, v_w_in_b, v_w_mem_kv, v_mem_norm_g, v_w_o, v_norm2_g, v_w_mlp1, v_w_mlp2, v_kv_norm_g, v_w_kv_shared, v_b_f, v_final_norm_g):
    big_w = dict(w_in_a=w_in_a, w_in_b=w_in_b, w_mem_kv=w_mem_kv, w_o=w_o, w_mlp1=w_mlp1, w_mlp2=w_mlp2,
                 w_kv_shared=w_kv_shared)
    small_w = dict(norm1_g=norm1_g, mem_norm_g=mem_norm_g, norm2_g=norm2_g, kv_norm_g=kv_norm_g,
                   final_norm_g=final_norm_g, b_f=b_f)
    big_m = dict(w_in_a=m_w_in_a, w_in_b=m_w_in_b, w_mem_kv=m_w_mem_kv, w_o=m_w_o, w_mlp1=m_w_mlp1,
                 w_mlp2=m_w_mlp2, w_kv_shared=m_w_kv_shared)
    small_m = dict(norm1_g=m_norm1_g, mem_norm_g=m_mem_norm_g, norm2_g=m_norm2_g, kv_norm_g=m_kv_norm_g,
                   final_norm_g=m_final_norm_g, b_f=m_b_f)
    big_v = dict(w_in_a=v_w_in_a, w_in_b=v_w_in_b, w_mem_kv=v_w_mem_kv, w_o=v_w_o, w_mlp1=v_w_mlp1,
                 w_mlp2=v_w_mlp2, w_kv_shared=v_w_kv_shared)
    small_v = dict(norm1_g=v_norm1_g, mem_norm_g=v_mem_norm_g, norm2_g=v_norm2_g, kv_norm_g=v_kv_norm_g,
                   final_norm_g=v_final_norm_g, b_f=v_b_f)

    def pack(k, big, small, dtype):
        return _pack_local(PARTS[k], PART_ROWS[k], big, small if k == 0 else None, dtype)

    def pack_all(big, small):
        return jnp.concatenate([pack(k, big, small, F32) for k in range(len(PARTS))], axis=0)

    wb = {n: {} for n in BIG_NAMES}
    _unpack_gathered(PARTS[0], _allgather_chips(pack(0, big_w, small_w, BF16)), wb)
    shards = [pack(1 + l, big_w, None, BF16) for l in range(N_A)]

    loss, dx, gb, gsmall, received = _local_step(x[0], mem[0], wb, shards, small_w, loss_target[0])

    received = [_scatter_chips(_pack_grads(PARTS[0], PART_ROWS[0], gb, gsmall))] + received
    part = jnp.concatenate([_sum4(r) for r in received], axis=0)
    other = _swap_cores(part)
    g, delta, new_m, new_v = _adamw(part, other, pack_all(big_w, small_w), pack_all(big_m, small_m),
                                    pack_all(big_v, small_v))

    outs = [lax.psum(loss[0, 0], ("x", "y", "c")), dx[None]]
    for packed in (g, delta, new_m, new_v):
        pieces, d, off = {n: [] for n in BIG_NAMES}, {}, 0
        for k, part_k in enumerate(PARTS):
            _unpack_local(part_k, packed[off:off + PART_ROWS[k]], k == 0, pieces, d)
            off += PART_ROWS[k]
        d.update(_join_layers(pieces))
        outs.extend(d[n] for n in WEIGHT_ORDER)
    return tuple(outs)
```

```python
import functools
import math

import jax
import jax.numpy as jnp
from jax import lax
from jax.experimental import pallas as pl
from jax.experimental.pallas import tpu as pltpu

F32 = jnp.float32
BF16 = jnp.bfloat16

D_MODEL = 1024
HEAD_DIM = 64
MIX_WIDTH = 512
MEM_WIDTH = 256
MERGED_WIDTH = MIX_WIDTH + MEM_WIDTH
DEPTH = 4
N_A = 2
D_FF = 4096
EPS = 1e-6
NEG_INF = -1e30
SCALE = 1.0 / math.sqrt(HEAD_DIM)

ADAM_LR = 0.001
ADAM_B1 = 0.9
ADAM_B2 = 0.999
ADAM_EPS = 1e-08
ADAM_WD = 0.01
ADAM_STEP = 10

LANES = 128
GROUP_COLS = MIX_WIDTH // LANES
Q_MEM_COL_A = 3 * GROUP_COLS
Q_MEM_COL_B = GROUP_COLS
W_KV_SHARED = 2 * MIX_WIDTH + 8
KVF_WIDTH = 1152
BQ = 256
BK = 128
DIAG_TILES = BQ // BK
CHAINS = 8
UNDERFLOW_BOUND = -110.0
VMEM_LIMIT = 56 * 1024 * 1024

MESH = pl.DeviceIdType.MESH
N_CHIPS = 4

PARTS = (
    (("w_in_a", 0, 1, (1024, 448), 1),
     ("w_mem_kv", 0, 1, (256, 512), 0)),
    (("w_o", 0, 1, (768, 256), 1),
     ("w_mlp1", 0, 1, (1024, 1024), 1),
     ("w_mlp2", 0, 1, (1024, 1024), 0),
     ("w_in_a", 1, 2, (1024, 448), 1),
     ("w_mem_kv", 1, 2, (256, 512), 0)),
    (("w_o", 1, 4, (768, 256), 1),
     ("w_mlp1", 1, 4, (1024, 1024), 1),
     ("w_mlp2", 1, 4, (1024, 1024), 0),
     ("w_in_b", 0, 2, (256, 768), 0),
     ("w_mem_kv", 2, 4, (256, 512), 0),
     ("w_kv_shared", None, None, (1024, 258), 1)),
)
BIG_NAMES = ("w_in_a", "w_in_b", "w_mem_kv", "w_o", "w_mlp1", "w_mlp2", "w_kv_shared")
SMALL = (
    ("norm1_g", (4, 1024)),
    ("mem_norm_g", (4, 1024)),
    ("norm2_g", (4, 1024)),
    ("kv_norm_g", (1, 1024)),
    ("final_norm_g", (1, 1024)),
    ("b_f", (1, 1024)),
)
WEIGHT_ORDER = ("norm1_g", "w_in_a", "w_in_b", "w_mem_kv", "mem_norm_g", "w_o", "norm2_g", "w_mlp1",
                "w_mlp2", "kv_norm_g", "w_kv_shared", "b_f", "final_norm_g")


ROW_ALIGN = 16
PACK_TILE = 256
SMALL_ROWS = ROW_ALIGN
assert sum(s[0] for _, s in SMALL) <= SMALL_ROWS


def _section_rows(entry):
    _, lo, hi, shape, _ = entry
    rows = (1 if lo is None else hi - lo) * math.prod(shape) // D_MODEL
    return rows, -(-rows // ROW_ALIGN) * ROW_ALIGN


def _round_up(n, m):
    return -(-n // m) * m


SUM_TILE = 128
_used = [sum(_section_rows(e)[1] for e in part) for part in PARTS]
PART_ROWS = [_round_up(_used[0] + SMALL_ROWS, SUM_TILE), _round_up(_used[1], SUM_TILE)]
PART_ROWS.append(_round_up(_used[2] + sum(PART_ROWS), PACK_TILE) - sum(PART_ROWS))
assert PART_ROWS[2] % SUM_TILE == 0


def _params(sem=None):
    return pltpu.CompilerParams(dimension_semantics=sem, vmem_limit_bytes=VMEM_LIMIT)


def _pick(n, cands):
    for c in cands:
        if n % c == 0:
            return c
    raise ValueError(f"no tile for {n}")


def _section(a, entry):
    a = a.reshape(-1, D_MODEL)
    return jnp.pad(a, ((0, _section_rows(entry)[1] - a.shape[0]), (0, 0)))


def _small_block(small, dtype):
    blk = jnp.zeros((SMALL_ROWS, D_MODEL), dtype)
    off = 0
    for n, shp in SMALL:
        a = small[n].astype(dtype)
        if n == "b_f":
            blk = blk.at[off, :a.size].set(a.reshape(-1))
        else:
            blk = blk.at[off:off + shp[0]].set(a.reshape(shp))
        off += shp[0]
    return blk


def _fill(parts, rows, dtype):
    used = sum(p.shape[0] for p in parts)
    return jnp.concatenate(parts + [jnp.zeros((rows - used, D_MODEL), dtype)], axis=0)


def _pack_local(part, rows, big, small, dtype):
    parts = [_section((big[e[0]] if e[1] is None else big[e[0]][e[1]:e[2]]).astype(dtype), e) for e in part]
    if small is not None:
        parts.append(_small_block(small, dtype))
    return _fill(parts, rows, dtype)


def _unpack_local(part, p, with_small, pieces, small):
    off = 0
    for e in part:
        n, lo, hi, shp, _ = e
        rows, reserved = _section_rows(e)
        pieces[n].append((lo, p[off:off + rows].reshape(shp if lo is None else (hi - lo,) + shp)))
        off += reserved
    if with_small:
        for n, shp in SMALL:
            a = p[off:off + shp[0]]
            small[n] = a[0, :8] if n == "b_f" else (a.reshape(D_MODEL) if shp[0] == 1 else a)
            off += shp[0]


def _join_layers(pieces):
    out = {}
    for n, ps in pieces.items():
        ps = sorted(ps, key=lambda t: -1 if t[0] is None else t[0])
        out[n] = ps[0][1] if len(ps) == 1 else jnp.concatenate([a for _, a in ps], axis=0)
    return out


def _unpack_gathered(part, g, weights):
    off = 0
    for e in part:
        n, lo, hi, shp, ax = e
        rows, reserved = _section_rows(e)
        if lo is None:
            sec = g[:, off:off + rows].reshape((N_CHIPS,) + shp)
            weights[n] = jnp.concatenate([sec[j] for j in range(N_CHIPS)], axis=ax)
        else:
            sec = g[:, off:off + rows].reshape((N_CHIPS, hi - lo) + shp)
            for l in range(lo, hi):
                weights[n][l] = jnp.concatenate([sec[j, l - lo] for j in range(N_CHIPS)], axis=ax)
        off += reserved


def _pack_grads(part, rows, gbig, gsmall):
    small = None if gsmall is None else _small_block(gsmall, BF16)
    chunks = []
    for j in range(N_CHIPS):
        parts = []
        for e in part:
            n, lo, hi, shp, ax = e
            w = shp[ax]
            layers = [gbig[n]] if lo is None else [gbig[n][l] for l in range(lo, hi)]
            cut = [lax.slice_in_dim(g, j * w, (j + 1) * w, axis=ax).astype(BF16).reshape(-1, D_MODEL) for g in layers]
            parts.append(_section(cut[0] if len(cut) == 1 else jnp.concatenate(cut, axis=0), e))
        if small is not None:
            parts.append(small)
        chunks.append(_fill(parts, rows, BF16))
    return jnp.stack(chunks, axis=0)


ANY = pl.BlockSpec(memory_space=pl.ANY)


def _other_chips(x, y):
    return [(1 - x, y), (x, 1 - y), (1 - x, 1 - y)]


class _AllGather:
    SCRATCH = [pltpu.SemaphoreType.DMA((3,)), pltpu.SemaphoreType.DMA((3,)), pltpu.SemaphoreType.DMA((3,)),
               pltpu.SemaphoreType.DMA((3,)), pltpu.SemaphoreType.DMA]

    def __init__(self, w_ref, o_ref, send_sems, recv_sems, pass_send, pass_recv, local_sem):
        self.w_ref, self.o_ref = w_ref, o_ref
        self.sems = (send_sems, recv_sems, pass_send, pass_recv, local_sem)
        x, y, c = lax.axis_index("x"), lax.axis_index("y"), lax.axis_index("c")
        half = w_ref.shape[0] // 2
        self.c, self.me, self.sibling = c, 2 * x + y, (x, y, 1 - c)
        self.mine = pl.ds(pl.multiple_of(c * half, ROW_ALIGN), half)
        self.other = pl.ds(pl.multiple_of((1 - c) * half, ROW_ALIGN), half)
        self.chips = _other_chips(x, y)

    def _over_ici(self, j, rows_of):
        chip = self.chips[j]
        return pltpu.make_async_remote_copy(
            src_ref=self.w_ref.at[self.mine], dst_ref=self.o_ref.at[rows_of, self.mine],
            send_sem=self.sems[0].at[j], recv_sem=self.sems[1].at[j],
            device_id=(chip[0], chip[1], self.c), device_id_type=MESH)

    def _over_d2d(self, j, rows):
        where = self.o_ref.at[2 * self.chips[j][0] + self.chips[j][1], rows]
        return pltpu.make_async_remote_copy(src_ref=where, dst_ref=where, send_sem=self.sems[2].at[j],
                                            recv_sem=self.sems[3].at[j], device_id=self.sibling,
                                            device_id_type=MESH)

    def _local(self):
        return pltpu.make_async_copy(self.w_ref, self.o_ref.at[self.me], self.sems[4])

    def start(self):
        self._local().start()
        for j in range(3):
            self._over_ici(j, self.me).start()

    def finish(self):
        for j in range(3):
            self._over_ici(j, 2 * self.chips[j][0] + self.chips[j][1]).wait_recv()
            self._over_d2d(j, self.mine).start()
        for j in range(3):
            self._over_d2d(j, self.other).wait_recv()
        for j in range(3):
            self._over_ici(j, self.me).wait_send()
            self._over_d2d(j, self.mine).wait_send()
        self._local().wait()


class _Scatter:
    SCRATCH = [pltpu.SemaphoreType.DMA((3,)), pltpu.SemaphoreType.DMA((3,)), pltpu.SemaphoreType.DMA]

    def __init__(self, g_ref, o_ref, send_sems, recv_sems, local_sem):
        self.g_ref, self.o_ref, self.sems = g_ref, o_ref, (send_sems, recv_sems, local_sem)
        x, y, c = lax.axis_index("x"), lax.axis_index("y"), lax.axis_index("c")
        self.c, self.me, self.chips = c, 2 * x + y, _other_chips(x, y)

    def _copy(self, j):
        chip = self.chips[j]
        return pltpu.make_async_remote_copy(
            src_ref=self.g_ref.at[2 * chip[0] + chip[1]], dst_ref=self.o_ref.at[self.me],
            send_sem=self.sems[0].at[j], recv_sem=self.sems[1].at[j],
            device_id=(chip[0], chip[1], self.c), device_id_type=MESH)

    def _local(self):
        return pltpu.make_async_copy(self.g_ref.at[self.me], self.o_ref.at[self.me], self.sems[2])

    def start(self):
        self._local().start()
        for j in range(3):
            self._copy(j).start()

    def finish(self):
        for j in range(3):
            self._copy(j).wait()
        self._local().wait()


def _allgather_chips(w):
    def body(w_ref, o_ref, *sems):
        ag = _AllGather(w_ref, o_ref, *sems)
        ag.start()
        ag.finish()

    return pl.pallas_call(
        body, name="allgather_weights",
        out_shape=jax.ShapeDtypeStruct((N_CHIPS,) + w.shape, w.dtype),
        in_specs=[ANY], out_specs=ANY, scratch_shapes=_AllGather.SCRATCH,
    )(w)


def _scatter_chips(g4):
    def body(g_ref, o_ref, *sems):
        sc = _Scatter(g_ref, o_ref, *sems)
        sc.start()
        sc.finish()

    return pl.pallas_call(
        body, name="scatter_grads",
        out_shape=jax.ShapeDtypeStruct(g4.shape, g4.dtype),
        in_specs=[ANY], out_specs=ANY, scratch_shapes=_Scatter.SCRATCH,
    )(g4)


def _swap_cores(p):
    def body(p_ref, o_ref, send_sem, recv_sem):
        x, y, c = lax.axis_index("x"), lax.axis_index("y"), lax.axis_index("c")
        cp = pltpu.make_async_remote_copy(src_ref=p_ref, dst_ref=o_ref, send_sem=send_sem, recv_sem=recv_sem,
                                          device_id=(x, y, 1 - c), device_id_type=MESH)
        cp.start()
        cp.wait()

    return pl.pallas_call(
        body, name="swap_cores",
        out_shape=jax.ShapeDtypeStruct(p.shape, p.dtype),
        in_specs=[ANY], out_specs=ANY,
        scratch_shapes=[pltpu.SemaphoreType.DMA, pltpu.SemaphoreType.DMA],
    )(p)


def _sum4(r4):
    _, R, C = r4.shape

    def body(r_ref, o_ref):
        o_ref[...] = ((r_ref[0].astype(F32) + r_ref[1].astype(F32)) + r_ref[2].astype(F32)) + r_ref[3].astype(F32)

    return pl.pallas_call(
        body, name="sum_chips", grid=(R // SUM_TILE,),
        in_specs=[pl.BlockSpec((N_CHIPS, SUM_TILE, C), lambda i: (0, i, 0))],
        out_specs=pl.BlockSpec((SUM_TILE, C), lambda i: (i, 0)),
        out_shape=jax.ShapeDtypeStruct((R, C), F32),
        compiler_params=_params(("parallel",)),
    )(r4)


def _adamw(pa, pb, w, m, v):
    R, C = w.shape
    c1 = 1.0 - ADAM_B1
    c2 = 1.0 - ADAM_B2
    bc1 = 1.0 - ADAM_B1 ** ADAM_STEP
    bc2 = 1.0 - ADAM_B2 ** ADAM_STEP

    def body(pa_ref, pb_ref, w_ref, m_ref, v_ref, g_ref, d_ref, mo_ref, vo_ref):
        g = pa_ref[...] + pb_ref[...]
        mn = ADAM_B1 * m_ref[...] + c1 * g
        vn = ADAM_B2 * v_ref[...] + c2 * (g * g)
        m_hat = mn / bc1
        v_hat = vn / bc2
        g_ref[...] = g
        d_ref[...] = -ADAM_LR * (m_hat / (jnp.sqrt(v_hat) + ADAM_EPS) + ADAM_WD * w_ref[...])
        mo_ref[...] = mn
        vo_ref[...] = vn

    spec = pl.BlockSpec((PACK_TILE, C), lambda i: (i, 0))
    shp = jax.ShapeDtypeStruct((R, C), F32)
    return pl.pallas_call(
        body, name="adamw", grid=(R // PACK_TILE,),
        in_specs=[spec] * 5, out_specs=[spec] * 4, out_shape=[shp] * 4,
        compiler_params=_params(("parallel",)),
    )(pa, pb, w, m, v)


def _rms_fwd(x, g, name):
    R, Dm = x.shape
    tr = _pick(R, (512, 256, 128))

    def body(x_ref, g_ref, o_ref):
        xf = x_ref[...]
        r = lax.rsqrt(jnp.mean(xf * xf, axis=-1, keepdims=True) + EPS)
        o_ref[...] = (xf * r * g_ref[...]).astype(o_ref.dtype)

    return pl.pallas_call(
        body, name=name, grid=(R // tr,),
        in_specs=[pl.BlockSpec((tr, Dm), lambda i: (i, 0)), pl.BlockSpec((1, Dm), lambda i: (0, 0))],
        out_specs=pl.BlockSpec((tr, Dm), lambda i: (i, 0)),
        out_shape=jax.ShapeDtypeStruct((R, Dm), BF16),
        compiler_params=_params(("parallel",)),
    )(x, g)


def _rms_gain_grad(x, dy, name):
    R, Dm = x.shape
    tr = _pick(R, (256, 128))

    def body(x_ref, dy_ref, dg_ref):
        xf = x_ref[...]
        r = lax.rsqrt(jnp.mean(xf * xf, axis=-1, keepdims=True) + EPS)

        @pl.when(pl.program_id(0) == 0)
        def _():
            dg_ref[...] = jnp.zeros_like(dg_ref)

        dg_ref[...] += jnp.sum(dy_ref[...] * (xf * r), axis=0, keepdims=True)

    row = pl.BlockSpec((tr, Dm), lambda i: (i, 0))
    return pl.pallas_call(
        body, name=name, grid=(R // tr,),
        in_specs=[row, row], out_specs=pl.BlockSpec((1, Dm), lambda i: (0, 0)),
        out_shape=jax.ShapeDtypeStruct((1, Dm), F32),
        compiler_params=_params(("arbitrary",)),
    )(x, dy)


def _final_loss(x, g, tgt):
    R, Dm = x.shape
    tr = _pick(R, (256, 128))

    def body(x_ref, g_ref, t_ref, l_ref, dx_ref, dxb_ref, dg_ref):
        xf = x_ref[...]
        gv = g_ref[...]
        r = lax.rsqrt(jnp.mean(xf * xf, axis=-1, keepdims=True) + EPS)
        xr = xf * r
        err = xr * gv - t_ref[...]
        dy_ = err * (1.0 / Dm)
        gdy = dy_ * gv
        mdot = jnp.mean(xf * gdy, axis=-1, keepdims=True)
        dx = r * gdy - xf * ((r * r * r) * mdot)
        dx_ref[...] = dx
        dxb_ref[...] = dx.astype(BF16)

        @pl.when(pl.program_id(0) == 0)
        def _():
            dg_ref[...] = jnp.zeros_like(dg_ref)
            l_ref[...] = jnp.zeros_like(l_ref)

        dg_ref[...] += jnp.sum(dy_ * xr, axis=0, keepdims=True)
        sq = jnp.sum(err * err, axis=1, keepdims=True)
        l_ref[...] += jnp.sum(sq, axis=0, keepdims=True) * (0.5 / Dm)

    row = pl.BlockSpec((tr, Dm), lambda i: (i, 0))
    vec = pl.BlockSpec((1, Dm), lambda i: (0, 0))
    return pl.pallas_call(
        body, name="final_norm_loss", grid=(R // tr,),
        in_specs=[row, vec, row],
        out_specs=[pl.BlockSpec((1, 1), lambda i: (0, 0)), row, row, vec],
        out_shape=[jax.ShapeDtypeStruct((1, 1), F32), jax.ShapeDtypeStruct((R, Dm), F32),
                   jax.ShapeDtypeStruct((R, Dm), BF16), jax.ShapeDtypeStruct((1, Dm), F32)],
        compiler_params=_params(("arbitrary",)),
    )(x, g, tgt)


MAX_TK = 2048

_DIMS = {"nn": (((1,), (0,)), ((), ())), "nt": (((1,), (1,)), ((), ())), "tn": (((0,), (0,)), ((), ()))}


def _mm(a, b, *, mode="nn", out_dtype=BF16, epi=None, extra=None, name):
    if mode == "nn":
        (M, K), N = a.shape, b.shape[1]
    elif mode == "nt":
        (M, K), N = a.shape, b.shape[0]
    else:
        (K, M), N = a.shape, b.shape[1]
    deep = K > MAX_TK
    tm = _pick(M, (512, 256, 128) if (epi == "rms_bwd" or deep) else (1024, 768, 512, 256, 128))
    wide = (2048,) if (K <= MAX_TK // 2 and mode != "tn") else ()
    tn = _pick(N, wide + (1024, 896, 768, 640, 512, 384, 256, 128))
    tk = K if K <= (2 * MAX_TK if deep else MAX_TK) else _pick(K, (2 * MAX_TK, MAX_TK, 1024, 512, 256, 128))
    nk = K // tk
    extras = () if extra is None else (extra if isinstance(extra, tuple) else (extra,))
    n_out = {"relu2": 2, "add_rms": 2, "rms_bwd": 3}.get(epi, 1)
    assert epi not in ("rms_bwd", "add_rms") or tn == N

    def body(*refs):
        a_ref, b_ref = refs[:2]
        e_refs = refs[2:2 + len(extras)]
        e_ref = e_refs[0] if e_refs else None
        outs = refs[2 + len(extras):2 + len(extras) + n_out]
        k = pl.program_id(2)
        part = lax.dot_general(a_ref[...].astype(BF16), b_ref[...].astype(BF16), _DIMS[mode],
                               preferred_element_type=F32)

        def finish(acc):
            if epi is None:
                outs[0][...] = acc.astype(outs[0].dtype)
            elif epi == "add":
                outs[0][...] = (e_ref[...] + acc).astype(outs[0].dtype)
            elif epi == "add_rms":
                y = e_refs[0][...] + acc
                outs[0][...] = y
                r = lax.rsqrt(jnp.mean(y * y, axis=-1, keepdims=True) + EPS)
                outs[1][...] = (y * r * e_refs[1][...]).astype(BF16)
            elif epi == "relu2":
                outs[0][...] = acc.astype(BF16)
                rl = jnp.maximum(acc, 0.0)
                outs[1][...] = (rl * rl).astype(BF16)
            elif epi == "drelu2":
                u = e_ref[...].astype(F32)
                outs[0][...] = (acc * (2.0 * jnp.maximum(u, 0.0))).astype(outs[0].dtype)
            elif epi == "rms_bwd":
                x_ref, g_ref, dres_ref = e_refs
                xf = x_ref[...]
                r = lax.rsqrt(jnp.mean(xf * xf, axis=-1, keepdims=True) + EPS)
                gdy = acc * g_ref[...]
                mdot = jnp.mean(xf * gdy, axis=-1, keepdims=True)
                dx = dres_ref[...] + (r * gdy - xf * ((r * r * r) * mdot))
                outs[0][...] = dx
                outs[1][...] = dx.astype(BF16)

                @pl.when(pl.program_id(0) == 0)
                def _():
                    outs[2][...] = jnp.zeros_like(outs[2])

                outs[2][...] += jnp.sum(acc * (xf * r), axis=0, keepdims=True)

        if nk == 1:
            finish(part)
        else:
            acc_ref = refs[-1]

            @pl.when(k == 0)
            def _():
                acc_ref[...] = part

            @pl.when(jnp.logical_and(k > 0, k < nk - 1))
            def _():
                acc_ref[...] += part

            @pl.when(k == nk - 1)
            def _():
                finish(acc_ref[...] + part)

    if mode == "tn":
        a_spec = pl.BlockSpec((tk, tm), lambda i, j, k: (k, i))
    else:
        a_spec = pl.BlockSpec((tm, tk), lambda i, j, k: (i, k))
    if mode == "nt":
        b_spec = pl.BlockSpec((tn, tk), lambda i, j, k: (j, k))
    else:
        b_spec = pl.BlockSpec((tk, tn), lambda i, j, k: (k, j))
    o_spec = pl.BlockSpec((tm, tn), lambda i, j, k: (i, j))
    vec_spec = pl.BlockSpec((1, tn), lambda i, j, k: (0, j))
    ins, in_specs = [a, b] + list(extras), [a_spec, b_spec]
    if epi == "rms_bwd":
        in_specs += [o_spec, vec_spec, o_spec]
        out_shape = [jax.ShapeDtypeStruct((M, N), F32), jax.ShapeDtypeStruct((M, N), BF16),
                     jax.ShapeDtypeStruct((1, N), F32)]
        out_specs = [o_spec, o_spec, vec_spec]
    elif epi == "add_rms":
        in_specs += [o_spec, vec_spec]
        out_shape = [jax.ShapeDtypeStruct((M, N), F32), jax.ShapeDtypeStruct((M, N), BF16)]
        out_specs = [o_spec, o_spec]
    else:
        in_specs += [o_spec] * len(extras)
        out_shape = [jax.ShapeDtypeStruct((M, N), BF16 if epi == "relu2" else out_dtype)] * n_out
        out_specs = [o_spec] * n_out
    res = pl.pallas_call(
        body, name=name, grid=(M // tm, N // tn, nk),
        in_specs=in_specs, out_specs=out_specs, out_shape=out_shape,
        scratch_shapes=[pltpu.VMEM((tm, tn), F32)] if nk > 1 else [],
        compiler_params=_params(("arbitrary",) * 3 if epi == "rms_bwd" else ("parallel", "parallel", "arbitrary")),
    )(*ins)
    return res if n_out > 1 else res[0]


def _dot(a, b):
    return lax.dot_general(a, b, _DIMS["nn"], preferred_element_type=F32)


def _dot_nt(a, b):
    return lax.dot_general(a, b, _DIMS["nt"], preferred_element_type=F32)


def _dot_tn(a, b):
    return lax.dot_general(a, b, _DIMS["tn"], preferred_element_type=F32)


def _split_dot(x, t):
    hi = x.astype(BF16)
    lo = (x - hi.astype(F32)).astype(BF16)
    return _dot(jnp.concatenate([hi, lo], axis=1), jnp.concatenate([t, t], axis=0))


def _head_pair(ref, scale=None):
    xf = ref[...].astype(F32)
    if scale is not None:
        xf = xf * scale
    is_a = lax.broadcasted_iota(jnp.int32, xf.shape, 1) < HEAD_DIM
    return jnp.where(is_a, xf, 0.0).astype(BF16), jnp.where(is_a, 0.0, xf).astype(BF16)


def _stack(a, b):
    return jnp.concatenate([a, b], axis=0)


def _head_rows(ref, scale=None):
    return _stack(*_head_pair(ref, scale))


def _unstack_heads(x):
    rows = x.shape[0] // 2
    return _select_pair(x[:rows], x[rows:])


def _pair_rowsum(x):
    is_a = lax.broadcasted_iota(jnp.int32, x.shape, 1) < HEAD_DIM
    return (jnp.sum(jnp.where(is_a, x, 0.0), axis=1, keepdims=True),
            jnp.sum(jnp.where(is_a, 0.0, x), axis=1, keepdims=True))


def _select_pair(xa, xb):
    is_a = lax.broadcasted_iota(jnp.int32, xa.shape, 1) < HEAD_DIM
    return jnp.where(is_a, xa, xb)


def _two_cols(xa, xb):
    rows = xa.shape[0]
    first = lax.broadcasted_iota(jnp.int32, (rows, 2), 1) == 0
    return jnp.where(first, xa, xb)


def _softplus_parts(z):
    e = jnp.exp(-jnp.abs(z))
    return jnp.maximum(z, 0.0) + jnp.log(1.0 + e), e


def _tile_iotas():
    row = lax.broadcasted_iota(jnp.int32, (BK, BK), 0)
    col = lax.broadcasted_iota(jnp.int32, (BK, BK), 1)
    return row, col


def _stacked_iotas(bq, nk):
    row = lax.broadcasted_iota(jnp.int32, (2 * bq, nk), 0) & (bq - 1)
    col = lax.broadcasted_iota(jnp.int32, (2 * bq, nk), 1)
    return row, col


def _side_exchange(exchange, operand, n_in, n_out):
    out_shape = jax.ShapeDtypeStruct(((N_CHIPS,) + operand.shape) if exchange is _AllGather else operand.shape,
                                     operand.dtype)
    n_sem = len(exchange.SCRATCH)

    def pick(refs):
        def make():
            return exchange(refs[n_in], refs[n_in + 1 + n_out], *refs[len(refs) - n_sem:])

        return (lambda: make().start()), (lambda: make().finish())

    return [operand], [ANY], [out_shape], [ANY], pick


def _sb_fwd(proj, name, exchange, operand):
    S = proj.shape[0]
    nqb = S // (CHAINS * BQ)
    x_in, x_in_specs, x_out, x_out_specs, pick = _side_exchange(exchange, operand, 3, 1)

    def body(*refs):
        q_ref, k_ref, v_ref = refs[:3]
        o_ref = refs[3 + len(x_in)]
        acc_ref = refs[3 + len(x_in) + 1 + len(x_out)]
        start, finish = pick(refs)
        p = pl.program_id(0)
        i = pl.program_id(1)

        @pl.when(jnp.logical_and(p == 0, i == 0))
        def _():
            start()

        q2 = [_head_rows(q_ref.at[pl.ds(ch * BQ, BQ)], SCALE) for ch in range(CHAINS)]
        row, col = _tile_iotas()
        tri = (row > col).astype(BF16)
        srow, scol = _stacked_iotas(BQ, BK)
        acc_ref[...] = jnp.zeros_like(acc_ref)

        def tile(ch, kb, c, dmask=None, valid=None):
            r0 = pl.multiple_of(kb * BK, BK)
            kblk = k_ref[pl.ds(r0, BK), :]
            vblk = v_ref[pl.ds(r0, BK), :]
            z = _dot_nt(q2[ch], kblk)
            sp, _ = _softplus_parts(z)
            lm = -sp
            if dmask is not None:
                lm = jnp.where(dmask, lm, 0.0)
            btw = _split_dot(lm, tri)
            w = jnp.exp((z - sp) + btw + c)
            if dmask is not None:
                w = jnp.where(dmask, w, 0.0)
            if valid is not None:
                w = w * valid
            acc_ref[ch] += _dot(w.astype(BF16), vblk)
            return c + btw[:, 0:1] + lm[:, 0:1]

        def alive(c):
            return jnp.max(c) > UNDERFLOW_BOUND

        cs = [jnp.zeros((2 * BQ, 1), F32)] * CHAINS
        for d in reversed(range(DIAG_TILES)):
            cs = [tile(ch, (CHAINS * i + ch) * DIAG_TILES + d, cs[ch], dmask=scol < srow - d * BK)
                  for ch in range(CHAINS)]

        def tile_of(ch, t):
            return (CHAINS * i + ch) * DIAG_TILES - 1 - t

        def more(cs, t):
            go = [jnp.logical_and(alive(cs[ch]), tile_of(ch, t) >= 0) for ch in range(CHAINS)]
            return functools.reduce(jnp.logical_or, go).astype(jnp.int32)

        def step(st):
            t, _, cs = st
            new = []
            for ch in range(CHAINS):
                kb = tile_of(ch, t)
                if ch == CHAINS - 1:
                    new.append(tile(ch, kb, cs[ch]))
                else:
                    new.append(tile(ch, jnp.maximum(kb, 0), cs[ch], valid=(kb >= 0).astype(F32)))
            return t + 1, more(new, t + 1), new

        lax.while_loop(lambda st: st[1] > 0, step, (0, more(cs, 0), cs))
        for ch in range(CHAINS):
            o_ref[pl.ds(ch * BQ, BQ), :] = _unstack_heads(acc_ref[ch])

        @pl.when(jnp.logical_and(p == 3, i == nqb - 1))
        def _():
            finish()

    blk = pl.BlockSpec((CHAINS * BQ, LANES), lambda p, i: (i, p))
    res = pl.pallas_call(
        body, name=name, grid=(4, nqb),
        in_specs=[blk, pl.BlockSpec((S, LANES), lambda p, i: (0, GROUP_COLS + p)),
                  pl.BlockSpec((S, LANES), lambda p, i: (0, 2 * GROUP_COLS + p))] + x_in_specs,
        out_specs=[blk] + x_out_specs,
        out_shape=[jax.ShapeDtypeStruct((S, MERGED_WIDTH), F32)] + x_out,
        scratch_shapes=[pltpu.VMEM((CHAINS, 2 * BQ, LANES), F32)] + exchange.SCRATCH,
        compiler_params=_params(("arbitrary", "arbitrary")),
    )(proj, proj, proj, *x_in)
    return res


def _sb_bwd(proj, merged, dmerged, name, exchange, operand):
    S = proj.shape[0]
    nqb = S // (CHAINS * BQ)
    x_in, x_in_specs, x_out, x_out_specs, pick = _side_exchange(exchange, operand, 5, 3)

    def body(*refs):
        q_ref, k_ref, v_ref, o_ref, do_ref = refs[:5]
        dq_ref, dk_hbm, dv_hbm = refs[5 + len(x_in):8 + len(x_in)]
        dq_acc, dk_acc, dv_acc, sem = refs[8 + len(x_in) + len(x_out):12 + len(x_in) + len(x_out)]
        start, finish = pick(refs)
        p = pl.program_id(0)
        i = pl.program_id(1)

        @pl.when(jnp.logical_and(p == 0, i == 0))
        def _():
            start()

        @pl.when(i == 0)
        def _():
            dk_acc[...] = jnp.zeros_like(dk_acc)
            dv_acc[...] = jnp.zeros_like(dv_acc)

        rows = [pl.ds(ch * BQ, BQ) for ch in range(CHAINS)]
        q2 = [_head_rows(q_ref.at[rw], SCALE) for rw in rows]
        do2 = [_head_rows(do_ref.at[rw]) for rw in rows]
        tot = [_stack(*_pair_rowsum(do_ref[rw, :].astype(F32) * o_ref[rw, :])) for rw in rows]
        row, col = _tile_iotas()
        tri_gt = (row > col).astype(BF16)
        tri_ge = (row >= col).astype(BF16)
        srow, scol = _stacked_iotas(BQ, BK)
        dq_acc[...] = jnp.zeros_like(dq_acc)

        def tile(ch, kb, st, dmask=None, valid=None):
            masked = dmask is not None
            c, r = st
            r0 = pl.multiple_of(kb * BK, BK)
            kblk = k_ref[pl.ds(r0, BK), :]
            vblk = v_ref[pl.ds(r0, BK), :]
            z = _dot_nt(q2[ch], kblk)
            sp, e = _softplus_parts(z)
            lm = -sp
            if masked:
                lm = jnp.where(dmask, lm, 0.0)
            btw = _split_dot(lm, tri_gt)
            w = jnp.exp((z - sp) + btw + c)
            if masked:
                w = jnp.where(dmask, w, 0.0)
            if valid is not None:
                w = w * valid
            wb = w.astype(BF16)
            a = wb.astype(F32) * _dot_nt(do2[ch], vblk)
            suffix = _split_dot(a, tri_ge) + r
            rcp = 1.0 / (1.0 + e)
            pos = z >= 0.0
            sig = jnp.where(pos, rcp, e * rcp)
            sig_neg = jnp.where(pos, e * rcp, rcp)
            dz = a * sig_neg - (tot[ch] - suffix) * sig
            if masked:
                dz = jnp.where(dmask, dz, 0.0)
            if valid is not None:
                dz = dz * valid
            dzb = dz.astype(BF16)
            dq_acc[ch] += _dot(dzb, kblk)
            dk_acc[pl.ds(r0, BK), :] += _dot_tn(dzb, q2[ch])
            dv_acc[pl.ds(r0, BK), :] += _dot_tn(wb, do2[ch])
            return c + btw[:, 0:1] + lm[:, 0:1], suffix[:, 0:1]

        def alive(st):
            return jnp.max(st[0]) > UNDERFLOW_BOUND

        zero = jnp.zeros((2 * BQ, 1), F32)
        sts = [(zero, zero)] * CHAINS
        for d in reversed(range(DIAG_TILES)):
            sts = [tile(ch, (CHAINS * i + ch) * DIAG_TILES + d, sts[ch], dmask=scol < srow - d * BK)
                   for ch in range(CHAINS)]

        def tile_of(ch, t):
            return (CHAINS * i + ch) * DIAG_TILES - 1 - t

        def more(sts, t):
            go = [jnp.logical_and(alive(sts[ch]), tile_of(ch, t) >= 0) for ch in range(CHAINS)]
            return functools.reduce(jnp.logical_or, go).astype(jnp.int32)

        def step(s):
            t, _, sts = s
            new = []
            for ch in range(CHAINS):
                kb = tile_of(ch, t)
                if ch == CHAINS - 1:
                    new.append(tile(ch, kb, sts[ch]))
                else:
                    new.append(tile(ch, jnp.maximum(kb, 0), sts[ch], valid=(kb >= 0).astype(F32)))
            return t + 1, more(new, t + 1), new

        lax.while_loop(lambda s: s[1] > 0, step, (0, more(sts, 0), sts))
        for ch in range(CHAINS):
            dq_ref[rows[ch], :] = (_unstack_heads(dq_acc[ch]) * SCALE).astype(dq_ref.dtype)

        @pl.when(i == nqb - 1)
        def _():
            ck = pltpu.make_async_copy(dk_acc, dk_hbm.at[p], sem.at[0])
            cv = pltpu.make_async_copy(dv_acc, dv_hbm.at[p], sem.at[1])
            ck.start()
            cv.start()
            ck.wait()
            cv.wait()

        @pl.when(jnp.logical_and(p == 3, i == nqb - 1))
        def _():
            finish()

    blk = lambda off: pl.BlockSpec((CHAINS * BQ, LANES), lambda p, i: (i, off + p))
    slab = lambda off: pl.BlockSpec((S, LANES), lambda p, i: (0, off + p))
    return pl.pallas_call(
        body, name=name, grid=(4, nqb),
        in_specs=[blk(0), slab(GROUP_COLS), slab(2 * GROUP_COLS), blk(0), blk(0)] + x_in_specs,
        out_specs=[blk(0), ANY, ANY] + x_out_specs,
        out_shape=[jax.ShapeDtypeStruct((S, MIX_WIDTH), BF16),
                   jax.ShapeDtypeStruct((4, S, LANES), F32), jax.ShapeDtypeStruct((4, S, LANES), F32)] + x_out,
        scratch_shapes=[pltpu.VMEM((CHAINS, 2 * BQ, LANES), F32), pltpu.VMEM((S, LANES), F32),
                        pltpu.VMEM((S, LANES), F32), pltpu.SemaphoreType.DMA((2,))]
        + exchange.SCRATCH,
        compiler_params=_params(("arbitrary", "arbitrary")),
    )(proj, proj, proj, merged, dmerged, *x_in)


def _key_norm_max(k_ref, knorm_ref, nkb):
    def step(kb, m):
        r0 = pl.multiple_of(kb * BK, BK)
        blk = k_ref[pl.ds(r0, BK), :].astype(F32)
        sa, sb = _pair_rowsum(blk * blk)
        return (jnp.maximum(m[0], jnp.max(sa, axis=0, keepdims=True)),
                jnp.maximum(m[1], jnp.max(sb, axis=0, keepdims=True)))

    zero = jnp.zeros((1, 1), F32)
    ma, mb = lax.fori_loop(0, nkb, step, (zero, zero))
    knorm_ref[...] = _select_pair(jnp.broadcast_to(ma, (1, LANES)), jnp.broadcast_to(mb, (1, LANES)))


FQ = 512
FK = FQ
GATE_BLOCKS = FK // BK


def _key_gates(cr_ref, kb):
    blocks = [cr_ref[0, GATE_BLOCKS * kb + j] for j in range(GATE_BLOCKS)]
    per_head = [jnp.broadcast_to(jnp.concatenate([b[h:h + 1] for b in blocks], axis=1), (FQ, FK)) for h in range(2)]
    return _stack(*per_head)


def _last_gate(cr_ref, kb):
    last = cr_ref[0, GATE_BLOCKS * jnp.maximum(kb, 0) + GATE_BLOCKS - 1]
    return _stack(*[jnp.broadcast_to(last[h:h + 1, BK - 1:BK], (FQ, 1)) for h in range(2)])


def _logit_bound(q_ref, knorm_ref):
    qf = q_ref[...].astype(F32) * SCALE
    qa, qb = _pair_rowsum(qf * qf)
    kn = knorm_ref[...]
    return _stack(jnp.sqrt(qa * kn[:, 0:1]), jnp.sqrt(qb * kn[:, HEAD_DIM:HEAD_DIM + 1]))


def _causal_bias(bias_ref):
    srow, scol = _stacked_iotas(FQ, FK)
    bias_ref[...] = jnp.where(scol <= srow, 0.0, NEG_INF)


def _fox_fwd(proj, kv, c_col, c_row, name):
    S = proj.shape[0]
    nqb = S // FQ

    def body(q_ref, k_ref, v_ref, cc_ref, cr_ref, o_ref, lse_ref, acc_ref, knorm_ref, bias_ref):
        i = pl.program_id(1)

        @pl.when(i == 0)
        def _():
            _key_norm_max(k_ref, knorm_ref, S // BK)
            _causal_bias(bias_ref)

        q2 = _head_rows(q_ref, SCALE)
        bound = _logit_bound(q_ref, knorm_ref)
        cc = cc_ref[0]
        ct = _stack(cc[:, 0:1], cc[:, 1:2])
        acc_ref[...] = jnp.zeros_like(acc_ref)

        def tile(kb, st, on_diagonal):
            m, l = st
            r0 = pl.multiple_of(kb * FK, FK)
            kblk = k_ref[pl.ds(r0, FK), :]
            vblk = v_ref[pl.ds(r0, FK), :]
            z = _dot_nt(q2, kblk) + ct - _key_gates(cr_ref, kb) + bias_ref[...] * on_diagonal
            m_new = jnp.maximum(m, jnp.max(z, axis=1, keepdims=True))
            alpha = jnp.exp(m - m_new)
            pr = jnp.exp(z - m_new)
            acc_ref[...] = alpha * acc_ref[...] + _split_dot(pr, vblk)
            return m_new, alpha * l + jnp.sum(pr, axis=1, keepdims=True)

        def alive(kb, st):
            reach = bound + ct - _last_gate(cr_ref, kb) - st[0]
            return (jnp.max(reach) > UNDERFLOW_BOUND).astype(jnp.int32)

        neg = jnp.full((2 * FQ, 1), NEG_INF, F32)
        zero = jnp.zeros((2 * FQ, 1), F32)
        def cond(s):
            return jnp.logical_and(s[0] >= 0, s[1] > 0)

        def step(s):
            kb, _, st = s
            st = tile(kb, st, (kb == i).astype(F32))
            return kb - 1, alive(kb - 1, st), st

        _, _, (m, l) = lax.while_loop(cond, step, (i, jnp.int32(1), (neg, zero)))
        o_ref[...] = _unstack_heads(acc_ref[...] / l)
        lse = m + jnp.log(l)
        lse_ref[0] = _two_cols(lse[:FQ], lse[FQ:])

    return pl.pallas_call(
        body, name=name, grid=(4, nqb),
        in_specs=[pl.BlockSpec((FQ, LANES), lambda p, i: (i, p)),
                  pl.BlockSpec((S, LANES), lambda p, i: (0, p)),
                  pl.BlockSpec((S, LANES), lambda p, i: (0, GROUP_COLS + p)),
                  pl.BlockSpec((1, FQ, 2), lambda p, i: (p, i, 0)),
                  pl.BlockSpec((1, S // BK, 8, LANES), lambda p, i: (p, 0, 0, 0))],
        out_specs=[pl.BlockSpec((FQ, LANES), lambda p, i: (i, p)),
                   pl.BlockSpec((1, FQ, 2), lambda p, i: (p, i, 0))],
        out_shape=[jax.ShapeDtypeStruct((S, MERGED_WIDTH), F32), jax.ShapeDtypeStruct((4, S, 2), F32)],
        scratch_shapes=[pltpu.VMEM((2 * FQ, LANES), F32), pltpu.VMEM((1, LANES), F32),
                        pltpu.VMEM((2 * FQ, FK), F32)],
        compiler_params=_params(("arbitrary", "arbitrary")),
    )(proj, kv, kv, c_col, c_row)


def _fox_bwd(proj, kv, c_col, c_row, lse, merged, dmerged, dk_prev, dv_prev, dc_prev, name):
    S = proj.shape[0]
    nqb = S // FQ

    def body(q_ref, k_ref, v_ref, cc_ref, cr_ref, lse_ref, o_ref, do_ref, dkp_hbm, dvp_hbm, dcp_ref,
             dq_ref, dk_hbm, dv_hbm, dc_ref, dq_acc, dk_acc, dv_acc, knorm_ref, bias_ref, sem):
        p = pl.program_id(0)
        i = pl.program_id(1)

        @pl.when(i == 0)
        def _():
            ck = pltpu.make_async_copy(dkp_hbm.at[p], dk_acc, sem.at[0])
            cv = pltpu.make_async_copy(dvp_hbm.at[p], dv_acc, sem.at[1])
            ck.start()
            cv.start()
            dc_ref[...] = dcp_ref[...]
            _key_norm_max(k_ref, knorm_ref, S // BK)
            _causal_bias(bias_ref)
            ck.wait()
            cv.wait()

        q2 = _head_rows(q_ref, SCALE)
        do2 = _head_rows(do_ref)
        tot = _stack(*_pair_rowsum(do_ref[...].astype(F32) * o_ref[...]))
        bound = _logit_bound(q_ref, knorm_ref)
        cc = cc_ref[0]
        ct = _stack(cc[:, 0:1], cc[:, 1:2])
        ls = lse_ref[0]
        lse = _stack(ls[:, 0:1], ls[:, 1:2])
        sub = lax.broadcasted_iota(jnp.int32, (8, LANES), 0)
        dq_acc[...] = jnp.zeros_like(dq_acc)

        def tile(kb, masked):
            r0 = pl.multiple_of(kb * FK, FK)
            kblk = k_ref[pl.ds(r0, FK), :]
            vblk = v_ref[pl.ds(r0, FK), :]
            z = _dot_nt(q2, kblk) + ct - _key_gates(cr_ref, kb)
            if masked:
                z = z + bias_ref[...]
            pr = jnp.exp(z - lse)
            ds = pr * (_dot_nt(do2, vblk) - tot)
            dsb = ds.astype(BF16)
            dq_acc[...] += _dot(dsb, kblk)
            dk_acc[pl.ds(r0, FK), :] += _dot_tn(dsb, q2)
            dv_acc[pl.ds(r0, FK), :] += _dot_tn(pr.astype(BF16), do2)
            dca = jnp.sum(ds[:FQ], axis=0, keepdims=True)
            dcb = jnp.sum(ds[FQ:], axis=0, keepdims=True)
            for j in range(GATE_BLOCKS):
                cols = slice(j * BK, (j + 1) * BK)
                old = dc_ref[0, GATE_BLOCKS * kb + j]
                dc_ref[0, GATE_BLOCKS * kb + j] = jnp.where(sub == 0, old - dca[:, cols],
                                                            jnp.where(sub == 1, old - dcb[:, cols], old))

        def alive(kb):
            reach = bound + ct - _last_gate(cr_ref, kb) - lse
            return (jnp.max(reach) > UNDERFLOW_BOUND).astype(jnp.int32)

        tile(i, True)

        def cond(s):
            return jnp.logical_and(s[0] >= 0, s[1] > 0)

        def step(s):
            kb, _ = s
            tile(kb, False)
            return kb - 1, alive(kb - 1)

        lax.while_loop(cond, step, (i - 1, alive(i - 1)))
        dq_ref[...] = (_unstack_heads(dq_acc[...]) * SCALE).astype(dq_ref.dtype)

        @pl.when(i == nqb - 1)
        def _():
            ck = pltpu.make_async_copy(dk_acc, dk_hbm.at[p], sem.at[0])
            cv = pltpu.make_async_copy(dv_acc, dv_hbm.at[p], sem.at[1])
            ck.start()
            cv.start()
            ck.wait()
            cv.wait()

    blk = lambda off: pl.BlockSpec((FQ, LANES), lambda p, i: (i, off + p))
    slab = lambda off: pl.BlockSpec((S, LANES), lambda p, i: (0, off + p))
    cols = pl.BlockSpec((1, FQ, 2), lambda p, i: (p, i, 0))
    rows = pl.BlockSpec((1, S // BK, 8, LANES), lambda p, i: (p, 0, 0, 0))
    return pl.pallas_call(
        body, name=name, grid=(4, nqb),
        in_specs=[blk(0), slab(0), slab(GROUP_COLS), cols, rows, cols, blk(0), blk(0), ANY, ANY, rows],
        out_specs=[blk(0), ANY, ANY, rows],
        out_shape=[jax.ShapeDtypeStruct((S, MIX_WIDTH), BF16),
                   jax.ShapeDtypeStruct((4, S, LANES), F32), jax.ShapeDtypeStruct((4, S, LANES), F32),
                   jax.ShapeDtypeStruct((4, S // BK, 8, LANES), F32)],
        scratch_shapes=[pltpu.VMEM((2 * FQ, LANES), F32), pltpu.VMEM((S, LANES), F32),
                        pltpu.VMEM((S, LANES), F32), pltpu.VMEM((1, LANES), F32),
                        pltpu.VMEM((2 * FQ, FK), F32), pltpu.SemaphoreType.DMA((2,))],
        compiler_params=_params(("arbitrary", "arbitrary")),
    )(proj, kv, kv, c_col, c_row, lse, merged, dmerged, dk_prev, dv_prev, dc_prev)


def _lane_scan(x, reverse):
    lane = lax.broadcasted_iota(jnp.int32, x.shape, 1)
    d = 1
    while d < LANES:
        if reverse:
            x = x + jnp.where(lane < LANES - d, pltpu.roll(x, LANES - d, 1), 0.0)
        else:
            x = x + jnp.where(lane >= d, pltpu.roll(x, d, 1), 0.0)
        d *= 2
    return x


def _gate_fwd(fl3, b8):
    nb = fl3.shape[0]

    def body(fl_ref, b_ref, c_ref):
        def step(kb, carry):
            x = fl_ref[kb] + b_ref[...]
            sp, _ = _softplus_parts(-x)
            c = _lane_scan(-sp, False) + carry
            c_ref[kb] = c
            return c[:, LANES - 1:LANES]

        lax.fori_loop(0, nb, step, jnp.zeros((8, 1), F32))

    return pl.pallas_call(body, name="forget_gate_cumsum",
                          out_shape=jax.ShapeDtypeStruct(fl3.shape, F32),
                          compiler_params=_params())(fl3, b8)


def _gate_bwd(dc3, fl3, b8):
    nb = fl3.shape[0]

    def body(dc_ref, fl_ref, b_ref, dfl_ref, db_ref):
        def step(t, st):
            carry, dbs = st
            kb = nb - 1 - t
            g = _lane_scan(dc_ref[kb], True) + carry
            x = fl_ref[kb] + b_ref[...]
            e = jnp.exp(-jnp.abs(x))
            rcp = 1.0 / (1.0 + e)
            dfl = g * jnp.where(x >= 0.0, e * rcp, rcp)
            dfl_ref[kb] = dfl
            return g[:, 0:1], dbs + dfl

        _, dbs = lax.fori_loop(0, nb, step, (jnp.zeros((8, 1), F32), jnp.zeros((8, LANES), F32)))
        db_ref[...] = jnp.broadcast_to(jnp.sum(dbs, axis=1, keepdims=True), (8, LANES))

    return pl.pallas_call(body, name="forget_gate_bwd",
                          out_shape=[jax.ShapeDtypeStruct(fl3.shape, F32), jax.ShapeDtypeStruct((8, LANES), F32)],
                          compiler_params=_params())(dc3, fl3, b8)


MEM_TQ = 1024
MEM_COLS = MEM_WIDTH // LANES


def _mem_fwd(proj, qcol, mkv, mix, name):
    S = proj.shape[0]
    M = mkv.shape[0]

    def body(q_ref, mk_ref, mv_ref, mix_ref, o_ref, lse_ref):
        q2 = _head_rows(q_ref, SCALE)
        s = _dot_nt(q2, mk_ref[...])
        m = jnp.max(s, axis=1, keepdims=True)
        pr = jnp.exp(s - m)
        l = jnp.sum(pr, axis=1, keepdims=True)
        o_ref[...] = _unstack_heads(_dot(pr.astype(BF16), mv_ref[...]) / l)
        lse = m + jnp.log(l)
        lse_ref[0] = _two_cols(lse[:MEM_TQ], lse[MEM_TQ:])

    return pl.pallas_call(
        body, name=name, grid=(MEM_COLS, S // MEM_TQ),
        in_specs=[pl.BlockSpec((MEM_TQ, LANES), lambda p, i: (i, qcol + p)),
                  pl.BlockSpec((M, LANES), lambda p, i: (0, p)),
                  pl.BlockSpec((M, LANES), lambda p, i: (0, MEM_COLS + p)), ANY],
        out_specs=[pl.BlockSpec((MEM_TQ, LANES), lambda p, i: (i, GROUP_COLS + p)),
                   pl.BlockSpec((1, MEM_TQ, 2), lambda p, i: (p, i, 0))],
        out_shape=[jax.ShapeDtypeStruct((S, MERGED_WIDTH), F32), jax.ShapeDtypeStruct((2, S, 2), F32)],
        input_output_aliases={3: 0},
        compiler_params=_params(("parallel", "parallel")),
    )(proj, mkv, mkv, mix)


def _mem_bwd(proj, qcol, mkv, lse, merged, dmerged, name):
    S = proj.shape[0]
    M = mkv.shape[0]

    def body(q_ref, mk_ref, mv_ref, lse_ref, o_ref, do_ref, dq_ref, dmk_ref, dmv_ref):
        @pl.when(pl.program_id(1) == 0)
        def _():
            dmk_ref[...] = jnp.zeros_like(dmk_ref)
            dmv_ref[...] = jnp.zeros_like(dmv_ref)

        q2 = _head_rows(q_ref, SCALE)
        do2 = _head_rows(do_ref)
        tot = _stack(*_pair_rowsum(do_ref[...].astype(F32) * o_ref[...]))
        ls = lse_ref[0]
        pr = jnp.exp(_dot_nt(q2, mk_ref[...]) - _stack(ls[:, 0:1], ls[:, 1:2]))
        ds = pr * (_dot_nt(do2, mv_ref[...]) - tot)
        dsb = ds.astype(BF16)
        dmk_ref[...] += _dot_tn(dsb, q2)
        dmv_ref[...] += _dot_tn(pr.astype(BF16), do2)
        dq_ref[...] = (_unstack_heads(_dot(dsb, mk_ref[...])) * SCALE).astype(dq_ref.dtype)

    blk = lambda off: pl.BlockSpec((MEM_TQ, LANES), lambda p, i: (i, off + p))
    acc = pl.BlockSpec((M, LANES), lambda p, i: (0, p))
    return pl.pallas_call(
        body, name=name, grid=(MEM_COLS, S // MEM_TQ),
        in_specs=[blk(qcol), pl.BlockSpec((M, LANES), lambda p, i: (0, p)),
                  pl.BlockSpec((M, LANES), lambda p, i: (0, MEM_COLS + p)),
                  pl.BlockSpec((1, MEM_TQ, 2), lambda p, i: (p, i, 0)), blk(GROUP_COLS), blk(GROUP_COLS)],
        out_specs=[blk(0), acc, acc],
        out_shape=[jax.ShapeDtypeStruct((S, MEM_WIDTH), BF16), jax.ShapeDtypeStruct((M, MEM_WIDTH), F32),
                   jax.ShapeDtypeStruct((M, MEM_WIDTH), F32)],
        compiler_params=_params(("parallel", "arbitrary")),
    )(proj, mkv, mkv, lse, merged, dmerged)


def _c_layouts(c3):
    nb = c3.shape[0]
    pairs = c3.reshape(nb, 4, 2, LANES).transpose(1, 0, 2, 3)
    c_row = jnp.pad(pairs, ((0, 0), (0, 0), (0, 6), (0, 0)))
    c_col = pairs.transpose(0, 1, 3, 2).reshape(4, nb * LANES, 2)
    return c_col, c_row


def _local_step(x, mem, wb, shards, sm, loss_target):
    S = x.shape[0]
    nb = S // BK
    vec = lambda a: a.reshape(1, D_MODEL)
    b8 = jnp.broadcast_to(sm["b_f"].reshape(8, 1), (8, LANES))

    saved = []
    shared = None
    h = x
    hn = _rms_fwd(h, vec(sm["norm1_g"][0]), "norm1_0")
    for l in range(DEPTH):
        if l == N_A:
            w_kvf = jnp.pad(wb["w_kv_shared"], ((0, 0), (0, KVF_WIDTH - W_KV_SHARED)))
            hs = _rms_fwd(h, vec(sm["kv_norm_g"]), "kv_norm")
            kvf = _mm(hs, w_kvf, out_dtype=F32, name="kv_shared_proj")
            kv = kvf[:, :2 * MIX_WIDTH].astype(BF16)
            fl3 = kvf[:, 2 * MIX_WIDTH:2 * MIX_WIDTH + 8].T.reshape(8, nb, LANES).transpose(1, 0, 2)
            c3 = _gate_fwd(fl3, b8)
            c_col, c_row = _c_layouts(c3)
            shared = dict(h=h, hs=hs, kv=kv, fl3=fl3, c_col=c_col, c_row=c_row)
        mn = _rms_fwd(mem, vec(sm["mem_norm_g"][l]), f"mem_norm_{l}")
        mkv = _mm(mn, wb["w_mem_kv"][l], name=f"mem_kv_proj_{l}")
        if l < N_A:
            w_in = wb["w_in_a"][l]
            proj = _mm(hn, w_in, name=f"in_proj_{l}")
            mix, gathered = _sb_fwd(proj, f"stickbreak_fwd_{l}", _AllGather, shards[l])
            _unpack_gathered(PARTS[1 + l], gathered, wb)
            lse, qcol = None, Q_MEM_COL_A
        else:
            w_in = wb["w_in_b"][l - N_A]
            proj = _mm(hn, w_in, name=f"in_proj_{l}")
            mix, lse = _fox_fwd(proj, shared["kv"], shared["c_col"], shared["c_row"], f"fox_fwd_{l}")
            qcol = Q_MEM_COL_B
        merged, mlse = _mem_fwd(proj, qcol, mkv, mix, f"mem_attn_fwd_{l}")
        h_mid, hn2 = _mm(merged, wb["w_o"][l], epi="add_rms", extra=(h, vec(sm["norm2_g"][l])),
                         name=f"out_proj_{l}")
        u, act = _mm(hn2, wb["w_mlp1"][l], epi="relu2", name=f"mlp1_{l}")
        saved.append(dict(h=h, hn=hn, mn=mn, mkv=mkv, proj=proj, lse=lse, mlse=mlse, qcol=qcol, merged=merged,
                          h_mid=h_mid, hn2=hn2, u=u, act=act, w_in=w_in))
        if l + 1 < DEPTH:
            h, hn = _mm(act, wb["w_mlp2"][l], epi="add_rms", extra=(h_mid, vec(sm["norm1_g"][l + 1])),
                        name=f"mlp2_{l}")
        else:
            h = _mm(act, wb["w_mlp2"][l], out_dtype=F32, epi="add", extra=h_mid, name=f"mlp2_{l}")

    loss, dh, dhb, dg_final = _final_loss(h, vec(sm["final_norm_g"]), loss_target)

    gb = {n: [None] * (DEPTH if n not in ("w_in_a", "w_in_b") else 2) for n in
          ("w_in_a", "w_in_b", "w_mem_kv", "w_o", "w_mlp1", "w_mlp2")}
    gs = {n: [None] * DEPTH for n in ("norm1_g", "mem_norm_g", "norm2_g")}
    received = [None] * N_A
    dk_sh = jnp.zeros((4, S, LANES), F32)
    dv_sh = jnp.zeros((4, S, LANES), F32)
    dc_sh = jnp.zeros((4, nb, 8, LANES), F32)
    for l in reversed(range(DEPTH)):
        sv = saved[l]
        du = _mm(dhb, wb["w_mlp2"][l], mode="nt", epi="drelu2", extra=sv["u"], name=f"mlp2_dx_{l}")
        gb["w_mlp2"][l] = _mm(sv["act"], dhb, mode="tn", out_dtype=F32, name=f"mlp2_dw_{l}")
        gb["w_mlp1"][l] = _mm(sv["hn2"], du, mode="tn", out_dtype=F32, name=f"mlp1_dw_{l}")
        dh, dhb, gs["norm2_g"][l] = _mm(du, wb["w_mlp1"][l], mode="nt", epi="rms_bwd",
                                        extra=(sv["h_mid"], vec(sm["norm2_g"][l]), dh),
                                        name=f"mlp1_dx_norm2_bwd_{l}")
        dmerged = _mm(dhb, wb["w_o"][l], mode="nt", name=f"out_proj_dx_{l}")
        gb["w_o"][l] = _mm(sv["merged"], dhb, mode="tn", out_dtype=F32, name=f"out_proj_dw_{l}")
        if l < N_A:
            ready = _pack_grads(PARTS[1 + l], PART_ROWS[1 + l], gb, None)
            dq, dk, dv, received[l] = _sb_bwd(sv["proj"], sv["merged"], dmerged, f"stickbreak_bwd_{l}",
                                              _Scatter, ready)
        else:
            dq, dk_sh, dv_sh, dc_sh = _fox_bwd(sv["proj"], shared["kv"], shared["c_col"], shared["c_row"],
                                               sv["lse"], sv["merged"], dmerged, dk_sh, dv_sh, dc_sh,
                                               f"fox_bwd_{l}")
        dqm, dmk, dmv = _mem_bwd(sv["proj"], sv["qcol"], sv["mkv"], sv["mlse"], sv["merged"], dmerged,
                                 f"mem_attn_bwd_{l}")
        if l < N_A:
            flat = lambda t: t.transpose(1, 0, 2).reshape(S, MIX_WIDTH).astype(BF16)
            dproj = jnp.concatenate([dq, flat(dk), flat(dv), dqm], axis=1)
        else:
            dproj = jnp.concatenate([dq, dqm], axis=1)
        name_in = "w_in_a" if l < N_A else "w_in_b"
        gb[name_in][l if l < N_A else l - N_A] = _mm(sv["hn"], dproj, mode="tn", out_dtype=F32,
                                                      name=f"in_proj_dw_{l}")
        dh, dhb, gs["norm1_g"][l] = _mm(dproj, sv["w_in"], mode="nt", epi="rms_bwd",
                                        extra=(sv["h"], vec(sm["norm1_g"][l]), dh),
                                        name=f"in_proj_dx_norm1_bwd_{l}")
        dmkv = jnp.concatenate([dmk, dmv], axis=1)
        gb["w_mem_kv"][l] = _mm(sv["mn"], dmkv, mode="tn", out_dtype=F32, name=f"mem_kv_dw_{l}")
        dmn = _mm(dmkv, wb["w_mem_kv"][l], mode="nt", out_dtype=F32, name=f"mem_kv_dx_{l}")
        gs["mem_norm_g"][l] = _rms_gain_grad(mem, dmn, f"mem_norm_bwd_{l}")
        if l == N_A:
            dfl3, db8 = _gate_bwd(dc_sh.reshape(4, nb, 8, LANES)[:, :, :2].transpose(1, 0, 2, 3).reshape(nb, 8, LANES),
                                  shared["fl3"], b8)
            dfl = dfl3.transpose(1, 0, 2).reshape(8, S).T
            flat = lambda t: t.transpose(1, 0, 2).reshape(S, MIX_WIDTH).astype(BF16)
            dkvf = jnp.concatenate([flat(dk_sh), flat(dv_sh),
                                    jnp.pad(dfl, ((0, 0), (0, LANES - 8))).astype(BF16)], axis=1)
            gb["w_kv_shared"] = _mm(shared["hs"], dkvf, mode="tn", out_dtype=F32, name="kv_shared_dw")[:, :W_KV_SHARED]
            dh, dhb, g_kvn = _mm(dkvf, w_kvf, mode="nt", epi="rms_bwd",
                                 extra=(shared["h"], vec(sm["kv_norm_g"]), dh), name="kv_shared_dx_norm_bwd")
            g_bf = db8[:, 0]

    gsmall = {n: jnp.concatenate(v, axis=0) for n, v in gs.items()}
    gsmall["kv_norm_g"] = g_kvn
    gsmall["final_norm_g"] = dg_final
    gsmall["b_f"] = g_bf
    return loss, dh, gb, gsmall, received


def kernel(x, mem, norm1_g, w_in_a, w_in_b, w_mem_kv, mem_norm_g, w_o, norm2_g, w_mlp1, w_mlp2, kv_norm_g, w_kv_shared, b_f, final_norm_g, loss_target, m_norm1_g, m_w_in_a, m_w_in_b, m_w_mem_kv, m_mem_norm_g, m_w_o, m_norm2_g, m_w_mlp1, m_w_mlp2, m_kv_norm_g, m_w_kv_shared, m_b_f, m_final_norm_g, v_norm1_g, v_w_in_a, v_w_in_b, v_w_mem_kv, v_mem_norm_g, v_w_o, v_norm2_g, v_w_mlp1, v_w_mlp2, v_kv_norm_g, v_w_kv_shared, v_b_f, v_final_norm_g):
    big_w = dict(w_in_a=w_in_a, w_in_b=w_in_b, w_mem_kv=w_mem_kv, w_o=w_o, w_mlp1=w_mlp1, w_mlp2=w_mlp2,
                 w_kv_shared=w_kv_shared)
    small_w = dict(norm1_g=norm1_g, mem_norm_g=mem_norm_g, norm2_g=norm2_g, kv_norm_g=kv_norm_g,
                   final_norm_g=final_norm_g, b_f=b_f)
    big_m = dict(w_in_a=m_w_in_a, w_in_b=m_w_in_b, w_mem_kv=m_w_mem_kv, w_o=m_w_o, w_mlp1=m_w_mlp1,
                 w_mlp2=m_w_mlp2, w_kv_shared=m_w_kv_shared)
    small_m = dict(norm1_g=m_norm1_g, mem_norm_g=m_mem_norm_g, norm2_g=m_norm2_g, kv_norm_g=m_kv_norm_g,
                   final_norm_g=m_final_norm_g, b_f=m_b_f)
    big_v = dict(w_in_a=v_w_in_a, w_in_b=v_w_in_b, w_mem_kv=v_w_mem_kv, w_o=v_w_o, w_mlp1=v_w_mlp1,
                 w_mlp2=v_w_mlp2, w_kv_shared=v_w_kv_shared)
    small_v = dict(norm1_g=v_norm1_g, mem_norm_g=v_mem_norm_g, norm2_g=v_norm2_g, kv_norm_g=v_kv_norm_g,
                   final_norm_g=v_final_norm_g, b_f=v_b_f)

    def pack(k, big, small, dtype):
        return _pack_local(PARTS[k], PART_ROWS[k], big, small if k == 0 else None, dtype)

    def pack_all(big, small):
        return jnp.concatenate([pack(k, big, small, F32) for k in range(len(PARTS))], axis=0)

    wb = {n: {} for n in BIG_NAMES}
    _unpack_gathered(PARTS[0], _allgather_chips(pack(0, big_w, small_w, BF16)), wb)
    shards = [pack(1 + l, big_w, None, BF16) for l in range(N_A)]

    loss, dx, gb, gsmall, received = _local_step(x[0], mem[0], wb, shards, small_w, loss_target[0])

    received = [_scatter_chips(_pack_grads(PARTS[0], PART_ROWS[0], gb, gsmall))] + received
    part = jnp.concatenate([_sum4(r) for r in received], axis=0)
    other = _swap_cores(part)
    g, delta, new_m, new_v = _adamw(part, other, pack_all(big_w, small_w), pack_all(big_m, small_m),
                                    pack_all(big_v, small_v))

    outs = [lax.psum(loss[0, 0], ("x", "y", "c")), dx[None]]
    for packed in (g, delta, new_m, new_v):
        pieces, d, off = {n: [] for n in BIG_NAMES}, {}, 0
        for k, part_k in enumerate(PARTS):
            _unpack_local(part_k, packed[off:off + PART_ROWS[k]], k == 0, pieces, d)
            off += PART_ROWS[k]
        d.update(_join_layers(pieces))
        outs.extend(d[n] for n in WEIGHT_ORDER)
    return tuple(outs)
```
